```python
import jax, jax.numpy as jnp
from jax import lax
import numpy as np

D_MODEL = 2048
BATCH = 2
SEQ = 4096
DEPTH = 1

MEM_LEN = 256
MLA_HEADS = 8
MLA_NOPE_DIM = 128
MLA_ROPE_DIM = 64
MLA_QK_DIM = MLA_NOPE_DIM + MLA_ROPE_DIM
MLA_V_DIM = 128
MLA_Q_RANK = 512
MLA_KV_RANK = 256
SB_HEADS = 8
SB_HEAD_DIM = 128
MLA_OUT = MLA_HEADS * MLA_V_DIM
SB_OUT = SB_HEADS * SB_HEAD_DIM
D_MIX = MLA_OUT + SB_OUT
IN_SPLITS = (MLA_Q_RANK,
             MLA_Q_RANK + MLA_KV_RANK,
             MLA_Q_RANK + MLA_KV_RANK + MLA_ROPE_DIM,
             MLA_Q_RANK + MLA_KV_RANK + MLA_ROPE_DIM + SB_OUT,
             MLA_Q_RANK + MLA_KV_RANK + MLA_ROPE_DIM + 2 * SB_OUT)
IN_COLS = MLA_Q_RANK + MLA_KV_RANK + MLA_ROPE_DIM + 3 * SB_OUT
MEM_HEADS = 4
MEM_HEAD_DIM = 128
N_GROUPS = 4
EXPERTS_PER_GROUP = 8
N_EXPERTS = N_GROUPS * EXPERTS_PER_GROUP
TOP_K = 2
D_EXPERT = 512

Q_BLOCK = 128
ROPE_THETA = 10000.0
EPS = 1e-6

kernel_name = "hymba_mla_stickbreaking_hiermoe_block"


def rms_norm(x, g):
    xf = x.astype(jnp.float32)
    y = xf * lax.rsqrt(jnp.mean(xf * xf, axis=-1, keepdims=True) + EPS)
    return (y * g.astype(jnp.float32)).astype(x.dtype)


def rotary_tables(positions):
    inv_freq = ROPE_THETA ** (-jnp.arange(0, MLA_ROPE_DIM, 2, dtype=jnp.float32) / MLA_ROPE_DIM)
    ang = positions.astype(jnp.float32)[..., None] * inv_freq
    return jnp.cos(ang), jnp.sin(ang)


def apply_rope(x, cos, sin):
    x1, x2 = jnp.split(x.astype(jnp.float32), 2, axis=-1)
    return jnp.concatenate([x1 * cos - x2 * sin, x2 * cos + x1 * sin], axis=-1).astype(x.dtype)


def to_blocks(t):
    b, s, h, d = t.shape
    return t.reshape(b, s // Q_BLOCK, Q_BLOCK, h, d).transpose(1, 0, 3, 2, 4)


def from_blocks(o):
    nb, b, h, blk, d = o.shape
    return o.transpose(1, 0, 3, 2, 4).reshape(b, nb * blk, h, d)


def mla_attention(q, k, v):
    s_len = q.shape[1]
    scale = MLA_QK_DIM ** -0.5
    k_pos = jnp.arange(s_len)

    def block(args):
        q_blk, bi = args
        s = jnp.einsum('bhqd,bkhd->bhqk', q_blk, k).astype(jnp.float32) * scale
        q_pos = bi * Q_BLOCK + jnp.arange(Q_BLOCK)
        mask = k_pos[None, :] <= q_pos[:, None]
        p = jax.nn.softmax(jnp.where(mask, s, -jnp.inf), axis=-1).astype(v.dtype)
        return jnp.einsum('bhqk,bkhd->bhqd', p, v)

    o = lax.map(block, (to_blocks(q), jnp.arange(s_len // Q_BLOCK)))
    return from_blocks(o)


def stick_breaking_attention(q, k, v):
    s_len = q.shape[1]
    scale = SB_HEAD_DIM ** -0.5
    k_pos = jnp.arange(s_len)

    def block(args):
        q_blk, bi = args
        z = jnp.einsum('bhqd,bkhd->bhqk', q_blk, k).astype(jnp.float32) * scale
        q_pos = bi * Q_BLOCK + jnp.arange(Q_BLOCK)
        mask = k_pos[None, :] < q_pos[:, None]
        log_beta = jax.nn.log_sigmoid(z)
        log_1m = jnp.where(mask, jax.nn.log_sigmoid(-z), 0.0)
        later = lax.cumsum(log_1m, axis=3, reverse=True) - log_1m
        a = jnp.where(mask, jnp.exp(log_beta + later), 0.0).astype(v.dtype)
        return jnp.einsum('bhqk,bkhd->bhqd', a, v)

    o = lax.map(block, (to_blocks(q), jnp.arange(s_len // Q_BLOCK)))
    return from_blocks(o)


def memory_cross_attention(h, mem_n, w_q, w_kv, w_o):
    b, s, _ = h.shape
    m = mem_n.shape[1]
    q = (h @ w_q).reshape(b, s, MEM_HEADS, MEM_HEAD_DIM)
    kv = (mem_n @ w_kv).reshape(b, m, 2, MEM_HEADS, MEM_HEAD_DIM)
    k, v = kv[:, :, 0], kv[:, :, 1]
    sc = jnp.einsum('bshd,bmhd->bhsm', q, k).astype(jnp.float32) * (MEM_HEAD_DIM ** -0.5)
    p = jax.nn.softmax(sc, axis=-1).astype(v.dtype)
    o = jnp.einsum('bhsm,bmhd->bshd', p, v).reshape(b, s, MEM_HEADS * MEM_HEAD_DIM)
    return o @ w_o


def hierarchical_moe(h, w_group, b_group, w_expert_router, b_expert, w_gate, w_up, w_down):
    b, s, d = h.shape
    t = h.reshape(b * s, d)
    group_prob = jax.nn.softmax((t @ w_group).astype(jnp.float32) + b_group.astype(jnp.float32), axis=-1)
    p_g, g_idx = lax.top_k(group_prob, 1)
    exp_logits = ((t @ w_expert_router).astype(jnp.float32) + b_expert.astype(jnp.float32)
                  ).reshape(-1, N_GROUPS, EXPERTS_PER_GROUP)
    sel_logits = jnp.einsum('tg,tge->te', jax.nn.one_hot(g_idx[:, 0], N_GROUPS, dtype=jnp.float32), exp_logits)
    p_e, e_idx = lax.top_k(jax.nn.softmax(sel_logits, axis=-1), TOP_K)
    p_e = p_e / jnp.sum(p_e, axis=-1, keepdims=True)
    gates = p_g * p_e
    expert_id = g_idx * EXPERTS_PER_GROUP + e_idx
    combine = jnp.einsum('tk,tke->te', gates,
                         jax.nn.one_hot(expert_id, N_EXPERTS, dtype=jnp.float32)).astype(h.dtype)
    a = jnp.einsum('td,edf->tef', t, w_gate)
    u = jnp.einsum('td,edf->tef', t, w_up)
    act = jax.nn.silu(a) * u * combine[:, :, None]
    y = jnp.einsum('tef,efd->td', act, w_down)
    return y.reshape(b, s, d)


def setup_inputs(seed: int = 0) -> dict:
    key = jax.random.key(seed)
    ks = jax.random.split(key, 26)
    f32 = jnp.float32
    L = DEPTH

    def w(k, shape, fan_in):
        return jax.random.normal(k, shape, f32) * (fan_in ** -0.5)

    def gain(k, shape):
        return 1.0 + 0.02 * jax.random.normal(k, shape, f32)

    return {
        "x": jax.random.normal(ks[0], (BATCH, SEQ, D_MODEL), f32),
        "mem": jax.random.normal(ks[1], (BATCH, MEM_LEN, D_MODEL), f32),
        "positions": jnp.broadcast_to(jnp.arange(SEQ, dtype=jnp.int32), (BATCH, SEQ)),
        "norm_mix": gain(ks[2], (L, D_MODEL)),
        "w_in": w(ks[3], (L, D_MODEL, IN_COLS), D_MODEL),
        "norm_q_lat": gain(ks[4], (L, MLA_Q_RANK)),
        "w_q_b": w(ks[5], (L, MLA_Q_RANK, MLA_HEADS * MLA_QK_DIM), MLA_Q_RANK),
        "norm_kv_lat": gain(ks[6], (L, MLA_KV_RANK)),
        "w_kv_b": w(ks[7], (L, MLA_KV_RANK, MLA_HEADS * (MLA_NOPE_DIM + MLA_V_DIM)), MLA_KV_RANK),
        "norm_mla_out": gain(ks[8], (L, MLA_OUT)),
        "norm_sb_out": gain(ks[9], (L, SB_OUT)),
        "w_out": w(ks[10], (L, D_MIX, D_MODEL), D_MIX),
        "norm_mem_x": gain(ks[11], (L, D_MODEL)),
        "norm_mem_src": gain(ks[12], (L, D_MODEL)),
        "w_mem_q": w(ks[13], (L, D_MODEL, MEM_HEADS * MEM_HEAD_DIM), D_MODEL),
        "w_mem_kv": w(ks[14], (L, D_MODEL, 2 * MEM_HEADS * MEM_HEAD_DIM), D_MODEL),
        "w_mem_o": w(ks[15], (L, MEM_HEADS * MEM_HEAD_DIM, D_MODEL), MEM_HEADS * MEM_HEAD_DIM),
        "norm_ffn": gain(ks[16], (L, D_MODEL)),
        "w_group": w(ks[17], (L, D_MODEL, N_GROUPS), D_MODEL),
        "b_group": 0.01 * jax.random.normal(ks[18], (L, N_GROUPS), f32),
        "w_expert_router": w(ks[19], (L, D_MODEL, N_EXPERTS), D_MODEL),
        "b_expert": 0.01 * jax.random.normal(ks[20], (L, N_EXPERTS), f32),
        "w_gate": w(ks[21], (L, N_EXPERTS, D_MODEL, D_EXPERT), D_MODEL),
        "w_up": w(ks[22], (L, N_EXPERTS, D_MODEL, D_EXPERT), D_MODEL),
        "w_down": w(ks[23], (L, N_EXPERTS, D_EXPERT, D_MODEL), D_EXPERT),
        "norm_final": gain(ks[24], (D_MODEL,)),
    }


def reference(x, mem, positions, norm_mix, w_in, norm_q_lat, w_q_b, norm_kv_lat, w_kv_b,
              norm_mla_out, norm_sb_out, w_out, norm_mem_x, norm_mem_src, w_mem_q, w_mem_kv,
              w_mem_o, norm_ffn, w_group, b_group, w_expert_router, b_expert, w_gate, w_up,
              w_down, norm_final):
    b, s, _ = x.shape
    cos, sin = rotary_tables(positions)
    for l in range(DEPTH):
        h = rms_norm(x, norm_mix[l])
        proj = h @ w_in[l]
        c_q, c_kv, k_pe, q_sb, k_sb, v_sb = jnp.split(proj, IN_SPLITS, axis=-1)

        q = (rms_norm(c_q, norm_q_lat[l]) @ w_q_b[l]).reshape(b, s, MLA_HEADS, MLA_QK_DIM)
        q_nope, q_pe = q[..., :MLA_NOPE_DIM], q[..., MLA_NOPE_DIM:]
        kv = (rms_norm(c_kv, norm_kv_lat[l]) @ w_kv_b[l]).reshape(b, s, MLA_HEADS, MLA_NOPE_DIM + MLA_V_DIM)
        k_nope, v_mla = kv[..., :MLA_NOPE_DIM], kv[..., MLA_NOPE_DIM:]
        q_pe = apply_rope(q_pe, cos[:, :, None, :], sin[:, :, None, :])
        k_pe = apply_rope(k_pe, cos, sin)
        q_mla = jnp.concatenate([q_nope, q_pe], axis=-1)
        k_mla = jnp.concatenate(
            [k_nope, jnp.broadcast_to(k_pe[:, :, None, :], (b, s, MLA_HEADS, MLA_ROPE_DIM))], axis=-1)
        o_mla = mla_attention(q_mla, k_mla, v_mla).reshape(b, s, MLA_OUT)

        o_sb = stick_breaking_attention(q_sb.reshape(b, s, SB_HEADS, SB_HEAD_DIM),
                                        k_sb.reshape(b, s, SB_HEADS, SB_HEAD_DIM),
                                        v_sb.reshape(b, s, SB_HEADS, SB_HEAD_DIM)).reshape(b, s, SB_OUT)

        mixed = jnp.concatenate([rms_norm(o_mla, norm_mla_out[l]), rms_norm(o_sb, norm_sb_out[l])], axis=-1)
        x = x + mixed @ w_out[l]

        mem_n = rms_norm(mem, norm_mem_src[l])
        x = x + memory_cross_attention(rms_norm(x, norm_mem_x[l]), mem_n, w_mem_q[l], w_mem_kv[l], w_mem_o[l])

        x = x + hierarchical_moe(rms_norm(x, norm_ffn[l]), w_group[l], b_group[l], w_expert_router[l],
                                 b_expert[l], w_gate[l], w_up[l], w_down[l])
    return rms_norm(x, norm_final)
```

```python
import functools

import jax
import jax.numpy as jnp
from jax import lax
from jax.experimental import pallas as pl
from jax.experimental.pallas import tpu as pltpu

F32 = jnp.float32
BF16 = jnp.bfloat16

EPS = 1e-6
ROPE_THETA = 10000.0

D_MODEL = 2048
MEM_LEN = 256
MLA_HEADS = 8
MLA_NOPE = 128
MLA_ROPE = 64
MLA_QK = MLA_NOPE + MLA_ROPE
MLA_V = 128
MLA_Q_RANK = 512
MLA_KV_RANK = 256
MLA_PAD = 256
SB_HEADS = 8
SB_DIM = 128
MLA_OUT = MLA_HEADS * MLA_V
SB_OUT = SB_HEADS * SB_DIM
MEM_HEADS = 4
MEM_DIM = 128
N_GROUPS = 4
EXPERTS_PER_GROUP = 8
N_EXPERTS = N_GROUPS * EXPERTS_PER_GROUP
D_EXPERT = 512

LANES = 128
LAT_COLS = 1024
IN_COLS_PAD = LAT_COLS + 3 * SB_OUT

SLOT_TILE = 256
VMEM_LIMIT = 56 * 1024 * 1024


def _rms(x, g):
    return x * lax.rsqrt(jnp.mean(x * x, axis=-1, keepdims=True) + EPS) * g


def _dot(a, b):
    return jnp.dot(a, b, preferred_element_type=F32)


def _dot_nt(a, b):
    return lax.dot_general(a, b, (((1,), (1,)), ((), ())), preferred_element_type=F32)


def _split_bf16(x):
    hi = x.astype(BF16)
    lo = (x - hi.astype(F32)).astype(BF16)
    return hi, lo


def _params(*sem):
    return pltpu.CompilerParams(dimension_semantics=sem, vmem_limit_bytes=VMEM_LIMIT)


def _proj_in_kernel(x_ref, g_ref, w_ref, lat_ref, sb_ref, h_ref):
    j = pl.program_id(1)

    @pl.when(j == 0)
    def _():
        h_ref[...] = _rms(x_ref[...], g_ref[...]).astype(BF16)

    y = _dot(h_ref[...], w_ref[...])

    @pl.when(j == 0)
    def _():
        lat_ref[...] = y

    @pl.when(j > 0)
    def _():
        sb_ref[...] = y.astype(BF16)


def _proj_in(x, g, w_pad, tm=512):
    t = x.shape[0]
    tn = LAT_COLS
    return pl.pallas_call(
        _proj_in_kernel,
        grid=(t // tm, IN_COLS_PAD // tn),
        in_specs=[
            pl.BlockSpec((tm, D_MODEL), lambda i, j: (i, 0)),
            pl.BlockSpec((1, D_MODEL), lambda i, j: (0, 0)),
            pl.BlockSpec((D_MODEL, tn), lambda i, j: (0, j)),
        ],
        out_specs=[
            pl.BlockSpec((tm, tn), lambda i, j: (i, 0)),
            pl.BlockSpec((tm, tn), lambda i, j: (i, jnp.maximum(j - 1, 0))),
        ],
        out_shape=[
            jax.ShapeDtypeStruct((t, LAT_COLS), F32),
            jax.ShapeDtypeStruct((t, 3 * SB_OUT), BF16),
        ],
        scratch_shapes=[pltpu.VMEM((tm, D_MODEL), BF16)],
        compiler_params=_params("parallel", "arbitrary"),
        name="proj_in",
    )(x, g, w_pad)


def _mla_proj_kernel(lat_ref, pos_ref, invf_ref, gq_ref, gkv_ref, wqa_ref, wqb_ref, wk_ref, wv_ref,
                     q_ref, k_ref, v_ref):
    cq = _rms(lat_ref[:, :MLA_Q_RANK], gq_ref[...]).astype(BF16)
    ckv = _rms(lat_ref[:, MLA_Q_RANK:MLA_Q_RANK + MLA_KV_RANK], gkv_ref[...]).astype(BF16)
    ang = pos_ref[...].astype(F32) * invf_ref[...]
    live = lax.broadcasted_iota(jnp.int32, ang.shape, 1) < MLA_ROPE
    cos2 = jnp.where(live, jnp.cos(ang), 0.0)
    sin2 = jnp.where(live, jnp.sin(ang), 0.0)

    qa = _dot(cq, wqa_ref[...])
    qb = _dot(cq, wqb_ref[...])
    kn = _dot(ckv, wk_ref[...])
    v_ref[...] = _dot(ckv, wv_ref[...]).astype(BF16)
    k_pe = (lat_ref[:, 768:896] * cos2 + lat_ref[:, 896:1024] * sin2).astype(BF16)
    for h in range(MLA_HEADS):
        lo = h * MLA_PAD
        mid = lo + LANES
        q_ref[:, lo:mid] = qa[:, lo:mid].astype(BF16)
        q_ref[:, mid:mid + LANES] = (qa[:, mid:mid + LANES] * cos2
                                     + qb[:, h * LANES:(h + 1) * LANES] * sin2).astype(BF16)
        k_ref[:, lo:mid] = kn[:, h * LANES:(h + 1) * LANES].astype(BF16)
        k_ref[:, mid:mid + LANES] = k_pe


def _mla_proj(lat, pos, invf, gq, gkv, wqa, wqb, wk, wv, tm=512):
    t = lat.shape[0]
    full = lambda a: pl.BlockSpec(a.shape, lambda i: (0, 0))
    return pl.pallas_call(
        _mla_proj_kernel,
        grid=(t // tm,),
        in_specs=[
            pl.BlockSpec((tm, LAT_COLS), lambda i: (i, 0)),
            pl.BlockSpec((tm, 1), lambda i: (i, 0)),
            full(invf), full(gq), full(gkv), full(wqa), full(wqb), full(wk), full(wv),
        ],
        out_specs=[
            pl.BlockSpec((tm, MLA_HEADS * MLA_PAD), lambda i: (i, 0)),
            pl.BlockSpec((tm, MLA_HEADS * MLA_PAD), lambda i: (i, 0)),
            pl.BlockSpec((tm, MLA_OUT), lambda i: (i, 0)),
        ],
        out_shape=[
            jax.ShapeDtypeStruct((t, MLA_HEADS * MLA_PAD), BF16),
            jax.ShapeDtypeStruct((t, MLA_HEADS * MLA_PAD), BF16),
            jax.ShapeDtypeStruct((t, MLA_OUT), BF16),
        ],
        compiler_params=_params("parallel"),
        name="mla_proj",
    )(lat, pos, invf, gq, gkv, wqa, wqb, wk, wv)


def _mla_attn_kernel(q_ref, k_ref, v_ref, o_ref, m_ref, l_ref, acc_ref, *, tile, scale):
    i = pl.program_id(2)
    q = q_ref[...]
    m_ref[...] = jnp.full(m_ref.shape, -jnp.inf, F32)
    l_ref[...] = jnp.zeros(l_ref.shape, F32)
    acc_ref[...] = jnp.zeros(acc_ref.shape, F32)

    def step(j, diagonal):
        ks = pl.multiple_of(j * tile, tile)
        s = _dot_nt(q, k_ref[pl.ds(ks, tile), :]) * scale
        if diagonal:
            qpos = lax.broadcasted_iota(jnp.int32, s.shape, 0)
            kpos = lax.broadcasted_iota(jnp.int32, s.shape, 1)
            s = jnp.where(kpos <= qpos, s, -jnp.inf)
        m_old = m_ref[...]
        m_new = jnp.maximum(m_old, jnp.max(s, axis=-1, keepdims=True))
        alpha = jnp.exp(m_old - m_new)
        p = jnp.exp(s - m_new)
        l_ref[...] = alpha * l_ref[...] + jnp.sum(p, axis=-1, keepdims=True)
        acc_ref[...] = alpha * acc_ref[...] + _dot(p.astype(BF16), v_ref[pl.ds(ks, tile), :])
        m_ref[...] = m_new

    def body(j, c):
        step(j, False)
        return c

    lax.fori_loop(0, i, body, 0)
    step(i, True)
    o_ref[...] = acc_ref[...] / l_ref[...]


def _mla_attn(q, k, v, batch, seq, tile=512):
    nq = seq // tile
    kern = functools.partial(_mla_attn_kernel, tile=tile, scale=MLA_QK ** -0.5)
    return pl.pallas_call(
        kern,
        grid=(batch, MLA_HEADS, nq),
        in_specs=[
            pl.BlockSpec((tile, MLA_PAD), lambda b, h, i: (b * nq + i, h)),
            pl.BlockSpec((seq, MLA_PAD), lambda b, h, i: (b, h)),
            pl.BlockSpec((seq, MLA_V), lambda b, h, i: (b, h)),
        ],
        out_specs=pl.BlockSpec((tile, MLA_V), lambda b, h, i: (b * nq + i, h)),
        out_shape=jax.ShapeDtypeStruct((batch * seq, MLA_OUT), F32),
        scratch_shapes=[
            pltpu.VMEM((tile, 1), F32),
            pltpu.VMEM((tile, 1), F32),
            pltpu.VMEM((tile, MLA_V), F32),
        ],
        compiler_params=_params("parallel", "parallel", "arbitrary"),
        name="mla_attn",
    )(q, k, v)


def _sb_attn_kernel(q_ref, k_ref, v_ref, o_ref, acc_ref, c_ref, *, tq, tk, scale):
    i = pl.program_id(2)
    q = q_ref[...]
    acc_ref[...] = jnp.zeros(acc_ref.shape, F32)
    c_ref[...] = jnp.zeros(c_ref.shape, F32)
    tri = (lax.broadcasted_iota(jnp.int32, (tk, tk), 0)
           > lax.broadcasted_iota(jnp.int32, (tk, tk), 1)).astype(BF16)

    def step(j, masked):
        ks = pl.multiple_of(j * tk, tk)
        z = _dot_nt(q, k_ref[pl.ds(ks, tk), :]) * scale
        sp = jnp.maximum(z, 0.0) + jnp.log1p(jnp.exp(-jnp.abs(z)))
        log_1m = -sp
        if masked:
            qpos = i * tq + lax.broadcasted_iota(jnp.int32, z.shape, 0)
            kpos = j * tk + lax.broadcasted_iota(jnp.int32, z.shape, 1)
            mask = kpos < qpos
            log_1m = jnp.where(mask, log_1m, 0.0)
        hi, lo = _split_bf16(log_1m)
        later = _dot(hi, tri) + _dot(lo, tri) + c_ref[...]
        a = jnp.exp(z - sp + later)
        if masked:
            a = jnp.where(mask, a, 0.0)
        acc_ref[...] += _dot(a.astype(BF16), v_ref[pl.ds(ks, tk), :])
        c_ref[...] += jnp.sum(log_1m, axis=-1, keepdims=True)

    n_diag = tq // tk
    last = (i + 1) * n_diag - 1
    for d in range(n_diag):
        step(last - d, True)

    def body(jj, c):
        step(i * n_diag - 1 - jj, False)
        return c

    lax.fori_loop(0, i * n_diag, body, 0)
    o_ref[...] = acc_ref[...]


def _sb_attn(qkv, batch, seq, tq=512, tk=256):
    nq = seq // tq
    kern = functools.partial(_sb_attn_kernel, tq=tq, tk=tk, scale=SB_DIM ** -0.5)
    return pl.pallas_call(
        kern,
        grid=(batch, SB_HEADS, nq),
        in_specs=[
            pl.BlockSpec((tq, SB_DIM), lambda b, h, i: (b * nq + i, h)),
            pl.BlockSpec((seq, SB_DIM), lambda b, h, i: (b, SB_HEADS + h)),
            pl.BlockSpec((seq, SB_DIM), lambda b, h, i: (b, 2 * SB_HEADS + h)),
        ],
        out_specs=pl.BlockSpec((tq, SB_DIM), lambda b, h, i: (b * nq + i, h)),
        out_shape=jax.ShapeDtypeStruct((batch * seq, SB_OUT), F32),
        scratch_shapes=[
            pltpu.VMEM((tq, SB_DIM), F32),
            pltpu.VMEM((tq, 1), F32),
        ],
        compiler_params=_params("parallel", "parallel", "arbitrary"),
        name="sb_attn",
    )(qkv, qkv, qkv)


def _out_proj_kernel(x_ref, oa_ref, ob_ref, ga_ref, gb_ref, w_ref, y_ref):
    na = _rms(oa_ref[...], ga_ref[...]).astype(BF16)
    nb = _rms(ob_ref[...], gb_ref[...]).astype(BF16)
    y_ref[...] = x_ref[...] + _dot(na, w_ref[:MLA_OUT, :]) + _dot(nb, w_ref[MLA_OUT:, :])


def _out_proj(x, oa, ob, ga, gb, w, tm=512):
    t = x.shape[0]
    full = lambda a: pl.BlockSpec(a.shape, lambda i: (0, 0))
    return pl.pallas_call(
        _out_proj_kernel,
        grid=(t // tm,),
        in_specs=[
            pl.BlockSpec((tm, D_MODEL), lambda i: (i, 0)),
            pl.BlockSpec((tm, MLA_OUT), lambda i: (i, 0)),
            pl.BlockSpec((tm, SB_OUT), lambda i: (i, 0)),
            full(ga), full(gb), full(w),
        ],
        out_specs=pl.BlockSpec((tm, D_MODEL), lambda i: (i, 0)),
        out_shape=jax.ShapeDtypeStruct((t, D_MODEL), F32),
        compiler_params=_params("parallel"),
        name="out_proj",
    )(x, oa, ob, ga, gb, w)


def _mem_kv_kernel(mem_ref, g_ref, w_ref, kv_ref):
    kv_ref[...] = _dot(_rms(mem_ref[...], g_ref[...]).astype(BF16), w_ref[...]).astype(BF16)


def _mem_kv(mem, g, w):
    m = mem.shape[0]
    n = w.shape[1]
    full = lambda a: pl.BlockSpec(a.shape, lambda i: (0, 0))
    return pl.pallas_call(
        _mem_kv_kernel,
        grid=(1,),
        in_specs=[full(mem), full(g), full(w)],
        out_specs=pl.BlockSpec((m, n), lambda i: (0, 0)),
        out_shape=jax.ShapeDtypeStruct((m, n), BF16),
        compiler_params=_params("arbitrary"),
        name="mem_kv",
    )(mem, g, w)


GROUP_LANE0 = N_EXPERTS


def _mem_route_kernel(x_ref, gx_ref, wq_ref, kv_ref, wo_ref, gf_ref, wrh_ref, wrl_ref, br_ref,
                      x2_ref, h2_ref, route_ref):
    x1 = x_ref[...]
    q = _dot(_rms(x1, gx_ref[...]).astype(BF16), wq_ref[...]).astype(BF16)
    scale = MEM_DIM ** -0.5
    kw = MEM_HEADS * MEM_DIM
    heads = []
    for h in range(MEM_HEADS):
        lo = h * MEM_DIM
        s = _dot_nt(q[:, lo:lo + MEM_DIM], kv_ref[:, lo:lo + MEM_DIM]) * scale
        e = jnp.exp(s - jnp.max(s, axis=-1, keepdims=True))
        p = (e / jnp.sum(e, axis=-1, keepdims=True)).astype(BF16)
        heads.append(_dot(p, kv_ref[:, kw + lo:kw + lo + MEM_DIM]).astype(BF16))
    o = jnp.concatenate(heads, axis=-1)
    x2 = x1 + _dot(o, wo_ref[...])
    x2_ref[...] = x2
    h2 = _rms(x2, gf_ref[...])
    h2_ref[...] = h2

    hh, hl = _split_bf16(h2)
    lg = _dot(hh, wrh_ref[...]) + _dot(hl, wrh_ref[...]) + _dot(hh, wrl_ref[...]) + br_ref[...]
    lane = lax.broadcasted_iota(jnp.int32, lg.shape, 1)
    big = jnp.int32(1 << 20)
    ninf = -jnp.inf

    def lane_max(v):
        return jnp.max(v, axis=-1, keepdims=True)

    def first_lane(cond):
        return jnp.min(jnp.where(cond, lane, big), axis=-1, keepdims=True)

    is_g = (lane >= GROUP_LANE0) & (lane < GROUP_LANE0 + N_GROUPS)
    g_max = lane_max(jnp.where(is_g, lg, ninf))
    g_sum = jnp.sum(jnp.where(is_g, jnp.exp(lg - g_max), 0.0), axis=-1, keepdims=True)
    p_g = 1.0 / g_sum
    g_idx = first_lane(is_g & (lg == g_max)) - GROUP_LANE0
    in_grp = (lane < N_EXPERTS) & ((lane // EXPERTS_PER_GROUP) == g_idx)
    e_max = lane_max(jnp.where(in_grp, lg, ninf))
    e_sum = jnp.sum(jnp.where(in_grp, jnp.exp(lg - e_max), 0.0), axis=-1, keepdims=True)
    i1 = first_lane(in_grp & (lg == e_max))
    rest = in_grp & (lane != i1)
    e_max2 = lane_max(jnp.where(rest, lg, ninf))
    i2 = first_lane(rest & (lg == e_max2))
    p1 = 1.0 / e_sum
    p2 = jnp.exp(e_max2 - e_max) / e_sum
    den = p1 + p2
    gate1 = p_g * (p1 / den)
    gate2 = p_g * (p2 / den)
    route = jnp.where(lane == 0, i1.astype(F32),
                      jnp.where(lane == 1, i2.astype(F32),
                                jnp.where(lane == 2, gate1,
                                          jnp.where(lane == 3, gate2, 0.0))))
    route_ref[...] = route


def _mem_route(x1, gx, wq, kv, wo, gf, wrh, wrl, br, seq, tm=512):
    t = x1.shape[0]
    per_batch = seq // tm
    full = lambda a: pl.BlockSpec(a.shape, lambda i: (0, 0))
    return pl.pallas_call(
        _mem_route_kernel,
        grid=(t // tm,),
        in_specs=[
            pl.BlockSpec((tm, D_MODEL), lambda i: (i, 0)),
            full(gx), full(wq),
            pl.BlockSpec((MEM_LEN, kv.shape[1]), lambda i: (i // per_batch, 0)),
            full(wo), full(gf), full(wrh), full(wrl), full(br),
        ],
        out_specs=[
            pl.BlockSpec((tm, D_MODEL), lambda i: (i, 0)),
            pl.BlockSpec((tm, D_MODEL), lambda i: (i, 0)),
            pl.BlockSpec((tm, LANES), lambda i: (i, 0)),
        ],
        out_shape=[
            jax.ShapeDtypeStruct((t, D_MODEL), F32),
            jax.ShapeDtypeStruct((t, D_MODEL), F32),
            jax.ShapeDtypeStruct((t, LANES), F32),
        ],
        compiler_params=_params("parallel"),
        name="mem_route",
    )(x1, gx, wq, kv, wo, gf, wrh, wrl, br)


SLOT_BLK = 256


def _slots_kernel(route_ref, slot_ref, meta_ref, cum_ref, *, n_tok):
    nblk = n_tok // SLOT_BLK
    e_iota = lax.broadcasted_iota(jnp.int32, (LANES, SLOT_BLK), 0)
    incl = (lax.broadcasted_iota(jnp.int32, (SLOT_BLK, SLOT_BLK), 0)
            <= lax.broadcasted_iota(jnp.int32, (SLOT_BLK, SLOT_BLK), 1)).astype(BF16)

    def onehots(b):
        ts = pl.multiple_of(b * SLOT_BLK, SLOT_BLK)
        ids = route_ref[pl.ds(ts, SLOT_BLK), :].T
        oh1 = e_iota == ids[0:1, :].astype(jnp.int32)
        oh2 = e_iota == ids[1:2, :].astype(jnp.int32)
        return ts, oh1, oh2

    def count(b, carry):
        ts, oh1, oh2 = onehots(b)
        cnt = (oh1 | oh2).astype(F32).astype(BF16)
        c = _dot(cnt, incl) + carry
        cum_ref[:, pl.ds(ts, SLOT_BLK)] = c
        return c[:, SLOT_BLK - 1:SLOT_BLK]

    total = lax.fori_loop(0, nblk, count, jnp.zeros((LANES, 1), F32))
    tiles = jnp.floor((total + (SLOT_TILE - 1)) * (1.0 / SLOT_TILE))
    below = (lax.broadcasted_iota(jnp.int32, (LANES, LANES), 1)
             < lax.broadcasted_iota(jnp.int32, (LANES, LANES), 0)).astype(BF16)
    tile_lo = _dot(below, jnp.broadcast_to(tiles, (LANES, LANES)).astype(BF16))
    base = tile_lo[:, 0:1] * SLOT_TILE

    def assign(b, carry):
        ts, oh1, oh2 = onehots(b)
        pos = base + cum_ref[:, pl.ds(ts, SLOT_BLK)] - 1.0
        s1 = jnp.sum(jnp.where(oh1, pos, 0.0), axis=0, keepdims=True)
        s2 = jnp.sum(jnp.where(oh2, pos, 0.0), axis=0, keepdims=True)
        row = lax.broadcasted_iota(jnp.int32, (8, SLOT_BLK), 0)
        slot_ref[:, pl.ds(ts, SLOT_BLK)] = jnp.where(row == 0, s1, jnp.where(row == 1, s2, 0.0)).astype(jnp.int32)
        return carry

    lax.fori_loop(0, nblk, assign, 0)

    tile_hi = tile_lo + tiles
    tix = lax.broadcasted_iota(jnp.int32, (LANES, LANES), 1).astype(F32)
    is_e = lax.broadcasted_iota(jnp.int32, (LANES, LANES), 0) < N_EXPERTS
    owner = jnp.sum(jnp.where(is_e & (tile_hi <= tix), 1.0, 0.0), axis=0, keepdims=True)
    n_tiles = jnp.max(jnp.where(is_e, tile_hi, 0.0), axis=0, keepdims=True)
    valid = tix[0:1, :] < n_tiles
    mine = is_e & (tile_lo <= tix) & (tix < tile_hi)
    live = jnp.sum(jnp.where(mine, jnp.clip(total - (tix - tile_lo) * SLOT_TILE, 0.0, SLOT_TILE), 0.0),
                   axis=0, keepdims=True)
    last_owner = jnp.max(jnp.where(valid, owner, 0.0), axis=1, keepdims=True)
    owner = jnp.where(valid, owner, last_owner)
    last_tile = jnp.maximum(n_tiles - 1.0, 0.0)
    row = lax.broadcasted_iota(jnp.int32, (8, LANES), 0)
    meta = jnp.where(row == 0, owner,
                     jnp.where(row == 1, valid.astype(F32),
                               jnp.where(row == 2, live,
                                         jnp.where(row == 3, jnp.where(valid, tix[0:1, :], last_tile), 0.0))))
    meta_ref[...] = meta.astype(jnp.int32)


def _slots(route):
    t = route.shape[0]
    kern = functools.partial(_slots_kernel, n_tok=t)
    return pl.pallas_call(
        kern,
        grid=(1,),
        in_specs=[pl.BlockSpec(route.shape, lambda i: (0, 0))],
        out_specs=[
            pl.BlockSpec((8, t), lambda i: (0, 0)),
            pl.BlockSpec((8, LANES), lambda i: (0, 0)),
        ],
        out_shape=[
            jax.ShapeDtypeStruct((8, t), jnp.int32),
            jax.ShapeDtypeStruct((8, LANES), jnp.int32),
        ],
        scratch_shapes=[pltpu.VMEM((LANES, t), F32)],
        compiler_params=_params("arbitrary"),
        name="slots",
    )(route)


def _row_copy(src_ref, src_row, dst_ref, dst_row, sem):
    return pltpu.make_async_copy(src_ref.at[pl.ds(src_row, 1), :], dst_ref.at[pl.ds(dst_row, 1), :], sem)


def _dispatch_kernel(s1_ref, s2_ref, h_ref, xs_ref, sem, *, tm):
    t0 = pl.program_id(0) * tm

    def issue(r, c):
        _row_copy(h_ref, r, xs_ref, s1_ref[t0 + r], sem).start()
        _row_copy(h_ref, r, xs_ref, s2_ref[t0 + r], sem).start()
        return c

    lax.fori_loop(0, tm, issue, 0)

    def drain(r, c):
        _row_copy(h_ref, r, xs_ref, s1_ref[t0 + r], sem).wait()
        _row_copy(h_ref, r, xs_ref, s2_ref[t0 + r], sem).wait()
        return c

    lax.fori_loop(0, tm, drain, 0)


def _dispatch(slot1, slot2, h2, n_slots, tm=256):
    t = h2.shape[0]
    kern = functools.partial(_dispatch_kernel, tm=tm)
    return pl.pallas_call(
        kern,
        grid_spec=pltpu.PrefetchScalarGridSpec(
            num_scalar_prefetch=2,
            grid=(t // tm,),
            in_specs=[pl.BlockSpec((tm, D_MODEL), lambda i, s1, s2: (i, 0))],
            out_specs=pl.BlockSpec(memory_space=pl.ANY),
            scratch_shapes=[pltpu.SemaphoreType.DMA(())],
        ),
        out_shape=jax.ShapeDtypeStruct((n_slots, D_MODEL), F32),
        compiler_params=_params("arbitrary"),
        name="dispatch",
    )(slot1, slot2, h2)


def _experts_kernel(own_ref, valid_ref, live_ref, blk_ref, xs_ref, wg_ref, wu_ref, wd_ref, ys_ref,
                    wgb_ref, wub_ref, wdb_ref):
    i = pl.program_id(0)
    new_expert = (i == 0) | (own_ref[i] != own_ref[jnp.maximum(i - 1, 0)])

    @pl.when(valid_ref[i] > 0)
    def _():
        @pl.when(new_expert)
        def _():
            wgb_ref[...] = wg_ref[0].astype(BF16)
            wub_ref[...] = wu_ref[0].astype(BF16)
            wdb_ref[...] = wd_ref[0].astype(BF16)

        rows = lax.broadcasted_iota(jnp.int32, (SLOT_TILE, 1), 0)
        x = jnp.where(rows < live_ref[i], xs_ref[...], 0.0).astype(BF16)
        a = _dot(x, wgb_ref[...])
        u = _dot(x, wub_ref[...])
        act = (a * jax.nn.sigmoid(a) * u).astype(BF16)
        ys_ref[...] = _dot(act, wdb_ref[...])


def _experts(own, valid, live, blk, xs, wg, wu, wd):
    n_tiles = xs.shape[0] // SLOT_TILE
    return pl.pallas_call(
        _experts_kernel,
        grid_spec=pltpu.PrefetchScalarGridSpec(
            num_scalar_prefetch=4,
            grid=(n_tiles,),
            in_specs=[
                pl.BlockSpec((SLOT_TILE, D_MODEL), lambda i, o, v, l, b: (b[i], 0)),
                pl.BlockSpec((1, D_MODEL, D_EXPERT), lambda i, o, v, l, b: (o[i], 0, 0)),
                pl.BlockSpec((1, D_MODEL, D_EXPERT), lambda i, o, v, l, b: (o[i], 0, 0)),
                pl.BlockSpec((1, D_EXPERT, D_MODEL), lambda i, o, v, l, b: (o[i], 0, 0)),
            ],
            out_specs=pl.BlockSpec((SLOT_TILE, D_MODEL), lambda i, o, v, l, b: (b[i], 0)),
            scratch_shapes=[
                pltpu.VMEM((D_MODEL, D_EXPERT), BF16),
                pltpu.VMEM((D_MODEL, D_EXPERT), BF16),
                pltpu.VMEM((D_EXPERT, D_MODEL), BF16),
            ],
        ),
        out_shape=jax.ShapeDtypeStruct(xs.shape, F32),
        compiler_params=_params("arbitrary"),
        name="experts",
    )(own, valid, live, blk, xs, wg, wu, wd)


def _combine_kernel(s1_ref, s2_ref, x_ref, route_ref, g_ref, ys_ref, y_ref, buf_ref, sem, *, tm):
    t0 = pl.program_id(0) * tm

    def issue(r, c):
        _row_copy(ys_ref, s1_ref[t0 + r], buf_ref.at[0], r, sem).start()
        _row_copy(ys_ref, s2_ref[t0 + r], buf_ref.at[1], r, sem).start()
        return c

    lax.fori_loop(0, tm, issue, 0)

    def drain(r, c):
        _row_copy(ys_ref, s1_ref[t0 + r], buf_ref.at[0], r, sem).wait()
        _row_copy(ys_ref, s2_ref[t0 + r], buf_ref.at[1], r, sem).wait()
        return c

    lax.fori_loop(0, tm, drain, 0)
    y = x_ref[...] + route_ref[:, 2:3] * buf_ref[0] + route_ref[:, 3:4] * buf_ref[1]
    y_ref[...] = _rms(y, g_ref[...])


def _combine(slot1, slot2, x2, route, g, ys, tm=256):
    t = x2.shape[0]
    kern = functools.partial(_combine_kernel, tm=tm)
    return pl.pallas_call(
        kern,
        grid_spec=pltpu.PrefetchScalarGridSpec(
            num_scalar_prefetch=2,
            grid=(t // tm,),
            in_specs=[
                pl.BlockSpec((tm, D_MODEL), lambda i, s1, s2: (i, 0)),
                pl.BlockSpec((tm, LANES), lambda i, s1, s2: (i, 0)),
                pl.BlockSpec((1, D_MODEL), lambda i, s1, s2: (0, 0)),
                pl.BlockSpec(memory_space=pl.ANY),
            ],
            out_specs=pl.BlockSpec((tm, D_MODEL), lambda i, s1, s2: (i, 0)),
            scratch_shapes=[
                pltpu.VMEM((2, tm, D_MODEL), F32),
                pltpu.SemaphoreType.DMA(()),
            ],
        ),
        out_shape=jax.ShapeDtypeStruct((t, D_MODEL), F32),
        compiler_params=_params("arbitrary"),
        name="combine",
    )(slot1, slot2, x2, route, g, ys)


def _swap_halves(w):
    half = w.shape[-1] // 2
    return jnp.concatenate([-w[..., half:], w[..., :half]], axis=-1)


def _row(v):
    return v.reshape(1, -1).astype(F32)


def kernel(x, mem, positions, norm_mix, w_in, norm_q_lat, w_q_b, norm_kv_lat, w_kv_b, norm_mla_out, norm_sb_out, w_out, norm_mem_x, norm_mem_src, w_mem_q, w_mem_kv, w_mem_o, norm_ffn, w_group, b_group, w_expert_router, b_expert, w_gate, w_up, w_down, norm_final):
    batch, seq, d = x.shape
    t = batch * seq
    depth = w_in.shape[0]
    xt = x.reshape(t, d)
    pos = positions.reshape(t, 1)
    inv_freq = ROPE_THETA ** (-jnp.arange(0, MLA_ROPE, 2, dtype=F32) / MLA_ROPE)
    invf = jnp.concatenate([inv_freq, inv_freq, jnp.zeros((LANES - MLA_ROPE,), F32)]).reshape(1, LANES)
    n_slots = (2 * t // SLOT_TILE + N_EXPERTS) * SLOT_TILE

    for l in range(depth):
        wi = w_in[l]
        lat_w = MLA_Q_RANK + MLA_KV_RANK
        w_kpe = wi[:, lat_w:lat_w + MLA_ROPE]
        zpad = jnp.zeros((d, LANES - MLA_ROPE), F32)
        w_in_pad = jnp.concatenate(
            [wi[:, :lat_w + MLA_ROPE], zpad, _swap_halves(w_kpe), zpad, wi[:, lat_w + MLA_ROPE:]],
            axis=1).astype(BF16)
        wq = w_q_b[l].reshape(MLA_Q_RANK, MLA_HEADS, MLA_QK)
        wq_pe = wq[:, :, MLA_NOPE:]
        zq = jnp.zeros((MLA_Q_RANK, MLA_HEADS, MLA_PAD - MLA_QK), F32)
        wqa = jnp.concatenate([wq, zq], axis=-1).reshape(MLA_Q_RANK, MLA_HEADS * MLA_PAD).astype(BF16)
        wqb = jnp.concatenate([_swap_halves(wq_pe), zq], axis=-1).reshape(MLA_Q_RANK, MLA_HEADS * LANES).astype(BF16)
        wkv = w_kv_b[l].reshape(MLA_KV_RANK, MLA_HEADS, MLA_NOPE + MLA_V)
        wk = wkv[:, :, :MLA_NOPE].reshape(MLA_KV_RANK, MLA_OUT).astype(BF16)
        wv = wkv[:, :, MLA_NOPE:].reshape(MLA_KV_RANK, MLA_OUT).astype(BF16)
        w_router = jnp.concatenate(
            [w_expert_router[l], w_group[l], jnp.zeros((d, LANES - N_EXPERTS - N_GROUPS), F32)], axis=1)
        wr_hi = w_router.astype(BF16)
        wr_lo = (w_router - wr_hi.astype(F32)).astype(BF16)
        b_router = jnp.concatenate(
            [b_expert[l].astype(F32), b_group[l].astype(F32), jnp.zeros((LANES - N_EXPERTS - N_GROUPS,), F32)]
        ).reshape(1, LANES)

        lat, sb_qkv = _proj_in(xt, _row(norm_mix[l]), w_in_pad)
        q, k, v = _mla_proj(lat, pos, invf, _row(norm_q_lat[l]), _row(norm_kv_lat[l]), wqa, wqb, wk, wv)
        o_mla = _mla_attn(q, k, v, batch, seq)
        o_sb = _sb_attn(sb_qkv, batch, seq)
        x1 = _out_proj(xt, o_mla, o_sb, _row(norm_mla_out[l]), _row(norm_sb_out[l]), w_out[l].astype(BF16))

        kv = _mem_kv(mem.reshape(batch * MEM_LEN, d), _row(norm_mem_src[l]), w_mem_kv[l].astype(BF16))
        x2, h2, route = _mem_route(x1, _row(norm_mem_x[l]), w_mem_q[l].astype(BF16), kv,
                                   w_mem_o[l].astype(BF16), _row(norm_ffn[l]), wr_hi, wr_lo, b_router, seq)

        slots, meta = _slots(route)
        xs = _dispatch(slots[0], slots[1], h2, n_slots)
        ys = _experts(meta[0], meta[1], meta[2], meta[3], xs, w_gate[l], w_up[l], w_down[l])
        last = l == depth - 1
        g_out = _row(norm_final) if last else jnp.ones((1, d), F32)
        xt = _combine(slots[0], slots[1], x2, route, g_out, ys)
        if not last:
            raise NotImplementedError("depth > 1 needs an un-normalised combine output")
    return xt.reshape(batch, seq, d)
```

```python
import functools

import jax
import jax.numpy as jnp
from jax import lax
from jax.experimental import pallas as pl
from jax.experimental.pallas import tpu as pltpu

F32 = jnp.float32
BF16 = jnp.bfloat16

EPS = 1e-6
ROPE_THETA = 10000.0

D_MODEL = 2048
MEM_LEN = 256
MLA_HEADS = 8
MLA_NOPE = 128
MLA_ROPE = 64
MLA_QK = MLA_NOPE + MLA_ROPE
MLA_V = 128
MLA_Q_RANK = 512
MLA_KV_RANK = 256
MLA_PAD = 256
SB_HEADS = 8
SB_DIM = 128
MLA_OUT = MLA_HEADS * MLA_V
SB_OUT = SB_HEADS * SB_DIM
MEM_HEADS = 4
MEM_DIM = 128
N_GROUPS = 4
EXPERTS_PER_GROUP = 8
N_EXPERTS = N_GROUPS * EXPERTS_PER_GROUP
D_EXPERT = 512

LANES = 128
LAT_COLS = 1024
IN_COLS_PAD = LAT_COLS + 3 * SB_OUT

SLOT_TILE = 256
VMEM_LIMIT = 56 * 1024 * 1024


def _rms(x, g):
    return x * lax.rsqrt(jnp.mean(x * x, axis=-1, keepdims=True) + EPS) * g


def _dot(a, b):
    return jnp.dot(a, b, preferred_element_type=F32)


def _dot_nt(a, b):
    return lax.dot_general(a, b, (((1,), (1,)), ((), ())), preferred_element_type=F32)


def _split_bf16(x):
    hi = x.astype(BF16)
    lo = (x - hi.astype(F32)).astype(BF16)
    return hi, lo


def _params(*sem):
    return pltpu.CompilerParams(dimension_semantics=sem, vmem_limit_bytes=VMEM_LIMIT)


def _proj_in_kernel(x_ref, g_ref, w_ref, wvt_ref, lat_ref, qk_ref, vt_ref, h_ref):
    j = pl.program_id(1)
    last = pl.num_programs(1) - 1

    @pl.when(j == 0)
    def _():
        h_ref[...] = _rms(x_ref[...], g_ref[...]).astype(BF16)
        lat_ref[...] = _dot(h_ref[...], w_ref[...])

    @pl.when((j > 0) & (j < last))
    def _():
        qk_ref[...] = _dot(h_ref[...], w_ref[...]).astype(BF16)

    @pl.when(j == last)
    def _():
        vt_ref[...] = _dot_nt(wvt_ref[...], h_ref[...]).astype(BF16)


def _proj_in(x, g, w_pad, wvt, tm=512):
    t = x.shape[0]
    tn = LAT_COLS
    n_w = w_pad.shape[1] // tn
    return pl.pallas_call(
        _proj_in_kernel,
        grid=(t // tm, n_w + 1),
        in_specs=[
            pl.BlockSpec((tm, D_MODEL), lambda i, j: (i, 0)),
            pl.BlockSpec((1, D_MODEL), lambda i, j: (0, 0)),
            pl.BlockSpec((D_MODEL, tn), lambda i, j: (0, jnp.minimum(j, n_w - 1))),
            pl.BlockSpec(wvt.shape, lambda i, j: (0, 0)),
        ],
        out_specs=[
            pl.BlockSpec((tm, tn), lambda i, j: (i, 0)),
            pl.BlockSpec((tm, tn), lambda i, j: (i, jnp.clip(j - 1, 0, n_w - 2))),
            pl.BlockSpec((SB_OUT, tm), lambda i, j: (0, i)),
        ],
        out_shape=[
            jax.ShapeDtypeStruct((t, LAT_COLS), F32),
            jax.ShapeDtypeStruct((t, 2 * SB_OUT), BF16),
            jax.ShapeDtypeStruct((SB_OUT, t), BF16),
        ],
        scratch_shapes=[pltpu.VMEM((tm, D_MODEL), BF16)],
        compiler_params=_params("parallel", "arbitrary"),
        name="proj_in",
    )(x, g, w_pad, wvt)


def _mla_proj_kernel(lat_ref, pos_ref, invf_ref, gq_ref, gkv_ref, wqa_ref, wqb_ref, wk_ref, wvt_ref,
                     q_ref, k_ref, vt_ref):
    cq = _rms(lat_ref[:, :MLA_Q_RANK], gq_ref[...]).astype(BF16)
    ckv = _rms(lat_ref[:, MLA_Q_RANK:MLA_Q_RANK + MLA_KV_RANK], gkv_ref[...]).astype(BF16)
    ang = pos_ref[...].astype(F32) * invf_ref[...]
    live = lax.broadcasted_iota(jnp.int32, ang.shape, 1) < MLA_ROPE
    cos2 = jnp.where(live, jnp.cos(ang), 0.0)
    sin2 = jnp.where(live, jnp.sin(ang), 0.0)

    qa = _dot(cq, wqa_ref[...])
    qb = _dot(cq, wqb_ref[...])
    kn = _dot(ckv, wk_ref[...])
    vt_ref[...] = _dot_nt(wvt_ref[...], ckv).astype(BF16)
    k_pe = (lat_ref[:, 768:896] * cos2 + lat_ref[:, 896:1024] * sin2).astype(BF16)
    for h in range(MLA_HEADS):
        lo = h * MLA_PAD
        mid = lo + LANES
        q_ref[:, lo:mid] = qa[:, lo:mid].astype(BF16)
        q_ref[:, mid:mid + LANES] = (qa[:, mid:mid + LANES] * cos2
                                     + qb[:, h * LANES:(h + 1) * LANES] * sin2).astype(BF16)
        k_ref[:, lo:mid] = kn[:, h * LANES:(h + 1) * LANES].astype(BF16)
        k_ref[:, mid:mid + LANES] = k_pe


def _mla_proj(lat, pos, invf, gq, gkv, wqa, wqb, wk, wvt, tm=512):
    t = lat.shape[0]
    full = lambda a: pl.BlockSpec(a.shape, lambda i: (0, 0))
    return pl.pallas_call(
        _mla_proj_kernel,
        grid=(t // tm,),
        in_specs=[
            pl.BlockSpec((tm, LAT_COLS), lambda i: (i, 0)),
            pl.BlockSpec((tm, 1), lambda i: (i, 0)),
            full(invf), full(gq), full(gkv), full(wqa), full(wqb), full(wk), full(wvt),
        ],
        out_specs=[
            pl.BlockSpec((tm, MLA_HEADS * MLA_PAD), lambda i: (i, 0)),
            pl.BlockSpec((tm, MLA_HEADS * MLA_PAD), lambda i: (i, 0)),
            pl.BlockSpec((MLA_OUT, tm), lambda i: (0, i)),
        ],
        out_shape=[
            jax.ShapeDtypeStruct((t, MLA_HEADS * MLA_PAD), BF16),
            jax.ShapeDtypeStruct((t, MLA_HEADS * MLA_PAD), BF16),
            jax.ShapeDtypeStruct((MLA_OUT, t), BF16),
        ],
        compiler_params=_params("parallel"),
        name="mla_proj",
    )(lat, pos, invf, gq, gkv, wqa, wqb, wk, wvt)


LOG2E = 1.4426950408889634


def _three_stage(n_pairs, stage_a, stage_b, stage_c):
    stage_a(0, 0, True)
    stage_a(1, 1, True)
    stage_b(0, 0)

    def body(p, carry):
        stage_a(2 * p, 0, False)
        stage_b(2 * p - 1, 1)
        stage_c(2 * p - 2, 0)
        stage_a(2 * p + 1, 1, False)
        stage_b(2 * p, 0)
        stage_c(2 * p - 1, 1)
        return carry

    lax.fori_loop(1, n_pairs + 1, body, 0)
    last = 2 * n_pairs + 1
    stage_b(last, 1)
    stage_c(last - 1, 0)
    stage_c(last, 1)


def _two_stage(n_pairs, produce, consume):
    produce(0, 0, True)
    produce(1, 1, True)
    consume(0, 0)

    def body(p, carry):
        produce(2 * p, 0, False)
        consume(2 * p - 1, 1)
        produce(2 * p + 1, 1, False)
        consume(2 * p, 0)
        return carry

    lax.fori_loop(1, n_pairs + 1, body, 0)
    consume(2 * n_pairs + 1, 1)


def _mla_attn_kernel(q_ref, k_ref, vt_ref, o_ref, s0_ref, s1_ref, m_ref, l_ref, acc_ref, *, tq, tk):
    i = pl.program_id(2)
    q = q_ref[...]
    m_ref[...] = jnp.full(m_ref.shape, -jnp.inf, F32)
    l_ref[...] = jnp.zeros(l_ref.shape, F32)
    acc_ref[...] = jnp.zeros(acc_ref.shape, F32)
    scores = (s0_ref, s1_ref)

    def key_start(n):
        tile = jnp.where(n < 2, 2 * i + n, 2 * i + 1 - n)
        return pl.multiple_of(tile * tk, tk)

    def produce(n, slot, diagonal):
        ks = key_start(n)
        st = _dot_nt(k_ref[pl.ds(ks, tk), :], q)
        if diagonal:
            kpos = ks + lax.broadcasted_iota(jnp.int32, st.shape, 0)
            qpos = i * tq + lax.broadcasted_iota(jnp.int32, st.shape, 1)
            st = jnp.where(kpos <= qpos, st, -jnp.inf)
        scores[slot][...] = st

    def consume(n, slot):
        ks = key_start(n)
        st = scores[slot][...]
        m_old = m_ref[...]
        m_new = jnp.maximum(m_old, jnp.max(st, axis=0, keepdims=True))
        alpha = jnp.exp2(m_old - m_new)
        p = jnp.exp2(st - m_new)
        l_ref[...] = alpha * l_ref[...] + jnp.sum(p, axis=0, keepdims=True)
        acc_ref[...] = alpha * acc_ref[...] + _dot(vt_ref[:, pl.ds(ks, tk)], p.astype(BF16))
        m_ref[...] = m_new

    _two_stage(i, produce, consume)
    o_ref[...] = (acc_ref[...] / l_ref[...]).T


def _mla_attn(q, k, vt, batch, seq, tq=512):
    tk = tq // 2
    nq = seq // tq
    kern = functools.partial(_mla_attn_kernel, tq=tq, tk=tk)
    return pl.pallas_call(
        kern,
        grid=(batch, MLA_HEADS, nq),
        in_specs=[
            pl.BlockSpec((tq, MLA_PAD), lambda b, h, i: (b * nq + i, h)),
            pl.BlockSpec((seq, MLA_PAD), lambda b, h, i: (b, h)),
            pl.BlockSpec((MLA_V, seq), lambda b, h, i: (h, b)),
        ],
        out_specs=pl.BlockSpec((tq, MLA_V), lambda b, h, i: (b * nq + i, h)),
        out_shape=jax.ShapeDtypeStruct((batch * seq, MLA_OUT), F32),
        scratch_shapes=[
            pltpu.VMEM((tk, tq), F32),
            pltpu.VMEM((tk, tq), F32),
            pltpu.VMEM((1, tq), F32),
            pltpu.VMEM((1, tq), F32),
            pltpu.VMEM((MLA_V, tq), F32),
        ],
        compiler_params=_params("parallel", "parallel", "arbitrary"),
        name="mla_attn",
    )(q, k, vt)


def _neg_abs(x):
    bits = lax.bitcast_convert_type(x, jnp.uint32) | jnp.uint32(0x80000000)
    return lax.bitcast_convert_type(bits, F32)


def _sb_attn_kernel(q_ref, k_ref, vt_ref, o_ref, z0_ref, z1_ref, hl0_ref, hl1_ref, a0_ref, a1_ref,
                    acc_ref, c_ref, *, tq, tk):
    i = pl.program_id(2)
    q = q_ref[...]
    acc_ref[...] = jnp.zeros(acc_ref.shape, F32)
    c_ref[...] = jnp.zeros(c_ref.shape, F32)
    logits = (z0_ref, z1_ref)
    splits = (hl0_ref, hl1_ref)
    args = (a0_ref, a1_ref)
    col = lax.broadcasted_iota(jnp.int32, (tk, 2 * tk), 1)
    row = lax.broadcasted_iota(jnp.int32, (tk, 2 * tk), 0)
    neg_tri2 = jnp.where((col % tk) >= row, -1.0, 0.0).astype(BF16)

    def key_start(n):
        return pl.multiple_of((2 * i + 1 - n) * tk, tk)

    def stage_a(n, slot, masked):
        ks = key_start(n)
        z2 = _dot_nt(k_ref[pl.ds(ks, tk), :], q)
        sp = jnp.maximum(z2, 0.0) + jnp.log2(1.0 + jnp.exp2(_neg_abs(z2)))
        if masked:
            kpos = ks + lax.broadcasted_iota(jnp.int32, z2.shape, 0)
            qpos = i * tq + lax.broadcasted_iota(jnp.int32, z2.shape, 1)
            mask = kpos < qpos
            sp = jnp.where(mask, sp, 0.0)
            z2 = jnp.where(mask, z2, -jnp.inf)
        hi, lo = _split_bf16(sp)
        splits[slot][:tk, :] = hi
        splits[slot][tk:, :] = lo
        logits[slot][...] = z2

    def stage_b(n, slot):
        later = _dot(neg_tri2, splits[slot][...])
        args[slot][...] = logits[slot][...] + later + c_ref[...]
        c_ref[...] += later[0:1, :]

    def stage_c(n, slot):
        ks = key_start(n)
        a = jnp.exp2(args[slot][...]).astype(BF16)
        acc_ref[...] += _dot(vt_ref[:, pl.ds(ks, tk)], a)

    _three_stage(i, stage_a, stage_b, stage_c)
    o_ref[...] = acc_ref[...].T


def _sb_attn(qk, vt, batch, seq, tq=512):
    tk = tq // 2
    nq = seq // tq
    kern = functools.partial(_sb_attn_kernel, tq=tq, tk=tk)
    return pl.pallas_call(
        kern,
        grid=(batch, SB_HEADS, nq),
        in_specs=[
            pl.BlockSpec((tq, SB_DIM), lambda b, h, i: (b * nq + i, h)),
            pl.BlockSpec((seq, SB_DIM), lambda b, h, i: (b, SB_HEADS + h)),
            pl.BlockSpec((SB_DIM, seq), lambda b, h, i: (h, b)),
        ],
        out_specs=pl.BlockSpec((tq, SB_DIM), lambda b, h, i: (b * nq + i, h)),
        out_shape=jax.ShapeDtypeStruct((batch * seq, SB_OUT), F32),
        scratch_shapes=[
            pltpu.VMEM((tk, tq), F32),
            pltpu.VMEM((tk, tq), F32),
            pltpu.VMEM((2 * tk, tq), BF16),
            pltpu.VMEM((2 * tk, tq), BF16),
            pltpu.VMEM((tk, tq), F32),
            pltpu.VMEM((tk, tq), F32),
            pltpu.VMEM((SB_DIM, tq), F32),
            pltpu.VMEM((1, tq), F32),
        ],
        compiler_params=_params("parallel", "parallel", "arbitrary"),
        name="sb_attn",
    )(qk, qk, vt)


def _out_proj_kernel(x_ref, oa_ref, ob_ref, ga_ref, gb_ref, w_ref, y_ref):
    na = _rms(oa_ref[...], ga_ref[...]).astype(BF16)
    nb = _rms(ob_ref[...], gb_ref[...]).astype(BF16)
    y_ref[...] = x_ref[...] + _dot(na, w_ref[:MLA_OUT, :]) + _dot(nb, w_ref[MLA_OUT:, :])


def _out_proj(x, oa, ob, ga, gb, w, tm=512):
    t = x.shape[0]
    full = lambda a: pl.BlockSpec(a.shape, lambda i: (0, 0))
    return pl.pallas_call(
        _out_proj_kernel,
        grid=(t // tm,),
        in_specs=[
            pl.BlockSpec((tm, D_MODEL), lambda i: (i, 0)),
            pl.BlockSpec((tm, MLA_OUT), lambda i: (i, 0)),
            pl.BlockSpec((tm, SB_OUT), lambda i: (i, 0)),
            full(ga), full(gb), full(w),
        ],
        out_specs=pl.BlockSpec((tm, D_MODEL), lambda i: (i, 0)),
        out_shape=jax.ShapeDtypeStruct((t, D_MODEL), F32),
        compiler_params=_params("parallel"),
        name="out_proj",
    )(x, oa, ob, ga, gb, w)


def _mem_kv_kernel(mem_ref, g_ref, w_ref, kv_ref):
    kv_ref[...] = _dot(_rms(mem_ref[...], g_ref[...]).astype(BF16), w_ref[...]).astype(BF16)


def _mem_kv(mem, g, w):
    m = mem.shape[0]
    n = w.shape[1]
    full = lambda a: pl.BlockSpec(a.shape, lambda i: (0, 0))
    return pl.pallas_call(
        _mem_kv_kernel,
        grid=(1,),
        in_specs=[full(mem), full(g), full(w)],
        out_specs=pl.BlockSpec((m, n), lambda i: (0, 0)),
        out_shape=jax.ShapeDtypeStruct((m, n), BF16),
        compiler_params=_params("arbitrary"),
        name="mem_kv",
    )(mem, g, w)


GROUP_LANE0 = N_EXPERTS


def _mem_route_kernel(x_ref, gx_ref, wq_ref, kv_ref, wo_ref, gf_ref, wrh_ref, wrl_ref, br_ref,
                      x2_ref, h2_ref, route_ref):
    x1 = x_ref[...]
    q = _dot(_rms(x1, gx_ref[...]).astype(BF16), wq_ref[...]).astype(BF16)
    scale = MEM_DIM ** -0.5
    kw = MEM_HEADS * MEM_DIM
    heads = []
    for h in range(MEM_HEADS):
        lo = h * MEM_DIM
        s = _dot_nt(q[:, lo:lo + MEM_DIM], kv_ref[:, lo:lo + MEM_DIM]) * scale
        e = jnp.exp(s - jnp.max(s, axis=-1, keepdims=True))
        p = (e / jnp.sum(e, axis=-1, keepdims=True)).astype(BF16)
        heads.append(_dot(p, kv_ref[:, kw + lo:kw + lo + MEM_DIM]).astype(BF16))
    o = jnp.concatenate(heads, axis=-1)
    x2 = x1 + _dot(o, wo_ref[...])
    x2_ref[...] = x2
    h2 = _rms(x2, gf_ref[...])
    h2_ref[...] = h2

    hh, hl = _split_bf16(h2)
    lg = _dot(hh, wrh_ref[...]) + _dot(hl, wrh_ref[...]) + _dot(hh, wrl_ref[...]) + br_ref[...]
    lane = lax.broadcasted_iota(jnp.int32, lg.shape, 1)
    big = jnp.int32(1 << 20)
    ninf = -jnp.inf

    def lane_max(v):
        return jnp.max(v, axis=-1, keepdims=True)

    def first_lane(cond):
        return jnp.min(jnp.where(cond, lane, big), axis=-1, keepdims=True)

    is_g = (lane >= GROUP_LANE0) & (lane < GROUP_LANE0 + N_GROUPS)
    g_max = lane_max(jnp.where(is_g, lg, ninf))
    g_sum = jnp.sum(jnp.where(is_g, jnp.exp(lg - g_max), 0.0), axis=-1, keepdims=True)
    p_g = 1.0 / g_sum
    g_idx = first_lane(is_g & (lg == g_max)) - GROUP_LANE0
    in_grp = (lane < N_EXPERTS) & ((lane // EXPERTS_PER_GROUP) == g_idx)
    e_max = lane_max(jnp.where(in_grp, lg, ninf))
    e_sum = jnp.sum(jnp.where(in_grp, jnp.exp(lg - e_max), 0.0), axis=-1, keepdims=True)
    i1 = first_lane(in_grp & (lg == e_max))
    rest = in_grp & (lane != i1)
    e_max2 = lane_max(jnp.where(rest, lg, ninf))
    i2 = first_lane(rest & (lg == e_max2))
    p1 = 1.0 / e_sum
    p2 = jnp.exp(e_max2 - e_max) / e_sum
    den = p1 + p2
    gate1 = p_g * (p1 / den)
    gate2 = p_g * (p2 / den)
    route = jnp.where(lane == 0, i1.astype(F32),
                      jnp.where(lane == 1, i2.astype(F32),
                                jnp.where(lane == 2, gate1,
                                          jnp.where(lane == 3, gate2, 0.0))))
    route_ref[...] = route


def _mem_route(x1, gx, wq, kv, wo, gf, wrh, wrl, br, seq, tm=512):
    t = x1.shape[0]
    per_batch = seq // tm
    full = lambda a: pl.BlockSpec(a.shape, lambda i: (0, 0))
    return pl.pallas_call(
        _mem_route_kernel,
        grid=(t // tm,),
        in_specs=[
            pl.BlockSpec((tm, D_MODEL), lambda i: (i, 0)),
            full(gx), full(wq),
            pl.BlockSpec((MEM_LEN, kv.shape[1]), lambda i: (i // per_batch, 0)),
            full(wo), full(gf), full(wrh), full(wrl), full(br),
        ],
        out_specs=[
            pl.BlockSpec((tm, D_MODEL), lambda i: (i, 0)),
            pl.BlockSpec((tm, D_MODEL), lambda i: (i, 0)),
            pl.BlockSpec((tm, LANES), lambda i: (i, 0)),
        ],
        out_shape=[
            jax.ShapeDtypeStruct((t, D_MODEL), F32),
            jax.ShapeDtypeStruct((t, D_MODEL), F32),
            jax.ShapeDtypeStruct((t, LANES), F32),
        ],
        compiler_params=_params("parallel"),
        name="mem_route",
    )(x1, gx, wq, kv, wo, gf, wrh, wrl, br)


SLOT_BLK = 256


def _slots_kernel(route_ref, slot_ref, meta_ref, cum_ref, *, n_tok):
    nblk = n_tok // SLOT_BLK
    e_iota = lax.broadcasted_iota(jnp.int32, (LANES, SLOT_BLK), 0)
    incl = (lax.broadcasted_iota(jnp.int32, (SLOT_BLK, SLOT_BLK), 0)
            <= lax.broadcasted_iota(jnp.int32, (SLOT_BLK, SLOT_BLK), 1)).astype(BF16)

    def onehots(b):
        ts = pl.multiple_of(b * SLOT_BLK, SLOT_BLK)
        ids = route_ref[pl.ds(ts, SLOT_BLK), :].T
        oh1 = e_iota == ids[0:1, :].astype(jnp.int32)
        oh2 = e_iota == ids[1:2, :].astype(jnp.int32)
        return ts, oh1, oh2

    def count(b, carry):
        ts, oh1, oh2 = onehots(b)
        cnt = (oh1 | oh2).astype(F32).astype(BF16)
        c = _dot(cnt, incl) + carry
        cum_ref[:, pl.ds(ts, SLOT_BLK)] = c
        return c[:, SLOT_BLK - 1:SLOT_BLK]

    total = lax.fori_loop(0, nblk, count, jnp.zeros((LANES, 1), F32))
    tiles = jnp.floor((total + (SLOT_TILE - 1)) * (1.0 / SLOT_TILE))
    below = (lax.broadcasted_iota(jnp.int32, (LANES, LANES), 1)
             < lax.broadcasted_iota(jnp.int32, (LANES, LANES), 0)).astype(BF16)
    tile_lo = _dot(below, jnp.broadcast_to(tiles, (LANES, LANES)).astype(BF16))
    base = tile_lo[:, 0:1] * SLOT_TILE

    def assign(b, carry):
        ts, oh1, oh2 = onehots(b)
        pos = base + cum_ref[:, pl.ds(ts, SLOT_BLK)] - 1.0
        s1 = jnp.sum(jnp.where(oh1, pos, 0.0), axis=0, keepdims=True)
        s2 = jnp.sum(jnp.where(oh2, pos, 0.0), axis=0, keepdims=True)
        row = lax.broadcasted_iota(jnp.int32, (8, SLOT_BLK), 0)
        slot_ref[:, pl.ds(ts, SLOT_BLK)] = jnp.where(row == 0, s1, jnp.where(row == 1, s2, 0.0)).astype(jnp.int32)
        return carry

    lax.fori_loop(0, nblk, assign, 0)

    tile_hi = tile_lo + tiles
    tix = lax.broadcasted_iota(jnp.int32, (LANES, LANES), 1).astype(F32)
    is_e = lax.broadcasted_iota(jnp.int32, (LANES, LANES), 0) < N_EXPERTS
    owner = jnp.sum(jnp.where(is_e & (tile_hi <= tix), 1.0, 0.0), axis=0, keepdims=True)
    n_tiles = jnp.max(jnp.where(is_e, tile_hi, 0.0), axis=0, keepdims=True)
    valid = tix[0:1, :] < n_tiles
    mine = is_e & (tile_lo <= tix) & (tix < tile_hi)
    live = jnp.sum(jnp.where(mine, jnp.clip(total - (tix - tile_lo) * SLOT_TILE, 0.0, SLOT_TILE), 0.0),
                   axis=0, keepdims=True)
    last_owner = jnp.max(jnp.where(valid, owner, 0.0), axis=1, keepdims=True)
    owner = jnp.where(valid, owner, last_owner)
    last_tile = jnp.maximum(n_tiles - 1.0, 0.0)
    row = lax.broadcasted_iota(jnp.int32, (8, LANES), 0)
    meta = jnp.where(row == 0, owner,
                     jnp.where(row == 1, valid.astype(F32),
                               jnp.where(row == 2, live,
                                         jnp.where(row == 3, jnp.where(valid, tix[0:1, :], last_tile), 0.0))))
    meta_ref[...] = meta.astype(jnp.int32)


def _slots(route):
    t = route.shape[0]
    kern = functools.partial(_slots_kernel, n_tok=t)
    return pl.pallas_call(
        kern,
        grid=(1,),
        in_specs=[pl.BlockSpec(route.shape, lambda i: (0, 0))],
        out_specs=[
            pl.BlockSpec((8, t), lambda i: (0, 0)),
            pl.BlockSpec((8, LANES), lambda i: (0, 0)),
        ],
        out_shape=[
            jax.ShapeDtypeStruct((8, t), jnp.int32),
            jax.ShapeDtypeStruct((8, LANES), jnp.int32),
        ],
        scratch_shapes=[pltpu.VMEM((LANES, t), F32)],
        compiler_params=_params("arbitrary"),
        name="slots",
    )(route)


def _row_copy(src_ref, src_row, dst_ref, dst_row, sem):
    return pltpu.make_async_copy(src_ref.at[pl.ds(src_row, 1), :], dst_ref.at[pl.ds(dst_row, 1), :], sem)


def _dispatch_kernel(s1_ref, s2_ref, h_ref, xs_ref, sem, *, tm):
    t0 = pl.program_id(0) * tm

    def issue(r, c):
        _row_copy(h_ref, r, xs_ref, s1_ref[t0 + r], sem).start()
        _row_copy(h_ref, r, xs_ref, s2_ref[t0 + r], sem).start()
        return c

    lax.fori_loop(0, tm, issue, 0)

    def drain(r, c):
        _row_copy(h_ref, r, xs_ref, s1_ref[t0 + r], sem).wait()
        _row_copy(h_ref, r, xs_ref, s2_ref[t0 + r], sem).wait()
        return c

    lax.fori_loop(0, tm, drain, 0)


def _dispatch(slot1, slot2, h2, n_slots, tm=256):
    t = h2.shape[0]
    kern = functools.partial(_dispatch_kernel, tm=tm)
    return pl.pallas_call(
        kern,
        grid_spec=pltpu.PrefetchScalarGridSpec(
            num_scalar_prefetch=2,
            grid=(t // tm,),
            in_specs=[pl.BlockSpec((tm, D_MODEL), lambda i, s1, s2: (i, 0))],
            out_specs=pl.BlockSpec(memory_space=pl.ANY),
            scratch_shapes=[pltpu.SemaphoreType.DMA(())],
        ),
        out_shape=jax.ShapeDtypeStruct((n_slots, D_MODEL), F32),
        compiler_params=_params("arbitrary"),
        name="dispatch",
    )(slot1, slot2, h2)


def _experts_kernel(own_ref, valid_ref, live_ref, blk_ref, xs_ref, wg_ref, wu_ref, wd_ref, ys_ref,
                    wgb_ref, wub_ref, wdb_ref):
    i = pl.program_id(0)
    new_expert = (i == 0) | (own_ref[i] != own_ref[jnp.maximum(i - 1, 0)])

    @pl.when(valid_ref[i] > 0)
    def _():
        @pl.when(new_expert)
        def _():
            wgb_ref[...] = wg_ref[0].astype(BF16)
            wub_ref[...] = wu_ref[0].astype(BF16)
            wdb_ref[...] = wd_ref[0].astype(BF16)

        rows = lax.broadcasted_iota(jnp.int32, (SLOT_TILE, 1), 0)
        x = jnp.where(rows < live_ref[i], xs_ref[...], 0.0).astype(BF16)
        a = _dot(x, wgb_ref[...])
        u = _dot(x, wub_ref[...])
        act = (a * jax.nn.sigmoid(a) * u).astype(BF16)
        ys_ref[...] = _dot(act, wdb_ref[...])


def _experts(own, valid, live, blk, xs, wg, wu, wd):
    n_tiles = xs.shape[0] // SLOT_TILE
    return pl.pallas_call(
        _experts_kernel,
        grid_spec=pltpu.PrefetchScalarGridSpec(
            num_scalar_prefetch=4,
            grid=(n_tiles,),
            in_specs=[
                pl.BlockSpec((SLOT_TILE, D_MODEL), lambda i, o, v, l, b: (b[i], 0)),
                pl.BlockSpec((1, D_MODEL, D_EXPERT), lambda i, o, v, l, b: (o[i], 0, 0)),
                pl.BlockSpec((1, D_MODEL, D_EXPERT), lambda i, o, v, l, b: (o[i], 0, 0)),
                pl.BlockSpec((1, D_EXPERT, D_MODEL), lambda i, o, v, l, b: (o[i], 0, 0)),
            ],
            out_specs=pl.BlockSpec((SLOT_TILE, D_MODEL), lambda i, o, v, l, b: (b[i], 0)),
            scratch_shapes=[
                pltpu.VMEM((D_MODEL, D_EXPERT), BF16),
                pltpu.VMEM((D_MODEL, D_EXPERT), BF16),
                pltpu.VMEM((D_EXPERT, D_MODEL), BF16),
            ],
        ),
        out_shape=jax.ShapeDtypeStruct(xs.shape, F32),
        compiler_params=_params("arbitrary"),
        name="experts",
    )(own, valid, live, blk, xs, wg, wu, wd)


def _combine_kernel(s1_ref, s2_ref, x_ref, route_ref, g_ref, ys_ref, y_ref, buf_ref, sem, *, tm):
    t0 = pl.program_id(0) * tm

    def issue(r, c):
        _row_copy(ys_ref, s1_ref[t0 + r], buf_ref.at[0], r, sem).start()
        _row_copy(ys_ref, s2_ref[t0 + r], buf_ref.at[1], r, sem).start()
        return c

    lax.fori_loop(0, tm, issue, 0)

    def drain(r, c):
        _row_copy(ys_ref, s1_ref[t0 + r], buf_ref.at[0], r, sem).wait()
        _row_copy(ys_ref, s2_ref[t0 + r], buf_ref.at[1], r, sem).wait()
        return c

    lax.fori_loop(0, tm, drain, 0)
    y = x_ref[...] + route_ref[:, 2:3] * buf_ref[0] + route_ref[:, 3:4] * buf_ref[1]
    y_ref[...] = _rms(y, g_ref[...])


def _combine(slot1, slot2, x2, route, g, ys, tm=256):
    t = x2.shape[0]
    kern = functools.partial(_combine_kernel, tm=tm)
    return pl.pallas_call(
        kern,
        grid_spec=pltpu.PrefetchScalarGridSpec(
            num_scalar_prefetch=2,
            grid=(t // tm,),
            in_specs=[
                pl.BlockSpec((tm, D_MODEL), lambda i, s1, s2: (i, 0)),
                pl.BlockSpec((tm, LANES), lambda i, s1, s2: (i, 0)),
                pl.BlockSpec((1, D_MODEL), lambda i, s1, s2: (0, 0)),
                pl.BlockSpec(memory_space=pl.ANY),
            ],
            out_specs=pl.BlockSpec((tm, D_MODEL), lambda i, s1, s2: (i, 0)),
            scratch_shapes=[
                pltpu.VMEM((2, tm, D_MODEL), F32),
                pltpu.SemaphoreType.DMA(()),
            ],
        ),
        out_shape=jax.ShapeDtypeStruct((t, D_MODEL), F32),
        compiler_params=_params("arbitrary"),
        name="combine",
    )(slot1, slot2, x2, route, g, ys)


def _swap_halves(w):
    half = w.shape[-1] // 2
    return jnp.concatenate([-w[..., half:], w[..., :half]], axis=-1)


def _row(v):
    return v.reshape(1, -1).astype(F32)


def kernel(x, mem, positions, norm_mix, w_in, norm_q_lat, w_q_b, norm_kv_lat, w_kv_b, norm_mla_out, norm_sb_out, w_out, norm_mem_x, norm_mem_src, w_mem_q, w_mem_kv, w_mem_o, norm_ffn, w_group, b_group, w_expert_router, b_expert, w_gate, w_up, w_down, norm_final):
    batch, seq, d = x.shape
    t = batch * seq
    depth = w_in.shape[0]
    assert depth == 1, "single-layer trunk only"
    xt = x.reshape(t, d)
    pos = positions.reshape(t, 1)
    inv_freq = ROPE_THETA ** (-jnp.arange(0, MLA_ROPE, 2, dtype=F32) / MLA_ROPE)
    invf = jnp.concatenate([inv_freq, inv_freq, jnp.zeros((LANES - MLA_ROPE,), F32)]).reshape(1, LANES)
    n_slots = (2 * t // SLOT_TILE + N_EXPERTS) * SLOT_TILE

    for l in range(depth):
        wi = w_in[l]
        lat_w = MLA_Q_RANK + MLA_KV_RANK
        w_kpe = wi[:, lat_w:lat_w + MLA_ROPE]
        zpad = jnp.zeros((d, LANES - MLA_ROPE), F32)
        sb0 = lat_w + MLA_ROPE
        w_sbq = wi[:, sb0:sb0 + SB_OUT] * (SB_DIM ** -0.5 * LOG2E)
        w_in_pad = jnp.concatenate(
            [wi[:, :sb0], zpad, _swap_halves(w_kpe), zpad, w_sbq, wi[:, sb0 + SB_OUT:sb0 + 2 * SB_OUT]],
            axis=1).astype(BF16)
        w_sbv_t = wi[:, sb0 + 2 * SB_OUT:].T.astype(BF16)
        wq = w_q_b[l].reshape(MLA_Q_RANK, MLA_HEADS, MLA_QK) * (MLA_QK ** -0.5 * LOG2E)
        wq_pe = wq[:, :, MLA_NOPE:]
        zq = jnp.zeros((MLA_Q_RANK, MLA_HEADS, MLA_PAD - MLA_QK), F32)
        wqa = jnp.concatenate([wq, zq], axis=-1).reshape(MLA_Q_RANK, MLA_HEADS * MLA_PAD).astype(BF16)
        wqb = jnp.concatenate([_swap_halves(wq_pe), zq], axis=-1).reshape(MLA_Q_RANK, MLA_HEADS * LANES).astype(BF16)
        wkv = w_kv_b[l].reshape(MLA_KV_RANK, MLA_HEADS, MLA_NOPE + MLA_V)
        wk = wkv[:, :, :MLA_NOPE].reshape(MLA_KV_RANK, MLA_OUT).astype(BF16)
        wvt = wkv[:, :, MLA_NOPE:].reshape(MLA_KV_RANK, MLA_OUT).T.astype(BF16)
        w_router = jnp.concatenate(
            [w_expert_router[l], w_group[l], jnp.zeros((d, LANES - N_EXPERTS - N_GROUPS), F32)], axis=1)
        wr_hi = w_router.astype(BF16)
        wr_lo = (w_router - wr_hi.astype(F32)).astype(BF16)
        b_router = jnp.concatenate(
            [b_expert[l].astype(F32), b_group[l].astype(F32), jnp.zeros((LANES - N_EXPERTS - N_GROUPS,), F32)]
        ).reshape(1, LANES)

        lat, sb_qk, sb_vt = _proj_in(xt, _row(norm_mix[l]), w_in_pad, w_sbv_t)
        q, k, vt = _mla_proj(lat, pos, invf, _row(norm_q_lat[l]), _row(norm_kv_lat[l]), wqa, wqb, wk, wvt)
        o_mla = _mla_attn(q, k, vt, batch, seq)
        o_sb = _sb_attn(sb_qk, sb_vt, batch, seq)
        x1 = _out_proj(xt, o_mla, o_sb, _row(norm_mla_out[l]), _row(norm_sb_out[l]), w_out[l].astype(BF16))

        kv = _mem_kv(mem.reshape(batch * MEM_LEN, d), _row(norm_mem_src[l]), w_mem_kv[l].astype(BF16))
        x2, h2, route = _mem_route(x1, _row(norm_mem_x[l]), w_mem_q[l].astype(BF16), kv,
                                   w_mem_o[l].astype(BF16), _row(norm_ffn[l]), wr_hi, wr_lo, b_router, seq)

        slots, meta = _slots(route)
        xs = _dispatch(slots[0], slots[1], h2, n_slots)
        ys = _experts(meta[0], meta[1], meta[2], meta[3], xs, w_gate[l], w_up[l], w_down[l])
        xt = _combine(slots[0], slots[1], x2, route, _row(norm_final), ys)
    return xt.reshape(batch, seq, d)
```

```python
import functools

import jax
import jax.numpy as jnp
from jax import lax
from jax.experimental import pallas as pl
from jax.experimental.pallas import tpu as pltpu

F32 = jnp.float32
BF16 = jnp.bfloat16

EPS = 1e-6
ROPE_THETA = 10000.0

D_MODEL = 2048
MEM_LEN = 256
MLA_HEADS = 8
MLA_NOPE = 128
MLA_ROPE = 64
MLA_QK = MLA_NOPE + MLA_ROPE
MLA_V = 128
MLA_Q_RANK = 512
MLA_KV_RANK = 256
MLA_PAD = 256
SB_HEADS = 8
SB_DIM = 128
MLA_OUT = MLA_HEADS * MLA_V
SB_OUT = SB_HEADS * SB_DIM
MEM_HEADS = 4
MEM_DIM = 128
N_GROUPS = 4
EXPERTS_PER_GROUP = 8
N_EXPERTS = N_GROUPS * EXPERTS_PER_GROUP
D_EXPERT = 512

LANES = 128
LAT_COLS = 1024
IN_COLS_PAD = LAT_COLS + 3 * SB_OUT

SLOT_TILE = 256
VMEM_LIMIT = 56 * 1024 * 1024


def _rms(x, g):
    return x * lax.rsqrt(jnp.mean(x * x, axis=-1, keepdims=True) + EPS) * g


def _dot(a, b):
    return jnp.dot(a, b, preferred_element_type=F32)


def _dot_nt(a, b):
    return lax.dot_general(a, b, (((1,), (1,)), ((), ())), preferred_element_type=F32)


def _split_bf16(x):
    hi = x.astype(BF16)
    lo = (x - hi.astype(F32)).astype(BF16)
    return hi, lo


def _params(*sem):
    return pltpu.CompilerParams(dimension_semantics=sem, vmem_limit_bytes=VMEM_LIMIT)


def _proj_in_kernel(x_ref, g_ref, w_ref, wvt_ref, lat_ref, qk_ref, vt_ref, h_ref):
    j = pl.program_id(1)
    last = pl.num_programs(1) - 1

    @pl.when(j == 0)
    def _():
        h_ref[...] = _rms(x_ref[...], g_ref[...]).astype(BF16)
        lat_ref[...] = _dot(h_ref[...], w_ref[...])

    @pl.when((j > 0) & (j < last))
    def _():
        qk_ref[...] = _dot(h_ref[...], w_ref[...]).astype(BF16)

    @pl.when(j == last)
    def _():
        vt_ref[...] = _dot_nt(wvt_ref[...], h_ref[...]).astype(BF16)


def _proj_in(x, g, w_pad, wvt, tm=512):
    t = x.shape[0]
    tn = LAT_COLS
    n_w = w_pad.shape[1] // tn
    return pl.pallas_call(
        _proj_in_kernel,
        grid=(t // tm, n_w + 1),
        in_specs=[
            pl.BlockSpec((tm, D_MODEL), lambda i, j: (i, 0)),
            pl.BlockSpec((1, D_MODEL), lambda i, j: (0, 0)),
            pl.BlockSpec((D_MODEL, tn), lambda i, j: (0, jnp.minimum(j, n_w - 1))),
            pl.BlockSpec(wvt.shape, lambda i, j: (0, 0)),
        ],
        out_specs=[
            pl.BlockSpec((tm, tn), lambda i, j: (i, 0)),
            pl.BlockSpec((tm, tn), lambda i, j: (i, jnp.clip(j - 1, 0, n_w - 2))),
            pl.BlockSpec((SB_OUT, tm), lambda i, j: (0, i)),
        ],
        out_shape=[
            jax.ShapeDtypeStruct((t, LAT_COLS), F32),
            jax.ShapeDtypeStruct((t, 2 * SB_OUT), BF16),
            jax.ShapeDtypeStruct((SB_OUT, t), BF16),
        ],
        scratch_shapes=[pltpu.VMEM((tm, D_MODEL), BF16)],
        compiler_params=_params("parallel", "arbitrary"),
        name="proj_in",
    )(x, g, w_pad, wvt)


def _mla_proj_kernel(lat_ref, pos_ref, invf_ref, gq_ref, gkv_ref, wqa_ref, wqb_ref, wk_ref, wvt_ref,
                     q_ref, k_ref, vt_ref):
    cq = _rms(lat_ref[:, :MLA_Q_RANK], gq_ref[...]).astype(BF16)
    ckv = _rms(lat_ref[:, MLA_Q_RANK:MLA_Q_RANK + MLA_KV_RANK], gkv_ref[...]).astype(BF16)
    ang = pos_ref[...].astype(F32) * invf_ref[...]
    live = lax.broadcasted_iota(jnp.int32, ang.shape, 1) < MLA_ROPE
    cos2 = jnp.where(live, jnp.cos(ang), 0.0)
    sin2 = jnp.where(live, jnp.sin(ang), 0.0)

    qa = _dot(cq, wqa_ref[...])
    qb = _dot(cq, wqb_ref[...])
    kn = _dot(ckv, wk_ref[...])
    vt_ref[...] = _dot_nt(wvt_ref[...], ckv).astype(BF16)
    k_pe = (lat_ref[:, 768:896] * cos2 + lat_ref[:, 896:1024] * sin2).astype(BF16)
    for h in range(MLA_HEADS):
        lo = h * MLA_PAD
        mid = lo + LANES
        q_ref[:, lo:mid] = qa[:, lo:mid].astype(BF16)
        q_ref[:, mid:mid + LANES] = (qa[:, mid:mid + LANES] * cos2
                                     + qb[:, h * LANES:(h + 1) * LANES] * sin2).astype(BF16)
        k_ref[:, lo:mid] = kn[:, h * LANES:(h + 1) * LANES].astype(BF16)
        k_ref[:, mid:mid + LANES] = k_pe


def _mla_proj(lat, pos, invf, gq, gkv, wqa, wqb, wk, wvt, tm=512):
    t = lat.shape[0]
    full = lambda a: pl.BlockSpec(a.shape, lambda i: (0, 0))
    return pl.pallas_call(
        _mla_proj_kernel,
        grid=(t // tm,),
        in_specs=[
            pl.BlockSpec((tm, LAT_COLS), lambda i: (i, 0)),
            pl.BlockSpec((tm, 1), lambda i: (i, 0)),
            full(invf), full(gq), full(gkv), full(wqa), full(wqb), full(wk), full(wvt),
        ],
        out_specs=[
            pl.BlockSpec((tm, MLA_HEADS * MLA_PAD), lambda i: (i, 0)),
            pl.BlockSpec((tm, MLA_HEADS * MLA_PAD), lambda i: (i, 0)),
            pl.BlockSpec((MLA_OUT, tm), lambda i: (0, i)),
        ],
        out_shape=[
            jax.ShapeDtypeStruct((t, MLA_HEADS * MLA_PAD), BF16),
            jax.ShapeDtypeStruct((t, MLA_HEADS * MLA_PAD), BF16),
            jax.ShapeDtypeStruct((MLA_OUT, t), BF16),
        ],
        compiler_params=_params("parallel"),
        name="mla_proj",
    )(lat, pos, invf, gq, gkv, wqa, wqb, wk, wvt)


LOG2E = 1.4426950408889634


def _three_stage(n_pairs, stage_a, stage_b, stage_c):
    a_issue, a_finish = stage_a
    b_issue, b_finish = stage_b
    c_issue, c_finish = stage_c

    def run_a(n, slot, first):
        a_finish(n, slot, a_issue(n, slot, first), first)

    run_a(0, 0, True)
    run_a(1, 1, True)
    b_finish(0, 0, b_issue(0, 0))

    def half(na, sa, nb, sb, nc, sc):
        ra = a_issue(na, sa, False)
        rb = b_issue(nb, sb)
        rc = c_issue(nc, sc)
        a_finish(na, sa, ra, False)
        b_finish(nb, sb, rb)
        c_finish(nc, sc, rc)

    def body(p, carry):
        half(2 * p, 0, 2 * p - 1, 1, 2 * p - 2, 0)
        half(2 * p + 1, 1, 2 * p, 0, 2 * p - 1, 1)
        return carry

    lax.fori_loop(1, n_pairs + 1, body, 0)
    last = 2 * n_pairs + 1
    rb = b_issue(last, 1)
    rc = c_issue(last - 1, 0)
    b_finish(last, 1, rb)
    c_finish(last - 1, 0, rc)
    c_finish(last, 1, c_issue(last, 1))


def _two_stage(n_pairs, produce, consume):
    produce(0, 0, True)
    produce(1, 1, True)
    consume(0, 0)

    def body(p, carry):
        produce(2 * p, 0, False)
        consume(2 * p - 1, 1)
        produce(2 * p + 1, 1, False)
        consume(2 * p, 0)
        return carry

    lax.fori_loop(1, n_pairs + 1, body, 0)
    consume(2 * n_pairs + 1, 1)


def _mla_attn_kernel(q_ref, k_ref, vt_ref, o_ref, s0_ref, s1_ref, m_ref, l_ref, acc_ref, *, tq, tk):
    i = pl.program_id(2)
    q = q_ref[...]
    m_ref[...] = jnp.full(m_ref.shape, -jnp.inf, F32)
    l_ref[...] = jnp.zeros(l_ref.shape, F32)
    acc_ref[...] = jnp.zeros(acc_ref.shape, F32)
    scores = (s0_ref, s1_ref)

    def key_start(n):
        tile = jnp.where(n < 2, 2 * i + n, 2 * i + 1 - n)
        return pl.multiple_of(tile * tk, tk)

    def produce(n, slot, diagonal):
        ks = key_start(n)
        st = _dot_nt(k_ref[pl.ds(ks, tk), :], q)
        if diagonal:
            kpos = ks + lax.broadcasted_iota(jnp.int32, st.shape, 0)
            qpos = i * tq + lax.broadcasted_iota(jnp.int32, st.shape, 1)
            st = jnp.where(kpos <= qpos, st, -jnp.inf)
        scores[slot][...] = st

    def consume(n, slot):
        ks = key_start(n)
        st = scores[slot][...]
        m_old = m_ref[...]
        m_new = jnp.maximum(m_old, jnp.max(st, axis=0, keepdims=True))
        alpha = jnp.exp2(m_old - m_new)
        p = jnp.exp2(st - m_new)
        l_ref[...] = alpha * l_ref[...] + jnp.sum(p, axis=0, keepdims=True)
        acc_ref[...] = alpha * acc_ref[...] + _dot(vt_ref[:, pl.ds(ks, tk)], p.astype(BF16))
        m_ref[...] = m_new

    _two_stage(i, produce, consume)
    o_ref[...] = (acc_ref[...] / l_ref[...]).T


def _mla_attn(q, k, vt, batch, seq, tq=512):
    tk = tq // 2
    nq = seq // tq
    kern = functools.partial(_mla_attn_kernel, tq=tq, tk=tk)
    return pl.pallas_call(
        kern,
        grid=(batch, MLA_HEADS, nq),
        in_specs=[
            pl.BlockSpec((tq, MLA_PAD), lambda b, h, i: (b * nq + i, h)),
            pl.BlockSpec((seq, MLA_PAD), lambda b, h, i: (b, h)),
            pl.BlockSpec((MLA_V, seq), lambda b, h, i: (h, b)),
        ],
        out_specs=pl.BlockSpec((tq, MLA_V), lambda b, h, i: (b * nq + i, h)),
        out_shape=jax.ShapeDtypeStruct((batch * seq, MLA_OUT), F32),
        scratch_shapes=[
            pltpu.VMEM((tk, tq), F32),
            pltpu.VMEM((tk, tq), F32),
            pltpu.VMEM((1, tq), F32),
            pltpu.VMEM((1, tq), F32),
            pltpu.VMEM((MLA_V, tq), F32),
        ],
        compiler_params=_params("parallel", "parallel", "arbitrary"),
        name="mla_attn",
    )(q, k, vt)


def _neg_abs(x):
    bits = lax.bitcast_convert_type(x, jnp.uint32) | jnp.uint32(0x80000000)
    return lax.bitcast_convert_type(bits, F32)


def _sb_attn_kernel(q_ref, k_ref, vt_ref, o_ref, z0_ref, z1_ref, hl0_ref, hl1_ref, a0_ref, a1_ref,
                    acc_ref, c_ref, *, tq, tk):
    i = pl.program_id(2)
    q = q_ref[...]
    acc_ref[...] = jnp.zeros(acc_ref.shape, F32)
    c_ref[...] = jnp.zeros(c_ref.shape, F32)
    logits = (z0_ref, z1_ref)
    splits = (hl0_ref, hl1_ref)
    args = (a0_ref, a1_ref)
    col = lax.broadcasted_iota(jnp.int32, (tk, 2 * tk), 1)
    row = lax.broadcasted_iota(jnp.int32, (tk, 2 * tk), 0)
    neg_tri2 = jnp.where((col % tk) >= row, -1.0, 0.0).astype(BF16)

    def key_start(n):
        return pl.multiple_of((2 * i + 1 - n) * tk, tk)

    def a_issue(n, slot, masked):
        return _dot_nt(k_ref[pl.ds(key_start(n), tk), :], q)

    def a_finish(n, slot, z2, masked):
        sp = jnp.maximum(z2, 0.0) + jnp.log2(1.0 + jnp.exp2(_neg_abs(z2)))
        if masked:
            kpos = key_start(n) + lax.broadcasted_iota(jnp.int32, z2.shape, 0)
            qpos = i * tq + lax.broadcasted_iota(jnp.int32, z2.shape, 1)
            mask = kpos < qpos
            sp = jnp.where(mask, sp, 0.0)
            z2 = jnp.where(mask, z2, -jnp.inf)
        hi, lo = _split_bf16(sp)
        splits[slot][:tk, :] = hi
        splits[slot][tk:, :] = lo
        logits[slot][...] = z2

    def b_issue(n, slot):
        return _dot(neg_tri2, splits[slot][...])

    def b_finish(n, slot, later):
        args[slot][...] = logits[slot][...] + later + c_ref[...]
        c_ref[...] += later[0:1, :]

    def c_issue(n, slot):
        a = jnp.exp2(args[slot][...]).astype(BF16)
        return _dot(vt_ref[:, pl.ds(key_start(n), tk)], a)

    def c_finish(n, slot, pv):
        acc_ref[...] += pv

    _three_stage(i, (a_issue, a_finish), (b_issue, b_finish), (c_issue, c_finish))
    o_ref[...] = acc_ref[...].T


def _sb_attn(qk, vt, batch, seq, tq=512):
    tk = tq // 2
    nq = seq // tq
    kern = functools.partial(_sb_attn_kernel, tq=tq, tk=tk)
    return pl.pallas_call(
        kern,
        grid=(batch, SB_HEADS, nq),
        in_specs=[
            pl.BlockSpec((tq, SB_DIM), lambda b, h, i: (b * nq + i, h)),
            pl.BlockSpec((seq, SB_DIM), lambda b, h, i: (b, SB_HEADS + h)),
            pl.BlockSpec((SB_DIM, seq), lambda b, h, i: (h, b)),
        ],
        out_specs=pl.BlockSpec((tq, SB_DIM), lambda b, h, i: (b * nq + i, h)),
        out_shape=jax.ShapeDtypeStruct((batch * seq, SB_OUT), F32),
        scratch_shapes=[
            pltpu.VMEM((tk, tq), F32),
            pltpu.VMEM((tk, tq), F32),
            pltpu.VMEM((2 * tk, tq), BF16),
            pltpu.VMEM((2 * tk, tq), BF16),
            pltpu.VMEM((tk, tq), F32),
            pltpu.VMEM((tk, tq), F32),
            pltpu.VMEM((SB_DIM, tq), F32),
            pltpu.VMEM((1, tq), F32),
        ],
        compiler_params=_params("parallel", "parallel", "arbitrary"),
        name="sb_attn",
    )(qk, qk, vt)


def _out_proj_kernel(x_ref, oa_ref, ob_ref, ga_ref, gb_ref, w_ref, y_ref):
    na = _rms(oa_ref[...], ga_ref[...]).astype(BF16)
    nb = _rms(ob_ref[...], gb_ref[...]).astype(BF16)
    y_ref[...] = x_ref[...] + _dot(na, w_ref[:MLA_OUT, :]) + _dot(nb, w_ref[MLA_OUT:, :])


def _out_proj(x, oa, ob, ga, gb, w, tm=512):
    t = x.shape[0]
    full = lambda a: pl.BlockSpec(a.shape, lambda i: (0, 0))
    return pl.pallas_call(
        _out_proj_kernel,
        grid=(t // tm,),
        in_specs=[
            pl.BlockSpec((tm, D_MODEL), lambda i: (i, 0)),
            pl.BlockSpec((tm, MLA_OUT), lambda i: (i, 0)),
            pl.BlockSpec((tm, SB_OUT), lambda i: (i, 0)),
            full(ga), full(gb), full(w),
        ],
        out_specs=pl.BlockSpec((tm, D_MODEL), lambda i: (i, 0)),
        out_shape=jax.ShapeDtypeStruct((t, D_MODEL), F32),
        compiler_params=_params("parallel"),
        name="out_proj",
    )(x, oa, ob, ga, gb, w)


def _mem_kv_kernel(mem_ref, g_ref, w_ref, kv_ref):
    kv_ref[...] = _dot(_rms(mem_ref[...], g_ref[...]).astype(BF16), w_ref[...]).astype(BF16)


def _mem_kv(mem, g, w):
    m = mem.shape[0]
    n = w.shape[1]
    full = lambda a: pl.BlockSpec(a.shape, lambda i: (0, 0))
    return pl.pallas_call(
        _mem_kv_kernel,
        grid=(1,),
        in_specs=[full(mem), full(g), full(w)],
        out_specs=pl.BlockSpec((m, n), lambda i: (0, 0)),
        out_shape=jax.ShapeDtypeStruct((m, n), BF16),
        compiler_params=_params("arbitrary"),
        name="mem_kv",
    )(mem, g, w)


GROUP_LANE0 = N_EXPERTS


def _mem_route_kernel(x_ref, gx_ref, wq_ref, kv_ref, wo_ref, gf_ref, wrh_ref, wrl_ref, br_ref,
                      x2_ref, h2_ref, route_ref):
    x1 = x_ref[...]
    q = _dot(_rms(x1, gx_ref[...]).astype(BF16), wq_ref[...]).astype(BF16)
    scale = MEM_DIM ** -0.5
    kw = MEM_HEADS * MEM_DIM
    heads = []
    for h in range(MEM_HEADS):
        lo = h * MEM_DIM
        s = _dot_nt(q[:, lo:lo + MEM_DIM], kv_ref[:, lo:lo + MEM_DIM]) * scale
        e = jnp.exp(s - jnp.max(s, axis=-1, keepdims=True))
        p = (e / jnp.sum(e, axis=-1, keepdims=True)).astype(BF16)
        heads.append(_dot(p, kv_ref[:, kw + lo:kw + lo + MEM_DIM]).astype(BF16))
    o = jnp.concatenate(heads, axis=-1)
    x2 = x1 + _dot(o, wo_ref[...])
    x2_ref[...] = x2
    h2 = _rms(x2, gf_ref[...])
    h2_ref[...] = h2

    hh, hl = _split_bf16(h2)
    lg = _dot(hh, wrh_ref[...]) + _dot(hl, wrh_ref[...]) + _dot(hh, wrl_ref[...]) + br_ref[...]
    lane = lax.broadcasted_iota(jnp.int32, lg.shape, 1)
    big = jnp.int32(1 << 20)
    ninf = -jnp.inf

    def lane_max(v):
        return jnp.max(v, axis=-1, keepdims=True)

    def first_lane(cond):
        return jnp.min(jnp.where(cond, lane, big), axis=-1, keepdims=True)

    is_g = (lane >= GROUP_LANE0) & (lane < GROUP_LANE0 + N_GROUPS)
    g_max = lane_max(jnp.where(is_g, lg, ninf))
    g_sum = jnp.sum(jnp.where(is_g, jnp.exp(lg - g_max), 0.0), axis=-1, keepdims=True)
    p_g = 1.0 / g_sum
    g_idx = first_lane(is_g & (lg == g_max)) - GROUP_LANE0
    in_grp = (lane < N_EXPERTS) & ((lane // EXPERTS_PER_GROUP) == g_idx)
    e_max = lane_max(jnp.where(in_grp, lg, ninf))
    e_sum = jnp.sum(jnp.where(in_grp, jnp.exp(lg - e_max), 0.0), axis=-1, keepdims=True)
    i1 = first_lane(in_grp & (lg == e_max))
    rest = in_grp & (lane != i1)
    e_max2 = lane_max(jnp.where(rest, lg, ninf))
    i2 = first_lane(rest & (lg == e_max2))
    p1 = 1.0 / e_sum
    p2 = jnp.exp(e_max2 - e_max) / e_sum
    den = p1 + p2
    gate1 = p_g * (p1 / den)
    gate2 = p_g * (p2 / den)
    route = jnp.where(lane == 0, i1.astype(F32),
                      jnp.where(lane == 1, i2.astype(F32),
                                jnp.where(lane == 2, gate1,
                                          jnp.where(lane == 3, gate2, 0.0))))
    route_ref[...] = route


def _mem_route(x1, gx, wq, kv, wo, gf, wrh, wrl, br, seq, tm=512):
    t = x1.shape[0]
    per_batch = seq // tm
    full = lambda a: pl.BlockSpec(a.shape, lambda i: (0, 0))
    return pl.pallas_call(
        _mem_route_kernel,
        grid=(t // tm,),
        in_specs=[
            pl.BlockSpec((tm, D_MODEL), lambda i: (i, 0)),
            full(gx), full(wq),
            pl.BlockSpec((MEM_LEN, kv.shape[1]), lambda i: (i // per_batch, 0)),
            full(wo), full(gf), full(wrh), full(wrl), full(br),
        ],
        out_specs=[
            pl.BlockSpec((tm, D_MODEL), lambda i: (i, 0)),
            pl.BlockSpec((tm, D_MODEL), lambda i: (i, 0)),
            pl.BlockSpec((tm, LANES), lambda i: (i, 0)),
        ],
        out_shape=[
            jax.ShapeDtypeStruct((t, D_MODEL), F32),
            jax.ShapeDtypeStruct((t, D_MODEL), F32),
            jax.ShapeDtypeStruct((t, LANES), F32),
        ],
        compiler_params=_params("parallel"),
        name="mem_route",
    )(x1, gx, wq, kv, wo, gf, wrh, wrl, br)


SLOT_BLK = 256


def _slots_kernel(route_ref, slot_ref, meta_ref, cum_ref, *, n_tok):
    nblk = n_tok // SLOT_BLK
    e_iota = lax.broadcasted_iota(jnp.int32, (LANES, SLOT_BLK), 0)
    incl = (lax.broadcasted_iota(jnp.int32, (SLOT_BLK, SLOT_BLK), 0)
            <= lax.broadcasted_iota(jnp.int32, (SLOT_BLK, SLOT_BLK), 1)).astype(BF16)

    def onehots(b):
        ts = pl.multiple_of(b * SLOT_BLK, SLOT_BLK)
        ids = route_ref[pl.ds(ts, SLOT_BLK), :].T
        oh1 = e_iota == ids[0:1, :].astype(jnp.int32)
        oh2 = e_iota == ids[1:2, :].astype(jnp.int32)
        return ts, oh1, oh2

    def count(b, carry):
        ts, oh1, oh2 = onehots(b)
        cnt = (oh1 | oh2).astype(F32).astype(BF16)
        c = _dot(cnt, incl) + carry
        cum_ref[:, pl.ds(ts, SLOT_BLK)] = c
        return c[:, SLOT_BLK - 1:SLOT_BLK]

    total = lax.fori_loop(0, nblk, count, jnp.zeros((LANES, 1), F32))
    tiles = jnp.floor((total + (SLOT_TILE - 1)) * (1.0 / SLOT_TILE))
    below = (lax.broadcasted_iota(jnp.int32, (LANES, LANES), 1)
             < lax.broadcasted_iota(jnp.int32, (LANES, LANES), 0)).astype(BF16)
    tile_lo = _dot(below, jnp.broadcast_to(tiles, (LANES, LANES)).astype(BF16))
    base = tile_lo[:, 0:1] * SLOT_TILE

    def assign(b, carry):
        ts, oh1, oh2 = onehots(b)
        pos = base + cum_ref[:, pl.ds(ts, SLOT_BLK)] - 1.0
        s1 = jnp.sum(jnp.where(oh1, pos, 0.0), axis=0, keepdims=True)
        s2 = jnp.sum(jnp.where(oh2, pos, 0.0), axis=0, keepdims=True)
        row = lax.broadcasted_iota(jnp.int32, (8, SLOT_BLK), 0)
        slot_ref[:, pl.ds(ts, SLOT_BLK)] = jnp.where(row == 0, s1, jnp.where(row == 1, s2, 0.0)).astype(jnp.int32)
        return carry

    lax.fori_loop(0, nblk, assign, 0)

    tile_hi = tile_lo + tiles
    tix = lax.broadcasted_iota(jnp.int32, (LANES, LANES), 1).astype(F32)
    is_e = lax.broadcasted_iota(jnp.int32, (LANES, LANES), 0) < N_EXPERTS
    owner = jnp.sum(jnp.where(is_e & (tile_hi <= tix), 1.0, 0.0), axis=0, keepdims=True)
    n_tiles = jnp.max(jnp.where(is_e, tile_hi, 0.0), axis=0, keepdims=True)
    valid = tix[0:1, :] < n_tiles
    last_owner = jnp.max(jnp.where(valid, owner, 0.0), axis=1, keepdims=True)
    owner = jnp.where(valid, owner, last_owner)
    row = lax.broadcasted_iota(jnp.int32, (8, LANES), 0)
    meta = jnp.where(row == 0, owner, jnp.where(row == 1, valid.astype(F32), 0.0))
    meta_ref[...] = meta.astype(jnp.int32)


def _slots(route):
    t = route.shape[0]
    kern = functools.partial(_slots_kernel, n_tok=t)
    return pl.pallas_call(
        kern,
        grid=(1,),
        in_specs=[pl.BlockSpec(route.shape, lambda i: (0, 0))],
        out_specs=[
            pl.BlockSpec((8, t), lambda i: (0, 0)),
            pl.BlockSpec((8, LANES), lambda i: (0, 0)),
        ],
        out_shape=[
            jax.ShapeDtypeStruct((8, t), jnp.int32),
            jax.ShapeDtypeStruct((8, LANES), jnp.int32),
        ],
        scratch_shapes=[pltpu.VMEM((LANES, t), F32)],
        compiler_params=_params("arbitrary"),
        name="slots",
    )(route)


def _row_copy(src_ref, src_row, dst_ref, dst_row, sem):
    return pltpu.make_async_copy(src_ref.at[pl.ds(src_row, 1), :], dst_ref.at[pl.ds(dst_row, 1), :], sem)


def _invert_kernel(s1_ref, s2_ref, tok_ref, *, n_tok, n_slots):
    def clear(j, c):
        tok_ref[j] = 0
        return c

    lax.fori_loop(0, n_slots, clear, 0, unroll=8)

    def put(t, c):
        tok_ref[s1_ref[t]] = t
        tok_ref[s2_ref[t]] = t
        return c

    lax.fori_loop(0, n_tok, put, 0, unroll=4)


def _invert(slot1, slot2, n_slots):
    n_tok = slot1.shape[0]
    kern = functools.partial(_invert_kernel, n_tok=n_tok, n_slots=n_slots)
    smem = pl.BlockSpec(memory_space=pltpu.SMEM)
    return pl.pallas_call(
        kern,
        in_specs=[smem, smem],
        out_specs=smem,
        out_shape=jax.ShapeDtypeStruct((n_slots,), jnp.int32),
        name="invert",
    )(slot1, slot2)


def _experts_kernel(own_ref, valid_ref, tok_ref, h_ref, wg_hbm, wu_hbm, wd_hbm, ys_ref,
                    xbuf, wgf, wuf, wdf, wgb, wub, wdb, xsem, wsem, wslot_ref):
    i = pl.program_id(0)
    nt = pl.num_programs(0)
    at = lambda ref, j: ref[jnp.minimum(j, nt - 1)]

    def weight_copies(e, s):
        return (pltpu.make_async_copy(wg_hbm.at[e], wgf.at[s], wsem.at[s]),
                pltpu.make_async_copy(wu_hbm.at[e], wuf.at[s], wsem.at[s]),
                pltpu.make_async_copy(wd_hbm.at[e], wdf.at[s], wsem.at[s]))

    def gather_start(tile, s):
        base = tile * SLOT_TILE

        def issue(r, c):
            _row_copy(h_ref, tok_ref[base + r], xbuf.at[s], r, xsem.at[s]).start()
            return c

        lax.fori_loop(0, SLOT_TILE, issue, 0, unroll=8)

    def gather_wait(s):
        def drain(r, c):
            _row_copy(h_ref, 0, xbuf.at[s], r, xsem.at[s]).wait()
            return c

        lax.fori_loop(0, SLOT_TILE, drain, 0, unroll=8)

    @pl.when(i == 0)
    def _():
        wslot_ref[0] = 0
        for cp in weight_copies(own_ref[0], 0):
            cp.start()
        gather_start(0, 0)

    xs = i % 2

    @pl.when(valid_ref[i] > 0)
    def _():
        e = own_ref[i]

        @pl.when((i == 0) | (own_ref[jnp.maximum(i - 1, 0)] != e))
        def _():
            s = wslot_ref[0]
            for cp in weight_copies(e, s):
                cp.wait()
            j = lax.while_loop(lambda j: (j < nt) & (at(own_ref, j) == e), lambda j: j + 1, i + 1)

            @pl.when((j < nt) & (at(valid_ref, j) > 0))
            def _():
                for cp in weight_copies(at(own_ref, j), 1 - s):
                    cp.start()

            wgb[...] = wgf[s].astype(BF16)
            wub[...] = wuf[s].astype(BF16)
            wdb[...] = wdf[s].astype(BF16)
            wslot_ref[0] = 1 - s

        @pl.when((i + 1 < nt) & (at(valid_ref, i + 1) > 0))
        def _():
            gather_start(i + 1, 1 - xs)

        gather_wait(xs)
        x = xbuf[xs].astype(BF16)
        a = _dot(x, wgb[...])
        u = _dot(x, wub[...])
        act = (a * jax.nn.sigmoid(a) * u).astype(BF16)
        ys_ref[...] = _dot(act, wdb[...])

    @pl.when(valid_ref[i] == 0)
    def _():
        ys_ref[...] = jnp.zeros(ys_ref.shape, F32)


def _experts(own, valid, tok, h2, wg, wu, wd):
    n_tiles = tok.shape[0] // SLOT_TILE
    hbm = pl.BlockSpec(memory_space=pl.ANY)
    return pl.pallas_call(
        _experts_kernel,
        grid_spec=pltpu.PrefetchScalarGridSpec(
            num_scalar_prefetch=3,
            grid=(n_tiles,),
            in_specs=[hbm, hbm, hbm, hbm],
            out_specs=pl.BlockSpec((SLOT_TILE, D_MODEL), lambda i, o, v, t: (i, 0)),
            scratch_shapes=[
                pltpu.VMEM((2, SLOT_TILE, D_MODEL), F32),
                pltpu.VMEM((2, D_MODEL, D_EXPERT), F32),
                pltpu.VMEM((2, D_MODEL, D_EXPERT), F32),
                pltpu.VMEM((2, D_EXPERT, D_MODEL), F32),
                pltpu.VMEM((D_MODEL, D_EXPERT), BF16),
                pltpu.VMEM((D_MODEL, D_EXPERT), BF16),
                pltpu.VMEM((D_EXPERT, D_MODEL), BF16),
                pltpu.SemaphoreType.DMA((2,)),
                pltpu.SemaphoreType.DMA((2,)),
                pltpu.SMEM((1,), jnp.int32),
            ],
        ),
        out_shape=jax.ShapeDtypeStruct((tok.shape[0], D_MODEL), F32),
        compiler_params=_params("arbitrary"),
        name="experts",
    )(own, valid, tok, h2, wg, wu, wd)


def _combine_kernel(s1_ref, s2_ref, x_ref, route_ref, g_ref, ys_ref, y_ref, buf_ref, sem, *, tm):
    t0 = pl.program_id(0) * tm

    def issue(r, c):
        _row_copy(ys_ref, s1_ref[t0 + r], buf_ref.at[0], r, sem).start()
        _row_copy(ys_ref, s2_ref[t0 + r], buf_ref.at[1], r, sem).start()
        return c

    lax.fori_loop(0, tm, issue, 0)

    def drain(r, c):
        _row_copy(ys_ref, s1_ref[t0 + r], buf_ref.at[0], r, sem).wait()
        _row_copy(ys_ref, s2_ref[t0 + r], buf_ref.at[1], r, sem).wait()
        return c

    lax.fori_loop(0, tm, drain, 0)
    y = x_ref[...] + route_ref[:, 2:3] * buf_ref[0] + route_ref[:, 3:4] * buf_ref[1]
    y_ref[...] = _rms(y, g_ref[...])


def _combine(slot1, slot2, x2, route, g, ys, tm=256):
    t = x2.shape[0]
    kern = functools.partial(_combine_kernel, tm=tm)
    return pl.pallas_call(
        kern,
        grid_spec=pltpu.PrefetchScalarGridSpec(
            num_scalar_prefetch=2,
            grid=(t // tm,),
            in_specs=[
                pl.BlockSpec((tm, D_MODEL), lambda i, s1, s2: (i, 0)),
                pl.BlockSpec((tm, LANES), lambda i, s1, s2: (i, 0)),
                pl.BlockSpec((1, D_MODEL), lambda i, s1, s2: (0, 0)),
                pl.BlockSpec(memory_space=pl.ANY),
            ],
            out_specs=pl.BlockSpec((tm, D_MODEL), lambda i, s1, s2: (i, 0)),
            scratch_shapes=[
                pltpu.VMEM((2, tm, D_MODEL), F32),
                pltpu.SemaphoreType.DMA(()),
            ],
        ),
        out_shape=jax.ShapeDtypeStruct((t, D_MODEL), F32),
        compiler_params=_params("arbitrary"),
        name="combine",
    )(slot1, slot2, x2, route, g, ys)


def _swap_halves(w):
    half = w.shape[-1] // 2
    return jnp.concatenate([-w[..., half:], w[..., :half]], axis=-1)


def _row(v):
    return v.reshape(1, -1).astype(F32)


def kernel(x, mem, positions, norm_mix, w_in, norm_q_lat, w_q_b, norm_kv_lat, w_kv_b, norm_mla_out, norm_sb_out, w_out, norm_mem_x, norm_mem_src, w_mem_q, w_mem_kv, w_mem_o, norm_ffn, w_group, b_group, w_expert_router, b_expert, w_gate, w_up, w_down, norm_final):
    batch, seq, d = x.shape
    t = batch * seq
    depth = w_in.shape[0]
    assert depth == 1, "single-layer trunk only"
    xt = x.reshape(t, d)
    pos = positions.reshape(t, 1)
    inv_freq = ROPE_THETA ** (-jnp.arange(0, MLA_ROPE, 2, dtype=F32) / MLA_ROPE)
    invf = jnp.concatenate([inv_freq, inv_freq, jnp.zeros((LANES - MLA_ROPE,), F32)]).reshape(1, LANES)
    n_slots = (2 * t // SLOT_TILE + N_EXPERTS) * SLOT_TILE

    for l in range(depth):
        wi = w_in[l]
        lat_w = MLA_Q_RANK + MLA_KV_RANK
        w_kpe = wi[:, lat_w:lat_w + MLA_ROPE]
        zpad = jnp.zeros((d, LANES - MLA_ROPE), F32)
        sb0 = lat_w + MLA_ROPE
        w_sbq = wi[:, sb0:sb0 + SB_OUT] * (SB_DIM ** -0.5 * LOG2E)
        w_in_pad = jnp.concatenate(
            [wi[:, :sb0], zpad, _swap_halves(w_kpe), zpad, w_sbq, wi[:, sb0 + SB_OUT:sb0 + 2 * SB_OUT]],
            axis=1).astype(BF16)
        w_sbv_t = wi[:, sb0 + 2 * SB_OUT:].T.astype(BF16)
        wq = w_q_b[l].reshape(MLA_Q_RANK, MLA_HEADS, MLA_QK) * (MLA_QK ** -0.5 * LOG2E)
        wq_pe = wq[:, :, MLA_NOPE:]
        zq = jnp.zeros((MLA_Q_RANK, MLA_HEADS, MLA_PAD - MLA_QK), F32)
        wqa = jnp.concatenate([wq, zq], axis=-1).reshape(MLA_Q_RANK, MLA_HEADS * MLA_PAD).astype(BF16)
        wqb = jnp.concatenate([_swap_halves(wq_pe), zq], axis=-1).reshape(MLA_Q_RANK, MLA_HEADS * LANES).astype(BF16)
        wkv = w_kv_b[l].reshape(MLA_KV_RANK, MLA_HEADS, MLA_NOPE + MLA_V)
        wk = wkv[:, :, :MLA_NOPE].reshape(MLA_KV_RANK, MLA_OUT).astype(BF16)
        wvt = wkv[:, :, MLA_NOPE:].reshape(MLA_KV_RANK, MLA_OUT).T.astype(BF16)
        w_router = jnp.concatenate(
            [w_expert_router[l], w_group[l], jnp.zeros((d, LANES - N_EXPERTS - N_GROUPS), F32)], axis=1)
        wr_hi = w_router.astype(BF16)
        wr_lo = (w_router - wr_hi.astype(F32)).astype(BF16)
        b_router = jnp.concatenate(
            [b_expert[l].astype(F32), b_group[l].astype(F32), jnp.zeros((LANES - N_EXPERTS - N_GROUPS,), F32)]
        ).reshape(1, LANES)

        lat, sb_qk, sb_vt = _proj_in(xt, _row(norm_mix[l]), w_in_pad, w_sbv_t)
        q, k, vt = _mla_proj(lat, pos, invf, _row(norm_q_lat[l]), _row(norm_kv_lat[l]), wqa, wqb, wk, wvt)
        o_mla = _mla_attn(q, k, vt, batch, seq)
        o_sb = _sb_attn(sb_qk, sb_vt, batch, seq)
        x1 = _out_proj(xt, o_mla, o_sb, _row(norm_mla_out[l]), _row(norm_sb_out[l]), w_out[l].astype(BF16))

        kv = _mem_kv(mem.reshape(batch * MEM_LEN, d), _row(norm_mem_src[l]), w_mem_kv[l].astype(BF16))
        x2, h2, route = _mem_route(x1, _row(norm_mem_x[l]), w_mem_q[l].astype(BF16), kv,
                                   w_mem_o[l].astype(BF16), _row(norm_ffn[l]), wr_hi, wr_lo, b_router, seq)

        slots, meta = _slots(route)
        tok = _invert(slots[0], slots[1], n_slots)
        ys = _experts(meta[0], meta[1], tok, h2, w_gate[l], w_up[l], w_down[l])
        xt = _combine(slots[0], slots[1], x2, route, _row(norm_final), ys)
    return xt.reshape(batch, seq, d)
```

```python
import functools

import jax
import jax.numpy as jnp
from jax import lax
from jax.experimental import pallas as pl
from jax.experimental.pallas import tpu as pltpu

F32 = jnp.float32
BF16 = jnp.bfloat16

EPS = 1e-6
ROPE_THETA = 10000.0

D_MODEL = 2048
MEM_LEN = 256
MLA_HEADS = 8
MLA_NOPE = 128
MLA_ROPE = 64
MLA_QK = MLA_NOPE + MLA_ROPE
MLA_V = 128
MLA_Q_RANK = 512
MLA_KV_RANK = 256
MLA_PAD = 256
SB_HEADS = 8
SB_DIM = 128
MLA_OUT = MLA_HEADS * MLA_V
SB_OUT = SB_HEADS * SB_DIM
MEM_HEADS = 4
MEM_DIM = 128
N_GROUPS = 4
EXPERTS_PER_GROUP = 8
N_EXPERTS = N_GROUPS * EXPERTS_PER_GROUP
D_EXPERT = 512

LANES = 128
LAT_COLS = 1024
IN_COLS_PAD = LAT_COLS + 3 * SB_OUT

SLOT_TILE = 256
VMEM_LIMIT = 56 * 1024 * 1024


def _rms(x, g):
    return x * lax.rsqrt(jnp.mean(x * x, axis=-1, keepdims=True) + EPS) * g


def _dot(a, b):
    return jnp.dot(a, b, preferred_element_type=F32)


def _dot_nt(a, b):
    return lax.dot_general(a, b, (((1,), (1,)), ((), ())), preferred_element_type=F32)


def _split_bf16(x):
    hi = x.astype(BF16)
    lo = (x - hi.astype(F32)).astype(BF16)
    return hi, lo


def _params(*sem):
    return pltpu.CompilerParams(dimension_semantics=sem, vmem_limit_bytes=VMEM_LIMIT)


def _proj_in_kernel(x_ref, g_ref, w_ref, wvt_ref, lat_ref, qk_ref, vt_ref, h_ref):
    j = pl.program_id(1)
    last = pl.num_programs(1) - 1

    @pl.when(j == 0)
    def _():
        h_ref[...] = _rms(x_ref[...], g_ref[...]).astype(BF16)
        lat_ref[...] = _dot(h_ref[...], w_ref[...])

    @pl.when((j > 0) & (j < last))
    def _():
        qk_ref[...] = _dot(h_ref[...], w_ref[...]).astype(BF16)

    @pl.when(j == last)
    def _():
        vt_ref[...] = _dot_nt(wvt_ref[...], h_ref[...]).astype(BF16)


def _proj_in(x, g, w_pad, wvt, tm=512):
    t = x.shape[0]
    tn = LAT_COLS
    n_w = w_pad.shape[1] // tn
    return pl.pallas_call(
        _proj_in_kernel,
        grid=(t // tm, n_w + 1),
        in_specs=[
            pl.BlockSpec((tm, D_MODEL), lambda i, j: (i, 0)),
            pl.BlockSpec((1, D_MODEL), lambda i, j: (0, 0)),
            pl.BlockSpec((D_MODEL, tn), lambda i, j: (0, jnp.minimum(j, n_w - 1))),
            pl.BlockSpec(wvt.shape, lambda i, j: (0, 0)),
        ],
        out_specs=[
            pl.BlockSpec((tm, tn), lambda i, j: (i, 0)),
            pl.BlockSpec((tm, tn), lambda i, j: (i, jnp.clip(j - 1, 0, n_w - 2))),
            pl.BlockSpec((SB_OUT, tm), lambda i, j: (0, i)),
        ],
        out_shape=[
            jax.ShapeDtypeStruct((t, LAT_COLS), F32),
            jax.ShapeDtypeStruct((t, 2 * SB_OUT), BF16),
            jax.ShapeDtypeStruct((SB_OUT, t), BF16),
        ],
        scratch_shapes=[pltpu.VMEM((tm, D_MODEL), BF16)],
        compiler_params=_params("parallel", "arbitrary"),
        name="proj_in",
    )(x, g, w_pad, wvt)


def _mla_proj_kernel(lat_ref, pos_ref, invf_ref, gq_ref, gkv_ref, wqa_ref, wqb_ref, wk_ref, wvt_ref,
                     q_ref, k_ref, vt_ref):
    cq = _rms(lat_ref[:, :MLA_Q_RANK], gq_ref[...]).astype(BF16)
    ckv = _rms(lat_ref[:, MLA_Q_RANK:MLA_Q_RANK + MLA_KV_RANK], gkv_ref[...]).astype(BF16)
    ang = pos_ref[...].astype(F32) * invf_ref[...]
    live = lax.broadcasted_iota(jnp.int32, ang.shape, 1) < MLA_ROPE
    cos2 = jnp.where(live, jnp.cos(ang), 0.0)
    sin2 = jnp.where(live, jnp.sin(ang), 0.0)

    qa = _dot(cq, wqa_ref[...])
    qb = _dot(cq, wqb_ref[...])
    kn = _dot(ckv, wk_ref[...])
    vt_ref[...] = _dot_nt(wvt_ref[...], ckv).astype(BF16)
    k_pe = (lat_ref[:, 768:896] * cos2 + lat_ref[:, 896:1024] * sin2).astype(BF16)
    for h in range(MLA_HEADS):
        lo = h * MLA_PAD
        mid = lo + LANES
        q_ref[:, lo:mid] = qa[:, lo:mid].astype(BF16)
        q_ref[:, mid:mid + LANES] = (qa[:, mid:mid + LANES] * cos2
                                     + qb[:, h * LANES:(h + 1) * LANES] * sin2).astype(BF16)
        k_ref[:, lo:mid] = kn[:, h * LANES:(h + 1) * LANES].astype(BF16)
        k_ref[:, mid:mid + LANES] = k_pe


def _mla_proj(lat, pos, invf, gq, gkv, wqa, wqb, wk, wvt, tm=512):
    t = lat.shape[0]
    full = lambda a: pl.BlockSpec(a.shape, lambda i: (0, 0))
    return pl.pallas_call(
        _mla_proj_kernel,
        grid=(t // tm,),
        in_specs=[
            pl.BlockSpec((tm, LAT_COLS), lambda i: (i, 0)),
            pl.BlockSpec((tm, 1), lambda i: (i, 0)),
            full(invf), full(gq), full(gkv), full(wqa), full(wqb), full(wk), full(wvt),
        ],
        out_specs=[
            pl.BlockSpec((tm, MLA_HEADS * MLA_PAD), lambda i: (i, 0)),
            pl.BlockSpec((tm, MLA_HEADS * MLA_PAD), lambda i: (i, 0)),
            pl.BlockSpec((MLA_OUT, tm), lambda i: (0, i)),
        ],
        out_shape=[
            jax.ShapeDtypeStruct((t, MLA_HEADS * MLA_PAD), BF16),
            jax.ShapeDtypeStruct((t, MLA_HEADS * MLA_PAD), BF16),
            jax.ShapeDtypeStruct((MLA_OUT, t), BF16),
        ],
        compiler_params=_params("parallel"),
        name="mla_proj",
    )(lat, pos, invf, gq, gkv, wqa, wqb, wk, wvt)


LOG2E = 1.4426950408889634


def _three_stage(n_pairs, stage_a, stage_b, stage_c):
    a_issue, a_finish = stage_a
    b_issue, b_finish = stage_b
    c_issue, c_finish = stage_c

    def run_a(n, slot, first):
        a_finish(n, slot, a_issue(n, slot, first), first)

    run_a(0, 0, True)
    run_a(1, 1, True)
    b_finish(0, 0, b_issue(0, 0))

    def half(na, sa, nb, sb, nc, sc):
        ra = a_issue(na, sa, False)
        rb = b_issue(nb, sb)
        rc = c_issue(nc, sc)
        a_finish(na, sa, ra, False)
        b_finish(nb, sb, rb)
        c_finish(nc, sc, rc)

    def body(p, carry):
        half(2 * p, 0, 2 * p - 1, 1, 2 * p - 2, 0)
        half(2 * p + 1, 1, 2 * p, 0, 2 * p - 1, 1)
        return carry

    lax.fori_loop(1, n_pairs + 1, body, 0)
    last = 2 * n_pairs + 1
    rb = b_issue(last, 1)
    rc = c_issue(last - 1, 0)
    b_finish(last, 1, rb)
    c_finish(last - 1, 0, rc)
    c_finish(last, 1, c_issue(last, 1))


def _two_stage(n_pairs, produce, consume):
    produce(0, 0, True)
    produce(1, 1, True)
    consume(0, 0)

    def body(p, carry):
        produce(2 * p, 0, False)
        consume(2 * p - 1, 1)
        produce(2 * p + 1, 1, False)
        consume(2 * p, 0)
        return carry

    lax.fori_loop(1, n_pairs + 1, body, 0)
    consume(2 * n_pairs + 1, 1)


def _mla_attn_kernel(q_ref, k_ref, vt_ref, o_ref, s0_ref, s1_ref, m_ref, l_ref, acc_ref, *, tq, tk):
    i = pl.program_id(2)
    q = q_ref[...]
    m_ref[...] = jnp.full(m_ref.shape, -jnp.inf, F32)
    l_ref[...] = jnp.zeros(l_ref.shape, F32)
    acc_ref[...] = jnp.zeros(acc_ref.shape, F32)
    scores = (s0_ref, s1_ref)

    def key_start(n):
        tile = jnp.where(n < 2, 2 * i + n, 2 * i + 1 - n)
        return pl.multiple_of(tile * tk, tk)

    def produce(n, slot, diagonal):
        ks = key_start(n)
        st = _dot_nt(k_ref[pl.ds(ks, tk), :], q)
        if diagonal:
            kpos = ks + lax.broadcasted_iota(jnp.int32, st.shape, 0)
            qpos = i * tq + lax.broadcasted_iota(jnp.int32, st.shape, 1)
            st = jnp.where(kpos <= qpos, st, -jnp.inf)
        scores[slot][...] = st

    def consume(n, slot):
        ks = key_start(n)
        st = scores[slot][...]
        m_old = m_ref[...]
        m_new = jnp.maximum(m_old, jnp.max(st, axis=0, keepdims=True))
        alpha = jnp.exp2(m_old - m_new)
        p = jnp.exp2(st - m_new)
        l_ref[...] = alpha * l_ref[...] + jnp.sum(p, axis=0, keepdims=True)
        acc_ref[...] = alpha * acc_ref[...] + _dot(vt_ref[:, pl.ds(ks, tk)], p.astype(BF16))
        m_ref[...] = m_new

    _two_stage(i, produce, consume)
    o_ref[...] = (acc_ref[...] / l_ref[...]).T


def _mla_attn(q, k, vt, batch, seq, tq=512):
    tk = tq // 2
    nq = seq // tq
    kern = functools.partial(_mla_attn_kernel, tq=tq, tk=tk)
    return pl.pallas_call(
        kern,
        grid=(batch, MLA_HEADS, nq),
        in_specs=[
            pl.BlockSpec((tq, MLA_PAD), lambda b, h, i: (b * nq + i, h)),
            pl.BlockSpec((seq, MLA_PAD), lambda b, h, i: (b, h)),
            pl.BlockSpec((MLA_V, seq), lambda b, h, i: (h, b)),
        ],
        out_specs=pl.BlockSpec((tq, MLA_V), lambda b, h, i: (b * nq + i, h)),
        out_shape=jax.ShapeDtypeStruct((batch * seq, MLA_OUT), F32),
        scratch_shapes=[
            pltpu.VMEM((tk, tq), F32),
            pltpu.VMEM((tk, tq), F32),
            pltpu.VMEM((1, tq), F32),
            pltpu.VMEM((1, tq), F32),
            pltpu.VMEM((MLA_V, tq), F32),
        ],
        compiler_params=_params("parallel", "parallel", "arbitrary"),
        name="mla_attn",
    )(q, k, vt)


def _sb_attn_kernel(q_ref, k_ref, vt_ref, o_ref, z0_ref, z1_ref, hl0_ref, hl1_ref, a0_ref, a1_ref,
                    acc_ref, c_ref, *, tq, tk):
    i = pl.program_id(2)
    q = q_ref[...]
    acc_ref[...] = jnp.zeros(acc_ref.shape, F32)
    c_ref[...] = jnp.zeros(c_ref.shape, F32)
    logits = (z0_ref, z1_ref)
    splits = (hl0_ref, hl1_ref)
    args = (a0_ref, a1_ref)
    col = lax.broadcasted_iota(jnp.int32, (tk, 2 * tk), 1)
    row = lax.broadcasted_iota(jnp.int32, (tk, 2 * tk), 0)
    neg_tri2 = jnp.where((col % tk) >= row, -1.0, 0.0).astype(BF16)

    def key_start(n):
        return pl.multiple_of((2 * i + 1 - n) * tk, tk)

    def a_issue(n, slot, masked):
        return _dot_nt(k_ref[pl.ds(key_start(n), tk), :], q)

    def a_finish(n, slot, z2, masked):
        sp = jnp.maximum(z2, 0.0) + jnp.log2(1.0 + jnp.exp2(-jnp.abs(z2)))
        if masked:
            kpos = key_start(n) + lax.broadcasted_iota(jnp.int32, z2.shape, 0)
            qpos = i * tq + lax.broadcasted_iota(jnp.int32, z2.shape, 1)
            mask = kpos < qpos
            sp = jnp.where(mask, sp, 0.0)
            z2 = jnp.where(mask, z2, -jnp.inf)
        hi, lo = _split_bf16(sp)
        splits[slot][:tk, :] = hi
        splits[slot][tk:, :] = lo
        logits[slot][...] = z2

    def b_issue(n, slot):
        return _dot(neg_tri2, splits[slot][...])

    def b_finish(n, slot, later):
        args[slot][...] = logits[slot][...] + later + c_ref[...]
        c_ref[...] += later[0:1, :]

    def c_issue(n, slot):
        a = jnp.exp2(args[slot][...]).astype(BF16)
        return _dot(vt_ref[:, pl.ds(key_start(n), tk)], a)

    def c_finish(n, slot, pv):
        acc_ref[...] += pv

    _three_stage(i, (a_issue, a_finish), (b_issue, b_finish), (c_issue, c_finish))
    o_ref[...] = acc_ref[...].T


def _sb_attn(qk, vt, batch, seq, tq=512):
    tk = tq // 2
    nq = seq // tq
    kern = functools.partial(_sb_attn_kernel, tq=tq, tk=tk)
    return pl.pallas_call(
        kern,
        grid=(batch, SB_HEADS, nq),
        in_specs=[
            pl.BlockSpec((tq, SB_DIM), lambda b, h, i: (b * nq + i, h)),
            pl.BlockSpec((seq, SB_DIM), lambda b, h, i: (b, SB_HEADS + h)),
            pl.BlockSpec((SB_DIM, seq), lambda b, h, i: (h, b)),
        ],
        out_specs=pl.BlockSpec((tq, SB_DIM), lambda b, h, i: (b * nq + i, h)),
        out_shape=jax.ShapeDtypeStruct((batch * seq, SB_OUT), F32),
        scratch_shapes=[
            pltpu.VMEM((tk, tq), F32),
            pltpu.VMEM((tk, tq), F32),
            pltpu.VMEM((2 * tk, tq), BF16),
            pltpu.VMEM((2 * tk, tq), BF16),
            pltpu.VMEM((tk, tq), F32),
            pltpu.VMEM((tk, tq), F32),
            pltpu.VMEM((SB_DIM, tq), F32),
            pltpu.VMEM((1, tq), F32),
        ],
        compiler_params=_params("parallel", "parallel", "arbitrary"),
        name="sb_attn",
    )(qk, qk, vt)


def _out_proj_kernel(x_ref, oa_ref, ob_ref, ga_ref, gb_ref, w_ref, y_ref):
    na = _rms(oa_ref[...], ga_ref[...]).astype(BF16)
    nb = _rms(ob_ref[...], gb_ref[...]).astype(BF16)
    y_ref[...] = x_ref[...] + _dot(na, w_ref[:MLA_OUT, :]) + _dot(nb, w_ref[MLA_OUT:, :])


def _out_proj(x, oa, ob, ga, gb, w, tm=512):
    t = x.shape[0]
    full = lambda a: pl.BlockSpec(a.shape, lambda i: (0, 0))
    return pl.pallas_call(
        _out_proj_kernel,
        grid=(t // tm,),
        in_specs=[
            pl.BlockSpec((tm, D_MODEL), lambda i: (i, 0)),
            pl.BlockSpec((tm, MLA_OUT), lambda i: (i, 0)),
            pl.BlockSpec((tm, SB_OUT), lambda i: (i, 0)),
            full(ga), full(gb), full(w),
        ],
        out_specs=pl.BlockSpec((tm, D_MODEL), lambda i: (i, 0)),
        out_shape=jax.ShapeDtypeStruct((t, D_MODEL), F32),
        compiler_params=_params("parallel"),
        name="out_proj",
    )(x, oa, ob, ga, gb, w)


def _mem_kv_kernel(mem_ref, g_ref, w_ref, kv_ref):
    kv_ref[...] = _dot(_rms(mem_ref[...], g_ref[...]).astype(BF16), w_ref[...]).astype(BF16)


def _mem_kv(mem, g, w):
    m = mem.shape[0]
    n = w.shape[1]
    full = lambda a: pl.BlockSpec(a.shape, lambda i: (0, 0))
    return pl.pallas_call(
        _mem_kv_kernel,
        grid=(1,),
        in_specs=[full(mem), full(g), full(w)],
        out_specs=pl.BlockSpec((m, n), lambda i: (0, 0)),
        out_shape=jax.ShapeDtypeStruct((m, n), BF16),
        compiler_params=_params("arbitrary"),
        name="mem_kv",
    )(mem, g, w)


GROUP_LANE0 = N_EXPERTS


def _mem_route_kernel(x_ref, gx_ref, wq_ref, kv_ref, wo_ref, gf_ref, wrh_ref, wrl_ref, br_ref,
                      x2_ref, h2_ref, route_ref):
    x1 = x_ref[...]
    q = _dot(_rms(x1, gx_ref[...]).astype(BF16), wq_ref[...]).astype(BF16)
    scale = MEM_DIM ** -0.5
    kw = MEM_HEADS * MEM_DIM
    heads = []
    for h in range(MEM_HEADS):
        lo = h * MEM_DIM
        s = _dot_nt(q[:, lo:lo + MEM_DIM], kv_ref[:, lo:lo + MEM_DIM]) * scale
        e = jnp.exp(s - jnp.max(s, axis=-1, keepdims=True))
        p = (e / jnp.sum(e, axis=-1, keepdims=True)).astype(BF16)
        heads.append(_dot(p, kv_ref[:, kw + lo:kw + lo + MEM_DIM]).astype(BF16))
    o = jnp.concatenate(heads, axis=-1)
    x2 = x1 + _dot(o, wo_ref[...])
    x2_ref[...] = x2
    h2 = _rms(x2, gf_ref[...])
    h2_ref[...] = h2

    hh, hl = _split_bf16(h2)
    lg = _dot(hh, wrh_ref[...]) + _dot(hl, wrh_ref[...]) + _dot(hh, wrl_ref[...]) + br_ref[...]
    lane = lax.broadcasted_iota(jnp.int32, lg.shape, 1)
    big = jnp.int32(1 << 20)
    ninf = -jnp.inf

    def lane_max(v):
        return jnp.max(v, axis=-1, keepdims=True)

    def first_lane(cond):
        return jnp.min(jnp.where(cond, lane, big), axis=-1, keepdims=True)

    is_g = (lane >= GROUP_LANE0) & (lane < GROUP_LANE0 + N_GROUPS)
    g_max = lane_max(jnp.where(is_g, lg, ninf))
    g_sum = jnp.sum(jnp.where(is_g, jnp.exp(lg - g_max), 0.0), axis=-1, keepdims=True)
    p_g = 1.0 / g_sum
    g_idx = first_lane(is_g & (lg == g_max)) - GROUP_LANE0
    in_grp = (lane < N_EXPERTS) & ((lane // EXPERTS_PER_GROUP) == g_idx)
    e_max = lane_max(jnp.where(in_grp, lg, ninf))
    e_sum = jnp.sum(jnp.where(in_grp, jnp.exp(lg - e_max), 0.0), axis=-1, keepdims=True)
    i1 = first_lane(in_grp & (lg == e_max))
    rest = in_grp & (lane != i1)
    e_max2 = lane_max(jnp.where(rest, lg, ninf))
    i2 = first_lane(rest & (lg == e_max2))
    p1 = 1.0 / e_sum
    p2 = jnp.exp(e_max2 - e_max) / e_sum
    den = p1 + p2
    gate1 = p_g * (p1 / den)
    gate2 = p_g * (p2 / den)
    route = jnp.where(lane == 0, i1.astype(F32),
                      jnp.where(lane == 1, i2.astype(F32),
                                jnp.where(lane == 2, gate1,
                                          jnp.where(lane == 3, gate2, 0.0))))
    route_ref[...] = route


def _mem_route(x1, gx, wq, kv, wo, gf, wrh, wrl, br, seq, tm=512):
    t = x1.shape[0]
    per_batch = seq // tm
    full = lambda a: pl.BlockSpec(a.shape, lambda i: (0, 0))
    return pl.pallas_call(
        _mem_route_kernel,
        grid=(t // tm,),
        in_specs=[
            pl.BlockSpec((tm, D_MODEL), lambda i: (i, 0)),
            full(gx), full(wq),
            pl.BlockSpec((MEM_LEN, kv.shape[1]), lambda i: (i // per_batch, 0)),
            full(wo), full(gf), full(wrh), full(wrl), full(br),
        ],
        out_specs=[
            pl.BlockSpec((tm, D_MODEL), lambda i: (i, 0)),
            pl.BlockSpec((tm, D_MODEL), lambda i: (i, 0)),
            pl.BlockSpec((tm, LANES), lambda i: (i, 0)),
        ],
        out_shape=[
            jax.ShapeDtypeStruct((t, D_MODEL), F32),
            jax.ShapeDtypeStruct((t, D_MODEL), F32),
            jax.ShapeDtypeStruct((t, LANES), F32),
        ],
        compiler_params=_params("parallel"),
        name="mem_route",
    )(x1, gx, wq, kv, wo, gf, wrh, wrl, br)


SLOT_BLK = 256


def _slots_kernel(route_ref, slot_ref, meta_ref, cum_ref, *, n_tok):
    nblk = n_tok // SLOT_BLK
    e_iota = lax.broadcasted_iota(jnp.int32, (LANES, SLOT_BLK), 0)
    incl = (lax.broadcasted_iota(jnp.int32, (SLOT_BLK, SLOT_BLK), 0)
            <= lax.broadcasted_iota(jnp.int32, (SLOT_BLK, SLOT_BLK), 1)).astype(BF16)

    def onehots(b):
        ts = pl.multiple_of(b * SLOT_BLK, SLOT_BLK)
        ids = route_ref[pl.ds(ts, SLOT_BLK), :].T
        oh1 = e_iota == ids[0:1, :].astype(jnp.int32)
        oh2 = e_iota == ids[1:2, :].astype(jnp.int32)
        return ts, oh1, oh2

    def count(b, carry):
        ts, oh1, oh2 = onehots(b)
        cnt = (oh1 | oh2).astype(F32).astype(BF16)
        c = _dot(cnt, incl) + carry
        cum_ref[:, pl.ds(ts, SLOT_BLK)] = c
        return c[:, SLOT_BLK - 1:SLOT_BLK]

    total = lax.fori_loop(0, nblk, count, jnp.zeros((LANES, 1), F32))
    tiles = jnp.floor((total + (SLOT_TILE - 1)) * (1.0 / SLOT_TILE))
    below = (lax.broadcasted_iota(jnp.int32, (LANES, LANES), 1)
             < lax.broadcasted_iota(jnp.int32, (LANES, LANES), 0)).astype(BF16)
    tile_lo = _dot(below, jnp.broadcast_to(tiles, (LANES, LANES)).astype(BF16))
    base = tile_lo[:, 0:1] * SLOT_TILE

    def assign(b, carry):
        ts, oh1, oh2 = onehots(b)
        pos = base + cum_ref[:, pl.ds(ts, SLOT_BLK)] - 1.0
        s1 = jnp.sum(jnp.where(oh1, pos, 0.0), axis=0, keepdims=True)
        s2 = jnp.sum(jnp.where(oh2, pos, 0.0), axis=0, keepdims=True)
        row = lax.broadcasted_iota(jnp.int32, (8, SLOT_BLK), 0)
        slot_ref[:, pl.ds(ts, SLOT_BLK)] = jnp.where(row == 0, s1, jnp.where(row == 1, s2, 0.0)).astype(jnp.int32)
        return carry

    lax.fori_loop(0, nblk, assign, 0)

    tile_hi = tile_lo + tiles
    tix = lax.broadcasted_iota(jnp.int32, (LANES, LANES), 1).astype(F32)
    is_e = lax.broadcasted_iota(jnp.int32, (LANES, LANES), 0) < N_EXPERTS
    owner = jnp.sum(jnp.where(is_e & (tile_hi <= tix), 1.0, 0.0), axis=0, keepdims=True)
    n_tiles = jnp.max(jnp.where(is_e, tile_hi, 0.0), axis=0, keepdims=True)
    valid = tix[0:1, :] < n_tiles
    last_owner = jnp.max(jnp.where(valid, owner, 0.0), axis=1, keepdims=True)
    owner = jnp.where(valid, owner, last_owner)
    row = lax.broadcasted_iota(jnp.int32, (8, LANES), 0)
    meta = jnp.where(row == 0, owner, jnp.where(row == 1, valid.astype(F32), 0.0))
    meta_ref[...] = meta.astype(jnp.int32)


def _slots(route):
    t = route.shape[0]
    kern = functools.partial(_slots_kernel, n_tok=t)
    return pl.pallas_call(
        kern,
        grid=(1,),
        in_specs=[pl.BlockSpec(route.shape, lambda i: (0, 0))],
        out_specs=[
            pl.BlockSpec((8, t), lambda i: (0, 0)),
            pl.BlockSpec((8, LANES), lambda i: (0, 0)),
        ],
        out_shape=[
            jax.ShapeDtypeStruct((8, t), jnp.int32),
            jax.ShapeDtypeStruct((8, LANES), jnp.int32),
        ],
        scratch_shapes=[pltpu.VMEM((LANES, t), F32)],
        compiler_params=_params("arbitrary"),
        name="slots",
    )(route)


def _row_copy(src_ref, src_row, dst_ref, dst_row, sem):
    return pltpu.make_async_copy(src_ref.at[pl.ds(src_row, 1), :], dst_ref.at[pl.ds(dst_row, 1), :], sem)


def _invert_kernel(s1_ref, s2_ref, zeros_hbm, tok_ref, sem, *, n_tok):
    fill = pltpu.make_async_copy(zeros_hbm, tok_ref, sem)
    fill.start()
    fill.wait()

    def put(t, c):
        tok_ref[s1_ref[t]] = t
        tok_ref[s2_ref[t]] = t
        return c

    lax.fori_loop(0, n_tok, put, 0, unroll=8)


def _invert(slot1, slot2, n_slots):
    n_tok = slot1.shape[0]
    kern = functools.partial(_invert_kernel, n_tok=n_tok)
    smem = pl.BlockSpec(memory_space=pltpu.SMEM)
    return pl.pallas_call(
        kern,
        in_specs=[smem, smem, pl.BlockSpec(memory_space=pl.ANY)],
        out_specs=smem,
        out_shape=jax.ShapeDtypeStruct((n_slots,), jnp.int32),
        scratch_shapes=[pltpu.SemaphoreType.DMA(())],
        name="invert",
    )(slot1, slot2, jnp.zeros((n_slots,), jnp.int32))


def _experts_kernel(own_ref, valid_ref, tok_ref, h_ref, wg_hbm, wu_hbm, wd_hbm, ys_ref,
                    xbuf, wgf, wuf, wdf, wgb, wub, wdb, xsem, wsem, wslot_ref):
    i = pl.program_id(0)
    nt = pl.num_programs(0)
    at = lambda ref, j: ref[jnp.minimum(j, nt - 1)]

    def weight_copies(e, s):
        return (pltpu.make_async_copy(wg_hbm.at[e], wgf.at[s], wsem.at[s]),
                pltpu.make_async_copy(wu_hbm.at[e], wuf.at[s], wsem.at[s]),
                pltpu.make_async_copy(wd_hbm.at[e], wdf.at[s], wsem.at[s]))

    def gather_start(tile, s):
        base = tile * SLOT_TILE

        def issue(r, c):
            _row_copy(h_ref, tok_ref[base + r], xbuf.at[s], r, xsem.at[s]).start()
            return c

        lax.fori_loop(0, SLOT_TILE, issue, 0, unroll=8)

    def gather_wait(s):
        def drain(r, c):
            _row_copy(h_ref, 0, xbuf.at[s], r, xsem.at[s]).wait()
            return c

        lax.fori_loop(0, SLOT_TILE, drain, 0, unroll=8)

    @pl.when(i == 0)
    def _():
        wslot_ref[0] = 0
        for cp in weight_copies(own_ref[0], 0):
            cp.start(priority=1)
        gather_start(0, 0)

    xs = i % 2

    @pl.when(valid_ref[i] > 0)
    def _():
        e = own_ref[i]

        @pl.when((i == 0) | (own_ref[jnp.maximum(i - 1, 0)] != e))
        def _():
            s = wslot_ref[0]
            for cp in weight_copies(e, s):
                cp.wait()
            j = lax.while_loop(lambda j: (j < nt) & (at(own_ref, j) == e), lambda j: j + 1, i + 1)

            @pl.when((j < nt) & (at(valid_ref, j) > 0))
            def _():
                for cp in weight_copies(at(own_ref, j), 1 - s):
                    cp.start(priority=1)

            wgb[...] = wgf[s].astype(BF16)
            wub[...] = wuf[s].astype(BF16)
            wdb[...] = wdf[s].astype(BF16)
            wslot_ref[0] = 1 - s

        @pl.when((i + 1 < nt) & (at(valid_ref, i + 1) > 0))
        def _():
            gather_start(i + 1, 1 - xs)

        gather_wait(xs)
        x = xbuf[xs].astype(BF16)
        a = _dot(x, wgb[...])
        u = _dot(x, wub[...])
        act = (a * jax.nn.sigmoid(a) * u).astype(BF16)
        ys_ref[...] = _dot(act, wdb[...])

    @pl.when(valid_ref[i] == 0)
    def _():
        ys_ref[...] = jnp.zeros(ys_ref.shape, F32)


def _experts(own, valid, tok, h2, wg, wu, wd):
    n_tiles = tok.shape[0] // SLOT_TILE
    hbm = pl.BlockSpec(memory_space=pl.ANY)
    return pl.pallas_call(
        _experts_kernel,
        grid_spec=pltpu.PrefetchScalarGridSpec(
            num_scalar_prefetch=3,
            grid=(n_tiles,),
            in_specs=[hbm, hbm, hbm, hbm],
            out_specs=pl.BlockSpec((SLOT_TILE, D_MODEL), lambda i, o, v, t: (i, 0)),
            scratch_shapes=[
                pltpu.VMEM((2, SLOT_TILE, D_MODEL), F32),
                pltpu.VMEM((2, D_MODEL, D_EXPERT), F32),
                pltpu.VMEM((2, D_MODEL, D_EXPERT), F32),
                pltpu.VMEM((2, D_EXPERT, D_MODEL), F32),
                pltpu.VMEM((D_MODEL, D_EXPERT), BF16),
                pltpu.VMEM((D_MODEL, D_EXPERT), BF16),
                pltpu.VMEM((D_EXPERT, D_MODEL), BF16),
                pltpu.SemaphoreType.DMA((2,)),
                pltpu.SemaphoreType.DMA((2,)),
                pltpu.SMEM((1,), jnp.int32),
            ],
        ),
        out_shape=jax.ShapeDtypeStruct((tok.shape[0], D_MODEL), F32),
        compiler_params=_params("arbitrary"),
        name="experts",
    )(own, valid, tok, h2, wg, wu, wd)


def _combine_kernel(s1_ref, s2_ref, x_ref, route_ref, g_ref, ys_ref, y_ref, buf_ref, sem, *, tm):
    i = pl.program_id(0)
    n = pl.num_programs(0)

    def copies(tile, p, r):
        t = tile * tm + r
        return (_row_copy(ys_ref, s1_ref[t], buf_ref.at[p, 0], r, sem.at[p]),
                _row_copy(ys_ref, s2_ref[t], buf_ref.at[p, 1], r, sem.at[p]))

    def gather_start(tile, p):
        def issue(r, c):
            first, second = copies(tile, p, r)
            first.start(priority=0)
            second.start(priority=1)
            return c

        lax.fori_loop(0, tm, issue, 0, unroll=8)

    def gather_wait(tile, p):
        def drain(r, c):
            for cp in copies(tile, p, r):
                cp.wait()
            return c

        lax.fori_loop(0, tm, drain, 0, unroll=8)

    @pl.when(i == 0)
    def _():
        gather_start(0, 0)

    p = i % 2

    @pl.when(i + 1 < n)
    def _():
        gather_start(i + 1, 1 - p)

    gather_wait(i, p)
    y = x_ref[...] + route_ref[:, 2:3] * buf_ref[p, 0] + route_ref[:, 3:4] * buf_ref[p, 1]
    y_ref[...] = _rms(y, g_ref[...])


def _combine(slot1, slot2, x2, route, g, ys, tm=256):
    t = x2.shape[0]
    kern = functools.partial(_combine_kernel, tm=tm)
    return pl.pallas_call(
        kern,
        grid_spec=pltpu.PrefetchScalarGridSpec(
            num_scalar_prefetch=2,
            grid=(t // tm,),
            in_specs=[
                pl.BlockSpec((tm, D_MODEL), lambda i, s1, s2: (i, 0)),
                pl.BlockSpec((tm, LANES), lambda i, s1, s2: (i, 0)),
                pl.BlockSpec((1, D_MODEL), lambda i, s1, s2: (0, 0)),
                pl.BlockSpec(memory_space=pl.ANY),
            ],
            out_specs=pl.BlockSpec((tm, D_MODEL), lambda i, s1, s2: (i, 0)),
            scratch_shapes=[
                pltpu.VMEM((2, 2, tm, D_MODEL), F32),
                pltpu.SemaphoreType.DMA((2,)),
            ],
        ),
        out_shape=jax.ShapeDtypeStruct((t, D_MODEL), F32),
        compiler_params=_params("arbitrary"),
        name="combine",
    )(slot1, slot2, x2, route, g, ys)


def _swap_halves(w):
    half = w.shape[-1] // 2
    return jnp.concatenate([-w[..., half:], w[..., :half]], axis=-1)


def _row(v):
    return v.reshape(1, -1).astype(F32)


def kernel(x, mem, positions, norm_mix, w_in, norm_q_lat, w_q_b, norm_kv_lat, w_kv_b, norm_mla_out, norm_sb_out, w_out, norm_mem_x, norm_mem_src, w_mem_q, w_mem_kv, w_mem_o, norm_ffn, w_group, b_group, w_expert_router, b_expert, w_gate, w_up, w_down, norm_final):
    batch, seq, d = x.shape
    t = batch * seq
    depth = w_in.shape[0]
    assert depth == 1, "single-layer trunk only"
    xt = x.reshape(t, d)
    pos = positions.reshape(t, 1)
    inv_freq = ROPE_THETA ** (-jnp.arange(0, MLA_ROPE, 2, dtype=F32) / MLA_ROPE)
    invf = jnp.concatenate([inv_freq, inv_freq, jnp.zeros((LANES - MLA_ROPE,), F32)]).reshape(1, LANES)
    n_slots = (2 * t // SLOT_TILE + N_EXPERTS) * SLOT_TILE

    for l in range(depth):
        wi = w_in[l]
        lat_w = MLA_Q_RANK + MLA_KV_RANK
        w_kpe = wi[:, lat_w:lat_w + MLA_ROPE]
        zpad = jnp.zeros((d, LANES - MLA_ROPE), F32)
        sb0 = lat_w + MLA_ROPE
        w_sbq = wi[:, sb0:sb0 + SB_OUT] * (SB_DIM ** -0.5 * LOG2E)
        w_in_pad = jnp.concatenate(
            [wi[:, :sb0], zpad, _swap_halves(w_kpe), zpad, w_sbq, wi[:, sb0 + SB_OUT:sb0 + 2 * SB_OUT]],
            axis=1).astype(BF16)
        w_sbv_t = wi[:, sb0 + 2 * SB_OUT:].T.astype(BF16)
        wq = w_q_b[l].reshape(MLA_Q_RANK, MLA_HEADS, MLA_QK) * (MLA_QK ** -0.5 * LOG2E)
        wq_pe = wq[:, :, MLA_NOPE:]
        zq = jnp.zeros((MLA_Q_RANK, MLA_HEADS, MLA_PAD - MLA_QK), F32)
        wqa = jnp.concatenate([wq, zq], axis=-1).reshape(MLA_Q_RANK, MLA_HEADS * MLA_PAD).astype(BF16)
        wqb = jnp.concatenate([_swap_halves(wq_pe), zq], axis=-1).reshape(MLA_Q_RANK, MLA_HEADS * LANES).astype(BF16)
        wkv = w_kv_b[l].reshape(MLA_KV_RANK, MLA_HEADS, MLA_NOPE + MLA_V)
        wk = wkv[:, :, :MLA_NOPE].reshape(MLA_KV_RANK, MLA_OUT).astype(BF16)
        wvt = wkv[:, :, MLA_NOPE:].reshape(MLA_KV_RANK, MLA_OUT).T.astype(BF16)
        w_router = jnp.concatenate(
            [w_expert_router[l], w_group[l], jnp.zeros((d, LANES - N_EXPERTS - N_GROUPS), F32)], axis=1)
        wr_hi = w_router.astype(BF16)
        wr_lo = (w_router - wr_hi.astype(F32)).astype(BF16)
        b_router = jnp.concatenate(
            [b_expert[l].astype(F32), b_group[l].astype(F32), jnp.zeros((LANES - N_EXPERTS - N_GROUPS,), F32)]
        ).reshape(1, LANES)

        lat, sb_qk, sb_vt = _proj_in(xt, _row(norm_mix[l]), w_in_pad, w_sbv_t)
        q, k, vt = _mla_proj(lat, pos, invf, _row(norm_q_lat[l]), _row(norm_kv_lat[l]), wqa, wqb, wk, wvt)
        o_mla = _mla_attn(q, k, vt, batch, seq)
        o_sb = _sb_attn(sb_qk, sb_vt, batch, seq)
        x1 = _out_proj(xt, o_mla, o_sb, _row(norm_mla_out[l]), _row(norm_sb_out[l]), w_out[l].astype(BF16))

        kv = _mem_kv(mem.reshape(batch * MEM_LEN, d), _row(norm_mem_src[l]), w_mem_kv[l].astype(BF16))
        x2, h2, route = _mem_route(x1, _row(norm_mem_x[l]), w_mem_q[l].astype(BF16), kv,
                                   w_mem_o[l].astype(BF16), _row(norm_ffn[l]), wr_hi, wr_lo, b_router, seq)

        slots, meta = _slots(route)
        tok = _invert(slots[0], slots[1], n_slots)
        ys = _experts(meta[0], meta[1], tok, h2, w_gate[l], w_up[l], w_down[l])
        xt = _combine(slots[0], slots[1], x2, route, _row(norm_final), ys)
    return xt.reshape(batch, seq, d)
```

```python
import functools

import jax
import jax.numpy as jnp
from jax import lax
from jax.experimental import pallas as pl
from jax.experimental.pallas import tpu as pltpu

F32 = jnp.float32
BF16 = jnp.bfloat16

EPS = 1e-6
ROPE_THETA = 10000.0

D_MODEL = 2048
MEM_LEN = 256
MLA_HEADS = 8
MLA_NOPE = 128
MLA_ROPE = 64
MLA_QK = MLA_NOPE + MLA_ROPE
MLA_V = 128
MLA_Q_RANK = 512
MLA_KV_RANK = 256
MLA_PAD = 256
SB_HEADS = 8
SB_DIM = 128
MLA_OUT = MLA_HEADS * MLA_V
SB_OUT = SB_HEADS * SB_DIM
MEM_HEADS = 4
MEM_DIM = 128
N_GROUPS = 4
EXPERTS_PER_GROUP = 8
N_EXPERTS = N_GROUPS * EXPERTS_PER_GROUP
D_EXPERT = 512

LANES = 128
LAT_COLS = 1024
IN_COLS_PAD = LAT_COLS + 3 * SB_OUT

SLOT_TILE = 256
VMEM_LIMIT = 56 * 1024 * 1024


def _rms(x, g):
    return x * lax.rsqrt(jnp.mean(x * x, axis=-1, keepdims=True) + EPS) * g


def _dot(a, b):
    return jnp.dot(a, b, preferred_element_type=F32)


def _dot_nt(a, b):
    return lax.dot_general(a, b, (((1,), (1,)), ((), ())), preferred_element_type=F32)


def _split_bf16(x):
    hi = x.astype(BF16)
    lo = (x - hi.astype(F32)).astype(BF16)
    return hi, lo


def _params(*sem):
    return pltpu.CompilerParams(dimension_semantics=sem, vmem_limit_bytes=VMEM_LIMIT)


def _proj_in_kernel(x_ref, g_ref, w_ref, wvt_ref, lat_ref, qk_ref, vt_ref, h_ref):
    j = pl.program_id(1)
    last = pl.num_programs(1) - 1

    @pl.when(j == 0)
    def _():
        h_ref[...] = _rms(x_ref[...], g_ref[...]).astype(BF16)
        lat_ref[...] = _dot(h_ref[...], w_ref[...])

    @pl.when((j > 0) & (j < last))
    def _():
        qk_ref[...] = _dot(h_ref[...], w_ref[...]).astype(BF16)

    @pl.when(j == last)
    def _():
        vt_ref[...] = _dot_nt(wvt_ref[...], h_ref[...]).astype(BF16)


def _proj_in(x, g, w_pad, wvt, tm=512):
    t = x.shape[0]
    tn = LAT_COLS
    n_w = w_pad.shape[1] // tn
    return pl.pallas_call(
        _proj_in_kernel,
        grid=(t // tm, n_w + 1),
        in_specs=[
            pl.BlockSpec((tm, D_MODEL), lambda i, j: (i, 0)),
            pl.BlockSpec((1, D_MODEL), lambda i, j: (0, 0)),
            pl.BlockSpec((D_MODEL, tn), lambda i, j: (0, jnp.minimum(j, n_w - 1))),
            pl.BlockSpec(wvt.shape, lambda i, j: (0, 0)),
        ],
        out_specs=[
            pl.BlockSpec((tm, tn), lambda i, j: (i, 0)),
            pl.BlockSpec((tm, tn), lambda i, j: (i, jnp.clip(j - 1, 0, n_w - 2))),
            pl.BlockSpec((SB_OUT, tm), lambda i, j: (0, i)),
        ],
        out_shape=[
            jax.ShapeDtypeStruct((t, LAT_COLS), F32),
            jax.ShapeDtypeStruct((t, 2 * SB_OUT), BF16),
            jax.ShapeDtypeStruct((SB_OUT, t), BF16),
        ],
        scratch_shapes=[pltpu.VMEM((tm, D_MODEL), BF16)],
        compiler_params=_params("parallel", "arbitrary"),
        name="proj_in",
    )(x, g, w_pad, wvt)


def _mla_proj_kernel(lat_ref, pos_ref, invf_ref, gq_ref, gkv_ref, wqa_ref, wqb_ref, wk_ref, wvt_ref,
                     q_ref, k_ref, vt_ref):
    cq = _rms(lat_ref[:, :MLA_Q_RANK], gq_ref[...]).astype(BF16)
    ckv = _rms(lat_ref[:, MLA_Q_RANK:MLA_Q_RANK + MLA_KV_RANK], gkv_ref[...]).astype(BF16)
    ang = pos_ref[...].astype(F32) * invf_ref[...]
    live = lax.broadcasted_iota(jnp.int32, ang.shape, 1) < MLA_ROPE
    cos2 = jnp.where(live, jnp.cos(ang), 0.0)
    sin2 = jnp.where(live, jnp.sin(ang), 0.0)

    qa = _dot(cq, wqa_ref[...])
    qb = _dot(cq, wqb_ref[...])
    kn = _dot(ckv, wk_ref[...])
    vt_ref[...] = _dot_nt(wvt_ref[...], ckv).astype(BF16)
    k_pe = (lat_ref[:, 768:896] * cos2 + lat_ref[:, 896:1024] * sin2).astype(BF16)
    for h in range(MLA_HEADS):
        lo = h * MLA_PAD
        mid = lo + LANES
        q_ref[:, lo:mid] = qa[:, lo:mid].astype(BF16)
        q_ref[:, mid:mid + LANES] = (qa[:, mid:mid + LANES] * cos2
                                     + qb[:, h * LANES:(h + 1) * LANES] * sin2).astype(BF16)
        k_ref[:, lo:mid] = kn[:, h * LANES:(h + 1) * LANES].astype(BF16)
        k_ref[:, mid:mid + LANES] = k_pe


def _mla_proj(lat, pos, invf, gq, gkv, wqa, wqb, wk, wvt, tm=512):
    t = lat.shape[0]
    full = lambda a: pl.BlockSpec(a.shape, lambda i: (0, 0))
    return pl.pallas_call(
        _mla_proj_kernel,
        grid=(t // tm,),
        in_specs=[
            pl.BlockSpec((tm, LAT_COLS), lambda i: (i, 0)),
            pl.BlockSpec((tm, 1), lambda i: (i, 0)),
            full(invf), full(gq), full(gkv), full(wqa), full(wqb), full(wk), full(wvt),
        ],
        out_specs=[
            pl.BlockSpec((tm, MLA_HEADS * MLA_PAD), lambda i: (i, 0)),
            pl.BlockSpec((tm, MLA_HEADS * MLA_PAD), lambda i: (i, 0)),
            pl.BlockSpec((MLA_OUT, tm), lambda i: (0, i)),
        ],
        out_shape=[
            jax.ShapeDtypeStruct((t, MLA_HEADS * MLA_PAD), BF16),
            jax.ShapeDtypeStruct((t, MLA_HEADS * MLA_PAD), BF16),
            jax.ShapeDtypeStruct((MLA_OUT, t), BF16),
        ],
        compiler_params=_params("parallel"),
        name="mla_proj",
    )(lat, pos, invf, gq, gkv, wqa, wqb, wk, wvt)


LOG2E = 1.4426950408889634


def _three_stage(n_pairs, stage_a, stage_b, stage_c):
    a_issue, a_finish = stage_a
    b_issue, b_finish = stage_b
    c_issue, c_finish = stage_c

    def run_a(n, slot, first):
        a_finish(n, slot, a_issue(n, slot, first), first)

    run_a(0, 0, True)
    run_a(1, 1, True)
    b_finish(0, 0, b_issue(0, 0))

    def half(na, sa, nb, sb, nc, sc):
        ra = a_issue(na, sa, False)
        rb = b_issue(nb, sb)
        rc = c_issue(nc, sc)
        a_finish(na, sa, ra, False)
        b_finish(nb, sb, rb)
        c_finish(nc, sc, rc)

    def body(p, carry):
        half(2 * p, 0, 2 * p - 1, 1, 2 * p - 2, 0)
        half(2 * p + 1, 1, 2 * p, 0, 2 * p - 1, 1)
        return carry

    lax.fori_loop(1, n_pairs + 1, body, 0)
    last = 2 * n_pairs + 1
    rb = b_issue(last, 1)
    rc = c_issue(last - 1, 0)
    b_finish(last, 1, rb)
    c_finish(last - 1, 0, rc)
    c_finish(last, 1, c_issue(last, 1))


def _two_stage(n_pairs, produce, consume):
    produce(0, 0, True)
    produce(1, 1, True)
    consume(0, 0)

    def body(p, carry):
        produce(2 * p, 0, False)
        consume(2 * p - 1, 1)
        produce(2 * p + 1, 1, False)
        consume(2 * p, 0)
        return carry

    lax.fori_loop(1, n_pairs + 1, body, 0)
    consume(2 * n_pairs + 1, 1)


MLA_GROUP = 4


def _mla_attn_kernel(q_ref, k_ref, vt_ref, o_ref, s_ref, m_ref, l_ref, acc_ref, *, tq, tk):
    i = pl.program_id(2)
    heads = range(MLA_GROUP)
    dq, dv = MLA_PAD, MLA_V
    m_ref[...] = jnp.full(m_ref.shape, -jnp.inf, F32)
    l_ref[...] = jnp.zeros(l_ref.shape, F32)
    acc_ref[...] = jnp.zeros(acc_ref.shape, F32)

    def key_start(n):
        tile = jnp.where(n < 2, 2 * i + n, 2 * i + 1 - n)
        return pl.multiple_of(tile * tk, tk)

    def produce(n, slot, diagonal):
        ks = key_start(n)
        for g in heads:
            st = _dot_nt(k_ref[pl.ds(ks, tk), g * dq:(g + 1) * dq], q_ref[:, g * dq:(g + 1) * dq])
            if diagonal:
                kpos = ks + lax.broadcasted_iota(jnp.int32, st.shape, 0)
                qpos = i * tq + lax.broadcasted_iota(jnp.int32, st.shape, 1)
                st = jnp.where(kpos <= qpos, st, -jnp.inf)
            s_ref[g, slot] = st

    def consume(n, slot):
        ks = key_start(n)
        for g in heads:
            st = s_ref[g, slot]
            m_old = m_ref[g]
            m_new = jnp.maximum(m_old, jnp.max(st, axis=0, keepdims=True))
            alpha = jnp.exp2(m_old - m_new)
            p = jnp.exp2(st - m_new)
            l_ref[g] = alpha * l_ref[g] + jnp.sum(p, axis=0, keepdims=True)
            acc_ref[g] = alpha * acc_ref[g] + _dot(vt_ref[g * dv:(g + 1) * dv, pl.ds(ks, tk)], p.astype(BF16))
            m_ref[g] = m_new

    _two_stage(i, produce, consume)
    for g in heads:
        o_ref[:, g * dv:(g + 1) * dv] = (acc_ref[g] / l_ref[g]).T


def _mla_attn(q, k, vt, batch, seq, tq=512):
    tk = tq // 2
    nq = seq // tq
    groups = MLA_HEADS // MLA_GROUP
    kern = functools.partial(_mla_attn_kernel, tq=tq, tk=tk)
    return pl.pallas_call(
        kern,
        grid=(batch, groups, nq),
        in_specs=[
            pl.BlockSpec((tq, MLA_GROUP * MLA_PAD), lambda b, h, i: (b * nq + i, h)),
            pl.BlockSpec((seq, MLA_GROUP * MLA_PAD), lambda b, h, i: (b, h)),
            pl.BlockSpec((MLA_GROUP * MLA_V, seq), lambda b, h, i: (h, b)),
        ],
        out_specs=pl.BlockSpec((tq, MLA_GROUP * MLA_V), lambda b, h, i: (b * nq + i, h)),
        out_shape=jax.ShapeDtypeStruct((batch * seq, MLA_OUT), F32),
        scratch_shapes=[
            pltpu.VMEM((MLA_GROUP, 2, tk, tq), F32),
            pltpu.VMEM((MLA_GROUP, 1, tq), F32),
            pltpu.VMEM((MLA_GROUP, 1, tq), F32),
            pltpu.VMEM((MLA_GROUP, MLA_V, tq), F32),
        ],
        compiler_params=_params("parallel", "parallel", "arbitrary"),
        name="mla_attn",
    )(q, k, vt)


SB_GROUP = 4


def _sb_attn_kernel(q_ref, k_ref, vt_ref, o_ref, z_ref, hl_ref, arg_ref, acc_ref, c_ref, *, tq, tk):
    i = pl.program_id(2)
    heads = range(SB_GROUP)
    d = SB_DIM
    acc_ref[...] = jnp.zeros(acc_ref.shape, F32)
    c_ref[...] = jnp.zeros(c_ref.shape, F32)
    col = lax.broadcasted_iota(jnp.int32, (tk, 2 * tk), 1)
    row = lax.broadcasted_iota(jnp.int32, (tk, 2 * tk), 0)
    neg_tri2 = jnp.where((col % tk) >= row, -1.0, 0.0).astype(BF16)

    def key_start(n):
        return pl.multiple_of((2 * i + 1 - n) * tk, tk)

    def a_issue(n, slot, masked):
        ks = key_start(n)
        return [_dot_nt(k_ref[pl.ds(ks, tk), g * d:(g + 1) * d], q_ref[:, g * d:(g + 1) * d])
                for g in heads]

    def a_finish(n, slot, z2s, masked):
        for g in heads:
            z2 = z2s[g]
            sp = jnp.maximum(z2, 0.0) + jnp.log2(1.0 + jnp.exp2(-jnp.abs(z2)))
            if masked:
                kpos = key_start(n) + lax.broadcasted_iota(jnp.int32, z2.shape, 0)
                qpos = i * tq + lax.broadcasted_iota(jnp.int32, z2.shape, 1)
                mask = kpos < qpos
                sp = jnp.where(mask, sp, 0.0)
                z2 = jnp.where(mask, z2, -jnp.inf)
            hi, lo = _split_bf16(sp)
            hl_ref[g, slot, :tk, :] = hi
            hl_ref[g, slot, tk:, :] = lo
            z_ref[g, slot] = z2

    def b_issue(n, slot):
        return [_dot(neg_tri2, hl_ref[g, slot]) for g in heads]

    def b_finish(n, slot, laters):
        for g in heads:
            arg_ref[g, slot] = z_ref[g, slot] + laters[g] + c_ref[g]
            c_ref[g] += laters[g][0:1, :]

    def c_issue(n, slot):
        ks = key_start(n)
        return [_dot(vt_ref[g * d:(g + 1) * d, pl.ds(ks, tk)], jnp.exp2(arg_ref[g, slot]).astype(BF16))
                for g in heads]

    def c_finish(n, slot, pvs):
        for g in heads:
            acc_ref[g] += pvs[g]

    _three_stage(i, (a_issue, a_finish), (b_issue, b_finish), (c_issue, c_finish))
    for g in heads:
        o_ref[:, g * d:(g + 1) * d] = acc_ref[g].T


def _sb_attn(qk, vt, batch, seq, tq=512):
    tk = tq // 2
    nq = seq // tq
    groups = SB_HEADS // SB_GROUP
    gd = SB_GROUP * SB_DIM
    kern = functools.partial(_sb_attn_kernel, tq=tq, tk=tk)
    return pl.pallas_call(
        kern,
        grid=(batch, groups, nq),
        in_specs=[
            pl.BlockSpec((tq, gd), lambda b, h, i: (b * nq + i, h)),
            pl.BlockSpec((seq, gd), lambda b, h, i: (b, groups + h)),
            pl.BlockSpec((gd, seq), lambda b, h, i: (h, b)),
        ],
        out_specs=pl.BlockSpec((tq, gd), lambda b, h, i: (b * nq + i, h)),
        out_shape=jax.ShapeDtypeStruct((batch * seq, SB_OUT), F32),
        scratch_shapes=[
            pltpu.VMEM((SB_GROUP, 2, tk, tq), F32),
            pltpu.VMEM((SB_GROUP, 2, 2 * tk, tq), BF16),
            pltpu.VMEM((SB_GROUP, 2, tk, tq), F32),
            pltpu.VMEM((SB_GROUP, SB_DIM, tq), F32),
            pltpu.VMEM((SB_GROUP, 1, tq), F32),
        ],
        compiler_params=_params("parallel", "parallel", "arbitrary"),
        name="sb_attn",
    )(qk, qk, vt)


def _out_proj_kernel(x_ref, oa_ref, ob_ref, ga_ref, gb_ref, w_ref, y_ref):
    na = _rms(oa_ref[...], ga_ref[...]).astype(BF16)
    nb = _rms(ob_ref[...], gb_ref[...]).astype(BF16)
    y_ref[...] = x_ref[...] + _dot(na, w_ref[:MLA_OUT, :]) + _dot(nb, w_ref[MLA_OUT:, :])


def _out_proj(x, oa, ob, ga, gb, w, tm=512):
    t = x.shape[0]
    full = lambda a: pl.BlockSpec(a.shape, lambda i: (0, 0))
    return pl.pallas_call(
        _out_proj_kernel,
        grid=(t // tm,),
        in_specs=[
            pl.BlockSpec((tm, D_MODEL), lambda i: (i, 0)),
            pl.BlockSpec((tm, MLA_OUT), lambda i: (i, 0)),
            pl.BlockSpec((tm, SB_OUT), lambda i: (i, 0)),
            full(ga), full(gb), full(w),
        ],
        out_specs=pl.BlockSpec((tm, D_MODEL), lambda i: (i, 0)),
        out_shape=jax.ShapeDtypeStruct((t, D_MODEL), F32),
        compiler_params=_params("parallel"),
        name="out_proj",
    )(x, oa, ob, ga, gb, w)


def _mem_kv_kernel(mem_ref, g_ref, w_ref, kv_ref):
    kv_ref[...] = _dot(_rms(mem_ref[...], g_ref[...]).astype(BF16), w_ref[...]).astype(BF16)


def _mem_kv(mem, g, w):
    m = mem.shape[0]
    n = w.shape[1]
    full = lambda a: pl.BlockSpec(a.shape, lambda i: (0, 0))
    return pl.pallas_call(
        _mem_kv_kernel,
        grid=(1,),
        in_specs=[full(mem), full(g), full(w)],
        out_specs=pl.BlockSpec((m, n), lambda i: (0, 0)),
        out_shape=jax.ShapeDtypeStruct((m, n), BF16),
        compiler_params=_params("arbitrary"),
        name="mem_kv",
    )(mem, g, w)


GROUP_LANE0 = N_EXPERTS


def _mem_route_kernel(x_ref, gx_ref, wq_ref, kv_ref, wo_ref, gf_ref, wrh_ref, wrl_ref, br_ref,
                      x2_ref, h2_ref, route_ref):
    x1 = x_ref[...]
    q = _dot(_rms(x1, gx_ref[...]).astype(BF16), wq_ref[...]).astype(BF16)
    scale = MEM_DIM ** -0.5
    kw = MEM_HEADS * MEM_DIM
    heads = []
    for h in range(MEM_HEADS):
        lo = h * MEM_DIM
        s = _dot_nt(q[:, lo:lo + MEM_DIM], kv_ref[:, lo:lo + MEM_DIM]) * scale
        e = jnp.exp(s - jnp.max(s, axis=-1, keepdims=True))
        p = (e / jnp.sum(e, axis=-1, keepdims=True)).astype(BF16)
        heads.append(_dot(p, kv_ref[:, kw + lo:kw + lo + MEM_DIM]).astype(BF16))
    o = jnp.concatenate(heads, axis=-1)
    x2 = x1 + _dot(o, wo_ref[...])
    x2_ref[...] = x2
    h2 = _rms(x2, gf_ref[...])
    h2_ref[...] = h2

    hh, hl = _split_bf16(h2)
    lg = _dot(hh, wrh_ref[...]) + _dot(hl, wrh_ref[...]) + _dot(hh, wrl_ref[...]) + br_ref[...]
    lane = lax.broadcasted_iota(jnp.int32, lg.shape, 1)
    big = jnp.int32(1 << 20)
    ninf = -jnp.inf

    def lane_max(v):
        return jnp.max(v, axis=-1, keepdims=True)

    def first_lane(cond):
        return jnp.min(jnp.where(cond, lane, big), axis=-1, keepdims=True)

    is_g = (lane >= GROUP_LANE0) & (lane < GROUP_LANE0 + N_GROUPS)
    g_max = lane_max(jnp.where(is_g, lg, ninf))
    g_sum = jnp.sum(jnp.where(is_g, jnp.exp(lg - g_max), 0.0), axis=-1, keepdims=True)
    p_g = 1.0 / g_sum
    g_idx = first_lane(is_g & (lg == g_max)) - GROUP_LANE0
    in_grp = (lane < N_EXPERTS) & ((lane // EXPERTS_PER_GROUP) == g_idx)
    e_max = lane_max(jnp.where(in_grp, lg, ninf))
    e_sum = jnp.sum(jnp.where(in_grp, jnp.exp(lg - e_max), 0.0), axis=-1, keepdims=True)
    i1 = first_lane(in_grp & (lg == e_max))
    rest = in_grp & (lane != i1)
    e_max2 = lane_max(jnp.where(rest, lg, ninf))
    i2 = first_lane(rest & (lg == e_max2))
    p1 = 1.0 / e_sum
    p2 = jnp.exp(e_max2 - e_max) / e_sum
    den = p1 + p2
    gate1 = p_g * (p1 / den)
    gate2 = p_g * (p2 / den)
    route = jnp.where(lane == 0, i1.astype(F32),
                      jnp.where(lane == 1, i2.astype(F32),
                                jnp.where(lane == 2, gate1,
                                          jnp.where(lane == 3, gate2, 0.0))))
    route_ref[...] = route


def _mem_route(x1, gx, wq, kv, wo, gf, wrh, wrl, br, seq, tm=512):
    t = x1.shape[0]
    per_batch = seq // tm
    full = lambda a: pl.BlockSpec(a.shape, lambda i: (0, 0))
    return pl.pallas_call(
        _mem_route_kernel,
        grid=(t // tm,),
        in_specs=[
            pl.BlockSpec((tm, D_MODEL), lambda i: (i, 0)),
            full(gx), full(wq),
            pl.BlockSpec((MEM_LEN, kv.shape[1]), lambda i: (i // per_batch, 0)),
            full(wo), full(gf), full(wrh), full(wrl), full(br),
        ],
        out_specs=[
            pl.BlockSpec((tm, D_MODEL), lambda i: (i, 0)),
            pl.BlockSpec((tm, D_MODEL), lambda i: (i, 0)),
            pl.BlockSpec((tm, LANES), lambda i: (i, 0)),
        ],
        out_shape=[
            jax.ShapeDtypeStruct((t, D_MODEL), F32),
            jax.ShapeDtypeStruct((t, D_MODEL), F32),
            jax.ShapeDtypeStruct((t, LANES), F32),
        ],
        compiler_params=_params("parallel"),
        name="mem_route",
    )(x1, gx, wq, kv, wo, gf, wrh, wrl, br)


SLOT_BLK = 256


def _slots_kernel(route_ref, slot_ref, meta_ref, cum_ref, *, n_tok):
    nblk = n_tok // SLOT_BLK
    e_iota = lax.broadcasted_iota(jnp.int32, (LANES, SLOT_BLK), 0)
    incl = (lax.broadcasted_iota(jnp.int32, (SLOT_BLK, SLOT_BLK), 0)
            <= lax.broadcasted_iota(jnp.int32, (SLOT_BLK, SLOT_BLK), 1)).astype(BF16)

    def onehots(b):
        ts = pl.multiple_of(b * SLOT_BLK, SLOT_BLK)
        ids = route_ref[pl.ds(ts, SLOT_BLK), :].T
        oh1 = e_iota == ids[0:1, :].astype(jnp.int32)
        oh2 = e_iota == ids[1:2, :].astype(jnp.int32)
        return ts, oh1, oh2

    def count(b, carry):
        ts, oh1, oh2 = onehots(b)
        cnt = (oh1 | oh2).astype(F32).astype(BF16)
        c = _dot(cnt, incl) + carry
        cum_ref[:, pl.ds(ts, SLOT_BLK)] = c
        return c[:, SLOT_BLK - 1:SLOT_BLK]

    total = lax.fori_loop(0, nblk, count, jnp.zeros((LANES, 1), F32))
    tiles = jnp.floor((total + (SLOT_TILE - 1)) * (1.0 / SLOT_TILE))
    below = (lax.broadcasted_iota(jnp.int32, (LANES, LANES), 1)
             < lax.broadcasted_iota(jnp.int32, (LANES, LANES), 0)).astype(BF16)
    tile_lo = _dot(below, jnp.broadcast_to(tiles, (LANES, LANES)).astype(BF16))
    base = tile_lo[:, 0:1] * SLOT_TILE

    def assign(b, carry):
        ts, oh1, oh2 = onehots(b)
        pos = base + cum_ref[:, pl.ds(ts, SLOT_BLK)] - 1.0
        s1 = jnp.sum(jnp.where(oh1, pos, 0.0), axis=0, keepdims=True)
        s2 = jnp.sum(jnp.where(oh2, pos, 0.0), axis=0, keepdims=True)
        row = lax.broadcasted_iota(jnp.int32, (8, SLOT_BLK), 0)
        slot_ref[:, pl.ds(ts, SLOT_BLK)] = jnp.where(row == 0, s1, jnp.where(row == 1, s2, 0.0)).astype(jnp.int32)
        return carry

    lax.fori_loop(0, nblk, assign, 0)

    tile_hi = tile_lo + tiles
    tix = lax.broadcasted_iota(jnp.int32, (LANES, LANES), 1).astype(F32)
    is_e = lax.broadcasted_iota(jnp.int32, (LANES, LANES), 0) < N_EXPERTS
    owner = jnp.sum(jnp.where(is_e & (tile_hi <= tix), 1.0, 0.0), axis=0, keepdims=True)
    n_tiles = jnp.max(jnp.where(is_e, tile_hi, 0.0), axis=0, keepdims=True)
    valid = tix[0:1, :] < n_tiles
    last_owner = jnp.max(jnp.where(valid, owner, 0.0), axis=1, keepdims=True)
    owner = jnp.where(valid, owner, last_owner)
    row = lax.broadcasted_iota(jnp.int32, (8, LANES), 0)
    meta = jnp.where(row == 0, owner, jnp.where(row == 1, valid.astype(F32), 0.0))
    meta_ref[...] = meta.astype(jnp.int32)


def _slots(route):
    t = route.shape[0]
    kern = functools.partial(_slots_kernel, n_tok=t)
    return pl.pallas_call(
        kern,
        grid=(1,),
        in_specs=[pl.BlockSpec(route.shape, lambda i: (0, 0))],
        out_specs=[
            pl.BlockSpec((8, t), lambda i: (0, 0)),
            pl.BlockSpec((8, LANES), lambda i: (0, 0)),
        ],
        out_shape=[
            jax.ShapeDtypeStruct((8, t), jnp.int32),
            jax.ShapeDtypeStruct((8, LANES), jnp.int32),
        ],
        scratch_shapes=[pltpu.VMEM((LANES, t), F32)],
        compiler_params=_params("arbitrary"),
        name="slots",
    )(route)


def _row_copy(src_ref, src_row, dst_ref, dst_row, sem):
    return pltpu.make_async_copy(src_ref.at[pl.ds(src_row, 1), :], dst_ref.at[pl.ds(dst_row, 1), :], sem)


def _invert_kernel(s1_ref, s2_ref, zeros_hbm, tok_ref, sem, *, n_tok):
    fill = pltpu.make_async_copy(zeros_hbm, tok_ref, sem)
    fill.start()
    fill.wait()

    def put(t, c):
        tok_ref[s1_ref[t]] = t
        tok_ref[s2_ref[t]] = t
        return c

    lax.fori_loop(0, n_tok, put, 0, unroll=8)


def _invert(slot1, slot2, n_slots):
    n_tok = slot1.shape[0]
    kern = functools.partial(_invert_kernel, n_tok=n_tok)
    smem = pl.BlockSpec(memory_space=pltpu.SMEM)
    return pl.pallas_call(
        kern,
        in_specs=[smem, smem, pl.BlockSpec(memory_space=pl.ANY)],
        out_specs=smem,
        out_shape=jax.ShapeDtypeStruct((n_slots,), jnp.int32),
        scratch_shapes=[pltpu.SemaphoreType.DMA(())],
        name="invert",
    )(slot1, slot2, jnp.zeros((n_slots,), jnp.int32))


def _experts_kernel(own_ref, valid_ref, tok_ref, h_ref, wg_hbm, wu_hbm, wd_hbm, ys_ref,
                    xbuf, wgf, wuf, wdf, wgb, wub, wdb, xsem, wsem, wslot_ref):
    i = pl.program_id(0)
    nt = pl.num_programs(0)
    at = lambda ref, j: ref[jnp.minimum(j, nt - 1)]

    def weight_copies(e, s):
        return (pltpu.make_async_copy(wg_hbm.at[e], wgf.at[s], wsem.at[s]),
                pltpu.make_async_copy(wu_hbm.at[e], wuf.at[s], wsem.at[s]),
                pltpu.make_async_copy(wd_hbm.at[e], wdf.at[s], wsem.at[s]))

    def gather_start(tile, s):
        base = tile * SLOT_TILE

        def issue(r, c):
            _row_copy(h_ref, tok_ref[base + r], xbuf.at[s], r, xsem.at[s]).start()
            return c

        lax.fori_loop(0, SLOT_TILE, issue, 0, unroll=8)

    def gather_wait(s):
        def drain(r, c):
            _row_copy(h_ref, 0, xbuf.at[s], r, xsem.at[s]).wait()
            return c

        lax.fori_loop(0, SLOT_TILE, drain, 0, unroll=8)

    @pl.when(i == 0)
    def _():
        wslot_ref[0] = 0
        for cp in weight_copies(own_ref[0], 0):
            cp.start(priority=1)
        gather_start(0, 0)

    xs = i % 2

    @pl.when(valid_ref[i] > 0)
    def _():
        e = own_ref[i]

        @pl.when((i == 0) | (own_ref[jnp.maximum(i - 1, 0)] != e))
        def _():
            s = wslot_ref[0]
            for cp in weight_copies(e, s):
                cp.wait()
            j = lax.while_loop(lambda j: (j < nt) & (at(own_ref, j) == e), lambda j: j + 1, i + 1)

            @pl.when((j < nt) & (at(valid_ref, j) > 0))
            def _():
                for cp in weight_copies(at(own_ref, j), 1 - s):
                    cp.start(priority=1)

            wgb[...] = wgf[s].astype(BF16)
            wub[...] = wuf[s].astype(BF16)
            wdb[...] = wdf[s].astype(BF16)
            wslot_ref[0] = 1 - s

        @pl.when((i + 1 < nt) & (at(valid_ref, i + 1) > 0))
        def _():
            gather_start(i + 1, 1 - xs)

        gather_wait(xs)
        x = xbuf[xs].astype(BF16)
        a = _dot(x, wgb[...])
        u = _dot(x, wub[...])
        act = (a * jax.nn.sigmoid(a) * u).astype(BF16)
        ys_ref[...] = _dot(act, wdb[...])

    @pl.when(valid_ref[i] == 0)
    def _():
        ys_ref[...] = jnp.zeros(ys_ref.shape, F32)


def _experts(own, valid, tok, h2, wg, wu, wd):
    n_tiles = tok.shape[0] // SLOT_TILE
    hbm = pl.BlockSpec(memory_space=pl.ANY)
    return pl.pallas_call(
        _experts_kernel,
        grid_spec=pltpu.PrefetchScalarGridSpec(
            num_scalar_prefetch=3,
            grid=(n_tiles,),
            in_specs=[hbm, hbm, hbm, hbm],
            out_specs=pl.BlockSpec((SLOT_TILE, D_MODEL), lambda i, o, v, t: (i, 0)),
            scratch_shapes=[
                pltpu.VMEM((2, SLOT_TILE, D_MODEL), F32),
                pltpu.VMEM((2, D_MODEL, D_EXPERT), F32),
                pltpu.VMEM((2, D_MODEL, D_EXPERT), F32),
                pltpu.VMEM((2, D_EXPERT, D_MODEL), F32),
                pltpu.VMEM((D_MODEL, D_EXPERT), BF16),
                pltpu.VMEM((D_MODEL, D_EXPERT), BF16),
                pltpu.VMEM((D_EXPERT, D_MODEL), BF16),
                pltpu.SemaphoreType.DMA((2,)),
                pltpu.SemaphoreType.DMA((2,)),
                pltpu.SMEM((1,), jnp.int32),
            ],
        ),
        out_shape=jax.ShapeDtypeStruct((tok.shape[0], D_MODEL), F32),
        compiler_params=_params("arbitrary"),
        name="experts",
    )(own, valid, tok, h2, wg, wu, wd)


def _combine_kernel(s1_ref, s2_ref, x_ref, route_ref, g_ref, ys_ref, y_ref, buf_ref, sem, *, tm):
    i = pl.program_id(0)
    n = pl.num_programs(0)

    def copies(tile, p, r):
        t = tile * tm + r
        return (_row_copy(ys_ref, s1_ref[t], buf_ref.at[p, 0], r, sem.at[p]),
                _row_copy(ys_ref, s2_ref[t], buf_ref.at[p, 1], r, sem.at[p]))

    def gather_start(tile, p):
        def issue(r, c):
            first, second = copies(tile, p, r)
            first.start(priority=0)
            second.start(priority=1)
            return c

        lax.fori_loop(0, tm, issue, 0, unroll=8)

    def gather_wait(tile, p):
        def drain(r, c):
            for cp in copies(tile, p, r):
                cp.wait()
            return c

        lax.fori_loop(0, tm, drain, 0, unroll=8)

    @pl.when(i == 0)
    def _():
        gather_start(0, 0)

    p = i % 2

    @pl.when(i + 1 < n)
    def _():
        gather_start(i + 1, 1 - p)

    gather_wait(i, p)
    y = x_ref[...] + route_ref[:, 2:3] * buf_ref[p, 0] + route_ref[:, 3:4] * buf_ref[p, 1]
    y_ref[...] = _rms(y, g_ref[...])


def _combine(slot1, slot2, x2, route, g, ys, tm=256):
    t = x2.shape[0]
    kern = functools.partial(_combine_kernel, tm=tm)
    return pl.pallas_call(
        kern,
        grid_spec=pltpu.PrefetchScalarGridSpec(
            num_scalar_prefetch=2,
            grid=(t // tm,),
            in_specs=[
                pl.BlockSpec((tm, D_MODEL), lambda i, s1, s2: (i, 0)),
                pl.BlockSpec((tm, LANES), lambda i, s1, s2: (i, 0)),
                pl.BlockSpec((1, D_MODEL), lambda i, s1, s2: (0, 0)),
                pl.BlockSpec(memory_space=pl.ANY),
            ],
            out_specs=pl.BlockSpec((tm, D_MODEL), lambda i, s1, s2: (i, 0)),
            scratch_shapes=[
                pltpu.VMEM((2, 2, tm, D_MODEL), F32),
                pltpu.SemaphoreType.DMA((2,)),
            ],
        ),
        out_shape=jax.ShapeDtypeStruct((t, D_MODEL), F32),
        compiler_params=_params("arbitrary"),
        name="combine",
    )(slot1, slot2, x2, route, g, ys)


def _swap_halves(w):
    half = w.shape[-1] // 2
    return jnp.concatenate([-w[..., half:], w[..., :half]], axis=-1)


def _row(v):
    return v.reshape(1, -1).astype(F32)


def kernel(x, mem, positions, norm_mix, w_in, norm_q_lat, w_q_b, norm_kv_lat, w_kv_b, norm_mla_out, norm_sb_out, w_out, norm_mem_x, norm_mem_src, w_mem_q, w_mem_kv, w_mem_o, norm_ffn, w_group, b_group, w_expert_router, b_expert, w_gate, w_up, w_down, norm_final):
    batch, seq, d = x.shape
    t = batch * seq
    depth = w_in.shape[0]
    assert depth == 1, "single-layer trunk only"
    xt = x.reshape(t, d)
    pos = positions.reshape(t, 1)
    inv_freq = ROPE_THETA ** (-jnp.arange(0, MLA_ROPE, 2, dtype=F32) / MLA_ROPE)
    invf = jnp.concatenate([inv_freq, inv_freq, jnp.zeros((LANES - MLA_ROPE,), F32)]).reshape(1, LANES)
    n_slots = (2 * t // SLOT_TILE + N_EXPERTS) * SLOT_TILE

    for l in range(depth):
        wi = w_in[l]
        lat_w = MLA_Q_RANK + MLA_KV_RANK
        w_kpe = wi[:, lat_w:lat_w + MLA_ROPE]
        zpad = jnp.zeros((d, LANES - MLA_ROPE), F32)
        sb0 = lat_w + MLA_ROPE
        w_sbq = wi[:, sb0:sb0 + SB_OUT] * (SB_DIM ** -0.5 * LOG2E)
        w_in_pad = jnp.concatenate(
            [wi[:, :sb0], zpad, _swap_halves(w_kpe), zpad, w_sbq, wi[:, sb0 + SB_OUT:sb0 + 2 * SB_OUT]],
            axis=1).astype(BF16)
        w_sbv_t = wi[:, sb0 + 2 * SB_OUT:].T.astype(BF16)
        wq = w_q_b[l].reshape(MLA_Q_RANK, MLA_HEADS, MLA_QK) * (MLA_QK ** -0.5 * LOG2E)
        wq_pe = wq[:, :, MLA_NOPE:]
        zq = jnp.zeros((MLA_Q_RANK, MLA_HEADS, MLA_PAD - MLA_QK), F32)
        wqa = jnp.concatenate([wq, zq], axis=-1).reshape(MLA_Q_RANK, MLA_HEADS * MLA_PAD).astype(BF16)
        wqb = jnp.concatenate([_swap_halves(wq_pe), zq], axis=-1).reshape(MLA_Q_RANK, MLA_HEADS * LANES).astype(BF16)
        wkv = w_kv_b[l].reshape(MLA_KV_RANK, MLA_HEADS, MLA_NOPE + MLA_V)
        wk = wkv[:, :, :MLA_NOPE].reshape(MLA_KV_RANK, MLA_OUT).astype(BF16)
        wvt = wkv[:, :, MLA_NOPE:].reshape(MLA_KV_RANK, MLA_OUT).T.astype(BF16)
        w_router = jnp.concatenate(
            [w_expert_router[l], w_group[l], jnp.zeros((d, LANES - N_EXPERTS - N_GROUPS), F32)], axis=1)
        wr_hi = w_router.astype(BF16)
        wr_lo = (w_router - wr_hi.astype(F32)).astype(BF16)
        b_router = jnp.concatenate(
            [b_expert[l].astype(F32), b_group[l].astype(F32), jnp.zeros((LANES - N_EXPERTS - N_GROUPS,), F32)]
        ).reshape(1, LANES)

        lat, sb_qk, sb_vt = _proj_in(xt, _row(norm_mix[l]), w_in_pad, w_sbv_t)
        q, k, vt = _mla_proj(lat, pos, invf, _row(norm_q_lat[l]), _row(norm_kv_lat[l]), wqa, wqb, wk, wvt)
        o_mla = _mla_attn(q, k, vt, batch, seq)
        o_sb = _sb_attn(sb_qk, sb_vt, batch, seq)
        x1 = _out_proj(xt, o_mla, o_sb, _row(norm_mla_out[l]), _row(norm_sb_out[l]), w_out[l].astype(BF16))

        kv = _mem_kv(mem.reshape(batch * MEM_LEN, d), _row(norm_mem_src[l]), w_mem_kv[l].astype(BF16))
        x2, h2, route = _mem_route(x1, _row(norm_mem_x[l]), w_mem_q[l].astype(BF16), kv,
                                   w_mem_o[l].astype(BF16), _row(norm_ffn[l]), wr_hi, wr_lo, b_router, seq)

        slots, meta = _slots(route)
        tok = _invert(slots[0], slots[1], n_slots)
        ys = _experts(meta[0], meta[1], tok, h2, w_gate[l], w_up[l], w_down[l])
        xt = _combine(slots[0], slots[1], x2, route, _row(norm_final), ys)
    return xt.reshape(batch, seq, d)
```

```python
import functools

import jax
import jax.numpy as jnp
from jax import lax
from jax.experimental import pallas as pl
from jax.experimental.pallas import tpu as pltpu

F32 = jnp.float32
BF16 = jnp.bfloat16

EPS = 1e-6
ROPE_THETA = 10000.0

D_MODEL = 2048
MEM_LEN = 256
MLA_HEADS = 8
MLA_NOPE = 128
MLA_ROPE = 64
MLA_QK = MLA_NOPE + MLA_ROPE
MLA_V = 128
MLA_Q_RANK = 512
MLA_KV_RANK = 256
MLA_PAD = 256
SB_HEADS = 8
SB_DIM = 128
MLA_OUT = MLA_HEADS * MLA_V
SB_OUT = SB_HEADS * SB_DIM
MEM_HEADS = 4
MEM_DIM = 128
N_GROUPS = 4
EXPERTS_PER_GROUP = 8
N_EXPERTS = N_GROUPS * EXPERTS_PER_GROUP
D_EXPERT = 512

LANES = 128
LAT_COLS = 1024
IN_COLS_PAD = LAT_COLS + 3 * SB_OUT

SLOT_TILE = 256
VMEM_LIMIT = 56 * 1024 * 1024


def _rms(x, g):
    return x * lax.rsqrt(jnp.mean(x * x, axis=-1, keepdims=True) + EPS) * g


def _dot(a, b):
    return jnp.dot(a, b, preferred_element_type=F32)


def _dot_nt(a, b):
    return lax.dot_general(a, b, (((1,), (1,)), ((), ())), preferred_element_type=F32)


def _split_bf16(x):
    hi = x.astype(BF16)
    lo = (x - hi.astype(F32)).astype(BF16)
    return hi, lo


def _params(*sem):
    return pltpu.CompilerParams(dimension_semantics=sem, vmem_limit_bytes=VMEM_LIMIT)


def _proj_in_kernel(x_ref, g_ref, w_ref, wvt_ref, lat_ref, qk_ref, vt_ref, h_ref):
    j = pl.program_id(1)
    last = pl.num_programs(1) - 1

    @pl.when(j == 0)
    def _():
        h_ref[...] = _rms(x_ref[...], g_ref[...]).astype(BF16)
        lat_ref[...] = _dot(h_ref[...], w_ref[...])

    @pl.when((j > 0) & (j < last))
    def _():
        qk_ref[...] = _dot(h_ref[...], w_ref[...]).astype(BF16)

    @pl.when(j == last)
    def _():
        vt_ref[...] = _dot_nt(wvt_ref[...], h_ref[...]).astype(BF16)


def _proj_in(x, g, w_pad, wvt, tm=512):
    t = x.shape[0]
    tn = LAT_COLS
    n_w = w_pad.shape[1] // tn
    return pl.pallas_call(
        _proj_in_kernel,
        grid=(t // tm, n_w + 1),
        in_specs=[
            pl.BlockSpec((tm, D_MODEL), lambda i, j: (i, 0)),
            pl.BlockSpec((1, D_MODEL), lambda i, j: (0, 0)),
            pl.BlockSpec((D_MODEL, tn), lambda i, j: (0, jnp.minimum(j, n_w - 1))),
            pl.BlockSpec(wvt.shape, lambda i, j: (0, 0)),
        ],
        out_specs=[
            pl.BlockSpec((tm, tn), lambda i, j: (i, 0)),
            pl.BlockSpec((tm, tn), lambda i, j: (i, jnp.clip(j - 1, 0, n_w - 2))),
            pl.BlockSpec((SB_OUT, tm), lambda i, j: (0, i)),
        ],
        out_shape=[
            jax.ShapeDtypeStruct((t, LAT_COLS), F32),
            jax.ShapeDtypeStruct((t, 2 * SB_OUT), BF16),
            jax.ShapeDtypeStruct((SB_OUT, t), BF16),
        ],
        scratch_shapes=[pltpu.VMEM((tm, D_MODEL), BF16)],
        compiler_params=_params("parallel", "arbitrary"),
        name="proj_in",
    )(x, g, w_pad, wvt)


def _mla_proj_kernel(lat_ref, pos_ref, invf_ref, gq_ref, gkv_ref, wqa_ref, wqb_ref, wk_ref, wvt_ref,
                     q_ref, k_ref, vt_ref):
    cq = _rms(lat_ref[:, :MLA_Q_RANK], gq_ref[...]).astype(BF16)
    ckv = _rms(lat_ref[:, MLA_Q_RANK:MLA_Q_RANK + MLA_KV_RANK], gkv_ref[...]).astype(BF16)
    ang = pos_ref[...].astype(F32) * invf_ref[...]
    live = lax.broadcasted_iota(jnp.int32, ang.shape, 1) < MLA_ROPE
    cos2 = jnp.where(live, jnp.cos(ang), 0.0)
    sin2 = jnp.where(live, jnp.sin(ang), 0.0)

    qa = _dot(cq, wqa_ref[...])
    qb = _dot(cq, wqb_ref[...])
    kn = _dot(ckv, wk_ref[...])
    vt_ref[...] = _dot_nt(wvt_ref[...], ckv).astype(BF16)
    k_pe = (lat_ref[:, 768:896] * cos2 + lat_ref[:, 896:1024] * sin2).astype(BF16)
    for h in range(MLA_HEADS):
        lo = h * MLA_PAD
        mid = lo + LANES
        q_ref[:, lo:mid] = qa[:, lo:mid].astype(BF16)
        q_ref[:, mid:mid + LANES] = (qa[:, mid:mid + LANES] * cos2
                                     + qb[:, h * LANES:(h + 1) * LANES] * sin2).astype(BF16)
        k_ref[:, lo:mid] = kn[:, h * LANES:(h + 1) * LANES].astype(BF16)
        k_ref[:, mid:mid + LANES] = k_pe


def _mla_proj(lat, pos, invf, gq, gkv, wqa, wqb, wk, wvt, tm=512):
    t = lat.shape[0]
    full = lambda a: pl.BlockSpec(a.shape, lambda i: (0, 0))
    return pl.pallas_call(
        _mla_proj_kernel,
        grid=(t // tm,),
        in_specs=[
            pl.BlockSpec((tm, LAT_COLS), lambda i: (i, 0)),
            pl.BlockSpec((tm, 1), lambda i: (i, 0)),
            full(invf), full(gq), full(gkv), full(wqa), full(wqb), full(wk), full(wvt),
        ],
        out_specs=[
            pl.BlockSpec((tm, MLA_HEADS * MLA_PAD), lambda i: (i, 0)),
            pl.BlockSpec((tm, MLA_HEADS * MLA_PAD), lambda i: (i, 0)),
            pl.BlockSpec((MLA_OUT, tm), lambda i: (0, i)),
        ],
        out_shape=[
            jax.ShapeDtypeStruct((t, MLA_HEADS * MLA_PAD), BF16),
            jax.ShapeDtypeStruct((t, MLA_HEADS * MLA_PAD), BF16),
            jax.ShapeDtypeStruct((MLA_OUT, t), BF16),
        ],
        compiler_params=_params("parallel"),
        name="mla_proj",
    )(lat, pos, invf, gq, gkv, wqa, wqb, wk, wvt)


LOG2E = 1.4426950408889634


def _three_stage(n_pairs, stage_a, stage_b, stage_c):
    a_issue, a_finish = stage_a
    b_issue, b_finish = stage_b
    c_issue, c_finish = stage_c

    def run_a(n, slot, first):
        a_finish(n, slot, a_issue(n, slot, first), first)

    run_a(0, 0, True)
    run_a(1, 1, True)
    b_finish(0, 0, b_issue(0, 0))

    def half(na, sa, nb, sb, nc, sc):
        ra = a_issue(na, sa, False)
        rb = b_issue(nb, sb)
        rc = c_issue(nc, sc)
        a_finish(na, sa, ra, False)
        b_finish(nb, sb, rb)
        c_finish(nc, sc, rc)

    def body(p, carry):
        half(2 * p, 0, 2 * p - 1, 1, 2 * p - 2, 0)
        half(2 * p + 1, 1, 2 * p, 0, 2 * p - 1, 1)
        return carry

    lax.fori_loop(1, n_pairs + 1, body, 0)
    last = 2 * n_pairs + 1
    rb = b_issue(last, 1)
    rc = c_issue(last - 1, 0)
    b_finish(last, 1, rb)
    c_finish(last - 1, 0, rc)
    c_finish(last, 1, c_issue(last, 1))


def _two_stage(n_pairs, produce, consume):
    produce(0, 0, True)
    produce(1, 1, True)
    consume(0, 0)

    def body(p, carry):
        produce(2 * p, 0, False)
        consume(2 * p - 1, 1)
        produce(2 * p + 1, 1, False)
        consume(2 * p, 0)
        return carry

    lax.fori_loop(1, n_pairs + 1, body, 0)
    consume(2 * n_pairs + 1, 1)


MLA_GROUP = 4


def _mla_attn_kernel(q_ref, k_ref, vt_ref, o_ref, s_ref, m_ref, l_ref, acc_ref, *, tq, tk):
    i = pl.program_id(2)
    heads = range(MLA_GROUP)
    dq, dv = MLA_PAD, MLA_V
    m_ref[...] = jnp.full(m_ref.shape, -jnp.inf, F32)
    l_ref[...] = jnp.zeros(l_ref.shape, F32)
    acc_ref[...] = jnp.zeros(acc_ref.shape, F32)

    def key_start(n):
        tile = jnp.where(n < 2, 2 * i + n, 2 * i + 1 - n)
        return pl.multiple_of(tile * tk, tk)

    def produce(n, slot, diagonal):
        ks = key_start(n)
        for g in heads:
            st = _dot_nt(k_ref[pl.ds(ks, tk), g * dq:(g + 1) * dq], q_ref[:, g * dq:(g + 1) * dq])
            if diagonal:
                kpos = ks + lax.broadcasted_iota(jnp.int32, st.shape, 0)
                qpos = i * tq + lax.broadcasted_iota(jnp.int32, st.shape, 1)
                st = jnp.where(kpos <= qpos, st, -jnp.inf)
            s_ref[g, slot] = st

    def consume(n, slot):
        ks = key_start(n)
        for g in heads:
            st = s_ref[g, slot]
            m_old = m_ref[g]
            m_new = jnp.maximum(m_old, jnp.max(st, axis=0, keepdims=True))
            alpha = jnp.exp2(m_old - m_new)
            p = jnp.exp2(st - m_new)
            l_ref[g] = alpha * l_ref[g] + jnp.sum(p, axis=0, keepdims=True)
            acc_ref[g] = alpha * acc_ref[g] + _dot(vt_ref[g * dv:(g + 1) * dv, pl.ds(ks, tk)], p.astype(BF16))
            m_ref[g] = m_new

    _two_stage(i, produce, consume)
    for g in heads:
        o_ref[:, g * dv:(g + 1) * dv] = (acc_ref[g] / l_ref[g]).T


def _mla_attn(q, k, vt, batch, seq, tq=512):
    tk = tq // 2
    nq = seq // tq
    groups = MLA_HEADS // MLA_GROUP
    kern = functools.partial(_mla_attn_kernel, tq=tq, tk=tk)
    return pl.pallas_call(
        kern,
        grid=(batch, groups, nq),
        in_specs=[
            pl.BlockSpec((tq, MLA_GROUP * MLA_PAD), lambda b, h, i: (b * nq + i, h)),
            pl.BlockSpec((seq, MLA_GROUP * MLA_PAD), lambda b, h, i: (b, h)),
            pl.BlockSpec((MLA_GROUP * MLA_V, seq), lambda b, h, i: (h, b)),
        ],
        out_specs=pl.BlockSpec((tq, MLA_GROUP * MLA_V), lambda b, h, i: (b * nq + i, h)),
        out_shape=jax.ShapeDtypeStruct((batch * seq, MLA_OUT), F32),
        scratch_shapes=[
            pltpu.VMEM((MLA_GROUP, 2, tk, tq), F32),
            pltpu.VMEM((MLA_GROUP, 1, tq), F32),
            pltpu.VMEM((MLA_GROUP, 1, tq), F32),
            pltpu.VMEM((MLA_GROUP, MLA_V, tq), F32),
        ],
        compiler_params=_params("parallel", "parallel", "arbitrary"),
        name="mla_attn",
    )(q, k, vt)


SB_GROUP = 4


def _sb_attn_kernel(q_ref, k_ref, vt_ref, o_ref, z_ref, hl_ref, arg_ref, acc_ref, c_ref, *, tq, tk):
    i = pl.program_id(2)
    heads = range(SB_GROUP)
    d = SB_DIM
    acc_ref[...] = jnp.zeros(acc_ref.shape, F32)
    c_ref[...] = jnp.zeros(c_ref.shape, F32)
    col = lax.broadcasted_iota(jnp.int32, (tk, 2 * tk), 1)
    row = lax.broadcasted_iota(jnp.int32, (tk, 2 * tk), 0)
    neg_tri2 = jnp.where((col % tk) >= row, -1.0, 0.0).astype(BF16)

    def key_start(n):
        return pl.multiple_of((2 * i + 1 - n) * tk, tk)

    def a_issue(n, slot, masked):
        ks = key_start(n)
        return [_dot_nt(k_ref[pl.ds(ks, tk), g * d:(g + 1) * d], q_ref[:, g * d:(g + 1) * d])
                for g in heads]

    def a_finish(n, slot, z2s, masked):
        for g in heads:
            z2 = z2s[g]
            sp = jnp.maximum(z2, 0.0) + jnp.log2(1.0 + jnp.exp2(-jnp.abs(z2)))
            if masked:
                kpos = key_start(n) + lax.broadcasted_iota(jnp.int32, z2.shape, 0)
                qpos = i * tq + lax.broadcasted_iota(jnp.int32, z2.shape, 1)
                mask = kpos < qpos
                sp = jnp.where(mask, sp, 0.0)
                z2 = jnp.where(mask, z2, -jnp.inf)
            hi, lo = _split_bf16(sp)
            hl_ref[g, slot, :tk, :] = hi
            hl_ref[g, slot, tk:, :] = lo
            z_ref[g, slot] = z2

    def b_issue(n, slot):
        return [_dot(neg_tri2, hl_ref[g, slot]) for g in heads]

    def b_finish(n, slot, laters):
        for g in heads:
            arg_ref[g, slot] = z_ref[g, slot] + laters[g] + c_ref[g]
            c_ref[g] += laters[g][0:1, :]

    def c_issue(n, slot):
        ks = key_start(n)
        return [_dot(vt_ref[g * d:(g + 1) * d, pl.ds(ks, tk)], jnp.exp2(arg_ref[g, slot]).astype(BF16))
                for g in heads]

    def c_finish(n, slot, pvs):
        for g in heads:
            acc_ref[g] += pvs[g]

    _three_stage(i, (a_issue, a_finish), (b_issue, b_finish), (c_issue, c_finish))
    for g in heads:
        o_ref[:, g * d:(g + 1) * d] = acc_ref[g].T


def _sb_attn(qk, vt, batch, seq, tq=512):
    tk = tq // 2
    nq = seq // tq
    groups = SB_HEADS // SB_GROUP
    gd = SB_GROUP * SB_DIM
    kern = functools.partial(_sb_attn_kernel, tq=tq, tk=tk)
    return pl.pallas_call(
        kern,
        grid=(batch, groups, nq),
        in_specs=[
            pl.BlockSpec((tq, gd), lambda b, h, i: (b * nq + i, h)),
            pl.BlockSpec((seq, gd), lambda b, h, i: (b, groups + h)),
            pl.BlockSpec((gd, seq), lambda b, h, i: (h, b)),
        ],
        out_specs=pl.BlockSpec((tq, gd), lambda b, h, i: (b * nq + i, h)),
        out_shape=jax.ShapeDtypeStruct((batch * seq, SB_OUT), F32),
        scratch_shapes=[
            pltpu.VMEM((SB_GROUP, 2, tk, tq), F32),
            pltpu.VMEM((SB_GROUP, 2, 2 * tk, tq), BF16),
            pltpu.VMEM((SB_GROUP, 2, tk, tq), F32),
            pltpu.VMEM((SB_GROUP, SB_DIM, tq), F32),
            pltpu.VMEM((SB_GROUP, 1, tq), F32),
        ],
        compiler_params=_params("parallel", "parallel", "arbitrary"),
        name="sb_attn",
    )(qk, qk, vt)


def _out_proj_kernel(x_ref, oa_ref, ob_ref, ga_ref, gb_ref, w_ref, y_ref):
    na = _rms(oa_ref[...], ga_ref[...]).astype(BF16)
    nb = _rms(ob_ref[...], gb_ref[...]).astype(BF16)
    y_ref[...] = x_ref[...] + _dot(na, w_ref[:MLA_OUT, :]) + _dot(nb, w_ref[MLA_OUT:, :])


def _out_proj(x, oa, ob, ga, gb, w, tm=512):
    t = x.shape[0]
    full = lambda a: pl.BlockSpec(a.shape, lambda i: (0, 0))
    return pl.pallas_call(
        _out_proj_kernel,
        grid=(t // tm,),
        in_specs=[
            pl.BlockSpec((tm, D_MODEL), lambda i: (i, 0)),
            pl.BlockSpec((tm, MLA_OUT), lambda i: (i, 0)),
            pl.BlockSpec((tm, SB_OUT), lambda i: (i, 0)),
            full(ga), full(gb), full(w),
        ],
        out_specs=pl.BlockSpec((tm, D_MODEL), lambda i: (i, 0)),
        out_shape=jax.ShapeDtypeStruct((t, D_MODEL), F32),
        compiler_params=_params("parallel"),
        name="out_proj",
    )(x, oa, ob, ga, gb, w)


def _mem_kv_kernel(mem_ref, g_ref, w_ref, kv_ref):
    kv_ref[...] = _dot(_rms(mem_ref[...], g_ref[...]).astype(BF16), w_ref[...]).astype(BF16)


def _mem_kv(mem, g, w):
    m = mem.shape[0]
    n = w.shape[1]
    full = lambda a: pl.BlockSpec(a.shape, lambda i: (0, 0))
    return pl.pallas_call(
        _mem_kv_kernel,
        grid=(1,),
        in_specs=[full(mem), full(g), full(w)],
        out_specs=pl.BlockSpec((m, n), lambda i: (0, 0)),
        out_shape=jax.ShapeDtypeStruct((m, n), BF16),
        compiler_params=_params("arbitrary"),
        name="mem_kv",
    )(mem, g, w)


GROUP_LANE0 = N_EXPERTS


def _mem_route_kernel(x_ref, gx_ref, wq_ref, kv_ref, wo_ref, gf_ref, wrh_ref, wrl_ref, br_ref,
                      x2_ref, h2_ref, route_ref):
    x1 = x_ref[...]
    q = _dot(_rms(x1, gx_ref[...]).astype(BF16), wq_ref[...]).astype(BF16)
    scale = MEM_DIM ** -0.5
    kw = MEM_HEADS * MEM_DIM
    heads = []
    for h in range(MEM_HEADS):
        lo = h * MEM_DIM
        s = _dot_nt(q[:, lo:lo + MEM_DIM], kv_ref[:, lo:lo + MEM_DIM]) * scale
        e = jnp.exp(s - jnp.max(s, axis=-1, keepdims=True))
        p = (e / jnp.sum(e, axis=-1, keepdims=True)).astype(BF16)
        heads.append(_dot(p, kv_ref[:, kw + lo:kw + lo + MEM_DIM]).astype(BF16))
    o = jnp.concatenate(heads, axis=-1)
    x2 = x1 + _dot(o, wo_ref[...])
    x2_ref[...] = x2
    h2 = _rms(x2, gf_ref[...])
    h2_ref[...] = h2

    hh, hl = _split_bf16(h2)
    lg = _dot(hh, wrh_ref[...]) + _dot(hl, wrh_ref[...]) + _dot(hh, wrl_ref[...]) + br_ref[...]
    lane = lax.broadcasted_iota(jnp.int32, lg.shape, 1)
    big = jnp.int32(1 << 20)
    ninf = -jnp.inf

    def lane_max(v):
        return jnp.max(v, axis=-1, keepdims=True)

    def first_lane(cond):
        return jnp.min(jnp.where(cond, lane, big), axis=-1, keepdims=True)

    is_g = (lane >= GROUP_LANE0) & (lane < GROUP_LANE0 + N_GROUPS)
    g_max = lane_max(jnp.where(is_g, lg, ninf))
    g_sum = jnp.sum(jnp.where(is_g, jnp.exp(lg - g_max), 0.0), axis=-1, keepdims=True)
    p_g = 1.0 / g_sum
    g_idx = first_lane(is_g & (lg == g_max)) - GROUP_LANE0
    in_grp = (lane < N_EXPERTS) & ((lane // EXPERTS_PER_GROUP) == g_idx)
    e_max = lane_max(jnp.where(in_grp, lg, ninf))
    e_sum = jnp.sum(jnp.where(in_grp, jnp.exp(lg - e_max), 0.0), axis=-1, keepdims=True)
    i1 = first_lane(in_grp & (lg == e_max))
    rest = in_grp & (lane != i1)
    e_max2 = lane_max(jnp.where(rest, lg, ninf))
    i2 = first_lane(rest & (lg == e_max2))
    p1 = 1.0 / e_sum
    p2 = jnp.exp(e_max2 - e_max) / e_sum
    den = p1 + p2
    gate1 = p_g * (p1 / den)
    gate2 = p_g * (p2 / den)
    route = jnp.where(lane == 0, i1.astype(F32),
                      jnp.where(lane == 1, i2.astype(F32),
                                jnp.where(lane == 2, gate1,
                                          jnp.where(lane == 3, gate2, 0.0))))
    route_ref[...] = route


def _mem_route(x1, gx, wq, kv, wo, gf, wrh, wrl, br, seq, tm=512):
    t = x1.shape[0]
    per_batch = seq // tm
    full = lambda a: pl.BlockSpec(a.shape, lambda i: (0, 0))
    return pl.pallas_call(
        _mem_route_kernel,
        grid=(t // tm,),
        in_specs=[
            pl.BlockSpec((tm, D_MODEL), lambda i: (i, 0)),
            full(gx), full(wq),
            pl.BlockSpec((MEM_LEN, kv.shape[1]), lambda i: (i // per_batch, 0)),
            full(wo), full(gf), full(wrh), full(wrl), full(br),
        ],
        out_specs=[
            pl.BlockSpec((tm, D_MODEL), lambda i: (i, 0)),
            pl.BlockSpec((tm, D_MODEL), lambda i: (i, 0)),
            pl.BlockSpec((tm, LANES), lambda i: (i, 0)),
        ],
        out_shape=[
            jax.ShapeDtypeStruct((t, D_MODEL), F32),
            jax.ShapeDtypeStruct((t, D_MODEL), F32),
            jax.ShapeDtypeStruct((t, LANES), F32),
        ],
        compiler_params=_params("parallel"),
        name="mem_route",
    )(x1, gx, wq, kv, wo, gf, wrh, wrl, br)


SLOT_BLK = 256


def _slots_kernel(route_ref, slot_ref, meta_ref, cum_ref, *, n_tok):
    nblk = n_tok // SLOT_BLK
    e_iota = lax.broadcasted_iota(jnp.int32, (LANES, SLOT_BLK), 0)
    incl = (lax.broadcasted_iota(jnp.int32, (SLOT_BLK, SLOT_BLK), 0)
            <= lax.broadcasted_iota(jnp.int32, (SLOT_BLK, SLOT_BLK), 1)).astype(BF16)

    def onehots(b):
        ts = pl.multiple_of(b * SLOT_BLK, SLOT_BLK)
        ids = route_ref[pl.ds(ts, SLOT_BLK), :].T
        oh1 = e_iota == ids[0:1, :].astype(jnp.int32)
        oh2 = e_iota == ids[1:2, :].astype(jnp.int32)
        return ts, oh1, oh2

    def count(b, carry):
        ts, oh1, oh2 = onehots(b)
        cnt = (oh1 | oh2).astype(F32).astype(BF16)
        c = _dot(cnt, incl) + carry
        cum_ref[:, pl.ds(ts, SLOT_BLK)] = c
        return c[:, SLOT_BLK - 1:SLOT_BLK]

    total = lax.fori_loop(0, nblk, count, jnp.zeros((LANES, 1), F32))
    tiles = jnp.floor((total + (SLOT_TILE - 1)) * (1.0 / SLOT_TILE))
    below = (lax.broadcasted_iota(jnp.int32, (LANES, LANES), 1)
             < lax.broadcasted_iota(jnp.int32, (LANES, LANES), 0)).astype(BF16)
    tile_lo = _dot(below, jnp.broadcast_to(tiles, (LANES, LANES)).astype(BF16))
    base = tile_lo[:, 0:1] * SLOT_TILE

    def assign(b, carry):
        ts, oh1, oh2 = onehots(b)
        pos = base + cum_ref[:, pl.ds(ts, SLOT_BLK)] - 1.0
        s1 = jnp.sum(jnp.where(oh1, pos, 0.0), axis=0, keepdims=True)
        s2 = jnp.sum(jnp.where(oh2, pos, 0.0), axis=0, keepdims=True)
        row = lax.broadcasted_iota(jnp.int32, (8, SLOT_BLK), 0)
        slot_ref[:, pl.ds(ts, SLOT_BLK)] = jnp.where(row == 0, s1, jnp.where(row == 1, s2, 0.0)).astype(jnp.int32)
        return carry

    lax.fori_loop(0, nblk, assign, 0)

    tile_hi = tile_lo + tiles
    tix = lax.broadcasted_iota(jnp.int32, (LANES, LANES), 1).astype(F32)
    is_e = lax.broadcasted_iota(jnp.int32, (LANES, LANES), 0) < N_EXPERTS
    owner = jnp.sum(jnp.where(is_e & (tile_hi <= tix), 1.0, 0.0), axis=0, keepdims=True)
    n_tiles = jnp.max(jnp.where(is_e, tile_hi, 0.0), axis=0, keepdims=True)
    valid = tix[0:1, :] < n_tiles
    last_owner = jnp.max(jnp.where(valid, owner, 0.0), axis=1, keepdims=True)
    owner = jnp.where(valid, owner, last_owner)
    eye = (lax.broadcasted_iota(jnp.int32, (LANES, LANES), 0)
           == lax.broadcasted_iota(jnp.int32, (LANES, LANES), 1))
    to_lanes = lambda colvec: jnp.sum(jnp.where(eye, colvec, 0.0), axis=0, keepdims=True)
    pad_first = to_lanes(base + total)
    pad_end = to_lanes(base + tiles * SLOT_TILE)
    row = lax.broadcasted_iota(jnp.int32, (8, LANES), 0)
    meta = jnp.where(row == 0, owner,
                     jnp.where(row == 1, valid.astype(F32),
                               jnp.where(row == 2, pad_first, jnp.where(row == 3, pad_end, 0.0))))
    meta_ref[...] = meta.astype(jnp.int32)


def _slots(route):
    t = route.shape[0]
    kern = functools.partial(_slots_kernel, n_tok=t)
    return pl.pallas_call(
        kern,
        grid=(1,),
        in_specs=[pl.BlockSpec(route.shape, lambda i: (0, 0))],
        out_specs=[
            pl.BlockSpec((8, t), lambda i: (0, 0)),
            pl.BlockSpec((8, LANES), lambda i: (0, 0)),
        ],
        out_shape=[
            jax.ShapeDtypeStruct((8, t), jnp.int32),
            jax.ShapeDtypeStruct((8, LANES), jnp.int32),
        ],
        scratch_shapes=[pltpu.VMEM((LANES, t), F32)],
        compiler_params=_params("arbitrary"),
        name="slots",
    )(route)


def _row_copy(src_ref, src_row, dst_ref, dst_row, sem):
    return pltpu.make_async_copy(src_ref.at[pl.ds(src_row, 1), :], dst_ref.at[pl.ds(dst_row, 1), :], sem)


def _dispatch_kernel(s1_ref, s2_ref, pad_first_ref, pad_end_ref, h_ref, xs_ref, zero_ref, sem, pad_sem, *, tm):
    i = pl.program_id(0)
    t0 = i * tm

    def copies(r):
        return (_row_copy(h_ref, r, xs_ref, s1_ref[t0 + r], sem),
                _row_copy(h_ref, r, xs_ref, s2_ref[t0 + r], sem))

    def issue(r, c):
        first, second = copies(r)
        first.start(priority=0)
        second.start(priority=1)
        return c

    lax.fori_loop(0, tm, issue, 0, unroll=8)

    @pl.when(i == pl.num_programs(0) - 1)
    def _():
        zero_ref[...] = jnp.zeros(zero_ref.shape, F32)

        def pad_copy(slot):
            return _row_copy(zero_ref, 0, xs_ref, slot, pad_sem)

        def fill(e, c):
            def one(s, cc):
                pad_copy(s).start()
                return cc

            return lax.fori_loop(pad_first_ref[e], pad_end_ref[e], one, c)

        lax.fori_loop(0, N_EXPERTS, fill, 0)

        def settle(e, c):
            def one(s, cc):
                pad_copy(s).wait()
                return cc

            return lax.fori_loop(pad_first_ref[e], pad_end_ref[e], one, c)

        lax.fori_loop(0, N_EXPERTS, settle, 0)

    def drain(r, c):
        for cp in copies(r):
            cp.wait()
        return c

    lax.fori_loop(0, tm, drain, 0, unroll=8)


def _dispatch(slot1, slot2, pad_first, pad_end, h2, n_slots, tm=256):
    t = h2.shape[0]
    kern = functools.partial(_dispatch_kernel, tm=tm)
    return pl.pallas_call(
        kern,
        grid_spec=pltpu.PrefetchScalarGridSpec(
            num_scalar_prefetch=4,
            grid=(t // tm,),
            in_specs=[pl.BlockSpec((tm, D_MODEL), lambda i, *_: (i, 0))],
            out_specs=pl.BlockSpec(memory_space=pl.ANY),
            scratch_shapes=[
                pltpu.VMEM((8, D_MODEL), F32),
                pltpu.SemaphoreType.DMA(()),
                pltpu.SemaphoreType.DMA(()),
            ],
        ),
        out_shape=jax.ShapeDtypeStruct((n_slots, D_MODEL), F32),
        compiler_params=_params("arbitrary"),
        name="dispatch",
    )(slot1, slot2, pad_first, pad_end, h2)


def _experts_kernel(own_ref, valid_ref, xs_ref, wg_hbm, wu_hbm, wd_hbm, ys_ref,
                    wgf, wuf, wdf, wgb, wub, wdb, wsem, wslot_ref):
    i = pl.program_id(0)
    nt = pl.num_programs(0)
    at = lambda ref, j: ref[jnp.minimum(j, nt - 1)]

    def weight_copies(e, s):
        return (pltpu.make_async_copy(wg_hbm.at[e], wgf.at[s], wsem.at[s]),
                pltpu.make_async_copy(wu_hbm.at[e], wuf.at[s], wsem.at[s]),
                pltpu.make_async_copy(wd_hbm.at[e], wdf.at[s], wsem.at[s]))

    @pl.when(i == 0)
    def _():
        wslot_ref[0] = 0
        for cp in weight_copies(own_ref[0], 0):
            cp.start()

    @pl.when(valid_ref[i] > 0)
    def _():
        e = own_ref[i]

        @pl.when((i == 0) | (own_ref[jnp.maximum(i - 1, 0)] != e))
        def _():
            s = wslot_ref[0]
            for cp in weight_copies(e, s):
                cp.wait()
            j = lax.while_loop(lambda j: (j < nt) & (at(own_ref, j) == e), lambda j: j + 1, i + 1)

            @pl.when((j < nt) & (at(valid_ref, j) > 0))
            def _():
                for cp in weight_copies(at(own_ref, j), 1 - s):
                    cp.start()

            wgb[...] = wgf[s].astype(BF16)
            wub[...] = wuf[s].astype(BF16)
            wdb[...] = wdf[s].astype(BF16)
            wslot_ref[0] = 1 - s

        x = xs_ref[...].astype(BF16)
        a = _dot(x, wgb[...])
        u = _dot(x, wub[...])
        act = (a * jax.nn.sigmoid(a) * u).astype(BF16)
        ys_ref[...] = _dot(act, wdb[...])

    @pl.when(valid_ref[i] == 0)
    def _():
        ys_ref[...] = jnp.zeros(ys_ref.shape, F32)


def _experts(own, valid, xs, wg, wu, wd):
    n_tiles = xs.shape[0] // SLOT_TILE
    hbm = pl.BlockSpec(memory_space=pl.ANY)
    tile = (SLOT_TILE, D_MODEL)
    return pl.pallas_call(
        _experts_kernel,
        grid_spec=pltpu.PrefetchScalarGridSpec(
            num_scalar_prefetch=2,
            grid=(n_tiles,),
            in_specs=[
                pl.BlockSpec(tile, lambda i, o, v: (jnp.where(v[i] > 0, i, 0), 0)),
                hbm, hbm, hbm,
            ],
            out_specs=pl.BlockSpec(tile, lambda i, o, v: (i, 0)),
            scratch_shapes=[
                pltpu.VMEM((2, D_MODEL, D_EXPERT), F32),
                pltpu.VMEM((2, D_MODEL, D_EXPERT), F32),
                pltpu.VMEM((2, D_EXPERT, D_MODEL), F32),
                pltpu.VMEM((D_MODEL, D_EXPERT), BF16),
                pltpu.VMEM((D_MODEL, D_EXPERT), BF16),
                pltpu.VMEM((D_EXPERT, D_MODEL), BF16),
                pltpu.SemaphoreType.DMA((2,)),
                pltpu.SMEM((1,), jnp.int32),
            ],
        ),
        out_shape=jax.ShapeDtypeStruct(xs.shape, F32),
        compiler_params=_params("arbitrary"),
        name="experts",
    )(own, valid, xs, wg, wu, wd)


def _combine_kernel(s1_ref, s2_ref, x_ref, route_ref, g_ref, ys_ref, y_ref, buf_ref, sem, *, tm):
    i = pl.program_id(0)
    n = pl.num_programs(0)

    def copies(tile, p, r):
        t = tile * tm + r
        return (_row_copy(ys_ref, s1_ref[t], buf_ref.at[p, 0], r, sem.at[p]),
                _row_copy(ys_ref, s2_ref[t], buf_ref.at[p, 1], r, sem.at[p]))

    def gather_start(tile, p):
        def issue(r, c):
            first, second = copies(tile, p, r)
            first.start(priority=0)
            second.start(priority=1)
            return c

        lax.fori_loop(0, tm, issue, 0, unroll=8)

    def gather_wait(tile, p):
        def drain(r, c):
            for cp in copies(tile, p, r):
                cp.wait()
            return c

        lax.fori_loop(0, tm, drain, 0, unroll=8)

    @pl.when(i == 0)
    def _():
        gather_start(0, 0)

    p = i % 2

    @pl.when(i + 1 < n)
    def _():
        gather_start(i + 1, 1 - p)

    gather_wait(i, p)
    y = x_ref[...] + route_ref[:, 2:3] * buf_ref[p, 0] + route_ref[:, 3:4] * buf_ref[p, 1]
    y_ref[...] = _rms(y, g_ref[...])


def _combine(slot1, slot2, x2, route, g, ys, tm=256):
    t = x2.shape[0]
    kern = functools.partial(_combine_kernel, tm=tm)
    return pl.pallas_call(
        kern,
        grid_spec=pltpu.PrefetchScalarGridSpec(
            num_scalar_prefetch=2,
            grid=(t // tm,),
            in_specs=[
                pl.BlockSpec((tm, D_MODEL), lambda i, s1, s2: (i, 0)),
                pl.BlockSpec((tm, LANES), lambda i, s1, s2: (i, 0)),
                pl.BlockSpec((1, D_MODEL), lambda i, s1, s2: (0, 0)),
                pl.BlockSpec(memory_space=pl.ANY),
            ],
            out_specs=pl.BlockSpec((tm, D_MODEL), lambda i, s1, s2: (i, 0)),
            scratch_shapes=[
                pltpu.VMEM((2, 2, tm, D_MODEL), F32),
                pltpu.SemaphoreType.DMA((2,)),
            ],
        ),
        out_shape=jax.ShapeDtypeStruct((t, D_MODEL), F32),
        compiler_params=_params("arbitrary"),
        name="combine",
    )(slot1, slot2, x2, route, g, ys)


def _swap_halves(w):
    half = w.shape[-1] // 2
    return jnp.concatenate([-w[..., half:], w[..., :half]], axis=-1)


def _row(v):
    return v.reshape(1, -1).astype(F32)


def kernel(x, mem, positions, norm_mix, w_in, norm_q_lat, w_q_b, norm_kv_lat, w_kv_b, norm_mla_out, norm_sb_out, w_out, norm_mem_x, norm_mem_src, w_mem_q, w_mem_kv, w_mem_o, norm_ffn, w_group, b_group, w_expert_router, b_expert, w_gate, w_up, w_down, norm_final):
    batch, seq, d = x.shape
    t = batch * seq
    depth = w_in.shape[0]
    assert depth == 1, "single-layer trunk only"
    xt = x.reshape(t, d)
    pos = positions.reshape(t, 1)
    inv_freq = ROPE_THETA ** (-jnp.arange(0, MLA_ROPE, 2, dtype=F32) / MLA_ROPE)
    invf = jnp.concatenate([inv_freq, inv_freq, jnp.zeros((LANES - MLA_ROPE,), F32)]).reshape(1, LANES)
    n_slots = (2 * t // SLOT_TILE + N_EXPERTS) * SLOT_TILE

    for l in range(depth):
        wi = w_in[l]
        lat_w = MLA_Q_RANK + MLA_KV_RANK
        w_kpe = wi[:, lat_w:lat_w + MLA_ROPE]
        zpad = jnp.zeros((d, LANES - MLA_ROPE), F32)
        sb0 = lat_w + MLA_ROPE
        w_sbq = wi[:, sb0:sb0 + SB_OUT] * (SB_DIM ** -0.5 * LOG2E)
        w_in_pad = jnp.concatenate(
            [wi[:, :sb0], zpad, _swap_halves(w_kpe), zpad, w_sbq, wi[:, sb0 + SB_OUT:sb0 + 2 * SB_OUT]],
            axis=1).astype(BF16)
        w_sbv_t = wi[:, sb0 + 2 * SB_OUT:].T.astype(BF16)
        wq = w_q_b[l].reshape(MLA_Q_RANK, MLA_HEADS, MLA_QK) * (MLA_QK ** -0.5 * LOG2E)
        wq_pe = wq[:, :, MLA_NOPE:]
        zq = jnp.zeros((MLA_Q_RANK, MLA_HEADS, MLA_PAD - MLA_QK), F32)
        wqa = jnp.concatenate([wq, zq], axis=-1).reshape(MLA_Q_RANK, MLA_HEADS * MLA_PAD).astype(BF16)
        wqb = jnp.concatenate([_swap_halves(wq_pe), zq], axis=-1).reshape(MLA_Q_RANK, MLA_HEADS * LANES).astype(BF16)
        wkv = w_kv_b[l].reshape(MLA_KV_RANK, MLA_HEADS, MLA_NOPE + MLA_V)
        wk = wkv[:, :, :MLA_NOPE].reshape(MLA_KV_RANK, MLA_OUT).astype(BF16)
        wvt = wkv[:, :, MLA_NOPE:].reshape(MLA_KV_RANK, MLA_OUT).T.astype(BF16)
        w_router = jnp.concatenate(
            [w_expert_router[l], w_group[l], jnp.zeros((d, LANES - N_EXPERTS - N_GROUPS), F32)], axis=1)
        wr_hi = w_router.astype(BF16)
        wr_lo = (w_router - wr_hi.astype(F32)).astype(BF16)
        b_router = jnp.concatenate(
            [b_expert[l].astype(F32), b_group[l].astype(F32), jnp.zeros((LANES - N_EXPERTS - N_GROUPS,), F32)]
        ).reshape(1, LANES)

        lat, sb_qk, sb_vt = _proj_in(xt, _row(norm_mix[l]), w_in_pad, w_sbv_t)
        q, k, vt = _mla_proj(lat, pos, invf, _row(norm_q_lat[l]), _row(norm_kv_lat[l]), wqa, wqb, wk, wvt)
        o_mla = _mla_attn(q, k, vt, batch, seq)
        o_sb = _sb_attn(sb_qk, sb_vt, batch, seq)
        x1 = _out_proj(xt, o_mla, o_sb, _row(norm_mla_out[l]), _row(norm_sb_out[l]), w_out[l].astype(BF16))

        kv = _mem_kv(mem.reshape(batch * MEM_LEN, d), _row(norm_mem_src[l]), w_mem_kv[l].astype(BF16))
        x2, h2, route = _mem_route(x1, _row(norm_mem_x[l]), w_mem_q[l].astype(BF16), kv,
                                   w_mem_o[l].astype(BF16), _row(norm_ffn[l]), wr_hi, wr_lo, b_router, seq)

        slots, meta = _slots(route)
        xs = _dispatch(slots[0], slots[1], meta[2], meta[3], h2, n_slots)
        ys = _experts(meta[0], meta[1], xs, w_gate[l], w_up[l], w_down[l])
        xt = _combine(slots[0], slots[1], x2, route, _row(norm_final), ys)
    return xt.reshape(batch, seq, d)
```

```python
import functools

import jax
import jax.numpy as jnp
from jax import lax
from jax.experimental import pallas as pl
from jax.experimental.pallas import tpu as pltpu

F32 = jnp.float32
BF16 = jnp.bfloat16

EPS = 1e-6
ROPE_THETA = 10000.0

D_MODEL = 2048
MEM_LEN = 256
MLA_HEADS = 8
MLA_NOPE = 128
MLA_ROPE = 64
MLA_QK = MLA_NOPE + MLA_ROPE
MLA_V = 128
MLA_Q_RANK = 512
MLA_KV_RANK = 256
MLA_PAD = 256
SB_HEADS = 8
SB_DIM = 128
MLA_OUT = MLA_HEADS * MLA_V
SB_OUT = SB_HEADS * SB_DIM
MEM_HEADS = 4
MEM_DIM = 128
N_GROUPS = 4
EXPERTS_PER_GROUP = 8
N_EXPERTS = N_GROUPS * EXPERTS_PER_GROUP
D_EXPERT = 512

LANES = 128
LAT_COLS = 1024
IN_COLS_PAD = LAT_COLS + 3 * SB_OUT

SLOT_TILE = 256
VMEM_LIMIT = 56 * 1024 * 1024


def _rms(x, g):
    return x * lax.rsqrt(jnp.mean(x * x, axis=-1, keepdims=True) + EPS) * g


def _dot(a, b):
    return jnp.dot(a, b, preferred_element_type=F32)


def _dot_nt(a, b):
    return lax.dot_general(a, b, (((1,), (1,)), ((), ())), preferred_element_type=F32)


def _split_bf16(x):
    hi = x.astype(BF16)
    lo = (x - hi.astype(F32)).astype(BF16)
    return hi, lo


def _params(*sem):
    return pltpu.CompilerParams(dimension_semantics=sem, vmem_limit_bytes=VMEM_LIMIT)


def _proj_in_kernel(x_ref, g_ref, w_ref, wvt_ref, lat_ref, qk_ref, vt_ref, h_ref):
    j = pl.program_id(1)
    last = pl.num_programs(1) - 1

    @pl.when(j == 0)
    def _():
        h_ref[...] = _rms(x_ref[...], g_ref[...]).astype(BF16)
        lat_ref[...] = _dot(h_ref[...], w_ref[...])

    @pl.when((j > 0) & (j < last))
    def _():
        qk_ref[...] = _dot(h_ref[...], w_ref[...]).astype(BF16)

    @pl.when(j == last)
    def _():
        vt_ref[...] = _dot_nt(wvt_ref[...], h_ref[...]).astype(BF16)


def _proj_in(x, g, w_pad, wvt, tm=512):
    t = x.shape[0]
    tn = LAT_COLS
    n_w = w_pad.shape[1] // tn
    return pl.pallas_call(
        _proj_in_kernel,
        grid=(t // tm, n_w + 1),
        in_specs=[
            pl.BlockSpec((tm, D_MODEL), lambda i, j: (i, 0)),
            pl.BlockSpec((1, D_MODEL), lambda i, j: (0, 0)),
            pl.BlockSpec((D_MODEL, tn), lambda i, j: (0, jnp.minimum(j, n_w - 1))),
            pl.BlockSpec(wvt.shape, lambda i, j: (0, 0)),
        ],
        out_specs=[
            pl.BlockSpec((tm, tn), lambda i, j: (i, 0)),
            pl.BlockSpec((tm, tn), lambda i, j: (i, jnp.clip(j - 1, 0, n_w - 2))),
            pl.BlockSpec((SB_OUT, tm), lambda i, j: (0, i)),
        ],
        out_shape=[
            jax.ShapeDtypeStruct((t, LAT_COLS), F32),
            jax.ShapeDtypeStruct((t, 2 * SB_OUT), BF16),
            jax.ShapeDtypeStruct((SB_OUT, t), BF16),
        ],
        scratch_shapes=[pltpu.VMEM((tm, D_MODEL), BF16)],
        compiler_params=_params("parallel", "arbitrary"),
        name="proj_in",
    )(x, g, w_pad, wvt)


def _mla_proj_kernel(lat_ref, pos_ref, invf_ref, gq_ref, gkv_ref, wqa_ref, wqb_ref, wk_ref, wvt_ref,
                     q_ref, k_ref, vt_ref):
    cq = _rms(lat_ref[:, :MLA_Q_RANK], gq_ref[...]).astype(BF16)
    ckv = _rms(lat_ref[:, MLA_Q_RANK:MLA_Q_RANK + MLA_KV_RANK], gkv_ref[...]).astype(BF16)
    ang = pos_ref[...].astype(F32) * invf_ref[...]
    live = lax.broadcasted_iota(jnp.int32, ang.shape, 1) < MLA_ROPE
    cos2 = jnp.where(live, jnp.cos(ang), 0.0)
    sin2 = jnp.where(live, jnp.sin(ang), 0.0)

    qa = _dot(cq, wqa_ref[...])
    qb = _dot(cq, wqb_ref[...])
    kn = _dot(ckv, wk_ref[...])
    vt_ref[...] = _dot_nt(wvt_ref[...], ckv).astype(BF16)
    k_pe = (lat_ref[:, 768:896] * cos2 + lat_ref[:, 896:1024] * sin2).astype(BF16)
    for h in range(MLA_HEADS):
        lo = h * MLA_PAD
        mid = lo + LANES
        q_ref[:, lo:mid] = qa[:, lo:mid].astype(BF16)
        q_ref[:, mid:mid + LANES] = (qa[:, mid:mid + LANES] * cos2
                                     + qb[:, h * LANES:(h + 1) * LANES] * sin2).astype(BF16)
        k_ref[:, lo:mid] = kn[:, h * LANES:(h + 1) * LANES].astype(BF16)
        k_ref[:, mid:mid + LANES] = k_pe


def _mla_proj(lat, pos, invf, gq, gkv, wqa, wqb, wk, wvt, tm=512):
    t = lat.shape[0]
    full = lambda a: pl.BlockSpec(a.shape, lambda i: (0, 0))
    return pl.pallas_call(
        _mla_proj_kernel,
        grid=(t // tm,),
        in_specs=[
            pl.BlockSpec((tm, LAT_COLS), lambda i: (i, 0)),
            pl.BlockSpec((tm, 1), lambda i: (i, 0)),
            full(invf), full(gq), full(gkv), full(wqa), full(wqb), full(wk), full(wvt),
        ],
        out_specs=[
            pl.BlockSpec((tm, MLA_HEADS * MLA_PAD), lambda i: (i, 0)),
            pl.BlockSpec((tm, MLA_HEADS * MLA_PAD), lambda i: (i, 0)),
            pl.BlockSpec((MLA_OUT, tm), lambda i: (0, i)),
        ],
        out_shape=[
            jax.ShapeDtypeStruct((t, MLA_HEADS * MLA_PAD), BF16),
            jax.ShapeDtypeStruct((t, MLA_HEADS * MLA_PAD), BF16),
            jax.ShapeDtypeStruct((MLA_OUT, t), BF16),
        ],
        compiler_params=_params("parallel"),
        name="mla_proj",
    )(lat, pos, invf, gq, gkv, wqa, wqb, wk, wvt)


LOG2E = 1.4426950408889634


def _three_stage(n_pairs, stage_a, stage_b, stage_c):
    a_issue, a_finish = stage_a
    b_issue, b_finish = stage_b
    c_issue, c_finish = stage_c

    def run_a(n, slot, first):
        a_finish(n, slot, a_issue(n, slot, first), first)

    run_a(0, 0, True)
    run_a(1, 1, True)
    b_finish(0, 0, b_issue(0, 0))

    def half(na, sa, nb, sb, nc, sc):
        ra = a_issue(na, sa, False)
        rb = b_issue(nb, sb)
        rc = c_issue(nc, sc)
        a_finish(na, sa, ra, False)
        b_finish(nb, sb, rb)
        c_finish(nc, sc, rc)

    def body(p, carry):
        half(2 * p, 0, 2 * p - 1, 1, 2 * p - 2, 0)
        half(2 * p + 1, 1, 2 * p, 0, 2 * p - 1, 1)
        return carry

    lax.fori_loop(1, n_pairs + 1, body, 0)
    last = 2 * n_pairs + 1
    rb = b_issue(last, 1)
    rc = c_issue(last - 1, 0)
    b_finish(last, 1, rb)
    c_finish(last - 1, 0, rc)
    c_finish(last, 1, c_issue(last, 1))


def _two_stage(n_pairs, produce, consume):
    produce(0, 0, True)
    produce(1, 1, True)
    consume(0, 0)

    def body(p, carry):
        produce(2 * p, 0, False)
        consume(2 * p - 1, 1)
        produce(2 * p + 1, 1, False)
        consume(2 * p, 0)
        return carry

    lax.fori_loop(1, n_pairs + 1, body, 0)
    consume(2 * n_pairs + 1, 1)


MLA_GROUP = 4


def _mla_attn_kernel(q_ref, k_ref, vt_ref, o_ref, s_ref, m_ref, l_ref, acc_ref, *, tq, tk):
    i = pl.program_id(2)
    heads = range(MLA_GROUP)
    dq, dv = MLA_PAD, MLA_V
    m_ref[...] = jnp.full(m_ref.shape, -jnp.inf, F32)
    l_ref[...] = jnp.zeros(l_ref.shape, F32)
    acc_ref[...] = jnp.zeros(acc_ref.shape, F32)

    def key_start(n):
        tile = jnp.where(n < 2, 2 * i + n, 2 * i + 1 - n)
        return pl.multiple_of(tile * tk, tk)

    def produce(n, slot, diagonal):
        ks = key_start(n)
        for g in heads:
            st = _dot_nt(k_ref[pl.ds(ks, tk), g * dq:(g + 1) * dq], q_ref[:, g * dq:(g + 1) * dq])
            if diagonal:
                kpos = ks + lax.broadcasted_iota(jnp.int32, st.shape, 0)
                qpos = i * tq + lax.broadcasted_iota(jnp.int32, st.shape, 1)
                st = jnp.where(kpos <= qpos, st, -jnp.inf)
            s_ref[g, slot] = st

    def consume(n, slot):
        ks = key_start(n)
        for g in heads:
            st = s_ref[g, slot]
            m_old = m_ref[g]
            m_new = jnp.maximum(m_old, jnp.max(st, axis=0, keepdims=True))
            alpha = jnp.exp2(m_old - m_new)
            p = jnp.exp2(st - m_new)
            l_ref[g] = alpha * l_ref[g] + jnp.sum(p, axis=0, keepdims=True)
            acc_ref[g] = alpha * acc_ref[g] + _dot(vt_ref[g * dv:(g + 1) * dv, pl.ds(ks, tk)], p.astype(BF16))
            m_ref[g] = m_new

    _two_stage(i, produce, consume)
    for g in heads:
        o_ref[:, g * dv:(g + 1) * dv] = (acc_ref[g] / l_ref[g]).T


def _mla_attn(q, k, vt, batch, seq, tq=512):
    tk = tq // 2
    nq = seq // tq
    groups = MLA_HEADS // MLA_GROUP
    kern = functools.partial(_mla_attn_kernel, tq=tq, tk=tk)
    return pl.pallas_call(
        kern,
        grid=(batch, groups, nq),
        in_specs=[
            pl.BlockSpec((tq, MLA_GROUP * MLA_PAD), lambda b, h, i: (b * nq + i, h)),
            pl.BlockSpec((seq, MLA_GROUP * MLA_PAD), lambda b, h, i: (b, h)),
            pl.BlockSpec((MLA_GROUP * MLA_V, seq), lambda b, h, i: (h, b)),
        ],
        out_specs=pl.BlockSpec((tq, MLA_GROUP * MLA_V), lambda b, h, i: (b * nq + i, h)),
        out_shape=jax.ShapeDtypeStruct((batch * seq, MLA_OUT), F32),
        scratch_shapes=[
            pltpu.VMEM((MLA_GROUP, 2, tk, tq), F32),
            pltpu.VMEM((MLA_GROUP, 1, tq), F32),
            pltpu.VMEM((MLA_GROUP, 1, tq), F32),
            pltpu.VMEM((MLA_GROUP, MLA_V, tq), F32),
        ],
        compiler_params=_params("parallel", "parallel", "arbitrary"),
        name="mla_attn",
    )(q, k, vt)


SB_EXP_CLAMP = 126.0
SB_GROUP = 4


def _sb_attn_kernel(q_ref, k_ref, vt_ref, o_ref, z_ref, hl_ref, arg_ref, acc_ref, c_ref, *, tq, tk):
    i = pl.program_id(2)
    heads = range(SB_GROUP)
    d = SB_DIM
    acc_ref[...] = jnp.zeros(acc_ref.shape, F32)
    c_ref[...] = jnp.zeros(c_ref.shape, F32)
    col = lax.broadcasted_iota(jnp.int32, (tk, tk), 1)
    row = lax.broadcasted_iota(jnp.int32, (tk, tk), 0)
    neg_tri = jnp.where(col >= row, -1.0, 0.0).astype(BF16)

    def key_start(n):
        return pl.multiple_of((2 * i + 1 - n) * tk, tk)

    def a_issue(n, slot, masked):
        ks = key_start(n)
        return [_dot_nt(k_ref[pl.ds(ks, tk), g * d:(g + 1) * d], q_ref[:, g * d:(g + 1) * d])
                for g in heads]

    def a_finish(n, slot, z2s, masked):
        for g in heads:
            z2 = z2s[g]
            sp = jnp.maximum(z2, jnp.log2(1.0 + jnp.exp2(jnp.minimum(z2, SB_EXP_CLAMP))))
            if masked:
                kpos = key_start(n) + lax.broadcasted_iota(jnp.int32, z2.shape, 0)
                qpos = i * tq + lax.broadcasted_iota(jnp.int32, z2.shape, 1)
                mask = kpos < qpos
                sp = jnp.where(mask, sp, 0.0)
                z2 = jnp.where(mask, z2, -jnp.inf)
            hl_ref[g, slot] = sp.astype(BF16)
            z_ref[g, slot] = z2

    def b_issue(n, slot):
        return [_dot(neg_tri, hl_ref[g, slot]) for g in heads]

    def b_finish(n, slot, laters):
        for g in heads:
            arg_ref[g, slot] = z_ref[g, slot] + laters[g] + c_ref[g]
            c_ref[g] += laters[g][0:1, :]

    def c_issue(n, slot):
        ks = key_start(n)
        return [_dot(vt_ref[g * d:(g + 1) * d, pl.ds(ks, tk)], jnp.exp2(arg_ref[g, slot]).astype(BF16))
                for g in heads]

    def c_finish(n, slot, pvs):
        for g in heads:
            acc_ref[g] += pvs[g]

    _three_stage(i, (a_issue, a_finish), (b_issue, b_finish), (c_issue, c_finish))
    for g in heads:
        o_ref[:, g * d:(g + 1) * d] = acc_ref[g].T


def _sb_attn(qk, vt, batch, seq, tq=512):
    tk = tq // 2
    nq = seq // tq
    groups = SB_HEADS // SB_GROUP
    gd = SB_GROUP * SB_DIM
    kern = functools.partial(_sb_attn_kernel, tq=tq, tk=tk)
    return pl.pallas_call(
        kern,
        grid=(batch, groups, nq),
        in_specs=[
            pl.BlockSpec((tq, gd), lambda b, h, i: (b * nq + i, h)),
            pl.BlockSpec((seq, gd), lambda b, h, i: (b, groups + h)),
            pl.BlockSpec((gd, seq), lambda b, h, i: (h, b)),
        ],
        out_specs=pl.BlockSpec((tq, gd), lambda b, h, i: (b * nq + i, h)),
        out_shape=jax.ShapeDtypeStruct((batch * seq, SB_OUT), F32),
        scratch_shapes=[
            pltpu.VMEM((SB_GROUP, 2, tk, tq), F32),
            pltpu.VMEM((SB_GROUP, 2, tk, tq), BF16),
            pltpu.VMEM((SB_GROUP, 2, tk, tq), F32),
            pltpu.VMEM((SB_GROUP, SB_DIM, tq), F32),
            pltpu.VMEM((SB_GROUP, 1, tq), F32),
        ],
        compiler_params=_params("parallel", "parallel", "arbitrary"),
        name="sb_attn",
    )(qk, qk, vt)


def _out_proj_kernel(x_ref, oa_ref, ob_ref, ga_ref, gb_ref, w_ref, y_ref):
    na = _rms(oa_ref[...], ga_ref[...]).astype(BF16)
    nb = _rms(ob_ref[...], gb_ref[...]).astype(BF16)
    y_ref[...] = x_ref[...] + _dot(na, w_ref[:MLA_OUT, :]) + _dot(nb, w_ref[MLA_OUT:, :])


def _out_proj(x, oa, ob, ga, gb, w, tm=512):
    t = x.shape[0]
    full = lambda a: pl.BlockSpec(a.shape, lambda i: (0, 0))
    return pl.pallas_call(
        _out_proj_kernel,
        grid=(t // tm,),
        in_specs=[
            pl.BlockSpec((tm, D_MODEL), lambda i: (i, 0)),
            pl.BlockSpec((tm, MLA_OUT), lambda i: (i, 0)),
            pl.BlockSpec((tm, SB_OUT), lambda i: (i, 0)),
            full(ga), full(gb), full(w),
        ],
        out_specs=pl.BlockSpec((tm, D_MODEL), lambda i: (i, 0)),
        out_shape=jax.ShapeDtypeStruct((t, D_MODEL), F32),
        compiler_params=_params("parallel"),
        name="out_proj",
    )(x, oa, ob, ga, gb, w)


def _mem_kv_kernel(mem_ref, g_ref, w_ref, kv_ref):
    kv_ref[...] = _dot(_rms(mem_ref[...], g_ref[...]).astype(BF16), w_ref[...]).astype(BF16)


def _mem_kv(mem, g, w):
    m = mem.shape[0]
    n = w.shape[1]
    full = lambda a: pl.BlockSpec(a.shape, lambda i: (0, 0))
    return pl.pallas_call(
        _mem_kv_kernel,
        grid=(1,),
        in_specs=[full(mem), full(g), full(w)],
        out_specs=pl.BlockSpec((m, n), lambda i: (0, 0)),
        out_shape=jax.ShapeDtypeStruct((m, n), BF16),
        compiler_params=_params("arbitrary"),
        name="mem_kv",
    )(mem, g, w)


GROUP_LANE0 = N_EXPERTS


def _mem_route_kernel(x_ref, gx_ref, wq_ref, kv_ref, wo_ref, gf_ref, wrh_ref, wrl_ref, br_ref,
                      x2_ref, h2_ref, route_ref):
    x1 = x_ref[...]
    q = _dot(_rms(x1, gx_ref[...]).astype(BF16), wq_ref[...]).astype(BF16)
    scale = MEM_DIM ** -0.5
    kw = MEM_HEADS * MEM_DIM
    heads = []
    for h in range(MEM_HEADS):
        lo = h * MEM_DIM
        s = _dot_nt(q[:, lo:lo + MEM_DIM], kv_ref[:, lo:lo + MEM_DIM]) * scale
        e = jnp.exp(s - jnp.max(s, axis=-1, keepdims=True))
        p = (e / jnp.sum(e, axis=-1, keepdims=True)).astype(BF16)
        heads.append(_dot(p, kv_ref[:, kw + lo:kw + lo + MEM_DIM]).astype(BF16))
    o = jnp.concatenate(heads, axis=-1)
    x2 = x1 + _dot(o, wo_ref[...])
    x2_ref[...] = x2
    h2 = _rms(x2, gf_ref[...])
    h2_ref[...] = h2

    hh, hl = _split_bf16(h2)
    lg = _dot(hh, wrh_ref[...]) + _dot(hl, wrh_ref[...]) + _dot(hh, wrl_ref[...]) + br_ref[...]
    lane = lax.broadcasted_iota(jnp.int32, lg.shape, 1)
    big = jnp.int32(1 << 20)
    ninf = -jnp.inf

    def lane_max(v):
        return jnp.max(v, axis=-1, keepdims=True)

    def first_lane(cond):
        return jnp.min(jnp.where(cond, lane, big), axis=-1, keepdims=True)

    is_g = (lane >= GROUP_LANE0) & (lane < GROUP_LANE0 + N_GROUPS)
    g_max = lane_max(jnp.where(is_g, lg, ninf))
    g_sum = jnp.sum(jnp.where(is_g, jnp.exp(lg - g_max), 0.0), axis=-1, keepdims=True)
    p_g = 1.0 / g_sum
    g_idx = first_lane(is_g & (lg == g_max)) - GROUP_LANE0
    in_grp = (lane < N_EXPERTS) & ((lane // EXPERTS_PER_GROUP) == g_idx)
    e_max = lane_max(jnp.where(in_grp, lg, ninf))
    e_sum = jnp.sum(jnp.where(in_grp, jnp.exp(lg - e_max), 0.0), axis=-1, keepdims=True)
    i1 = first_lane(in_grp & (lg == e_max))
    rest = in_grp & (lane != i1)
    e_max2 = lane_max(jnp.where(rest, lg, ninf))
    i2 = first_lane(rest & (lg == e_max2))
    p1 = 1.0 / e_sum
    p2 = jnp.exp(e_max2 - e_max) / e_sum
    den = p1 + p2
    gate1 = p_g * (p1 / den)
    gate2 = p_g * (p2 / den)
    route = jnp.where(lane == 0, i1.astype(F32),
                      jnp.where(lane == 1, i2.astype(F32),
                                jnp.where(lane == 2, gate1,
                                          jnp.where(lane == 3, gate2, 0.0))))
    route_ref[...] = route


def _mem_route(x1, gx, wq, kv, wo, gf, wrh, wrl, br, seq, tm=512):
    t = x1.shape[0]
    per_batch = seq // tm
    full = lambda a: pl.BlockSpec(a.shape, lambda i: (0, 0))
    return pl.pallas_call(
        _mem_route_kernel,
        grid=(t // tm,),
        in_specs=[
            pl.BlockSpec((tm, D_MODEL), lambda i: (i, 0)),
            full(gx), full(wq),
            pl.BlockSpec((MEM_LEN, kv.shape[1]), lambda i: (i // per_batch, 0)),
            full(wo), full(gf), full(wrh), full(wrl), full(br),
        ],
        out_specs=[
            pl.BlockSpec((tm, D_MODEL), lambda i: (i, 0)),
            pl.BlockSpec((tm, D_MODEL), lambda i: (i, 0)),
            pl.BlockSpec((tm, LANES), lambda i: (i, 0)),
        ],
        out_shape=[
            jax.ShapeDtypeStruct((t, D_MODEL), F32),
            jax.ShapeDtypeStruct((t, D_MODEL), F32),
            jax.ShapeDtypeStruct((t, LANES), F32),
        ],
        compiler_params=_params("parallel"),
        name="mem_route",
    )(x1, gx, wq, kv, wo, gf, wrh, wrl, br)


SLOT_BLK = 256


def _slots_kernel(route_ref, slot_ref, meta_ref, cum_ref, *, n_tok):
    nblk = n_tok // SLOT_BLK
    e_iota = lax.broadcasted_iota(jnp.int32, (LANES, SLOT_BLK), 0)
    incl = (lax.broadcasted_iota(jnp.int32, (SLOT_BLK, SLOT_BLK), 0)
            <= lax.broadcasted_iota(jnp.int32, (SLOT_BLK, SLOT_BLK), 1)).astype(BF16)

    def onehots(b):
        ts = pl.multiple_of(b * SLOT_BLK, SLOT_BLK)
        ids = route_ref[pl.ds(ts, SLOT_BLK), :].T
        oh1 = e_iota == ids[0:1, :].astype(jnp.int32)
        oh2 = e_iota == ids[1:2, :].astype(jnp.int32)
        return ts, oh1, oh2

    def count(b, carry):
        ts, oh1, oh2 = onehots(b)
        cnt = (oh1 | oh2).astype(F32).astype(BF16)
        c = _dot(cnt, incl) + carry
        cum_ref[:, pl.ds(ts, SLOT_BLK)] = c
        return c[:, SLOT_BLK - 1:SLOT_BLK]

    total = lax.fori_loop(0, nblk, count, jnp.zeros((LANES, 1), F32))
    tiles = jnp.floor((total + (SLOT_TILE - 1)) * (1.0 / SLOT_TILE))
    below = (lax.broadcasted_iota(jnp.int32, (LANES, LANES), 1)
             < lax.broadcasted_iota(jnp.int32, (LANES, LANES), 0)).astype(BF16)
    tile_lo = _dot(below, jnp.broadcast_to(tiles, (LANES, LANES)).astype(BF16))
    base = tile_lo[:, 0:1] * SLOT_TILE

    def assign(b, carry):
        ts, oh1, oh2 = onehots(b)
        pos = base + cum_ref[:, pl.ds(ts, SLOT_BLK)] - 1.0
        s1 = jnp.sum(jnp.where(oh1, pos, 0.0), axis=0, keepdims=True)
        s2 = jnp.sum(jnp.where(oh2, pos, 0.0), axis=0, keepdims=True)
        row = lax.broadcasted_iota(jnp.int32, (8, SLOT_BLK), 0)
        slot_ref[:, pl.ds(ts, SLOT_BLK)] = jnp.where(row == 0, s1, jnp.where(row == 1, s2, 0.0)).astype(jnp.int32)
        return carry

    lax.fori_loop(0, nblk, assign, 0)

    tile_hi = tile_lo + tiles
    tix = lax.broadcasted_iota(jnp.int32, (LANES, LANES), 1).astype(F32)
    is_e = lax.broadcasted_iota(jnp.int32, (LANES, LANES), 0) < N_EXPERTS
    owner = jnp.sum(jnp.where(is_e & (tile_hi <= tix), 1.0, 0.0), axis=0, keepdims=True)
    n_tiles = jnp.max(jnp.where(is_e, tile_hi, 0.0), axis=0, keepdims=True)
    valid = tix[0:1, :] < n_tiles
    last_owner = jnp.max(jnp.where(valid, owner, 0.0), axis=1, keepdims=True)
    owner = jnp.where(valid, owner, last_owner)
    eye = (lax.broadcasted_iota(jnp.int32, (LANES, LANES), 0)
           == lax.broadcasted_iota(jnp.int32, (LANES, LANES), 1))
    to_lanes = lambda colvec: jnp.sum(jnp.where(eye, colvec, 0.0), axis=0, keepdims=True)
    pad_first = to_lanes(base + total)
    pad_end = to_lanes(base + tiles * SLOT_TILE)
    row = lax.broadcasted_iota(jnp.int32, (8, LANES), 0)
    meta = jnp.where(row == 0, owner,
                     jnp.where(row == 1, valid.astype(F32),
                               jnp.where(row == 2, pad_first, jnp.where(row == 3, pad_end, 0.0))))
    meta_ref[...] = meta.astype(jnp.int32)


def _slots(route):
    t = route.shape[0]
    kern = functools.partial(_slots_kernel, n_tok=t)
    return pl.pallas_call(
        kern,
        grid=(1,),
        in_specs=[pl.BlockSpec(route.shape, lambda i: (0, 0))],
        out_specs=[
            pl.BlockSpec((8, t), lambda i: (0, 0)),
            pl.BlockSpec((8, LANES), lambda i: (0, 0)),
        ],
        out_shape=[
            jax.ShapeDtypeStruct((8, t), jnp.int32),
            jax.ShapeDtypeStruct((8, LANES), jnp.int32),
        ],
        scratch_shapes=[pltpu.VMEM((LANES, t), F32)],
        compiler_params=_params("arbitrary"),
        name="slots",
    )(route)


def _row_copy(src_ref, src_row, dst_ref, dst_row, sem):
    return pltpu.make_async_copy(src_ref.at[pl.ds(src_row, 1), :], dst_ref.at[pl.ds(dst_row, 1), :], sem)


def _dispatch_kernel(s1_ref, s2_ref, pad_first_ref, pad_end_ref, h_ref, xs_ref, zero_ref, sem, pad_sem, *, tm):
    i = pl.program_id(0)
    t0 = i * tm

    def copies(r):
        return (_row_copy(h_ref, r, xs_ref, s1_ref[t0 + r], sem),
                _row_copy(h_ref, r, xs_ref, s2_ref[t0 + r], sem))

    def issue(r, c):
        first, second = copies(r)
        first.start(priority=0)
        second.start(priority=1)
        return c

    lax.fori_loop(0, tm, issue, 0, unroll=8)

    @pl.when(i == pl.num_programs(0) - 1)
    def _():
        zero_ref[...] = jnp.zeros(zero_ref.shape, F32)

        def pad_copy(slot):
            return _row_copy(zero_ref, 0, xs_ref, slot, pad_sem)

        def fill(e, c):
            def one(s, cc):
                pad_copy(s).start()
                return cc

            return lax.fori_loop(pad_first_ref[e], pad_end_ref[e], one, c)

        lax.fori_loop(0, N_EXPERTS, fill, 0)

        def settle(e, c):
            def one(s, cc):
                pad_copy(s).wait()
                return cc

            return lax.fori_loop(pad_first_ref[e], pad_end_ref[e], one, c)

        lax.fori_loop(0, N_EXPERTS, settle, 0)

    def drain(r, c):
        for cp in copies(r):
            cp.wait()
        return c

    lax.fori_loop(0, tm, drain, 0, unroll=8)


def _dispatch(slot1, slot2, pad_first, pad_end, h2, n_slots, tm=256):
    t = h2.shape[0]
    kern = functools.partial(_dispatch_kernel, tm=tm)
    return pl.pallas_call(
        kern,
        grid_spec=pltpu.PrefetchScalarGridSpec(
            num_scalar_prefetch=4,
            grid=(t // tm,),
            in_specs=[pl.BlockSpec((tm, D_MODEL), lambda i, *_: (i, 0))],
            out_specs=pl.BlockSpec(memory_space=pl.ANY),
            scratch_shapes=[
                pltpu.VMEM((8, D_MODEL), F32),
                pltpu.SemaphoreType.DMA(()),
                pltpu.SemaphoreType.DMA(()),
            ],
        ),
        out_shape=jax.ShapeDtypeStruct((n_slots, D_MODEL), F32),
        compiler_params=_params("arbitrary"),
        name="dispatch",
    )(slot1, slot2, pad_first, pad_end, h2)


def _experts_kernel(own_ref, valid_ref, xs_ref, wg_hbm, wu_hbm, wd_hbm, ys_ref,
                    wgf, wuf, wdf, wgb, wub, wdb, wsem, wslot_ref):
    i = pl.program_id(0)
    nt = pl.num_programs(0)
    at = lambda ref, j: ref[jnp.minimum(j, nt - 1)]

    def weight_copies(e, s):
        return (pltpu.make_async_copy(wg_hbm.at[e], wgf.at[s], wsem.at[s]),
                pltpu.make_async_copy(wu_hbm.at[e], wuf.at[s], wsem.at[s]),
                pltpu.make_async_copy(wd_hbm.at[e], wdf.at[s], wsem.at[s]))

    @pl.when(i == 0)
    def _():
        wslot_ref[0] = 0
        for cp in weight_copies(own_ref[0], 0):
            cp.start()

    @pl.when(valid_ref[i] > 0)
    def _():
        e = own_ref[i]

        @pl.when((i == 0) | (own_ref[jnp.maximum(i - 1, 0)] != e))
        def _():
            s = wslot_ref[0]
            for cp in weight_copies(e, s):
                cp.wait()
            j = lax.while_loop(lambda j: (j < nt) & (at(own_ref, j) == e), lambda j: j + 1, i + 1)

            @pl.when((j < nt) & (at(valid_ref, j) > 0))
            def _():
                for cp in weight_copies(at(own_ref, j), 1 - s):
                    cp.start()

            wgb[...] = wgf[s].astype(BF16)
            wub[...] = wuf[s].astype(BF16)
            wdb[...] = wdf[s].astype(BF16)
            wslot_ref[0] = 1 - s

        x = xs_ref[...].astype(BF16)
        a = _dot(x, wgb[...])
        u = _dot(x, wub[...])
        act = (a * jax.nn.sigmoid(a) * u).astype(BF16)
        ys_ref[...] = _dot(act, wdb[...])

    @pl.when(valid_ref[i] == 0)
    def _():
        ys_ref[...] = jnp.zeros(ys_ref.shape, F32)


def _experts(own, valid, xs, wg, wu, wd):
    n_tiles = xs.shape[0] // SLOT_TILE
    hbm = pl.BlockSpec(memory_space=pl.ANY)
    tile = (SLOT_TILE, D_MODEL)
    return pl.pallas_call(
        _experts_kernel,
        grid_spec=pltpu.PrefetchScalarGridSpec(
            num_scalar_prefetch=2,
            grid=(n_tiles,),
            in_specs=[
                pl.BlockSpec(tile, lambda i, o, v: (jnp.where(v[i] > 0, i, 0), 0)),
                hbm, hbm, hbm,
            ],
            out_specs=pl.BlockSpec(tile, lambda i, o, v: (i, 0)),
            scratch_shapes=[
                pltpu.VMEM((2, D_MODEL, D_EXPERT), F32),
                pltpu.VMEM((2, D_MODEL, D_EXPERT), F32),
                pltpu.VMEM((2, D_EXPERT, D_MODEL), F32),
                pltpu.VMEM((D_MODEL, D_EXPERT), BF16),
                pltpu.VMEM((D_MODEL, D_EXPERT), BF16),
                pltpu.VMEM((D_EXPERT, D_MODEL), BF16),
                pltpu.SemaphoreType.DMA((2,)),
                pltpu.SMEM((1,), jnp.int32),
            ],
        ),
        out_shape=jax.ShapeDtypeStruct(xs.shape, F32),
        compiler_params=_params("arbitrary"),
        name="experts",
    )(own, valid, xs, wg, wu, wd)


def _combine_kernel(s1_ref, s2_ref, x_ref, route_ref, g_ref, ys_ref, y_ref, buf_ref, sem, *, tm):
    i = pl.program_id(0)
    n = pl.num_programs(0)

    def copies(tile, p, r):
        t = tile * tm + r
        return (_row_copy(ys_ref, s1_ref[t], buf_ref.at[p, 0], r, sem.at[p]),
                _row_copy(ys_ref, s2_ref[t], buf_ref.at[p, 1], r, sem.at[p]))

    def gather_start(tile, p):
        def issue(r, c):
            first, second = copies(tile, p, r)
            first.start(priority=0)
            second.start(priority=1)
            return c

        lax.fori_loop(0, tm, issue, 0, unroll=8)

    def gather_wait(tile, p):
        def drain(r, c):
            for cp in copies(tile, p, r):
                cp.wait()
            return c

        lax.fori_loop(0, tm, drain, 0, unroll=8)

    @pl.when(i == 0)
    def _():
        gather_start(0, 0)

    p = i % 2

    @pl.when(i + 1 < n)
    def _():
        gather_start(i + 1, 1 - p)

    gather_wait(i, p)
    y = x_ref[...] + route_ref[:, 2:3] * buf_ref[p, 0] + route_ref[:, 3:4] * buf_ref[p, 1]
    y_ref[...] = _rms(y, g_ref[...])


def _combine(slot1, slot2, x2, route, g, ys, tm=256):
    t = x2.shape[0]
    kern = functools.partial(_combine_kernel, tm=tm)
    return pl.pallas_call(
        kern,
        grid_spec=pltpu.PrefetchScalarGridSpec(
            num_scalar_prefetch=2,
            grid=(t // tm,),
            in_specs=[
                pl.BlockSpec((tm, D_MODEL), lambda i, s1, s2: (i, 0)),
                pl.BlockSpec((tm, LANES), lambda i, s1, s2: (i, 0)),
                pl.BlockSpec((1, D_MODEL), lambda i, s1, s2: (0, 0)),
                pl.BlockSpec(memory_space=pl.ANY),
            ],
            out_specs=pl.BlockSpec((tm, D_MODEL), lambda i, s1, s2: (i, 0)),
            scratch_shapes=[
                pltpu.VMEM((2, 2, tm, D_MODEL), F32),
                pltpu.SemaphoreType.DMA((2,)),
            ],
        ),
        out_shape=jax.ShapeDtypeStruct((t, D_MODEL), F32),
        compiler_params=_params("arbitrary"),
        name="combine",
    )(slot1, slot2, x2, route, g, ys)


def _swap_halves(w):
    half = w.shape[-1] // 2
    return jnp.concatenate([-w[..., half:], w[..., :half]], axis=-1)


def _row(v):
    return v.reshape(1, -1).astype(F32)


def kernel(x, mem, positions, norm_mix, w_in, norm_q_lat, w_q_b, norm_kv_lat, w_kv_b, norm_mla_out, norm_sb_out, w_out, norm_mem_x, norm_mem_src, w_mem_q, w_mem_kv, w_mem_o, norm_ffn, w_group, b_group, w_expert_router, b_expert, w_gate, w_up, w_down, norm_final):
    batch, seq, d = x.shape
    t = batch * seq
    depth = w_in.shape[0]
    assert depth == 1, "single-layer trunk only"
    xt = x.reshape(t, d)
    pos = positions.reshape(t, 1)
    inv_freq = ROPE_THETA ** (-jnp.arange(0, MLA_ROPE, 2, dtype=F32) / MLA_ROPE)
    invf = jnp.concatenate([inv_freq, inv_freq, jnp.zeros((LANES - MLA_ROPE,), F32)]).reshape(1, LANES)
    n_slots = (2 * t // SLOT_TILE + N_EXPERTS) * SLOT_TILE

    for l in range(depth):
        wi = w_in[l]
        lat_w = MLA_Q_RANK + MLA_KV_RANK
        w_kpe = wi[:, lat_w:lat_w + MLA_ROPE]
        zpad = jnp.zeros((d, LANES - MLA_ROPE), F32)
        sb0 = lat_w + MLA_ROPE
        w_sbq = wi[:, sb0:sb0 + SB_OUT] * (SB_DIM ** -0.5 * LOG2E)
        w_in_pad = jnp.concatenate(
            [wi[:, :sb0], zpad, _swap_halves(w_kpe), zpad, w_sbq, wi[:, sb0 + SB_OUT:sb0 + 2 * SB_OUT]],
            axis=1).astype(BF16)
        w_sbv_t = wi[:, sb0 + 2 * SB_OUT:].T.astype(BF16)
        wq = w_q_b[l].reshape(MLA_Q_RANK, MLA_HEADS, MLA_QK) * (MLA_QK ** -0.5 * LOG2E)
        wq_pe = wq[:, :, MLA_NOPE:]
        zq = jnp.zeros((MLA_Q_RANK, MLA_HEADS, MLA_PAD - MLA_QK), F32)
        wqa = jnp.concatenate([wq, zq], axis=-1).reshape(MLA_Q_RANK, MLA_HEADS * MLA_PAD).astype(BF16)
        wqb = jnp.concatenate([_swap_halves(wq_pe), zq], axis=-1).reshape(MLA_Q_RANK, MLA_HEADS * LANES).astype(BF16)
        wkv = w_kv_b[l].reshape(MLA_KV_RANK, MLA_HEADS, MLA_NOPE + MLA_V)
        wk = wkv[:, :, :MLA_NOPE].reshape(MLA_KV_RANK, MLA_OUT).astype(BF16)
        wvt = wkv[:, :, MLA_NOPE:].reshape(MLA_KV_RANK, MLA_OUT).T.astype(BF16)
        w_router = jnp.concatenate(
            [w_expert_router[l], w_group[l], jnp.zeros((d, LANES - N_EXPERTS - N_GROUPS), F32)], axis=1)
        wr_hi = w_router.astype(BF16)
        wr_lo = (w_router - wr_hi.astype(F32)).astype(BF16)
        b_router = jnp.concatenate(
            [b_expert[l].astype(F32), b_group[l].astype(F32), jnp.zeros((LANES - N_EXPERTS - N_GROUPS,), F32)]
        ).reshape(1, LANES)

        lat, sb_qk, sb_vt = _proj_in(xt, _row(norm_mix[l]), w_in_pad, w_sbv_t)
        q, k, vt = _mla_proj(lat, pos, invf, _row(norm_q_lat[l]), _row(norm_kv_lat[l]), wqa, wqb, wk, wvt)
        o_mla = _mla_attn(q, k, vt, batch, seq)
        o_sb = _sb_attn(sb_qk, sb_vt, batch, seq)
        x1 = _out_proj(xt, o_mla, o_sb, _row(norm_mla_out[l]), _row(norm_sb_out[l]), w_out[l].astype(BF16))

        kv = _mem_kv(mem.reshape(batch * MEM_LEN, d), _row(norm_mem_src[l]), w_mem_kv[l].astype(BF16))
        x2, h2, route = _mem_route(x1, _row(norm_mem_x[l]), w_mem_q[l].astype(BF16), kv,
                                   w_mem_o[l].astype(BF16), _row(norm_ffn[l]), wr_hi, wr_lo, b_router, seq)

        slots, meta = _slots(route)
        xs = _dispatch(slots[0], slots[1], meta[2], meta[3], h2, n_slots)
        ys = _experts(meta[0], meta[1], xs, w_gate[l], w_up[l], w_down[l])
        xt = _combine(slots[0], slots[1], x2, route, _row(norm_final), ys)
    return xt.reshape(batch, seq, d)
```

```python
import functools

import jax
import jax.numpy as jnp
from jax import lax
from jax.experimental import pallas as pl
from jax.experimental.pallas import tpu as pltpu

F32 = jnp.float32
BF16 = jnp.bfloat16

EPS = 1e-6
ROPE_THETA = 10000.0

D_MODEL = 2048
MEM_LEN = 256
MLA_HEADS = 8
MLA_NOPE = 128
MLA_ROPE = 64
MLA_QK = MLA_NOPE + MLA_ROPE
MLA_V = 128
MLA_Q_RANK = 512
MLA_KV_RANK = 256
MLA_PAD = 256
SB_HEADS = 8
SB_DIM = 128
MLA_OUT = MLA_HEADS * MLA_V
SB_OUT = SB_HEADS * SB_DIM
MEM_HEADS = 4
MEM_DIM = 128
N_GROUPS = 4
EXPERTS_PER_GROUP = 8
N_EXPERTS = N_GROUPS * EXPERTS_PER_GROUP
D_EXPERT = 512

LANES = 128
LAT_COLS = 1024
IN_COLS_PAD = LAT_COLS + 3 * SB_OUT

SLOT_TILE = 256
VMEM_LIMIT = 56 * 1024 * 1024


def _rms(x, g):
    return x * lax.rsqrt(jnp.mean(x * x, axis=-1, keepdims=True) + EPS) * g


def _dot(a, b):
    return jnp.dot(a, b, preferred_element_type=F32)


def _dot_nt(a, b):
    return lax.dot_general(a, b, (((1,), (1,)), ((), ())), preferred_element_type=F32)


def _split_bf16(x):
    hi = x.astype(BF16)
    lo = (x - hi.astype(F32)).astype(BF16)
    return hi, lo


def _params(*sem):
    return pltpu.CompilerParams(dimension_semantics=sem, vmem_limit_bytes=VMEM_LIMIT)


def _proj_in_kernel(x_ref, g_ref, w_ref, lat_ref, qk_ref, vt_ref, h_ref):
    j = pl.program_id(1)
    last = pl.num_programs(1) - 1

    @pl.when(j == 0)
    def _():
        h_ref[...] = _rms(x_ref[...], g_ref[...]).astype(BF16)
        lat_ref[...] = _dot(h_ref[...], w_ref[...])

    @pl.when((j > 0) & (j < last))
    def _():
        qk_ref[...] = _dot(h_ref[...], w_ref[...]).astype(BF16)

    @pl.when(j == last)
    def _():
        vt_ref[...] = _dot(h_ref[...], w_ref[...]).T.astype(BF16)


def _proj_in(x, g, w_pad, tm=512):
    t = x.shape[0]
    tn = LAT_COLS
    n_w = w_pad.shape[1] // tn
    return pl.pallas_call(
        _proj_in_kernel,
        grid=(t // tm, n_w),
        in_specs=[
            pl.BlockSpec((tm, D_MODEL), lambda i, j: (i, 0)),
            pl.BlockSpec((1, D_MODEL), lambda i, j: (0, 0)),
            pl.BlockSpec((D_MODEL, tn), lambda i, j: (0, j)),
        ],
        out_specs=[
            pl.BlockSpec((tm, tn), lambda i, j: (i, 0)),
            pl.BlockSpec((tm, tn), lambda i, j: (i, jnp.clip(j - 1, 0, n_w - 3))),
            pl.BlockSpec((SB_OUT, tm), lambda i, j: (0, i)),
        ],
        out_shape=[
            jax.ShapeDtypeStruct((t, LAT_COLS), F32),
            jax.ShapeDtypeStruct((t, 2 * SB_OUT), BF16),
            jax.ShapeDtypeStruct((SB_OUT, t), BF16),
        ],
        scratch_shapes=[pltpu.VMEM((tm, D_MODEL), BF16)],
        compiler_params=_params("parallel", "arbitrary"),
        name="proj_in",
    )(x, g, w_pad)


def _mla_proj_kernel(lat_ref, pos_ref, invf_ref, gq_ref, gkv_ref, wqa_ref, wqb_ref, wk_ref, wvt_ref,
                     q_ref, k_ref, vt_ref):
    cq = _rms(lat_ref[:, :MLA_Q_RANK], gq_ref[...]).astype(BF16)
    ckv = _rms(lat_ref[:, MLA_Q_RANK:MLA_Q_RANK + MLA_KV_RANK], gkv_ref[...]).astype(BF16)
    ang = pos_ref[...].astype(F32) * invf_ref[...]
    live = lax.broadcasted_iota(jnp.int32, ang.shape, 1) < MLA_ROPE
    cos2 = jnp.where(live, jnp.cos(ang), 0.0)
    sin2 = jnp.where(live, jnp.sin(ang), 0.0)

    qa = _dot(cq, wqa_ref[...])
    qb = _dot(cq, wqb_ref[...])
    kn = _dot(ckv, wk_ref[...])
    vt_ref[...] = _dot_nt(wvt_ref[...], ckv).astype(BF16)
    k_pe = (lat_ref[:, 768:896] * cos2 + lat_ref[:, 896:1024] * sin2).astype(BF16)
    for h in range(MLA_HEADS):
        lo = h * MLA_PAD
        mid = lo + LANES
        q_ref[:, lo:mid] = qa[:, lo:mid].astype(BF16)
        q_ref[:, mid:mid + LANES] = (qa[:, mid:mid + LANES] * cos2
                                     + qb[:, h * LANES:(h + 1) * LANES] * sin2).astype(BF16)
        k_ref[:, lo:mid] = kn[:, h * LANES:(h + 1) * LANES].astype(BF16)
        k_ref[:, mid:mid + LANES] = k_pe


def _mla_proj(lat, pos, invf, gq, gkv, wqa, wqb, wk, wvt, tm=512):
    t = lat.shape[0]
    full = lambda a: pl.BlockSpec(a.shape, lambda i: (0, 0))
    return pl.pallas_call(
        _mla_proj_kernel,
        grid=(t // tm,),
        in_specs=[
            pl.BlockSpec((tm, LAT_COLS), lambda i: (i, 0)),
            pl.BlockSpec((tm, 1), lambda i: (i, 0)),
            full(invf), full(gq), full(gkv), full(wqa), full(wqb), full(wk), full(wvt),
        ],
        out_specs=[
            pl.BlockSpec((tm, MLA_HEADS * MLA_PAD), lambda i: (i, 0)),
            pl.BlockSpec((tm, MLA_HEADS * MLA_PAD), lambda i: (i, 0)),
            pl.BlockSpec((MLA_OUT, tm), lambda i: (0, i)),
        ],
        out_shape=[
            jax.ShapeDtypeStruct((t, MLA_HEADS * MLA_PAD), BF16),
            jax.ShapeDtypeStruct((t, MLA_HEADS * MLA_PAD), BF16),
            jax.ShapeDtypeStruct((MLA_OUT, t), BF16),
        ],
        compiler_params=_params("parallel"),
        name="mla_proj",
    )(lat, pos, invf, gq, gkv, wqa, wqb, wk, wvt)


LOG2E = 1.4426950408889634


def _three_stage(n_pairs, stage_a, stage_b, stage_c):
    a_issue, a_finish = stage_a
    b_issue, b_finish = stage_b
    c_issue, c_finish = stage_c

    def run_a(n, slot, first):
        a_finish(n, slot, a_issue(n, slot, first), first)

    run_a(0, 0, True)
    run_a(1, 1, True)
    b_finish(0, 0, b_issue(0, 0))

    def half(na, sa, nb, sb, nc, sc):
        ra = a_issue(na, sa, False)
        rb = b_issue(nb, sb)
        rc = c_issue(nc, sc)
        a_finish(na, sa, ra, False)
        b_finish(nb, sb, rb)
        c_finish(nc, sc, rc)

    def body(p, carry):
        half(2 * p, 0, 2 * p - 1, 1, 2 * p - 2, 0)
        half(2 * p + 1, 1, 2 * p, 0, 2 * p - 1, 1)
        return carry

    lax.fori_loop(1, n_pairs + 1, body, 0)
    last = 2 * n_pairs + 1
    rb = b_issue(last, 1)
    rc = c_issue(last - 1, 0)
    b_finish(last, 1, rb)
    c_finish(last - 1, 0, rc)
    c_finish(last, 1, c_issue(last, 1))


def _two_stage(n_pairs, produce, consume):
    produce(0, 0, True)
    produce(1, 1, True)
    consume(0, 0)

    def body(p, carry):
        produce(2 * p, 0, False)
        consume(2 * p - 1, 1)
        produce(2 * p + 1, 1, False)
        consume(2 * p, 0)
        return carry

    lax.fori_loop(1, n_pairs + 1, body, 0)
    consume(2 * n_pairs + 1, 1)


MLA_SUM_ROWS = 16
MLA_GROUP = 4


def _mla_attn_kernel(q_ref, k_ref, vt_ref, o_ref, s_ref, m_ref, acc_ref, *, tq, tk):
    i = pl.program_id(2)
    heads = range(MLA_GROUP)
    dq, dv = MLA_PAD, MLA_V
    m_ref[...] = jnp.full(m_ref.shape, -jnp.inf, F32)
    acc_ref[...] = jnp.zeros(acc_ref.shape, F32)
    ones = jnp.ones((MLA_SUM_ROWS, tk), BF16)

    def key_start(n):
        tile = jnp.where(n < 2, 2 * i + n, 2 * i + 1 - n)
        return pl.multiple_of(tile * tk, tk)

    def produce(n, slot, diagonal):
        ks = key_start(n)
        for g in heads:
            st = _dot_nt(k_ref[pl.ds(ks, tk), g * dq:(g + 1) * dq], q_ref[:, g * dq:(g + 1) * dq])
            if diagonal:
                kpos = ks + lax.broadcasted_iota(jnp.int32, st.shape, 0)
                qpos = i * tq + lax.broadcasted_iota(jnp.int32, st.shape, 1)
                st = jnp.where(kpos <= qpos, st, -jnp.inf)
            s_ref[g, slot] = st

    def consume(n, slot):
        ks = key_start(n)
        for g in heads:
            st = s_ref[g, slot]
            m_old = m_ref[g]
            m_new = jnp.maximum(m_old, jnp.max(st, axis=0, keepdims=True))
            alpha = jnp.exp2(m_old - m_new)
            p = jnp.exp2(st - m_new).astype(BF16)
            v_ones = jnp.concatenate([vt_ref[g * dv:(g + 1) * dv, pl.ds(ks, tk)], ones], axis=0)
            acc_ref[g] = alpha * acc_ref[g] + _dot(v_ones, p)
            m_ref[g] = m_new

    _two_stage(i, produce, consume)
    for g in heads:
        o_ref[:, g * dv:(g + 1) * dv] = (acc_ref[g, :dv, :] / acc_ref[g, dv:dv + 1, :]).T


def _mla_attn(q, k, vt, batch, seq, tq=512):
    tk = tq // 2
    nq = seq // tq
    groups = MLA_HEADS // MLA_GROUP
    kern = functools.partial(_mla_attn_kernel, tq=tq, tk=tk)
    return pl.pallas_call(
        kern,
        grid=(batch, groups, nq),
        in_specs=[
            pl.BlockSpec((tq, MLA_GROUP * MLA_PAD), lambda b, h, i: (b * nq + i, h)),
            pl.BlockSpec((seq, MLA_GROUP * MLA_PAD), lambda b, h, i: (b, h)),
            pl.BlockSpec((MLA_GROUP * MLA_V, seq), lambda b, h, i: (h, b)),
        ],
        out_specs=pl.BlockSpec((tq, MLA_GROUP * MLA_V), lambda b, h, i: (b * nq + i, h)),
        out_shape=jax.ShapeDtypeStruct((batch * seq, MLA_OUT), F32),
        scratch_shapes=[
            pltpu.VMEM((MLA_GROUP, 2, tk, tq), F32),
            pltpu.VMEM((MLA_GROUP, 1, tq), F32),
            pltpu.VMEM((MLA_GROUP, MLA_V + MLA_SUM_ROWS, tq), F32),
        ],
        compiler_params=_params("parallel", "parallel", "arbitrary"),
        name="mla_attn",
    )(q, k, vt)


SB_EXP_CLAMP = 126.0
SB_GROUP = 4


def _sb_attn_kernel(q_ref, k_ref, vt_ref, o_ref, z_ref, hl_ref, arg_ref, acc_ref, c_ref, *, tq, tk):
    i = pl.program_id(2)
    heads = range(SB_GROUP)
    d = SB_DIM
    acc_ref[...] = jnp.zeros(acc_ref.shape, F32)
    c_ref[...] = jnp.zeros(c_ref.shape, F32)
    col = lax.broadcasted_iota(jnp.int32, (tk, tk), 1)
    row = lax.broadcasted_iota(jnp.int32, (tk, tk), 0)
    neg_tri = jnp.where(col >= row, -1.0, 0.0).astype(BF16)

    def key_start(n):
        return pl.multiple_of((2 * i + 1 - n) * tk, tk)

    def a_issue(n, slot, masked):
        ks = key_start(n)
        return [_dot_nt(k_ref[pl.ds(ks, tk), g * d:(g + 1) * d], q_ref[:, g * d:(g + 1) * d])
                for g in heads]

    def a_finish(n, slot, z2s, masked):
        for g in heads:
            z2 = z2s[g]
            sp = jnp.maximum(z2, jnp.log2(1.0 + jnp.exp2(jnp.minimum(z2, SB_EXP_CLAMP))))
            if masked:
                kpos = key_start(n) + lax.broadcasted_iota(jnp.int32, z2.shape, 0)
                qpos = i * tq + lax.broadcasted_iota(jnp.int32, z2.shape, 1)
                mask = kpos < qpos
                sp = jnp.where(mask, sp, 0.0)
                z2 = jnp.where(mask, z2, -jnp.inf)
            hl_ref[g, slot] = sp.astype(BF16)
            z_ref[g, slot] = z2

    def b_issue(n, slot):
        return [_dot(neg_tri, hl_ref[g, slot]) for g in heads]

    def b_finish(n, slot, laters):
        for g in heads:
            arg_ref[g, slot] = z_ref[g, slot] + laters[g] + c_ref[g]
            c_ref[g] += laters[g][0:1, :]

    def c_issue(n, slot):
        ks = key_start(n)
        return [_dot(vt_ref[g * d:(g + 1) * d, pl.ds(ks, tk)], jnp.exp2(arg_ref[g, slot]).astype(BF16))
                for g in heads]

    def c_finish(n, slot, pvs):
        for g in heads:
            acc_ref[g] += pvs[g]

    _three_stage(i, (a_issue, a_finish), (b_issue, b_finish), (c_issue, c_finish))
    for g in heads:
        o_ref[:, g * d:(g + 1) * d] = acc_ref[g].T


def _sb_attn(qk, vt, batch, seq, tq=512):
    tk = tq // 2
    nq = seq // tq
    groups = SB_HEADS // SB_GROUP
    gd = SB_GROUP * SB_DIM
    kern = functools.partial(_sb_attn_kernel, tq=tq, tk=tk)
    return pl.pallas_call(
        kern,
        grid=(batch, groups, nq),
        in_specs=[
            pl.BlockSpec((tq, gd), lambda b, h, i: (b * nq + i, h)),
            pl.BlockSpec((seq, gd), lambda b, h, i: (b, groups + h)),
            pl.BlockSpec((gd, seq), lambda b, h, i: (h, b)),
        ],
        out_specs=pl.BlockSpec((tq, gd), lambda b, h, i: (b * nq + i, h)),
        out_shape=jax.ShapeDtypeStruct((batch * seq, SB_OUT), F32),
        scratch_shapes=[
            pltpu.VMEM((SB_GROUP, 2, tk, tq), F32),
            pltpu.VMEM((SB_GROUP, 2, tk, tq), BF16),
            pltpu.VMEM((SB_GROUP, 2, tk, tq), F32),
            pltpu.VMEM((SB_GROUP, SB_DIM, tq), F32),
            pltpu.VMEM((SB_GROUP, 1, tq), F32),
        ],
        compiler_params=_params("parallel", "parallel", "arbitrary"),
        name="sb_attn",
    )(qk, qk, vt)


def _out_proj_kernel(x_ref, oa_ref, ob_ref, ga_ref, gb_ref, w_ref, y_ref):
    na = _rms(oa_ref[...], ga_ref[...]).astype(BF16)
    nb = _rms(ob_ref[...], gb_ref[...]).astype(BF16)
    y_ref[...] = x_ref[...] + _dot(na, w_ref[:MLA_OUT, :]) + _dot(nb, w_ref[MLA_OUT:, :])


def _out_proj(x, oa, ob, ga, gb, w, tm=512):
    t = x.shape[0]
    full = lambda a: pl.BlockSpec(a.shape, lambda i: (0, 0))
    return pl.pallas_call(
        _out_proj_kernel,
        grid=(t // tm,),
        in_specs=[
            pl.BlockSpec((tm, D_MODEL), lambda i: (i, 0)),
            pl.BlockSpec((tm, MLA_OUT), lambda i: (i, 0)),
            pl.BlockSpec((tm, SB_OUT), lambda i: (i, 0)),
            full(ga), full(gb), full(w),
        ],
        out_specs=pl.BlockSpec((tm, D_MODEL), lambda i: (i, 0)),
        out_shape=jax.ShapeDtypeStruct((t, D_MODEL), F32),
        compiler_params=_params("parallel"),
        name="out_proj",
    )(x, oa, ob, ga, gb, w)


def _mem_kv_kernel(mem_ref, g_ref, w_ref, kv_ref):
    kv_ref[...] = _dot(_rms(mem_ref[...], g_ref[...]).astype(BF16), w_ref[...]).astype(BF16)


def _mem_kv(mem, g, w):
    m = mem.shape[0]
    n = w.shape[1]
    full = lambda a: pl.BlockSpec(a.shape, lambda i: (0, 0))
    return pl.pallas_call(
        _mem_kv_kernel,
        grid=(1,),
        in_specs=[full(mem), full(g), full(w)],
        out_specs=pl.BlockSpec((m, n), lambda i: (0, 0)),
        out_shape=jax.ShapeDtypeStruct((m, n), BF16),
        compiler_params=_params("arbitrary"),
        name="mem_kv",
    )(mem, g, w)


GROUP_LANE0 = N_EXPERTS


def _mem_route_kernel(x_ref, gx_ref, wq_ref, kv_ref, wo_ref, gf_ref, wrh_ref, wrl_ref, br_ref,
                      x2_ref, h2_ref, route_ref):
    x1 = x_ref[...]
    q = _dot(_rms(x1, gx_ref[...]).astype(BF16), wq_ref[...]).astype(BF16)
    scale = MEM_DIM ** -0.5
    kw = MEM_HEADS * MEM_DIM
    heads = []
    for h in range(MEM_HEADS):
        lo = h * MEM_DIM
        s = _dot_nt(q[:, lo:lo + MEM_DIM], kv_ref[:, lo:lo + MEM_DIM]) * scale
        e = jnp.exp(s - jnp.max(s, axis=-1, keepdims=True))
        p = (e / jnp.sum(e, axis=-1, keepdims=True)).astype(BF16)
        heads.append(_dot(p, kv_ref[:, kw + lo:kw + lo + MEM_DIM]).astype(BF16))
    o = jnp.concatenate(heads, axis=-1)
    x2 = x1 + _dot(o, wo_ref[...])
    x2_ref[...] = x2
    h2 = _rms(x2, gf_ref[...])
    h2_ref[...] = h2

    hh, hl = _split_bf16(h2)
    lg = _dot(hh, wrh_ref[...]) + _dot(hl, wrh_ref[...]) + _dot(hh, wrl_ref[...]) + br_ref[...]
    lane = lax.broadcasted_iota(jnp.int32, lg.shape, 1)
    big = jnp.int32(1 << 20)
    ninf = -jnp.inf

    def lane_max(v):
        return jnp.max(v, axis=-1, keepdims=True)

    def first_lane(cond):
        return jnp.min(jnp.where(cond, lane, big), axis=-1, keepdims=True)

    is_g = (lane >= GROUP_LANE0) & (lane < GROUP_LANE0 + N_GROUPS)
    g_max = lane_max(jnp.where(is_g, lg, ninf))
    g_sum = jnp.sum(jnp.where(is_g, jnp.exp(lg - g_max), 0.0), axis=-1, keepdims=True)
    p_g = 1.0 / g_sum
    g_idx = first_lane(is_g & (lg == g_max)) - GROUP_LANE0
    in_grp = (lane < N_EXPERTS) & ((lane // EXPERTS_PER_GROUP) == g_idx)
    e_max = lane_max(jnp.where(in_grp, lg, ninf))
    e_sum = jnp.sum(jnp.where(in_grp, jnp.exp(lg - e_max), 0.0), axis=-1, keepdims=True)
    i1 = first_lane(in_grp & (lg == e_max))
    rest = in_grp & (lane != i1)
    e_max2 = lane_max(jnp.where(rest, lg, ninf))
    i2 = first_lane(rest & (lg == e_max2))
    p1 = 1.0 / e_sum
    p2 = jnp.exp(e_max2 - e_max) / e_sum
    den = p1 + p2
    gate1 = p_g * (p1 / den)
    gate2 = p_g * (p2 / den)
    route = jnp.where(lane == 0, i1.astype(F32),
                      jnp.where(lane == 1, i2.astype(F32),
                                jnp.where(lane == 2, gate1,
                                          jnp.where(lane == 3, gate2, 0.0))))
    route_ref[...] = route


def _mem_route(x1, gx, wq, kv, wo, gf, wrh, wrl, br, seq, tm=512):
    t = x1.shape[0]
    per_batch = seq // tm
    full = lambda a: pl.BlockSpec(a.shape, lambda i: (0, 0))
    return pl.pallas_call(
        _mem_route_kernel,
        grid=(t // tm,),
        in_specs=[
            pl.BlockSpec((tm, D_MODEL), lambda i: (i, 0)),
            full(gx), full(wq),
            pl.BlockSpec((MEM_LEN, kv.shape[1]), lambda i: (i // per_batch, 0)),
            full(wo), full(gf), full(wrh), full(wrl), full(br),
        ],
        out_specs=[
            pl.BlockSpec((tm, D_MODEL), lambda i: (i, 0)),
            pl.BlockSpec((tm, D_MODEL), lambda i: (i, 0)),
            pl.BlockSpec((tm, LANES), lambda i: (i, 0)),
        ],
        out_shape=[
            jax.ShapeDtypeStruct((t, D_MODEL), F32),
            jax.ShapeDtypeStruct((t, D_MODEL), F32),
            jax.ShapeDtypeStruct((t, LANES), F32),
        ],
        compiler_params=_params("parallel"),
        name="mem_route",
    )(x1, gx, wq, kv, wo, gf, wrh, wrl, br)


SLOT_BLK = 256


def _slots_kernel(route_ref, slot_ref, meta_ref, cum_ref, *, n_tok):
    nblk = n_tok // SLOT_BLK
    e_iota = lax.broadcasted_iota(jnp.int32, (LANES, SLOT_BLK), 0)
    incl = (lax.broadcasted_iota(jnp.int32, (SLOT_BLK, SLOT_BLK), 0)
            <= lax.broadcasted_iota(jnp.int32, (SLOT_BLK, SLOT_BLK), 1)).astype(BF16)

    def onehots(b):
        ts = pl.multiple_of(b * SLOT_BLK, SLOT_BLK)
        ids = route_ref[pl.ds(ts, SLOT_BLK), :].T
        oh1 = e_iota == ids[0:1, :].astype(jnp.int32)
        oh2 = e_iota == ids[1:2, :].astype(jnp.int32)
        return ts, oh1, oh2

    def count(b, carry):
        ts, oh1, oh2 = onehots(b)
        cnt = (oh1 | oh2).astype(F32).astype(BF16)
        c = _dot(cnt, incl) + carry
        cum_ref[:, pl.ds(ts, SLOT_BLK)] = c
        return c[:, SLOT_BLK - 1:SLOT_BLK]

    total = lax.fori_loop(0, nblk, count, jnp.zeros((LANES, 1), F32))
    tiles = jnp.floor((total + (SLOT_TILE - 1)) * (1.0 / SLOT_TILE))
    below = (lax.broadcasted_iota(jnp.int32, (LANES, LANES), 1)
             < lax.broadcasted_iota(jnp.int32, (LANES, LANES), 0)).astype(BF16)
    tile_lo = _dot(below, jnp.broadcast_to(tiles, (LANES, LANES)).astype(BF16))
    base = tile_lo[:, 0:1] * SLOT_TILE

    def assign(b, carry):
        ts, oh1, oh2 = onehots(b)
        pos = base + cum_ref[:, pl.ds(ts, SLOT_BLK)] - 1.0
        s1 = jnp.sum(jnp.where(oh1, pos, 0.0), axis=0, keepdims=True)
        s2 = jnp.sum(jnp.where(oh2, pos, 0.0), axis=0, keepdims=True)
        row = lax.broadcasted_iota(jnp.int32, (8, SLOT_BLK), 0)
        slot_ref[:, pl.ds(ts, SLOT_BLK)] = jnp.where(row == 0, s1, jnp.where(row == 1, s2, 0.0)).astype(jnp.int32)
        return carry

    lax.fori_loop(0, nblk, assign, 0)

    tile_hi = tile_lo + tiles
    tix = lax.broadcasted_iota(jnp.int32, (LANES, LANES), 1).astype(F32)
    is_e = lax.broadcasted_iota(jnp.int32, (LANES, LANES), 0) < N_EXPERTS
    owner = jnp.sum(jnp.where(is_e & (tile_hi <= tix), 1.0, 0.0), axis=0, keepdims=True)
    n_tiles = jnp.max(jnp.where(is_e, tile_hi, 0.0), axis=0, keepdims=True)
    valid = tix[0:1, :] < n_tiles
    last_owner = jnp.max(jnp.where(valid, owner, 0.0), axis=1, keepdims=True)
    owner = jnp.where(valid, owner, last_owner)
    eye = (lax.broadcasted_iota(jnp.int32, (LANES, LANES), 0)
           == lax.broadcasted_iota(jnp.int32, (LANES, LANES), 1))
    to_lanes = lambda colvec: jnp.sum(jnp.where(eye, colvec, 0.0), axis=0, keepdims=True)
    pad_first = to_lanes(base + total)
    pad_end = to_lanes(base + tiles * SLOT_TILE)
    row = lax.broadcasted_iota(jnp.int32, (8, LANES), 0)
    meta = jnp.where(row == 0, owner,
                     jnp.where(row == 1, valid.astype(F32),
                               jnp.where(row == 2, pad_first, jnp.where(row == 3, pad_end, 0.0))))
    meta_ref[...] = meta.astype(jnp.int32)


def _slots(route):
    t = route.shape[0]
    kern = functools.partial(_slots_kernel, n_tok=t)
    return pl.pallas_call(
        kern,
        grid=(1,),
        in_specs=[pl.BlockSpec(route.shape, lambda i: (0, 0))],
        out_specs=[
            pl.BlockSpec((8, t), lambda i: (0, 0)),
            pl.BlockSpec((8, LANES), lambda i: (0, 0)),
        ],
        out_shape=[
            jax.ShapeDtypeStruct((8, t), jnp.int32),
            jax.ShapeDtypeStruct((8, LANES), jnp.int32),
        ],
        scratch_shapes=[pltpu.VMEM((LANES, t), F32)],
        compiler_params=_params("arbitrary"),
        name="slots",
    )(route)


def _row_copy(src_ref, src_row, dst_ref, dst_row, sem):
    return pltpu.make_async_copy(src_ref.at[pl.ds(src_row, 1), :], dst_ref.at[pl.ds(dst_row, 1), :], sem)


def _dispatch_kernel(s1_ref, s2_ref, pad_first_ref, pad_end_ref, h_ref, xs_ref, zero_ref, sem, pad_sem, *, tm):
    i = pl.program_id(0)
    t0 = i * tm

    def copies(r):
        return (_row_copy(h_ref, r, xs_ref, s1_ref[t0 + r], sem),
                _row_copy(h_ref, r, xs_ref, s2_ref[t0 + r], sem))

    def issue(r, c):
        first, second = copies(r)
        first.start(priority=0)
        second.start(priority=1)
        return c

    lax.fori_loop(0, tm, issue, 0, unroll=8)

    @pl.when(i == pl.num_programs(0) - 1)
    def _():
        zero_ref[...] = jnp.zeros(zero_ref.shape, F32)

        def pad_copy(slot):
            return _row_copy(zero_ref, 0, xs_ref, slot, pad_sem)

        def fill(e, c):
            def one(s, cc):
                pad_copy(s).start()
                return cc

            return lax.fori_loop(pad_first_ref[e], pad_end_ref[e], one, c)

        lax.fori_loop(0, N_EXPERTS, fill, 0)

        def settle(e, c):
            def one(s, cc):
                pad_copy(s).wait()
                return cc

            return lax.fori_loop(pad_first_ref[e], pad_end_ref[e], one, c)

        lax.fori_loop(0, N_EXPERTS, settle, 0)

    def drain(r, c):
        for cp in copies(r):
            cp.wait()
        return c

    lax.fori_loop(0, tm, drain, 0, unroll=8)


def _dispatch(slot1, slot2, pad_first, pad_end, h2, n_slots, tm=256):
    t = h2.shape[0]
    kern = functools.partial(_dispatch_kernel, tm=tm)
    return pl.pallas_call(
        kern,
        grid_spec=pltpu.PrefetchScalarGridSpec(
            num_scalar_prefetch=4,
            grid=(t // tm,),
            in_specs=[pl.BlockSpec((tm, D_MODEL), lambda i, *_: (i, 0))],
            out_specs=pl.BlockSpec(memory_space=pl.ANY),
            scratch_shapes=[
                pltpu.VMEM((8, D_MODEL), F32),
                pltpu.SemaphoreType.DMA(()),
                pltpu.SemaphoreType.DMA(()),
            ],
        ),
        out_shape=jax.ShapeDtypeStruct((n_slots, D_MODEL), F32),
        compiler_params=_params("arbitrary"),
        name="dispatch",
    )(slot1, slot2, pad_first, pad_end, h2)


def _experts_kernel(own_ref, valid_ref, xs_ref, wg_hbm, wu_hbm, wd_hbm, ys_ref,
                    wgf, wuf, wdf, wgb, wub, wdb, wsem, wslot_ref):
    i = pl.program_id(0)
    nt = pl.num_programs(0)
    at = lambda ref, j: ref[jnp.minimum(j, nt - 1)]

    def weight_copies(e, s):
        return (pltpu.make_async_copy(wg_hbm.at[e], wgf.at[s], wsem.at[s]),
                pltpu.make_async_copy(wu_hbm.at[e], wuf.at[s], wsem.at[s]),
                pltpu.make_async_copy(wd_hbm.at[e], wdf.at[s], wsem.at[s]))

    @pl.when(i == 0)
    def _():
        wslot_ref[0] = 0
        for cp in weight_copies(own_ref[0], 0):
            cp.start()

    @pl.when(valid_ref[i] > 0)
    def _():
        e = own_ref[i]

        @pl.when((i == 0) | (own_ref[jnp.maximum(i - 1, 0)] != e))
        def _():
            s = wslot_ref[0]
            for cp in weight_copies(e, s):
                cp.wait()
            j = lax.while_loop(lambda j: (j < nt) & (at(own_ref, j) == e), lambda j: j + 1, i + 1)

            @pl.when((j < nt) & (at(valid_ref, j) > 0))
            def _():
                for cp in weight_copies(at(own_ref, j), 1 - s):
                    cp.start()

            wgb[...] = wgf[s].astype(BF16)
            wub[...] = wuf[s].astype(BF16)
            wdb[...] = wdf[s].astype(BF16)
            wslot_ref[0] = 1 - s

        x = xs_ref[...].astype(BF16)
        a = _dot(x, wgb[...])
        u = _dot(x, wub[...])
        act = (a * jax.nn.sigmoid(a) * u).astype(BF16)
        ys_ref[...] = _dot(act, wdb[...])

    @pl.when(valid_ref[i] == 0)
    def _():
        ys_ref[...] = jnp.zeros(ys_ref.shape, F32)


def _experts(own, valid, xs, wg, wu, wd):
    n_tiles = xs.shape[0] // SLOT_TILE
    hbm = pl.BlockSpec(memory_space=pl.ANY)
    tile = (SLOT_TILE, D_MODEL)
    return pl.pallas_call(
        _experts_kernel,
        grid_spec=pltpu.PrefetchScalarGridSpec(
            num_scalar_prefetch=2,
            grid=(n_tiles,),
            in_specs=[
                pl.BlockSpec(tile, lambda i, o, v: (jnp.where(v[i] > 0, i, 0), 0)),
                hbm, hbm, hbm,
            ],
            out_specs=pl.BlockSpec(tile, lambda i, o, v: (i, 0)),
            scratch_shapes=[
                pltpu.VMEM((2, D_MODEL, D_EXPERT), F32),
                pltpu.VMEM((2, D_MODEL, D_EXPERT), F32),
                pltpu.VMEM((2, D_EXPERT, D_MODEL), F32),
                pltpu.VMEM((D_MODEL, D_EXPERT), BF16),
                pltpu.VMEM((D_MODEL, D_EXPERT), BF16),
                pltpu.VMEM((D_EXPERT, D_MODEL), BF16),
                pltpu.SemaphoreType.DMA((2,)),
                pltpu.SMEM((1,), jnp.int32),
            ],
        ),
        out_shape=jax.ShapeDtypeStruct(xs.shape, F32),
        compiler_params=_params("arbitrary"),
        name="experts",
    )(own, valid, xs, wg, wu, wd)


def _combine_kernel(s1_ref, s2_ref, x_ref, route_ref, g_ref, ys_ref, y_ref, buf_ref, sem, *, tm):
    i = pl.program_id(0)
    n = pl.num_programs(0)

    def copies(tile, p, r):
        t = tile * tm + r
        return (_row_copy(ys_ref, s1_ref[t], buf_ref.at[p, 0], r, sem.at[p]),
                _row_copy(ys_ref, s2_ref[t], buf_ref.at[p, 1], r, sem.at[p]))

    def gather_start(tile, p):
        def issue(r, c):
            first, second = copies(tile, p, r)
            first.start(priority=0)
            second.start(priority=1)
            return c

        lax.fori_loop(0, tm, issue, 0, unroll=8)

    def gather_wait(tile, p):
        def drain(r, c):
            for cp in copies(tile, p, r):
                cp.wait()
            return c

        lax.fori_loop(0, tm, drain, 0, unroll=8)

    @pl.when(i == 0)
    def _():
        gather_start(0, 0)

    p = i % 2

    @pl.when(i + 1 < n)
    def _():
        gather_start(i + 1, 1 - p)

    gather_wait(i, p)
    y = x_ref[...] + route_ref[:, 2:3] * buf_ref[p, 0] + route_ref[:, 3:4] * buf_ref[p, 1]
    y_ref[...] = _rms(y, g_ref[...])


def _combine(slot1, slot2, x2, route, g, ys, tm=256):
    t = x2.shape[0]
    kern = functools.partial(_combine_kernel, tm=tm)
    return pl.pallas_call(
        kern,
        grid_spec=pltpu.PrefetchScalarGridSpec(
            num_scalar_prefetch=2,
            grid=(t // tm,),
            in_specs=[
                pl.BlockSpec((tm, D_MODEL), lambda i, s1, s2: (i, 0)),
                pl.BlockSpec((tm, LANES), lambda i, s1, s2: (i, 0)),
                pl.BlockSpec((1, D_MODEL), lambda i, s1, s2: (0, 0)),
                pl.BlockSpec(memory_space=pl.ANY),
            ],
            out_specs=pl.BlockSpec((tm, D_MODEL), lambda i, s1, s2: (i, 0)),
            scratch_shapes=[
                pltpu.VMEM((2, 2, tm, D_MODEL), F32),
                pltpu.SemaphoreType.DMA((2,)),
            ],
        ),
        out_shape=jax.ShapeDtypeStruct((t, D_MODEL), F32),
        compiler_params=_params("arbitrary"),
        name="combine",
    )(slot1, slot2, x2, route, g, ys)


def _swap_halves(w):
    half = w.shape[-1] // 2
    return jnp.concatenate([-w[..., half:], w[..., :half]], axis=-1)


def _row(v):
    return v.reshape(1, -1).astype(F32)


def kernel(x, mem, positions, norm_mix, w_in, norm_q_lat, w_q_b, norm_kv_lat, w_kv_b, norm_mla_out, norm_sb_out, w_out, norm_mem_x, norm_mem_src, w_mem_q, w_mem_kv, w_mem_o, norm_ffn, w_group, b_group, w_expert_router, b_expert, w_gate, w_up, w_down, norm_final):
    batch, seq, d = x.shape
    t = batch * seq
    depth = w_in.shape[0]
    assert depth == 1, "single-layer trunk only"
    xt = x.reshape(t, d)
    pos = positions.reshape(t, 1)
    inv_freq = ROPE_THETA ** (-jnp.arange(0, MLA_ROPE, 2, dtype=F32) / MLA_ROPE)
    invf = jnp.concatenate([inv_freq, inv_freq, jnp.zeros((LANES - MLA_ROPE,), F32)]).reshape(1, LANES)
    n_slots = (2 * t // SLOT_TILE + N_EXPERTS) * SLOT_TILE

    for l in range(depth):
        wi = w_in[l]
        lat_w = MLA_Q_RANK + MLA_KV_RANK
        w_kpe = wi[:, lat_w:lat_w + MLA_ROPE]
        zpad = jnp.zeros((d, LANES - MLA_ROPE), F32)
        sb0 = lat_w + MLA_ROPE
        w_sbq = wi[:, sb0:sb0 + SB_OUT] * (SB_DIM ** -0.5 * LOG2E)
        w_in_pad = jnp.concatenate(
            [wi[:, :sb0], zpad, _swap_halves(w_kpe), zpad, w_sbq, wi[:, sb0 + SB_OUT:]], axis=1).astype(BF16)
        wq = w_q_b[l].reshape(MLA_Q_RANK, MLA_HEADS, MLA_QK) * (MLA_QK ** -0.5 * LOG2E)
        wq_pe = wq[:, :, MLA_NOPE:]
        zq = jnp.zeros((MLA_Q_RANK, MLA_HEADS, MLA_PAD - MLA_QK), F32)
        wqa = jnp.concatenate([wq, zq], axis=-1).reshape(MLA_Q_RANK, MLA_HEADS * MLA_PAD).astype(BF16)
        wqb = jnp.concatenate([_swap_halves(wq_pe), zq], axis=-1).reshape(MLA_Q_RANK, MLA_HEADS * LANES).astype(BF16)
        wkv = w_kv_b[l].reshape(MLA_KV_RANK, MLA_HEADS, MLA_NOPE + MLA_V)
        wk = wkv[:, :, :MLA_NOPE].reshape(MLA_KV_RANK, MLA_OUT).astype(BF16)
        wvt = wkv[:, :, MLA_NOPE:].reshape(MLA_KV_RANK, MLA_OUT).T.astype(BF16)
        w_router = jnp.concatenate(
            [w_expert_router[l], w_group[l], jnp.zeros((d, LANES - N_EXPERTS - N_GROUPS), F32)], axis=1)
        wr_hi = w_router.astype(BF16)
        wr_lo = (w_router - wr_hi.astype(F32)).astype(BF16)
        b_router = jnp.concatenate(
            [b_expert[l].astype(F32), b_group[l].astype(F32), jnp.zeros((LANES - N_EXPERTS - N_GROUPS,), F32)]
        ).reshape(1, LANES)

        lat, sb_qk, sb_vt = _proj_in(xt, _row(norm_mix[l]), w_in_pad)
        q, k, vt = _mla_proj(lat, pos, invf, _row(norm_q_lat[l]), _row(norm_kv_lat[l]), wqa, wqb, wk, wvt)
        o_mla = _mla_attn(q, k, vt, batch, seq)
        o_sb = _sb_attn(sb_qk, sb_vt, batch, seq)
        x1 = _out_proj(xt, o_mla, o_sb, _row(norm_mla_out[l]), _row(norm_sb_out[l]), w_out[l].astype(BF16))

        kv = _mem_kv(mem.reshape(batch * MEM_LEN, d), _row(norm_mem_src[l]), w_mem_kv[l].astype(BF16))
        x2, h2, route = _mem_route(x1, _row(norm_mem_x[l]), w_mem_q[l].astype(BF16), kv,
                                   w_mem_o[l].astype(BF16), _row(norm_ffn[l]), wr_hi, wr_lo, b_router, seq)

        slots, meta = _slots(route)
        xs = _dispatch(slots[0], slots[1], meta[2], meta[3], h2, n_slots)
        ys = _experts(meta[0], meta[1], xs, w_gate[l], w_up[l], w_down[l])
        xt = _combine(slots[0], slots[1], x2, route, _row(norm_final), ys)
    return xt.reshape(batch, seq, d)
```

```python
import functools

import jax
import jax.numpy as jnp
from jax import lax
from jax.experimental import pallas as pl
from jax.experimental.pallas import tpu as pltpu

F32 = jnp.float32
BF16 = jnp.bfloat16

EPS = 1e-6
ROPE_THETA = 10000.0

D_MODEL = 2048
MEM_LEN = 256
MLA_HEADS = 8
MLA_NOPE = 128
MLA_ROPE = 64
MLA_QK = MLA_NOPE + MLA_ROPE
MLA_V = 128
MLA_Q_RANK = 512
MLA_KV_RANK = 256
MLA_PAD = 256
SB_HEADS = 8
SB_DIM = 128
MLA_OUT = MLA_HEADS * MLA_V
SB_OUT = SB_HEADS * SB_DIM
MEM_HEADS = 4
MEM_DIM = 128
N_GROUPS = 4
EXPERTS_PER_GROUP = 8
N_EXPERTS = N_GROUPS * EXPERTS_PER_GROUP
D_EXPERT = 512

LANES = 128
LAT_COLS = 1024
IN_COLS_PAD = LAT_COLS + 3 * SB_OUT

SLOT_TILE = 256
VMEM_LIMIT = 56 * 1024 * 1024


def _rms(x, g):
    return x * lax.rsqrt(jnp.mean(x * x, axis=-1, keepdims=True) + EPS) * g


def _dot(a, b):
    return jnp.dot(a, b, preferred_element_type=F32)


def _dot_nt(a, b):
    return lax.dot_general(a, b, (((1,), (1,)), ((), ())), preferred_element_type=F32)


def _split_bf16(x):
    hi = x.astype(BF16)
    lo = (x - hi.astype(F32)).astype(BF16)
    return hi, lo


def _params(*sem):
    return pltpu.CompilerParams(dimension_semantics=sem, vmem_limit_bytes=VMEM_LIMIT)


def _proj_in_kernel(x_ref, g_ref, wl_ref, ws_ref, lat_ref, qk_ref, vt_ref, h_ref):
    j = pl.program_id(1)
    last = pl.num_programs(1) - 1

    @pl.when(j == 0)
    def _():
        h_ref[...] = _rms(x_ref[...], g_ref[...]).astype(BF16)
        lat_ref[...] = _dot(h_ref[...], wl_ref[...])

    @pl.when((j > 0) & (j < last))
    def _():
        qk_ref[...] = _dot(h_ref[...], ws_ref[...]).astype(BF16)

    @pl.when(j == last)
    def _():
        vt_ref[...] = _dot(h_ref[...], ws_ref[...]).T.astype(BF16)


def _proj_in(x, g, w_lat, w_sb, tm=512):
    t = x.shape[0]
    tn = LAT_COLS
    n_sb = w_sb.shape[1] // tn
    return pl.pallas_call(
        _proj_in_kernel,
        grid=(t // tm, n_sb + 1),
        in_specs=[
            pl.BlockSpec((tm, D_MODEL), lambda i, j: (i, 0)),
            pl.BlockSpec((1, D_MODEL), lambda i, j: (0, 0)),
            pl.BlockSpec((D_MODEL, tn), lambda i, j: (0, 0)),
            pl.BlockSpec((D_MODEL, tn), lambda i, j: (0, jnp.maximum(j - 1, 0))),
        ],
        out_specs=[
            pl.BlockSpec((tm, tn), lambda i, j: (i, 0)),
            pl.BlockSpec((tm, tn), lambda i, j: (i, jnp.clip(j - 1, 0, n_sb - 2))),
            pl.BlockSpec((SB_OUT, tm), lambda i, j: (0, i)),
        ],
        out_shape=[
            jax.ShapeDtypeStruct((t, LAT_COLS), F32),
            jax.ShapeDtypeStruct((t, 2 * SB_OUT), BF16),
            jax.ShapeDtypeStruct((SB_OUT, t), BF16),
        ],
        scratch_shapes=[pltpu.VMEM((tm, D_MODEL), BF16)],
        compiler_params=_params("parallel", "arbitrary"),
        name="proj_in",
    )(x, g, w_lat, w_sb)


def _mla_proj_kernel(lat_ref, pos_ref, invf_ref, gq_ref, gkv_ref, wqa_ref, wqb_ref, wk_ref, wvt_ref,
                     q_ref, k_ref, vt_ref):
    cq = _rms(lat_ref[:, :MLA_Q_RANK], gq_ref[...]).astype(BF16)
    ckv = _rms(lat_ref[:, MLA_Q_RANK:MLA_Q_RANK + MLA_KV_RANK], gkv_ref[...]).astype(BF16)
    ang = pos_ref[...].astype(F32) * invf_ref[...]
    live = lax.broadcasted_iota(jnp.int32, ang.shape, 1) < MLA_ROPE
    cos2 = jnp.where(live, jnp.cos(ang), 0.0)
    sin2 = jnp.where(live, jnp.sin(ang), 0.0)

    qa = _dot(cq, wqa_ref[...])
    qb = _dot(cq, wqb_ref[...])
    kn = _dot(ckv, wk_ref[...])
    vt_ref[...] = _dot_nt(wvt_ref[...], ckv).astype(BF16)
    k_pe = (lat_ref[:, 768:896] * cos2 + lat_ref[:, 896:1024] * sin2).astype(BF16)
    for h in range(MLA_HEADS):
        lo = h * MLA_PAD
        mid = lo + LANES
        q_ref[:, lo:mid] = qa[:, lo:mid].astype(BF16)
        q_ref[:, mid:mid + LANES] = (qa[:, mid:mid + LANES] * cos2
                                     + qb[:, h * LANES:(h + 1) * LANES] * sin2).astype(BF16)
        k_ref[:, lo:mid] = kn[:, h * LANES:(h + 1) * LANES].astype(BF16)
        k_ref[:, mid:mid + LANES] = k_pe


def _mla_proj(lat, pos, invf, gq, gkv, wqa, wqb, wk, wvt, tm=512):
    t = lat.shape[0]
    full = lambda a: pl.BlockSpec(a.shape, lambda i: (0, 0))
    return pl.pallas_call(
        _mla_proj_kernel,
        grid=(t // tm,),
        in_specs=[
            pl.BlockSpec((tm, LAT_COLS), lambda i: (i, 0)),
            pl.BlockSpec((tm, 1), lambda i: (i, 0)),
            full(invf), full(gq), full(gkv), full(wqa), full(wqb), full(wk), full(wvt),
        ],
        out_specs=[
            pl.BlockSpec((tm, MLA_HEADS * MLA_PAD), lambda i: (i, 0)),
            pl.BlockSpec((tm, MLA_HEADS * MLA_PAD), lambda i: (i, 0)),
            pl.BlockSpec((MLA_OUT, tm), lambda i: (0, i)),
        ],
        out_shape=[
            jax.ShapeDtypeStruct((t, MLA_HEADS * MLA_PAD), BF16),
            jax.ShapeDtypeStruct((t, MLA_HEADS * MLA_PAD), BF16),
            jax.ShapeDtypeStruct((MLA_OUT, t), BF16),
        ],
        compiler_params=_params("parallel"),
        name="mla_proj",
    )(lat, pos, invf, gq, gkv, wqa, wqb, wk, wvt)


LOG2E = 1.4426950408889634


def _three_stage(n_pairs, stage_a, stage_b, stage_c):
    a_issue, a_finish = stage_a
    b_issue, b_finish = stage_b
    c_issue, c_finish = stage_c

    def run_a(n, slot, first):
        a_finish(n, slot, a_issue(n, slot, first), first)

    run_a(0, 0, True)
    run_a(1, 1, True)
    b_finish(0, 0, b_issue(0, 0))

    def half(na, sa, nb, sb, nc, sc):
        ra = a_issue(na, sa, False)
        rb = b_issue(nb, sb)
        rc = c_issue(nc, sc)
        a_finish(na, sa, ra, False)
        b_finish(nb, sb, rb)
        c_finish(nc, sc, rc)

    def body(p, carry):
        half(2 * p, 0, 2 * p - 1, 1, 2 * p - 2, 0)
        half(2 * p + 1, 1, 2 * p, 0, 2 * p - 1, 1)
        return carry

    lax.fori_loop(1, n_pairs + 1, body, 0)
    last = 2 * n_pairs + 1
    rb = b_issue(last, 1)
    rc = c_issue(last - 1, 0)
    b_finish(last, 1, rb)
    c_finish(last - 1, 0, rc)
    c_finish(last, 1, c_issue(last, 1))


def _two_stage(n_pairs, produce, consume):
    produce(0, 0, True)
    produce(1, 1, True)
    consume(0, 0)

    def body(p, carry):
        produce(2 * p, 0, False)
        consume(2 * p - 1, 1)
        produce(2 * p + 1, 1, False)
        consume(2 * p, 0)
        return carry

    lax.fori_loop(1, n_pairs + 1, body, 0)
    consume(2 * n_pairs + 1, 1)


MLA_SUM_ROWS = 16
MLA_GROUP = 4


def _mla_attn_kernel(q_ref, k_ref, vt_ref, o_ref, s_ref, m_ref, acc_ref, *, tq, tk):
    i = pl.program_id(2)
    heads = range(MLA_GROUP)
    dq, dv = MLA_PAD, MLA_V
    m_ref[...] = jnp.full(m_ref.shape, -jnp.inf, F32)
    acc_ref[...] = jnp.zeros(acc_ref.shape, F32)
    ones = jnp.ones((MLA_SUM_ROWS, tk), BF16)

    def key_start(n):
        tile = jnp.where(n < 2, 2 * i + n, 2 * i + 1 - n)
        return pl.multiple_of(tile * tk, tk)

    def produce(n, slot, diagonal):
        ks = key_start(n)
        for g in heads:
            st = _dot_nt(k_ref[pl.ds(ks, tk), g * dq:(g + 1) * dq], q_ref[:, g * dq:(g + 1) * dq])
            if diagonal:
                kpos = ks + lax.broadcasted_iota(jnp.int32, st.shape, 0)
                qpos = i * tq + lax.broadcasted_iota(jnp.int32, st.shape, 1)
                st = jnp.where(kpos <= qpos, st, -jnp.inf)
            s_ref[g, slot] = st

    def consume(n, slot):
        ks = key_start(n)
        for g in heads:
            st = s_ref[g, slot]
            m_old = m_ref[g]
            m_new = jnp.maximum(m_old, jnp.max(st, axis=0, keepdims=True))
            alpha = jnp.exp2(m_old - m_new)
            p = jnp.exp2(st - m_new).astype(BF16)
            v_ones = jnp.concatenate([vt_ref[g * dv:(g + 1) * dv, pl.ds(ks, tk)], ones], axis=0)
            acc_ref[g] = alpha * acc_ref[g] + _dot(v_ones, p)
            m_ref[g] = m_new

    _two_stage(i, produce, consume)
    for g in heads:
        o_ref[:, g * dv:(g + 1) * dv] = (acc_ref[g, :dv, :] / acc_ref[g, dv:dv + 1, :]).T


def _mla_attn(q, k, vt, batch, seq, tq=512):
    tk = tq // 2
    nq = seq // tq
    groups = MLA_HEADS // MLA_GROUP
    kern = functools.partial(_mla_attn_kernel, tq=tq, tk=tk)
    return pl.pallas_call(
        kern,
        grid=(batch, groups, nq),
        in_specs=[
            pl.BlockSpec((tq, MLA_GROUP * MLA_PAD), lambda b, h, i: (b * nq + i, h)),
            pl.BlockSpec((seq, MLA_GROUP * MLA_PAD), lambda b, h, i: (b, h)),
            pl.BlockSpec((MLA_GROUP * MLA_V, seq), lambda b, h, i: (h, b)),
        ],
        out_specs=pl.BlockSpec((tq, MLA_GROUP * MLA_V), lambda b, h, i: (b * nq + i, h)),
        out_shape=jax.ShapeDtypeStruct((batch * seq, MLA_OUT), F32),
        scratch_shapes=[
            pltpu.VMEM((MLA_GROUP, 2, tk, tq), F32),
            pltpu.VMEM((MLA_GROUP, 1, tq), F32),
            pltpu.VMEM((MLA_GROUP, MLA_V + MLA_SUM_ROWS, tq), F32),
        ],
        compiler_params=_params("parallel", "parallel", "arbitrary"),
        name="mla_attn",
    )(q, k, vt)


SB_EXP_CLAMP = 126.0
SB_GROUP = 4


def _sb_attn_kernel(q_ref, k_ref, vt_ref, o_ref, z_ref, hl_ref, arg_ref, acc_ref, c_ref, *, tq, tk):
    i = pl.program_id(2)
    heads = range(SB_GROUP)
    d = SB_DIM
    acc_ref[...] = jnp.zeros(acc_ref.shape, F32)
    c_ref[...] = jnp.zeros(c_ref.shape, F32)
    col = lax.broadcasted_iota(jnp.int32, (tk, tk), 1)
    row = lax.broadcasted_iota(jnp.int32, (tk, tk), 0)
    neg_tri = jnp.where(col >= row, -1.0, 0.0).astype(BF16)

    def key_start(n):
        return pl.multiple_of((2 * i + 1 - n) * tk, tk)

    def a_issue(n, slot, masked):
        ks = key_start(n)
        return [_dot_nt(k_ref[pl.ds(ks, tk), g * d:(g + 1) * d], q_ref[:, g * d:(g + 1) * d])
                for g in heads]

    def a_finish(n, slot, z2s, masked):
        for g in heads:
            z2 = z2s[g]
            sp = jnp.maximum(z2, jnp.log2(1.0 + jnp.exp2(jnp.minimum(z2, SB_EXP_CLAMP))))
            if masked:
                kpos = key_start(n) + lax.broadcasted_iota(jnp.int32, z2.shape, 0)
                qpos = i * tq + lax.broadcasted_iota(jnp.int32, z2.shape, 1)
                mask = kpos < qpos
                sp = jnp.where(mask, sp, 0.0)
                z2 = jnp.where(mask, z2, -jnp.inf)
            hl_ref[g, slot] = sp.astype(BF16)
            z_ref[g, slot] = z2

    def b_issue(n, slot):
        return [_dot(neg_tri, hl_ref[g, slot]) for g in heads]

    def b_finish(n, slot, laters):
        for g in heads:
            arg_ref[g, slot] = z_ref[g, slot] + laters[g] + c_ref[g]
            c_ref[g] += laters[g][0:1, :]

    def c_issue(n, slot):
        ks = key_start(n)
        return [_dot(vt_ref[g * d:(g + 1) * d, pl.ds(ks, tk)], jnp.exp2(arg_ref[g, slot]).astype(BF16))
                for g in heads]

    def c_finish(n, slot, pvs):
        for g in heads:
            acc_ref[g] += pvs[g]

    _three_stage(i, (a_issue, a_finish), (b_issue, b_finish), (c_issue, c_finish))
    for g in heads:
        o_ref[:, g * d:(g + 1) * d] = acc_ref[g].T


def _sb_attn(qk, vt, batch, seq, tq=512):
    tk = tq // 2
    nq = seq // tq
    groups = SB_HEADS // SB_GROUP
    gd = SB_GROUP * SB_DIM
    kern = functools.partial(_sb_attn_kernel, tq=tq, tk=tk)
    return pl.pallas_call(
        kern,
        grid=(batch, groups, nq),
        in_specs=[
            pl.BlockSpec((tq, gd), lambda b, h, i: (b * nq + i, h)),
            pl.BlockSpec((seq, gd), lambda b, h, i: (b, groups + h)),
            pl.BlockSpec((gd, seq), lambda b, h, i: (h, b)),
        ],
        out_specs=pl.BlockSpec((tq, gd), lambda b, h, i: (b * nq + i, h)),
        out_shape=jax.ShapeDtypeStruct((batch * seq, SB_OUT), F32),
        scratch_shapes=[
            pltpu.VMEM((SB_GROUP, 2, tk, tq), F32),
            pltpu.VMEM((SB_GROUP, 2, tk, tq), BF16),
            pltpu.VMEM((SB_GROUP, 2, tk, tq), F32),
            pltpu.VMEM((SB_GROUP, SB_DIM, tq), F32),
            pltpu.VMEM((SB_GROUP, 1, tq), F32),
        ],
        compiler_params=_params("parallel", "parallel", "arbitrary"),
        name="sb_attn",
    )(qk, qk, vt)


def _out_proj_kernel(x_ref, oa_ref, ob_ref, ga_ref, gb_ref, w_ref, y_ref):
    na = _rms(oa_ref[...], ga_ref[...]).astype(BF16)
    nb = _rms(ob_ref[...], gb_ref[...]).astype(BF16)
    y_ref[...] = x_ref[...] + _dot(na, w_ref[:MLA_OUT, :]) + _dot(nb, w_ref[MLA_OUT:, :])


def _out_proj(x, oa, ob, ga, gb, w, tm=512):
    t = x.shape[0]
    full = lambda a: pl.BlockSpec(a.shape, lambda i: (0, 0))
    return pl.pallas_call(
        _out_proj_kernel,
        grid=(t // tm,),
        in_specs=[
            pl.BlockSpec((tm, D_MODEL), lambda i: (i, 0)),
            pl.BlockSpec((tm, MLA_OUT), lambda i: (i, 0)),
            pl.BlockSpec((tm, SB_OUT), lambda i: (i, 0)),
            full(ga), full(gb), full(w),
        ],
        out_specs=pl.BlockSpec((tm, D_MODEL), lambda i: (i, 0)),
        out_shape=jax.ShapeDtypeStruct((t, D_MODEL), F32),
        compiler_params=_params("parallel"),
        name="out_proj",
    )(x, oa, ob, ga, gb, w)


def _mem_kv_kernel(mem_ref, g_ref, w_ref, kv_ref):
    kv_ref[...] = _dot(_rms(mem_ref[...], g_ref[...]).astype(BF16), w_ref[...]).astype(BF16)


def _mem_kv(mem, g, w):
    m = mem.shape[0]
    n = w.shape[1]
    full = lambda a: pl.BlockSpec(a.shape, lambda i: (0, 0))
    return pl.pallas_call(
        _mem_kv_kernel,
        grid=(1,),
        in_specs=[full(mem), full(g), full(w)],
        out_specs=pl.BlockSpec((m, n), lambda i: (0, 0)),
        out_shape=jax.ShapeDtypeStruct((m, n), BF16),
        compiler_params=_params("arbitrary"),
        name="mem_kv",
    )(mem, g, w)


GROUP_LANE0 = N_EXPERTS


def _mem_route_kernel(x_ref, gx_ref, wq_ref, kv_ref, wo_ref, gf_ref, wrh_ref, wrl_ref, br_ref,
                      x2_ref, h2_ref, route_ref):
    x1 = x_ref[...]
    q = _dot(_rms(x1, gx_ref[...]).astype(BF16), wq_ref[...]).astype(BF16)
    scale = MEM_DIM ** -0.5
    kw = MEM_HEADS * MEM_DIM
    heads = []
    for h in range(MEM_HEADS):
        lo = h * MEM_DIM
        s = _dot_nt(q[:, lo:lo + MEM_DIM], kv_ref[:, lo:lo + MEM_DIM]) * scale
        e = jnp.exp(s - jnp.max(s, axis=-1, keepdims=True))
        p = (e / jnp.sum(e, axis=-1, keepdims=True)).astype(BF16)
        heads.append(_dot(p, kv_ref[:, kw + lo:kw + lo + MEM_DIM]).astype(BF16))
    o = jnp.concatenate(heads, axis=-1)
    x2 = x1 + _dot(o, wo_ref[...])
    x2_ref[...] = x2
    h2 = _rms(x2, gf_ref[...])
    h2_ref[...] = h2

    hh, hl = _split_bf16(h2)
    lg = _dot(hh, wrh_ref[...]) + _dot(hl, wrh_ref[...]) + _dot(hh, wrl_ref[...]) + br_ref[...]
    lane = lax.broadcasted_iota(jnp.int32, lg.shape, 1)
    big = jnp.int32(1 << 20)
    ninf = -jnp.inf

    def lane_max(v):
        return jnp.max(v, axis=-1, keepdims=True)

    def first_lane(cond):
        return jnp.min(jnp.where(cond, lane, big), axis=-1, keepdims=True)

    is_g = (lane >= GROUP_LANE0) & (lane < GROUP_LANE0 + N_GROUPS)
    g_max = lane_max(jnp.where(is_g, lg, ninf))
    g_sum = jnp.sum(jnp.where(is_g, jnp.exp(lg - g_max), 0.0), axis=-1, keepdims=True)
    p_g = 1.0 / g_sum
    g_idx = first_lane(is_g & (lg == g_max)) - GROUP_LANE0
    in_grp = (lane < N_EXPERTS) & ((lane // EXPERTS_PER_GROUP) == g_idx)
    e_max = lane_max(jnp.where(in_grp, lg, ninf))
    e_sum = jnp.sum(jnp.where(in_grp, jnp.exp(lg - e_max), 0.0), axis=-1, keepdims=True)
    i1 = first_lane(in_grp & (lg == e_max))
    rest = in_grp & (lane != i1)
    e_max2 = lane_max(jnp.where(rest, lg, ninf))
    i2 = first_lane(rest & (lg == e_max2))
    p1 = 1.0 / e_sum
    p2 = jnp.exp(e_max2 - e_max) / e_sum
    den = p1 + p2
    gate1 = p_g * (p1 / den)
    gate2 = p_g * (p2 / den)
    route = jnp.where(lane == 0, i1.astype(F32),
                      jnp.where(lane == 1, i2.astype(F32),
                                jnp.where(lane == 2, gate1,
                                          jnp.where(lane == 3, gate2, 0.0))))
    route_ref[...] = route


def _mem_route(x1, gx, wq, kv, wo, gf, wrh, wrl, br, seq, tm=512):
    t = x1.shape[0]
    per_batch = seq // tm
    full = lambda a: pl.BlockSpec(a.shape, lambda i: (0, 0))
    return pl.pallas_call(
        _mem_route_kernel,
        grid=(t // tm,),
        in_specs=[
            pl.BlockSpec((tm, D_MODEL), lambda i: (i, 0)),
            full(gx), full(wq),
            pl.BlockSpec((MEM_LEN, kv.shape[1]), lambda i: (i // per_batch, 0)),
            full(wo), full(gf), full(wrh), full(wrl), full(br),
        ],
        out_specs=[
            pl.BlockSpec((tm, D_MODEL), lambda i: (i, 0)),
            pl.BlockSpec((tm, D_MODEL), lambda i: (i, 0)),
            pl.BlockSpec((tm, LANES), lambda i: (i, 0)),
        ],
        out_shape=[
            jax.ShapeDtypeStruct((t, D_MODEL), F32),
            jax.ShapeDtypeStruct((t, D_MODEL), F32),
            jax.ShapeDtypeStruct((t, LANES), F32),
        ],
        compiler_params=_params("parallel"),
        name="mem_route",
    )(x1, gx, wq, kv, wo, gf, wrh, wrl, br)


SLOT_BLK = 256


def _slots_kernel(route_ref, slot_ref, meta_ref, cum_ref, *, n_tok):
    nblk = n_tok // SLOT_BLK
    e_iota = lax.broadcasted_iota(jnp.int32, (LANES, SLOT_BLK), 0)
    incl = (lax.broadcasted_iota(jnp.int32, (SLOT_BLK, SLOT_BLK), 0)
            <= lax.broadcasted_iota(jnp.int32, (SLOT_BLK, SLOT_BLK), 1)).astype(BF16)

    def onehots(b):
        ts = pl.multiple_of(b * SLOT_BLK, SLOT_BLK)
        ids = route_ref[pl.ds(ts, SLOT_BLK), :].T
        oh1 = e_iota == ids[0:1, :].astype(jnp.int32)
        oh2 = e_iota == ids[1:2, :].astype(jnp.int32)
        return ts, oh1, oh2

    def count(b, carry):
        ts, oh1, oh2 = onehots(b)
        cnt = (oh1 | oh2).astype(F32).astype(BF16)
        c = _dot(cnt, incl) + carry
        cum_ref[:, pl.ds(ts, SLOT_BLK)] = c
        return c[:, SLOT_BLK - 1:SLOT_BLK]

    total = lax.fori_loop(0, nblk, count, jnp.zeros((LANES, 1), F32))
    tiles = jnp.floor((total + (SLOT_TILE - 1)) * (1.0 / SLOT_TILE))
    below = (lax.broadcasted_iota(jnp.int32, (LANES, LANES), 1)
             < lax.broadcasted_iota(jnp.int32, (LANES, LANES), 0)).astype(BF16)
    tile_lo = _dot(below, jnp.broadcast_to(tiles, (LANES, LANES)).astype(BF16))
    base = tile_lo[:, 0:1] * SLOT_TILE

    def assign(b, carry):
        ts, oh1, oh2 = onehots(b)
        pos = base + cum_ref[:, pl.ds(ts, SLOT_BLK)] - 1.0
        s1 = jnp.sum(jnp.where(oh1, pos, 0.0), axis=0, keepdims=True)
        s2 = jnp.sum(jnp.where(oh2, pos, 0.0), axis=0, keepdims=True)
        row = lax.broadcasted_iota(jnp.int32, (8, SLOT_BLK), 0)
        slot_ref[:, pl.ds(ts, SLOT_BLK)] = jnp.where(row == 0, s1, jnp.where(row == 1, s2, 0.0)).astype(jnp.int32)
        return carry

    lax.fori_loop(0, nblk, assign, 0)

    tile_hi = tile_lo + tiles
    tix = lax.broadcasted_iota(jnp.int32, (LANES, LANES), 1).astype(F32)
    is_e = lax.broadcasted_iota(jnp.int32, (LANES, LANES), 0) < N_EXPERTS
    owner = jnp.sum(jnp.where(is_e & (tile_hi <= tix), 1.0, 0.0), axis=0, keepdims=True)
    n_tiles = jnp.max(jnp.where(is_e, tile_hi, 0.0), axis=0, keepdims=True)
    valid = tix[0:1, :] < n_tiles
    last_owner = jnp.max(jnp.where(valid, owner, 0.0), axis=1, keepdims=True)
    owner = jnp.where(valid, owner, last_owner)
    eye = (lax.broadcasted_iota(jnp.int32, (LANES, LANES), 0)
           == lax.broadcasted_iota(jnp.int32, (LANES, LANES), 1))
    to_lanes = lambda colvec: jnp.sum(jnp.where(eye, colvec, 0.0), axis=0, keepdims=True)
    pad_first = to_lanes(base + total)
    pad_end = to_lanes(base + tiles * SLOT_TILE)
    row = lax.broadcasted_iota(jnp.int32, (8, LANES), 0)
    meta = jnp.where(row == 0, owner,
                     jnp.where(row == 1, valid.astype(F32),
                               jnp.where(row == 2, pad_first, jnp.where(row == 3, pad_end, 0.0))))
    meta_ref[...] = meta.astype(jnp.int32)


def _slots(route):
    t = route.shape[0]
    kern = functools.partial(_slots_kernel, n_tok=t)
    return pl.pallas_call(
        kern,
        grid=(1,),
        in_specs=[pl.BlockSpec(route.shape, lambda i: (0, 0))],
        out_specs=[
            pl.BlockSpec((8, t), lambda i: (0, 0)),
            pl.BlockSpec((8, LANES), lambda i: (0, 0)),
        ],
        out_shape=[
            jax.ShapeDtypeStruct((8, t), jnp.int32),
            jax.ShapeDtypeStruct((8, LANES), jnp.int32),
        ],
        scratch_shapes=[pltpu.VMEM((LANES, t), F32)],
        compiler_params=_params("arbitrary"),
        name="slots",
    )(route)


def _row_copy(src_ref, src_row, dst_ref, dst_row, sem):
    return pltpu.make_async_copy(src_ref.at[pl.ds(src_row, 1), :], dst_ref.at[pl.ds(dst_row, 1), :], sem)


def _dispatch_kernel(s1_ref, s2_ref, pad_first_ref, pad_end_ref, h_ref, xs_ref, zero_ref, sem, pad_sem, *, tm):
    i = pl.program_id(0)
    t0 = i * tm

    def copies(r):
        return (_row_copy(h_ref, r, xs_ref, s1_ref[t0 + r], sem),
                _row_copy(h_ref, r, xs_ref, s2_ref[t0 + r], sem))

    def issue(r, c):
        first, second = copies(r)
        first.start(priority=0)
        second.start(priority=1)
        return c

    lax.fori_loop(0, tm, issue, 0, unroll=8)

    @pl.when(i == pl.num_programs(0) - 1)
    def _():
        zero_ref[...] = jnp.zeros(zero_ref.shape, F32)

        def pad_copy(slot):
            return _row_copy(zero_ref, 0, xs_ref, slot, pad_sem)

        def fill(e, c):
            def one(s, cc):
                pad_copy(s).start()
                return cc

            return lax.fori_loop(pad_first_ref[e], pad_end_ref[e], one, c)

        lax.fori_loop(0, N_EXPERTS, fill, 0)

        def settle(e, c):
            def one(s, cc):
                pad_copy(s).wait()
                return cc

            return lax.fori_loop(pad_first_ref[e], pad_end_ref[e], one, c)

        lax.fori_loop(0, N_EXPERTS, settle, 0)

    def drain(r, c):
        for cp in copies(r):
            cp.wait()
        return c

    lax.fori_loop(0, tm, drain, 0, unroll=8)


def _dispatch(slot1, slot2, pad_first, pad_end, h2, n_slots, tm=256):
    t = h2.shape[0]
    kern = functools.partial(_dispatch_kernel, tm=tm)
    return pl.pallas_call(
        kern,
        grid_spec=pltpu.PrefetchScalarGridSpec(
            num_scalar_prefetch=4,
            grid=(t // tm,),
            in_specs=[pl.BlockSpec((tm, D_MODEL), lambda i, *_: (i, 0))],
            out_specs=pl.BlockSpec(memory_space=pl.ANY),
            scratch_shapes=[
                pltpu.VMEM((8, D_MODEL), F32),
                pltpu.SemaphoreType.DMA(()),
                pltpu.SemaphoreType.DMA(()),
            ],
        ),
        out_shape=jax.ShapeDtypeStruct((n_slots, D_MODEL), F32),
        compiler_params=_params("arbitrary"),
        name="dispatch",
    )(slot1, slot2, pad_first, pad_end, h2)


def _experts_kernel(own_ref, valid_ref, xs_ref, wg_hbm, wu_hbm, wd_hbm, ys_ref,
                    wgf, wuf, wdf, wgb, wub, wdb, wsem, wslot_ref):
    i = pl.program_id(0)
    nt = pl.num_programs(0)
    at = lambda ref, j: ref[jnp.minimum(j, nt - 1)]

    def weight_copies(e, s):
        return (pltpu.make_async_copy(wg_hbm.at[e], wgf.at[s], wsem.at[s]),
                pltpu.make_async_copy(wu_hbm.at[e], wuf.at[s], wsem.at[s]),
                pltpu.make_async_copy(wd_hbm.at[e], wdf.at[s], wsem.at[s]))

    @pl.when(i == 0)
    def _():
        wslot_ref[0] = 0
        for cp in weight_copies(own_ref[0], 0):
            cp.start()

    @pl.when(valid_ref[i] > 0)
    def _():
        e = own_ref[i]

        @pl.when((i == 0) | (own_ref[jnp.maximum(i - 1, 0)] != e))
        def _():
            s = wslot_ref[0]
            for cp in weight_copies(e, s):
                cp.wait()
            j = lax.while_loop(lambda j: (j < nt) & (at(own_ref, j) == e), lambda j: j + 1, i + 1)

            @pl.when((j < nt) & (at(valid_ref, j) > 0))
            def _():
                for cp in weight_copies(at(own_ref, j), 1 - s):
                    cp.start()

            wgb[...] = wgf[s].astype(BF16)
            wub[...] = wuf[s].astype(BF16)
            wdb[...] = wdf[s].astype(BF16)
            wslot_ref[0] = 1 - s

        x = xs_ref[...].astype(BF16)
        a = _dot(x, wgb[...])
        u = _dot(x, wub[...])
        act = (a * jax.nn.sigmoid(a) * u).astype(BF16)
        ys_ref[...] = _dot(act, wdb[...])

    @pl.when(valid_ref[i] == 0)
    def _():
        ys_ref[...] = jnp.zeros(ys_ref.shape, F32)


def _experts(own, valid, xs, wg, wu, wd):
    n_tiles = xs.shape[0] // SLOT_TILE
    hbm = pl.BlockSpec(memory_space=pl.ANY)
    tile = (SLOT_TILE, D_MODEL)
    return pl.pallas_call(
        _experts_kernel,
        grid_spec=pltpu.PrefetchScalarGridSpec(
            num_scalar_prefetch=2,
            grid=(n_tiles,),
            in_specs=[
                pl.BlockSpec(tile, lambda i, o, v: (jnp.where(v[i] > 0, i, 0), 0)),
                hbm, hbm, hbm,
            ],
            out_specs=pl.BlockSpec(tile, lambda i, o, v: (i, 0)),
            scratch_shapes=[
                pltpu.VMEM((2, D_MODEL, D_EXPERT), F32),
                pltpu.VMEM((2, D_MODEL, D_EXPERT), F32),
                pltpu.VMEM((2, D_EXPERT, D_MODEL), F32),
                pltpu.VMEM((D_MODEL, D_EXPERT), BF16),
                pltpu.VMEM((D_MODEL, D_EXPERT), BF16),
                pltpu.VMEM((D_EXPERT, D_MODEL), BF16),
                pltpu.SemaphoreType.DMA((2,)),
                pltpu.SMEM((1,), jnp.int32),
            ],
        ),
        out_shape=jax.ShapeDtypeStruct(xs.shape, F32),
        compiler_params=_params("arbitrary"),
        name="experts",
    )(own, valid, xs, wg, wu, wd)


def _combine_kernel(s1_ref, s2_ref, x_ref, route_ref, g_ref, ys_ref, y_ref, buf_ref, sem, *, tm):
    i = pl.program_id(0)
    n = pl.num_programs(0)

    def copies(tile, p, r):
        t = tile * tm + r
        return (_row_copy(ys_ref, s1_ref[t], buf_ref.at[p, 0], r, sem.at[p]),
                _row_copy(ys_ref, s2_ref[t], buf_ref.at[p, 1], r, sem.at[p]))

    def gather_start(tile, p):
        def issue(r, c):
            first, second = copies(tile, p, r)
            first.start(priority=0)
            second.start(priority=1)
            return c

        lax.fori_loop(0, tm, issue, 0, unroll=8)

    def gather_wait(tile, p):
        def drain(r, c):
            for cp in copies(tile, p, r):
                cp.wait()
            return c

        lax.fori_loop(0, tm, drain, 0, unroll=8)

    @pl.when(i == 0)
    def _():
        gather_start(0, 0)

    p = i % 2

    @pl.when(i + 1 < n)
    def _():
        gather_start(i + 1, 1 - p)

    gather_wait(i, p)
    y = x_ref[...] + route_ref[:, 2:3] * buf_ref[p, 0] + route_ref[:, 3:4] * buf_ref[p, 1]
    y_ref[...] = _rms(y, g_ref[...])


def _combine(slot1, slot2, x2, route, g, ys, tm=256):
    t = x2.shape[0]
    kern = functools.partial(_combine_kernel, tm=tm)
    return pl.pallas_call(
        kern,
        grid_spec=pltpu.PrefetchScalarGridSpec(
            num_scalar_prefetch=2,
            grid=(t // tm,),
            in_specs=[
                pl.BlockSpec((tm, D_MODEL), lambda i, s1, s2: (i, 0)),
                pl.BlockSpec((tm, LANES), lambda i, s1, s2: (i, 0)),
                pl.BlockSpec((1, D_MODEL), lambda i, s1, s2: (0, 0)),
                pl.BlockSpec(memory_space=pl.ANY),
            ],
            out_specs=pl.BlockSpec((tm, D_MODEL), lambda i, s1, s2: (i, 0)),
            scratch_shapes=[
                pltpu.VMEM((2, 2, tm, D_MODEL), F32),
                pltpu.SemaphoreType.DMA((2,)),
            ],
        ),
        out_shape=jax.ShapeDtypeStruct((t, D_MODEL), F32),
        compiler_params=_params("arbitrary"),
        name="combine",
    )(slot1, slot2, x2, route, g, ys)


def _swap_halves(w):
    half = w.shape[-1] // 2
    return jnp.concatenate([-w[..., half:], w[..., :half]], axis=-1)


def _row(v):
    return v.reshape(1, -1).astype(F32)


def kernel(x, mem, positions, norm_mix, w_in, norm_q_lat, w_q_b, norm_kv_lat, w_kv_b, norm_mla_out, norm_sb_out, w_out, norm_mem_x, norm_mem_src, w_mem_q, w_mem_kv, w_mem_o, norm_ffn, w_group, b_group, w_expert_router, b_expert, w_gate, w_up, w_down, norm_final):
    batch, seq, d = x.shape
    t = batch * seq
    depth = w_in.shape[0]
    assert depth == 1, "single-layer trunk only"
    xt = x.reshape(t, d)
    pos = positions.reshape(t, 1)
    inv_freq = ROPE_THETA ** (-jnp.arange(0, MLA_ROPE, 2, dtype=F32) / MLA_ROPE)
    invf = jnp.concatenate([inv_freq, inv_freq, jnp.zeros((LANES - MLA_ROPE,), F32)]).reshape(1, LANES)
    n_slots = (2 * t // SLOT_TILE + N_EXPERTS) * SLOT_TILE

    for l in range(depth):
        wi = w_in[l]
        lat_w = MLA_Q_RANK + MLA_KV_RANK
        w_kpe = wi[:, lat_w:lat_w + MLA_ROPE]
        zpad = jnp.zeros((d, LANES - MLA_ROPE), F32)
        sb0 = lat_w + MLA_ROPE
        w_lat = jnp.concatenate([wi[:, :sb0], zpad, _swap_halves(w_kpe), zpad], axis=1).astype(BF16)
        sb_scale = jnp.concatenate([jnp.full((SB_OUT,), SB_DIM ** -0.5 * LOG2E, F32), jnp.ones((2 * SB_OUT,), F32)])
        w_sb = (wi[:, sb0:] * sb_scale).astype(BF16)
        wq = w_q_b[l].reshape(MLA_Q_RANK, MLA_HEADS, MLA_QK) * (MLA_QK ** -0.5 * LOG2E)
        wq_pe = wq[:, :, MLA_NOPE:]
        zq = jnp.zeros((MLA_Q_RANK, MLA_HEADS, MLA_PAD - MLA_QK), F32)
        wqa = jnp.concatenate([wq, zq], axis=-1).reshape(MLA_Q_RANK, MLA_HEADS * MLA_PAD).astype(BF16)
        wqb = jnp.concatenate([_swap_halves(wq_pe), zq], axis=-1).reshape(MLA_Q_RANK, MLA_HEADS * LANES).astype(BF16)
        wkv = w_kv_b[l].reshape(MLA_KV_RANK, MLA_HEADS, MLA_NOPE + MLA_V)
        wk = wkv[:, :, :MLA_NOPE].reshape(MLA_KV_RANK, MLA_OUT).astype(BF16)
        wvt = wkv[:, :, MLA_NOPE:].reshape(MLA_KV_RANK, MLA_OUT).T.astype(BF16)
        w_router = jnp.concatenate(
            [w_expert_router[l], w_group[l], jnp.zeros((d, LANES - N_EXPERTS - N_GROUPS), F32)], axis=1)
        wr_hi = w_router.astype(BF16)
        wr_lo = (w_router - wr_hi.astype(F32)).astype(BF16)
        b_router = jnp.concatenate(
            [b_expert[l].astype(F32), b_group[l].astype(F32), jnp.zeros((LANES - N_EXPERTS - N_GROUPS,), F32)]
        ).reshape(1, LANES)

        lat, sb_qk, sb_vt = _proj_in(xt, _row(norm_mix[l]), w_lat, w_sb)
        q, k, vt = _mla_proj(lat, pos, invf, _row(norm_q_lat[l]), _row(norm_kv_lat[l]), wqa, wqb, wk, wvt)
        o_mla = _mla_attn(q, k, vt, batch, seq)
        o_sb = _sb_attn(sb_qk, sb_vt, batch, seq)
        x1 = _out_proj(xt, o_mla, o_sb, _row(norm_mla_out[l]), _row(norm_sb_out[l]), w_out[l].astype(BF16))

        kv = _mem_kv(mem.reshape(batch * MEM_LEN, d), _row(norm_mem_src[l]), w_mem_kv[l].astype(BF16))
        x2, h2, route = _mem_route(x1, _row(norm_mem_x[l]), w_mem_q[l].astype(BF16), kv,
                                   w_mem_o[l].astype(BF16), _row(norm_ffn[l]), wr_hi, wr_lo, b_router, seq)

        slots, meta = _slots(route)
        xs = _dispatch(slots[0], slots[1], meta[2], meta[3], h2, n_slots)
        ys = _experts(meta[0], meta[1], xs, w_gate[l], w_up[l], w_down[l])
        xt = _combine(slots[0], slots[1], x2, route, _row(norm_final), ys)
    return xt.reshape(batch, seq, d)
```

```python
import functools

import jax
import jax.numpy as jnp
from jax import lax
from jax.experimental import pallas as pl
from jax.experimental.pallas import tpu as pltpu

F32 = jnp.float32
BF16 = jnp.bfloat16

EPS = 1e-6
ROPE_THETA = 10000.0

D_MODEL = 2048
MEM_LEN = 256
MLA_HEADS = 8
MLA_NOPE = 128
MLA_ROPE = 64
MLA_QK = MLA_NOPE + MLA_ROPE
MLA_V = 128
MLA_Q_RANK = 512
MLA_KV_RANK = 256
MLA_PAD = 256
SB_HEADS = 8
SB_DIM = 128
MLA_OUT = MLA_HEADS * MLA_V
SB_OUT = SB_HEADS * SB_DIM
MEM_HEADS = 4
MEM_DIM = 128
N_GROUPS = 4
EXPERTS_PER_GROUP = 8
N_EXPERTS = N_GROUPS * EXPERTS_PER_GROUP
D_EXPERT = 512

LANES = 128
LAT_COLS = 1024
IN_COLS_PAD = LAT_COLS + 3 * SB_OUT

SLOT_TILE = 256
VMEM_LIMIT = 56 * 1024 * 1024


def _rms(x, g):
    return x * lax.rsqrt(jnp.mean(x * x, axis=-1, keepdims=True) + EPS) * g


def _dot(a, b):
    return jnp.dot(a, b, preferred_element_type=F32)


def _dot_nt(a, b):
    return lax.dot_general(a, b, (((1,), (1,)), ((), ())), preferred_element_type=F32)


def _split_bf16(x):
    hi = x.astype(BF16)
    lo = (x - hi.astype(F32)).astype(BF16)
    return hi, lo


def _params(*sem):
    return pltpu.CompilerParams(dimension_semantics=sem, vmem_limit_bytes=VMEM_LIMIT)


def _proj_in_kernel(x_ref, g_ref, wl_ref, ws_ref, lat_ref, qk_ref, vt_ref, h_ref):
    j = pl.program_id(1)
    last = pl.num_programs(1) - 1

    @pl.when(j == 0)
    def _():
        h_ref[...] = _rms(x_ref[...], g_ref[...]).astype(BF16)
        lat_ref[...] = _dot_nt(h_ref[...], wl_ref[...])

    @pl.when((j > 0) & (j < last))
    def _():
        qk_ref[...] = _dot_nt(h_ref[...], ws_ref[...]).astype(BF16)

    @pl.when(j == last)
    def _():
        vt_ref[...] = _dot_nt(ws_ref[...], h_ref[...]).astype(BF16)


def _proj_in(x, g, w_lat, w_sb, tm=512):
    t = x.shape[0]
    tn = LAT_COLS
    n_sb = w_sb.shape[0] // tn
    return pl.pallas_call(
        _proj_in_kernel,
        grid=(t // tm, n_sb + 1),
        in_specs=[
            pl.BlockSpec((tm, D_MODEL), lambda i, j: (i, 0)),
            pl.BlockSpec((1, D_MODEL), lambda i, j: (0, 0)),
            pl.BlockSpec((tn, D_MODEL), lambda i, j: (0, 0)),
            pl.BlockSpec((tn, D_MODEL), lambda i, j: (jnp.maximum(j - 1, 0), 0)),
        ],
        out_specs=[
            pl.BlockSpec((tm, tn), lambda i, j: (i, 0)),
            pl.BlockSpec((tm, tn), lambda i, j: (i, jnp.clip(j - 1, 0, n_sb - 2))),
            pl.BlockSpec((SB_OUT, tm), lambda i, j: (0, i)),
        ],
        out_shape=[
            jax.ShapeDtypeStruct((t, LAT_COLS), F32),
            jax.ShapeDtypeStruct((t, 2 * SB_OUT), BF16),
            jax.ShapeDtypeStruct((SB_OUT, t), BF16),
        ],
        scratch_shapes=[pltpu.VMEM((tm, D_MODEL), BF16)],
        compiler_params=_params("parallel", "arbitrary"),
        name="proj_in",
    )(x, g, w_lat, w_sb)


def _mla_proj_kernel(lat_ref, pos_ref, invf_ref, gq_ref, gkv_ref, wqa_ref, wqb_ref, wk_ref, wvt_ref,
                     q_ref, k_ref, vt_ref):
    cq = _rms(lat_ref[:, :MLA_Q_RANK], gq_ref[...]).astype(BF16)
    ckv = _rms(lat_ref[:, MLA_Q_RANK:MLA_Q_RANK + MLA_KV_RANK], gkv_ref[...]).astype(BF16)
    ang = pos_ref[...].astype(F32) * invf_ref[...]
    live = lax.broadcasted_iota(jnp.int32, ang.shape, 1) < MLA_ROPE
    cos2 = jnp.where(live, jnp.cos(ang), 0.0)
    sin2 = jnp.where(live, jnp.sin(ang), 0.0)

    qa = _dot(cq, wqa_ref[...])
    qb = _dot(cq, wqb_ref[...])
    kn = _dot(ckv, wk_ref[...])
    vt_ref[...] = _dot_nt(wvt_ref[...], ckv).astype(BF16)
    k_pe = (lat_ref[:, 768:896] * cos2 + lat_ref[:, 896:1024] * sin2).astype(BF16)
    for h in range(MLA_HEADS):
        lo = h * MLA_PAD
        mid = lo + LANES
        q_ref[:, lo:mid] = qa[:, lo:mid].astype(BF16)
        q_ref[:, mid:mid + LANES] = (qa[:, mid:mid + LANES] * cos2
                                     + qb[:, h * LANES:(h + 1) * LANES] * sin2).astype(BF16)
        k_ref[:, lo:mid] = kn[:, h * LANES:(h + 1) * LANES].astype(BF16)
        k_ref[:, mid:mid + LANES] = k_pe


def _mla_proj(lat, pos, invf, gq, gkv, wqa, wqb, wk, wvt, tm=512):
    t = lat.shape[0]
    full = lambda a: pl.BlockSpec(a.shape, lambda i: (0, 0))
    return pl.pallas_call(
        _mla_proj_kernel,
        grid=(t // tm,),
        in_specs=[
            pl.BlockSpec((tm, LAT_COLS), lambda i: (i, 0)),
            pl.BlockSpec((tm, 1), lambda i: (i, 0)),
            full(invf), full(gq), full(gkv), full(wqa), full(wqb), full(wk), full(wvt),
        ],
        out_specs=[
            pl.BlockSpec((tm, MLA_HEADS * MLA_PAD), lambda i: (i, 0)),
            pl.BlockSpec((tm, MLA_HEADS * MLA_PAD), lambda i: (i, 0)),
            pl.BlockSpec((MLA_OUT, tm), lambda i: (0, i)),
        ],
        out_shape=[
            jax.ShapeDtypeStruct((t, MLA_HEADS * MLA_PAD), BF16),
            jax.ShapeDtypeStruct((t, MLA_HEADS * MLA_PAD), BF16),
            jax.ShapeDtypeStruct((MLA_OUT, t), BF16),
        ],
        compiler_params=_params("parallel"),
        name="mla_proj",
    )(lat, pos, invf, gq, gkv, wqa, wqb, wk, wvt)


LOG2E = 1.4426950408889634


def _three_stage(n_pairs, stage_a, stage_b, stage_c):
    a_issue, a_finish = stage_a
    b_issue, b_finish = stage_b
    c_issue, c_finish = stage_c

    def run_a(n, slot, first):
        a_finish(n, slot, a_issue(n, slot, first), first)

    run_a(0, 0, True)
    run_a(1, 1, True)
    b_finish(0, 0, b_issue(0, 0))

    def half(na, sa, nb, sb, nc, sc):
        ra = a_issue(na, sa, False)
        rb = b_issue(nb, sb)
        rc = c_issue(nc, sc)
        a_finish(na, sa, ra, False)
        b_finish(nb, sb, rb)
        c_finish(nc, sc, rc)

    def body(p, carry):
        half(2 * p, 0, 2 * p - 1, 1, 2 * p - 2, 0)
        half(2 * p + 1, 1, 2 * p, 0, 2 * p - 1, 1)
        return carry

    lax.fori_loop(1, n_pairs + 1, body, 0)
    last = 2 * n_pairs + 1
    rb = b_issue(last, 1)
    rc = c_issue(last - 1, 0)
    b_finish(last, 1, rb)
    c_finish(last - 1, 0, rc)
    c_finish(last, 1, c_issue(last, 1))


def _two_stage(n_pairs, produce, consume):
    produce(0, 0, True)
    produce(1, 1, True)
    consume(0, 0)

    def body(p, carry):
        produce(2 * p, 0, False)
        consume(2 * p - 1, 1)
        produce(2 * p + 1, 1, False)
        consume(2 * p, 0)
        return carry

    lax.fori_loop(1, n_pairs + 1, body, 0)
    consume(2 * n_pairs + 1, 1)


MLA_SUM_ROWS = 16
MLA_GROUP = 4


def _mla_attn_kernel(q_ref, k_ref, vt_ref, o_ref, s_ref, m_ref, acc_ref, *, tq, tk):
    i = pl.program_id(2)
    heads = range(MLA_GROUP)
    dq, dv = MLA_PAD, MLA_V
    m_ref[...] = jnp.full(m_ref.shape, -jnp.inf, F32)
    acc_ref[...] = jnp.zeros(acc_ref.shape, F32)
    ones = jnp.ones((MLA_SUM_ROWS, tk), BF16)

    def key_start(n):
        tile = jnp.where(n < 2, 2 * i + n, 2 * i + 1 - n)
        return pl.multiple_of(tile * tk, tk)

    def produce(n, slot, diagonal):
        ks = key_start(n)
        for g in heads:
            st = _dot_nt(k_ref[pl.ds(ks, tk), g * dq:(g + 1) * dq], q_ref[:, g * dq:(g + 1) * dq])
            if diagonal:
                kpos = ks + lax.broadcasted_iota(jnp.int32, st.shape, 0)
                qpos = i * tq + lax.broadcasted_iota(jnp.int32, st.shape, 1)
                st = jnp.where(kpos <= qpos, st, -jnp.inf)
            s_ref[g, slot] = st

    def consume(n, slot):
        ks = key_start(n)
        for g in heads:
            st = s_ref[g, slot]
            m_old = m_ref[g]
            m_new = jnp.maximum(m_old, jnp.max(st, axis=0, keepdims=True))
            alpha = jnp.exp2(m_old - m_new)
            p = jnp.exp2(st - m_new).astype(BF16)
            v_ones = jnp.concatenate([vt_ref[g * dv:(g + 1) * dv, pl.ds(ks, tk)], ones], axis=0)
            acc_ref[g] = alpha * acc_ref[g] + _dot(v_ones, p)
            m_ref[g] = m_new

    _two_stage(i, produce, consume)
    for g in heads:
        o_ref[:, g * dv:(g + 1) * dv] = (acc_ref[g, :dv, :] / acc_ref[g, dv:dv + 1, :]).T


def _mla_attn(q, k, vt, batch, seq, tq=512):
    tk = tq // 2
    nq = seq // tq
    groups = MLA_HEADS // MLA_GROUP
    kern = functools.partial(_mla_attn_kernel, tq=tq, tk=tk)
    return pl.pallas_call(
        kern,
        grid=(batch, groups, nq),
        in_specs=[
            pl.BlockSpec((tq, MLA_GROUP * MLA_PAD), lambda b, h, i: (b * nq + i, h)),
            pl.BlockSpec((seq, MLA_GROUP * MLA_PAD), lambda b, h, i: (b, h)),
            pl.BlockSpec((MLA_GROUP * MLA_V, seq), lambda b, h, i: (h, b)),
        ],
        out_specs=pl.BlockSpec((tq, MLA_GROUP * MLA_V), lambda b, h, i: (b * nq + i, h)),
        out_shape=jax.ShapeDtypeStruct((batch * seq, MLA_OUT), F32),
        scratch_shapes=[
            pltpu.VMEM((MLA_GROUP, 2, tk, tq), F32),
            pltpu.VMEM((MLA_GROUP, 1, tq), F32),
            pltpu.VMEM((MLA_GROUP, MLA_V + MLA_SUM_ROWS, tq), F32),
        ],
        compiler_params=_params("parallel", "parallel", "arbitrary"),
        name="mla_attn",
    )(q, k, vt)


SB_EXP_CLAMP = 126.0
SB_GROUP = 4


def _sb_attn_kernel(q_ref, k_ref, vt_ref, o_ref, z_ref, hl_ref, arg_ref, acc_ref, c_ref, *, tq, tk):
    i = pl.program_id(2)
    heads = range(SB_GROUP)
    d = SB_DIM
    acc_ref[...] = jnp.zeros(acc_ref.shape, F32)
    c_ref[...] = jnp.zeros(c_ref.shape, F32)
    col = lax.broadcasted_iota(jnp.int32, (tk, tk), 1)
    row = lax.broadcasted_iota(jnp.int32, (tk, tk), 0)
    neg_tri = jnp.where(col >= row, -1.0, 0.0).astype(BF16)

    def key_start(n):
        return pl.multiple_of((2 * i + 1 - n) * tk, tk)

    def a_issue(n, slot, masked):
        ks = key_start(n)
        return [_dot_nt(k_ref[pl.ds(ks, tk), g * d:(g + 1) * d], q_ref[:, g * d:(g + 1) * d])
                for g in heads]

    def a_finish(n, slot, z2s, masked):
        for g in heads:
            z2 = z2s[g]
            sp = jnp.maximum(z2, jnp.log2(1.0 + jnp.exp2(jnp.minimum(z2, SB_EXP_CLAMP))))
            if masked:
                kpos = key_start(n) + lax.broadcasted_iota(jnp.int32, z2.shape, 0)
                qpos = i * tq + lax.broadcasted_iota(jnp.int32, z2.shape, 1)
                mask = kpos < qpos
                sp = jnp.where(mask, sp, 0.0)
                z2 = jnp.where(mask, z2, -jnp.inf)
            hl_ref[g, slot] = sp.astype(BF16)
            z_ref[g, slot] = z2

    def b_issue(n, slot):
        return [_dot(neg_tri, hl_ref[g, slot]) for g in heads]

    def b_finish(n, slot, laters):
        for g in heads:
            arg_ref[g, slot] = z_ref[g, slot] + laters[g] + c_ref[g]
            c_ref[g] += laters[g][0:1, :]

    def c_issue(n, slot):
        ks = key_start(n)
        return [_dot(vt_ref[g * d:(g + 1) * d, pl.ds(ks, tk)], jnp.exp2(arg_ref[g, slot]).astype(BF16))
                for g in heads]

    def c_finish(n, slot, pvs):
        for g in heads:
            acc_ref[g] += pvs[g]

    _three_stage(i, (a_issue, a_finish), (b_issue, b_finish), (c_issue, c_finish))
    for g in heads:
        o_ref[:, g * d:(g + 1) * d] = acc_ref[g].T


def _sb_attn(qk, vt, batch, seq, tq=512):
    tk = tq // 2
    nq = seq // tq
    groups = SB_HEADS // SB_GROUP
    gd = SB_GROUP * SB_DIM
    kern = functools.partial(_sb_attn_kernel, tq=tq, tk=tk)
    return pl.pallas_call(
        kern,
        grid=(batch, groups, nq),
        in_specs=[
            pl.BlockSpec((tq, gd), lambda b, h, i: (b * nq + i, h)),
            pl.BlockSpec((seq, gd), lambda b, h, i: (b, groups + h)),
            pl.BlockSpec((gd, seq), lambda b, h, i: (h, b)),
        ],
        out_specs=pl.BlockSpec((tq, gd), lambda b, h, i: (b * nq + i, h)),
        out_shape=jax.ShapeDtypeStruct((batch * seq, SB_OUT), F32),
        scratch_shapes=[
            pltpu.VMEM((SB_GROUP, 2, tk, tq), F32),
            pltpu.VMEM((SB_GROUP, 2, tk, tq), BF16),
            pltpu.VMEM((SB_GROUP, 2, tk, tq), F32),
            pltpu.VMEM((SB_GROUP, SB_DIM, tq), F32),
            pltpu.VMEM((SB_GROUP, 1, tq), F32),
        ],
        compiler_params=_params("parallel", "parallel", "arbitrary"),
        name="sb_attn",
    )(qk, qk, vt)


def _out_proj_kernel(x_ref, oa_ref, ob_ref, ga_ref, gb_ref, w_ref, y_ref):
    na = _rms(oa_ref[...], ga_ref[...]).astype(BF16)
    nb = _rms(ob_ref[...], gb_ref[...]).astype(BF16)
    y_ref[...] = x_ref[...] + _dot(na, w_ref[:MLA_OUT, :]) + _dot(nb, w_ref[MLA_OUT:, :])


def _out_proj(x, oa, ob, ga, gb, w, tm=512):
    t = x.shape[0]
    full = lambda a: pl.BlockSpec(a.shape, lambda i: (0, 0))
    return pl.pallas_call(
        _out_proj_kernel,
        grid=(t // tm,),
        in_specs=[
            pl.BlockSpec((tm, D_MODEL), lambda i: (i, 0)),
            pl.BlockSpec((tm, MLA_OUT), lambda i: (i, 0)),
            pl.BlockSpec((tm, SB_OUT), lambda i: (i, 0)),
            full(ga), full(gb), full(w),
        ],
        out_specs=pl.BlockSpec((tm, D_MODEL), lambda i: (i, 0)),
        out_shape=jax.ShapeDtypeStruct((t, D_MODEL), F32),
        compiler_params=_params("parallel"),
        name="out_proj",
    )(x, oa, ob, ga, gb, w)


def _mem_kv_kernel(mem_ref, g_ref, w_ref, kv_ref):
    kv_ref[...] = _dot(_rms(mem_ref[...], g_ref[...]).astype(BF16), w_ref[...]).astype(BF16)


def _mem_kv(mem, g, w):
    m = mem.shape[0]
    n = w.shape[1]
    full = lambda a: pl.BlockSpec(a.shape, lambda i: (0, 0))
    return pl.pallas_call(
        _mem_kv_kernel,
        grid=(1,),
        in_specs=[full(mem), full(g), full(w)],
        out_specs=pl.BlockSpec((m, n), lambda i: (0, 0)),
        out_shape=jax.ShapeDtypeStruct((m, n), BF16),
        compiler_params=_params("arbitrary"),
        name="mem_kv",
    )(mem, g, w)


GROUP_LANE0 = N_EXPERTS


def _mem_route_kernel(x_ref, gx_ref, wq_ref, kv_ref, wo_ref, gf_ref, wrh_ref, wrl_ref, br_ref,
                      x2_ref, h2_ref, route_ref):
    x1 = x_ref[...]
    q = _dot(_rms(x1, gx_ref[...]).astype(BF16), wq_ref[...]).astype(BF16)
    scale = MEM_DIM ** -0.5
    kw = MEM_HEADS * MEM_DIM
    heads = []
    for h in range(MEM_HEADS):
        lo = h * MEM_DIM
        s = _dot_nt(q[:, lo:lo + MEM_DIM], kv_ref[:, lo:lo + MEM_DIM]) * scale
        e = jnp.exp(s - jnp.max(s, axis=-1, keepdims=True))
        p = (e / jnp.sum(e, axis=-1, keepdims=True)).astype(BF16)
        heads.append(_dot(p, kv_ref[:, kw + lo:kw + lo + MEM_DIM]).astype(BF16))
    o = jnp.concatenate(heads, axis=-1)
    x2 = x1 + _dot(o, wo_ref[...])
    x2_ref[...] = x2
    h2 = _rms(x2, gf_ref[...])
    h2_ref[...] = h2

    hh, hl = _split_bf16(h2)
    lg = _dot(hh, wrh_ref[...]) + _dot(hl, wrh_ref[...]) + _dot(hh, wrl_ref[...]) + br_ref[...]
    lane = lax.broadcasted_iota(jnp.int32, lg.shape, 1)
    big = jnp.int32(1 << 20)
    ninf = -jnp.inf

    def lane_max(v):
        return jnp.max(v, axis=-1, keepdims=True)

    def first_lane(cond):
        return jnp.min(jnp.where(cond, lane, big), axis=-1, keepdims=True)

    is_g = (lane >= GROUP_LANE0) & (lane < GROUP_LANE0 + N_GROUPS)
    g_max = lane_max(jnp.where(is_g, lg, ninf))
    g_sum = jnp.sum(jnp.where(is_g, jnp.exp(lg - g_max), 0.0), axis=-1, keepdims=True)
    p_g = 1.0 / g_sum
    g_idx = first_lane(is_g & (lg == g_max)) - GROUP_LANE0
    in_grp = (lane < N_EXPERTS) & ((lane // EXPERTS_PER_GROUP) == g_idx)
    e_max = lane_max(jnp.where(in_grp, lg, ninf))
    e_sum = jnp.sum(jnp.where(in_grp, jnp.exp(lg - e_max), 0.0), axis=-1, keepdims=True)
    i1 = first_lane(in_grp & (lg == e_max))
    rest = in_grp & (lane != i1)
    e_max2 = lane_max(jnp.where(rest, lg, ninf))
    i2 = first_lane(rest & (lg == e_max2))
    p1 = 1.0 / e_sum
    p2 = jnp.exp(e_max2 - e_max) / e_sum
    den = p1 + p2
    gate1 = p_g * (p1 / den)
    gate2 = p_g * (p2 / den)
    route = jnp.where(lane == 0, i1.astype(F32),
                      jnp.where(lane == 1, i2.astype(F32),
                                jnp.where(lane == 2, gate1,
                                          jnp.where(lane == 3, gate2, 0.0))))
    route_ref[...] = route


def _mem_route(x1, gx, wq, kv, wo, gf, wrh, wrl, br, seq, tm=512):
    t = x1.shape[0]
    per_batch = seq // tm
    full = lambda a: pl.BlockSpec(a.shape, lambda i: (0, 0))
    return pl.pallas_call(
        _mem_route_kernel,
        grid=(t // tm,),
        in_specs=[
            pl.BlockSpec((tm, D_MODEL), lambda i: (i, 0)),
            full(gx), full(wq),
            pl.BlockSpec((MEM_LEN, kv.shape[1]), lambda i: (i // per_batch, 0)),
            full(wo), full(gf), full(wrh), full(wrl), full(br),
        ],
        out_specs=[
            pl.BlockSpec((tm, D_MODEL), lambda i: (i, 0)),
            pl.BlockSpec((tm, D_MODEL), lambda i: (i, 0)),
            pl.BlockSpec((tm, LANES), lambda i: (i, 0)),
        ],
        out_shape=[
            jax.ShapeDtypeStruct((t, D_MODEL), F32),
            jax.ShapeDtypeStruct((t, D_MODEL), F32),
            jax.ShapeDtypeStruct((t, LANES), F32),
        ],
        compiler_params=_params("parallel"),
        name="mem_route",
    )(x1, gx, wq, kv, wo, gf, wrh, wrl, br)


SLOT_BLK = 256


def _slots_kernel(route_ref, slot_ref, meta_ref, cum_ref, *, n_tok):
    nblk = n_tok // SLOT_BLK
    e_iota = lax.broadcasted_iota(jnp.int32, (LANES, SLOT_BLK), 0)
    incl = (lax.broadcasted_iota(jnp.int32, (SLOT_BLK, SLOT_BLK), 0)
            <= lax.broadcasted_iota(jnp.int32, (SLOT_BLK, SLOT_BLK), 1)).astype(BF16)

    def onehots(b):
        ts = pl.multiple_of(b * SLOT_BLK, SLOT_BLK)
        ids = route_ref[pl.ds(ts, SLOT_BLK), :].T
        oh1 = e_iota == ids[0:1, :].astype(jnp.int32)
        oh2 = e_iota == ids[1:2, :].astype(jnp.int32)
        return ts, oh1, oh2

    def count(b, carry):
        ts, oh1, oh2 = onehots(b)
        cnt = (oh1 | oh2).astype(F32).astype(BF16)
        c = _dot(cnt, incl) + carry
        cum_ref[:, pl.ds(ts, SLOT_BLK)] = c
        return c[:, SLOT_BLK - 1:SLOT_BLK]

    total = lax.fori_loop(0, nblk, count, jnp.zeros((LANES, 1), F32))
    tiles = jnp.floor((total + (SLOT_TILE - 1)) * (1.0 / SLOT_TILE))
    below = (lax.broadcasted_iota(jnp.int32, (LANES, LANES), 1)
             < lax.broadcasted_iota(jnp.int32, (LANES, LANES), 0)).astype(BF16)
    tile_lo = _dot(below, jnp.broadcast_to(tiles, (LANES, LANES)).astype(BF16))
    base = tile_lo[:, 0:1] * SLOT_TILE

    def assign(b, carry):
        ts, oh1, oh2 = onehots(b)
        pos = base + cum_ref[:, pl.ds(ts, SLOT_BLK)] - 1.0
        s1 = jnp.sum(jnp.where(oh1, pos, 0.0), axis=0, keepdims=True)
        s2 = jnp.sum(jnp.where(oh2, pos, 0.0), axis=0, keepdims=True)
        row = lax.broadcasted_iota(jnp.int32, (8, SLOT_BLK), 0)
        slot_ref[:, pl.ds(ts, SLOT_BLK)] = jnp.where(row == 0, s1, jnp.where(row == 1, s2, 0.0)).astype(jnp.int32)
        return carry

    lax.fori_loop(0, nblk, assign, 0)

    tile_hi = tile_lo + tiles
    tix = lax.broadcasted_iota(jnp.int32, (LANES, LANES), 1).astype(F32)
    is_e = lax.broadcasted_iota(jnp.int32, (LANES, LANES), 0) < N_EXPERTS
    owner = jnp.sum(jnp.where(is_e & (tile_hi <= tix), 1.0, 0.0), axis=0, keepdims=True)
    n_tiles = jnp.max(jnp.where(is_e, tile_hi, 0.0), axis=0, keepdims=True)
    valid = tix[0:1, :] < n_tiles
    last_owner = jnp.max(jnp.where(valid, owner, 0.0), axis=1, keepdims=True)
    owner = jnp.where(valid, owner, last_owner)
    eye = (lax.broadcasted_iota(jnp.int32, (LANES, LANES), 0)
           == lax.broadcasted_iota(jnp.int32, (LANES, LANES), 1))
    to_lanes = lambda colvec: jnp.sum(jnp.where(eye, colvec, 0.0), axis=0, keepdims=True)
    pad_first = to_lanes(base + total)
    pad_end = to_lanes(base + tiles * SLOT_TILE)
    row = lax.broadcasted_iota(jnp.int32, (8, LANES), 0)
    meta = jnp.where(row == 0, owner,
                     jnp.where(row == 1, valid.astype(F32),
                               jnp.where(row == 2, pad_first, jnp.where(row == 3, pad_end, 0.0))))
    meta_ref[...] = meta.astype(jnp.int32)


def _slots(route):
    t = route.shape[0]
    kern = functools.partial(_slots_kernel, n_tok=t)
    return pl.pallas_call(
        kern,
        grid=(1,),
        in_specs=[pl.BlockSpec(route.shape, lambda i: (0, 0))],
        out_specs=[
            pl.BlockSpec((8, t), lambda i: (0, 0)),
            pl.BlockSpec((8, LANES), lambda i: (0, 0)),
        ],
        out_shape=[
            jax.ShapeDtypeStruct((8, t), jnp.int32),
            jax.ShapeDtypeStruct((8, LANES), jnp.int32),
        ],
        scratch_shapes=[pltpu.VMEM((LANES, t), F32)],
        compiler_params=_params("arbitrary"),
        name="slots",
    )(route)


def _row_copy(src_ref, src_row, dst_ref, dst_row, sem):
    return pltpu.make_async_copy(src_ref.at[pl.ds(src_row, 1), :], dst_ref.at[pl.ds(dst_row, 1), :], sem)


def _dispatch_kernel(s1_ref, s2_ref, pad_first_ref, pad_end_ref, h_ref, xs_ref, zero_ref, sem, pad_sem, *, tm):
    i = pl.program_id(0)
    t0 = i * tm

    def copies(r):
        return (_row_copy(h_ref, r, xs_ref, s1_ref[t0 + r], sem),
                _row_copy(h_ref, r, xs_ref, s2_ref[t0 + r], sem))

    def issue(r, c):
        first, second = copies(r)
        first.start(priority=0)
        second.start(priority=1)
        return c

    lax.fori_loop(0, tm, issue, 0, unroll=8)

    @pl.when(i == pl.num_programs(0) - 1)
    def _():
        zero_ref[...] = jnp.zeros(zero_ref.shape, F32)

        def pad_copy(slot):
            return _row_copy(zero_ref, 0, xs_ref, slot, pad_sem)

        def fill(e, c):
            def one(s, cc):
                pad_copy(s).start()
                return cc

            return lax.fori_loop(pad_first_ref[e], pad_end_ref[e], one, c)

        lax.fori_loop(0, N_EXPERTS, fill, 0)

        def settle(e, c):
            def one(s, cc):
                pad_copy(s).wait()
                return cc

            return lax.fori_loop(pad_first_ref[e], pad_end_ref[e], one, c)

        lax.fori_loop(0, N_EXPERTS, settle, 0)

    def drain(r, c):
        for cp in copies(r):
            cp.wait()
        return c

    lax.fori_loop(0, tm, drain, 0, unroll=8)


def _dispatch(slot1, slot2, pad_first, pad_end, h2, n_slots, tm=256):
    t = h2.shape[0]
    kern = functools.partial(_dispatch_kernel, tm=tm)
    return pl.pallas_call(
        kern,
        grid_spec=pltpu.PrefetchScalarGridSpec(
            num_scalar_prefetch=4,
            grid=(t // tm,),
            in_specs=[pl.BlockSpec((tm, D_MODEL), lambda i, *_: (i, 0))],
            out_specs=pl.BlockSpec(memory_space=pl.ANY),
            scratch_shapes=[
                pltpu.VMEM((8, D_MODEL), F32),
                pltpu.SemaphoreType.DMA(()),
                pltpu.SemaphoreType.DMA(()),
            ],
        ),
        out_shape=jax.ShapeDtypeStruct((n_slots, D_MODEL), F32),
        compiler_params=_params("arbitrary"),
        name="dispatch",
    )(slot1, slot2, pad_first, pad_end, h2)


def _experts_kernel(own_ref, valid_ref, xs_ref, wg_hbm, wu_hbm, wd_hbm, ys_ref,
                    wgf, wuf, wdf, wgb, wub, wdb, wsem, wslot_ref):
    i = pl.program_id(0)
    nt = pl.num_programs(0)
    at = lambda ref, j: ref[jnp.minimum(j, nt - 1)]

    def weight_copies(e, s):
        return (pltpu.make_async_copy(wg_hbm.at[e], wgf.at[s], wsem.at[s]),
                pltpu.make_async_copy(wu_hbm.at[e], wuf.at[s], wsem.at[s]),
                pltpu.make_async_copy(wd_hbm.at[e], wdf.at[s], wsem.at[s]))

    @pl.when(i == 0)
    def _():
        wslot_ref[0] = 0
        for cp in weight_copies(own_ref[0], 0):
            cp.start()

    @pl.when(valid_ref[i] > 0)
    def _():
        e = own_ref[i]

        @pl.when((i == 0) | (own_ref[jnp.maximum(i - 1, 0)] != e))
        def _():
            s = wslot_ref[0]
            for cp in weight_copies(e, s):
                cp.wait()
            j = lax.while_loop(lambda j: (j < nt) & (at(own_ref, j) == e), lambda j: j + 1, i + 1)

            @pl.when((j < nt) & (at(valid_ref, j) > 0))
            def _():
                for cp in weight_copies(at(own_ref, j), 1 - s):
                    cp.start()

            wgb[...] = wgf[s].astype(BF16)
            wub[...] = wuf[s].astype(BF16)
            wdb[...] = wdf[s].astype(BF16)
            wslot_ref[0] = 1 - s

        x = xs_ref[...].astype(BF16)
        a = _dot(x, wgb[...])
        u = _dot(x, wub[...])
        act = (a * jax.nn.sigmoid(a) * u).astype(BF16)
        ys_ref[...] = _dot(act, wdb[...])

    @pl.when(valid_ref[i] == 0)
    def _():
        ys_ref[...] = jnp.zeros(ys_ref.shape, F32)


def _experts(own, valid, xs, wg, wu, wd):
    n_tiles = xs.shape[0] // SLOT_TILE
    hbm = pl.BlockSpec(memory_space=pl.ANY)
    tile = (SLOT_TILE, D_MODEL)
    return pl.pallas_call(
        _experts_kernel,
        grid_spec=pltpu.PrefetchScalarGridSpec(
            num_scalar_prefetch=2,
            grid=(n_tiles,),
            in_specs=[
                pl.BlockSpec(tile, lambda i, o, v: (jnp.where(v[i] > 0, i, 0), 0)),
                hbm, hbm, hbm,
            ],
            out_specs=pl.BlockSpec(tile, lambda i, o, v: (i, 0)),
            scratch_shapes=[
                pltpu.VMEM((2, D_MODEL, D_EXPERT), F32),
                pltpu.VMEM((2, D_MODEL, D_EXPERT), F32),
                pltpu.VMEM((2, D_EXPERT, D_MODEL), F32),
                pltpu.VMEM((D_MODEL, D_EXPERT), BF16),
                pltpu.VMEM((D_MODEL, D_EXPERT), BF16),
                pltpu.VMEM((D_EXPERT, D_MODEL), BF16),
                pltpu.SemaphoreType.DMA((2,)),
                pltpu.SMEM((1,), jnp.int32),
            ],
        ),
        out_shape=jax.ShapeDtypeStruct(xs.shape, F32),
        compiler_params=_params("arbitrary"),
        name="experts",
    )(own, valid, xs, wg, wu, wd)


def _combine_kernel(s1_ref, s2_ref, x_ref, route_ref, g_ref, ys_ref, y_ref, buf_ref, sem, *, tm):
    i = pl.program_id(0)
    n = pl.num_programs(0)

    def copies(tile, p, r):
        t = tile * tm + r
        return (_row_copy(ys_ref, s1_ref[t], buf_ref.at[p, 0], r, sem.at[p]),
                _row_copy(ys_ref, s2_ref[t], buf_ref.at[p, 1], r, sem.at[p]))

    def gather_start(tile, p):
        def issue(r, c):
            first, second = copies(tile, p, r)
            first.start(priority=0)
            second.start(priority=1)
            return c

        lax.fori_loop(0, tm, issue, 0, unroll=8)

    def gather_wait(tile, p):
        def drain(r, c):
            for cp in copies(tile, p, r):
                cp.wait()
            return c

        lax.fori_loop(0, tm, drain, 0, unroll=8)

    @pl.when(i == 0)
    def _():
        gather_start(0, 0)

    p = i % 2

    @pl.when(i + 1 < n)
    def _():
        gather_start(i + 1, 1 - p)

    gather_wait(i, p)
    y = x_ref[...] + route_ref[:, 2:3] * buf_ref[p, 0] + route_ref[:, 3:4] * buf_ref[p, 1]
    y_ref[...] = _rms(y, g_ref[...])


def _combine(slot1, slot2, x2, route, g, ys, tm=256):
    t = x2.shape[0]
    kern = functools.partial(_combine_kernel, tm=tm)
    return pl.pallas_call(
        kern,
        grid_spec=pltpu.PrefetchScalarGridSpec(
            num_scalar_prefetch=2,
            grid=(t // tm,),
            in_specs=[
                pl.BlockSpec((tm, D_MODEL), lambda i, s1, s2: (i, 0)),
                pl.BlockSpec((tm, LANES), lambda i, s1, s2: (i, 0)),
                pl.BlockSpec((1, D_MODEL), lambda i, s1, s2: (0, 0)),
                pl.BlockSpec(memory_space=pl.ANY),
            ],
            out_specs=pl.BlockSpec((tm, D_MODEL), lambda i, s1, s2: (i, 0)),
            scratch_shapes=[
                pltpu.VMEM((2, 2, tm, D_MODEL), F32),
                pltpu.SemaphoreType.DMA((2,)),
            ],
        ),
        out_shape=jax.ShapeDtypeStruct((t, D_MODEL), F32),
        compiler_params=_params("arbitrary"),
        name="combine",
    )(slot1, slot2, x2, route, g, ys)


def _swap_halves(w):
    half = w.shape[-1] // 2
    return jnp.concatenate([-w[..., half:], w[..., :half]], axis=-1)


def _row(v):
    return v.reshape(1, -1).astype(F32)


def kernel(x, mem, positions, norm_mix, w_in, norm_q_lat, w_q_b, norm_kv_lat, w_kv_b, norm_mla_out, norm_sb_out, w_out, norm_mem_x, norm_mem_src, w_mem_q, w_mem_kv, w_mem_o, norm_ffn, w_group, b_group, w_expert_router, b_expert, w_gate, w_up, w_down, norm_final):
    batch, seq, d = x.shape
    t = batch * seq
    depth = w_in.shape[0]
    assert depth == 1, "single-layer trunk only"
    xt = x.reshape(t, d)
    pos = positions.reshape(t, 1)
    inv_freq = ROPE_THETA ** (-jnp.arange(0, MLA_ROPE, 2, dtype=F32) / MLA_ROPE)
    invf = jnp.concatenate([inv_freq, inv_freq, jnp.zeros((LANES - MLA_ROPE,), F32)]).reshape(1, LANES)
    n_slots = (2 * t // SLOT_TILE + N_EXPERTS) * SLOT_TILE

    for l in range(depth):
        wi_t = w_in[l].T
        lat_w = MLA_Q_RANK + MLA_KV_RANK
        w_kpe_t = wi_t[lat_w:lat_w + MLA_ROPE]
        zpad = jnp.zeros((LANES - MLA_ROPE, d), F32)
        sb0 = lat_w + MLA_ROPE
        half = MLA_ROPE // 2
        w_kpe_swapped_t = jnp.concatenate([-w_kpe_t[half:], w_kpe_t[:half]], axis=0)
        w_lat_t = jnp.concatenate([wi_t[:sb0], zpad, w_kpe_swapped_t, zpad], axis=0).astype(BF16)
        sb_scale = jnp.concatenate([jnp.full((SB_OUT,), SB_DIM ** -0.5 * LOG2E, F32), jnp.ones((2 * SB_OUT,), F32)])
        w_sb_t = (wi_t[sb0:] * sb_scale[:, None]).astype(BF16)
        wq = w_q_b[l].reshape(MLA_Q_RANK, MLA_HEADS, MLA_QK) * (MLA_QK ** -0.5 * LOG2E)
        wq_pe = wq[:, :, MLA_NOPE:]
        zq = jnp.zeros((MLA_Q_RANK, MLA_HEADS, MLA_PAD - MLA_QK), F32)
        wqa = jnp.concatenate([wq, zq], axis=-1).reshape(MLA_Q_RANK, MLA_HEADS * MLA_PAD).astype(BF16)
        wqb = jnp.concatenate([_swap_halves(wq_pe), zq], axis=-1).reshape(MLA_Q_RANK, MLA_HEADS * LANES).astype(BF16)
        wkv = w_kv_b[l].reshape(MLA_KV_RANK, MLA_HEADS, MLA_NOPE + MLA_V)
        wk = wkv[:, :, :MLA_NOPE].reshape(MLA_KV_RANK, MLA_OUT).astype(BF16)
        wvt = wkv[:, :, MLA_NOPE:].reshape(MLA_KV_RANK, MLA_OUT).T.astype(BF16)
        w_router = jnp.concatenate(
            [w_expert_router[l], w_group[l], jnp.zeros((d, LANES - N_EXPERTS - N_GROUPS), F32)], axis=1)
        wr_hi = w_router.astype(BF16)
        wr_lo = (w_router - wr_hi.astype(F32)).astype(BF16)
        b_router = jnp.concatenate(
            [b_expert[l].astype(F32), b_group[l].astype(F32), jnp.zeros((LANES - N_EXPERTS - N_GROUPS,), F32)]
        ).reshape(1, LANES)

        lat, sb_qk, sb_vt = _proj_in(xt, _row(norm_mix[l]), w_lat_t, w_sb_t)
        q, k, vt = _mla_proj(lat, pos, invf, _row(norm_q_lat[l]), _row(norm_kv_lat[l]), wqa, wqb, wk, wvt)
        o_mla = _mla_attn(q, k, vt, batch, seq)
        o_sb = _sb_attn(sb_qk, sb_vt, batch, seq)
        x1 = _out_proj(xt, o_mla, o_sb, _row(norm_mla_out[l]), _row(norm_sb_out[l]), w_out[l].astype(BF16))

        kv = _mem_kv(mem.reshape(batch * MEM_LEN, d), _row(norm_mem_src[l]), w_mem_kv[l].astype(BF16))
        x2, h2, route = _mem_route(x1, _row(norm_mem_x[l]), w_mem_q[l].astype(BF16), kv,
                                   w_mem_o[l].astype(BF16), _row(norm_ffn[l]), wr_hi, wr_lo, b_router, seq)

        slots, meta = _slots(route)
        xs = _dispatch(slots[0], slots[1], meta[2], meta[3], h2, n_slots)
        ys = _experts(meta[0], meta[1], xs, w_gate[l], w_up[l], w_down[l])
        xt = _combine(slots[0], slots[1], x2, route, _row(norm_final), ys)
    return xt.reshape(batch, seq, d)
```

```python
import functools

import jax
import jax.numpy as jnp
from jax import lax
from jax.experimental import pallas as pl
from jax.experimental.pallas import tpu as pltpu

F32 = jnp.float32
BF16 = jnp.bfloat16

EPS = 1e-6
ROPE_THETA = 10000.0

D_MODEL = 2048
MEM_LEN = 256
MLA_HEADS = 8
MLA_NOPE = 128
MLA_ROPE = 64
MLA_QK = MLA_NOPE + MLA_ROPE
MLA_V = 128
MLA_Q_RANK = 512
MLA_KV_RANK = 256
MLA_PAD = 256
SB_HEADS = 8
SB_DIM = 128
MLA_OUT = MLA_HEADS * MLA_V
SB_OUT = SB_HEADS * SB_DIM
MEM_HEADS = 4
MEM_DIM = 128
N_GROUPS = 4
EXPERTS_PER_GROUP = 8
N_EXPERTS = N_GROUPS * EXPERTS_PER_GROUP
D_EXPERT = 512

LANES = 128
LAT_COLS = 1024
IN_COLS_PAD = LAT_COLS + 3 * SB_OUT

SLOT_TILE = 256
VMEM_LIMIT = 56 * 1024 * 1024


def _rms(x, g):
    return x * lax.rsqrt(jnp.mean(x * x, axis=-1, keepdims=True) + EPS) * g


def _dot(a, b):
    return jnp.dot(a, b, preferred_element_type=F32)


def _dot_nt(a, b):
    return lax.dot_general(a, b, (((1,), (1,)), ((), ())), preferred_element_type=F32)


def _split_bf16(x):
    hi = x.astype(BF16)
    lo = (x - hi.astype(F32)).astype(BF16)
    return hi, lo


def _params(*sem):
    return pltpu.CompilerParams(dimension_semantics=sem, vmem_limit_bytes=VMEM_LIMIT)


def _proj_in_kernel(x_ref, g_ref, wl_ref, ws_ref, lat_ref, qk_ref, vt_ref, h_ref):
    j = pl.program_id(1)
    last = pl.num_programs(1) - 1

    @pl.when(j == 0)
    def _():
        h_ref[...] = _rms(x_ref[...], g_ref[...]).astype(BF16)
        lat_ref[...] = _dot_nt(h_ref[...], wl_ref[...])

    @pl.when((j > 0) & (j < last))
    def _():
        qk_ref[...] = _dot_nt(h_ref[...], ws_ref[...]).astype(BF16)

    @pl.when(j == last)
    def _():
        vt_ref[...] = _dot_nt(ws_ref[...], h_ref[...]).astype(BF16)


def _proj_in(x, g, w_lat, w_sb, tm=512):
    t = x.shape[0]
    tn = LAT_COLS
    n_sb = w_sb.shape[0] // tn
    return pl.pallas_call(
        _proj_in_kernel,
        grid=(t // tm, n_sb + 1),
        in_specs=[
            pl.BlockSpec((tm, D_MODEL), lambda i, j: (i, 0)),
            pl.BlockSpec((1, D_MODEL), lambda i, j: (0, 0)),
            pl.BlockSpec((tn, D_MODEL), lambda i, j: (0, 0)),
            pl.BlockSpec((tn, D_MODEL), lambda i, j: (jnp.maximum(j - 1, 0), 0)),
        ],
        out_specs=[
            pl.BlockSpec((tm, tn), lambda i, j: (i, 0)),
            pl.BlockSpec((tm, tn), lambda i, j: (i, jnp.clip(j - 1, 0, n_sb - 2))),
            pl.BlockSpec((SB_OUT, tm), lambda i, j: (0, i)),
        ],
        out_shape=[
            jax.ShapeDtypeStruct((t, LAT_COLS), F32),
            jax.ShapeDtypeStruct((t, 2 * SB_OUT), BF16),
            jax.ShapeDtypeStruct((SB_OUT, t), BF16),
        ],
        scratch_shapes=[pltpu.VMEM((tm, D_MODEL), BF16)],
        compiler_params=_params("parallel", "arbitrary"),
        name="proj_in",
    )(x, g, w_lat, w_sb)


def _mla_proj_kernel(lat_ref, pos_ref, invf_ref, gq_ref, gkv_ref, wqa_ref, wqb_ref, wk_ref, wvt_ref,
                     q_ref, k_ref, vt_ref):
    cq = _rms(lat_ref[:, :MLA_Q_RANK], gq_ref[...]).astype(BF16)
    ckv = _rms(lat_ref[:, MLA_Q_RANK:MLA_Q_RANK + MLA_KV_RANK], gkv_ref[...]).astype(BF16)
    ang = pos_ref[...].astype(F32) * invf_ref[...]
    live = lax.broadcasted_iota(jnp.int32, ang.shape, 1) < MLA_ROPE
    cos2 = jnp.where(live, jnp.cos(ang), 0.0)
    sin2 = jnp.where(live, jnp.sin(ang), 0.0)

    qa = _dot(cq, wqa_ref[...])
    qb = _dot(cq, wqb_ref[...])
    kn = _dot(ckv, wk_ref[...])
    vt_ref[...] = _dot_nt(wvt_ref[...], ckv).astype(BF16)
    k_pe = (lat_ref[:, 768:896] * cos2 + lat_ref[:, 896:1024] * sin2).astype(BF16)
    for h in range(MLA_HEADS):
        lo = h * MLA_PAD
        mid = lo + LANES
        q_ref[:, lo:mid] = qa[:, lo:mid].astype(BF16)
        q_ref[:, mid:mid + LANES] = (qa[:, mid:mid + LANES] * cos2
                                     + qb[:, h * LANES:(h + 1) * LANES] * sin2).astype(BF16)
        k_ref[:, lo:mid] = kn[:, h * LANES:(h + 1) * LANES].astype(BF16)
        k_ref[:, mid:mid + LANES] = k_pe


def _mla_proj(lat, pos, invf, gq, gkv, wqa, wqb, wk, wvt, tm=512):
    t = lat.shape[0]
    full = lambda a: pl.BlockSpec(a.shape, lambda i: (0, 0))
    return pl.pallas_call(
        _mla_proj_kernel,
        grid=(t // tm,),
        in_specs=[
            pl.BlockSpec((tm, LAT_COLS), lambda i: (i, 0)),
            pl.BlockSpec((tm, 1), lambda i: (i, 0)),
            full(invf), full(gq), full(gkv), full(wqa), full(wqb), full(wk), full(wvt),
        ],
        out_specs=[
            pl.BlockSpec((tm, MLA_HEADS * MLA_PAD), lambda i: (i, 0)),
            pl.BlockSpec((tm, MLA_HEADS * MLA_PAD), lambda i: (i, 0)),
            pl.BlockSpec((MLA_OUT, tm), lambda i: (0, i)),
        ],
        out_shape=[
            jax.ShapeDtypeStruct((t, MLA_HEADS * MLA_PAD), BF16),
            jax.ShapeDtypeStruct((t, MLA_HEADS * MLA_PAD), BF16),
            jax.ShapeDtypeStruct((MLA_OUT, t), BF16),
        ],
        compiler_params=_params("parallel"),
        name="mla_proj",
    )(lat, pos, invf, gq, gkv, wqa, wqb, wk, wvt)


LOG2E = 1.4426950408889634


def _three_stage(n_pairs, stage_a, stage_b, stage_c):
    a_issue, a_finish = stage_a
    b_issue, b_finish = stage_b
    c_issue, c_finish = stage_c

    def run_a(n, slot, first):
        a_finish(n, slot, a_issue(n, slot, first), first)

    run_a(0, 0, True)
    run_a(1, 1, True)
    b_finish(0, 0, b_issue(0, 0))

    def half(na, sa, nb, sb, nc, sc):
        ra = a_issue(na, sa, False)
        rb = b_issue(nb, sb)
        rc = c_issue(nc, sc)
        a_finish(na, sa, ra, False)
        b_finish(nb, sb, rb)
        c_finish(nc, sc, rc)

    def body(p, carry):
        half(2 * p, 0, 2 * p - 1, 1, 2 * p - 2, 0)
        half(2 * p + 1, 1, 2 * p, 0, 2 * p - 1, 1)
        return carry

    lax.fori_loop(1, n_pairs + 1, body, 0)
    last = 2 * n_pairs + 1
    rb = b_issue(last, 1)
    rc = c_issue(last - 1, 0)
    b_finish(last, 1, rb)
    c_finish(last - 1, 0, rc)
    c_finish(last, 1, c_issue(last, 1))


def _two_stage(n_pairs, produce, consume):
    produce(0, 0, True)
    produce(1, 1, True)
    consume(0, 0)

    def body(p, carry):
        produce(2 * p, 0, False)
        consume(2 * p - 1, 1)
        produce(2 * p + 1, 1, False)
        consume(2 * p, 0)
        return carry

    lax.fori_loop(1, n_pairs + 1, body, 0)
    consume(2 * n_pairs + 1, 1)


MLA_SUM_ROWS = 16
MLA_GROUP = 8


def _mla_attn_kernel(q_ref, k_ref, vt_ref, o_ref, s_ref, m_ref, acc_ref, *, tq, tk):
    i = pl.program_id(2)
    heads = range(MLA_GROUP)
    dq, dv = MLA_PAD, MLA_V
    m_ref[...] = jnp.full(m_ref.shape, -jnp.inf, F32)
    acc_ref[...] = jnp.zeros(acc_ref.shape, F32)
    ones = jnp.ones((MLA_SUM_ROWS, tk), BF16)

    def key_start(n):
        tile = jnp.where(n < 2, 2 * i + n, 2 * i + 1 - n)
        return pl.multiple_of(tile * tk, tk)

    def produce(n, slot, diagonal):
        ks = key_start(n)
        for g in heads:
            st = _dot_nt(k_ref[pl.ds(ks, tk), g * dq:(g + 1) * dq], q_ref[:, g * dq:(g + 1) * dq])
            if diagonal:
                kpos = ks + lax.broadcasted_iota(jnp.int32, st.shape, 0)
                qpos = i * tq + lax.broadcasted_iota(jnp.int32, st.shape, 1)
                st = jnp.where(kpos <= qpos, st, -jnp.inf)
            s_ref[g, slot] = st

    def consume(n, slot):
        ks = key_start(n)
        for g in heads:
            st = s_ref[g, slot]
            m_old = m_ref[g]
            m_new = jnp.maximum(m_old, jnp.max(st, axis=0, keepdims=True))
            alpha = jnp.exp2(m_old - m_new)
            p = jnp.exp2(st - m_new).astype(BF16)
            v_ones = jnp.concatenate([vt_ref[g * dv:(g + 1) * dv, pl.ds(ks, tk)], ones], axis=0)
            acc_ref[g] = alpha * acc_ref[g] + _dot(v_ones, p)
            m_ref[g] = m_new

    _two_stage(i, produce, consume)
    for g in heads:
        o_ref[:, g * dv:(g + 1) * dv] = (acc_ref[g, :dv, :] / acc_ref[g, dv:dv + 1, :]).T


def _mla_attn(q, k, vt, batch, seq, tq=512):
    tk = tq // 2
    nq = seq // tq
    groups = MLA_HEADS // MLA_GROUP
    kern = functools.partial(_mla_attn_kernel, tq=tq, tk=tk)
    return pl.pallas_call(
        kern,
        grid=(batch, groups, nq),
        in_specs=[
            pl.BlockSpec((tq, MLA_GROUP * MLA_PAD), lambda b, h, i: (b * nq + i, h)),
            pl.BlockSpec((seq, MLA_GROUP * MLA_PAD), lambda b, h, i: (b, h), pipeline_mode=pl.Buffered(1)),
            pl.BlockSpec((MLA_GROUP * MLA_V, seq), lambda b, h, i: (h, b), pipeline_mode=pl.Buffered(1)),
        ],
        out_specs=pl.BlockSpec((tq, MLA_GROUP * MLA_V), lambda b, h, i: (b * nq + i, h)),
        out_shape=jax.ShapeDtypeStruct((batch * seq, MLA_OUT), F32),
        scratch_shapes=[
            pltpu.VMEM((MLA_GROUP, 2, tk, tq), F32),
            pltpu.VMEM((MLA_GROUP, 1, tq), F32),
            pltpu.VMEM((MLA_GROUP, MLA_V + MLA_SUM_ROWS, tq), F32),
        ],
        compiler_params=_params("parallel", "parallel", "arbitrary"),
        name="mla_attn",
    )(q, k, vt)


SB_EXP_CLAMP = 126.0
SB_GROUP = 4


def _sb_attn_kernel(q_ref, k_ref, vt_ref, o_ref, z_ref, hl_ref, arg_ref, acc_ref, c_ref, *, tq, tk):
    i = pl.program_id(2)
    heads = range(SB_GROUP)
    d = SB_DIM
    acc_ref[...] = jnp.zeros(acc_ref.shape, F32)
    c_ref[...] = jnp.zeros(c_ref.shape, F32)
    col = lax.broadcasted_iota(jnp.int32, (tk, tk), 1)
    row = lax.broadcasted_iota(jnp.int32, (tk, tk), 0)
    neg_tri = jnp.where(col >= row, -1.0, 0.0).astype(BF16)

    def key_start(n):
        return pl.multiple_of((2 * i + 1 - n) * tk, tk)

    def a_issue(n, slot, masked):
        ks = key_start(n)
        return [_dot_nt(k_ref[pl.ds(ks, tk), g * d:(g + 1) * d], q_ref[:, g * d:(g + 1) * d])
                for g in heads]

    def a_finish(n, slot, z2s, masked):
        for g in heads:
            z2 = z2s[g]
            sp = jnp.maximum(z2, jnp.log2(1.0 + jnp.exp2(jnp.minimum(z2, SB_EXP_CLAMP))))
            if masked:
                kpos = key_start(n) + lax.broadcasted_iota(jnp.int32, z2.shape, 0)
                qpos = i * tq + lax.broadcasted_iota(jnp.int32, z2.shape, 1)
                mask = kpos < qpos
                sp = jnp.where(mask, sp, 0.0)
                z2 = jnp.where(mask, z2, -jnp.inf)
            hl_ref[g, slot] = sp.astype(BF16)
            z_ref[g, slot] = z2

    def b_issue(n, slot):
        return [_dot(neg_tri, hl_ref[g, slot]) for g in heads]

    def b_finish(n, slot, laters):
        for g in heads:
            arg_ref[g, slot] = z_ref[g, slot] + laters[g] + c_ref[g]
            c_ref[g] += laters[g][0:1, :]

    def c_issue(n, slot):
        ks = key_start(n)
        return [_dot(vt_ref[g * d:(g + 1) * d, pl.ds(ks, tk)], jnp.exp2(arg_ref[g, slot]).astype(BF16))
                for g in heads]

    def c_finish(n, slot, pvs):
        for g in heads:
            acc_ref[g] += pvs[g]

    _three_stage(i, (a_issue, a_finish), (b_issue, b_finish), (c_issue, c_finish))
    for g in heads:
        o_ref[:, g * d:(g + 1) * d] = acc_ref[g].T


def _sb_attn(qk, vt, batch, seq, tq=512):
    tk = tq // 2
    nq = seq // tq
    groups = SB_HEADS // SB_GROUP
    gd = SB_GROUP * SB_DIM
    kern = functools.partial(_sb_attn_kernel, tq=tq, tk=tk)
    return pl.pallas_call(
        kern,
        grid=(batch, groups, nq),
        in_specs=[
            pl.BlockSpec((tq, gd), lambda b, h, i: (b * nq + i, h)),
            pl.BlockSpec((seq, gd), lambda b, h, i: (b, groups + h)),
            pl.BlockSpec((gd, seq), lambda b, h, i: (h, b)),
        ],
        out_specs=pl.BlockSpec((tq, gd), lambda b, h, i: (b * nq + i, h)),
        out_shape=jax.ShapeDtypeStruct((batch * seq, SB_OUT), F32),
        scratch_shapes=[
            pltpu.VMEM((SB_GROUP, 2, tk, tq), F32),
            pltpu.VMEM((SB_GROUP, 2, tk, tq), BF16),
            pltpu.VMEM((SB_GROUP, 2, tk, tq), F32),
            pltpu.VMEM((SB_GROUP, SB_DIM, tq), F32),
            pltpu.VMEM((SB_GROUP, 1, tq), F32),
        ],
        compiler_params=_params("parallel", "parallel", "arbitrary"),
        name="sb_attn",
    )(qk, qk, vt)


def _out_proj_kernel(x_ref, oa_ref, ob_ref, ga_ref, gb_ref, w_ref, y_ref):
    na = _rms(oa_ref[...], ga_ref[...]).astype(BF16)
    nb = _rms(ob_ref[...], gb_ref[...]).astype(BF16)
    y_ref[...] = x_ref[...] + _dot(na, w_ref[:MLA_OUT, :]) + _dot(nb, w_ref[MLA_OUT:, :])


def _out_proj(x, oa, ob, ga, gb, w, tm=512):
    t = x.shape[0]
    full = lambda a: pl.BlockSpec(a.shape, lambda i: (0, 0))
    return pl.pallas_call(
        _out_proj_kernel,
        grid=(t // tm,),
        in_specs=[
            pl.BlockSpec((tm, D_MODEL), lambda i: (i, 0)),
            pl.BlockSpec((tm, MLA_OUT), lambda i: (i, 0)),
            pl.BlockSpec((tm, SB_OUT), lambda i: (i, 0)),
            full(ga), full(gb), full(w),
        ],
        out_specs=pl.BlockSpec((tm, D_MODEL), lambda i: (i, 0)),
        out_shape=jax.ShapeDtypeStruct((t, D_MODEL), F32),
        compiler_params=_params("parallel"),
        name="out_proj",
    )(x, oa, ob, ga, gb, w)


def _mem_kv_kernel(mem_ref, g_ref, w_ref, kv_ref):
    kv_ref[...] = _dot(_rms(mem_ref[...], g_ref[...]).astype(BF16), w_ref[...]).astype(BF16)


def _mem_kv(mem, g, w):
    m = mem.shape[0]
    n = w.shape[1]
    full = lambda a: pl.BlockSpec(a.shape, lambda i: (0, 0))
    return pl.pallas_call(
        _mem_kv_kernel,
        grid=(1,),
        in_specs=[full(mem), full(g), full(w)],
        out_specs=pl.BlockSpec((m, n), lambda i: (0, 0)),
        out_shape=jax.ShapeDtypeStruct((m, n), BF16),
        compiler_params=_params("arbitrary"),
        name="mem_kv",
    )(mem, g, w)


GROUP_LANE0 = N_EXPERTS


def _mem_route_kernel(x_ref, gx_ref, wq_ref, kv_ref, wo_ref, gf_ref, wrh_ref, wrl_ref, br_ref,
                      x2_ref, h2_ref, route_ref):
    x1 = x_ref[...]
    q = _dot(_rms(x1, gx_ref[...]).astype(BF16), wq_ref[...]).astype(BF16)
    scale = MEM_DIM ** -0.5
    kw = MEM_HEADS * MEM_DIM
    heads = []
    for h in range(MEM_HEADS):
        lo = h * MEM_DIM
        s = _dot_nt(q[:, lo:lo + MEM_DIM], kv_ref[:, lo:lo + MEM_DIM]) * scale
        e = jnp.exp(s - jnp.max(s, axis=-1, keepdims=True))
        p = (e / jnp.sum(e, axis=-1, keepdims=True)).astype(BF16)
        heads.append(_dot(p, kv_ref[:, kw + lo:kw + lo + MEM_DIM]).astype(BF16))
    o = jnp.concatenate(heads, axis=-1)
    x2 = x1 + _dot(o, wo_ref[...])
    x2_ref[...] = x2
    h2 = _rms(x2, gf_ref[...])
    h2_ref[...] = h2

    hh, hl = _split_bf16(h2)
    lg = _dot(hh, wrh_ref[...]) + _dot(hl, wrh_ref[...]) + _dot(hh, wrl_ref[...]) + br_ref[...]
    lane = lax.broadcasted_iota(jnp.int32, lg.shape, 1)
    big = jnp.int32(1 << 20)
    ninf = -jnp.inf

    def lane_max(v):
        return jnp.max(v, axis=-1, keepdims=True)

    def first_lane(cond):
        return jnp.min(jnp.where(cond, lane, big), axis=-1, keepdims=True)

    is_g = (lane >= GROUP_LANE0) & (lane < GROUP_LANE0 + N_GROUPS)
    g_max = lane_max(jnp.where(is_g, lg, ninf))
    g_sum = jnp.sum(jnp.where(is_g, jnp.exp(lg - g_max), 0.0), axis=-1, keepdims=True)
    p_g = 1.0 / g_sum
    g_idx = first_lane(is_g & (lg == g_max)) - GROUP_LANE0
    in_grp = (lane < N_EXPERTS) & ((lane // EXPERTS_PER_GROUP) == g_idx)
    e_max = lane_max(jnp.where(in_grp, lg, ninf))
    e_sum = jnp.sum(jnp.where(in_grp, jnp.exp(lg - e_max), 0.0), axis=-1, keepdims=True)
    i1 = first_lane(in_grp & (lg == e_max))
    rest = in_grp & (lane != i1)
    e_max2 = lane_max(jnp.where(rest, lg, ninf))
    i2 = first_lane(rest & (lg == e_max2))
    p1 = 1.0 / e_sum
    p2 = jnp.exp(e_max2 - e_max) / e_sum
    den = p1 + p2
    gate1 = p_g * (p1 / den)
    gate2 = p_g * (p2 / den)
    route = jnp.where(lane == 0, i1.astype(F32),
                      jnp.where(lane == 1, i2.astype(F32),
                                jnp.where(lane == 2, gate1,
                                          jnp.where(lane == 3, gate2, 0.0))))
    route_ref[...] = route


def _mem_route(x1, gx, wq, kv, wo, gf, wrh, wrl, br, seq, tm=512):
    t = x1.shape[0]
    per_batch = seq // tm
    full = lambda a: pl.BlockSpec(a.shape, lambda i: (0, 0))
    return pl.pallas_call(
        _mem_route_kernel,
        grid=(t // tm,),
        in_specs=[
            pl.BlockSpec((tm, D_MODEL), lambda i: (i, 0)),
            full(gx), full(wq),
            pl.BlockSpec((MEM_LEN, kv.shape[1]), lambda i: (i // per_batch, 0)),
            full(wo), full(gf), full(wrh), full(wrl), full(br),
        ],
        out_specs=[
            pl.BlockSpec((tm, D_MODEL), lambda i: (i, 0)),
            pl.BlockSpec((tm, D_MODEL), lambda i: (i, 0)),
            pl.BlockSpec((tm, LANES), lambda i: (i, 0)),
        ],
        out_shape=[
            jax.ShapeDtypeStruct((t, D_MODEL), F32),
            jax.ShapeDtypeStruct((t, D_MODEL), F32),
            jax.ShapeDtypeStruct((t, LANES), F32),
        ],
        compiler_params=_params("parallel"),
        name="mem_route",
    )(x1, gx, wq, kv, wo, gf, wrh, wrl, br)


SLOT_BLK = 256


def _slots_kernel(route_ref, slot_ref, meta_ref, cum_ref, *, n_tok):
    nblk = n_tok // SLOT_BLK
    e_iota = lax.broadcasted_iota(jnp.int32, (LANES, SLOT_BLK), 0)
    incl = (lax.broadcasted_iota(jnp.int32, (SLOT_BLK, SLOT_BLK), 0)
            <= lax.broadcasted_iota(jnp.int32, (SLOT_BLK, SLOT_BLK), 1)).astype(BF16)

    def onehots(b):
        ts = pl.multiple_of(b * SLOT_BLK, SLOT_BLK)
        ids = route_ref[pl.ds(ts, SLOT_BLK), :].T
        oh1 = e_iota == ids[0:1, :].astype(jnp.int32)
        oh2 = e_iota == ids[1:2, :].astype(jnp.int32)
        return ts, oh1, oh2

    def count(b, carry):
        ts, oh1, oh2 = onehots(b)
        cnt = (oh1 | oh2).astype(F32).astype(BF16)
        c = _dot(cnt, incl) + carry
        cum_ref[:, pl.ds(ts, SLOT_BLK)] = c
        return c[:, SLOT_BLK - 1:SLOT_BLK]

    total = lax.fori_loop(0, nblk, count, jnp.zeros((LANES, 1), F32))
    tiles = jnp.floor((total + (SLOT_TILE - 1)) * (1.0 / SLOT_TILE))
    below = (lax.broadcasted_iota(jnp.int32, (LANES, LANES), 1)
             < lax.broadcasted_iota(jnp.int32, (LANES, LANES), 0)).astype(BF16)
    tile_lo = _dot(below, jnp.broadcast_to(tiles, (LANES, LANES)).astype(BF16))
    base = tile_lo[:, 0:1] * SLOT_TILE

    def assign(b, carry):
        ts, oh1, oh2 = onehots(b)
        pos = base + cum_ref[:, pl.ds(ts, SLOT_BLK)] - 1.0
        s1 = jnp.sum(jnp.where(oh1, pos, 0.0), axis=0, keepdims=True)
        s2 = jnp.sum(jnp.where(oh2, pos, 0.0), axis=0, keepdims=True)
        row = lax.broadcasted_iota(jnp.int32, (8, SLOT_BLK), 0)
        slot_ref[:, pl.ds(ts, SLOT_BLK)] = jnp.where(row == 0, s1, jnp.where(row == 1, s2, 0.0)).astype(jnp.int32)
        return carry

    lax.fori_loop(0, nblk, assign, 0)

    tile_hi = tile_lo + tiles
    tix = lax.broadcasted_iota(jnp.int32, (LANES, LANES), 1).astype(F32)
    is_e = lax.broadcasted_iota(jnp.int32, (LANES, LANES), 0) < N_EXPERTS
    owner = jnp.sum(jnp.where(is_e & (tile_hi <= tix), 1.0, 0.0), axis=0, keepdims=True)
    n_tiles = jnp.max(jnp.where(is_e, tile_hi, 0.0), axis=0, keepdims=True)
    valid = tix[0:1, :] < n_tiles
    last_owner = jnp.max(jnp.where(valid, owner, 0.0), axis=1, keepdims=True)
    owner = jnp.where(valid, owner, last_owner)
    eye = (lax.broadcasted_iota(jnp.int32, (LANES, LANES), 0)
           == lax.broadcasted_iota(jnp.int32, (LANES, LANES), 1))
    to_lanes = lambda colvec: jnp.sum(jnp.where(eye, colvec, 0.0), axis=0, keepdims=True)
    pad_first = to_lanes(base + total)
    pad_end = to_lanes(base + tiles * SLOT_TILE)
    row = lax.broadcasted_iota(jnp.int32, (8, LANES), 0)
    meta = jnp.where(row == 0, owner,
                     jnp.where(row == 1, valid.astype(F32),
                               jnp.where(row == 2, pad_first,
                                         jnp.where(row == 3, pad_end,
                                                   jnp.where(row == 4, n_tiles, 0.0)))))
    meta_ref[...] = meta.astype(jnp.int32)


def _slots(route):
    t = route.shape[0]
    kern = functools.partial(_slots_kernel, n_tok=t)
    return pl.pallas_call(
        kern,
        grid=(1,),
        in_specs=[pl.BlockSpec(route.shape, lambda i: (0, 0))],
        out_specs=[
            pl.BlockSpec((8, t), lambda i: (0, 0)),
            pl.BlockSpec((8, LANES), lambda i: (0, 0)),
        ],
        out_shape=[
            jax.ShapeDtypeStruct((8, t), jnp.int32),
            jax.ShapeDtypeStruct((8, LANES), jnp.int32),
        ],
        scratch_shapes=[pltpu.VMEM((LANES, t), F32)],
        compiler_params=_params("arbitrary"),
        name="slots",
    )(route)


def _row_copy(src_ref, src_row, dst_ref, dst_row, sem):
    return pltpu.make_async_copy(src_ref.at[pl.ds(src_row, 1), :], dst_ref.at[pl.ds(dst_row, 1), :], sem)


def _dispatch_kernel(s1_ref, s2_ref, pad_first_ref, pad_end_ref, n_tiles_ref, h_ref, xs_ref, zero_ref,
                     sem, pad_sem, *, tm):
    i = pl.program_id(0)
    t0 = i * tm
    n_slot_tiles = xs_ref.shape[0] // SLOT_TILE

    def copies(r):
        return (_row_copy(h_ref, r, xs_ref, s1_ref[t0 + r], sem),
                _row_copy(h_ref, r, xs_ref, s2_ref[t0 + r], sem))

    def issue(r, c):
        first, second = copies(r)
        first.start(priority=0)
        second.start(priority=1)
        return c

    lax.fori_loop(0, tm, issue, 0, unroll=8)

    @pl.when(i == pl.num_programs(0) - 1)
    def _():
        zero_ref[...] = jnp.zeros(zero_ref.shape, F32)

        def pad_copy(slot):
            return _row_copy(zero_ref, 0, xs_ref, slot, pad_sem)

        def fill(e, c):
            def one(s, cc):
                pad_copy(s).start()
                return cc

            return lax.fori_loop(pad_first_ref[e], pad_end_ref[e], one, c)

        lax.fori_loop(0, N_EXPERTS, fill, 0)

        def tile_copy(tile):
            return pltpu.make_async_copy(zero_ref, xs_ref.at[pl.ds(tile * SLOT_TILE, SLOT_TILE), :], pad_sem)

        def fill_tile(tile, c):
            tile_copy(tile).start()
            return c

        lax.fori_loop(n_tiles_ref[0], n_slot_tiles, fill_tile, 0)

        def settle(e, c):
            def one(s, cc):
                pad_copy(s).wait()
                return cc

            return lax.fori_loop(pad_first_ref[e], pad_end_ref[e], one, c)

        lax.fori_loop(0, N_EXPERTS, settle, 0)

        def settle_tile(tile, c):
            tile_copy(tile).wait()
            return c

        lax.fori_loop(n_tiles_ref[0], n_slot_tiles, settle_tile, 0)

    def drain(r, c):
        for cp in copies(r):
            cp.wait()
        return c

    lax.fori_loop(0, tm, drain, 0, unroll=8)


def _dispatch(slot1, slot2, pad_first, pad_end, n_tiles, h2, n_slots, tm=256):
    t = h2.shape[0]
    kern = functools.partial(_dispatch_kernel, tm=tm)
    return pl.pallas_call(
        kern,
        grid_spec=pltpu.PrefetchScalarGridSpec(
            num_scalar_prefetch=5,
            grid=(t // tm,),
            in_specs=[pl.BlockSpec((tm, D_MODEL), lambda i, *_: (i, 0))],
            out_specs=pl.BlockSpec(memory_space=pl.ANY),
            scratch_shapes=[
                pltpu.VMEM((SLOT_TILE, D_MODEL), F32),
                pltpu.SemaphoreType.DMA(()),
                pltpu.SemaphoreType.DMA(()),
            ],
        ),
        out_shape=jax.ShapeDtypeStruct((n_slots, D_MODEL), F32),
        compiler_params=_params("arbitrary"),
        name="dispatch",
    )(slot1, slot2, pad_first, pad_end, n_tiles, h2)


def _experts_kernel(own_ref, valid_ref, xs_ref, wg_hbm, wu_hbm, wd_hbm, ys_ref,
                    wgf, wuf, wdf, wgb, wub, wdb, wsem, wslot_ref):
    i = pl.program_id(0)
    nt = pl.num_programs(0)
    at = lambda ref, j: ref[jnp.minimum(j, nt - 1)]

    def weight_copies(e, s):
        return (pltpu.make_async_copy(wg_hbm.at[e], wgf.at[s], wsem.at[s]),
                pltpu.make_async_copy(wu_hbm.at[e], wuf.at[s], wsem.at[s]),
                pltpu.make_async_copy(wd_hbm.at[e], wdf.at[s], wsem.at[s]))

    @pl.when(i == 0)
    def _():
        wslot_ref[0] = 0
        for cp in weight_copies(own_ref[0], 0):
            cp.start()

    @pl.when(valid_ref[i] > 0)
    def _():
        e = own_ref[i]

        @pl.when((i == 0) | (own_ref[jnp.maximum(i - 1, 0)] != e))
        def _():
            s = wslot_ref[0]
            for cp in weight_copies(e, s):
                cp.wait()
            j = lax.while_loop(lambda j: (j < nt) & (at(own_ref, j) == e), lambda j: j + 1, i + 1)

            @pl.when((j < nt) & (at(valid_ref, j) > 0))
            def _():
                for cp in weight_copies(at(own_ref, j), 1 - s):
                    cp.start()

            wgb[...] = wgf[s].astype(BF16)
            wub[...] = wuf[s].astype(BF16)
            wdb[...] = wdf[s].astype(BF16)
            wslot_ref[0] = 1 - s

        x = xs_ref[...].astype(BF16)
        a = _dot(x, wgb[...])
        u = _dot(x, wub[...])
        act = (a * jax.nn.sigmoid(a) * u).astype(BF16)
        ys_ref[...] = _dot(act, wdb[...])

    @pl.when(valid_ref[i] == 0)
    def _():
        ys_ref[...] = jnp.zeros(ys_ref.shape, F32)


def _experts(own, valid, xs, wg, wu, wd):
    n_tiles = xs.shape[0] // SLOT_TILE
    hbm = pl.BlockSpec(memory_space=pl.ANY)
    tile = (SLOT_TILE, D_MODEL)
    return pl.pallas_call(
        _experts_kernel,
        grid_spec=pltpu.PrefetchScalarGridSpec(
            num_scalar_prefetch=2,
            grid=(n_tiles,),
            in_specs=[
                pl.BlockSpec(tile, lambda i, o, v: (jnp.where(v[i] > 0, i, 0), 0)),
                hbm, hbm, hbm,
            ],
            out_specs=pl.BlockSpec(tile, lambda i, o, v: (i, 0)),
            scratch_shapes=[
                pltpu.VMEM((2, D_MODEL, D_EXPERT), F32),
                pltpu.VMEM((2, D_MODEL, D_EXPERT), F32),
                pltpu.VMEM((2, D_EXPERT, D_MODEL), F32),
                pltpu.VMEM((D_MODEL, D_EXPERT), BF16),
                pltpu.VMEM((D_MODEL, D_EXPERT), BF16),
                pltpu.VMEM((D_EXPERT, D_MODEL), BF16),
                pltpu.SemaphoreType.DMA((2,)),
                pltpu.SMEM((1,), jnp.int32),
            ],
        ),
        out_shape=jax.ShapeDtypeStruct(xs.shape, F32),
        compiler_params=_params("arbitrary"),
        name="experts",
    )(own, valid, xs, wg, wu, wd)


def _combine_kernel(s1_ref, s2_ref, x_ref, route_ref, g_ref, ys_ref, y_ref, buf_ref, sem, *, tm):
    i = pl.program_id(0)
    n = pl.num_programs(0)

    def copies(tile, p, r):
        t = tile * tm + r
        return (_row_copy(ys_ref, s1_ref[t], buf_ref.at[p, 0], r, sem.at[p]),
                _row_copy(ys_ref, s2_ref[t], buf_ref.at[p, 1], r, sem.at[p]))

    def gather_start(tile, p):
        def issue(r, c):
            first, second = copies(tile, p, r)
            first.start(priority=0)
            second.start(priority=1)
            return c

        lax.fori_loop(0, tm, issue, 0, unroll=8)

    def gather_wait(tile, p):
        def drain(r, c):
            for cp in copies(tile, p, r):
                cp.wait()
            return c

        lax.fori_loop(0, tm, drain, 0, unroll=8)

    @pl.when(i == 0)
    def _():
        gather_start(0, 0)

    p = i % 2

    @pl.when(i + 1 < n)
    def _():
        gather_start(i + 1, 1 - p)

    gather_wait(i, p)
    y = x_ref[...] + route_ref[:, 2:3] * buf_ref[p, 0] + route_ref[:, 3:4] * buf_ref[p, 1]
    y_ref[...] = _rms(y, g_ref[...])


def _combine(slot1, slot2, x2, route, g, ys, tm=256):
    t = x2.shape[0]
    kern = functools.partial(_combine_kernel, tm=tm)
    return pl.pallas_call(
        kern,
        grid_spec=pltpu.PrefetchScalarGridSpec(
            num_scalar_prefetch=2,
            grid=(t // tm,),
            in_specs=[
                pl.BlockSpec((tm, D_MODEL), lambda i, s1, s2: (i, 0)),
                pl.BlockSpec((tm, LANES), lambda i, s1, s2: (i, 0)),
                pl.BlockSpec((1, D_MODEL), lambda i, s1, s2: (0, 0)),
                pl.BlockSpec(memory_space=pl.ANY),
            ],
            out_specs=pl.BlockSpec((tm, D_MODEL), lambda i, s1, s2: (i, 0)),
            scratch_shapes=[
                pltpu.VMEM((2, 2, tm, D_MODEL), F32),
                pltpu.SemaphoreType.DMA((2,)),
            ],
        ),
        out_shape=jax.ShapeDtypeStruct((t, D_MODEL), F32),
        compiler_params=_params("arbitrary"),
        name="combine",
    )(slot1, slot2, x2, route, g, ys)


def _swap_halves(w):
    half = w.shape[-1] // 2
    return jnp.concatenate([-w[..., half:], w[..., :half]], axis=-1)


def _row(v):
    return v.reshape(1, -1).astype(F32)


def kernel(x, mem, positions, norm_mix, w_in, norm_q_lat, w_q_b, norm_kv_lat, w_kv_b, norm_mla_out, norm_sb_out, w_out, norm_mem_x, norm_mem_src, w_mem_q, w_mem_kv, w_mem_o, norm_ffn, w_group, b_group, w_expert_router, b_expert, w_gate, w_up, w_down, norm_final):
    batch, seq, d = x.shape
    t = batch * seq
    depth = w_in.shape[0]
    assert depth == 1, "single-layer trunk only"
    xt = x.reshape(t, d)
    pos = positions.reshape(t, 1)
    inv_freq = ROPE_THETA ** (-jnp.arange(0, MLA_ROPE, 2, dtype=F32) / MLA_ROPE)
    invf = jnp.concatenate([inv_freq, inv_freq, jnp.zeros((LANES - MLA_ROPE,), F32)]).reshape(1, LANES)
    n_slots = (2 * t // SLOT_TILE + N_EXPERTS) * SLOT_TILE

    for l in range(depth):
        wi_t = w_in[l].T
        lat_w = MLA_Q_RANK + MLA_KV_RANK
        w_kpe_t = wi_t[lat_w:lat_w + MLA_ROPE]
        zpad = jnp.zeros((LANES - MLA_ROPE, d), F32)
        sb0 = lat_w + MLA_ROPE
        half = MLA_ROPE // 2
        w_kpe_swapped_t = jnp.concatenate([-w_kpe_t[half:], w_kpe_t[:half]], axis=0)
        w_lat_t = jnp.concatenate([wi_t[:sb0], zpad, w_kpe_swapped_t, zpad], axis=0).astype(BF16)
        sb_scale = jnp.concatenate([jnp.full((SB_OUT,), SB_DIM ** -0.5 * LOG2E, F32), jnp.ones((2 * SB_OUT,), F32)])
        w_sb_t = (wi_t[sb0:] * sb_scale[:, None]).astype(BF16)
        wq = w_q_b[l].reshape(MLA_Q_RANK, MLA_HEADS, MLA_QK) * (MLA_QK ** -0.5 * LOG2E)
        wq_pe = wq[:, :, MLA_NOPE:]
        zq = jnp.zeros((MLA_Q_RANK, MLA_HEADS, MLA_PAD - MLA_QK), F32)
        wqa = jnp.concatenate([wq, zq], axis=-1).reshape(MLA_Q_RANK, MLA_HEADS * MLA_PAD).astype(BF16)
        wqb = jnp.concatenate([_swap_halves(wq_pe), zq], axis=-1).reshape(MLA_Q_RANK, MLA_HEADS * LANES).astype(BF16)
        wkv = w_kv_b[l].reshape(MLA_KV_RANK, MLA_HEADS, MLA_NOPE + MLA_V)
        wk = wkv[:, :, :MLA_NOPE].reshape(MLA_KV_RANK, MLA_OUT).astype(BF16)
        wvt = wkv[:, :, MLA_NOPE:].reshape(MLA_KV_RANK, MLA_OUT).T.astype(BF16)
        w_router = jnp.concatenate(
            [w_expert_router[l], w_group[l], jnp.zeros((d, LANES - N_EXPERTS - N_GROUPS), F32)], axis=1)
        wr_hi = w_router.astype(BF16)
        wr_lo = (w_router - wr_hi.astype(F32)).astype(BF16)
        b_router = jnp.concatenate(
            [b_expert[l].astype(F32), b_group[l].astype(F32), jnp.zeros((LANES - N_EXPERTS - N_GROUPS,), F32)]
        ).reshape(1, LANES)

        lat, sb_qk, sb_vt = _proj_in(xt, _row(norm_mix[l]), w_lat_t, w_sb_t)
        q, k, vt = _mla_proj(lat, pos, invf, _row(norm_q_lat[l]), _row(norm_kv_lat[l]), wqa, wqb, wk, wvt)
        o_mla = _mla_attn(q, k, vt, batch, seq)
        o_sb = _sb_attn(sb_qk, sb_vt, batch, seq)
        x1 = _out_proj(xt, o_mla, o_sb, _row(norm_mla_out[l]), _row(norm_sb_out[l]), w_out[l].astype(BF16))

        kv = _mem_kv(mem.reshape(batch * MEM_LEN, d), _row(norm_mem_src[l]), w_mem_kv[l].astype(BF16))
        x2, h2, route = _mem_route(x1, _row(norm_mem_x[l]), w_mem_q[l].astype(BF16), kv,
                                   w_mem_o[l].astype(BF16), _row(norm_ffn[l]), wr_hi, wr_lo, b_router, seq)

        slots, meta = _slots(route)
        xs = _dispatch(slots[0], slots[1], meta[2], meta[3], meta[4], h2, n_slots)
        ys = _experts(meta[0], meta[1], xs, w_gate[l], w_up[l], w_down[l])
        xt = _combine(slots[0], slots[1], x2, route, _row(norm_final), ys)
    return xt.reshape(batch, seq, d)
```

```python
import functools

import jax
import jax.numpy as jnp
from jax import lax
from jax.experimental import pallas as pl
from jax.experimental.pallas import tpu as pltpu

F32 = jnp.float32
BF16 = jnp.bfloat16

EPS = 1e-6
ROPE_THETA = 10000.0

D_MODEL = 2048
MEM_LEN = 256
MLA_HEADS = 8
MLA_NOPE = 128
MLA_ROPE = 64
MLA_QK = MLA_NOPE + MLA_ROPE
MLA_V = 128
MLA_Q_RANK = 512
MLA_KV_RANK = 256
MLA_PAD = 256
SB_HEADS = 8
SB_DIM = 128
MLA_OUT = MLA_HEADS * MLA_V
SB_OUT = SB_HEADS * SB_DIM
MEM_HEADS = 4
MEM_DIM = 128
N_GROUPS = 4
EXPERTS_PER_GROUP = 8
N_EXPERTS = N_GROUPS * EXPERTS_PER_GROUP
D_EXPERT = 512

LANES = 128
LAT_COLS = 1024

SLOT_TILE = 256
WEIGHT_SLOTS = 3
VMEM_LIMIT = 56 * 1024 * 1024


def _rms(x, g):
    return x * lax.rsqrt(jnp.mean(x * x, axis=-1, keepdims=True) + EPS) * g


def _dot(a, b):
    return jnp.dot(a, b, preferred_element_type=F32)


def _dot_nt(a, b):
    return lax.dot_general(a, b, (((1,), (1,)), ((), ())), preferred_element_type=F32)


def _split_bf16(x):
    hi = x.astype(BF16)
    lo = (x - hi.astype(F32)).astype(BF16)
    return hi, lo


def _params(*sem):
    return pltpu.CompilerParams(dimension_semantics=sem, vmem_limit_bytes=VMEM_LIMIT)


def _proj_in_kernel(x_ref, g_ref, wl_ref, ws_ref, lat_ref, qk_ref, vt_ref, h_ref):
    j = pl.program_id(1)
    last = pl.num_programs(1) - 1

    @pl.when(j == 0)
    def _():
        h_ref[...] = _rms(x_ref[...], g_ref[...]).astype(BF16)
        lat_ref[...] = _dot_nt(h_ref[...], wl_ref[...])

    @pl.when((j > 0) & (j < last))
    def _():
        qk_ref[...] = _dot_nt(h_ref[...], ws_ref[...]).astype(BF16)

    @pl.when(j == last)
    def _():
        vt_ref[...] = _dot_nt(ws_ref[...], h_ref[...]).astype(BF16)


def _proj_in(x, g, w_lat, w_sb, tm=512):
    t = x.shape[0]
    tn = LAT_COLS
    n_sb = w_sb.shape[0] // tn
    return pl.pallas_call(
        _proj_in_kernel,
        grid=(t // tm, n_sb + 1),
        in_specs=[
            pl.BlockSpec((tm, D_MODEL), lambda i, j: (i, 0)),
            pl.BlockSpec((1, D_MODEL), lambda i, j: (0, 0)),
            pl.BlockSpec((tn, D_MODEL), lambda i, j: (0, 0)),
            pl.BlockSpec((tn, D_MODEL), lambda i, j: (jnp.maximum(j - 1, 0), 0)),
        ],
        out_specs=[
            pl.BlockSpec((tm, tn), lambda i, j: (i, 0)),
            pl.BlockSpec((tm, tn), lambda i, j: (i, jnp.clip(j - 1, 0, n_sb - 2))),
            pl.BlockSpec((SB_OUT, tm), lambda i, j: (0, i)),
        ],
        out_shape=[
            jax.ShapeDtypeStruct((t, LAT_COLS), F32),
            jax.ShapeDtypeStruct((t, 2 * SB_OUT), BF16),
            jax.ShapeDtypeStruct((SB_OUT, t), BF16),
        ],
        scratch_shapes=[pltpu.VMEM((tm, D_MODEL), BF16)],
        compiler_params=_params("parallel", "arbitrary"),
        name="proj_in",
    )(x, g, w_lat, w_sb)


def _mla_proj_kernel(lat_ref, pos_ref, invf_ref, gq_ref, gkv_ref, wqa_ref, wqb_ref, wk_ref, wvt_ref,
                     q_ref, k_ref, vt_ref):
    cq = _rms(lat_ref[:, :MLA_Q_RANK], gq_ref[...]).astype(BF16)
    ckv = _rms(lat_ref[:, MLA_Q_RANK:MLA_Q_RANK + MLA_KV_RANK], gkv_ref[...]).astype(BF16)
    ang = pos_ref[...].astype(F32) * invf_ref[...]
    live = lax.broadcasted_iota(jnp.int32, ang.shape, 1) < MLA_ROPE
    cos2 = jnp.where(live, jnp.cos(ang), 0.0)
    sin2 = jnp.where(live, jnp.sin(ang), 0.0)

    qa = _dot(cq, wqa_ref[...])
    qb = _dot(cq, wqb_ref[...])
    kn = _dot(ckv, wk_ref[...])
    vt_ref[...] = _dot_nt(wvt_ref[...], ckv).astype(BF16)
    k_pe = (lat_ref[:, 768:896] * cos2 + lat_ref[:, 896:1024] * sin2).astype(BF16)
    for h in range(MLA_HEADS):
        lo = h * MLA_PAD
        mid = lo + LANES
        q_ref[:, lo:mid] = qa[:, lo:mid].astype(BF16)
        q_ref[:, mid:mid + LANES] = (qa[:, mid:mid + LANES] * cos2
                                     + qb[:, h * LANES:(h + 1) * LANES] * sin2).astype(BF16)
        k_ref[:, lo:mid] = kn[:, h * LANES:(h + 1) * LANES].astype(BF16)
        k_ref[:, mid:mid + LANES] = k_pe


def _mla_proj(lat, pos, invf, gq, gkv, wqa, wqb, wk, wvt, tm=512):
    t = lat.shape[0]
    full = lambda a: pl.BlockSpec(a.shape, lambda i: (0, 0))
    return pl.pallas_call(
        _mla_proj_kernel,
        grid=(t // tm,),
        in_specs=[
            pl.BlockSpec((tm, LAT_COLS), lambda i: (i, 0)),
            pl.BlockSpec((tm, 1), lambda i: (i, 0)),
            full(invf), full(gq), full(gkv), full(wqa), full(wqb), full(wk), full(wvt),
        ],
        out_specs=[
            pl.BlockSpec((tm, MLA_HEADS * MLA_PAD), lambda i: (i, 0)),
            pl.BlockSpec((tm, MLA_HEADS * MLA_PAD), lambda i: (i, 0)),
            pl.BlockSpec((MLA_OUT, tm), lambda i: (0, i)),
        ],
        out_shape=[
            jax.ShapeDtypeStruct((t, MLA_HEADS * MLA_PAD), BF16),
            jax.ShapeDtypeStruct((t, MLA_HEADS * MLA_PAD), BF16),
            jax.ShapeDtypeStruct((MLA_OUT, t), BF16),
        ],
        compiler_params=_params("parallel"),
        name="mla_proj",
    )(lat, pos, invf, gq, gkv, wqa, wqb, wk, wvt)


LOG2E = 1.4426950408889634


def _three_stage(n_pairs, stage_a, stage_b, stage_c):
    a_issue, a_finish = stage_a
    b_issue, b_finish = stage_b
    c_issue, c_finish = stage_c

    def run_a(n, slot, first):
        a_finish(n, slot, a_issue(n, slot, first), first)

    run_a(0, 0, True)
    run_a(1, 1, True)
    b_finish(0, 0, b_issue(0, 0))

    def half(na, sa, nb, sb, nc, sc):
        ra = a_issue(na, sa, False)
        rb = b_issue(nb, sb)
        rc = c_issue(nc, sc)
        a_finish(na, sa, ra, False)
        b_finish(nb, sb, rb)
        c_finish(nc, sc, rc)

    def body(p, carry):
        half(2 * p, 0, 2 * p - 1, 1, 2 * p - 2, 0)
        half(2 * p + 1, 1, 2 * p, 0, 2 * p - 1, 1)
        return carry

    lax.fori_loop(1, n_pairs + 1, body, 0)
    last = 2 * n_pairs + 1
    rb = b_issue(last, 1)
    rc = c_issue(last - 1, 0)
    b_finish(last, 1, rb)
    c_finish(last - 1, 0, rc)
    c_finish(last, 1, c_issue(last, 1))


def _two_stage(n_pairs, produce, consume):
    produce(0, 0, True)
    produce(1, 1, True)
    consume(0, 0)

    def body(p, carry):
        produce(2 * p, 0, False)
        consume(2 * p - 1, 1)
        produce(2 * p + 1, 1, False)
        consume(2 * p, 0)
        return carry

    lax.fori_loop(1, n_pairs + 1, body, 0)
    consume(2 * n_pairs + 1, 1)


MLA_SUM_ROWS = 16
MLA_GROUP = 8


def _mla_attn_kernel(q_ref, k_ref, vt_ref, o_ref, s_ref, m_ref, acc_ref, *, tq, tk):
    i = pl.program_id(2)
    heads = range(MLA_GROUP)
    dq, dv = MLA_PAD, MLA_V
    m_ref[...] = jnp.full(m_ref.shape, -jnp.inf, F32)
    acc_ref[...] = jnp.zeros(acc_ref.shape, F32)
    ones = jnp.ones((MLA_SUM_ROWS, tk), BF16)

    def key_start(n):
        tile = jnp.where(n < 2, 2 * i + n, 2 * i + 1 - n)
        return pl.multiple_of(tile * tk, tk)

    def produce(n, slot, diagonal):
        ks = key_start(n)
        for g in heads:
            st = _dot_nt(k_ref[pl.ds(ks, tk), g * dq:(g + 1) * dq], q_ref[:, g * dq:(g + 1) * dq])
            if diagonal:
                kpos = ks + lax.broadcasted_iota(jnp.int32, st.shape, 0)
                qpos = i * tq + lax.broadcasted_iota(jnp.int32, st.shape, 1)
                st = jnp.where(kpos <= qpos, st, -jnp.inf)
            s_ref[g, slot] = st

    def consume(n, slot):
        ks = key_start(n)
        for g in heads:
            st = s_ref[g, slot]
            m_old = m_ref[g]
            m_new = jnp.maximum(m_old, jnp.max(st, axis=0, keepdims=True))
            alpha = jnp.exp2(m_old - m_new)
            p = jnp.exp2(st - m_new).astype(BF16)
            v_ones = jnp.concatenate([vt_ref[g * dv:(g + 1) * dv, pl.ds(ks, tk)], ones], axis=0)
            acc_ref[g] = alpha * acc_ref[g] + _dot(v_ones, p)
            m_ref[g] = m_new

    _two_stage(i, produce, consume)
    for g in heads:
        o_ref[:, g * dv:(g + 1) * dv] = (acc_ref[g, :dv, :] / acc_ref[g, dv:dv + 1, :]).T


def _mla_attn(q, k, vt, batch, seq, tq=512):
    tk = tq // 2
    nq = seq // tq
    groups = MLA_HEADS // MLA_GROUP
    kern = functools.partial(_mla_attn_kernel, tq=tq, tk=tk)
    return pl.pallas_call(
        kern,
        grid=(batch, groups, nq),
        in_specs=[
            pl.BlockSpec((tq, MLA_GROUP * MLA_PAD), lambda b, h, i: (b * nq + i, h)),
            pl.BlockSpec((seq, MLA_GROUP * MLA_PAD), lambda b, h, i: (b, h), pipeline_mode=pl.Buffered(1)),
            pl.BlockSpec((MLA_GROUP * MLA_V, seq), lambda b, h, i: (h, b), pipeline_mode=pl.Buffered(1)),
        ],
        out_specs=pl.BlockSpec((tq, MLA_GROUP * MLA_V), lambda b, h, i: (b * nq + i, h)),
        out_shape=jax.ShapeDtypeStruct((batch * seq, MLA_OUT), F32),
        scratch_shapes=[
            pltpu.VMEM((MLA_GROUP, 2, tk, tq), F32),
            pltpu.VMEM((MLA_GROUP, 1, tq), F32),
            pltpu.VMEM((MLA_GROUP, MLA_V + MLA_SUM_ROWS, tq), F32),
        ],
        compiler_params=_params("parallel", "parallel", "arbitrary"),
        name="mla_attn",
    )(q, k, vt)


SB_EXP_CLAMP = 126.0
SB_GROUP = 4


def _sb_attn_kernel(q_ref, k_ref, vt_ref, o_ref, z_ref, hl_ref, arg_ref, acc_ref, c_ref, *, tq, tk):
    i = pl.program_id(2)
    heads = range(SB_GROUP)
    d = SB_DIM
    acc_ref[...] = jnp.zeros(acc_ref.shape, F32)
    c_ref[...] = jnp.zeros(c_ref.shape, F32)
    col = lax.broadcasted_iota(jnp.int32, (tk, tk), 1)
    row = lax.broadcasted_iota(jnp.int32, (tk, tk), 0)
    neg_tri = jnp.where(col >= row, -1.0, 0.0).astype(BF16)

    def key_start(n):
        return pl.multiple_of((2 * i + 1 - n) * tk, tk)

    def a_issue(n, slot, masked):
        ks = key_start(n)
        return [_dot_nt(k_ref[pl.ds(ks, tk), g * d:(g + 1) * d], q_ref[:, g * d:(g + 1) * d])
                for g in heads]

    def a_finish(n, slot, z2s, masked):
        for g in heads:
            z2 = z2s[g]
            sp = jnp.maximum(z2, jnp.log2(1.0 + jnp.exp2(jnp.minimum(z2, SB_EXP_CLAMP))))
            if masked:
                kpos = key_start(n) + lax.broadcasted_iota(jnp.int32, z2.shape, 0)
                qpos = i * tq + lax.broadcasted_iota(jnp.int32, z2.shape, 1)
                mask = kpos < qpos
                sp = jnp.where(mask, sp, 0.0)
                z2 = jnp.where(mask, z2, -jnp.inf)
            hl_ref[g, slot] = sp.astype(BF16)
            z_ref[g, slot] = z2

    def b_issue(n, slot):
        return [_dot(neg_tri, hl_ref[g, slot]) for g in heads]

    def b_finish(n, slot, laters):
        for g in heads:
            arg_ref[g, slot] = z_ref[g, slot] + laters[g] + c_ref[g]
            c_ref[g] += laters[g][0:1, :]

    def c_issue(n, slot):
        ks = key_start(n)
        return [_dot(vt_ref[g * d:(g + 1) * d, pl.ds(ks, tk)], jnp.exp2(arg_ref[g, slot]).astype(BF16))
                for g in heads]

    def c_finish(n, slot, pvs):
        for g in heads:
            acc_ref[g] += pvs[g]

    _three_stage(i, (a_issue, a_finish), (b_issue, b_finish), (c_issue, c_finish))
    for g in heads:
        o_ref[:, g * d:(g + 1) * d] = acc_ref[g].T


def _sb_attn(qk, vt, batch, seq, tq=512):
    tk = tq // 2
    nq = seq // tq
    groups = SB_HEADS // SB_GROUP
    gd = SB_GROUP * SB_DIM
    kern = functools.partial(_sb_attn_kernel, tq=tq, tk=tk)
    return pl.pallas_call(
        kern,
        grid=(batch, groups, nq),
        in_specs=[
            pl.BlockSpec((tq, gd), lambda b, h, i: (b * nq + i, h)),
            pl.BlockSpec((seq, gd), lambda b, h, i: (b, groups + h)),
            pl.BlockSpec((gd, seq), lambda b, h, i: (h, b)),
        ],
        out_specs=pl.BlockSpec((tq, gd), lambda b, h, i: (b * nq + i, h)),
        out_shape=jax.ShapeDtypeStruct((batch * seq, SB_OUT), F32),
        scratch_shapes=[
            pltpu.VMEM((SB_GROUP, 2, tk, tq), F32),
            pltpu.VMEM((SB_GROUP, 2, tk, tq), BF16),
            pltpu.VMEM((SB_GROUP, 2, tk, tq), F32),
            pltpu.VMEM((SB_GROUP, SB_DIM, tq), F32),
            pltpu.VMEM((SB_GROUP, 1, tq), F32),
        ],
        compiler_params=_params("parallel", "parallel", "arbitrary"),
        name="sb_attn",
    )(qk, qk, vt)


def _out_proj_kernel(x_ref, oa_ref, ob_ref, ga_ref, gb_ref, w_ref, y_ref):
    na = _rms(oa_ref[...], ga_ref[...]).astype(BF16)
    nb = _rms(ob_ref[...], gb_ref[...]).astype(BF16)
    y_ref[...] = x_ref[...] + _dot(na, w_ref[:MLA_OUT, :]) + _dot(nb, w_ref[MLA_OUT:, :])


def _out_proj(x, oa, ob, ga, gb, w, tm=512):
    t = x.shape[0]
    full = lambda a: pl.BlockSpec(a.shape, lambda i: (0, 0))
    return pl.pallas_call(
        _out_proj_kernel,
        grid=(t // tm,),
        in_specs=[
            pl.BlockSpec((tm, D_MODEL), lambda i: (i, 0)),
            pl.BlockSpec((tm, MLA_OUT), lambda i: (i, 0)),
            pl.BlockSpec((tm, SB_OUT), lambda i: (i, 0)),
            full(ga), full(gb), full(w),
        ],
        out_specs=pl.BlockSpec((tm, D_MODEL), lambda i: (i, 0)),
        out_shape=jax.ShapeDtypeStruct((t, D_MODEL), F32),
        compiler_params=_params("parallel"),
        name="out_proj",
    )(x, oa, ob, ga, gb, w)


def _mem_kv_kernel(mem_ref, g_ref, w_ref, kv_ref):
    kv_ref[...] = _dot(_rms(mem_ref[...], g_ref[...]).astype(BF16), w_ref[...]).astype(BF16)


def _mem_kv(mem, g, w):
    m = mem.shape[0]
    n = w.shape[1]
    full = lambda a: pl.BlockSpec(a.shape, lambda i: (0, 0))
    return pl.pallas_call(
        _mem_kv_kernel,
        grid=(1,),
        in_specs=[full(mem), full(g), full(w)],
        out_specs=pl.BlockSpec((m, n), lambda i: (0, 0)),
        out_shape=jax.ShapeDtypeStruct((m, n), BF16),
        compiler_params=_params("arbitrary"),
        name="mem_kv",
    )(mem, g, w)


GROUP_LANE0 = N_EXPERTS


def _mem_route_kernel(x_ref, gx_ref, wq_ref, kv_ref, wo_ref, gf_ref, wrh_ref, wrl_ref, br_ref,
                      x2_ref, h2_ref, route_ref):
    x1 = x_ref[...]
    q = _dot(_rms(x1, gx_ref[...]).astype(BF16), wq_ref[...]).astype(BF16)
    scale = MEM_DIM ** -0.5
    kw = MEM_HEADS * MEM_DIM
    heads = []
    for h in range(MEM_HEADS):
        lo = h * MEM_DIM
        s = _dot_nt(q[:, lo:lo + MEM_DIM], kv_ref[:, lo:lo + MEM_DIM]) * scale
        e = jnp.exp(s - jnp.max(s, axis=-1, keepdims=True))
        p = (e / jnp.sum(e, axis=-1, keepdims=True)).astype(BF16)
        heads.append(_dot(p, kv_ref[:, kw + lo:kw + lo + MEM_DIM]).astype(BF16))
    o = jnp.concatenate(heads, axis=-1)
    x2 = x1 + _dot(o, wo_ref[...])
    x2_ref[...] = x2
    h2 = _rms(x2, gf_ref[...])
    h2_ref[...] = h2

    hh, hl = _split_bf16(h2)
    lg = _dot(hh, wrh_ref[...]) + _dot(hl, wrh_ref[...]) + _dot(hh, wrl_ref[...]) + br_ref[...]
    lane = lax.broadcasted_iota(jnp.int32, lg.shape, 1)
    big = jnp.int32(1 << 20)
    ninf = -jnp.inf

    def lane_max(v):
        return jnp.max(v, axis=-1, keepdims=True)

    def first_lane(cond):
        return jnp.min(jnp.where(cond, lane, big), axis=-1, keepdims=True)

    is_g = (lane >= GROUP_LANE0) & (lane < GROUP_LANE0 + N_GROUPS)
    g_max = lane_max(jnp.where(is_g, lg, ninf))
    g_sum = jnp.sum(jnp.where(is_g, jnp.exp(lg - g_max), 0.0), axis=-1, keepdims=True)
    p_g = 1.0 / g_sum
    g_idx = first_lane(is_g & (lg == g_max)) - GROUP_LANE0
    in_grp = (lane < N_EXPERTS) & ((lane // EXPERTS_PER_GROUP) == g_idx)
    e_max = lane_max(jnp.where(in_grp, lg, ninf))
    e_sum = jnp.sum(jnp.where(in_grp, jnp.exp(lg - e_max), 0.0), axis=-1, keepdims=True)
    i1 = first_lane(in_grp & (lg == e_max))
    rest = in_grp & (lane != i1)
    e_max2 = lane_max(jnp.where(rest, lg, ninf))
    i2 = first_lane(rest & (lg == e_max2))
    p1 = 1.0 / e_sum
    p2 = jnp.exp(e_max2 - e_max) / e_sum
    den = p1 + p2
    gate1 = p_g * (p1 / den)
    gate2 = p_g * (p2 / den)
    route = jnp.where(lane == 0, i1.astype(F32),
                      jnp.where(lane == 1, i2.astype(F32),
                                jnp.where(lane == 2, gate1,
                                          jnp.where(lane == 3, gate2, 0.0))))
    route_ref[...] = route


def _mem_route(x1, gx, wq, kv, wo, gf, wrh, wrl, br, seq, tm=512):
    t = x1.shape[0]
    per_batch = seq // tm
    full = lambda a: pl.BlockSpec(a.shape, lambda i: (0, 0))
    return pl.pallas_call(
        _mem_route_kernel,
        grid=(t // tm,),
        in_specs=[
            pl.BlockSpec((tm, D_MODEL), lambda i: (i, 0)),
            full(gx), full(wq),
            pl.BlockSpec((MEM_LEN, kv.shape[1]), lambda i: (i // per_batch, 0)),
            full(wo), full(gf), full(wrh), full(wrl), full(br),
        ],
        out_specs=[
            pl.BlockSpec((tm, D_MODEL), lambda i: (i, 0)),
            pl.BlockSpec((tm, D_MODEL), lambda i: (i, 0)),
            pl.BlockSpec((tm, LANES), lambda i: (i, 0)),
        ],
        out_shape=[
            jax.ShapeDtypeStruct((t, D_MODEL), F32),
            jax.ShapeDtypeStruct((t, D_MODEL), F32),
            jax.ShapeDtypeStruct((t, LANES), F32),
        ],
        compiler_params=_params("parallel"),
        name="mem_route",
    )(x1, gx, wq, kv, wo, gf, wrh, wrl, br)


SLOT_BLK = 256


def _slots_kernel(route_ref, slot_ref, meta_ref, cum_ref, *, n_tok):
    nblk = n_tok // SLOT_BLK
    e_iota = lax.broadcasted_iota(jnp.int32, (LANES, SLOT_BLK), 0)
    incl = (lax.broadcasted_iota(jnp.int32, (SLOT_BLK, SLOT_BLK), 0)
            <= lax.broadcasted_iota(jnp.int32, (SLOT_BLK, SLOT_BLK), 1)).astype(BF16)

    def onehots(b):
        ts = pl.multiple_of(b * SLOT_BLK, SLOT_BLK)
        ids = route_ref[pl.ds(ts, SLOT_BLK), :].T
        oh1 = e_iota == ids[0:1, :].astype(jnp.int32)
        oh2 = e_iota == ids[1:2, :].astype(jnp.int32)
        return ts, oh1, oh2

    def count(b, carry):
        ts, oh1, oh2 = onehots(b)
        cnt = (oh1 | oh2).astype(F32).astype(BF16)
        c = _dot(cnt, incl) + carry
        cum_ref[:, pl.ds(ts, SLOT_BLK)] = c
        return c[:, SLOT_BLK - 1:SLOT_BLK]

    total = lax.fori_loop(0, nblk, count, jnp.zeros((LANES, 1), F32))
    tiles = jnp.floor((total + (SLOT_TILE - 1)) * (1.0 / SLOT_TILE))
    below = (lax.broadcasted_iota(jnp.int32, (LANES, LANES), 1)
             < lax.broadcasted_iota(jnp.int32, (LANES, LANES), 0)).astype(BF16)
    tile_lo = _dot(below, jnp.broadcast_to(tiles, (LANES, LANES)).astype(BF16))
    base = tile_lo[:, 0:1] * SLOT_TILE

    def assign(b, carry):
        ts, oh1, oh2 = onehots(b)
        pos = base + cum_ref[:, pl.ds(ts, SLOT_BLK)] - 1.0
        s1 = jnp.sum(jnp.where(oh1, pos, 0.0), axis=0, keepdims=True)
        s2 = jnp.sum(jnp.where(oh2, pos, 0.0), axis=0, keepdims=True)
        row = lax.broadcasted_iota(jnp.int32, (8, SLOT_BLK), 0)
        slot_ref[:, pl.ds(ts, SLOT_BLK)] = jnp.where(row == 0, s1, jnp.where(row == 1, s2, 0.0)).astype(jnp.int32)
        return carry

    lax.fori_loop(0, nblk, assign, 0)

    tile_hi = tile_lo + tiles
    tix = lax.broadcasted_iota(jnp.int32, (LANES, LANES), 1).astype(F32)
    is_e = lax.broadcasted_iota(jnp.int32, (LANES, LANES), 0) < N_EXPERTS
    owner = jnp.sum(jnp.where(is_e & (tile_hi <= tix), 1.0, 0.0), axis=0, keepdims=True)
    n_tiles = jnp.max(jnp.where(is_e, tile_hi, 0.0), axis=0, keepdims=True)
    valid = tix[0:1, :] < n_tiles
    last_owner = jnp.max(jnp.where(valid, owner, 0.0), axis=1, keepdims=True)
    owner = jnp.where(valid, owner, last_owner)
    eye = (lax.broadcasted_iota(jnp.int32, (LANES, LANES), 0)
           == lax.broadcasted_iota(jnp.int32, (LANES, LANES), 1))
    to_lanes = lambda colvec: jnp.sum(jnp.where(eye, colvec, 0.0), axis=0, keepdims=True)
    pad_first = to_lanes(base + total)
    pad_end = to_lanes(base + tiles * SLOT_TILE)
    row = lax.broadcasted_iota(jnp.int32, (8, LANES), 0)
    meta = jnp.where(row == 0, owner,
                     jnp.where(row == 1, valid.astype(F32),
                               jnp.where(row == 2, pad_first,
                                         jnp.where(row == 3, pad_end,
                                                   jnp.where(row == 4, n_tiles, 0.0)))))
    meta_ref[...] = meta.astype(jnp.int32)


def _slots(route):
    t = route.shape[0]
    kern = functools.partial(_slots_kernel, n_tok=t)
    return pl.pallas_call(
        kern,
        grid=(1,),
        in_specs=[pl.BlockSpec(route.shape, lambda i: (0, 0))],
        out_specs=[
            pl.BlockSpec((8, t), lambda i: (0, 0)),
            pl.BlockSpec((8, LANES), lambda i: (0, 0)),
        ],
        out_shape=[
            jax.ShapeDtypeStruct((8, t), jnp.int32),
            jax.ShapeDtypeStruct((8, LANES), jnp.int32),
        ],
        scratch_shapes=[pltpu.VMEM((LANES, t), F32)],
        compiler_params=_params("arbitrary"),
        name="slots",
    )(route)


def _row_copy(src_ref, src_row, dst_ref, dst_row, sem):
    return pltpu.make_async_copy(src_ref.at[pl.ds(src_row, 1), :], dst_ref.at[pl.ds(dst_row, 1), :], sem)


def _dispatch_kernel(s1_ref, s2_ref, pad_first_ref, pad_end_ref, n_tiles_ref, h_ref, xs_ref, zero_ref,
                     sem, pad_sem, *, tm):
    i = pl.program_id(0)
    t0 = i * tm
    n_slot_tiles = xs_ref.shape[0] // SLOT_TILE

    def copies(r):
        return (_row_copy(h_ref, r, xs_ref, s1_ref[t0 + r], sem),
                _row_copy(h_ref, r, xs_ref, s2_ref[t0 + r], sem))

    def issue(r, c):
        first, second = copies(r)
        first.start(priority=0)
        second.start(priority=1)
        return c

    lax.fori_loop(0, tm, issue, 0, unroll=8)

    @pl.when(i == pl.num_programs(0) - 1)
    def _():
        zero_ref[...] = jnp.zeros(zero_ref.shape, F32)

        def pad_copy(slot):
            return _row_copy(zero_ref, 0, xs_ref, slot, pad_sem)

        def fill(e, c):
            def one(s, cc):
                pad_copy(s).start()
                return cc

            return lax.fori_loop(pad_first_ref[e], pad_end_ref[e], one, c)

        lax.fori_loop(0, N_EXPERTS, fill, 0)

        def tile_copy(tile):
            return pltpu.make_async_copy(zero_ref, xs_ref.at[pl.ds(tile * SLOT_TILE, SLOT_TILE), :], pad_sem)

        def fill_tile(tile, c):
            tile_copy(tile).start()
            return c

        lax.fori_loop(n_tiles_ref[0], n_slot_tiles, fill_tile, 0)

        def settle(e, c):
            def one(s, cc):
                pad_copy(s).wait()
                return cc

            return lax.fori_loop(pad_first_ref[e], pad_end_ref[e], one, c)

        lax.fori_loop(0, N_EXPERTS, settle, 0)

        def settle_tile(tile, c):
            tile_copy(tile).wait()
            return c

        lax.fori_loop(n_tiles_ref[0], n_slot_tiles, settle_tile, 0)

    def drain(r, c):
        for cp in copies(r):
            cp.wait()
        return c

    lax.fori_loop(0, tm, drain, 0, unroll=8)


def _dispatch(slot1, slot2, pad_first, pad_end, n_tiles, h2, n_slots, tm=256):
    t = h2.shape[0]
    kern = functools.partial(_dispatch_kernel, tm=tm)
    return pl.pallas_call(
        kern,
        grid_spec=pltpu.PrefetchScalarGridSpec(
            num_scalar_prefetch=5,
            grid=(t // tm,),
            in_specs=[pl.BlockSpec((tm, D_MODEL), lambda i, *_: (i, 0))],
            out_specs=pl.BlockSpec(memory_space=pl.ANY),
            scratch_shapes=[
                pltpu.VMEM((SLOT_TILE, D_MODEL), F32),
                pltpu.SemaphoreType.DMA(()),
                pltpu.SemaphoreType.DMA(()),
            ],
        ),
        out_shape=jax.ShapeDtypeStruct((n_slots, D_MODEL), F32),
        compiler_params=_params("arbitrary"),
        name="dispatch",
    )(slot1, slot2, pad_first, pad_end, n_tiles, h2)


def _experts_kernel(own_ref, valid_ref, xs_ref, wg_hbm, wu_hbm, wd_hbm, ys_ref,
                    wgf, wuf, wdf, wgb, wub, wdb, wsem, wslot_ref):
    i = pl.program_id(0)
    nt = pl.num_programs(0)
    at = lambda ref, j: ref[jnp.minimum(j, nt - 1)]

    def weight_copies(e, s):
        return (pltpu.make_async_copy(wg_hbm.at[e], wgf.at[s], wsem.at[s]),
                pltpu.make_async_copy(wu_hbm.at[e], wuf.at[s], wsem.at[s]),
                pltpu.make_async_copy(wd_hbm.at[e], wdf.at[s], wsem.at[s]))

    def next_expert_tile(j0):
        e0 = at(own_ref, j0)
        return lax.while_loop(lambda j: (j < nt) & (at(own_ref, j) == e0), lambda j: j + 1, j0 + 1)

    def start_weights(j, s):
        @pl.when((j < nt) & (at(valid_ref, j) > 0))
        def _():
            for cp in weight_copies(at(own_ref, j), s):
                cp.start()

    @pl.when(i == 0)
    def _():
        wslot_ref[0] = 0
        j = 0
        for s in range(WEIGHT_SLOTS - 1):
            start_weights(j, s)
            j = next_expert_tile(j)

    @pl.when(valid_ref[i] > 0)
    def _():
        e = own_ref[i]

        @pl.when((i == 0) | (own_ref[jnp.maximum(i - 1, 0)] != e))
        def _():
            s = wslot_ref[0]
            for cp in weight_copies(e, s):
                cp.wait()
            j = i
            for _ in range(WEIGHT_SLOTS - 1):
                j = next_expert_tile(j)
            start_weights(j, (s + WEIGHT_SLOTS - 1) % WEIGHT_SLOTS)

            wgb[...] = wgf[s].astype(BF16)
            wub[...] = wuf[s].astype(BF16)
            wdb[...] = wdf[s].astype(BF16)
            wslot_ref[0] = (s + 1) % WEIGHT_SLOTS

        x = xs_ref[...].astype(BF16)
        a = _dot(x, wgb[...])
        u = _dot(x, wub[...])
        act = (a * jax.nn.sigmoid(a) * u).astype(BF16)
        ys_ref[...] = _dot(act, wdb[...])

    @pl.when(valid_ref[i] == 0)
    def _():
        ys_ref[...] = jnp.zeros(ys_ref.shape, F32)


def _experts(own, valid, xs, wg, wu, wd):
    n_tiles = xs.shape[0] // SLOT_TILE
    hbm = pl.BlockSpec(memory_space=pl.ANY)
    tile = (SLOT_TILE, D_MODEL)
    return pl.pallas_call(
        _experts_kernel,
        grid_spec=pltpu.PrefetchScalarGridSpec(
            num_scalar_prefetch=2,
            grid=(n_tiles,),
            in_specs=[
                pl.BlockSpec(tile, lambda i, o, v: (jnp.where(v[i] > 0, i, 0), 0)),
                hbm, hbm, hbm,
            ],
            out_specs=pl.BlockSpec(tile, lambda i, o, v: (i, 0)),
            scratch_shapes=[
                pltpu.VMEM((WEIGHT_SLOTS, D_MODEL, D_EXPERT), F32),
                pltpu.VMEM((WEIGHT_SLOTS, D_MODEL, D_EXPERT), F32),
                pltpu.VMEM((WEIGHT_SLOTS, D_EXPERT, D_MODEL), F32),
                pltpu.VMEM((D_MODEL, D_EXPERT), BF16),
                pltpu.VMEM((D_MODEL, D_EXPERT), BF16),
                pltpu.VMEM((D_EXPERT, D_MODEL), BF16),
                pltpu.SemaphoreType.DMA((WEIGHT_SLOTS,)),
                pltpu.SMEM((1,), jnp.int32),
            ],
        ),
        out_shape=jax.ShapeDtypeStruct(xs.shape, F32),
        compiler_params=_params("arbitrary"),
        name="experts",
    )(own, valid, xs, wg, wu, wd)


def _combine_kernel(s1_ref, s2_ref, x_ref, route_ref, g_ref, ys_ref, y_ref, buf_ref, sem, *, tm):
    i = pl.program_id(0)
    n = pl.num_programs(0)

    def copies(tile, p, r):
        t = tile * tm + r
        return (_row_copy(ys_ref, s1_ref[t], buf_ref.at[p, 0], r, sem.at[p]),
                _row_copy(ys_ref, s2_ref[t], buf_ref.at[p, 1], r, sem.at[p]))

    def gather_start(tile, p):
        def issue(r, c):
            first, second = copies(tile, p, r)
            first.start(priority=0)
            second.start(priority=1)
            return c

        lax.fori_loop(0, tm, issue, 0, unroll=8)

    def gather_wait(tile, p):
        def drain(r, c):
            for cp in copies(tile, p, r):
                cp.wait()
            return c

        lax.fori_loop(0, tm, drain, 0, unroll=8)

    @pl.when(i == 0)
    def _():
        gather_start(0, 0)

    p = i % 2

    @pl.when(i + 1 < n)
    def _():
        gather_start(i + 1, 1 - p)

    gather_wait(i, p)
    y = x_ref[...] + route_ref[:, 2:3] * buf_ref[p, 0] + route_ref[:, 3:4] * buf_ref[p, 1]
    y_ref[...] = _rms(y, g_ref[...])


def _combine(slot1, slot2, x2, route, g, ys, tm=256):
    t = x2.shape[0]
    kern = functools.partial(_combine_kernel, tm=tm)
    return pl.pallas_call(
        kern,
        grid_spec=pltpu.PrefetchScalarGridSpec(
            num_scalar_prefetch=2,
            grid=(t // tm,),
            in_specs=[
                pl.BlockSpec((tm, D_MODEL), lambda i, s1, s2: (i, 0)),
                pl.BlockSpec((tm, LANES), lambda i, s1, s2: (i, 0)),
                pl.BlockSpec((1, D_MODEL), lambda i, s1, s2: (0, 0)),
                pl.BlockSpec(memory_space=pl.ANY),
            ],
            out_specs=pl.BlockSpec((tm, D_MODEL), lambda i, s1, s2: (i, 0)),
            scratch_shapes=[
                pltpu.VMEM((2, 2, tm, D_MODEL), F32),
                pltpu.SemaphoreType.DMA((2,)),
            ],
        ),
        out_shape=jax.ShapeDtypeStruct((t, D_MODEL), F32),
        compiler_params=_params("arbitrary"),
        name="combine",
    )(slot1, slot2, x2, route, g, ys)


def _swap_halves(w):
    half = w.shape[-1] // 2
    return jnp.concatenate([-w[..., half:], w[..., :half]], axis=-1)


def _row(v):
    return v.reshape(1, -1).astype(F32)


def kernel(x, mem, positions, norm_mix, w_in, norm_q_lat, w_q_b, norm_kv_lat, w_kv_b, norm_mla_out, norm_sb_out, w_out, norm_mem_x, norm_mem_src, w_mem_q, w_mem_kv, w_mem_o, norm_ffn, w_group, b_group, w_expert_router, b_expert, w_gate, w_up, w_down, norm_final):
    batch, seq, d = x.shape
    t = batch * seq
    depth = w_in.shape[0]
    assert depth == 1, "single-layer trunk only"
    xt = x.reshape(t, d)
    pos = positions.reshape(t, 1)
    inv_freq = ROPE_THETA ** (-jnp.arange(0, MLA_ROPE, 2, dtype=F32) / MLA_ROPE)
    invf = jnp.concatenate([inv_freq, inv_freq, jnp.zeros((LANES - MLA_ROPE,), F32)]).reshape(1, LANES)
    n_slots = (2 * t // SLOT_TILE + N_EXPERTS) * SLOT_TILE

    for l in range(depth):
        wi_t = w_in[l].T
        lat_w = MLA_Q_RANK + MLA_KV_RANK
        w_kpe_t = wi_t[lat_w:lat_w + MLA_ROPE]
        zpad = jnp.zeros((LANES - MLA_ROPE, d), F32)
        sb0 = lat_w + MLA_ROPE
        half = MLA_ROPE // 2
        w_kpe_swapped_t = jnp.concatenate([-w_kpe_t[half:], w_kpe_t[:half]], axis=0)
        w_lat_t = jnp.concatenate([wi_t[:sb0], zpad, w_kpe_swapped_t, zpad], axis=0).astype(BF16)
        sb_scale = jnp.concatenate([jnp.full((SB_OUT,), SB_DIM ** -0.5 * LOG2E, F32), jnp.ones((2 * SB_OUT,), F32)])
        w_sb_t = (wi_t[sb0:] * sb_scale[:, None]).astype(BF16)
        wq = w_q_b[l].reshape(MLA_Q_RANK, MLA_HEADS, MLA_QK) * (MLA_QK ** -0.5 * LOG2E)
        wq_pe = wq[:, :, MLA_NOPE:]
        zq = jnp.zeros((MLA_Q_RANK, MLA_HEADS, MLA_PAD - MLA_QK), F32)
        wqa = jnp.concatenate([wq, zq], axis=-1).reshape(MLA_Q_RANK, MLA_HEADS * MLA_PAD).astype(BF16)
        wqb = jnp.concatenate([_swap_halves(wq_pe), zq], axis=-1).reshape(MLA_Q_RANK, MLA_HEADS * LANES).astype(BF16)
        wkv = w_kv_b[l].reshape(MLA_KV_RANK, MLA_HEADS, MLA_NOPE + MLA_V)
        wk = wkv[:, :, :MLA_NOPE].reshape(MLA_KV_RANK, MLA_OUT).astype(BF16)
        wvt = wkv[:, :, MLA_NOPE:].reshape(MLA_KV_RANK, MLA_OUT).T.astype(BF16)
        w_router = jnp.concatenate(
            [w_expert_router[l], w_group[l], jnp.zeros((d, LANES - N_EXPERTS - N_GROUPS), F32)], axis=1)
        wr_hi = w_router.astype(BF16)
        wr_lo = (w_router - wr_hi.astype(F32)).astype(BF16)
        b_router = jnp.concatenate(
            [b_expert[l].astype(F32), b_group[l].astype(F32), jnp.zeros((LANES - N_EXPERTS - N_GROUPS,), F32)]
        ).reshape(1, LANES)

        lat, sb_qk, sb_vt = _proj_in(xt, _row(norm_mix[l]), w_lat_t, w_sb_t)
        q, k, vt = _mla_proj(lat, pos, invf, _row(norm_q_lat[l]), _row(norm_kv_lat[l]), wqa, wqb, wk, wvt)
        o_mla = _mla_attn(q, k, vt, batch, seq)
        o_sb = _sb_attn(sb_qk, sb_vt, batch, seq)
        x1 = _out_proj(xt, o_mla, o_sb, _row(norm_mla_out[l]), _row(norm_sb_out[l]), w_out[l].astype(BF16))

        kv = _mem_kv(mem.reshape(batch * MEM_LEN, d), _row(norm_mem_src[l]), w_mem_kv[l].astype(BF16))
        x2, h2, route = _mem_route(x1, _row(norm_mem_x[l]), w_mem_q[l].astype(BF16), kv,
                                   w_mem_o[l].astype(BF16), _row(norm_ffn[l]), wr_hi, wr_lo, b_router, seq)

        slots, meta = _slots(route)
        xs = _dispatch(slots[0], slots[1], meta[2], meta[3], meta[4], h2, n_slots)
        ys = _experts(meta[0], meta[1], xs, w_gate[l], w_up[l], w_down[l])
        xt = _combine(slots[0], slots[1], x2, route, _row(norm_final), ys)
    return xt.reshape(batch, seq, d)
```

```python
import functools

import jax
import jax.numpy as jnp
from jax import lax
from jax.experimental import pallas as pl
from jax.experimental.pallas import tpu as pltpu

F32 = jnp.float32
BF16 = jnp.bfloat16

EPS = 1e-6
ROPE_THETA = 10000.0

D_MODEL = 2048
MEM_LEN = 256
MLA_HEADS = 8
MLA_NOPE = 128
MLA_ROPE = 64
MLA_QK = MLA_NOPE + MLA_ROPE
MLA_V = 128
MLA_Q_RANK = 512
MLA_KV_RANK = 256
MLA_PAD = 256
SB_HEADS = 8
SB_DIM = 128
MLA_OUT = MLA_HEADS * MLA_V
SB_OUT = SB_HEADS * SB_DIM
MEM_HEADS = 4
MEM_DIM = 128
N_GROUPS = 4
EXPERTS_PER_GROUP = 8
N_EXPERTS = N_GROUPS * EXPERTS_PER_GROUP
D_EXPERT = 512

LANES = 128
LAT_COLS = 1024

SLOT_TILE = 256
WEIGHT_SLOTS = 2
VMEM_LIMIT = 56 * 1024 * 1024


def _rms(x, g):
    return x * lax.rsqrt(jnp.mean(x * x, axis=-1, keepdims=True) + EPS) * g


def _dot(a, b):
    return jnp.dot(a, b, preferred_element_type=F32)


def _dot_nt(a, b):
    return lax.dot_general(a, b, (((1,), (1,)), ((), ())), preferred_element_type=F32)


def _split_bf16(x):
    hi = x.astype(BF16)
    lo = (x - hi.astype(F32)).astype(BF16)
    return hi, lo


def _params(*sem):
    return pltpu.CompilerParams(dimension_semantics=sem, vmem_limit_bytes=VMEM_LIMIT)


def _proj_in_kernel(x_ref, g_ref, wl_ref, ws_ref, lat_ref, qk_ref, vt_ref, h_ref):
    j = pl.program_id(1)
    last = pl.num_programs(1) - 1

    @pl.when(j == 0)
    def _():
        h_ref[...] = _rms(x_ref[...], g_ref[...]).astype(BF16)
        lat_ref[...] = _dot_nt(h_ref[...], wl_ref[...])

    @pl.when((j > 0) & (j < last))
    def _():
        qk_ref[...] = _dot_nt(h_ref[...], ws_ref[...]).astype(BF16)

    @pl.when(j == last)
    def _():
        vt_ref[...] = _dot_nt(ws_ref[...], h_ref[...]).astype(BF16)


def _proj_in(x, g, w_lat, w_sb, tm=512):
    t = x.shape[0]
    tn = LAT_COLS
    n_sb = w_sb.shape[0] // tn
    return pl.pallas_call(
        _proj_in_kernel,
        grid=(t // tm, n_sb + 1),
        in_specs=[
            pl.BlockSpec((tm, D_MODEL), lambda i, j: (i, 0)),
            pl.BlockSpec((1, D_MODEL), lambda i, j: (0, 0)),
            pl.BlockSpec((tn, D_MODEL), lambda i, j: (0, 0)),
            pl.BlockSpec((tn, D_MODEL), lambda i, j: (jnp.maximum(j - 1, 0), 0)),
        ],
        out_specs=[
            pl.BlockSpec((tm, tn), lambda i, j: (i, 0)),
            pl.BlockSpec((tm, tn), lambda i, j: (i, jnp.clip(j - 1, 0, n_sb - 2))),
            pl.BlockSpec((SB_OUT, tm), lambda i, j: (0, i)),
        ],
        out_shape=[
            jax.ShapeDtypeStruct((t, LAT_COLS), F32),
            jax.ShapeDtypeStruct((t, 2 * SB_OUT), BF16),
            jax.ShapeDtypeStruct((SB_OUT, t), BF16),
        ],
        scratch_shapes=[pltpu.VMEM((tm, D_MODEL), BF16)],
        compiler_params=_params("parallel", "arbitrary"),
        name="proj_in",
    )(x, g, w_lat, w_sb)


def _mla_proj_kernel(lat_ref, pos_ref, invf_ref, gq_ref, gkv_ref, wqa_ref, wqb_ref, wk_ref, wvt_ref,
                     q_ref, k_ref, vt_ref):
    cq = _rms(lat_ref[:, :MLA_Q_RANK], gq_ref[...]).astype(BF16)
    ckv = _rms(lat_ref[:, MLA_Q_RANK:MLA_Q_RANK + MLA_KV_RANK], gkv_ref[...]).astype(BF16)
    ang = pos_ref[...].astype(F32) * invf_ref[...]
    live = lax.broadcasted_iota(jnp.int32, ang.shape, 1) < MLA_ROPE
    cos2 = jnp.where(live, jnp.cos(ang), 0.0)
    sin2 = jnp.where(live, jnp.sin(ang), 0.0)

    qa = _dot(cq, wqa_ref[...])
    qb = _dot(cq, wqb_ref[...])
    kn = _dot(ckv, wk_ref[...])
    vt_ref[...] = _dot_nt(wvt_ref[...], ckv).astype(BF16)
    k_pe = (lat_ref[:, 768:896] * cos2 + lat_ref[:, 896:1024] * sin2).astype(BF16)
    for h in range(MLA_HEADS):
        lo = h * MLA_PAD
        mid = lo + LANES
        q_ref[:, lo:mid] = qa[:, lo:mid].astype(BF16)
        q_ref[:, mid:mid + LANES] = (qa[:, mid:mid + LANES] * cos2
                                     + qb[:, h * LANES:(h + 1) * LANES] * sin2).astype(BF16)
        k_ref[:, lo:mid] = kn[:, h * LANES:(h + 1) * LANES].astype(BF16)
        k_ref[:, mid:mid + LANES] = k_pe


def _mla_proj(lat, pos, invf, gq, gkv, wqa, wqb, wk, wvt, tm=512):
    t = lat.shape[0]
    full = lambda a: pl.BlockSpec(a.shape, lambda i: (0, 0))
    return pl.pallas_call(
        _mla_proj_kernel,
        grid=(t // tm,),
        in_specs=[
            pl.BlockSpec((tm, LAT_COLS), lambda i: (i, 0)),
            pl.BlockSpec((tm, 1), lambda i: (i, 0)),
            full(invf), full(gq), full(gkv), full(wqa), full(wqb), full(wk), full(wvt),
        ],
        out_specs=[
            pl.BlockSpec((tm, MLA_HEADS * MLA_PAD), lambda i: (i, 0)),
            pl.BlockSpec((tm, MLA_HEADS * MLA_PAD), lambda i: (i, 0)),
            pl.BlockSpec((MLA_OUT, tm), lambda i: (0, i)),
        ],
        out_shape=[
            jax.ShapeDtypeStruct((t, MLA_HEADS * MLA_PAD), BF16),
            jax.ShapeDtypeStruct((t, MLA_HEADS * MLA_PAD), BF16),
            jax.ShapeDtypeStruct((MLA_OUT, t), BF16),
        ],
        compiler_params=_params("parallel"),
        name="mla_proj",
    )(lat, pos, invf, gq, gkv, wqa, wqb, wk, wvt)


LOG2E = 1.4426950408889634


def _three_stage(n_pairs, stage_a, stage_b, stage_c):
    a_issue, a_finish = stage_a
    b_issue, b_finish = stage_b
    c_issue, c_finish = stage_c

    def run_a(n, slot, first):
        a_finish(n, slot, a_issue(n, slot, first), first)

    run_a(0, 0, True)
    run_a(1, 1, True)
    b_finish(0, 0, b_issue(0, 0))

    def half(na, sa, nb, sb, nc, sc):
        ra = a_issue(na, sa, False)
        rb = b_issue(nb, sb)
        rc = c_issue(nc, sc)
        a_finish(na, sa, ra, False)
        b_finish(nb, sb, rb)
        c_finish(nc, sc, rc)

    def body(p, carry):
        half(2 * p, 0, 2 * p - 1, 1, 2 * p - 2, 0)
        half(2 * p + 1, 1, 2 * p, 0, 2 * p - 1, 1)
        return carry

    lax.fori_loop(1, n_pairs + 1, body, 0)
    last = 2 * n_pairs + 1
    rb = b_issue(last, 1)
    rc = c_issue(last - 1, 0)
    b_finish(last, 1, rb)
    c_finish(last - 1, 0, rc)
    c_finish(last, 1, c_issue(last, 1))


def _two_stage(n_pairs, produce, consume):
    produce(0, 0, True)
    produce(1, 1, True)
    consume(0, 0)

    def body(p, carry):
        produce(2 * p, 0, False)
        consume(2 * p - 1, 1)
        produce(2 * p + 1, 1, False)
        consume(2 * p, 0)
        return carry

    lax.fori_loop(1, n_pairs + 1, body, 0)
    consume(2 * n_pairs + 1, 1)


MLA_SUM_ROWS = 16
MLA_GROUP = 8


def _mla_attn_kernel(q_ref, k_ref, vt_ref, o_ref, s_ref, m_ref, acc_ref, *, tq, tk):
    i = pl.program_id(2)
    heads = range(MLA_GROUP)
    dq, dv = MLA_PAD, MLA_V
    m_ref[...] = jnp.full(m_ref.shape, -jnp.inf, F32)
    acc_ref[...] = jnp.zeros(acc_ref.shape, F32)
    ones = jnp.ones((MLA_SUM_ROWS, tk), BF16)

    def key_start(n):
        tile = jnp.where(n < 2, 2 * i + n, 2 * i + 1 - n)
        return pl.multiple_of(tile * tk, tk)

    def produce(n, slot, diagonal):
        ks = key_start(n)
        for g in heads:
            st = _dot_nt(k_ref[pl.ds(ks, tk), g * dq:(g + 1) * dq], q_ref[:, g * dq:(g + 1) * dq])
            if diagonal:
                kpos = ks + lax.broadcasted_iota(jnp.int32, st.shape, 0)
                qpos = i * tq + lax.broadcasted_iota(jnp.int32, st.shape, 1)
                st = jnp.where(kpos <= qpos, st, -jnp.inf)
            s_ref[g, slot] = st

    def consume(n, slot):
        ks = key_start(n)
        for g in heads:
            st = s_ref[g, slot]
            m_old = m_ref[g]
            m_new = jnp.maximum(m_old, jnp.max(st, axis=0, keepdims=True))
            alpha = jnp.exp2(m_old - m_new)
            p = jnp.exp2(st - m_new).astype(BF16)
            v_ones = jnp.concatenate([vt_ref[g * dv:(g + 1) * dv, pl.ds(ks, tk)], ones], axis=0)
            acc_ref[g] = alpha * acc_ref[g] + _dot(v_ones, p)
            m_ref[g] = m_new

    _two_stage(i, produce, consume)
    for g in heads:
        o_ref[:, g * dv:(g + 1) * dv] = (acc_ref[g, :dv, :] / acc_ref[g, dv:dv + 1, :]).T


def _mla_attn(q, k, vt, batch, seq, tq=512):
    tk = tq // 2
    nq = seq // tq
    groups = MLA_HEADS // MLA_GROUP
    kern = functools.partial(_mla_attn_kernel, tq=tq, tk=tk)
    return pl.pallas_call(
        kern,
        grid=(batch, groups, nq),
        in_specs=[
            pl.BlockSpec((tq, MLA_GROUP * MLA_PAD), lambda b, h, i: (b * nq + i, h)),
            pl.BlockSpec((seq, MLA_GROUP * MLA_PAD), lambda b, h, i: (b, h), pipeline_mode=pl.Buffered(1)),
            pl.BlockSpec((MLA_GROUP * MLA_V, seq), lambda b, h, i: (h, b), pipeline_mode=pl.Buffered(1)),
        ],
        out_specs=pl.BlockSpec((tq, MLA_GROUP * MLA_V), lambda b, h, i: (b * nq + i, h)),
        out_shape=jax.ShapeDtypeStruct((batch * seq, MLA_OUT), F32),
        scratch_shapes=[
            pltpu.VMEM((MLA_GROUP, 2, tk, tq), F32),
            pltpu.VMEM((MLA_GROUP, 1, tq), F32),
            pltpu.VMEM((MLA_GROUP, MLA_V + MLA_SUM_ROWS, tq), F32),
        ],
        compiler_params=_params("parallel", "parallel", "arbitrary"),
        name="mla_attn",
    )(q, k, vt)


SB_EXP_CLAMP = 126.0
SB_GROUP = 4


def _sb_attn_kernel(q_ref, k_ref, vt_ref, o_ref, z_ref, hl_ref, arg_ref, acc_ref, c_ref, *, tq, tk):
    i = pl.program_id(2)
    heads = range(SB_GROUP)
    d = SB_DIM
    acc_ref[...] = jnp.zeros(acc_ref.shape, F32)
    c_ref[...] = jnp.zeros(c_ref.shape, F32)
    col = lax.broadcasted_iota(jnp.int32, (tk, tk), 1)
    row = lax.broadcasted_iota(jnp.int32, (tk, tk), 0)
    neg_tri = jnp.where(col >= row, -1.0, 0.0).astype(BF16)

    def key_start(n):
        return pl.multiple_of((2 * i + 1 - n) * tk, tk)

    def a_issue(n, slot, masked):
        ks = key_start(n)
        return [_dot_nt(k_ref[pl.ds(ks, tk), g * d:(g + 1) * d], q_ref[:, g * d:(g + 1) * d])
                for g in heads]

    def a_finish(n, slot, z2s, masked):
        for g in heads:
            z2 = z2s[g]
            sp = jnp.maximum(z2, jnp.log2(1.0 + jnp.exp2(jnp.minimum(z2, SB_EXP_CLAMP))))
            if masked:
                kpos = key_start(n) + lax.broadcasted_iota(jnp.int32, z2.shape, 0)
                qpos = i * tq + lax.broadcasted_iota(jnp.int32, z2.shape, 1)
                mask = kpos < qpos
                sp = jnp.where(mask, sp, 0.0)
                z2 = jnp.where(mask, z2, -jnp.inf)
            hl_ref[g, slot] = sp.astype(BF16)
            z_ref[g, slot] = z2

    def b_issue(n, slot):
        return [_dot(neg_tri, hl_ref[g, slot]) for g in heads]

    def b_finish(n, slot, laters):
        for g in heads:
            arg_ref[g, slot] = z_ref[g, slot] + laters[g] + c_ref[g]
            c_ref[g] += laters[g][0:1, :]

    def c_issue(n, slot):
        ks = key_start(n)
        return [_dot(vt_ref[g * d:(g + 1) * d, pl.ds(ks, tk)], jnp.exp2(arg_ref[g, slot]).astype(BF16))
                for g in heads]

    def c_finish(n, slot, pvs):
        for g in heads:
            acc_ref[g] += pvs[g]

    _three_stage(i, (a_issue, a_finish), (b_issue, b_finish), (c_issue, c_finish))
    for g in heads:
        o_ref[:, g * d:(g + 1) * d] = acc_ref[g].T


def _sb_attn(qk, vt, batch, seq, tq=512):
    tk = tq // 2
    nq = seq // tq
    groups = SB_HEADS // SB_GROUP
    gd = SB_GROUP * SB_DIM
    kern = functools.partial(_sb_attn_kernel, tq=tq, tk=tk)
    return pl.pallas_call(
        kern,
        grid=(batch, groups, nq),
        in_specs=[
            pl.BlockSpec((tq, gd), lambda b, h, i: (b * nq + i, h)),
            pl.BlockSpec((seq, gd), lambda b, h, i: (b, groups + h)),
            pl.BlockSpec((gd, seq), lambda b, h, i: (h, b)),
        ],
        out_specs=pl.BlockSpec((tq, gd), lambda b, h, i: (b * nq + i, h)),
        out_shape=jax.ShapeDtypeStruct((batch * seq, SB_OUT), F32),
        scratch_shapes=[
            pltpu.VMEM((SB_GROUP, 2, tk, tq), F32),
            pltpu.VMEM((SB_GROUP, 2, tk, tq), BF16),
            pltpu.VMEM((SB_GROUP, 2, tk, tq), F32),
            pltpu.VMEM((SB_GROUP, SB_DIM, tq), F32),
            pltpu.VMEM((SB_GROUP, 1, tq), F32),
        ],
        compiler_params=_params("parallel", "parallel", "arbitrary"),
        name="sb_attn",
    )(qk, qk, vt)


def _out_proj_kernel(x_ref, oa_ref, ob_ref, ga_ref, gb_ref, w_ref, y_ref):
    na = _rms(oa_ref[...], ga_ref[...]).astype(BF16)
    nb = _rms(ob_ref[...], gb_ref[...]).astype(BF16)
    y_ref[...] = x_ref[...] + _dot(na, w_ref[:MLA_OUT, :]) + _dot(nb, w_ref[MLA_OUT:, :])


def _out_proj(x, oa, ob, ga, gb, w, tm=512):
    t = x.shape[0]
    full = lambda a: pl.BlockSpec(a.shape, lambda i: (0, 0))
    return pl.pallas_call(
        _out_proj_kernel,
        grid=(t // tm,),
        in_specs=[
            pl.BlockSpec((tm, D_MODEL), lambda i: (i, 0)),
            pl.BlockSpec((tm, MLA_OUT), lambda i: (i, 0)),
            pl.BlockSpec((tm, SB_OUT), lambda i: (i, 0)),
            full(ga), full(gb), full(w),
        ],
        out_specs=pl.BlockSpec((tm, D_MODEL), lambda i: (i, 0)),
        out_shape=jax.ShapeDtypeStruct((t, D_MODEL), F32),
        compiler_params=_params("parallel"),
        name="out_proj",
    )(x, oa, ob, ga, gb, w)


def _mem_kv_kernel(mem_ref, g_ref, w_ref, kv_ref):
    kv_ref[...] = _dot(_rms(mem_ref[...], g_ref[...]).astype(BF16), w_ref[...]).astype(BF16)


def _mem_kv(mem, g, w):
    m = mem.shape[0]
    n = w.shape[1]
    full = lambda a: pl.BlockSpec(a.shape, lambda i: (0, 0))
    return pl.pallas_call(
        _mem_kv_kernel,
        grid=(1,),
        in_specs=[full(mem), full(g), full(w)],
        out_specs=pl.BlockSpec((m, n), lambda i: (0, 0)),
        out_shape=jax.ShapeDtypeStruct((m, n), BF16),
        compiler_params=_params("arbitrary"),
        name="mem_kv",
    )(mem, g, w)


GROUP_LANE0 = N_EXPERTS


def _mem_route_kernel(x_ref, gx_ref, wq_ref, kv_ref, wo_ref, gf_ref, wrhl_ref, br_ref,
                      x2_ref, h2_ref, route_ref):
    x1 = x_ref[...]
    q = _dot(_rms(x1, gx_ref[...]).astype(BF16), wq_ref[...]).astype(BF16)
    scale = MEM_DIM ** -0.5
    kw = MEM_HEADS * MEM_DIM
    heads = []
    for h in range(MEM_HEADS):
        lo = h * MEM_DIM
        s = _dot_nt(q[:, lo:lo + MEM_DIM], kv_ref[:, lo:lo + MEM_DIM]) * scale
        e = jnp.exp(s - jnp.max(s, axis=-1, keepdims=True))
        p = (e / jnp.sum(e, axis=-1, keepdims=True)).astype(BF16)
        heads.append(_dot(p, kv_ref[:, kw + lo:kw + lo + MEM_DIM]).astype(BF16))
    o = jnp.concatenate(heads, axis=-1)
    x2 = x1 + _dot(o, wo_ref[...])
    x2_ref[...] = x2
    h2 = _rms(x2, gf_ref[...])
    h2_ref[...] = h2

    hh, hl = _split_bf16(h2)
    both = _dot(hh, wrhl_ref[...])
    lg = both[:, :LANES] + both[:, LANES:] + _dot(hl, wrhl_ref[:, :LANES]) + br_ref[...]
    lane = lax.broadcasted_iota(jnp.int32, lg.shape, 1)
    big = jnp.int32(1 << 20)
    ninf = -jnp.inf

    def lane_max(v):
        return jnp.max(v, axis=-1, keepdims=True)

    def first_lane(cond):
        return jnp.min(jnp.where(cond, lane, big), axis=-1, keepdims=True)

    is_g = (lane >= GROUP_LANE0) & (lane < GROUP_LANE0 + N_GROUPS)
    g_max = lane_max(jnp.where(is_g, lg, ninf))
    g_sum = jnp.sum(jnp.where(is_g, jnp.exp(lg - g_max), 0.0), axis=-1, keepdims=True)
    p_g = 1.0 / g_sum
    g_idx = first_lane(is_g & (lg == g_max)) - GROUP_LANE0
    in_grp = (lane < N_EXPERTS) & ((lane // EXPERTS_PER_GROUP) == g_idx)
    e_max = lane_max(jnp.where(in_grp, lg, ninf))
    e_sum = jnp.sum(jnp.where(in_grp, jnp.exp(lg - e_max), 0.0), axis=-1, keepdims=True)
    i1 = first_lane(in_grp & (lg == e_max))
    rest = in_grp & (lane != i1)
    e_max2 = lane_max(jnp.where(rest, lg, ninf))
    i2 = first_lane(rest & (lg == e_max2))
    p1 = 1.0 / e_sum
    p2 = jnp.exp(e_max2 - e_max) / e_sum
    den = p1 + p2
    gate1 = p_g * (p1 / den)
    gate2 = p_g * (p2 / den)
    route = jnp.where(lane == 0, i1.astype(F32),
                      jnp.where(lane == 1, i2.astype(F32),
                                jnp.where(lane == 2, gate1,
                                          jnp.where(lane == 3, gate2, 0.0))))
    route_ref[...] = route


def _mem_route(x1, gx, wq, kv, wo, gf, wrhl, br, seq, tm=512):
    t = x1.shape[0]
    per_batch = seq // tm
    full = lambda a: pl.BlockSpec(a.shape, lambda i: (0, 0))
    return pl.pallas_call(
        _mem_route_kernel,
        grid=(t // tm,),
        in_specs=[
            pl.BlockSpec((tm, D_MODEL), lambda i: (i, 0)),
            full(gx), full(wq),
            pl.BlockSpec((MEM_LEN, kv.shape[1]), lambda i: (i // per_batch, 0)),
            full(wo), full(gf), full(wrhl), full(br),
        ],
        out_specs=[
            pl.BlockSpec((tm, D_MODEL), lambda i: (i, 0)),
            pl.BlockSpec((tm, D_MODEL), lambda i: (i, 0)),
            pl.BlockSpec((tm, LANES), lambda i: (i, 0)),
        ],
        out_shape=[
            jax.ShapeDtypeStruct((t, D_MODEL), F32),
            jax.ShapeDtypeStruct((t, D_MODEL), F32),
            jax.ShapeDtypeStruct((t, LANES), F32),
        ],
        compiler_params=_params("parallel"),
        name="mem_route",
    )(x1, gx, wq, kv, wo, gf, wrhl, br)


SLOT_BLK = 256


def _slots_kernel(route_ref, slot_ref, meta_ref, cum_ref, *, n_tok):
    nblk = n_tok // SLOT_BLK
    e_iota = lax.broadcasted_iota(jnp.int32, (LANES, SLOT_BLK), 0)
    incl = (lax.broadcasted_iota(jnp.int32, (SLOT_BLK, SLOT_BLK), 0)
            <= lax.broadcasted_iota(jnp.int32, (SLOT_BLK, SLOT_BLK), 1)).astype(BF16)

    def onehots(b):
        ts = pl.multiple_of(b * SLOT_BLK, SLOT_BLK)
        ids = route_ref[pl.ds(ts, SLOT_BLK), :].T
        oh1 = e_iota == ids[0:1, :].astype(jnp.int32)
        oh2 = e_iota == ids[1:2, :].astype(jnp.int32)
        return ts, oh1, oh2

    def count(b, carry):
        ts, oh1, oh2 = onehots(b)
        cnt = (oh1 | oh2).astype(F32).astype(BF16)
        c = _dot(cnt, incl) + carry
        cum_ref[:, pl.ds(ts, SLOT_BLK)] = c
        return c[:, SLOT_BLK - 1:SLOT_BLK]

    total = lax.fori_loop(0, nblk, count, jnp.zeros((LANES, 1), F32))
    tiles = jnp.floor((total + (SLOT_TILE - 1)) * (1.0 / SLOT_TILE))
    below = (lax.broadcasted_iota(jnp.int32, (LANES, LANES), 1)
             < lax.broadcasted_iota(jnp.int32, (LANES, LANES), 0)).astype(BF16)
    tile_lo = _dot(below, jnp.broadcast_to(tiles, (LANES, LANES)).astype(BF16))
    base = tile_lo[:, 0:1] * SLOT_TILE

    def assign(b, carry):
        ts, oh1, oh2 = onehots(b)
        pos = base + cum_ref[:, pl.ds(ts, SLOT_BLK)] - 1.0
        s1 = jnp.sum(jnp.where(oh1, pos, 0.0), axis=0, keepdims=True)
        s2 = jnp.sum(jnp.where(oh2, pos, 0.0), axis=0, keepdims=True)
        row = lax.broadcasted_iota(jnp.int32, (8, SLOT_BLK), 0)
        slot_ref[:, pl.ds(ts, SLOT_BLK)] = jnp.where(row == 0, s1, jnp.where(row == 1, s2, 0.0)).astype(jnp.int32)
        return carry

    lax.fori_loop(0, nblk, assign, 0)

    tile_hi = tile_lo + tiles
    tix = lax.broadcasted_iota(jnp.int32, (LANES, LANES), 1).astype(F32)
    is_e = lax.broadcasted_iota(jnp.int32, (LANES, LANES), 0) < N_EXPERTS
    owner = jnp.sum(jnp.where(is_e & (tile_hi <= tix), 1.0, 0.0), axis=0, keepdims=True)
    n_tiles = jnp.max(jnp.where(is_e, tile_hi, 0.0), axis=0, keepdims=True)
    valid = tix[0:1, :] < n_tiles
    last_owner = jnp.max(jnp.where(valid, owner, 0.0), axis=1, keepdims=True)
    owner = jnp.where(valid, owner, last_owner)
    eye = (lax.broadcasted_iota(jnp.int32, (LANES, LANES), 0)
           == lax.broadcasted_iota(jnp.int32, (LANES, LANES), 1))
    to_lanes = lambda colvec: jnp.sum(jnp.where(eye, colvec, 0.0), axis=0, keepdims=True)
    pad_first = to_lanes(base + total)
    pad_end = to_lanes(base + tiles * SLOT_TILE)
    row = lax.broadcasted_iota(jnp.int32, (8, LANES), 0)
    meta = jnp.where(row == 0, owner,
                     jnp.where(row == 1, valid.astype(F32),
                               jnp.where(row == 2, pad_first,
                                         jnp.where(row == 3, pad_end,
                                                   jnp.where(row == 4, n_tiles, 0.0)))))
    meta_ref[...] = meta.astype(jnp.int32)


def _slots(route):
    t = route.shape[0]
    kern = functools.partial(_slots_kernel, n_tok=t)
    return pl.pallas_call(
        kern,
        grid=(1,),
        in_specs=[pl.BlockSpec(route.shape, lambda i: (0, 0))],
        out_specs=[
            pl.BlockSpec((8, t), lambda i: (0, 0)),
            pl.BlockSpec((8, LANES), lambda i: (0, 0)),
        ],
        out_shape=[
            jax.ShapeDtypeStruct((8, t), jnp.int32),
            jax.ShapeDtypeStruct((8, LANES), jnp.int32),
        ],
        scratch_shapes=[pltpu.VMEM((LANES, t), F32)],
        compiler_params=_params("arbitrary"),
        name="slots",
    )(route)


def _row_copy(src_ref, src_row, dst_ref, dst_row, sem):
    return pltpu.make_async_copy(src_ref.at[pl.ds(src_row, 1), :], dst_ref.at[pl.ds(dst_row, 1), :], sem)


def _dispatch_kernel(s1_ref, s2_ref, pad_first_ref, pad_end_ref, n_tiles_ref, h_ref, xs_ref, zero_ref,
                     sem, pad_sem, *, tm):
    i = pl.program_id(0)
    t0 = i * tm
    n_slot_tiles = xs_ref.shape[0] // SLOT_TILE

    def copies(r):
        return (_row_copy(h_ref, r, xs_ref, s1_ref[t0 + r], sem),
                _row_copy(h_ref, r, xs_ref, s2_ref[t0 + r], sem))

    def issue(r, c):
        first, second = copies(r)
        first.start(priority=0)
        second.start(priority=1)
        return c

    lax.fori_loop(0, tm, issue, 0, unroll=8)

    @pl.when(i == pl.num_programs(0) - 1)
    def _():
        zero_ref[...] = jnp.zeros(zero_ref.shape, F32)

        def pad_copy(slot):
            return _row_copy(zero_ref, 0, xs_ref, slot, pad_sem)

        def fill(e, c):
            def one(s, cc):
                pad_copy(s).start()
                return cc

            return lax.fori_loop(pad_first_ref[e], pad_end_ref[e], one, c)

        lax.fori_loop(0, N_EXPERTS, fill, 0)

        def tile_copy(tile):
            return pltpu.make_async_copy(zero_ref, xs_ref.at[pl.ds(tile * SLOT_TILE, SLOT_TILE), :], pad_sem)

        def fill_tile(tile, c):
            tile_copy(tile).start()
            return c

        lax.fori_loop(n_tiles_ref[0], n_slot_tiles, fill_tile, 0)

        def settle(e, c):
            def one(s, cc):
                pad_copy(s).wait()
                return cc

            return lax.fori_loop(pad_first_ref[e], pad_end_ref[e], one, c)

        lax.fori_loop(0, N_EXPERTS, settle, 0)

        def settle_tile(tile, c):
            tile_copy(tile).wait()
            return c

        lax.fori_loop(n_tiles_ref[0], n_slot_tiles, settle_tile, 0)

    def drain(r, c):
        for cp in copies(r):
            cp.wait()
        return c

    lax.fori_loop(0, tm, drain, 0, unroll=8)


def _dispatch(slot1, slot2, pad_first, pad_end, n_tiles, h2, n_slots, tm=256):
    t = h2.shape[0]
    kern = functools.partial(_dispatch_kernel, tm=tm)
    return pl.pallas_call(
        kern,
        grid_spec=pltpu.PrefetchScalarGridSpec(
            num_scalar_prefetch=5,
            grid=(t // tm,),
            in_specs=[pl.BlockSpec((tm, D_MODEL), lambda i, *_: (i, 0))],
            out_specs=pl.BlockSpec(memory_space=pl.ANY),
            scratch_shapes=[
                pltpu.VMEM((SLOT_TILE, D_MODEL), F32),
                pltpu.SemaphoreType.DMA(()),
                pltpu.SemaphoreType.DMA(()),
            ],
        ),
        out_shape=jax.ShapeDtypeStruct((n_slots, D_MODEL), F32),
        compiler_params=_params("arbitrary"),
        name="dispatch",
    )(slot1, slot2, pad_first, pad_end, n_tiles, h2)


def _experts_kernel(own_ref, valid_ref, xs_ref, wg_hbm, wu_hbm, wd_hbm, ys_ref,
                    wgf, wuf, wdf, wgb, wub, wdb, wsem, wslot_ref):
    i = pl.program_id(0)
    nt = pl.num_programs(0)
    at = lambda ref, j: ref[jnp.minimum(j, nt - 1)]

    def weight_copies(e, s):
        return (pltpu.make_async_copy(wg_hbm.at[e], wgf.at[s], wsem.at[s]),
                pltpu.make_async_copy(wu_hbm.at[e], wuf.at[s], wsem.at[s]),
                pltpu.make_async_copy(wd_hbm.at[e], wdf.at[s], wsem.at[s]))

    def next_expert_tile(j0):
        e0 = at(own_ref, j0)
        return lax.while_loop(lambda j: (j < nt) & (at(own_ref, j) == e0), lambda j: j + 1, j0 + 1)

    def start_weights(j, s):
        @pl.when((j < nt) & (at(valid_ref, j) > 0))
        def _():
            for cp in weight_copies(at(own_ref, j), s):
                cp.start()

    @pl.when(i == 0)
    def _():
        wslot_ref[0] = 0
        j = 0
        for s in range(WEIGHT_SLOTS - 1):
            start_weights(j, s)
            j = next_expert_tile(j)

    @pl.when(valid_ref[i] > 0)
    def _():
        e = own_ref[i]

        @pl.when((i == 0) | (own_ref[jnp.maximum(i - 1, 0)] != e))
        def _():
            s = wslot_ref[0]
            for cp in weight_copies(e, s):
                cp.wait()
            j = i
            for _ in range(WEIGHT_SLOTS - 1):
                j = next_expert_tile(j)
            start_weights(j, (s + WEIGHT_SLOTS - 1) % WEIGHT_SLOTS)

            wgb[...] = wgf[s].astype(BF16)
            wub[...] = wuf[s].astype(BF16)
            wdb[...] = wdf[s].astype(BF16)
            wslot_ref[0] = (s + 1) % WEIGHT_SLOTS

        x = xs_ref[...].astype(BF16)
        a = _dot(x, wgb[...])
        u = _dot(x, wub[...])
        act = (a * jax.nn.sigmoid(a) * u).astype(BF16)
        ys_ref[...] = _dot(act, wdb[...])

    @pl.when(valid_ref[i] == 0)
    def _():
        ys_ref[...] = jnp.zeros(ys_ref.shape, F32)


def _experts(own, valid, xs, wg, wu, wd):
    n_tiles = xs.shape[0] // SLOT_TILE
    hbm = pl.BlockSpec(memory_space=pl.ANY)
    tile = (SLOT_TILE, D_MODEL)
    return pl.pallas_call(
        _experts_kernel,
        grid_spec=pltpu.PrefetchScalarGridSpec(
            num_scalar_prefetch=2,
            grid=(n_tiles,),
            in_specs=[
                pl.BlockSpec(tile, lambda i, o, v: (jnp.where(v[i] > 0, i, 0), 0)),
                hbm, hbm, hbm,
            ],
            out_specs=pl.BlockSpec(tile, lambda i, o, v: (i, 0)),
            scratch_shapes=[
                pltpu.VMEM((WEIGHT_SLOTS, D_MODEL, D_EXPERT), F32),
                pltpu.VMEM((WEIGHT_SLOTS, D_MODEL, D_EXPERT), F32),
                pltpu.VMEM((WEIGHT_SLOTS, D_EXPERT, D_MODEL), F32),
                pltpu.VMEM((D_MODEL, D_EXPERT), BF16),
                pltpu.VMEM((D_MODEL, D_EXPERT), BF16),
                pltpu.VMEM((D_EXPERT, D_MODEL), BF16),
                pltpu.SemaphoreType.DMA((WEIGHT_SLOTS,)),
                pltpu.SMEM((1,), jnp.int32),
            ],
        ),
        out_shape=jax.ShapeDtypeStruct(xs.shape, F32),
        compiler_params=_params("arbitrary"),
        name="experts",
    )(own, valid, xs, wg, wu, wd)


def _combine_kernel(s1_ref, s2_ref, x_ref, route_ref, g_ref, ys_ref, y_ref, buf_ref, sem, *, tm):
    i = pl.program_id(0)
    n = pl.num_programs(0)

    def copies(tile, p, r):
        t = tile * tm + r
        return (_row_copy(ys_ref, s1_ref[t], buf_ref.at[p, 0], r, sem.at[p]),
                _row_copy(ys_ref, s2_ref[t], buf_ref.at[p, 1], r, sem.at[p]))

    def gather_start(tile, p):
        def issue(r, c):
            first, second = copies(tile, p, r)
            first.start(priority=0)
            second.start(priority=1)
            return c

        lax.fori_loop(0, tm, issue, 0, unroll=8)

    def gather_wait(tile, p):
        def drain(r, c):
            for cp in copies(tile, p, r):
                cp.wait()
            return c

        lax.fori_loop(0, tm, drain, 0, unroll=8)

    @pl.when(i == 0)
    def _():
        gather_start(0, 0)

    p = i % 2

    @pl.when(i + 1 < n)
    def _():
        gather_start(i + 1, 1 - p)

    gather_wait(i, p)
    y = x_ref[...] + route_ref[:, 2:3] * buf_ref[p, 0] + route_ref[:, 3:4] * buf_ref[p, 1]
    y_ref[...] = _rms(y, g_ref[...])


def _combine(slot1, slot2, x2, route, g, ys, tm=256):
    t = x2.shape[0]
    kern = functools.partial(_combine_kernel, tm=tm)
    return pl.pallas_call(
        kern,
        grid_spec=pltpu.PrefetchScalarGridSpec(
            num_scalar_prefetch=2,
            grid=(t // tm,),
            in_specs=[
                pl.BlockSpec((tm, D_MODEL), lambda i, s1, s2: (i, 0)),
                pl.BlockSpec((tm, LANES), lambda i, s1, s2: (i, 0)),
                pl.BlockSpec((1, D_MODEL), lambda i, s1, s2: (0, 0)),
                pl.BlockSpec(memory_space=pl.ANY),
            ],
            out_specs=pl.BlockSpec((tm, D_MODEL), lambda i, s1, s2: (i, 0)),
            scratch_shapes=[
                pltpu.VMEM((2, 2, tm, D_MODEL), F32),
                pltpu.SemaphoreType.DMA((2,)),
            ],
        ),
        out_shape=jax.ShapeDtypeStruct((t, D_MODEL), F32),
        compiler_params=_params("arbitrary"),
        name="combine",
    )(slot1, slot2, x2, route, g, ys)


def _swap_halves(w):
    half = w.shape[-1] // 2
    return jnp.concatenate([-w[..., half:], w[..., :half]], axis=-1)


def _row(v):
    return v.reshape(1, -1).astype(F32)


def kernel(x, mem, positions, norm_mix, w_in, norm_q_lat, w_q_b, norm_kv_lat, w_kv_b, norm_mla_out, norm_sb_out, w_out, norm_mem_x, norm_mem_src, w_mem_q, w_mem_kv, w_mem_o, norm_ffn, w_group, b_group, w_expert_router, b_expert, w_gate, w_up, w_down, norm_final):
    batch, seq, d = x.shape
    t = batch * seq
    depth = w_in.shape[0]
    assert depth == 1, "single-layer trunk only"
    xt = x.reshape(t, d)
    pos = positions.reshape(t, 1)
    inv_freq = ROPE_THETA ** (-jnp.arange(0, MLA_ROPE, 2, dtype=F32) / MLA_ROPE)
    invf = jnp.concatenate([inv_freq, inv_freq, jnp.zeros((LANES - MLA_ROPE,), F32)]).reshape(1, LANES)
    n_slots = (2 * t // SLOT_TILE + N_EXPERTS) * SLOT_TILE

    for l in range(depth):
        wi_t = w_in[l].T
        lat_w = MLA_Q_RANK + MLA_KV_RANK
        w_kpe_t = wi_t[lat_w:lat_w + MLA_ROPE]
        zpad = jnp.zeros((LANES - MLA_ROPE, d), F32)
        sb0 = lat_w + MLA_ROPE
        half = MLA_ROPE // 2
        w_kpe_swapped_t = jnp.concatenate([-w_kpe_t[half:], w_kpe_t[:half]], axis=0)
        w_lat_t = jnp.concatenate([wi_t[:sb0], zpad, w_kpe_swapped_t, zpad], axis=0).astype(BF16)
        sb_scale = jnp.concatenate([jnp.full((SB_OUT,), SB_DIM ** -0.5 * LOG2E, F32), jnp.ones((2 * SB_OUT,), F32)])
        w_sb_t = (wi_t[sb0:] * sb_scale[:, None]).astype(BF16)
        wq = w_q_b[l].reshape(MLA_Q_RANK, MLA_HEADS, MLA_QK) * (MLA_QK ** -0.5 * LOG2E)
        wq_pe = wq[:, :, MLA_NOPE:]
        zq = jnp.zeros((MLA_Q_RANK, MLA_HEADS, MLA_PAD - MLA_QK), F32)
        wqa = jnp.concatenate([wq, zq], axis=-1).reshape(MLA_Q_RANK, MLA_HEADS * MLA_PAD).astype(BF16)
        wqb = jnp.concatenate([_swap_halves(wq_pe), zq], axis=-1).reshape(MLA_Q_RANK, MLA_HEADS * LANES).astype(BF16)
        wkv = w_kv_b[l].reshape(MLA_KV_RANK, MLA_HEADS, MLA_NOPE + MLA_V)
        wk = wkv[:, :, :MLA_NOPE].reshape(MLA_KV_RANK, MLA_OUT).astype(BF16)
        wvt = wkv[:, :, MLA_NOPE:].reshape(MLA_KV_RANK, MLA_OUT).T.astype(BF16)
        w_router = jnp.concatenate(
            [w_expert_router[l], w_group[l], jnp.zeros((d, LANES - N_EXPERTS - N_GROUPS), F32)], axis=1)
        wr_hi = w_router.astype(BF16)
        wr_hl = jnp.concatenate([wr_hi, (w_router - wr_hi.astype(F32)).astype(BF16)], axis=1)
        b_router = jnp.concatenate(
            [b_expert[l].astype(F32), b_group[l].astype(F32), jnp.zeros((LANES - N_EXPERTS - N_GROUPS,), F32)]
        ).reshape(1, LANES)

        lat, sb_qk, sb_vt = _proj_in(xt, _row(norm_mix[l]), w_lat_t, w_sb_t)
        q, k, vt = _mla_proj(lat, pos, invf, _row(norm_q_lat[l]), _row(norm_kv_lat[l]), wqa, wqb, wk, wvt)
        o_mla = _mla_attn(q, k, vt, batch, seq)
        o_sb = _sb_attn(sb_qk, sb_vt, batch, seq)
        x1 = _out_proj(xt, o_mla, o_sb, _row(norm_mla_out[l]), _row(norm_sb_out[l]), w_out[l].astype(BF16))

        kv = _mem_kv(mem.reshape(batch * MEM_LEN, d), _row(norm_mem_src[l]), w_mem_kv[l].astype(BF16))
        x2, h2, route = _mem_route(x1, _row(norm_mem_x[l]), w_mem_q[l].astype(BF16), kv,
                                   w_mem_o[l].astype(BF16), _row(norm_ffn[l]), wr_hl, b_router, seq)

        slots, meta = _slots(route)
        xs = _dispatch(slots[0], slots[1], meta[2], meta[3], meta[4], h2, n_slots)
        ys = _experts(meta[0], meta[1], xs, w_gate[l], w_up[l], w_down[l])
        xt = _combine(slots[0], slots[1], x2, route, _row(norm_final), ys)
    return xt.reshape(batch, seq, d)
```

```python
import functools

import jax
import jax.numpy as jnp
from jax import lax
from jax.experimental import pallas as pl
from jax.experimental.pallas import tpu as pltpu

F32 = jnp.float32
BF16 = jnp.bfloat16

EPS = 1e-6
ROPE_THETA = 10000.0

D_MODEL = 2048
MEM_LEN = 256
MLA_HEADS = 8
MLA_NOPE = 128
MLA_ROPE = 64
MLA_QK = MLA_NOPE + MLA_ROPE
MLA_V = 128
MLA_Q_RANK = 512
MLA_KV_RANK = 256
MLA_PAD = 256
SB_HEADS = 8
SB_DIM = 128
MLA_OUT = MLA_HEADS * MLA_V
SB_OUT = SB_HEADS * SB_DIM
MEM_HEADS = 4
MEM_DIM = 128
N_GROUPS = 4
EXPERTS_PER_GROUP = 8
N_EXPERTS = N_GROUPS * EXPERTS_PER_GROUP
D_EXPERT = 512

LANES = 128
LAT_COLS = 1024

SLOT_TILE = 256
WEIGHT_SLOTS = 2
VMEM_LIMIT = 56 * 1024 * 1024


def _rms(x, g):
    return x * lax.rsqrt(jnp.mean(x * x, axis=-1, keepdims=True) + EPS) * g


def _dot(a, b):
    return jnp.dot(a, b, preferred_element_type=F32)


def _dot_nt(a, b):
    return lax.dot_general(a, b, (((1,), (1,)), ((), ())), preferred_element_type=F32)


def _split_bf16(x):
    hi = x.astype(BF16)
    lo = (x - hi.astype(F32)).astype(BF16)
    return hi, lo


def _params(*sem):
    return pltpu.CompilerParams(dimension_semantics=sem, vmem_limit_bytes=VMEM_LIMIT)


def _proj_in_kernel(x_ref, g_ref, wl_ref, ws_ref, lat_ref, qk_ref, vt_ref, h_ref):
    j = pl.program_id(1)
    last = pl.num_programs(1) - 1

    @pl.when(j == 0)
    def _():
        h_ref[...] = _rms(x_ref[...], g_ref[...]).astype(BF16)
        lat_ref[...] = _dot_nt(h_ref[...], wl_ref[...])

    @pl.when((j > 0) & (j < last))
    def _():
        qk_ref[...] = _dot_nt(h_ref[...], ws_ref[...]).astype(BF16)

    @pl.when(j == last)
    def _():
        vt_ref[...] = _dot_nt(ws_ref[...], h_ref[...]).astype(BF16)


def _proj_in(x, g, w_lat, w_sb, tm=512):
    t = x.shape[0]
    tn = LAT_COLS
    n_sb = w_sb.shape[0] // tn
    return pl.pallas_call(
        _proj_in_kernel,
        grid=(t // tm, n_sb + 1),
        in_specs=[
            pl.BlockSpec((tm, D_MODEL), lambda i, j: (i, 0)),
            pl.BlockSpec((1, D_MODEL), lambda i, j: (0, 0)),
            pl.BlockSpec((tn, D_MODEL), lambda i, j: (0, 0)),
            pl.BlockSpec((tn, D_MODEL), lambda i, j: (jnp.maximum(j - 1, 0), 0)),
        ],
        out_specs=[
            pl.BlockSpec((tm, tn), lambda i, j: (i, 0)),
            pl.BlockSpec((tm, tn), lambda i, j: (i, jnp.clip(j - 1, 0, n_sb - 2))),
            pl.BlockSpec((SB_OUT, tm), lambda i, j: (0, i)),
        ],
        out_shape=[
            jax.ShapeDtypeStruct((t, LAT_COLS), F32),
            jax.ShapeDtypeStruct((t, 2 * SB_OUT), BF16),
            jax.ShapeDtypeStruct((SB_OUT, t), BF16),
        ],
        scratch_shapes=[pltpu.VMEM((tm, D_MODEL), BF16)],
        compiler_params=_params("parallel", "arbitrary"),
        name="proj_in",
    )(x, g, w_lat, w_sb)


def _mla_proj_kernel(lat_ref, pos_ref, invf_ref, gq_ref, gkv_ref, wqa_ref, wqb_ref, wk_ref, wvt_ref,
                     q_ref, k_ref, vt_ref):
    cq = _rms(lat_ref[:, :MLA_Q_RANK], gq_ref[...]).astype(BF16)
    ckv = _rms(lat_ref[:, MLA_Q_RANK:MLA_Q_RANK + MLA_KV_RANK], gkv_ref[...]).astype(BF16)
    ang = pos_ref[...].astype(F32) * invf_ref[...]
    live = lax.broadcasted_iota(jnp.int32, ang.shape, 1) < MLA_ROPE
    cos2 = jnp.where(live, jnp.cos(ang), 0.0)
    sin2 = jnp.where(live, jnp.sin(ang), 0.0)

    qa = _dot(cq, wqa_ref[...])
    qb = _dot(cq, wqb_ref[...])
    kn = _dot(ckv, wk_ref[...])
    vt_ref[...] = _dot_nt(wvt_ref[...], ckv).astype(BF16)
    k_pe = (lat_ref[:, 768:896] * cos2 + lat_ref[:, 896:1024] * sin2).astype(BF16)
    for h in range(MLA_HEADS):
        lo = h * MLA_PAD
        mid = lo + LANES
        q_ref[:, lo:mid] = qa[:, lo:mid].astype(BF16)
        q_ref[:, mid:mid + LANES] = (qa[:, mid:mid + LANES] * cos2
                                     + qb[:, h * LANES:(h + 1) * LANES] * sin2).astype(BF16)
        k_ref[:, lo:mid] = kn[:, h * LANES:(h + 1) * LANES].astype(BF16)
        k_ref[:, mid:mid + LANES] = k_pe


def _mla_proj(lat, pos, invf, gq, gkv, wqa, wqb, wk, wvt, tm=512):
    t = lat.shape[0]
    full = lambda a: pl.BlockSpec(a.shape, lambda i: (0, 0))
    return pl.pallas_call(
        _mla_proj_kernel,
        grid=(t // tm,),
        in_specs=[
            pl.BlockSpec((tm, LAT_COLS), lambda i: (i, 0)),
            pl.BlockSpec((tm, 1), lambda i: (i, 0)),
            full(invf), full(gq), full(gkv), full(wqa), full(wqb), full(wk), full(wvt),
        ],
        out_specs=[
            pl.BlockSpec((tm, MLA_HEADS * MLA_PAD), lambda i: (i, 0)),
            pl.BlockSpec((tm, MLA_HEADS * MLA_PAD), lambda i: (i, 0)),
            pl.BlockSpec((MLA_OUT, tm), lambda i: (0, i)),
        ],
        out_shape=[
            jax.ShapeDtypeStruct((t, MLA_HEADS * MLA_PAD), BF16),
            jax.ShapeDtypeStruct((t, MLA_HEADS * MLA_PAD), BF16),
            jax.ShapeDtypeStruct((MLA_OUT, t), BF16),
        ],
        compiler_params=_params("parallel"),
        name="mla_proj",
    )(lat, pos, invf, gq, gkv, wqa, wqb, wk, wvt)


LOG2E = 1.4426950408889634


def _three_stage(n_pairs, stage_a, stage_b, stage_c):
    a_issue, a_finish = stage_a
    b_issue, b_finish = stage_b
    c_issue, c_finish = stage_c

    def run_a(n, slot, first):
        a_finish(n, slot, a_issue(n, slot, first), first)

    run_a(0, 0, True)
    run_a(1, 1, True)
    b_finish(0, 0, b_issue(0, 0))

    def half(na, sa, nb, sb, nc, sc):
        ra = a_issue(na, sa, False)
        rb = b_issue(nb, sb)
        rc = c_issue(nc, sc)
        a_finish(na, sa, ra, False)
        b_finish(nb, sb, rb)
        c_finish(nc, sc, rc)

    def body(p, carry):
        half(2 * p, 0, 2 * p - 1, 1, 2 * p - 2, 0)
        half(2 * p + 1, 1, 2 * p, 0, 2 * p - 1, 1)
        return carry

    lax.fori_loop(1, n_pairs + 1, body, 0)
    last = 2 * n_pairs + 1
    rb = b_issue(last, 1)
    rc = c_issue(last - 1, 0)
    b_finish(last, 1, rb)
    c_finish(last - 1, 0, rc)
    c_finish(last, 1, c_issue(last, 1))


def _two_stage(n_pairs, produce, consume):
    produce(0, 0, True)
    produce(1, 1, True)
    consume(0, 0)

    def body(p, carry):
        produce(2 * p, 0, False)
        consume(2 * p - 1, 1)
        produce(2 * p + 1, 1, False)
        consume(2 * p, 0)
        return carry

    lax.fori_loop(1, n_pairs + 1, body, 0)
    consume(2 * n_pairs + 1, 1)


MLA_SUM_ROWS = 16
MLA_GROUP = 8


def _mla_attn_kernel(q_ref, k_ref, vt_ref, o_ref, s_ref, m_ref, acc_ref, *, tq, tk):
    i = pl.program_id(2)
    heads = range(MLA_GROUP)
    dq, dv = MLA_PAD, MLA_V
    m_ref[...] = jnp.full(m_ref.shape, -jnp.inf, F32)
    acc_ref[...] = jnp.zeros(acc_ref.shape, F32)
    ones = jnp.ones((MLA_SUM_ROWS, tk), BF16)

    def key_start(n):
        tile = jnp.where(n < 2, 2 * i + n, 2 * i + 1 - n)
        return pl.multiple_of(tile * tk, tk)

    def produce(n, slot, diagonal):
        ks = key_start(n)
        for g in heads:
            st = _dot_nt(k_ref[pl.ds(ks, tk), g * dq:(g + 1) * dq], q_ref[:, g * dq:(g + 1) * dq])
            if diagonal:
                kpos = ks + lax.broadcasted_iota(jnp.int32, st.shape, 0)
                qpos = i * tq + lax.broadcasted_iota(jnp.int32, st.shape, 1)
                st = jnp.where(kpos <= qpos, st, -jnp.inf)
            s_ref[g, slot] = st

    def consume(n, slot):
        ks = key_start(n)
        for g in heads:
            st = s_ref[g, slot]
            m_old = m_ref[g]
            m_new = jnp.maximum(m_old, jnp.max(st, axis=0, keepdims=True))
            alpha = jnp.exp2(m_old - m_new)
            p = jnp.exp2(st - m_new).astype(BF16)
            v_ones = jnp.concatenate([vt_ref[g * dv:(g + 1) * dv, pl.ds(ks, tk)], ones], axis=0)
            acc_ref[g] = alpha * acc_ref[g] + _dot(v_ones, p)
            m_ref[g] = m_new

    _two_stage(i, produce, consume)
    for g in heads:
        o_ref[:, g * dv:(g + 1) * dv] = (acc_ref[g, :dv, :] / acc_ref[g, dv:dv + 1, :]).T


def _mla_attn(q, k, vt, batch, seq, tq=512):
    tk = tq // 2
    nq = seq // tq
    groups = MLA_HEADS // MLA_GROUP
    kern = functools.partial(_mla_attn_kernel, tq=tq, tk=tk)
    return pl.pallas_call(
        kern,
        grid=(batch, groups, nq),
        in_specs=[
            pl.BlockSpec((tq, MLA_GROUP * MLA_PAD), lambda b, h, i: (b * nq + i, h)),
            pl.BlockSpec((seq, MLA_GROUP * MLA_PAD), lambda b, h, i: (b, h), pipeline_mode=pl.Buffered(1)),
            pl.BlockSpec((MLA_GROUP * MLA_V, seq), lambda b, h, i: (h, b), pipeline_mode=pl.Buffered(1)),
        ],
        out_specs=pl.BlockSpec((tq, MLA_GROUP * MLA_V), lambda b, h, i: (b * nq + i, h)),
        out_shape=jax.ShapeDtypeStruct((batch * seq, MLA_OUT), F32),
        scratch_shapes=[
            pltpu.VMEM((MLA_GROUP, 2, tk, tq), F32),
            pltpu.VMEM((MLA_GROUP, 1, tq), F32),
            pltpu.VMEM((MLA_GROUP, MLA_V + MLA_SUM_ROWS, tq), F32),
        ],
        compiler_params=_params("parallel", "parallel", "arbitrary"),
        name="mla_attn",
    )(q, k, vt)


SB_EXP_CLAMP = 126.0
SB_GROUP = 4


def _sb_attn_kernel(q_ref, k_ref, vt_ref, o_ref, z_ref, hl_ref, arg_ref, acc_ref, c_ref, *, tq, tk):
    i = pl.program_id(2)
    heads = range(SB_GROUP)
    d = SB_DIM
    acc_ref[...] = jnp.zeros(acc_ref.shape, F32)
    c_ref[...] = jnp.zeros(c_ref.shape, F32)
    col = lax.broadcasted_iota(jnp.int32, (tk, tk), 1)
    row = lax.broadcasted_iota(jnp.int32, (tk, tk), 0)
    neg_tri = jnp.where(col >= row, -1.0, 0.0).astype(BF16)

    def key_start(n):
        return pl.multiple_of((2 * i + 1 - n) * tk, tk)

    def a_issue(n, slot, masked):
        ks = key_start(n)
        return [_dot_nt(k_ref[pl.ds(ks, tk), g * d:(g + 1) * d], q_ref[:, g * d:(g + 1) * d])
                for g in heads]

    def a_finish(n, slot, z2s, masked):
        for g in heads:
            z2 = z2s[g]
            sp = jnp.maximum(z2, jnp.log2(1.0 + jnp.exp2(jnp.minimum(z2, SB_EXP_CLAMP))))
            if masked:
                kpos = key_start(n) + lax.broadcasted_iota(jnp.int32, z2.shape, 0)
                qpos = i * tq + lax.broadcasted_iota(jnp.int32, z2.shape, 1)
                mask = kpos < qpos
                sp = jnp.where(mask, sp, 0.0)
                z2 = jnp.where(mask, z2, -jnp.inf)
            hl_ref[g, slot] = sp.astype(BF16)
            z_ref[g, slot] = z2

    def b_issue(n, slot):
        return [_dot(neg_tri, hl_ref[g, slot]) for g in heads]

    def b_finish(n, slot, laters):
        for g in heads:
            arg_ref[g, slot] = z_ref[g, slot] + laters[g] + c_ref[g]
            c_ref[g] += laters[g][0:1, :]

    def c_issue(n, slot):
        ks = key_start(n)
        return [_dot(vt_ref[g * d:(g + 1) * d, pl.ds(ks, tk)], jnp.exp2(arg_ref[g, slot]).astype(BF16))
                for g in heads]

    def c_finish(n, slot, pvs):
        for g in heads:
            acc_ref[g] += pvs[g]

    _three_stage(i, (a_issue, a_finish), (b_issue, b_finish), (c_issue, c_finish))
    for g in heads:
        o_ref[:, g * d:(g + 1) * d] = acc_ref[g].T


def _sb_attn(qk, vt, batch, seq, tq=512):
    tk = tq // 2
    nq = seq // tq
    groups = SB_HEADS // SB_GROUP
    gd = SB_GROUP * SB_DIM
    kern = functools.partial(_sb_attn_kernel, tq=tq, tk=tk)
    return pl.pallas_call(
        kern,
        grid=(batch, groups, nq),
        in_specs=[
            pl.BlockSpec((tq, gd), lambda b, h, i: (b * nq + i, h)),
            pl.BlockSpec((seq, gd), lambda b, h, i: (b, groups + h)),
            pl.BlockSpec((gd, seq), lambda b, h, i: (h, b)),
        ],
        out_specs=pl.BlockSpec((tq, gd), lambda b, h, i: (b * nq + i, h)),
        out_shape=jax.ShapeDtypeStruct((batch * seq, SB_OUT), F32),
        scratch_shapes=[
            pltpu.VMEM((SB_GROUP, 2, tk, tq), F32),
            pltpu.VMEM((SB_GROUP, 2, tk, tq), BF16),
            pltpu.VMEM((SB_GROUP, 2, tk, tq), F32),
            pltpu.VMEM((SB_GROUP, SB_DIM, tq), F32),
            pltpu.VMEM((SB_GROUP, 1, tq), F32),
        ],
        compiler_params=_params("parallel", "parallel", "arbitrary"),
        name="sb_attn",
    )(qk, qk, vt)


def _resident(a):
    return pl.BlockSpec(a.shape, lambda i: (0,) * a.ndim, pipeline_mode=pl.Buffered(1))


def _cast_on_first_step(pairs):
    @pl.when(pl.program_id(0) == 0)
    def _():
        for src_ref, dst_ref in pairs:
            dst_ref[...] = src_ref[...].astype(BF16)


def _out_proj_kernel(x_ref, oa_ref, ob_ref, ga_ref, gb_ref, w_ref, y_ref, wb_ref):
    _cast_on_first_step([(w_ref, wb_ref)])
    na = _rms(oa_ref[...], ga_ref[...]).astype(BF16)
    nb = _rms(ob_ref[...], gb_ref[...]).astype(BF16)
    y_ref[...] = x_ref[...] + _dot(na, wb_ref[:MLA_OUT, :]) + _dot(nb, wb_ref[MLA_OUT:, :])


def _out_proj(x, oa, ob, ga, gb, w, tm=512):
    t = x.shape[0]
    full = lambda a: pl.BlockSpec(a.shape, lambda i: (0, 0))
    return pl.pallas_call(
        _out_proj_kernel,
        grid=(t // tm,),
        in_specs=[
            pl.BlockSpec((tm, D_MODEL), lambda i: (i, 0)),
            pl.BlockSpec((tm, MLA_OUT), lambda i: (i, 0)),
            pl.BlockSpec((tm, SB_OUT), lambda i: (i, 0)),
            full(ga), full(gb), _resident(w),
        ],
        out_specs=pl.BlockSpec((tm, D_MODEL), lambda i: (i, 0)),
        out_shape=jax.ShapeDtypeStruct((t, D_MODEL), F32),
        scratch_shapes=[pltpu.VMEM(w.shape, BF16)],
        compiler_params=_params("arbitrary"),
        name="out_proj",
    )(x, oa, ob, ga, gb, w)


def _mem_kv_kernel(mem_ref, g_ref, w_ref, kv_ref):
    kv_ref[...] = _dot(_rms(mem_ref[...], g_ref[...]).astype(BF16), w_ref[...].astype(BF16)).astype(BF16)


def _mem_kv(mem, g, w):
    m = mem.shape[0]
    n = w.shape[1]
    full = lambda a: pl.BlockSpec(a.shape, lambda i: (0, 0))
    return pl.pallas_call(
        _mem_kv_kernel,
        grid=(1,),
        in_specs=[full(mem), full(g), full(w)],
        out_specs=pl.BlockSpec((m, n), lambda i: (0, 0)),
        out_shape=jax.ShapeDtypeStruct((m, n), BF16),
        compiler_params=_params("arbitrary"),
        name="mem_kv",
    )(mem, g, w)


GROUP_LANE0 = N_EXPERTS


def _mem_route_kernel(x_ref, gx_ref, wq_ref, kv_ref, wo_ref, gf_ref, wrhl_ref, br_ref,
                      x2_ref, h2_ref, route_ref, wqb_ref, wob_ref):
    _cast_on_first_step([(wq_ref, wqb_ref), (wo_ref, wob_ref)])
    x1 = x_ref[...]
    q = _dot(_rms(x1, gx_ref[...]).astype(BF16), wqb_ref[...]).astype(BF16)
    scale = MEM_DIM ** -0.5
    kw = MEM_HEADS * MEM_DIM
    heads = []
    for h in range(MEM_HEADS):
        lo = h * MEM_DIM
        s = _dot_nt(q[:, lo:lo + MEM_DIM], kv_ref[:, lo:lo + MEM_DIM]) * scale
        e = jnp.exp(s - jnp.max(s, axis=-1, keepdims=True))
        p = (e / jnp.sum(e, axis=-1, keepdims=True)).astype(BF16)
        heads.append(_dot(p, kv_ref[:, kw + lo:kw + lo + MEM_DIM]).astype(BF16))
    o = jnp.concatenate(heads, axis=-1)
    x2 = x1 + _dot(o, wob_ref[...])
    x2_ref[...] = x2
    h2 = _rms(x2, gf_ref[...])
    h2_ref[...] = h2

    hh, hl = _split_bf16(h2)
    both = _dot(hh, wrhl_ref[...])
    lg = both[:, :LANES] + both[:, LANES:] + _dot(hl, wrhl_ref[:, :LANES]) + br_ref[...]
    lane = lax.broadcasted_iota(jnp.int32, lg.shape, 1)
    big = jnp.int32(1 << 20)
    ninf = -jnp.inf

    def lane_max(v):
        return jnp.max(v, axis=-1, keepdims=True)

    def first_lane(cond):
        return jnp.min(jnp.where(cond, lane, big), axis=-1, keepdims=True)

    is_g = (lane >= GROUP_LANE0) & (lane < GROUP_LANE0 + N_GROUPS)
    g_max = lane_max(jnp.where(is_g, lg, ninf))
    g_sum = jnp.sum(jnp.where(is_g, jnp.exp(lg - g_max), 0.0), axis=-1, keepdims=True)
    p_g = 1.0 / g_sum
    g_idx = first_lane(is_g & (lg == g_max)) - GROUP_LANE0
    in_grp = (lane < N_EXPERTS) & ((lane // EXPERTS_PER_GROUP) == g_idx)
    e_max = lane_max(jnp.where(in_grp, lg, ninf))
    e_sum = jnp.sum(jnp.where(in_grp, jnp.exp(lg - e_max), 0.0), axis=-1, keepdims=True)
    i1 = first_lane(in_grp & (lg == e_max))
    rest = in_grp & (lane != i1)
    e_max2 = lane_max(jnp.where(rest, lg, ninf))
    i2 = first_lane(rest & (lg == e_max2))
    p1 = 1.0 / e_sum
    p2 = jnp.exp(e_max2 - e_max) / e_sum
    den = p1 + p2
    gate1 = p_g * (p1 / den)
    gate2 = p_g * (p2 / den)
    route = jnp.where(lane == 0, i1.astype(F32),
                      jnp.where(lane == 1, i2.astype(F32),
                                jnp.where(lane == 2, gate1,
                                          jnp.where(lane == 3, gate2, 0.0))))
    route_ref[...] = route


def _mem_route(x1, gx, wq, kv, wo, gf, wrhl, br, seq, tm=512):
    t = x1.shape[0]
    per_batch = seq // tm
    full = lambda a: pl.BlockSpec(a.shape, lambda i: (0, 0))
    return pl.pallas_call(
        _mem_route_kernel,
        grid=(t // tm,),
        in_specs=[
            pl.BlockSpec((tm, D_MODEL), lambda i: (i, 0)),
            full(gx), _resident(wq),
            pl.BlockSpec((MEM_LEN, kv.shape[1]), lambda i: (i // per_batch, 0)),
            _resident(wo), full(gf), full(wrhl), full(br),
        ],
        out_specs=[
            pl.BlockSpec((tm, D_MODEL), lambda i: (i, 0)),
            pl.BlockSpec((tm, D_MODEL), lambda i: (i, 0)),
            pl.BlockSpec((tm, LANES), lambda i: (i, 0)),
        ],
        out_shape=[
            jax.ShapeDtypeStruct((t, D_MODEL), F32),
            jax.ShapeDtypeStruct((t, D_MODEL), F32),
            jax.ShapeDtypeStruct((t, LANES), F32),
        ],
        scratch_shapes=[pltpu.VMEM(wq.shape, BF16), pltpu.VMEM(wo.shape, BF16)],
        compiler_params=_params("arbitrary"),
        name="mem_route",
    )(x1, gx, wq, kv, wo, gf, wrhl, br)


SLOT_BLK = 256


def _slots_kernel(route_ref, slot_ref, meta_ref, cum_ref, *, n_tok):
    nblk = n_tok // SLOT_BLK
    e_iota = lax.broadcasted_iota(jnp.int32, (LANES, SLOT_BLK), 0)
    incl = (lax.broadcasted_iota(jnp.int32, (SLOT_BLK, SLOT_BLK), 0)
            <= lax.broadcasted_iota(jnp.int32, (SLOT_BLK, SLOT_BLK), 1)).astype(BF16)

    def onehots(b):
        ts = pl.multiple_of(b * SLOT_BLK, SLOT_BLK)
        ids = route_ref[pl.ds(ts, SLOT_BLK), :].T
        oh1 = e_iota == ids[0:1, :].astype(jnp.int32)
        oh2 = e_iota == ids[1:2, :].astype(jnp.int32)
        return ts, oh1, oh2

    def count(b, carry):
        ts, oh1, oh2 = onehots(b)
        cnt = (oh1 | oh2).astype(F32).astype(BF16)
        c = _dot(cnt, incl) + carry
        cum_ref[:, pl.ds(ts, SLOT_BLK)] = c
        return c[:, SLOT_BLK - 1:SLOT_BLK]

    total = lax.fori_loop(0, nblk, count, jnp.zeros((LANES, 1), F32))
    tiles = jnp.floor((total + (SLOT_TILE - 1)) * (1.0 / SLOT_TILE))
    below = (lax.broadcasted_iota(jnp.int32, (LANES, LANES), 1)
             < lax.broadcasted_iota(jnp.int32, (LANES, LANES), 0)).astype(BF16)
    tile_lo = _dot(below, jnp.broadcast_to(tiles, (LANES, LANES)).astype(BF16))
    base = tile_lo[:, 0:1] * SLOT_TILE

    def assign(b, carry):
        ts, oh1, oh2 = onehots(b)
        pos = base + cum_ref[:, pl.ds(ts, SLOT_BLK)] - 1.0
        s1 = jnp.sum(jnp.where(oh1, pos, 0.0), axis=0, keepdims=True)
        s2 = jnp.sum(jnp.where(oh2, pos, 0.0), axis=0, keepdims=True)
        row = lax.broadcasted_iota(jnp.int32, (8, SLOT_BLK), 0)
        slot_ref[:, pl.ds(ts, SLOT_BLK)] = jnp.where(row == 0, s1, jnp.where(row == 1, s2, 0.0)).astype(jnp.int32)
        return carry

    lax.fori_loop(0, nblk, assign, 0)

    tile_hi = tile_lo + tiles
    tix = lax.broadcasted_iota(jnp.int32, (LANES, LANES), 1).astype(F32)
    is_e = lax.broadcasted_iota(jnp.int32, (LANES, LANES), 0) < N_EXPERTS
    owner = jnp.sum(jnp.where(is_e & (tile_hi <= tix), 1.0, 0.0), axis=0, keepdims=True)
    n_tiles = jnp.max(jnp.where(is_e, tile_hi, 0.0), axis=0, keepdims=True)
    valid = tix[0:1, :] < n_tiles
    last_owner = jnp.max(jnp.where(valid, owner, 0.0), axis=1, keepdims=True)
    owner = jnp.where(valid, owner, last_owner)
    eye = (lax.broadcasted_iota(jnp.int32, (LANES, LANES), 0)
           == lax.broadcasted_iota(jnp.int32, (LANES, LANES), 1))
    to_lanes = lambda colvec: jnp.sum(jnp.where(eye, colvec, 0.0), axis=0, keepdims=True)
    pad_first = to_lanes(base + total)
    pad_end = to_lanes(base + tiles * SLOT_TILE)
    row = lax.broadcasted_iota(jnp.int32, (8, LANES), 0)
    meta = jnp.where(row == 0, owner,
                     jnp.where(row == 1, valid.astype(F32),
                               jnp.where(row == 2, pad_first,
                                         jnp.where(row == 3, pad_end,
                                                   jnp.where(row == 4, n_tiles, 0.0)))))
    meta_ref[...] = meta.astype(jnp.int32)


def _slots(route):
    t = route.shape[0]
    kern = functools.partial(_slots_kernel, n_tok=t)
    return pl.pallas_call(
        kern,
        grid=(1,),
        in_specs=[pl.BlockSpec(route.shape, lambda i: (0, 0))],
        out_specs=[
            pl.BlockSpec((8, t), lambda i: (0, 0)),
            pl.BlockSpec((8, LANES), lambda i: (0, 0)),
        ],
        out_shape=[
            jax.ShapeDtypeStruct((8, t), jnp.int32),
            jax.ShapeDtypeStruct((8, LANES), jnp.int32),
        ],
        scratch_shapes=[pltpu.VMEM((LANES, t), F32)],
        compiler_params=_params("arbitrary"),
        name="slots",
    )(route)


def _row_copy(src_ref, src_row, dst_ref, dst_row, sem):
    return pltpu.make_async_copy(src_ref.at[pl.ds(src_row, 1), :], dst_ref.at[pl.ds(dst_row, 1), :], sem)


def _dispatch_kernel(s1_ref, s2_ref, pad_first_ref, pad_end_ref, n_tiles_ref, h_ref, xs_ref, zero_ref,
                     sem, pad_sem, *, tm):
    i = pl.program_id(0)
    t0 = i * tm
    n_slot_tiles = xs_ref.shape[0] // SLOT_TILE

    def copies(r):
        return (_row_copy(h_ref, r, xs_ref, s1_ref[t0 + r], sem),
                _row_copy(h_ref, r, xs_ref, s2_ref[t0 + r], sem))

    def issue(r, c):
        first, second = copies(r)
        first.start(priority=0)
        second.start(priority=1)
        return c

    lax.fori_loop(0, tm, issue, 0, unroll=8)

    @pl.when(i == pl.num_programs(0) - 1)
    def _():
        zero_ref[...] = jnp.zeros(zero_ref.shape, F32)

        def pad_copy(slot):
            return _row_copy(zero_ref, 0, xs_ref, slot, pad_sem)

        def fill(e, c):
            def one(s, cc):
                pad_copy(s).start()
                return cc

            return lax.fori_loop(pad_first_ref[e], pad_end_ref[e], one, c)

        lax.fori_loop(0, N_EXPERTS, fill, 0)

        def tile_copy(tile):
            return pltpu.make_async_copy(zero_ref, xs_ref.at[pl.ds(tile * SLOT_TILE, SLOT_TILE), :], pad_sem)

        def fill_tile(tile, c):
            tile_copy(tile).start()
            return c

        lax.fori_loop(n_tiles_ref[0], n_slot_tiles, fill_tile, 0)

        def settle(e, c):
            def one(s, cc):
                pad_copy(s).wait()
                return cc

            return lax.fori_loop(pad_first_ref[e], pad_end_ref[e], one, c)

        lax.fori_loop(0, N_EXPERTS, settle, 0)

        def settle_tile(tile, c):
            tile_copy(tile).wait()
            return c

        lax.fori_loop(n_tiles_ref[0], n_slot_tiles, settle_tile, 0)

    def drain(r, c):
        for cp in copies(r):
            cp.wait()
        return c

    lax.fori_loop(0, tm, drain, 0, unroll=8)


def _dispatch(slot1, slot2, pad_first, pad_end, n_tiles, h2, n_slots, tm=256):
    t = h2.shape[0]
    kern = functools.partial(_dispatch_kernel, tm=tm)
    return pl.pallas_call(
        kern,
        grid_spec=pltpu.PrefetchScalarGridSpec(
            num_scalar_prefetch=5,
            grid=(t // tm,),
            in_specs=[pl.BlockSpec((tm, D_MODEL), lambda i, *_: (i, 0))],
            out_specs=pl.BlockSpec(memory_space=pl.ANY),
            scratch_shapes=[
                pltpu.VMEM((SLOT_TILE, D_MODEL), F32),
                pltpu.SemaphoreType.DMA(()),
                pltpu.SemaphoreType.DMA(()),
            ],
        ),
        out_shape=jax.ShapeDtypeStruct((n_slots, D_MODEL), F32),
        compiler_params=_params("arbitrary"),
        name="dispatch",
    )(slot1, slot2, pad_first, pad_end, n_tiles, h2)


def _experts_kernel(own_ref, valid_ref, xs_ref, wg_hbm, wu_hbm, wd_hbm, ys_ref,
                    wgf, wuf, wdf, wgb, wub, wdb, wsem, wslot_ref):
    i = pl.program_id(0)
    nt = pl.num_programs(0)
    at = lambda ref, j: ref[jnp.minimum(j, nt - 1)]

    def weight_copies(e, s):
        return (pltpu.make_async_copy(wg_hbm.at[e], wgf.at[s], wsem.at[s]),
                pltpu.make_async_copy(wu_hbm.at[e], wuf.at[s], wsem.at[s]),
                pltpu.make_async_copy(wd_hbm.at[e], wdf.at[s], wsem.at[s]))

    def next_expert_tile(j0):
        e0 = at(own_ref, j0)
        return lax.while_loop(lambda j: (j < nt) & (at(own_ref, j) == e0), lambda j: j + 1, j0 + 1)

    def start_weights(j, s):
        @pl.when((j < nt) & (at(valid_ref, j) > 0))
        def _():
            for cp in weight_copies(at(own_ref, j), s):
                cp.start()

    @pl.when(i == 0)
    def _():
        wslot_ref[0] = 0
        j = 0
        for s in range(WEIGHT_SLOTS - 1):
            start_weights(j, s)
            j = next_expert_tile(j)

    @pl.when(valid_ref[i] > 0)
    def _():
        e = own_ref[i]

        @pl.when((i == 0) | (own_ref[jnp.maximum(i - 1, 0)] != e))
        def _():
            s = wslot_ref[0]
            for cp in weight_copies(e, s):
                cp.wait()
            j = i
            for _ in range(WEIGHT_SLOTS - 1):
                j = next_expert_tile(j)
            start_weights(j, (s + WEIGHT_SLOTS - 1) % WEIGHT_SLOTS)

            wgb[...] = wgf[s].astype(BF16)
            wub[...] = wuf[s].astype(BF16)
            wdb[...] = wdf[s].astype(BF16)
            wslot_ref[0] = (s + 1) % WEIGHT_SLOTS

        x = xs_ref[...].astype(BF16)
        a = _dot(x, wgb[...])
        u = _dot(x, wub[...])
        act = (a * jax.nn.sigmoid(a) * u).astype(BF16)
        ys_ref[...] = _dot(act, wdb[...])

    @pl.when(valid_ref[i] == 0)
    def _():
        ys_ref[...] = jnp.zeros(ys_ref.shape, F32)


def _experts(own, valid, xs, wg, wu, wd):
    n_tiles = xs.shape[0] // SLOT_TILE
    hbm = pl.BlockSpec(memory_space=pl.ANY)
    tile = (SLOT_TILE, D_MODEL)
    return pl.pallas_call(
        _experts_kernel,
        grid_spec=pltpu.PrefetchScalarGridSpec(
            num_scalar_prefetch=2,
            grid=(n_tiles,),
            in_specs=[
                pl.BlockSpec(tile, lambda i, o, v: (jnp.where(v[i] > 0, i, 0), 0)),
                hbm, hbm, hbm,
            ],
            out_specs=pl.BlockSpec(tile, lambda i, o, v: (i, 0)),
            scratch_shapes=[
                pltpu.VMEM((WEIGHT_SLOTS, D_MODEL, D_EXPERT), F32),
                pltpu.VMEM((WEIGHT_SLOTS, D_MODEL, D_EXPERT), F32),
                pltpu.VMEM((WEIGHT_SLOTS, D_EXPERT, D_MODEL), F32),
                pltpu.VMEM((D_MODEL, D_EXPERT), BF16),
                pltpu.VMEM((D_MODEL, D_EXPERT), BF16),
                pltpu.VMEM((D_EXPERT, D_MODEL), BF16),
                pltpu.SemaphoreType.DMA((WEIGHT_SLOTS,)),
                pltpu.SMEM((1,), jnp.int32),
            ],
        ),
        out_shape=jax.ShapeDtypeStruct(xs.shape, F32),
        compiler_params=_params("arbitrary"),
        name="experts",
    )(own, valid, xs, wg, wu, wd)


def _combine_kernel(s1_ref, s2_ref, x_ref, route_ref, g_ref, ys_ref, y_ref, buf_ref, sem, *, tm):
    i = pl.program_id(0)
    n = pl.num_programs(0)

    def copies(tile, p, r):
        t = tile * tm + r
        return (_row_copy(ys_ref, s1_ref[t], buf_ref.at[p, 0], r, sem.at[p]),
                _row_copy(ys_ref, s2_ref[t], buf_ref.at[p, 1], r, sem.at[p]))

    def gather_start(tile, p):
        def issue(r, c):
            first, second = copies(tile, p, r)
            first.start(priority=0)
            second.start(priority=1)
            return c

        lax.fori_loop(0, tm, issue, 0, unroll=8)

    def gather_wait(tile, p):
        def drain(r, c):
            for cp in copies(tile, p, r):
                cp.wait()
            return c

        lax.fori_loop(0, tm, drain, 0, unroll=8)

    @pl.when(i == 0)
    def _():
        gather_start(0, 0)

    p = i % 2

    @pl.when(i + 1 < n)
    def _():
        gather_start(i + 1, 1 - p)

    gather_wait(i, p)
    y = x_ref[...] + route_ref[:, 2:3] * buf_ref[p, 0] + route_ref[:, 3:4] * buf_ref[p, 1]
    y_ref[...] = _rms(y, g_ref[...])


def _combine(slot1, slot2, x2, route, g, ys, tm=256):
    t = x2.shape[0]
    kern = functools.partial(_combine_kernel, tm=tm)
    return pl.pallas_call(
        kern,
        grid_spec=pltpu.PrefetchScalarGridSpec(
            num_scalar_prefetch=2,
            grid=(t // tm,),
            in_specs=[
                pl.BlockSpec((tm, D_MODEL), lambda i, s1, s2: (i, 0)),
                pl.BlockSpec((tm, LANES), lambda i, s1, s2: (i, 0)),
                pl.BlockSpec((1, D_MODEL), lambda i, s1, s2: (0, 0)),
                pl.BlockSpec(memory_space=pl.ANY),
            ],
            out_specs=pl.BlockSpec((tm, D_MODEL), lambda i, s1, s2: (i, 0)),
            scratch_shapes=[
                pltpu.VMEM((2, 2, tm, D_MODEL), F32),
                pltpu.SemaphoreType.DMA((2,)),
            ],
        ),
        out_shape=jax.ShapeDtypeStruct((t, D_MODEL), F32),
        compiler_params=_params("arbitrary"),
        name="combine",
    )(slot1, slot2, x2, route, g, ys)


def _swap_halves(w):
    half = w.shape[-1] // 2
    return jnp.concatenate([-w[..., half:], w[..., :half]], axis=-1)


def _row(v):
    return v.reshape(1, -1).astype(F32)


def kernel(x, mem, positions, norm_mix, w_in, norm_q_lat, w_q_b, norm_kv_lat, w_kv_b, norm_mla_out, norm_sb_out, w_out, norm_mem_x, norm_mem_src, w_mem_q, w_mem_kv, w_mem_o, norm_ffn, w_group, b_group, w_expert_router, b_expert, w_gate, w_up, w_down, norm_final):
    batch, seq, d = x.shape
    t = batch * seq
    depth = w_in.shape[0]
    assert depth == 1, "single-layer trunk only"
    xt = x.reshape(t, d)
    pos = positions.reshape(t, 1)
    inv_freq = ROPE_THETA ** (-jnp.arange(0, MLA_ROPE, 2, dtype=F32) / MLA_ROPE)
    invf = jnp.concatenate([inv_freq, inv_freq, jnp.zeros((LANES - MLA_ROPE,), F32)]).reshape(1, LANES)
    n_slots = (2 * t // SLOT_TILE + N_EXPERTS) * SLOT_TILE

    for l in range(depth):
        wi_t = w_in[l].T
        lat_w = MLA_Q_RANK + MLA_KV_RANK
        w_kpe_t = wi_t[lat_w:lat_w + MLA_ROPE]
        zpad = jnp.zeros((LANES - MLA_ROPE, d), F32)
        sb0 = lat_w + MLA_ROPE
        half = MLA_ROPE // 2
        w_kpe_swapped_t = jnp.concatenate([-w_kpe_t[half:], w_kpe_t[:half]], axis=0)
        w_lat_t = jnp.concatenate([wi_t[:sb0], zpad, w_kpe_swapped_t, zpad], axis=0).astype(BF16)
        sb_scale = jnp.concatenate([jnp.full((SB_OUT,), SB_DIM ** -0.5 * LOG2E, F32), jnp.ones((2 * SB_OUT,), F32)])
        w_sb_t = (wi_t[sb0:] * sb_scale[:, None]).astype(BF16)
        wq = w_q_b[l].reshape(MLA_Q_RANK, MLA_HEADS, MLA_QK) * (MLA_QK ** -0.5 * LOG2E)
        wq_pe = wq[:, :, MLA_NOPE:]
        zq = jnp.zeros((MLA_Q_RANK, MLA_HEADS, MLA_PAD - MLA_QK), F32)
        wqa = jnp.concatenate([wq, zq], axis=-1).reshape(MLA_Q_RANK, MLA_HEADS * MLA_PAD).astype(BF16)
        wqb = jnp.concatenate([_swap_halves(wq_pe), zq], axis=-1).reshape(MLA_Q_RANK, MLA_HEADS * LANES).astype(BF16)
        wkv = w_kv_b[l].reshape(MLA_KV_RANK, MLA_HEADS, MLA_NOPE + MLA_V)
        wk = wkv[:, :, :MLA_NOPE].reshape(MLA_KV_RANK, MLA_OUT).astype(BF16)
        wvt = wkv[:, :, MLA_NOPE:].reshape(MLA_KV_RANK, MLA_OUT).T.astype(BF16)
        w_router = jnp.concatenate(
            [w_expert_router[l], w_group[l], jnp.zeros((d, LANES - N_EXPERTS - N_GROUPS), F32)], axis=1)
        wr_hi = w_router.astype(BF16)
        wr_hl = jnp.concatenate([wr_hi, (w_router - wr_hi.astype(F32)).astype(BF16)], axis=1)
        b_router = jnp.concatenate(
            [b_expert[l].astype(F32), b_group[l].astype(F32), jnp.zeros((LANES - N_EXPERTS - N_GROUPS,), F32)]
        ).reshape(1, LANES)

        lat, sb_qk, sb_vt = _proj_in(xt, _row(norm_mix[l]), w_lat_t, w_sb_t)
        q, k, vt = _mla_proj(lat, pos, invf, _row(norm_q_lat[l]), _row(norm_kv_lat[l]), wqa, wqb, wk, wvt)
        o_mla = _mla_attn(q, k, vt, batch, seq)
        o_sb = _sb_attn(sb_qk, sb_vt, batch, seq)
        x1 = _out_proj(xt, o_mla, o_sb, _row(norm_mla_out[l]), _row(norm_sb_out[l]), w_out[l])

        kv = _mem_kv(mem.reshape(batch * MEM_LEN, d), _row(norm_mem_src[l]), w_mem_kv[l])
        x2, h2, route = _mem_route(x1, _row(norm_mem_x[l]), w_mem_q[l], kv,
                                   w_mem_o[l], _row(norm_ffn[l]), wr_hl, b_router, seq)

        slots, meta = _slots(route)
        xs = _dispatch(slots[0], slots[1], meta[2], meta[3], meta[4], h2, n_slots)
        ys = _experts(meta[0], meta[1], xs, w_gate[l], w_up[l], w_down[l])
        xt = _combine(slots[0], slots[1], x2, route, _row(norm_final), ys)
    return xt.reshape(batch, seq, d)
```

```python
import functools

import jax
import jax.numpy as jnp
from jax import lax
from jax.experimental import pallas as pl
from jax.experimental.pallas import tpu as pltpu

F32 = jnp.float32
BF16 = jnp.bfloat16

EPS = 1e-6
ROPE_THETA = 10000.0

D_MODEL = 2048
MEM_LEN = 256
MLA_HEADS = 8
MLA_NOPE = 128
MLA_ROPE = 64
MLA_QK = MLA_NOPE + MLA_ROPE
MLA_V = 128
MLA_Q_RANK = 512
MLA_KV_RANK = 256
MLA_PAD = 256
SB_HEADS = 8
SB_DIM = 128
MLA_OUT = MLA_HEADS * MLA_V
SB_OUT = SB_HEADS * SB_DIM
MEM_HEADS = 4
MEM_DIM = 128
N_GROUPS = 4
EXPERTS_PER_GROUP = 8
N_EXPERTS = N_GROUPS * EXPERTS_PER_GROUP
D_EXPERT = 512

LANES = 128
LAT_COLS = 1024

SLOT_TILE = 256
WEIGHT_SLOTS = 2
VMEM_LIMIT = 56 * 1024 * 1024


def _rms(x, g):
    return x * lax.rsqrt(jnp.mean(x * x, axis=-1, keepdims=True) + EPS) * g


def _dot(a, b):
    return jnp.dot(a, b, preferred_element_type=F32)


def _dot_nt(a, b):
    return lax.dot_general(a, b, (((1,), (1,)), ((), ())), preferred_element_type=F32)


def _split_bf16(x):
    hi = x.astype(BF16)
    lo = (x - hi.astype(F32)).astype(BF16)
    return hi, lo


def _params(*sem):
    return pltpu.CompilerParams(dimension_semantics=sem, vmem_limit_bytes=VMEM_LIMIT)


def _proj_in_kernel(x_ref, g_ref, wl_ref, ws_ref, lat_ref, qk_ref, vt_ref, h_ref):
    j = pl.program_id(1)
    last = pl.num_programs(1) - 1

    @pl.when(j == 0)
    def _():
        h_ref[...] = _rms(x_ref[...], g_ref[...]).astype(BF16)
        lat_ref[...] = _dot_nt(h_ref[...], wl_ref[...])

    @pl.when((j > 0) & (j < last))
    def _():
        qk_ref[...] = _dot_nt(h_ref[...], ws_ref[...]).astype(BF16)

    @pl.when(j == last)
    def _():
        vt_ref[...] = _dot_nt(ws_ref[...], h_ref[...]).astype(BF16)


def _proj_in(x, g, w_lat, w_sb, tm=512):
    t = x.shape[0]
    tn = LAT_COLS
    n_sb = w_sb.shape[0] // tn
    return pl.pallas_call(
        _proj_in_kernel,
        grid=(t // tm, n_sb + 1),
        in_specs=[
            pl.BlockSpec((tm, D_MODEL), lambda i, j: (i, 0)),
            pl.BlockSpec((1, D_MODEL), lambda i, j: (0, 0)),
            pl.BlockSpec((tn, D_MODEL), lambda i, j: (0, 0)),
            pl.BlockSpec((tn, D_MODEL), lambda i, j: (jnp.maximum(j - 1, 0), 0)),
        ],
        out_specs=[
            pl.BlockSpec((tm, tn), lambda i, j: (i, 0)),
            pl.BlockSpec((tm, tn), lambda i, j: (i, jnp.clip(j - 1, 0, n_sb - 2))),
            pl.BlockSpec((SB_OUT, tm), lambda i, j: (0, i)),
        ],
        out_shape=[
            jax.ShapeDtypeStruct((t, LAT_COLS), F32),
            jax.ShapeDtypeStruct((t, 2 * SB_OUT), BF16),
            jax.ShapeDtypeStruct((SB_OUT, t), BF16),
        ],
        scratch_shapes=[pltpu.VMEM((tm, D_MODEL), BF16)],
        compiler_params=_params("parallel", "arbitrary"),
        name="proj_in",
    )(x, g, w_lat, w_sb)


def _mla_proj_kernel(lat_ref, pos_ref, invf_ref, gq_ref, gkv_ref, wqa_ref, wk_ref, wvt_ref,
                     q_ref, k_ref, vt_ref):
    cq = _rms(lat_ref[:, :MLA_Q_RANK], gq_ref[...]).astype(BF16)
    ckv = _rms(lat_ref[:, MLA_Q_RANK:MLA_Q_RANK + MLA_KV_RANK], gkv_ref[...]).astype(BF16)
    ang = pos_ref[...].astype(F32) * invf_ref[...]
    lane = lax.broadcasted_iota(jnp.int32, ang.shape, 1)
    live = lane < MLA_ROPE
    cos2 = jnp.where(live, jnp.cos(ang), 0.0)
    sin2 = jnp.where(live, jnp.sin(ang), 0.0)
    half = MLA_ROPE // 2

    def swapped(x):
        return jnp.where(lane < half, -pltpu.roll(x, LANES - half, 1), pltpu.roll(x, half, 1))

    qa = _dot(cq, wqa_ref[...])
    kn = _dot(ckv, wk_ref[...])
    vt_ref[...] = _dot_nt(wvt_ref[...], ckv).astype(BF16)
    k_pe = (lat_ref[:, 768:896] * cos2 + lat_ref[:, 896:1024] * sin2).astype(BF16)
    for h in range(MLA_HEADS):
        lo = h * MLA_PAD
        mid = lo + LANES
        q_ref[:, lo:mid] = qa[:, lo:mid].astype(BF16)
        q_pe = qa[:, mid:mid + LANES]
        q_ref[:, mid:mid + LANES] = (q_pe * cos2 + swapped(q_pe) * sin2).astype(BF16)
        k_ref[:, lo:mid] = kn[:, h * LANES:(h + 1) * LANES].astype(BF16)
        k_ref[:, mid:mid + LANES] = k_pe


def _mla_proj(lat, pos, invf, gq, gkv, wqa, wk, wvt, tm=512):
    t = lat.shape[0]
    full = lambda a: pl.BlockSpec(a.shape, lambda i: (0, 0))
    return pl.pallas_call(
        _mla_proj_kernel,
        grid=(t // tm,),
        in_specs=[
            pl.BlockSpec((tm, LAT_COLS), lambda i: (i, 0)),
            pl.BlockSpec((tm, 1), lambda i: (i, 0)),
            full(invf), full(gq), full(gkv), full(wqa), full(wk), full(wvt),
        ],
        out_specs=[
            pl.BlockSpec((tm, MLA_HEADS * MLA_PAD), lambda i: (i, 0)),
            pl.BlockSpec((tm, MLA_HEADS * MLA_PAD), lambda i: (i, 0)),
            pl.BlockSpec((MLA_OUT, tm), lambda i: (0, i)),
        ],
        out_shape=[
            jax.ShapeDtypeStruct((t, MLA_HEADS * MLA_PAD), BF16),
            jax.ShapeDtypeStruct((t, MLA_HEADS * MLA_PAD), BF16),
            jax.ShapeDtypeStruct((MLA_OUT, t), BF16),
        ],
        compiler_params=_params("parallel"),
        name="mla_proj",
    )(lat, pos, invf, gq, gkv, wqa, wk, wvt)


LOG2E = 1.4426950408889634


def _three_stage(n_pairs, stage_a, stage_b, stage_c):
    a_issue, a_finish = stage_a
    b_issue, b_finish = stage_b
    c_issue, c_finish = stage_c

    def run_a(n, slot, first):
        a_finish(n, slot, a_issue(n, slot, first), first)

    run_a(0, 0, True)
    run_a(1, 1, True)
    b_finish(0, 0, b_issue(0, 0))

    def half(na, sa, nb, sb, nc, sc):
        ra = a_issue(na, sa, False)
        rb = b_issue(nb, sb)
        rc = c_issue(nc, sc)
        a_finish(na, sa, ra, False)
        b_finish(nb, sb, rb)
        c_finish(nc, sc, rc)

    def body(p, carry):
        half(2 * p, 0, 2 * p - 1, 1, 2 * p - 2, 0)
        half(2 * p + 1, 1, 2 * p, 0, 2 * p - 1, 1)
        return carry

    lax.fori_loop(1, n_pairs + 1, body, 0)
    last = 2 * n_pairs + 1
    rb = b_issue(last, 1)
    rc = c_issue(last - 1, 0)
    b_finish(last, 1, rb)
    c_finish(last - 1, 0, rc)
    c_finish(last, 1, c_issue(last, 1))


def _two_stage(n_pairs, produce, consume):
    produce(0, 0, True)
    produce(1, 1, True)
    consume(0, 0)

    def body(p, carry):
        produce(2 * p, 0, False)
        consume(2 * p - 1, 1)
        produce(2 * p + 1, 1, False)
        consume(2 * p, 0)
        return carry

    lax.fori_loop(1, n_pairs + 1, body, 0)
    consume(2 * n_pairs + 1, 1)


MLA_SUM_ROWS = 16
MLA_GROUP = 8


def _mla_attn_kernel(q_ref, k_ref, vt_ref, o_ref, s_ref, m_ref, acc_ref, *, tq, tk):
    i = pl.program_id(2)
    heads = range(MLA_GROUP)
    dq, dv = MLA_PAD, MLA_V
    m_ref[...] = jnp.full(m_ref.shape, -jnp.inf, F32)
    acc_ref[...] = jnp.zeros(acc_ref.shape, F32)
    ones = jnp.ones((MLA_SUM_ROWS, tk), BF16)

    def key_start(n):
        tile = jnp.where(n < 2, 2 * i + n, 2 * i + 1 - n)
        return pl.multiple_of(tile * tk, tk)

    def produce(n, slot, diagonal):
        ks = key_start(n)
        for g in heads:
            st = _dot_nt(k_ref[pl.ds(ks, tk), g * dq:(g + 1) * dq], q_ref[:, g * dq:(g + 1) * dq])
            if diagonal:
                kpos = ks + lax.broadcasted_iota(jnp.int32, st.shape, 0)
                qpos = i * tq + lax.broadcasted_iota(jnp.int32, st.shape, 1)
                st = jnp.where(kpos <= qpos, st, -jnp.inf)
            s_ref[g, slot] = st

    def consume(n, slot):
        ks = key_start(n)
        for g in heads:
            st = s_ref[g, slot]
            m_old = m_ref[g]
            m_new = jnp.maximum(m_old, jnp.max(st, axis=0, keepdims=True))
            alpha = jnp.exp2(m_old - m_new)
            p = jnp.exp2(st - m_new).astype(BF16)
            v_ones = jnp.concatenate([vt_ref[g * dv:(g + 1) * dv, pl.ds(ks, tk)], ones], axis=0)
            acc_ref[g] = alpha * acc_ref[g] + _dot(v_ones, p)
            m_ref[g] = m_new

    _two_stage(i, produce, consume)
    for g in heads:
        o_ref[:, g * dv:(g + 1) * dv] = (acc_ref[g, :dv, :] / acc_ref[g, dv:dv + 1, :]).T


def _mla_attn(q, k, vt, batch, seq, tq=512):
    tk = tq // 2
    nq = seq // tq
    groups = MLA_HEADS // MLA_GROUP
    kern = functools.partial(_mla_attn_kernel, tq=tq, tk=tk)
    return pl.pallas_call(
        kern,
        grid=(batch, groups, nq),
        in_specs=[
            pl.BlockSpec((tq, MLA_GROUP * MLA_PAD), lambda b, h, i: (b * nq + i, h)),
            pl.BlockSpec((seq, MLA_GROUP * MLA_PAD), lambda b, h, i: (b, h), pipeline_mode=pl.Buffered(1)),
            pl.BlockSpec((MLA_GROUP * MLA_V, seq), lambda b, h, i: (h, b), pipeline_mode=pl.Buffered(1)),
        ],
        out_specs=pl.BlockSpec((tq, MLA_GROUP * MLA_V), lambda b, h, i: (b * nq + i, h)),
        out_shape=jax.ShapeDtypeStruct((batch * seq, MLA_OUT), F32),
        scratch_shapes=[
            pltpu.VMEM((MLA_GROUP, 2, tk, tq), F32),
            pltpu.VMEM((MLA_GROUP, 1, tq), F32),
            pltpu.VMEM((MLA_GROUP, MLA_V + MLA_SUM_ROWS, tq), F32),
        ],
        compiler_params=_params("parallel", "parallel", "arbitrary"),
        name="mla_attn",
    )(q, k, vt)


SB_EXP_CLAMP = 126.0
SB_GROUP = 4


def _sb_attn_kernel(q_ref, k_ref, vt_ref, o_ref, z_ref, hl_ref, arg_ref, acc_ref, c_ref, *, tq, tk):
    i = pl.program_id(2)
    heads = range(SB_GROUP)
    d = SB_DIM
    acc_ref[...] = jnp.zeros(acc_ref.shape, F32)
    c_ref[...] = jnp.zeros(c_ref.shape, F32)
    col = lax.broadcasted_iota(jnp.int32, (tk, tk), 1)
    row = lax.broadcasted_iota(jnp.int32, (tk, tk), 0)
    neg_tri = jnp.where(col >= row, -1.0, 0.0).astype(BF16)

    def key_start(n):
        return pl.multiple_of((2 * i + 1 - n) * tk, tk)

    def a_issue(n, slot, masked):
        ks = key_start(n)
        return [_dot_nt(k_ref[pl.ds(ks, tk), g * d:(g + 1) * d], q_ref[:, g * d:(g + 1) * d])
                for g in heads]

    def a_finish(n, slot, z2s, masked):
        for g in heads:
            z2 = z2s[g]
            sp = jnp.maximum(z2, jnp.log2(1.0 + jnp.exp2(jnp.minimum(z2, SB_EXP_CLAMP))))
            if masked:
                kpos = key_start(n) + lax.broadcasted_iota(jnp.int32, z2.shape, 0)
                qpos = i * tq + lax.broadcasted_iota(jnp.int32, z2.shape, 1)
                mask = kpos < qpos
                sp = jnp.where(mask, sp, 0.0)
                z2 = jnp.where(mask, z2, -jnp.inf)
            hl_ref[g, slot] = sp.astype(BF16)
            z_ref[g, slot] = z2

    def b_issue(n, slot):
        return [_dot(neg_tri, hl_ref[g, slot]) for g in heads]

    def b_finish(n, slot, laters):
        for g in heads:
            arg_ref[g, slot] = z_ref[g, slot] + laters[g] + c_ref[g]
            c_ref[g] += laters[g][0:1, :]

    def c_issue(n, slot):
        ks = key_start(n)
        return [_dot(vt_ref[g * d:(g + 1) * d, pl.ds(ks, tk)], jnp.exp2(arg_ref[g, slot]).astype(BF16))
                for g in heads]

    def c_finish(n, slot, pvs):
        for g in heads:
            acc_ref[g] += pvs[g]

    _three_stage(i, (a_issue, a_finish), (b_issue, b_finish), (c_issue, c_finish))
    for g in heads:
        o_ref[:, g * d:(g + 1) * d] = acc_ref[g].T


def _sb_attn(qk, vt, batch, seq, tq=512):
    tk = tq // 2
    nq = seq // tq
    groups = SB_HEADS // SB_GROUP
    gd = SB_GROUP * SB_DIM
    kern = functools.partial(_sb_attn_kernel, tq=tq, tk=tk)
    return pl.pallas_call(
        kern,
        grid=(batch, groups, nq),
        in_specs=[
            pl.BlockSpec((tq, gd), lambda b, h, i: (b * nq + i, h)),
            pl.BlockSpec((seq, gd), lambda b, h, i: (b, groups + h)),
            pl.BlockSpec((gd, seq), lambda b, h, i: (h, b)),
        ],
        out_specs=pl.BlockSpec((tq, gd), lambda b, h, i: (b * nq + i, h)),
        out_shape=jax.ShapeDtypeStruct((batch * seq, SB_OUT), F32),
        scratch_shapes=[
            pltpu.VMEM((SB_GROUP, 2, tk, tq), F32),
            pltpu.VMEM((SB_GROUP, 2, tk, tq), BF16),
            pltpu.VMEM((SB_GROUP, 2, tk, tq), F32),
            pltpu.VMEM((SB_GROUP, SB_DIM, tq), F32),
            pltpu.VMEM((SB_GROUP, 1, tq), F32),
        ],
        compiler_params=_params("parallel", "parallel", "arbitrary"),
        name="sb_attn",
    )(qk, qk, vt)


def _resident(a):
    return pl.BlockSpec(a.shape, lambda i: (0,) * a.ndim, pipeline_mode=pl.Buffered(1))


def _cast_on_first_step(pairs):
    @pl.when(pl.program_id(0) == 0)
    def _():
        for src_ref, dst_ref in pairs:
            dst_ref[...] = src_ref[...].astype(BF16)


def _out_proj_kernel(x_ref, oa_ref, ob_ref, ga_ref, gb_ref, w_ref, y_ref, wb_ref):
    _cast_on_first_step([(w_ref, wb_ref)])
    na = _rms(oa_ref[...], ga_ref[...]).astype(BF16)
    nb = _rms(ob_ref[...], gb_ref[...]).astype(BF16)
    y_ref[...] = x_ref[...] + _dot(na, wb_ref[:MLA_OUT, :]) + _dot(nb, wb_ref[MLA_OUT:, :])


def _out_proj(x, oa, ob, ga, gb, w, tm=512):
    t = x.shape[0]
    full = lambda a: pl.BlockSpec(a.shape, lambda i: (0, 0))
    return pl.pallas_call(
        _out_proj_kernel,
        grid=(t // tm,),
        in_specs=[
            pl.BlockSpec((tm, D_MODEL), lambda i: (i, 0)),
            pl.BlockSpec((tm, MLA_OUT), lambda i: (i, 0)),
            pl.BlockSpec((tm, SB_OUT), lambda i: (i, 0)),
            full(ga), full(gb), _resident(w),
        ],
        out_specs=pl.BlockSpec((tm, D_MODEL), lambda i: (i, 0)),
        out_shape=jax.ShapeDtypeStruct((t, D_MODEL), F32),
        scratch_shapes=[pltpu.VMEM(w.shape, BF16)],
        compiler_params=_params("arbitrary"),
        name="out_proj",
    )(x, oa, ob, ga, gb, w)


def _mem_kv_kernel(mem_ref, g_ref, w_ref, kv_ref):
    kv_ref[...] = _dot(_rms(mem_ref[...], g_ref[...]).astype(BF16), w_ref[...].astype(BF16)).astype(BF16)


def _mem_kv(mem, g, w):
    m = mem.shape[0]
    n = w.shape[1]
    full = lambda a: pl.BlockSpec(a.shape, lambda i: (0, 0))
    return pl.pallas_call(
        _mem_kv_kernel,
        grid=(1,),
        in_specs=[full(mem), full(g), full(w)],
        out_specs=pl.BlockSpec((m, n), lambda i: (0, 0)),
        out_shape=jax.ShapeDtypeStruct((m, n), BF16),
        compiler_params=_params("arbitrary"),
        name="mem_kv",
    )(mem, g, w)


GROUP_LANE0 = N_EXPERTS


def _mem_route_kernel(x_ref, gx_ref, wq_ref, kv_ref, wo_ref, gf_ref, wrhl_ref, br_ref,
                      x2_ref, h2_ref, route_ref, wqb_ref, wob_ref):
    _cast_on_first_step([(wq_ref, wqb_ref), (wo_ref, wob_ref)])
    x1 = x_ref[...]
    q = _dot(_rms(x1, gx_ref[...]).astype(BF16), wqb_ref[...]).astype(BF16)
    scale = MEM_DIM ** -0.5
    kw = MEM_HEADS * MEM_DIM
    heads = []
    for h in range(MEM_HEADS):
        lo = h * MEM_DIM
        s = _dot_nt(q[:, lo:lo + MEM_DIM], kv_ref[:, lo:lo + MEM_DIM]) * scale
        e = jnp.exp(s - jnp.max(s, axis=-1, keepdims=True))
        p = (e / jnp.sum(e, axis=-1, keepdims=True)).astype(BF16)
        heads.append(_dot(p, kv_ref[:, kw + lo:kw + lo + MEM_DIM]).astype(BF16))
    o = jnp.concatenate(heads, axis=-1)
    x2 = x1 + _dot(o, wob_ref[...])
    x2_ref[...] = x2
    h2 = _rms(x2, gf_ref[...])
    h2_ref[...] = h2

    hh, hl = _split_bf16(h2)
    both = _dot(hh, wrhl_ref[...])
    lg = both[:, :LANES] + both[:, LANES:] + _dot(hl, wrhl_ref[:, :LANES]) + br_ref[...]
    lane = lax.broadcasted_iota(jnp.int32, lg.shape, 1)
    big = jnp.int32(1 << 20)
    ninf = -jnp.inf

    def lane_max(v):
        return jnp.max(v, axis=-1, keepdims=True)

    def first_lane(cond):
        return jnp.min(jnp.where(cond, lane, big), axis=-1, keepdims=True)

    is_g = (lane >= GROUP_LANE0) & (lane < GROUP_LANE0 + N_GROUPS)
    g_max = lane_max(jnp.where(is_g, lg, ninf))
    g_sum = jnp.sum(jnp.where(is_g, jnp.exp(lg - g_max), 0.0), axis=-1, keepdims=True)
    p_g = 1.0 / g_sum
    g_idx = first_lane(is_g & (lg == g_max)) - GROUP_LANE0
    in_grp = (lane < N_EXPERTS) & ((lane // EXPERTS_PER_GROUP) == g_idx)
    e_max = lane_max(jnp.where(in_grp, lg, ninf))
    e_sum = jnp.sum(jnp.where(in_grp, jnp.exp(lg - e_max), 0.0), axis=-1, keepdims=True)
    i1 = first_lane(in_grp & (lg == e_max))
    rest = in_grp & (lane != i1)
    e_max2 = lane_max(jnp.where(rest, lg, ninf))
    i2 = first_lane(rest & (lg == e_max2))
    p1 = 1.0 / e_sum
    p2 = jnp.exp(e_max2 - e_max) / e_sum
    den = p1 + p2
    gate1 = p_g * (p1 / den)
    gate2 = p_g * (p2 / den)
    route = jnp.where(lane == 0, i1.astype(F32),
                      jnp.where(lane == 1, i2.astype(F32),
                                jnp.where(lane == 2, gate1,
                                          jnp.where(lane == 3, gate2, 0.0))))
    route_ref[...] = route


def _mem_route(x1, gx, wq, kv, wo, gf, wrhl, br, seq, tm=512):
    t = x1.shape[0]
    per_batch = seq // tm
    full = lambda a: pl.BlockSpec(a.shape, lambda i: (0, 0))
    return pl.pallas_call(
        _mem_route_kernel,
        grid=(t // tm,),
        in_specs=[
            pl.BlockSpec((tm, D_MODEL), lambda i: (i, 0)),
            full(gx), _resident(wq),
            pl.BlockSpec((MEM_LEN, kv.shape[1]), lambda i: (i // per_batch, 0)),
            _resident(wo), full(gf), full(wrhl), full(br),
        ],
        out_specs=[
            pl.BlockSpec((tm, D_MODEL), lambda i: (i, 0)),
            pl.BlockSpec((tm, D_MODEL), lambda i: (i, 0)),
            pl.BlockSpec((tm, LANES), lambda i: (i, 0)),
        ],
        out_shape=[
            jax.ShapeDtypeStruct((t, D_MODEL), F32),
            jax.ShapeDtypeStruct((t, D_MODEL), F32),
            jax.ShapeDtypeStruct((t, LANES), F32),
        ],
        scratch_shapes=[pltpu.VMEM(wq.shape, BF16), pltpu.VMEM(wo.shape, BF16)],
        compiler_params=_params("arbitrary"),
        name="mem_route",
    )(x1, gx, wq, kv, wo, gf, wrhl, br)


SLOT_BLK = 256


def _slots_kernel(route_ref, slot_ref, meta_ref, cum_ref, *, n_tok):
    nblk = n_tok // SLOT_BLK
    e_iota = lax.broadcasted_iota(jnp.int32, (LANES, SLOT_BLK), 0)
    incl = (lax.broadcasted_iota(jnp.int32, (SLOT_BLK, SLOT_BLK), 0)
            <= lax.broadcasted_iota(jnp.int32, (SLOT_BLK, SLOT_BLK), 1)).astype(BF16)

    def onehots(b):
        ts = pl.multiple_of(b * SLOT_BLK, SLOT_BLK)
        ids = route_ref[pl.ds(ts, SLOT_BLK), :].T
        oh1 = e_iota == ids[0:1, :].astype(jnp.int32)
        oh2 = e_iota == ids[1:2, :].astype(jnp.int32)
        return ts, oh1, oh2

    def count(b, carry):
        ts, oh1, oh2 = onehots(b)
        cnt = (oh1 | oh2).astype(F32).astype(BF16)
        c = _dot(cnt, incl) + carry
        cum_ref[:, pl.ds(ts, SLOT_BLK)] = c
        return c[:, SLOT_BLK - 1:SLOT_BLK]

    total = lax.fori_loop(0, nblk, count, jnp.zeros((LANES, 1), F32))
    tiles = jnp.floor((total + (SLOT_TILE - 1)) * (1.0 / SLOT_TILE))
    below = (lax.broadcasted_iota(jnp.int32, (LANES, LANES), 1)
             < lax.broadcasted_iota(jnp.int32, (LANES, LANES), 0)).astype(BF16)
    tile_lo = _dot(below, jnp.broadcast_to(tiles, (LANES, LANES)).astype(BF16))
    base = tile_lo[:, 0:1] * SLOT_TILE

    def assign(b, carry):
        ts, oh1, oh2 = onehots(b)
        pos = base + cum_ref[:, pl.ds(ts, SLOT_BLK)] - 1.0
        s1 = jnp.sum(jnp.where(oh1, pos, 0.0), axis=0, keepdims=True)
        s2 = jnp.sum(jnp.where(oh2, pos, 0.0), axis=0, keepdims=True)
        row = lax.broadcasted_iota(jnp.int32, (8, SLOT_BLK), 0)
        slot_ref[:, pl.ds(ts, SLOT_BLK)] = jnp.where(row == 0, s1, jnp.where(row == 1, s2, 0.0)).astype(jnp.int32)
        return carry

    lax.fori_loop(0, nblk, assign, 0)

    tile_hi = tile_lo + tiles
    tix = lax.broadcasted_iota(jnp.int32, (LANES, LANES), 1).astype(F32)
    is_e = lax.broadcasted_iota(jnp.int32, (LANES, LANES), 0) < N_EXPERTS
    owner = jnp.sum(jnp.where(is_e & (tile_hi <= tix), 1.0, 0.0), axis=0, keepdims=True)
    n_tiles = jnp.max(jnp.where(is_e, tile_hi, 0.0), axis=0, keepdims=True)
    valid = tix[0:1, :] < n_tiles
    last_owner = jnp.max(jnp.where(valid, owner, 0.0), axis=1, keepdims=True)
    owner = jnp.where(valid, owner, last_owner)
    eye = (lax.broadcasted_iota(jnp.int32, (LANES, LANES), 0)
           == lax.broadcasted_iota(jnp.int32, (LANES, LANES), 1))
    to_lanes = lambda colvec: jnp.sum(jnp.where(eye, colvec, 0.0), axis=0, keepdims=True)
    pad_first = to_lanes(base + total)
    pad_end = to_lanes(base + tiles * SLOT_TILE)
    row = lax.broadcasted_iota(jnp.int32, (8, LANES), 0)
    meta = jnp.where(row == 0, owner,
                     jnp.where(row == 1, valid.astype(F32),
                               jnp.where(row == 2, pad_first,
                                         jnp.where(row == 3, pad_end,
                                                   jnp.where(row == 4, n_tiles, 0.0)))))
    meta_ref[...] = meta.astype(jnp.int32)


def _slots(route):
    t = route.shape[0]
    kern = functools.partial(_slots_kernel, n_tok=t)
    return pl.pallas_call(
        kern,
        grid=(1,),
        in_specs=[pl.BlockSpec(route.shape, lambda i: (0, 0))],
        out_specs=[
            pl.BlockSpec((8, t), lambda i: (0, 0)),
            pl.BlockSpec((8, LANES), lambda i: (0, 0)),
        ],
        out_shape=[
            jax.ShapeDtypeStruct((8, t), jnp.int32),
            jax.ShapeDtypeStruct((8, LANES), jnp.int32),
        ],
        scratch_shapes=[pltpu.VMEM((LANES, t), F32)],
        compiler_params=_params("arbitrary"),
        name="slots",
    )(route)


def _row_copy(src_ref, src_row, dst_ref, dst_row, sem):
    return pltpu.make_async_copy(src_ref.at[pl.ds(src_row, 1), :], dst_ref.at[pl.ds(dst_row, 1), :], sem)


def _dispatch_kernel(s1_ref, s2_ref, pad_first_ref, pad_end_ref, n_tiles_ref, h_ref, xs_ref, zero_ref,
                     sem, pad_sem, *, tm):
    i = pl.program_id(0)
    t0 = i * tm
    n_slot_tiles = xs_ref.shape[0] // SLOT_TILE

    def copies(r):
        return (_row_copy(h_ref, r, xs_ref, s1_ref[t0 + r], sem),
                _row_copy(h_ref, r, xs_ref, s2_ref[t0 + r], sem))

    def issue(r, c):
        first, second = copies(r)
        first.start(priority=0)
        second.start(priority=1)
        return c

    lax.fori_loop(0, tm, issue, 0, unroll=8)

    @pl.when(i == pl.num_programs(0) - 1)
    def _():
        zero_ref[...] = jnp.zeros(zero_ref.shape, F32)

        def pad_copy(slot):
            return _row_copy(zero_ref, 0, xs_ref, slot, pad_sem)

        def fill(e, c):
            def one(s, cc):
                pad_copy(s).start()
                return cc

            return lax.fori_loop(pad_first_ref[e], pad_end_ref[e], one, c)

        lax.fori_loop(0, N_EXPERTS, fill, 0)

        def tile_copy(tile):
            return pltpu.make_async_copy(zero_ref, xs_ref.at[pl.ds(tile * SLOT_TILE, SLOT_TILE), :], pad_sem)

        def fill_tile(tile, c):
            tile_copy(tile).start()
            return c

        lax.fori_loop(n_tiles_ref[0], n_slot_tiles, fill_tile, 0)

        def settle(e, c):
            def one(s, cc):
                pad_copy(s).wait()
                return cc

            return lax.fori_loop(pad_first_ref[e], pad_end_ref[e], one, c)

        lax.fori_loop(0, N_EXPERTS, settle, 0)

        def settle_tile(tile, c):
            tile_copy(tile).wait()
            return c

        lax.fori_loop(n_tiles_ref[0], n_slot_tiles, settle_tile, 0)

    def drain(r, c):
        for cp in copies(r):
            cp.wait()
        return c

    lax.fori_loop(0, tm, drain, 0, unroll=8)


def _dispatch(slot1, slot2, pad_first, pad_end, n_tiles, h2, n_slots, tm=256):
    t = h2.shape[0]
    kern = functools.partial(_dispatch_kernel, tm=tm)
    return pl.pallas_call(
        kern,
        grid_spec=pltpu.PrefetchScalarGridSpec(
            num_scalar_prefetch=5,
            grid=(t // tm,),
            in_specs=[pl.BlockSpec((tm, D_MODEL), lambda i, *_: (i, 0))],
            out_specs=pl.BlockSpec(memory_space=pl.ANY),
            scratch_shapes=[
                pltpu.VMEM((SLOT_TILE, D_MODEL), F32),
                pltpu.SemaphoreType.DMA(()),
                pltpu.SemaphoreType.DMA(()),
            ],
        ),
        out_shape=jax.ShapeDtypeStruct((n_slots, D_MODEL), F32),
        compiler_params=_params("arbitrary"),
        name="dispatch",
    )(slot1, slot2, pad_first, pad_end, n_tiles, h2)


def _experts_kernel(own_ref, valid_ref, xs_ref, wg_hbm, wu_hbm, wd_hbm, ys_ref,
                    wgf, wuf, wdf, wgb, wub, wdb, wsem, wslot_ref):
    i = pl.program_id(0)
    nt = pl.num_programs(0)
    at = lambda ref, j: ref[jnp.minimum(j, nt - 1)]

    def weight_copies(e, s):
        return (pltpu.make_async_copy(wg_hbm.at[e], wgf.at[s], wsem.at[s]),
                pltpu.make_async_copy(wu_hbm.at[e], wuf.at[s], wsem.at[s]),
                pltpu.make_async_copy(wd_hbm.at[e], wdf.at[s], wsem.at[s]))

    def next_expert_tile(j0):
        e0 = at(own_ref, j0)
        return lax.while_loop(lambda j: (j < nt) & (at(own_ref, j) == e0), lambda j: j + 1, j0 + 1)

    def start_weights(j, s):
        @pl.when((j < nt) & (at(valid_ref, j) > 0))
        def _():
            for cp in weight_copies(at(own_ref, j), s):
                cp.start()

    @pl.when(i == 0)
    def _():
        wslot_ref[0] = 0
        j = 0
        for s in range(WEIGHT_SLOTS - 1):
            start_weights(j, s)
            j = next_expert_tile(j)

    @pl.when(valid_ref[i] > 0)
    def _():
        e = own_ref[i]

        @pl.when((i == 0) | (own_ref[jnp.maximum(i - 1, 0)] != e))
        def _():
            s = wslot_ref[0]
            for cp in weight_copies(e, s):
                cp.wait()
            j = i
            for _ in range(WEIGHT_SLOTS - 1):
                j = next_expert_tile(j)
            start_weights(j, (s + WEIGHT_SLOTS - 1) % WEIGHT_SLOTS)

            wgb[...] = wgf[s].astype(BF16)
            wub[...] = wuf[s].astype(BF16)
            wdb[...] = wdf[s].astype(BF16)
            wslot_ref[0] = (s + 1) % WEIGHT_SLOTS

        x = xs_ref[...].astype(BF16)
        a = _dot(x, wgb[...])
        u = _dot(x, wub[...])
        act = (a * jax.nn.sigmoid(a) * u).astype(BF16)
        ys_ref[...] = _dot(act, wdb[...])

    @pl.when(valid_ref[i] == 0)
    def _():
        ys_ref[...] = jnp.zeros(ys_ref.shape, F32)


def _experts(own, valid, xs, wg, wu, wd):
    n_tiles = xs.shape[0] // SLOT_TILE
    hbm = pl.BlockSpec(memory_space=pl.ANY)
    tile = (SLOT_TILE, D_MODEL)
    return pl.pallas_call(
        _experts_kernel,
        grid_spec=pltpu.PrefetchScalarGridSpec(
            num_scalar_prefetch=2,
            grid=(n_tiles,),
            in_specs=[
                pl.BlockSpec(tile, lambda i, o, v: (jnp.where(v[i] > 0, i, 0), 0)),
                hbm, hbm, hbm,
            ],
            out_specs=pl.BlockSpec(tile, lambda i, o, v: (i, 0)),
            scratch_shapes=[
                pltpu.VMEM((WEIGHT_SLOTS, D_MODEL, D_EXPERT), F32),
                pltpu.VMEM((WEIGHT_SLOTS, D_MODEL, D_EXPERT), F32),
                pltpu.VMEM((WEIGHT_SLOTS, D_EXPERT, D_MODEL), F32),
                pltpu.VMEM((D_MODEL, D_EXPERT), BF16),
                pltpu.VMEM((D_MODEL, D_EXPERT), BF16),
                pltpu.VMEM((D_EXPERT, D_MODEL), BF16),
                pltpu.SemaphoreType.DMA((WEIGHT_SLOTS,)),
                pltpu.SMEM((1,), jnp.int32),
            ],
        ),
        out_shape=jax.ShapeDtypeStruct(xs.shape, F32),
        compiler_params=_params("arbitrary"),
        name="experts",
    )(own, valid, xs, wg, wu, wd)


def _combine_kernel(s1_ref, s2_ref, x_ref, route_ref, g_ref, ys_ref, y_ref, buf_ref, sem, *, tm):
    i = pl.program_id(0)
    n = pl.num_programs(0)

    def copies(tile, p, r):
        t = tile * tm + r
        return (_row_copy(ys_ref, s1_ref[t], buf_ref.at[p, 0], r, sem.at[p]),
                _row_copy(ys_ref, s2_ref[t], buf_ref.at[p, 1], r, sem.at[p]))

    def gather_start(tile, p):
        def issue(r, c):
            first, second = copies(tile, p, r)
            first.start(priority=0)
            second.start(priority=1)
            return c

        lax.fori_loop(0, tm, issue, 0, unroll=8)

    def gather_wait(tile, p):
        def drain(r, c):
            for cp in copies(tile, p, r):
                cp.wait()
            return c

        lax.fori_loop(0, tm, drain, 0, unroll=8)

    @pl.when(i == 0)
    def _():
        gather_start(0, 0)

    p = i % 2

    @pl.when(i + 1 < n)
    def _():
        gather_start(i + 1, 1 - p)

    gather_wait(i, p)
    y = x_ref[...] + route_ref[:, 2:3] * buf_ref[p, 0] + route_ref[:, 3:4] * buf_ref[p, 1]
    y_ref[...] = _rms(y, g_ref[...])


def _combine(slot1, slot2, x2, route, g, ys, tm=256):
    t = x2.shape[0]
    kern = functools.partial(_combine_kernel, tm=tm)
    return pl.pallas_call(
        kern,
        grid_spec=pltpu.PrefetchScalarGridSpec(
            num_scalar_prefetch=2,
            grid=(t // tm,),
            in_specs=[
                pl.BlockSpec((tm, D_MODEL), lambda i, s1, s2: (i, 0)),
                pl.BlockSpec((tm, LANES), lambda i, s1, s2: (i, 0)),
                pl.BlockSpec((1, D_MODEL), lambda i, s1, s2: (0, 0)),
                pl.BlockSpec(memory_space=pl.ANY),
            ],
            out_specs=pl.BlockSpec((tm, D_MODEL), lambda i, s1, s2: (i, 0)),
            scratch_shapes=[
                pltpu.VMEM((2, 2, tm, D_MODEL), F32),
                pltpu.SemaphoreType.DMA((2,)),
            ],
        ),
        out_shape=jax.ShapeDtypeStruct((t, D_MODEL), F32),
        compiler_params=_params("arbitrary"),
        name="combine",
    )(slot1, slot2, x2, route, g, ys)


def _row(v):
    return v.reshape(1, -1).astype(F32)


def kernel(x, mem, positions, norm_mix, w_in, norm_q_lat, w_q_b, norm_kv_lat, w_kv_b, norm_mla_out, norm_sb_out, w_out, norm_mem_x, norm_mem_src, w_mem_q, w_mem_kv, w_mem_o, norm_ffn, w_group, b_group, w_expert_router, b_expert, w_gate, w_up, w_down, norm_final):
    batch, seq, d = x.shape
    t = batch * seq
    depth = w_in.shape[0]
    assert depth == 1, "single-layer trunk only"
    xt = x.reshape(t, d)
    pos = positions.reshape(t, 1)
    inv_freq = ROPE_THETA ** (-jnp.arange(0, MLA_ROPE, 2, dtype=F32) / MLA_ROPE)
    invf = jnp.concatenate([inv_freq, inv_freq, jnp.zeros((LANES - MLA_ROPE,), F32)]).reshape(1, LANES)
    n_slots = (2 * t // SLOT_TILE + N_EXPERTS) * SLOT_TILE

    for l in range(depth):
        wi_t = w_in[l].T
        lat_w = MLA_Q_RANK + MLA_KV_RANK
        w_kpe_t = wi_t[lat_w:lat_w + MLA_ROPE]
        zpad = jnp.zeros((LANES - MLA_ROPE, d), F32)
        sb0 = lat_w + MLA_ROPE
        half = MLA_ROPE // 2
        w_kpe_swapped_t = jnp.concatenate([-w_kpe_t[half:], w_kpe_t[:half]], axis=0)
        w_lat_t = jnp.concatenate([wi_t[:sb0], zpad, w_kpe_swapped_t, zpad], axis=0).astype(BF16)
        sb_scale = jnp.concatenate([jnp.full((SB_OUT,), SB_DIM ** -0.5 * LOG2E, F32), jnp.ones((2 * SB_OUT,), F32)])
        w_sb_t = (wi_t[sb0:] * sb_scale[:, None]).astype(BF16)
        wq = w_q_b[l].reshape(MLA_Q_RANK, MLA_HEADS, MLA_QK) * (MLA_QK ** -0.5 * LOG2E)
        zq = jnp.zeros((MLA_Q_RANK, MLA_HEADS, MLA_PAD - MLA_QK), F32)
        wqa = jnp.concatenate([wq, zq], axis=-1).reshape(MLA_Q_RANK, MLA_HEADS * MLA_PAD).astype(BF16)
        wkv = w_kv_b[l].reshape(MLA_KV_RANK, MLA_HEADS, MLA_NOPE + MLA_V)
        wk = wkv[:, :, :MLA_NOPE].reshape(MLA_KV_RANK, MLA_OUT).astype(BF16)
        wvt = wkv[:, :, MLA_NOPE:].reshape(MLA_KV_RANK, MLA_OUT).T.astype(BF16)
        w_router = jnp.concatenate(
            [w_expert_router[l], w_group[l], jnp.zeros((d, LANES - N_EXPERTS - N_GROUPS), F32)], axis=1)
        wr_hi = w_router.astype(BF16)
        wr_hl = jnp.concatenate([wr_hi, (w_router - wr_hi.astype(F32)).astype(BF16)], axis=1)
        b_router = jnp.concatenate(
            [b_expert[l].astype(F32), b_group[l].astype(F32), jnp.zeros((LANES - N_EXPERTS - N_GROUPS,), F32)]
        ).reshape(1, LANES)

        lat, sb_qk, sb_vt = _proj_in(xt, _row(norm_mix[l]), w_lat_t, w_sb_t)
        q, k, vt = _mla_proj(lat, pos, invf, _row(norm_q_lat[l]), _row(norm_kv_lat[l]), wqa, wk, wvt)
        o_mla = _mla_attn(q, k, vt, batch, seq)
        o_sb = _sb_attn(sb_qk, sb_vt, batch, seq)
        x1 = _out_proj(xt, o_mla, o_sb, _row(norm_mla_out[l]), _row(norm_sb_out[l]), w_out[l])

        kv = _mem_kv(mem.reshape(batch * MEM_LEN, d), _row(norm_mem_src[l]), w_mem_kv[l])
        x2, h2, route = _mem_route(x1, _row(norm_mem_x[l]), w_mem_q[l], kv,
                                   w_mem_o[l], _row(norm_ffn[l]), wr_hl, b_router, seq)

        slots, meta = _slots(route)
        xs = _dispatch(slots[0], slots[1], meta[2], meta[3], meta[4], h2, n_slots)
        ys = _experts(meta[0], meta[1], xs, w_gate[l], w_up[l], w_down[l])
        xt = _combine(slots[0], slots[1], x2, route, _row(norm_final), ys)
    return xt.reshape(batch, seq, d)
```

```python
import functools

import jax
import jax.numpy as jnp
from jax import lax
from jax.experimental import pallas as pl
from jax.experimental.pallas import tpu as pltpu

F32 = jnp.float32
BF16 = jnp.bfloat16

EPS = 1e-6
ROPE_THETA = 10000.0

D_MODEL = 2048
MEM_LEN = 256
MLA_HEADS = 8
MLA_NOPE = 128
MLA_ROPE = 64
MLA_QK = MLA_NOPE + MLA_ROPE
MLA_V = 128
MLA_Q_RANK = 512
MLA_KV_RANK = 256
MLA_PAD = 256
SB_HEADS = 8
SB_DIM = 128
MLA_OUT = MLA_HEADS * MLA_V
SB_OUT = SB_HEADS * SB_DIM
MEM_HEADS = 4
MEM_DIM = 128
N_GROUPS = 4
EXPERTS_PER_GROUP = 8
N_EXPERTS = N_GROUPS * EXPERTS_PER_GROUP
D_EXPERT = 512

LANES = 128
LAT_COLS = 1024

SLOT_TILE = 256
WEIGHT_SLOTS = 2
VMEM_LIMIT = 56 * 1024 * 1024


def _rms(x, g):
    return x * lax.rsqrt(jnp.mean(x * x, axis=-1, keepdims=True) + EPS) * g


def _dot(a, b):
    return jnp.dot(a, b, preferred_element_type=F32)


def _dot_nt(a, b):
    return lax.dot_general(a, b, (((1,), (1,)), ((), ())), preferred_element_type=F32)


def _split_bf16(x):
    hi = x.astype(BF16)
    lo = (x - hi.astype(F32)).astype(BF16)
    return hi, lo


def _params(*sem):
    return pltpu.CompilerParams(dimension_semantics=sem, vmem_limit_bytes=VMEM_LIMIT)


def _proj_in_kernel(x_ref, g_ref, wl_ref, ws_ref, lat_ref, qk_ref, vt_ref, h_ref):
    j = pl.program_id(1)
    last = pl.num_programs(1) - 1

    @pl.when(j == 0)
    def _():
        h_ref[...] = _rms(x_ref[...], g_ref[...]).astype(BF16)
        lat_ref[...] = _dot_nt(h_ref[...], wl_ref[...])

    @pl.when((j > 0) & (j < last))
    def _():
        qk_ref[...] = _dot_nt(h_ref[...], ws_ref[...]).astype(BF16)

    @pl.when(j == last)
    def _():
        vt_ref[...] = _dot_nt(ws_ref[...], h_ref[...]).astype(BF16)


def _proj_in(x, g, w_t, sb0, tm=512):
    t = x.shape[0]
    tn = LAT_COLS
    n_sb = (w_t.shape[0] - sb0) // tn
    return pl.pallas_call(
        _proj_in_kernel,
        grid=(t // tm, n_sb + 1),
        in_specs=[
            pl.BlockSpec((tm, D_MODEL), lambda i, j: (i, 0)),
            pl.BlockSpec((1, D_MODEL), lambda i, j: (0, 0)),
            pl.BlockSpec((tn, D_MODEL), lambda i, j: (0, 0)),
            pl.BlockSpec((pl.Element(tn), pl.Element(D_MODEL)),
                         lambda i, j: (pl.multiple_of(sb0 + jnp.maximum(j - 1, 0) * tn, MLA_ROPE), 0)),
        ],
        out_specs=[
            pl.BlockSpec((tm, tn), lambda i, j: (i, 0)),
            pl.BlockSpec((tm, tn), lambda i, j: (i, jnp.clip(j - 1, 0, n_sb - 2))),
            pl.BlockSpec((SB_OUT, tm), lambda i, j: (0, i)),
        ],
        out_shape=[
            jax.ShapeDtypeStruct((t, LAT_COLS), F32),
            jax.ShapeDtypeStruct((t, 2 * SB_OUT), BF16),
            jax.ShapeDtypeStruct((SB_OUT, t), BF16),
        ],
        scratch_shapes=[pltpu.VMEM((tm, D_MODEL), BF16)],
        compiler_params=_params("parallel", "arbitrary"),
        name="proj_in",
    )(x, g, w_t, w_t)


def _mla_proj_kernel(lat_ref, pos_ref, invf_ref, gq_ref, gkv_ref, wqa_ref, wk_ref, wvt_ref,
                     q_ref, k_ref, vt_ref):
    cq = _rms(lat_ref[:, :MLA_Q_RANK], gq_ref[...]).astype(BF16)
    ckv = _rms(lat_ref[:, MLA_Q_RANK:MLA_Q_RANK + MLA_KV_RANK], gkv_ref[...]).astype(BF16)
    ang = pos_ref[...].astype(F32) * invf_ref[...]
    lane = lax.broadcasted_iota(jnp.int32, ang.shape, 1)
    live = lane < MLA_ROPE
    cos2 = jnp.where(live, jnp.cos(ang), 0.0)
    sin2 = jnp.where(live, jnp.sin(ang), 0.0)
    half = MLA_ROPE // 2

    def swapped(x):
        return jnp.where(lane < half, -pltpu.roll(x, LANES - half, 1), pltpu.roll(x, half, 1))

    qa = _dot(cq, wqa_ref[...])
    kn = _dot(ckv, wk_ref[...])
    vt_ref[...] = _dot_nt(wvt_ref[...], ckv).astype(BF16)
    k_raw = lat_ref[:, MLA_Q_RANK + MLA_KV_RANK:MLA_Q_RANK + MLA_KV_RANK + LANES]
    k_pe = jnp.where(live, k_raw * cos2 + swapped(k_raw) * sin2, 0.0).astype(BF16)
    for h in range(MLA_HEADS):
        lo = h * MLA_PAD
        mid = lo + LANES
        q_ref[:, lo:mid] = qa[:, lo:mid].astype(BF16)
        q_pe = qa[:, mid:mid + LANES]
        q_ref[:, mid:mid + LANES] = (q_pe * cos2 + swapped(q_pe) * sin2).astype(BF16)
        k_ref[:, lo:mid] = kn[:, h * LANES:(h + 1) * LANES].astype(BF16)
        k_ref[:, mid:mid + LANES] = k_pe


def _mla_proj(lat, pos, invf, gq, gkv, wqa, wk, wvt, tm=512):
    t = lat.shape[0]
    full = lambda a: pl.BlockSpec(a.shape, lambda i: (0, 0))
    return pl.pallas_call(
        _mla_proj_kernel,
        grid=(t // tm,),
        in_specs=[
            pl.BlockSpec((tm, LAT_COLS), lambda i: (i, 0)),
            pl.BlockSpec((tm, 1), lambda i: (i, 0)),
            full(invf), full(gq), full(gkv), full(wqa), full(wk), full(wvt),
        ],
        out_specs=[
            pl.BlockSpec((tm, MLA_HEADS * MLA_PAD), lambda i: (i, 0)),
            pl.BlockSpec((tm, MLA_HEADS * MLA_PAD), lambda i: (i, 0)),
            pl.BlockSpec((MLA_OUT, tm), lambda i: (0, i)),
        ],
        out_shape=[
            jax.ShapeDtypeStruct((t, MLA_HEADS * MLA_PAD), BF16),
            jax.ShapeDtypeStruct((t, MLA_HEADS * MLA_PAD), BF16),
            jax.ShapeDtypeStruct((MLA_OUT, t), BF16),
        ],
        compiler_params=_params("parallel"),
        name="mla_proj",
    )(lat, pos, invf, gq, gkv, wqa, wk, wvt)


LOG2E = 1.4426950408889634


def _three_stage(n_pairs, stage_a, stage_b, stage_c):
    a_issue, a_finish = stage_a
    b_issue, b_finish = stage_b
    c_issue, c_finish = stage_c

    def run_a(n, slot, first):
        a_finish(n, slot, a_issue(n, slot, first), first)

    run_a(0, 0, True)
    run_a(1, 1, True)
    b_finish(0, 0, b_issue(0, 0))

    def half(na, sa, nb, sb, nc, sc):
        ra = a_issue(na, sa, False)
        rb = b_issue(nb, sb)
        rc = c_issue(nc, sc)
        a_finish(na, sa, ra, False)
        b_finish(nb, sb, rb)
        c_finish(nc, sc, rc)

    def body(p, carry):
        half(2 * p, 0, 2 * p - 1, 1, 2 * p - 2, 0)
        half(2 * p + 1, 1, 2 * p, 0, 2 * p - 1, 1)
        return carry

    lax.fori_loop(1, n_pairs + 1, body, 0)
    last = 2 * n_pairs + 1
    rb = b_issue(last, 1)
    rc = c_issue(last - 1, 0)
    b_finish(last, 1, rb)
    c_finish(last - 1, 0, rc)
    c_finish(last, 1, c_issue(last, 1))


def _two_stage(n_pairs, produce, consume):
    produce(0, 0, True)
    produce(1, 1, True)
    consume(0, 0)

    def body(p, carry):
        produce(2 * p, 0, False)
        consume(2 * p - 1, 1)
        produce(2 * p + 1, 1, False)
        consume(2 * p, 0)
        return carry

    lax.fori_loop(1, n_pairs + 1, body, 0)
    consume(2 * n_pairs + 1, 1)


MLA_SUM_ROWS = 16
MLA_GROUP = 8


def _mla_attn_kernel(q_ref, k_ref, vt_ref, o_ref, s_ref, m_ref, acc_ref, *, tq, tk):
    i = pl.program_id(2)
    heads = range(MLA_GROUP)
    dq, dv = MLA_PAD, MLA_V
    m_ref[...] = jnp.full(m_ref.shape, -jnp.inf, F32)
    acc_ref[...] = jnp.zeros(acc_ref.shape, F32)
    ones = jnp.ones((MLA_SUM_ROWS, tk), BF16)

    def key_start(n):
        tile = jnp.where(n < 2, 2 * i + n, 2 * i + 1 - n)
        return pl.multiple_of(tile * tk, tk)

    def produce(n, slot, diagonal):
        ks = key_start(n)
        for g in heads:
            st = _dot_nt(k_ref[pl.ds(ks, tk), g * dq:(g + 1) * dq], q_ref[:, g * dq:(g + 1) * dq])
            if diagonal:
                kpos = ks + lax.broadcasted_iota(jnp.int32, st.shape, 0)
                qpos = i * tq + lax.broadcasted_iota(jnp.int32, st.shape, 1)
                st = jnp.where(kpos <= qpos, st, -jnp.inf)
            s_ref[g, slot] = st

    def consume(n, slot):
        ks = key_start(n)
        for g in heads:
            st = s_ref[g, slot]
            m_old = m_ref[g]
            m_new = jnp.maximum(m_old, jnp.max(st, axis=0, keepdims=True))
            alpha = jnp.exp2(m_old - m_new)
            p = jnp.exp2(st - m_new).astype(BF16)
            v_ones = jnp.concatenate([vt_ref[g * dv:(g + 1) * dv, pl.ds(ks, tk)], ones], axis=0)
            acc_ref[g] = alpha * acc_ref[g] + _dot(v_ones, p)
            m_ref[g] = m_new

    _two_stage(i, produce, consume)
    for g in heads:
        o_ref[:, g * dv:(g + 1) * dv] = (acc_ref[g, :dv, :] / acc_ref[g, dv:dv + 1, :]).T


def _mla_attn(q, k, vt, batch, seq, tq=512):
    tk = tq // 2
    nq = seq // tq
    groups = MLA_HEADS // MLA_GROUP
    kern = functools.partial(_mla_attn_kernel, tq=tq, tk=tk)
    return pl.pallas_call(
        kern,
        grid=(batch, groups, nq),
        in_specs=[
            pl.BlockSpec((tq, MLA_GROUP * MLA_PAD), lambda b, h, i: (b * nq + i, h)),
            pl.BlockSpec((seq, MLA_GROUP * MLA_PAD), lambda b, h, i: (b, h), pipeline_mode=pl.Buffered(1)),
            pl.BlockSpec((MLA_GROUP * MLA_V, seq), lambda b, h, i: (h, b), pipeline_mode=pl.Buffered(1)),
        ],
        out_specs=pl.BlockSpec((tq, MLA_GROUP * MLA_V), lambda b, h, i: (b * nq + i, h)),
        out_shape=jax.ShapeDtypeStruct((batch * seq, MLA_OUT), F32),
        scratch_shapes=[
            pltpu.VMEM((MLA_GROUP, 2, tk, tq), F32),
            pltpu.VMEM((MLA_GROUP, 1, tq), F32),
            pltpu.VMEM((MLA_GROUP, MLA_V + MLA_SUM_ROWS, tq), F32),
        ],
        compiler_params=_params("parallel", "parallel", "arbitrary"),
        name="mla_attn",
    )(q, k, vt)


SB_EXP_CLAMP = 126.0
SB_GROUP = 4


def _sb_attn_kernel(q_ref, k_ref, vt_ref, o_ref, z_ref, hl_ref, arg_ref, acc_ref, c_ref, *, tq, tk):
    i = pl.program_id(2)
    heads = range(SB_GROUP)
    d = SB_DIM
    acc_ref[...] = jnp.zeros(acc_ref.shape, F32)
    c_ref[...] = jnp.zeros(c_ref.shape, F32)
    col = lax.broadcasted_iota(jnp.int32, (tk, tk), 1)
    row = lax.broadcasted_iota(jnp.int32, (tk, tk), 0)
    neg_tri = jnp.where(col >= row, -1.0, 0.0).astype(BF16)

    def key_start(n):
        return pl.multiple_of((2 * i + 1 - n) * tk, tk)

    def a_issue(n, slot, masked):
        ks = key_start(n)
        return [_dot_nt(k_ref[pl.ds(ks, tk), g * d:(g + 1) * d], q_ref[:, g * d:(g + 1) * d])
                for g in heads]

    def a_finish(n, slot, z2s, masked):
        for g in heads:
            z2 = z2s[g]
            sp = jnp.maximum(z2, jnp.log2(1.0 + jnp.exp2(jnp.minimum(z2, SB_EXP_CLAMP))))
            if masked:
                kpos = key_start(n) + lax.broadcasted_iota(jnp.int32, z2.shape, 0)
                qpos = i * tq + lax.broadcasted_iota(jnp.int32, z2.shape, 1)
                mask = kpos < qpos
                sp = jnp.where(mask, sp, 0.0)
                z2 = jnp.where(mask, z2, -jnp.inf)
            hl_ref[g, slot] = sp.astype(BF16)
            z_ref[g, slot] = z2

    def b_issue(n, slot):
        return [_dot(neg_tri, hl_ref[g, slot]) for g in heads]

    def b_finish(n, slot, laters):
        for g in heads:
            arg_ref[g, slot] = z_ref[g, slot] + laters[g] + c_ref[g]
            c_ref[g] += laters[g][0:1, :]

    def c_issue(n, slot):
        ks = key_start(n)
        return [_dot(vt_ref[g * d:(g + 1) * d, pl.ds(ks, tk)], jnp.exp2(arg_ref[g, slot]).astype(BF16))
                for g in heads]

    def c_finish(n, slot, pvs):
        for g in heads:
            acc_ref[g] += pvs[g]

    _three_stage(i, (a_issue, a_finish), (b_issue, b_finish), (c_issue, c_finish))
    for g in heads:
        o_ref[:, g * d:(g + 1) * d] = acc_ref[g].T


def _sb_attn(qk, vt, batch, seq, tq=512):
    tk = tq // 2
    nq = seq // tq
    groups = SB_HEADS // SB_GROUP
    gd = SB_GROUP * SB_DIM
    kern = functools.partial(_sb_attn_kernel, tq=tq, tk=tk)
    return pl.pallas_call(
        kern,
        grid=(batch, groups, nq),
        in_specs=[
            pl.BlockSpec((tq, gd), lambda b, h, i: (b * nq + i, h)),
            pl.BlockSpec((seq, gd), lambda b, h, i: (b, groups + h)),
            pl.BlockSpec((gd, seq), lambda b, h, i: (h, b)),
        ],
        out_specs=pl.BlockSpec((tq, gd), lambda b, h, i: (b * nq + i, h)),
        out_shape=jax.ShapeDtypeStruct((batch * seq, SB_OUT), F32),
        scratch_shapes=[
            pltpu.VMEM((SB_GROUP, 2, tk, tq), F32),
            pltpu.VMEM((SB_GROUP, 2, tk, tq), BF16),
            pltpu.VMEM((SB_GROUP, 2, tk, tq), F32),
            pltpu.VMEM((SB_GROUP, SB_DIM, tq), F32),
            pltpu.VMEM((SB_GROUP, 1, tq), F32),
        ],
        compiler_params=_params("parallel", "parallel", "arbitrary"),
        name="sb_attn",
    )(qk, qk, vt)


def _resident(a):
    return pl.BlockSpec(a.shape, lambda i: (0,) * a.ndim, pipeline_mode=pl.Buffered(1))


def _cast_on_first_step(pairs):
    @pl.when(pl.program_id(0) == 0)
    def _():
        for src_ref, dst_ref in pairs:
            dst_ref[...] = src_ref[...].astype(BF16)


def _out_proj_kernel(x_ref, oa_ref, ob_ref, ga_ref, gb_ref, w_ref, y_ref, wb_ref):
    _cast_on_first_step([(w_ref, wb_ref)])
    na = _rms(oa_ref[...], ga_ref[...]).astype(BF16)
    nb = _rms(ob_ref[...], gb_ref[...]).astype(BF16)
    y_ref[...] = x_ref[...] + _dot(na, wb_ref[:MLA_OUT, :]) + _dot(nb, wb_ref[MLA_OUT:, :])


def _out_proj(x, oa, ob, ga, gb, w, tm=512):
    t = x.shape[0]
    full = lambda a: pl.BlockSpec(a.shape, lambda i: (0, 0))
    return pl.pallas_call(
        _out_proj_kernel,
        grid=(t // tm,),
        in_specs=[
            pl.BlockSpec((tm, D_MODEL), lambda i: (i, 0)),
            pl.BlockSpec((tm, MLA_OUT), lambda i: (i, 0)),
            pl.BlockSpec((tm, SB_OUT), lambda i: (i, 0)),
            full(ga), full(gb), _resident(w),
        ],
        out_specs=pl.BlockSpec((tm, D_MODEL), lambda i: (i, 0)),
        out_shape=jax.ShapeDtypeStruct((t, D_MODEL), F32),
        scratch_shapes=[pltpu.VMEM(w.shape, BF16)],
        compiler_params=_params("arbitrary"),
        name="out_proj",
    )(x, oa, ob, ga, gb, w)


def _mem_kv_kernel(mem_ref, g_ref, w_ref, kv_ref):
    kv_ref[...] = _dot(_rms(mem_ref[...], g_ref[...]).astype(BF16), w_ref[...].astype(BF16)).astype(BF16)


def _mem_kv(mem, g, w):
    m = mem.shape[0]
    n = w.shape[1]
    full = lambda a: pl.BlockSpec(a.shape, lambda i: (0, 0))
    return pl.pallas_call(
        _mem_kv_kernel,
        grid=(1,),
        in_specs=[full(mem), full(g), full(w)],
        out_specs=pl.BlockSpec((m, n), lambda i: (0, 0)),
        out_shape=jax.ShapeDtypeStruct((m, n), BF16),
        compiler_params=_params("arbitrary"),
        name="mem_kv",
    )(mem, g, w)


GROUP_LANE0 = N_EXPERTS


def _mem_route_kernel(x_ref, gx_ref, wq_ref, kv_ref, wo_ref, gf_ref, wrhl_ref, br_ref,
                      x2_ref, h2_ref, route_ref, wqb_ref, wob_ref):
    _cast_on_first_step([(wq_ref, wqb_ref), (wo_ref, wob_ref)])
    x1 = x_ref[...]
    q = _dot(_rms(x1, gx_ref[...]).astype(BF16), wqb_ref[...]).astype(BF16)
    scale = MEM_DIM ** -0.5
    kw = MEM_HEADS * MEM_DIM
    heads = []
    for h in range(MEM_HEADS):
        lo = h * MEM_DIM
        s = _dot_nt(q[:, lo:lo + MEM_DIM], kv_ref[:, lo:lo + MEM_DIM]) * scale
        e = jnp.exp(s - jnp.max(s, axis=-1, keepdims=True))
        p = (e / jnp.sum(e, axis=-1, keepdims=True)).astype(BF16)
        heads.append(_dot(p, kv_ref[:, kw + lo:kw + lo + MEM_DIM]).astype(BF16))
    o = jnp.concatenate(heads, axis=-1)
    x2 = x1 + _dot(o, wob_ref[...])
    x2_ref[...] = x2
    h2 = _rms(x2, gf_ref[...])
    h2_ref[...] = h2

    hh, hl = _split_bf16(h2)
    both = _dot(hh, wrhl_ref[...])
    lg = both[:, :LANES] + both[:, LANES:] + _dot(hl, wrhl_ref[:, :LANES]) + br_ref[...]
    lane = lax.broadcasted_iota(jnp.int32, lg.shape, 1)
    big = jnp.int32(1 << 20)
    ninf = -jnp.inf

    def lane_max(v):
        return jnp.max(v, axis=-1, keepdims=True)

    def first_lane(cond):
        return jnp.min(jnp.where(cond, lane, big), axis=-1, keepdims=True)

    is_g = (lane >= GROUP_LANE0) & (lane < GROUP_LANE0 + N_GROUPS)
    g_max = lane_max(jnp.where(is_g, lg, ninf))
    g_sum = jnp.sum(jnp.where(is_g, jnp.exp(lg - g_max), 0.0), axis=-1, keepdims=True)
    p_g = 1.0 / g_sum
    g_idx = first_lane(is_g & (lg == g_max)) - GROUP_LANE0
    in_grp = (lane < N_EXPERTS) & ((lane // EXPERTS_PER_GROUP) == g_idx)
    e_max = lane_max(jnp.where(in_grp, lg, ninf))
    e_sum = jnp.sum(jnp.where(in_grp, jnp.exp(lg - e_max), 0.0), axis=-1, keepdims=True)
    i1 = first_lane(in_grp & (lg == e_max))
    rest = in_grp & (lane != i1)
    e_max2 = lane_max(jnp.where(rest, lg, ninf))
    i2 = first_lane(rest & (lg == e_max2))
    p1 = 1.0 / e_sum
    p2 = jnp.exp(e_max2 - e_max) / e_sum
    den = p1 + p2
    gate1 = p_g * (p1 / den)
    gate2 = p_g * (p2 / den)
    route = jnp.where(lane == 0, i1.astype(F32),
                      jnp.where(lane == 1, i2.astype(F32),
                                jnp.where(lane == 2, gate1,
                                          jnp.where(lane == 3, gate2, 0.0))))
    route_ref[...] = route


def _mem_route(x1, gx, wq, kv, wo, gf, wrhl, br, seq, tm=512):
    t = x1.shape[0]
    per_batch = seq // tm
    full = lambda a: pl.BlockSpec(a.shape, lambda i: (0, 0))
    return pl.pallas_call(
        _mem_route_kernel,
        grid=(t // tm,),
        in_specs=[
            pl.BlockSpec((tm, D_MODEL), lambda i: (i, 0)),
            full(gx), _resident(wq),
            pl.BlockSpec((MEM_LEN, kv.shape[1]), lambda i: (i // per_batch, 0)),
            _resident(wo), full(gf), full(wrhl), full(br),
        ],
        out_specs=[
            pl.BlockSpec((tm, D_MODEL), lambda i: (i, 0)),
            pl.BlockSpec((tm, D_MODEL), lambda i: (i, 0)),
            pl.BlockSpec((tm, LANES), lambda i: (i, 0)),
        ],
        out_shape=[
            jax.ShapeDtypeStruct((t, D_MODEL), F32),
            jax.ShapeDtypeStruct((t, D_MODEL), F32),
            jax.ShapeDtypeStruct((t, LANES), F32),
        ],
        scratch_shapes=[pltpu.VMEM(wq.shape, BF16), pltpu.VMEM(wo.shape, BF16)],
        compiler_params=_params("arbitrary"),
        name="mem_route",
    )(x1, gx, wq, kv, wo, gf, wrhl, br)


SLOT_BLK = 256


def _slots_kernel(route_ref, slot_ref, meta_ref, cum_ref, *, n_tok):
    nblk = n_tok // SLOT_BLK
    e_iota = lax.broadcasted_iota(jnp.int32, (LANES, SLOT_BLK), 0)
    incl = (lax.broadcasted_iota(jnp.int32, (SLOT_BLK, SLOT_BLK), 0)
            <= lax.broadcasted_iota(jnp.int32, (SLOT_BLK, SLOT_BLK), 1)).astype(BF16)

    def onehots(b):
        ts = pl.multiple_of(b * SLOT_BLK, SLOT_BLK)
        ids = route_ref[pl.ds(ts, SLOT_BLK), :].T
        oh1 = e_iota == ids[0:1, :].astype(jnp.int32)
        oh2 = e_iota == ids[1:2, :].astype(jnp.int32)
        return ts, oh1, oh2

    def count(b, carry):
        ts, oh1, oh2 = onehots(b)
        cnt = (oh1 | oh2).astype(F32).astype(BF16)
        c = _dot(cnt, incl) + carry
        cum_ref[:, pl.ds(ts, SLOT_BLK)] = c
        return c[:, SLOT_BLK - 1:SLOT_BLK]

    total = lax.fori_loop(0, nblk, count, jnp.zeros((LANES, 1), F32))
    tiles = jnp.floor((total + (SLOT_TILE - 1)) * (1.0 / SLOT_TILE))
    below = (lax.broadcasted_iota(jnp.int32, (LANES, LANES), 1)
             < lax.broadcasted_iota(jnp.int32, (LANES, LANES), 0)).astype(BF16)
    tile_lo = _dot(below, jnp.broadcast_to(tiles, (LANES, LANES)).astype(BF16))
    base = tile_lo[:, 0:1] * SLOT_TILE

    def assign(b, carry):
        ts, oh1, oh2 = onehots(b)
        pos = base + cum_ref[:, pl.ds(ts, SLOT_BLK)] - 1.0
        s1 = jnp.sum(jnp.where(oh1, pos, 0.0), axis=0, keepdims=True)
        s2 = jnp.sum(jnp.where(oh2, pos, 0.0), axis=0, keepdims=True)
        row = lax.broadcasted_iota(jnp.int32, (8, SLOT_BLK), 0)
        slot_ref[:, pl.ds(ts, SLOT_BLK)] = jnp.where(row == 0, s1, jnp.where(row == 1, s2, 0.0)).astype(jnp.int32)
        return carry

    lax.fori_loop(0, nblk, assign, 0)

    tile_hi = tile_lo + tiles
    tix = lax.broadcasted_iota(jnp.int32, (LANES, LANES), 1).astype(F32)
    is_e = lax.broadcasted_iota(jnp.int32, (LANES, LANES), 0) < N_EXPERTS
    owner = jnp.sum(jnp.where(is_e & (tile_hi <= tix), 1.0, 0.0), axis=0, keepdims=True)
    n_tiles = jnp.max(jnp.where(is_e, tile_hi, 0.0), axis=0, keepdims=True)
    valid = tix[0:1, :] < n_tiles
    last_owner = jnp.max(jnp.where(valid, owner, 0.0), axis=1, keepdims=True)
    owner = jnp.where(valid, owner, last_owner)
    eye = (lax.broadcasted_iota(jnp.int32, (LANES, LANES), 0)
           == lax.broadcasted_iota(jnp.int32, (LANES, LANES), 1))
    to_lanes = lambda colvec: jnp.sum(jnp.where(eye, colvec, 0.0), axis=0, keepdims=True)
    pad_first = to_lanes(base + total)
    pad_end = to_lanes(base + tiles * SLOT_TILE)
    row = lax.broadcasted_iota(jnp.int32, (8, LANES), 0)
    meta = jnp.where(row == 0, owner,
                     jnp.where(row == 1, valid.astype(F32),
                               jnp.where(row == 2, pad_first,
                                         jnp.where(row == 3, pad_end,
                                                   jnp.where(row == 4, n_tiles, 0.0)))))
    meta_ref[...] = meta.astype(jnp.int32)


def _slots(route):
    t = route.shape[0]
    kern = functools.partial(_slots_kernel, n_tok=t)
    return pl.pallas_call(
        kern,
        grid=(1,),
        in_specs=[pl.BlockSpec(route.shape, lambda i: (0, 0))],
        out_specs=[
            pl.BlockSpec((8, t), lambda i: (0, 0)),
            pl.BlockSpec((8, LANES), lambda i: (0, 0)),
        ],
        out_shape=[
            jax.ShapeDtypeStruct((8, t), jnp.int32),
            jax.ShapeDtypeStruct((8, LANES), jnp.int32),
        ],
        scratch_shapes=[pltpu.VMEM((LANES, t), F32)],
        compiler_params=_params("arbitrary"),
        name="slots",
    )(route)


def _row_copy(src_ref, src_row, dst_ref, dst_row, sem):
    return pltpu.make_async_copy(src_ref.at[pl.ds(src_row, 1), :], dst_ref.at[pl.ds(dst_row, 1), :], sem)


def _dispatch_kernel(s1_ref, s2_ref, pad_first_ref, pad_end_ref, n_tiles_ref, h_ref, xs_ref, zero_ref,
                     sem, pad_sem, *, tm):
    i = pl.program_id(0)
    t0 = i * tm
    n_slot_tiles = xs_ref.shape[0] // SLOT_TILE

    def copies(r):
        return (_row_copy(h_ref, r, xs_ref, s1_ref[t0 + r], sem),
                _row_copy(h_ref, r, xs_ref, s2_ref[t0 + r], sem))

    def issue(r, c):
        first, second = copies(r)
        first.start(priority=0)
        second.start(priority=1)
        return c

    lax.fori_loop(0, tm, issue, 0, unroll=8)

    @pl.when(i == pl.num_programs(0) - 1)
    def _():
        zero_ref[...] = jnp.zeros(zero_ref.shape, F32)

        def pad_copy(slot):
            return _row_copy(zero_ref, 0, xs_ref, slot, pad_sem)

        def fill(e, c):
            def one(s, cc):
                pad_copy(s).start()
                return cc

            return lax.fori_loop(pad_first_ref[e], pad_end_ref[e], one, c)

        lax.fori_loop(0, N_EXPERTS, fill, 0)

        def tile_copy(tile):
            return pltpu.make_async_copy(zero_ref, xs_ref.at[pl.ds(tile * SLOT_TILE, SLOT_TILE), :], pad_sem)

        def fill_tile(tile, c):
            tile_copy(tile).start()
            return c

        lax.fori_loop(n_tiles_ref[0], n_slot_tiles, fill_tile, 0)

        def settle(e, c):
            def one(s, cc):
                pad_copy(s).wait()
                return cc

            return lax.fori_loop(pad_first_ref[e], pad_end_ref[e], one, c)

        lax.fori_loop(0, N_EXPERTS, settle, 0)

        def settle_tile(tile, c):
            tile_copy(tile).wait()
            return c

        lax.fori_loop(n_tiles_ref[0], n_slot_tiles, settle_tile, 0)

    def drain(r, c):
        for cp in copies(r):
            cp.wait()
        return c

    lax.fori_loop(0, tm, drain, 0, unroll=8)


def _dispatch(slot1, slot2, pad_first, pad_end, n_tiles, h2, n_slots, tm=256):
    t = h2.shape[0]
    kern = functools.partial(_dispatch_kernel, tm=tm)
    return pl.pallas_call(
        kern,
        grid_spec=pltpu.PrefetchScalarGridSpec(
            num_scalar_prefetch=5,
            grid=(t // tm,),
            in_specs=[pl.BlockSpec((tm, D_MODEL), lambda i, *_: (i, 0))],
            out_specs=pl.BlockSpec(memory_space=pl.ANY),
            scratch_shapes=[
                pltpu.VMEM((SLOT_TILE, D_MODEL), F32),
                pltpu.SemaphoreType.DMA(()),
                pltpu.SemaphoreType.DMA(()),
            ],
        ),
        out_shape=jax.ShapeDtypeStruct((n_slots, D_MODEL), F32),
        compiler_params=_params("arbitrary"),
        name="dispatch",
    )(slot1, slot2, pad_first, pad_end, n_tiles, h2)


def _experts_kernel(own_ref, valid_ref, xs_ref, wg_hbm, wu_hbm, wd_hbm, ys_ref,
                    wgf, wuf, wdf, wgb, wub, wdb, wsem, wslot_ref):
    i = pl.program_id(0)
    nt = pl.num_programs(0)
    at = lambda ref, j: ref[jnp.minimum(j, nt - 1)]

    def weight_copies(e, s):
        return (pltpu.make_async_copy(wg_hbm.at[e], wgf.at[s], wsem.at[s]),
                pltpu.make_async_copy(wu_hbm.at[e], wuf.at[s], wsem.at[s]),
                pltpu.make_async_copy(wd_hbm.at[e], wdf.at[s], wsem.at[s]))

    def next_expert_tile(j0):
        e0 = at(own_ref, j0)
        return lax.while_loop(lambda j: (j < nt) & (at(own_ref, j) == e0), lambda j: j + 1, j0 + 1)

    def start_weights(j, s):
        @pl.when((j < nt) & (at(valid_ref, j) > 0))
        def _():
            for cp in weight_copies(at(own_ref, j), s):
                cp.start()

    @pl.when(i == 0)
    def _():
        wslot_ref[0] = 0
        j = 0
        for s in range(WEIGHT_SLOTS - 1):
            start_weights(j, s)
            j = next_expert_tile(j)

    @pl.when(valid_ref[i] > 0)
    def _():
        e = own_ref[i]

        @pl.when((i == 0) | (own_ref[jnp.maximum(i - 1, 0)] != e))
        def _():
            s = wslot_ref[0]
            for cp in weight_copies(e, s):
                cp.wait()
            j = i
            for _ in range(WEIGHT_SLOTS - 1):
                j = next_expert_tile(j)
            start_weights(j, (s + WEIGHT_SLOTS - 1) % WEIGHT_SLOTS)

            wgb[...] = wgf[s].astype(BF16)
            wub[...] = wuf[s].astype(BF16)
            wdb[...] = wdf[s].astype(BF16)
            wslot_ref[0] = (s + 1) % WEIGHT_SLOTS

        x = xs_ref[...].astype(BF16)
        a = _dot(x, wgb[...])
        u = _dot(x, wub[...])
        act = (a * jax.nn.sigmoid(a) * u).astype(BF16)
        ys_ref[...] = _dot(act, wdb[...])

    @pl.when(valid_ref[i] == 0)
    def _():
        ys_ref[...] = jnp.zeros(ys_ref.shape, F32)


def _experts(own, valid, xs, wg, wu, wd):
    n_tiles = xs.shape[0] // SLOT_TILE
    hbm = pl.BlockSpec(memory_space=pl.ANY)
    tile = (SLOT_TILE, D_MODEL)
    return pl.pallas_call(
        _experts_kernel,
        grid_spec=pltpu.PrefetchScalarGridSpec(
            num_scalar_prefetch=2,
            grid=(n_tiles,),
            in_specs=[
                pl.BlockSpec(tile, lambda i, o, v: (jnp.where(v[i] > 0, i, 0), 0)),
                hbm, hbm, hbm,
            ],
            out_specs=pl.BlockSpec(tile, lambda i, o, v: (i, 0)),
            scratch_shapes=[
                pltpu.VMEM((WEIGHT_SLOTS, D_MODEL, D_EXPERT), F32),
                pltpu.VMEM((WEIGHT_SLOTS, D_MODEL, D_EXPERT), F32),
                pltpu.VMEM((WEIGHT_SLOTS, D_EXPERT, D_MODEL), F32),
                pltpu.VMEM((D_MODEL, D_EXPERT), BF16),
                pltpu.VMEM((D_MODEL, D_EXPERT), BF16),
                pltpu.VMEM((D_EXPERT, D_MODEL), BF16),
                pltpu.SemaphoreType.DMA((WEIGHT_SLOTS,)),
                pltpu.SMEM((1,), jnp.int32),
            ],
        ),
        out_shape=jax.ShapeDtypeStruct(xs.shape, F32),
        compiler_params=_params("arbitrary"),
        name="experts",
    )(own, valid, xs, wg, wu, wd)


def _combine_kernel(s1_ref, s2_ref, x_ref, route_ref, g_ref, ys_ref, y_ref, buf_ref, sem, *, tm):
    i = pl.program_id(0)
    n = pl.num_programs(0)

    def copies(tile, p, r):
        t = tile * tm + r
        return (_row_copy(ys_ref, s1_ref[t], buf_ref.at[p, 0], r, sem.at[p]),
                _row_copy(ys_ref, s2_ref[t], buf_ref.at[p, 1], r, sem.at[p]))

    def gather_start(tile, p):
        def issue(r, c):
            first, second = copies(tile, p, r)
            first.start(priority=0)
            second.start(priority=1)
            return c

        lax.fori_loop(0, tm, issue, 0, unroll=8)

    def gather_wait(tile, p):
        def drain(r, c):
            for cp in copies(tile, p, r):
                cp.wait()
            return c

        lax.fori_loop(0, tm, drain, 0, unroll=8)

    @pl.when(i == 0)
    def _():
        gather_start(0, 0)

    p = i % 2

    @pl.when(i + 1 < n)
    def _():
        gather_start(i + 1, 1 - p)

    gather_wait(i, p)
    y = x_ref[...] + route_ref[:, 2:3] * buf_ref[p, 0] + route_ref[:, 3:4] * buf_ref[p, 1]
    y_ref[...] = _rms(y, g_ref[...])


def _combine(slot1, slot2, x2, route, g, ys, tm=256):
    t = x2.shape[0]
    kern = functools.partial(_combine_kernel, tm=tm)
    return pl.pallas_call(
        kern,
        grid_spec=pltpu.PrefetchScalarGridSpec(
            num_scalar_prefetch=2,
            grid=(t // tm,),
            in_specs=[
                pl.BlockSpec((tm, D_MODEL), lambda i, s1, s2: (i, 0)),
                pl.BlockSpec((tm, LANES), lambda i, s1, s2: (i, 0)),
                pl.BlockSpec((1, D_MODEL), lambda i, s1, s2: (0, 0)),
                pl.BlockSpec(memory_space=pl.ANY),
            ],
            out_specs=pl.BlockSpec((tm, D_MODEL), lambda i, s1, s2: (i, 0)),
            scratch_shapes=[
                pltpu.VMEM((2, 2, tm, D_MODEL), F32),
                pltpu.SemaphoreType.DMA((2,)),
            ],
        ),
        out_shape=jax.ShapeDtypeStruct((t, D_MODEL), F32),
        compiler_params=_params("arbitrary"),
        name="combine",
    )(slot1, slot2, x2, route, g, ys)


def _row(v):
    return v.reshape(1, -1).astype(F32)


def kernel(x, mem, positions, norm_mix, w_in, norm_q_lat, w_q_b, norm_kv_lat, w_kv_b, norm_mla_out, norm_sb_out, w_out, norm_mem_x, norm_mem_src, w_mem_q, w_mem_kv, w_mem_o, norm_ffn, w_group, b_group, w_expert_router, b_expert, w_gate, w_up, w_down, norm_final):
    batch, seq, d = x.shape
    t = batch * seq
    depth = w_in.shape[0]
    assert depth == 1, "single-layer trunk only"
    xt = x.reshape(t, d)
    pos = positions.reshape(t, 1)
    inv_freq = ROPE_THETA ** (-jnp.arange(0, MLA_ROPE, 2, dtype=F32) / MLA_ROPE)
    invf = jnp.concatenate([inv_freq, inv_freq, jnp.zeros((LANES - MLA_ROPE,), F32)]).reshape(1, LANES)
    n_slots = (2 * t // SLOT_TILE + N_EXPERTS) * SLOT_TILE

    for l in range(depth):
        sb0 = MLA_Q_RANK + MLA_KV_RANK + MLA_ROPE
        row_scale = jnp.concatenate([jnp.ones((sb0,), F32), jnp.full((SB_OUT,), SB_DIM ** -0.5 * LOG2E, F32),
                                     jnp.ones((2 * SB_OUT,), F32)])
        w_in_t = (w_in[l].T * row_scale[:, None]).astype(BF16)
        wq = w_q_b[l].reshape(MLA_Q_RANK, MLA_HEADS, MLA_QK) * (MLA_QK ** -0.5 * LOG2E)
        zq = jnp.zeros((MLA_Q_RANK, MLA_HEADS, MLA_PAD - MLA_QK), F32)
        wqa = jnp.concatenate([wq, zq], axis=-1).reshape(MLA_Q_RANK, MLA_HEADS * MLA_PAD).astype(BF16)
        wkv = w_kv_b[l].reshape(MLA_KV_RANK, MLA_HEADS, MLA_NOPE + MLA_V)
        wk = wkv[:, :, :MLA_NOPE].reshape(MLA_KV_RANK, MLA_OUT).astype(BF16)
        wvt = wkv[:, :, MLA_NOPE:].reshape(MLA_KV_RANK, MLA_OUT).T.astype(BF16)
        w_router = jnp.concatenate(
            [w_expert_router[l], w_group[l], jnp.zeros((d, LANES - N_EXPERTS - N_GROUPS), F32)], axis=1)
        wr_hi = w_router.astype(BF16)
        wr_hl = jnp.concatenate([wr_hi, (w_router - wr_hi.astype(F32)).astype(BF16)], axis=1)
        b_router = jnp.concatenate(
            [b_expert[l].astype(F32), b_group[l].astype(F32), jnp.zeros((LANES - N_EXPERTS - N_GROUPS,), F32)]
        ).reshape(1, LANES)

        lat, sb_qk, sb_vt = _proj_in(xt, _row(norm_mix[l]), w_in_t, sb0)
        q, k, vt = _mla_proj(lat, pos, invf, _row(norm_q_lat[l]), _row(norm_kv_lat[l]), wqa, wk, wvt)
        o_mla = _mla_attn(q, k, vt, batch, seq)
        o_sb = _sb_attn(sb_qk, sb_vt, batch, seq)
        x1 = _out_proj(xt, o_mla, o_sb, _row(norm_mla_out[l]), _row(norm_sb_out[l]), w_out[l])

        kv = _mem_kv(mem.reshape(batch * MEM_LEN, d), _row(norm_mem_src[l]), w_mem_kv[l])
        x2, h2, route = _mem_route(x1, _row(norm_mem_x[l]), w_mem_q[l], kv,
                                   w_mem_o[l], _row(norm_ffn[l]), wr_hl, b_router, seq)

        slots, meta = _slots(route)
        xs = _dispatch(slots[0], slots[1], meta[2], meta[3], meta[4], h2, n_slots)
        ys = _experts(meta[0], meta[1], xs, w_gate[l], w_up[l], w_down[l])
        xt = _combine(slots[0], slots[1], x2, route, _row(norm_final), ys)
    return xt.reshape(batch, seq, d)
```

```python
import functools

import jax
import jax.numpy as jnp
from jax import lax
from jax.experimental import pallas as pl
from jax.experimental.pallas import tpu as pltpu

F32 = jnp.float32
BF16 = jnp.bfloat16

EPS = 1e-6
ROPE_THETA = 10000.0

D_MODEL = 2048
MEM_LEN = 256
MLA_HEADS = 8
MLA_NOPE = 128
MLA_ROPE = 64
MLA_QK = MLA_NOPE + MLA_ROPE
MLA_V = 128
MLA_Q_RANK = 512
MLA_KV_RANK = 256
MLA_PAD = 256
SB_HEADS = 8
SB_DIM = 128
MLA_OUT = MLA_HEADS * MLA_V
SB_OUT = SB_HEADS * SB_DIM
MEM_HEADS = 4
MEM_DIM = 128
N_GROUPS = 4
EXPERTS_PER_GROUP = 8
N_EXPERTS = N_GROUPS * EXPERTS_PER_GROUP
D_EXPERT = 512

LANES = 128
LAT_COLS = 1024

SLOT_TILE = 256
WEIGHT_SLOTS = 2
VMEM_LIMIT = 56 * 1024 * 1024


def _rms(x, g):
    return x * lax.rsqrt(jnp.mean(x * x, axis=-1, keepdims=True) + EPS) * g


def _dot(a, b):
    return jnp.dot(a, b, preferred_element_type=F32)


def _dot_nt(a, b):
    return lax.dot_general(a, b, (((1,), (1,)), ((), ())), preferred_element_type=F32)


def _split_bf16(x):
    hi = x.astype(BF16)
    lo = (x - hi.astype(F32)).astype(BF16)
    return hi, lo


def _params(*sem):
    return pltpu.CompilerParams(dimension_semantics=sem, vmem_limit_bytes=VMEM_LIMIT)


def _proj_in_kernel(x_ref, g_ref, wl_ref, ws_ref, lat_ref, qk_ref, vt_ref, h_ref):
    j = pl.program_id(1)
    last = pl.num_programs(1) - 1

    @pl.when(j == 0)
    def _():
        h_ref[...] = _rms(x_ref[...], g_ref[...]).astype(BF16)
        lat_ref[...] = _dot_nt(h_ref[...], wl_ref[...])

    @pl.when((j > 0) & (j < last))
    def _():
        qk_ref[...] = _dot_nt(h_ref[...], ws_ref[...]).astype(BF16)

    @pl.when(j == last)
    def _():
        vt_ref[...] = _dot_nt(ws_ref[...], h_ref[...]).astype(BF16)


def _proj_in(x, g, w_t, sb0, tm=512):
    t = x.shape[0]
    tn = LAT_COLS
    n_sb = (w_t.shape[0] - sb0) // tn
    return pl.pallas_call(
        _proj_in_kernel,
        grid=(t // tm, n_sb + 1),
        in_specs=[
            pl.BlockSpec((tm, D_MODEL), lambda i, j: (i, 0)),
            pl.BlockSpec((1, D_MODEL), lambda i, j: (0, 0)),
            pl.BlockSpec((tn, D_MODEL), lambda i, j: (0, 0)),
            pl.BlockSpec((pl.Element(tn), pl.Element(D_MODEL)),
                         lambda i, j: (pl.multiple_of(sb0 + jnp.maximum(j - 1, 0) * tn, MLA_ROPE), 0)),
        ],
        out_specs=[
            pl.BlockSpec((tm, tn), lambda i, j: (i, 0)),
            pl.BlockSpec((tm, tn), lambda i, j: (i, jnp.clip(j - 1, 0, n_sb - 2))),
            pl.BlockSpec((SB_OUT, tm), lambda i, j: (0, i)),
        ],
        out_shape=[
            jax.ShapeDtypeStruct((t, LAT_COLS), F32),
            jax.ShapeDtypeStruct((t, 2 * SB_OUT), BF16),
            jax.ShapeDtypeStruct((SB_OUT, t), BF16),
        ],
        scratch_shapes=[pltpu.VMEM((tm, D_MODEL), BF16)],
        compiler_params=_params("parallel", "arbitrary"),
        name="proj_in",
    )(x, g, w_t, w_t)


def _mla_proj_kernel(lat_ref, pos_ref, invf_ref, gq_ref, gkv_ref, wqa_ref, wk_ref, wvt_ref,
                     q_ref, k_ref, vt_ref):
    cq = _rms(lat_ref[:, :MLA_Q_RANK], gq_ref[...]).astype(BF16)
    ckv = _rms(lat_ref[:, MLA_Q_RANK:MLA_Q_RANK + MLA_KV_RANK], gkv_ref[...]).astype(BF16)
    ang = pos_ref[...].astype(F32) * invf_ref[...]
    lane = lax.broadcasted_iota(jnp.int32, ang.shape, 1)
    live = lane < MLA_ROPE
    cos2 = jnp.where(live, jnp.cos(ang), 0.0)
    sin2 = jnp.where(live, jnp.sin(ang), 0.0)
    half = MLA_ROPE // 2

    def swapped(x):
        return jnp.where(lane < half, -pltpu.roll(x, LANES - half, 1), pltpu.roll(x, half, 1))

    qa = _dot(cq, wqa_ref[...])
    kn = _dot(ckv, wk_ref[...])
    vt_ref[...] = _dot_nt(wvt_ref[...], ckv).astype(BF16)
    k_raw = lat_ref[:, MLA_Q_RANK + MLA_KV_RANK:MLA_Q_RANK + MLA_KV_RANK + LANES]
    k_pe = jnp.where(live, k_raw * cos2 + swapped(k_raw) * sin2, 0.0).astype(BF16)
    for h in range(MLA_HEADS):
        lo = h * MLA_PAD
        mid = lo + LANES
        q_ref[:, lo:mid] = qa[:, lo:mid].astype(BF16)
        q_pe = qa[:, mid:mid + LANES]
        q_ref[:, mid:mid + LANES] = (q_pe * cos2 + swapped(q_pe) * sin2).astype(BF16)
        k_ref[:, lo:mid] = kn[:, h * LANES:(h + 1) * LANES].astype(BF16)
        k_ref[:, mid:mid + LANES] = k_pe


def _mla_proj(lat, pos, invf, gq, gkv, wqa, wk, wvt, tm=512):
    t = lat.shape[0]
    full = lambda a: pl.BlockSpec(a.shape, lambda i: (0, 0))
    return pl.pallas_call(
        _mla_proj_kernel,
        grid=(t // tm,),
        in_specs=[
            pl.BlockSpec((tm, LAT_COLS), lambda i: (i, 0)),
            pl.BlockSpec((tm, 1), lambda i: (i, 0)),
            full(invf), full(gq), full(gkv), full(wqa), full(wk), full(wvt),
        ],
        out_specs=[
            pl.BlockSpec((tm, MLA_HEADS * MLA_PAD), lambda i: (i, 0)),
            pl.BlockSpec((tm, MLA_HEADS * MLA_PAD), lambda i: (i, 0)),
            pl.BlockSpec((MLA_OUT, tm), lambda i: (0, i)),
        ],
        out_shape=[
            jax.ShapeDtypeStruct((t, MLA_HEADS * MLA_PAD), BF16),
            jax.ShapeDtypeStruct((t, MLA_HEADS * MLA_PAD), BF16),
            jax.ShapeDtypeStruct((MLA_OUT, t), BF16),
        ],
        compiler_params=_params("parallel"),
        name="mla_proj",
    )(lat, pos, invf, gq, gkv, wqa, wk, wvt)


LOG2E = 1.4426950408889634


def _three_stage(n_pairs, stage_a, stage_b, stage_c):
    a_issue, a_finish = stage_a
    b_issue, b_finish = stage_b
    c_issue, c_finish = stage_c

    def run_a(n, slot, first):
        a_finish(n, slot, a_issue(n, slot, first), first)

    run_a(0, 0, True)
    run_a(1, 1, True)
    b_finish(0, 0, b_issue(0, 0))

    def half(na, sa, nb, sb, nc, sc):
        ra = a_issue(na, sa, False)
        rb = b_issue(nb, sb)
        rc = c_issue(nc, sc)
        a_finish(na, sa, ra, False)
        b_finish(nb, sb, rb)
        c_finish(nc, sc, rc)

    def body(p, carry):
        half(2 * p, 0, 2 * p - 1, 1, 2 * p - 2, 0)
        half(2 * p + 1, 1, 2 * p, 0, 2 * p - 1, 1)
        return carry

    lax.fori_loop(1, n_pairs + 1, body, 0)
    last = 2 * n_pairs + 1
    rb = b_issue(last, 1)
    rc = c_issue(last - 1, 0)
    b_finish(last, 1, rb)
    c_finish(last - 1, 0, rc)
    c_finish(last, 1, c_issue(last, 1))


def _two_stage(n_pairs, produce, consume):
    produce(0, 0, True)
    produce(1, 1, True)
    consume(0, 0)

    def body(p, carry):
        produce(2 * p, 0, False)
        consume(2 * p - 1, 1)
        produce(2 * p + 1, 1, False)
        consume(2 * p, 0)
        return carry

    lax.fori_loop(1, n_pairs + 1, body, 0)
    consume(2 * n_pairs + 1, 1)


MLA_SUM_ROWS = 16
MLA_GROUP = 8


def _mla_attn_kernel(q_ref, k_ref, vt_ref, o_ref, s_ref, m_ref, acc_ref, *, tq, tk):
    i = pl.program_id(2)
    heads = range(MLA_GROUP)
    dq, dv = MLA_PAD, MLA_V
    m_ref[...] = jnp.full(m_ref.shape, -jnp.inf, F32)
    acc_ref[...] = jnp.zeros(acc_ref.shape, F32)
    ones = jnp.ones((MLA_SUM_ROWS, tk), BF16)

    def key_start(n):
        tile = jnp.where(n < 2, 2 * i + n, 2 * i + 1 - n)
        return pl.multiple_of(tile * tk, tk)

    def produce(n, slot, diagonal):
        ks = key_start(n)
        for g in heads:
            st = _dot_nt(k_ref[pl.ds(ks, tk), g * dq:(g + 1) * dq], q_ref[:, g * dq:(g + 1) * dq])
            if diagonal:
                kpos = ks + lax.broadcasted_iota(jnp.int32, st.shape, 0)
                qpos = i * tq + lax.broadcasted_iota(jnp.int32, st.shape, 1)
                st = jnp.where(kpos <= qpos, st, -jnp.inf)
            s_ref[g, slot] = st

    def consume(n, slot):
        ks = key_start(n)
        for g in heads:
            st = s_ref[g, slot]
            m_old = m_ref[g]
            m_new = jnp.maximum(m_old, jnp.max(st, axis=0, keepdims=True))
            alpha = jnp.exp2(m_old - m_new)
            p = jnp.exp2(st - m_new).astype(BF16)
            v_ones = jnp.concatenate([vt_ref[g * dv:(g + 1) * dv, pl.ds(ks, tk)], ones], axis=0)
            acc_ref[g] = alpha * acc_ref[g] + _dot(v_ones, p)
            m_ref[g] = m_new

    _two_stage(i, produce, consume)
    for g in heads:
        o_ref[:, g * dv:(g + 1) * dv] = (acc_ref[g, :dv, :] / acc_ref[g, dv:dv + 1, :]).T


def _mla_attn(q, k, vt, batch, seq, tq=512):
    tk = tq // 2
    nq = seq // tq
    groups = MLA_HEADS // MLA_GROUP
    kern = functools.partial(_mla_attn_kernel, tq=tq, tk=tk)
    return pl.pallas_call(
        kern,
        grid=(batch, groups, nq),
        in_specs=[
            pl.BlockSpec((tq, MLA_GROUP * MLA_PAD), lambda b, h, i: (b * nq + i, h)),
            pl.BlockSpec((seq, MLA_GROUP * MLA_PAD), lambda b, h, i: (b, h), pipeline_mode=pl.Buffered(1)),
            pl.BlockSpec((MLA_GROUP * MLA_V, seq), lambda b, h, i: (h, b), pipeline_mode=pl.Buffered(1)),
        ],
        out_specs=pl.BlockSpec((tq, MLA_GROUP * MLA_V), lambda b, h, i: (b * nq + i, h)),
        out_shape=jax.ShapeDtypeStruct((batch * seq, MLA_OUT), F32),
        scratch_shapes=[
            pltpu.VMEM((MLA_GROUP, 2, tk, tq), F32),
            pltpu.VMEM((MLA_GROUP, 1, tq), F32),
            pltpu.VMEM((MLA_GROUP, MLA_V + MLA_SUM_ROWS, tq), F32),
        ],
        compiler_params=_params("parallel", "parallel", "arbitrary"),
        name="mla_attn",
    )(q, k, vt)


SB_EXP_CLAMP = 126.0
SB_GROUP = 4


def _sb_attn_kernel(q_ref, k_ref, vt_ref, o_ref, z_ref, hl_ref, arg_ref, acc_ref, c_ref, *, tq, tk):
    i = pl.program_id(2)
    heads = range(SB_GROUP)
    d = SB_DIM
    acc_ref[...] = jnp.zeros(acc_ref.shape, F32)
    c_ref[...] = jnp.zeros(c_ref.shape, F32)
    col = lax.broadcasted_iota(jnp.int32, (tk, tk), 1)
    row = lax.broadcasted_iota(jnp.int32, (tk, tk), 0)
    neg_tri = jnp.where(col >= row, -1.0, 0.0).astype(BF16)

    def key_start(n):
        return pl.multiple_of((2 * i + 1 - n) * tk, tk)

    def a_issue(n, slot, masked):
        ks = key_start(n)
        return [_dot_nt(k_ref[pl.ds(ks, tk), g * d:(g + 1) * d], q_ref[:, g * d:(g + 1) * d])
                for g in heads]

    def a_finish(n, slot, z2s, masked):
        for g in heads:
            z2 = z2s[g]
            sp = jnp.maximum(z2, jnp.log2(1.0 + jnp.exp2(jnp.minimum(z2, SB_EXP_CLAMP))))
            if masked:
                kpos = key_start(n) + lax.broadcasted_iota(jnp.int32, z2.shape, 0)
                qpos = i * tq + lax.broadcasted_iota(jnp.int32, z2.shape, 1)
                mask = kpos < qpos
                sp = jnp.where(mask, sp, 0.0)
                z2 = jnp.where(mask, z2, -jnp.inf)
            hl_ref[g, slot] = sp.astype(BF16)
            z_ref[g, slot] = z2

    def b_issue(n, slot):
        return [_dot(neg_tri, hl_ref[g, slot]) for g in heads]

    def b_finish(n, slot, laters):
        for g in heads:
            arg_ref[g, slot] = z_ref[g, slot] + laters[g] + c_ref[g]
            c_ref[g] += laters[g][0:1, :]

    def c_issue(n, slot):
        ks = key_start(n)
        return [_dot(vt_ref[g * d:(g + 1) * d, pl.ds(ks, tk)], jnp.exp2(arg_ref[g, slot]).astype(BF16))
                for g in heads]

    def c_finish(n, slot, pvs):
        for g in heads:
            acc_ref[g] += pvs[g]

    _three_stage(i, (a_issue, a_finish), (b_issue, b_finish), (c_issue, c_finish))
    for g in heads:
        o_ref[:, g * d:(g + 1) * d] = acc_ref[g].T


def _sb_attn(qk, vt, batch, seq, tq=512):
    tk = tq // 2
    nq = seq // tq
    groups = SB_HEADS // SB_GROUP
    gd = SB_GROUP * SB_DIM
    kern = functools.partial(_sb_attn_kernel, tq=tq, tk=tk)
    return pl.pallas_call(
        kern,
        grid=(batch, groups, nq),
        in_specs=[
            pl.BlockSpec((tq, gd), lambda b, h, i: (b * nq + i, h)),
            pl.BlockSpec((seq, gd), lambda b, h, i: (b, groups + h)),
            pl.BlockSpec((gd, seq), lambda b, h, i: (h, b)),
        ],
        out_specs=pl.BlockSpec((tq, gd), lambda b, h, i: (b * nq + i, h)),
        out_shape=jax.ShapeDtypeStruct((batch * seq, SB_OUT), F32),
        scratch_shapes=[
            pltpu.VMEM((SB_GROUP, 2, tk, tq), F32),
            pltpu.VMEM((SB_GROUP, 2, tk, tq), BF16),
            pltpu.VMEM((SB_GROUP, 2, tk, tq), F32),
            pltpu.VMEM((SB_GROUP, SB_DIM, tq), F32),
            pltpu.VMEM((SB_GROUP, 1, tq), F32),
        ],
        compiler_params=_params("parallel", "parallel", "arbitrary"),
        name="sb_attn",
    )(qk, qk, vt)


def _resident(a):
    return pl.BlockSpec(a.shape, lambda i: (0,) * a.ndim, pipeline_mode=pl.Buffered(1))


def _cast_on_first_step(pairs):
    @pl.when(pl.program_id(0) == 0)
    def _():
        for src_ref, dst_ref in pairs:
            dst_ref[...] = src_ref[...].astype(BF16)


def _out_proj_kernel(x_ref, oa_ref, ob_ref, ga_ref, gb_ref, w_ref, y_ref, wb_ref):
    _cast_on_first_step([(w_ref, wb_ref)])
    na = _rms(oa_ref[...], ga_ref[...]).astype(BF16)
    nb = _rms(ob_ref[...], gb_ref[...]).astype(BF16)
    y_ref[...] = x_ref[...] + _dot(na, wb_ref[:MLA_OUT, :]) + _dot(nb, wb_ref[MLA_OUT:, :])


def _out_proj(x, oa, ob, ga, gb, w, tm=512):
    t = x.shape[0]
    full = lambda a: pl.BlockSpec(a.shape, lambda i: (0, 0))
    return pl.pallas_call(
        _out_proj_kernel,
        grid=(t // tm,),
        in_specs=[
            pl.BlockSpec((tm, D_MODEL), lambda i: (i, 0)),
            pl.BlockSpec((tm, MLA_OUT), lambda i: (i, 0)),
            pl.BlockSpec((tm, SB_OUT), lambda i: (i, 0)),
            full(ga), full(gb), _resident(w),
        ],
        out_specs=pl.BlockSpec((tm, D_MODEL), lambda i: (i, 0)),
        out_shape=jax.ShapeDtypeStruct((t, D_MODEL), F32),
        scratch_shapes=[pltpu.VMEM(w.shape, BF16)],
        compiler_params=_params("arbitrary"),
        name="out_proj",
    )(x, oa, ob, ga, gb, w)


def _mem_kv_kernel(mem_ref, g_ref, w_ref, kv_ref):
    kv_ref[...] = _dot(_rms(mem_ref[...], g_ref[...]).astype(BF16), w_ref[...].astype(BF16)).astype(BF16)


def _mem_kv(mem, g, w):
    m = mem.shape[0]
    n = w.shape[1]
    full = lambda a: pl.BlockSpec(a.shape, lambda i: (0, 0))
    return pl.pallas_call(
        _mem_kv_kernel,
        grid=(1,),
        in_specs=[full(mem), full(g), full(w)],
        out_specs=pl.BlockSpec((m, n), lambda i: (0, 0)),
        out_shape=jax.ShapeDtypeStruct((m, n), BF16),
        compiler_params=_params("arbitrary"),
        name="mem_kv",
    )(mem, g, w)


GROUP_LANE0 = N_EXPERTS


def _mem_route_kernel(x_ref, gx_ref, wq_ref, kv_ref, wo_ref, gf_ref, wrhl_ref, br_ref,
                      x2_ref, h2_ref, route_ref, wqb_ref, wob_ref):
    _cast_on_first_step([(wq_ref, wqb_ref), (wo_ref, wob_ref)])
    x1 = x_ref[...]
    q = _dot(_rms(x1, gx_ref[...]).astype(BF16), wqb_ref[...]).astype(BF16)
    scale = MEM_DIM ** -0.5
    kw = MEM_HEADS * MEM_DIM
    heads = []
    for h in range(MEM_HEADS):
        lo = h * MEM_DIM
        s = _dot_nt(q[:, lo:lo + MEM_DIM], kv_ref[:, lo:lo + MEM_DIM]) * scale
        e = jnp.exp(s - jnp.max(s, axis=-1, keepdims=True))
        p = (e / jnp.sum(e, axis=-1, keepdims=True)).astype(BF16)
        heads.append(_dot(p, kv_ref[:, kw + lo:kw + lo + MEM_DIM]).astype(BF16))
    o = jnp.concatenate(heads, axis=-1)
    x2 = x1 + _dot(o, wob_ref[...])
    x2_ref[...] = x2
    h2 = _rms(x2, gf_ref[...])
    h2_ref[...] = h2

    hh, hl = _split_bf16(h2)
    both = _dot(hh, wrhl_ref[...])
    lg = both[:, :LANES] + both[:, LANES:] + _dot(hl, wrhl_ref[:, :LANES]) + br_ref[...]
    lane = lax.broadcasted_iota(jnp.int32, lg.shape, 1)
    big = jnp.int32(1 << 20)
    ninf = -jnp.inf

    def lane_max(v):
        return jnp.max(v, axis=-1, keepdims=True)

    def first_lane(cond):
        return jnp.min(jnp.where(cond, lane, big), axis=-1, keepdims=True)

    is_g = (lane >= GROUP_LANE0) & (lane < GROUP_LANE0 + N_GROUPS)
    g_max = lane_max(jnp.where(is_g, lg, ninf))
    g_sum = jnp.sum(jnp.where(is_g, jnp.exp(lg - g_max), 0.0), axis=-1, keepdims=True)
    p_g = 1.0 / g_sum
    g_idx = first_lane(is_g & (lg == g_max)) - GROUP_LANE0
    in_grp = (lane < N_EXPERTS) & ((lane // EXPERTS_PER_GROUP) == g_idx)
    e_max = lane_max(jnp.where(in_grp, lg, ninf))
    e_sum = jnp.sum(jnp.where(in_grp, jnp.exp(lg - e_max), 0.0), axis=-1, keepdims=True)
    i1 = first_lane(in_grp & (lg == e_max))
    rest = in_grp & (lane != i1)
    e_max2 = lane_max(jnp.where(rest, lg, ninf))
    i2 = first_lane(rest & (lg == e_max2))
    p1 = 1.0 / e_sum
    p2 = jnp.exp(e_max2 - e_max) / e_sum
    den = p1 + p2
    gate1 = p_g * (p1 / den)
    gate2 = p_g * (p2 / den)
    route = jnp.where(lane == 0, i1.astype(F32),
                      jnp.where(lane == 1, i2.astype(F32),
                                jnp.where(lane == 2, gate1,
                                          jnp.where(lane == 3, gate2, 0.0))))
    route_ref[...] = route


def _mem_route(x1, gx, wq, kv, wo, gf, wrhl, br, seq, tm=512):
    t = x1.shape[0]
    per_batch = seq // tm
    full = lambda a: pl.BlockSpec(a.shape, lambda i: (0, 0))
    return pl.pallas_call(
        _mem_route_kernel,
        grid=(t // tm,),
        in_specs=[
            pl.BlockSpec((tm, D_MODEL), lambda i: (i, 0)),
            full(gx), _resident(wq),
            pl.BlockSpec((MEM_LEN, kv.shape[1]), lambda i: (i // per_batch, 0)),
            _resident(wo), full(gf), full(wrhl), full(br),
        ],
        out_specs=[
            pl.BlockSpec((tm, D_MODEL), lambda i: (i, 0)),
            pl.BlockSpec((tm, D_MODEL), lambda i: (i, 0)),
            pl.BlockSpec((tm, LANES), lambda i: (i, 0)),
        ],
        out_shape=[
            jax.ShapeDtypeStruct((t, D_MODEL), F32),
            jax.ShapeDtypeStruct((t, D_MODEL), F32),
            jax.ShapeDtypeStruct((t, LANES), F32),
        ],
        scratch_shapes=[pltpu.VMEM(wq.shape, BF16), pltpu.VMEM(wo.shape, BF16)],
        compiler_params=_params("arbitrary"),
        name="mem_route",
    )(x1, gx, wq, kv, wo, gf, wrhl, br)


SLOT_BLK = 256


def _slots_kernel(route_ref, slot_ref, meta_ref, cum_ref, *, n_tok):
    nblk = n_tok // SLOT_BLK
    e_iota = lax.broadcasted_iota(jnp.int32, (LANES, SLOT_BLK), 0)
    incl = (lax.broadcasted_iota(jnp.int32, (SLOT_BLK, SLOT_BLK), 0)
            <= lax.broadcasted_iota(jnp.int32, (SLOT_BLK, SLOT_BLK), 1)).astype(BF16)

    def onehots(b):
        ts = pl.multiple_of(b * SLOT_BLK, SLOT_BLK)
        ids = route_ref[pl.ds(ts, SLOT_BLK), :].T
        oh1 = e_iota == ids[0:1, :].astype(jnp.int32)
        oh2 = e_iota == ids[1:2, :].astype(jnp.int32)
        return ts, oh1, oh2

    def count(b, carry):
        ts, oh1, oh2 = onehots(b)
        cnt = (oh1 | oh2).astype(F32).astype(BF16)
        c = _dot(cnt, incl) + carry
        cum_ref[:, pl.ds(ts, SLOT_BLK)] = c
        return c[:, SLOT_BLK - 1:SLOT_BLK]

    total = lax.fori_loop(0, nblk, count, jnp.zeros((LANES, 1), F32))
    tiles = jnp.floor((total + (SLOT_TILE - 1)) * (1.0 / SLOT_TILE))
    below = (lax.broadcasted_iota(jnp.int32, (LANES, LANES), 1)
             < lax.broadcasted_iota(jnp.int32, (LANES, LANES), 0)).astype(BF16)
    tile_lo = _dot(below, jnp.broadcast_to(tiles, (LANES, LANES)).astype(BF16))
    base = tile_lo[:, 0:1] * SLOT_TILE

    def assign(b, carry):
        ts, oh1, oh2 = onehots(b)
        pos = base + cum_ref[:, pl.ds(ts, SLOT_BLK)] - 1.0
        s1 = jnp.sum(jnp.where(oh1, pos, 0.0), axis=0, keepdims=True)
        s2 = jnp.sum(jnp.where(oh2, pos, 0.0), axis=0, keepdims=True)
        row = lax.broadcasted_iota(jnp.int32, (8, SLOT_BLK), 0)
        slot_ref[:, pl.ds(ts, SLOT_BLK)] = jnp.where(row == 0, s1, jnp.where(row == 1, s2, 0.0)).astype(jnp.int32)
        return carry

    lax.fori_loop(0, nblk, assign, 0)

    tile_hi = tile_lo + tiles
    tix = lax.broadcasted_iota(jnp.int32, (LANES, LANES), 1).astype(F32)
    is_e = lax.broadcasted_iota(jnp.int32, (LANES, LANES), 0) < N_EXPERTS
    owner = jnp.sum(jnp.where(is_e & (tile_hi <= tix), 1.0, 0.0), axis=0, keepdims=True)
    n_tiles = jnp.max(jnp.where(is_e, tile_hi, 0.0), axis=0, keepdims=True)
    valid = tix[0:1, :] < n_tiles
    last_owner = jnp.max(jnp.where(valid, owner, 0.0), axis=1, keepdims=True)
    owner = jnp.where(valid, owner, last_owner)
    eye = (lax.broadcasted_iota(jnp.int32, (LANES, LANES), 0)
           == lax.broadcasted_iota(jnp.int32, (LANES, LANES), 1))
    to_lanes = lambda colvec: jnp.sum(jnp.where(eye, colvec, 0.0), axis=0, keepdims=True)
    pad_first = to_lanes(base + total)
    pad_end = to_lanes(base + tiles * SLOT_TILE)
    row = lax.broadcasted_iota(jnp.int32, (8, LANES), 0)
    meta = jnp.where(row == 0, owner,
                     jnp.where(row == 1, valid.astype(F32),
                               jnp.where(row == 2, pad_first,
                                         jnp.where(row == 3, pad_end,
                                                   jnp.where(row == 4, n_tiles, 0.0)))))
    meta_ref[...] = meta.astype(jnp.int32)


def _slots(route):
    t = route.shape[0]
    kern = functools.partial(_slots_kernel, n_tok=t)
    return pl.pallas_call(
        kern,
        grid=(1,),
        in_specs=[pl.BlockSpec(route.shape, lambda i: (0, 0))],
        out_specs=[
            pl.BlockSpec((8, t), lambda i: (0, 0)),
            pl.BlockSpec((8, LANES), lambda i: (0, 0)),
        ],
        out_shape=[
            jax.ShapeDtypeStruct((8, t), jnp.int32),
            jax.ShapeDtypeStruct((8, LANES), jnp.int32),
        ],
        scratch_shapes=[pltpu.VMEM((LANES, t), F32)],
        compiler_params=_params("arbitrary"),
        name="slots",
    )(route)


def _row_copy(src_ref, src_row, dst_ref, dst_row, sem):
    return pltpu.make_async_copy(src_ref.at[pl.ds(src_row, 1), :], dst_ref.at[pl.ds(dst_row, 1), :], sem)


def _invert_kernel(s1_ref, s2_ref, zeros_hbm, tok_ref, sem, *, n_tok):
    fill = pltpu.make_async_copy(zeros_hbm, tok_ref, sem)
    fill.start()
    fill.wait()

    def put(t, c):
        tok_ref[s1_ref[t]] = t
        tok_ref[s2_ref[t]] = t
        return c

    lax.fori_loop(0, n_tok, put, 0, unroll=8)


def _invert(slot1, slot2, n_slots):
    n_tok = slot1.shape[0]
    kern = functools.partial(_invert_kernel, n_tok=n_tok)
    smem = pl.BlockSpec(memory_space=pltpu.SMEM)
    return pl.pallas_call(
        kern,
        in_specs=[smem, smem, pl.BlockSpec(memory_space=pl.ANY)],
        out_specs=smem,
        out_shape=jax.ShapeDtypeStruct((n_slots,), jnp.int32),
        scratch_shapes=[pltpu.SemaphoreType.DMA(())],
        name="invert",
    )(slot1, slot2, jnp.zeros((n_slots,), jnp.int32))


def _dispatch_kernel(valid_ref, tok_ref, h_hbm, xs_hbm, buf, gsem, wsem):
    i = pl.program_id(0)
    nt = pl.num_programs(0)
    s = i % 2
    at = lambda ref, j: ref[jnp.clip(j, 0, nt - 1)]

    def gather_copy(tile, slot, r):
        return _row_copy(h_hbm, tok_ref[tile * SLOT_TILE + r], buf.at[slot], r, gsem.at[slot])

    def gather_start(tile, slot):
        lax.fori_loop(0, SLOT_TILE, lambda r, c: (gather_copy(tile, slot, r).start(), c)[1], 0, unroll=8)

    def gather_wait(tile, slot):
        lax.fori_loop(0, SLOT_TILE, lambda r, c: (gather_copy(tile, slot, r).wait(), c)[1], 0, unroll=8)

    def write_copy(tile, slot):
        return pltpu.make_async_copy(buf.at[slot], xs_hbm.at[pl.ds(tile * SLOT_TILE, SLOT_TILE), :], wsem.at[slot])

    @pl.when(i == 0)
    def _():
        gather_start(0, 0)

    @pl.when(valid_ref[i] > 0)
    def _():
        gather_wait(i, s)
        write_copy(i, s).start()

        @pl.when(i > 0)
        def _():
            write_copy(i - 1, 1 - s).wait()

        @pl.when((i + 1 < nt) & (at(valid_ref, i + 1) > 0))
        def _():
            gather_start(i + 1, 1 - s)

        @pl.when(i == nt - 1)
        def _():
            write_copy(i, s).wait()

    @pl.when((valid_ref[i] == 0) & (at(valid_ref, i - 1) > 0))
    def _():
        write_copy(i - 1, 1 - s).wait()
        buf[0] = jnp.zeros(buf.shape[1:], F32)
        lax.fori_loop(i, nt, lambda tile, c: (write_copy(tile, 0).start(), c)[1], 0)
        lax.fori_loop(i, nt, lambda tile, c: (write_copy(tile, 0).wait(), c)[1], 0)


def _dispatch(valid, tok, h2):
    n_slots = tok.shape[0]
    hbm = pl.BlockSpec(memory_space=pl.ANY)
    return pl.pallas_call(
        _dispatch_kernel,
        grid_spec=pltpu.PrefetchScalarGridSpec(
            num_scalar_prefetch=2,
            grid=(n_slots // SLOT_TILE,),
            in_specs=[hbm],
            out_specs=hbm,
            scratch_shapes=[
                pltpu.VMEM((2, SLOT_TILE, D_MODEL), F32),
                pltpu.SemaphoreType.DMA((2,)),
                pltpu.SemaphoreType.DMA((2,)),
            ],
        ),
        out_shape=jax.ShapeDtypeStruct((n_slots, D_MODEL), F32),
        compiler_params=_params("arbitrary"),
        name="dispatch",
    )(valid, tok, h2)


def _experts_kernel(own_ref, valid_ref, xs_ref, wg_hbm, wu_hbm, wd_hbm, ys_ref,
                    wgf, wuf, wdf, wgb, wub, wdb, wsem, wslot_ref):
    i = pl.program_id(0)
    nt = pl.num_programs(0)
    at = lambda ref, j: ref[jnp.minimum(j, nt - 1)]

    def weight_copies(e, s):
        return (pltpu.make_async_copy(wg_hbm.at[e], wgf.at[s], wsem.at[s]),
                pltpu.make_async_copy(wu_hbm.at[e], wuf.at[s], wsem.at[s]),
                pltpu.make_async_copy(wd_hbm.at[e], wdf.at[s], wsem.at[s]))

    def next_expert_tile(j0):
        e0 = at(own_ref, j0)
        return lax.while_loop(lambda j: (j < nt) & (at(own_ref, j) == e0), lambda j: j + 1, j0 + 1)

    def start_weights(j, s):
        @pl.when((j < nt) & (at(valid_ref, j) > 0))
        def _():
            for cp in weight_copies(at(own_ref, j), s):
                cp.start()

    @pl.when(i == 0)
    def _():
        wslot_ref[0] = 0
        j = 0
        for s in range(WEIGHT_SLOTS - 1):
            start_weights(j, s)
            j = next_expert_tile(j)

    @pl.when(valid_ref[i] > 0)
    def _():
        e = own_ref[i]

        @pl.when((i == 0) | (own_ref[jnp.maximum(i - 1, 0)] != e))
        def _():
            s = wslot_ref[0]
            for cp in weight_copies(e, s):
                cp.wait()
            j = i
            for _ in range(WEIGHT_SLOTS - 1):
                j = next_expert_tile(j)
            start_weights(j, (s + WEIGHT_SLOTS - 1) % WEIGHT_SLOTS)

            wgb[...] = wgf[s].astype(BF16)
            wub[...] = wuf[s].astype(BF16)
            wdb[...] = wdf[s].astype(BF16)
            wslot_ref[0] = (s + 1) % WEIGHT_SLOTS

        x = xs_ref[...].astype(BF16)
        a = _dot(x, wgb[...])
        u = _dot(x, wub[...])
        act = (a * jax.nn.sigmoid(a) * u).astype(BF16)
        ys_ref[...] = _dot(act, wdb[...])

    @pl.when(valid_ref[i] == 0)
    def _():
        ys_ref[...] = jnp.zeros(ys_ref.shape, F32)


def _experts(own, valid, xs, wg, wu, wd):
    n_tiles = xs.shape[0] // SLOT_TILE
    hbm = pl.BlockSpec(memory_space=pl.ANY)
    tile = (SLOT_TILE, D_MODEL)
    return pl.pallas_call(
        _experts_kernel,
        grid_spec=pltpu.PrefetchScalarGridSpec(
            num_scalar_prefetch=2,
            grid=(n_tiles,),
            in_specs=[
                pl.BlockSpec(tile, lambda i, o, v: (jnp.where(v[i] > 0, i, 0), 0)),
                hbm, hbm, hbm,
            ],
            out_specs=pl.BlockSpec(tile, lambda i, o, v: (i, 0)),
            scratch_shapes=[
                pltpu.VMEM((WEIGHT_SLOTS, D_MODEL, D_EXPERT), F32),
                pltpu.VMEM((WEIGHT_SLOTS, D_MODEL, D_EXPERT), F32),
                pltpu.VMEM((WEIGHT_SLOTS, D_EXPERT, D_MODEL), F32),
                pltpu.VMEM((D_MODEL, D_EXPERT), BF16),
                pltpu.VMEM((D_MODEL, D_EXPERT), BF16),
                pltpu.VMEM((D_EXPERT, D_MODEL), BF16),
                pltpu.SemaphoreType.DMA((WEIGHT_SLOTS,)),
                pltpu.SMEM((1,), jnp.int32),
            ],
        ),
        out_shape=jax.ShapeDtypeStruct(xs.shape, F32),
        compiler_params=_params("arbitrary"),
        name="experts",
    )(own, valid, xs, wg, wu, wd)


def _combine_kernel(s1_ref, s2_ref, x_ref, route_ref, g_ref, ys_ref, y_ref, buf_ref, sem, *, tm):
    i = pl.program_id(0)
    n = pl.num_programs(0)

    def copies(tile, p, r):
        t = tile * tm + r
        return (_row_copy(ys_ref, s1_ref[t], buf_ref.at[p, 0], r, sem.at[p]),
                _row_copy(ys_ref, s2_ref[t], buf_ref.at[p, 1], r, sem.at[p]))

    def gather_start(tile, p):
        def issue(r, c):
            first, second = copies(tile, p, r)
            first.start(priority=0)
            second.start(priority=1)
            return c

        lax.fori_loop(0, tm, issue, 0, unroll=8)

    def gather_wait(tile, p):
        def drain(r, c):
            for cp in copies(tile, p, r):
                cp.wait()
            return c

        lax.fori_loop(0, tm, drain, 0, unroll=8)

    @pl.when(i == 0)
    def _():
        gather_start(0, 0)

    p = i % 2

    @pl.when(i + 1 < n)
    def _():
        gather_start(i + 1, 1 - p)

    gather_wait(i, p)
    y = x_ref[...] + route_ref[:, 2:3] * buf_ref[p, 0] + route_ref[:, 3:4] * buf_ref[p, 1]
    y_ref[...] = _rms(y, g_ref[...])


def _combine(slot1, slot2, x2, route, g, ys, tm=256):
    t = x2.shape[0]
    kern = functools.partial(_combine_kernel, tm=tm)
    return pl.pallas_call(
        kern,
        grid_spec=pltpu.PrefetchScalarGridSpec(
            num_scalar_prefetch=2,
            grid=(t // tm,),
            in_specs=[
                pl.BlockSpec((tm, D_MODEL), lambda i, s1, s2: (i, 0)),
                pl.BlockSpec((tm, LANES), lambda i, s1, s2: (i, 0)),
                pl.BlockSpec((1, D_MODEL), lambda i, s1, s2: (0, 0)),
                pl.BlockSpec(memory_space=pl.ANY),
            ],
            out_specs=pl.BlockSpec((tm, D_MODEL), lambda i, s1, s2: (i, 0)),
            scratch_shapes=[
                pltpu.VMEM((2, 2, tm, D_MODEL), F32),
                pltpu.SemaphoreType.DMA((2,)),
            ],
        ),
        out_shape=jax.ShapeDtypeStruct((t, D_MODEL), F32),
        compiler_params=_params("arbitrary"),
        name="combine",
    )(slot1, slot2, x2, route, g, ys)


def _row(v):
    return v.reshape(1, -1).astype(F32)


def kernel(x, mem, positions, norm_mix, w_in, norm_q_lat, w_q_b, norm_kv_lat, w_kv_b, norm_mla_out, norm_sb_out, w_out, norm_mem_x, norm_mem_src, w_mem_q, w_mem_kv, w_mem_o, norm_ffn, w_group, b_group, w_expert_router, b_expert, w_gate, w_up, w_down, norm_final):
    batch, seq, d = x.shape
    t = batch * seq
    depth = w_in.shape[0]
    assert depth == 1, "single-layer trunk only"
    xt = x.reshape(t, d)
    pos = positions.reshape(t, 1)
    inv_freq = ROPE_THETA ** (-jnp.arange(0, MLA_ROPE, 2, dtype=F32) / MLA_ROPE)
    invf = jnp.concatenate([inv_freq, inv_freq, jnp.zeros((LANES - MLA_ROPE,), F32)]).reshape(1, LANES)
    n_slots = (2 * t // SLOT_TILE + N_EXPERTS) * SLOT_TILE

    for l in range(depth):
        sb0 = MLA_Q_RANK + MLA_KV_RANK + MLA_ROPE
        row_scale = jnp.concatenate([jnp.ones((sb0,), F32), jnp.full((SB_OUT,), SB_DIM ** -0.5 * LOG2E, F32),
                                     jnp.ones((2 * SB_OUT,), F32)])
        w_in_t = (w_in[l].T * row_scale[:, None]).astype(BF16)
        wq = w_q_b[l].reshape(MLA_Q_RANK, MLA_HEADS, MLA_QK) * (MLA_QK ** -0.5 * LOG2E)
        zq = jnp.zeros((MLA_Q_RANK, MLA_HEADS, MLA_PAD - MLA_QK), F32)
        wqa = jnp.concatenate([wq, zq], axis=-1).reshape(MLA_Q_RANK, MLA_HEADS * MLA_PAD).astype(BF16)
        wkv = w_kv_b[l].reshape(MLA_KV_RANK, MLA_HEADS, MLA_NOPE + MLA_V)
        wk = wkv[:, :, :MLA_NOPE].reshape(MLA_KV_RANK, MLA_OUT).astype(BF16)
        wvt = wkv[:, :, MLA_NOPE:].reshape(MLA_KV_RANK, MLA_OUT).T.astype(BF16)
        w_router = jnp.concatenate(
            [w_expert_router[l], w_group[l], jnp.zeros((d, LANES - N_EXPERTS - N_GROUPS), F32)], axis=1)
        wr_hi = w_router.astype(BF16)
        wr_hl = jnp.concatenate([wr_hi, (w_router - wr_hi.astype(F32)).astype(BF16)], axis=1)
        b_router = jnp.concatenate(
            [b_expert[l].astype(F32), b_group[l].astype(F32), jnp.zeros((LANES - N_EXPERTS - N_GROUPS,), F32)]
        ).reshape(1, LANES)

        lat, sb_qk, sb_vt = _proj_in(xt, _row(norm_mix[l]), w_in_t, sb0)
        q, k, vt = _mla_proj(lat, pos, invf, _row(norm_q_lat[l]), _row(norm_kv_lat[l]), wqa, wk, wvt)
        o_mla = _mla_attn(q, k, vt, batch, seq)
        o_sb = _sb_attn(sb_qk, sb_vt, batch, seq)
        x1 = _out_proj(xt, o_mla, o_sb, _row(norm_mla_out[l]), _row(norm_sb_out[l]), w_out[l])

        kv = _mem_kv(mem.reshape(batch * MEM_LEN, d), _row(norm_mem_src[l]), w_mem_kv[l])
        x2, h2, route = _mem_route(x1, _row(norm_mem_x[l]), w_mem_q[l], kv,
                                   w_mem_o[l], _row(norm_ffn[l]), wr_hl, b_router, seq)

        slots, meta = _slots(route)
        xs = _dispatch(meta[1], _invert(slots[0], slots[1], n_slots), h2)
        ys = _experts(meta[0], meta[1], xs, w_gate[l], w_up[l], w_down[l])
        xt = _combine(slots[0], slots[1], x2, route, _row(norm_final), ys)
    return xt.reshape(batch, seq, d)
```

```python
import functools

import jax
import jax.numpy as jnp
from jax import lax
from jax.experimental import pallas as pl
from jax.experimental.pallas import tpu as pltpu

F32 = jnp.float32
BF16 = jnp.bfloat16

EPS = 1e-6
ROPE_THETA = 10000.0

D_MODEL = 2048
MEM_LEN = 256
MLA_HEADS = 8
MLA_NOPE = 128
MLA_ROPE = 64
MLA_QK = MLA_NOPE + MLA_ROPE
MLA_V = 128
MLA_Q_RANK = 512
MLA_KV_RANK = 256
MLA_PAD = 256
SB_HEADS = 8
SB_DIM = 128
MLA_OUT = MLA_HEADS * MLA_V
SB_OUT = SB_HEADS * SB_DIM
MEM_HEADS = 4
MEM_DIM = 128
N_GROUPS = 4
EXPERTS_PER_GROUP = 8
N_EXPERTS = N_GROUPS * EXPERTS_PER_GROUP
D_EXPERT = 512

LANES = 128
LAT_COLS = 1024

SLOT_TILE = 256
WEIGHT_SLOTS = 2
VMEM_LIMIT = 56 * 1024 * 1024


def _rms(x, g):
    return x * lax.rsqrt(jnp.mean(x * x, axis=-1, keepdims=True) + EPS) * g


def _dot(a, b):
    return jnp.dot(a, b, preferred_element_type=F32)


def _dot_nt(a, b):
    return lax.dot_general(a, b, (((1,), (1,)), ((), ())), preferred_element_type=F32)


def _split_bf16(x):
    hi = x.astype(BF16)
    lo = (x - hi.astype(F32)).astype(BF16)
    return hi, lo


def _params(*sem):
    return pltpu.CompilerParams(dimension_semantics=sem, vmem_limit_bytes=VMEM_LIMIT)


def _proj_in_kernel(x_ref, g_ref, wl_ref, ws_ref, lat_ref, qk_ref, vt_ref, h_ref):
    j = pl.program_id(1)
    last = pl.num_programs(1) - 1

    @pl.when(j == 0)
    def _():
        h_ref[...] = _rms(x_ref[...], g_ref[...]).astype(BF16)
        lat_ref[...] = _dot_nt(h_ref[...], wl_ref[...])

    @pl.when((j > 0) & (j < last))
    def _():
        qk_ref[...] = _dot_nt(h_ref[...], ws_ref[...]).astype(BF16)

    @pl.when(j == last)
    def _():
        vt_ref[...] = _dot_nt(ws_ref[...], h_ref[...]).astype(BF16)


def _proj_in(x, g, w_t, sb0, tm=512):
    t = x.shape[0]
    tn = LAT_COLS
    n_sb = (w_t.shape[0] - sb0) // tn
    return pl.pallas_call(
        _proj_in_kernel,
        grid=(t // tm, n_sb + 1),
        in_specs=[
            pl.BlockSpec((tm, D_MODEL), lambda i, j: (i, 0)),
            pl.BlockSpec((1, D_MODEL), lambda i, j: (0, 0)),
            pl.BlockSpec((tn, D_MODEL), lambda i, j: (0, 0)),
            pl.BlockSpec((pl.Element(tn), pl.Element(D_MODEL)),
                         lambda i, j: (pl.multiple_of(sb0 + jnp.maximum(j - 1, 0) * tn, MLA_ROPE), 0)),
        ],
        out_specs=[
            pl.BlockSpec((tm, tn), lambda i, j: (i, 0)),
            pl.BlockSpec((tm, tn), lambda i, j: (i, jnp.clip(j - 1, 0, n_sb - 2))),
            pl.BlockSpec((SB_OUT, tm), lambda i, j: (0, i)),
        ],
        out_shape=[
            jax.ShapeDtypeStruct((t, LAT_COLS), F32),
            jax.ShapeDtypeStruct((t, 2 * SB_OUT), BF16),
            jax.ShapeDtypeStruct((SB_OUT, t), BF16),
        ],
        scratch_shapes=[pltpu.VMEM((tm, D_MODEL), BF16)],
        compiler_params=_params("parallel", "arbitrary"),
        name="proj_in",
    )(x, g, w_t, w_t)


def _mla_proj_kernel(lat_ref, pos_ref, invf_ref, gq_ref, gkv_ref, wqa_ref, wk_ref, wvt_ref,
                     q_ref, k_ref, vt_ref):
    cq = _rms(lat_ref[:, :MLA_Q_RANK], gq_ref[...]).astype(BF16)
    ckv = _rms(lat_ref[:, MLA_Q_RANK:MLA_Q_RANK + MLA_KV_RANK], gkv_ref[...]).astype(BF16)
    ang = pos_ref[...].astype(F32) * invf_ref[...]
    lane = lax.broadcasted_iota(jnp.int32, ang.shape, 1)
    live = lane < MLA_ROPE
    cos2 = jnp.where(live, jnp.cos(ang), 0.0)
    sin2 = jnp.where(live, jnp.sin(ang), 0.0)
    half = MLA_ROPE // 2

    def swapped(x):
        return jnp.where(lane < half, -pltpu.roll(x, LANES - half, 1), pltpu.roll(x, half, 1))

    qa = _dot(cq, wqa_ref[...])
    kn = _dot(ckv, wk_ref[...])
    vt_ref[...] = _dot_nt(wvt_ref[...], ckv).astype(BF16)
    k_raw = lat_ref[:, MLA_Q_RANK + MLA_KV_RANK:MLA_Q_RANK + MLA_KV_RANK + LANES]
    k_pe = jnp.where(live, k_raw * cos2 + swapped(k_raw) * sin2, 0.0).astype(BF16)
    for h in range(MLA_HEADS):
        lo = h * MLA_PAD
        mid = lo + LANES
        q_ref[:, lo:mid] = qa[:, lo:mid].astype(BF16)
        q_pe = qa[:, mid:mid + LANES]
        q_ref[:, mid:mid + LANES] = (q_pe * cos2 + swapped(q_pe) * sin2).astype(BF16)
        k_ref[:, lo:mid] = kn[:, h * LANES:(h + 1) * LANES].astype(BF16)
        k_ref[:, mid:mid + LANES] = k_pe


def _mla_proj(lat, pos, invf, gq, gkv, wqa, wk, wvt, tm=512):
    t = lat.shape[0]
    full = lambda a: pl.BlockSpec(a.shape, lambda i: (0, 0))
    return pl.pallas_call(
        _mla_proj_kernel,
        grid=(t // tm,),
        in_specs=[
            pl.BlockSpec((tm, LAT_COLS), lambda i: (i, 0)),
            pl.BlockSpec((tm, 1), lambda i: (i, 0)),
            full(invf), full(gq), full(gkv), full(wqa), full(wk), full(wvt),
        ],
        out_specs=[
            pl.BlockSpec((tm, MLA_HEADS * MLA_PAD), lambda i: (i, 0)),
            pl.BlockSpec((tm, MLA_HEADS * MLA_PAD), lambda i: (i, 0)),
            pl.BlockSpec((MLA_OUT, tm), lambda i: (0, i)),
        ],
        out_shape=[
            jax.ShapeDtypeStruct((t, MLA_HEADS * MLA_PAD), BF16),
            jax.ShapeDtypeStruct((t, MLA_HEADS * MLA_PAD), BF16),
            jax.ShapeDtypeStruct((MLA_OUT, t), BF16),
        ],
        compiler_params=_params("parallel"),
        name="mla_proj",
    )(lat, pos, invf, gq, gkv, wqa, wk, wvt)


LOG2E = 1.4426950408889634


def _three_stage(n_pairs, stage_a, stage_b, stage_c):
    a_issue, a_finish = stage_a
    b_issue, b_finish = stage_b
    c_issue, c_finish = stage_c

    def run_a(n, slot, first):
        a_finish(n, slot, a_issue(n, slot, first), first)

    run_a(0, 0, True)
    run_a(1, 1, True)
    b_finish(0, 0, b_issue(0, 0))

    def half(na, sa, nb, sb, nc, sc):
        ra = a_issue(na, sa, False)
        rb = b_issue(nb, sb)
        rc = c_issue(nc, sc)
        a_finish(na, sa, ra, False)
        b_finish(nb, sb, rb)
        c_finish(nc, sc, rc)

    def body(p, carry):
        half(2 * p, 0, 2 * p - 1, 1, 2 * p - 2, 0)
        half(2 * p + 1, 1, 2 * p, 0, 2 * p - 1, 1)
        return carry

    lax.fori_loop(1, n_pairs + 1, body, 0)
    last = 2 * n_pairs + 1
    rb = b_issue(last, 1)
    rc = c_issue(last - 1, 0)
    b_finish(last, 1, rb)
    c_finish(last - 1, 0, rc)
    c_finish(last, 1, c_issue(last, 1))


def _two_stage(n_pairs, produce, consume):
    produce(0, 0, True)
    produce(1, 1, True)
    consume(0, 0)

    def body(p, carry):
        produce(2 * p, 0, False)
        consume(2 * p - 1, 1)
        produce(2 * p + 1, 1, False)
        consume(2 * p, 0)
        return carry

    lax.fori_loop(1, n_pairs + 1, body, 0)
    consume(2 * n_pairs + 1, 1)


MLA_SUM_ROWS = 16
MLA_GROUP = 8


def _mla_attn_kernel(q_ref, k_ref, vt_ref, o_ref, s_ref, m_ref, acc_ref, *, tq, tk):
    i = pl.program_id(2)
    heads = range(MLA_GROUP)
    dq, dv = MLA_PAD, MLA_V
    m_ref[...] = jnp.full(m_ref.shape, -jnp.inf, F32)
    acc_ref[...] = jnp.zeros(acc_ref.shape, F32)
    ones = jnp.ones((MLA_SUM_ROWS, tk), BF16)

    def key_start(n):
        tile = jnp.where(n < 2, 2 * i + n, 2 * i + 1 - n)
        return pl.multiple_of(tile * tk, tk)

    def produce(n, slot, diagonal):
        ks = key_start(n)
        for g in heads:
            st = _dot_nt(k_ref[pl.ds(ks, tk), g * dq:(g + 1) * dq], q_ref[:, g * dq:(g + 1) * dq])
            if diagonal:
                kpos = ks + lax.broadcasted_iota(jnp.int32, st.shape, 0)
                qpos = i * tq + lax.broadcasted_iota(jnp.int32, st.shape, 1)
                st = jnp.where(kpos <= qpos, st, -jnp.inf)
            s_ref[g, slot] = st

    def consume(n, slot):
        ks = key_start(n)
        for g in heads:
            st = s_ref[g, slot]
            m_old = m_ref[g]
            m_new = jnp.maximum(m_old, jnp.max(st, axis=0, keepdims=True))
            alpha = jnp.exp2(m_old - m_new)
            p = jnp.exp2(st - m_new).astype(BF16)
            v_ones = jnp.concatenate([vt_ref[g * dv:(g + 1) * dv, pl.ds(ks, tk)], ones], axis=0)
            acc_ref[g] = alpha * acc_ref[g] + _dot(v_ones, p)
            m_ref[g] = m_new

    _two_stage(i, produce, consume)
    for g in heads:
        o_ref[:, g * dv:(g + 1) * dv] = (acc_ref[g, :dv, :] / acc_ref[g, dv:dv + 1, :]).T


def _mla_attn(q, k, vt, batch, seq, tq=512):
    tk = tq // 2
    nq = seq // tq
    groups = MLA_HEADS // MLA_GROUP
    kern = functools.partial(_mla_attn_kernel, tq=tq, tk=tk)
    return pl.pallas_call(
        kern,
        grid=(batch, groups, nq),
        in_specs=[
            pl.BlockSpec((tq, MLA_GROUP * MLA_PAD), lambda b, h, i: (b * nq + i, h)),
            pl.BlockSpec((seq, MLA_GROUP * MLA_PAD), lambda b, h, i: (b, h), pipeline_mode=pl.Buffered(1)),
            pl.BlockSpec((MLA_GROUP * MLA_V, seq), lambda b, h, i: (h, b), pipeline_mode=pl.Buffered(1)),
        ],
        out_specs=pl.BlockSpec((tq, MLA_GROUP * MLA_V), lambda b, h, i: (b * nq + i, h)),
        out_shape=jax.ShapeDtypeStruct((batch * seq, MLA_OUT), F32),
        scratch_shapes=[
            pltpu.VMEM((MLA_GROUP, 2, tk, tq), F32),
            pltpu.VMEM((MLA_GROUP, 1, tq), F32),
            pltpu.VMEM((MLA_GROUP, MLA_V + MLA_SUM_ROWS, tq), F32),
        ],
        compiler_params=_params("parallel", "parallel", "arbitrary"),
        name="mla_attn",
    )(q, k, vt)


SB_EXP_CLAMP = 126.0
SB_GROUP = 4


def _sb_attn_kernel(q_ref, k_ref, vt_ref, o_ref, z_ref, hl_ref, arg_ref, acc_ref, c_ref, *, tq, tk):
    i = pl.program_id(2)
    heads = range(SB_GROUP)
    d = SB_DIM
    acc_ref[...] = jnp.zeros(acc_ref.shape, F32)
    c_ref[...] = jnp.zeros(c_ref.shape, F32)
    col = lax.broadcasted_iota(jnp.int32, (tk, tk), 1)
    row = lax.broadcasted_iota(jnp.int32, (tk, tk), 0)
    neg_tri = jnp.where(col >= row, -1.0, 0.0).astype(BF16)

    def key_start(n):
        return pl.multiple_of((2 * i + 1 - n) * tk, tk)

    def a_issue(n, slot, masked):
        ks = key_start(n)
        return [_dot_nt(k_ref[pl.ds(ks, tk), g * d:(g + 1) * d], q_ref[:, g * d:(g + 1) * d])
                for g in heads]

    def a_finish(n, slot, z2s, masked):
        for g in heads:
            z2 = z2s[g]
            sp = jnp.maximum(z2, jnp.log2(1.0 + jnp.exp2(jnp.minimum(z2, SB_EXP_CLAMP))))
            if masked:
                kpos = key_start(n) + lax.broadcasted_iota(jnp.int32, z2.shape, 0)
                qpos = i * tq + lax.broadcasted_iota(jnp.int32, z2.shape, 1)
                mask = kpos < qpos
                sp = jnp.where(mask, sp, 0.0)
                z2 = jnp.where(mask, z2, -jnp.inf)
            hl_ref[g, slot] = sp.astype(BF16)
            z_ref[g, slot] = z2

    def b_issue(n, slot):
        return [_dot(neg_tri, hl_ref[g, slot]) for g in heads]

    def b_finish(n, slot, laters):
        for g in heads:
            arg_ref[g, slot] = z_ref[g, slot] + laters[g] + c_ref[g]
            c_ref[g] += laters[g][0:1, :]

    def c_issue(n, slot):
        ks = key_start(n)
        return [_dot(vt_ref[g * d:(g + 1) * d, pl.ds(ks, tk)], jnp.exp2(arg_ref[g, slot]).astype(BF16))
                for g in heads]

    def c_finish(n, slot, pvs):
        for g in heads:
            acc_ref[g] += pvs[g]

    _three_stage(i, (a_issue, a_finish), (b_issue, b_finish), (c_issue, c_finish))
    for g in heads:
        o_ref[:, g * d:(g + 1) * d] = acc_ref[g].T


def _sb_attn(qk, vt, batch, seq, tq=512):
    tk = tq // 2
    nq = seq // tq
    groups = SB_HEADS // SB_GROUP
    gd = SB_GROUP * SB_DIM
    kern = functools.partial(_sb_attn_kernel, tq=tq, tk=tk)
    return pl.pallas_call(
        kern,
        grid=(batch, groups, nq),
        in_specs=[
            pl.BlockSpec((tq, gd), lambda b, h, i: (b * nq + i, h)),
            pl.BlockSpec((seq, gd), lambda b, h, i: (b, groups + h)),
            pl.BlockSpec((gd, seq), lambda b, h, i: (h, b)),
        ],
        out_specs=pl.BlockSpec((tq, gd), lambda b, h, i: (b * nq + i, h)),
        out_shape=jax.ShapeDtypeStruct((batch * seq, SB_OUT), F32),
        scratch_shapes=[
            pltpu.VMEM((SB_GROUP, 2, tk, tq), F32),
            pltpu.VMEM((SB_GROUP, 2, tk, tq), BF16),
            pltpu.VMEM((SB_GROUP, 2, tk, tq), F32),
            pltpu.VMEM((SB_GROUP, SB_DIM, tq), F32),
            pltpu.VMEM((SB_GROUP, 1, tq), F32),
        ],
        compiler_params=_params("parallel", "parallel", "arbitrary"),
        name="sb_attn",
    )(qk, qk, vt)


def _resident(a):
    return pl.BlockSpec(a.shape, lambda i: (0,) * a.ndim, pipeline_mode=pl.Buffered(1))


def _cast_on_first_step(pairs):
    @pl.when(pl.program_id(0) == 0)
    def _():
        for src_ref, dst_ref in pairs:
            dst_ref[...] = src_ref[...].astype(BF16)


def _out_proj_kernel(x_ref, oa_ref, ob_ref, ga_ref, gb_ref, w_ref, y_ref, wb_ref):
    _cast_on_first_step([(w_ref, wb_ref)])
    na = _rms(oa_ref[...], ga_ref[...]).astype(BF16)
    nb = _rms(ob_ref[...], gb_ref[...]).astype(BF16)
    y_ref[...] = x_ref[...] + _dot(na, wb_ref[:MLA_OUT, :]) + _dot(nb, wb_ref[MLA_OUT:, :])


def _out_proj(x, oa, ob, ga, gb, w, tm=512):
    t = x.shape[0]
    full = lambda a: pl.BlockSpec(a.shape, lambda i: (0, 0))
    return pl.pallas_call(
        _out_proj_kernel,
        grid=(t // tm,),
        in_specs=[
            pl.BlockSpec((tm, D_MODEL), lambda i: (i, 0)),
            pl.BlockSpec((tm, MLA_OUT), lambda i: (i, 0)),
            pl.BlockSpec((tm, SB_OUT), lambda i: (i, 0)),
            full(ga), full(gb), _resident(w),
        ],
        out_specs=pl.BlockSpec((tm, D_MODEL), lambda i: (i, 0)),
        out_shape=jax.ShapeDtypeStruct((t, D_MODEL), F32),
        scratch_shapes=[pltpu.VMEM(w.shape, BF16)],
        compiler_params=_params("arbitrary"),
        name="out_proj",
    )(x, oa, ob, ga, gb, w)


def _mem_kv_kernel(mem_ref, g_ref, w_ref, kv_ref):
    kv_ref[...] = _dot(_rms(mem_ref[...], g_ref[...]).astype(BF16), w_ref[...].astype(BF16)).astype(BF16)


def _mem_kv(mem, g, w):
    m = mem.shape[0]
    n = w.shape[1]
    full = lambda a: pl.BlockSpec(a.shape, lambda i: (0, 0))
    return pl.pallas_call(
        _mem_kv_kernel,
        grid=(1,),
        in_specs=[full(mem), full(g), full(w)],
        out_specs=pl.BlockSpec((m, n), lambda i: (0, 0)),
        out_shape=jax.ShapeDtypeStruct((m, n), BF16),
        compiler_params=_params("arbitrary"),
        name="mem_kv",
    )(mem, g, w)


GROUP_LANE0 = N_EXPERTS


def _mem_route_kernel(x_ref, gx_ref, wq_ref, kv_ref, wo_ref, gf_ref, wrhl_ref, br_ref,
                      x2_ref, h2_ref, route_ref, wqb_ref, wob_ref):
    _cast_on_first_step([(wq_ref, wqb_ref), (wo_ref, wob_ref)])
    x1 = x_ref[...]
    q = _dot(_rms(x1, gx_ref[...]).astype(BF16), wqb_ref[...]).astype(BF16)
    scale = MEM_DIM ** -0.5
    kw = MEM_HEADS * MEM_DIM
    heads = []
    for h in range(MEM_HEADS):
        lo = h * MEM_DIM
        s = _dot_nt(q[:, lo:lo + MEM_DIM], kv_ref[:, lo:lo + MEM_DIM]) * scale
        e = jnp.exp(s - jnp.max(s, axis=-1, keepdims=True))
        p = (e / jnp.sum(e, axis=-1, keepdims=True)).astype(BF16)
        heads.append(_dot(p, kv_ref[:, kw + lo:kw + lo + MEM_DIM]).astype(BF16))
    o = jnp.concatenate(heads, axis=-1)
    x2 = x1 + _dot(o, wob_ref[...])
    x2_ref[...] = x2
    h2 = _rms(x2, gf_ref[...])
    h2_ref[...] = h2

    hh, hl = _split_bf16(h2)
    both = _dot(hh, wrhl_ref[...])
    lg = both[:, :LANES] + both[:, LANES:] + _dot(hl, wrhl_ref[:, :LANES]) + br_ref[...]
    lane = lax.broadcasted_iota(jnp.int32, lg.shape, 1)
    big = jnp.int32(1 << 20)
    ninf = -jnp.inf

    def lane_max(v):
        return jnp.max(v, axis=-1, keepdims=True)

    def first_lane(cond):
        return jnp.min(jnp.where(cond, lane, big), axis=-1, keepdims=True)

    is_g = (lane >= GROUP_LANE0) & (lane < GROUP_LANE0 + N_GROUPS)
    g_max = lane_max(jnp.where(is_g, lg, ninf))
    g_sum = jnp.sum(jnp.where(is_g, jnp.exp(lg - g_max), 0.0), axis=-1, keepdims=True)
    p_g = 1.0 / g_sum
    g_idx = first_lane(is_g & (lg == g_max)) - GROUP_LANE0
    in_grp = (lane < N_EXPERTS) & ((lane // EXPERTS_PER_GROUP) == g_idx)
    e_max = lane_max(jnp.where(in_grp, lg, ninf))
    e_sum = jnp.sum(jnp.where(in_grp, jnp.exp(lg - e_max), 0.0), axis=-1, keepdims=True)
    i1 = first_lane(in_grp & (lg == e_max))
    rest = in_grp & (lane != i1)
    e_max2 = lane_max(jnp.where(rest, lg, ninf))
    i2 = first_lane(rest & (lg == e_max2))
    p1 = 1.0 / e_sum
    p2 = jnp.exp(e_max2 - e_max) / e_sum
    den = p1 + p2
    gate1 = p_g * (p1 / den)
    gate2 = p_g * (p2 / den)
    route = jnp.where(lane == 0, i1.astype(F32),
                      jnp.where(lane == 1, i2.astype(F32),
                                jnp.where(lane == 2, gate1,
                                          jnp.where(lane == 3, gate2, 0.0))))
    route_ref[...] = route


def _mem_route(x1, gx, wq, kv, wo, gf, wrhl, br, seq, tm=512):
    t = x1.shape[0]
    per_batch = seq // tm
    full = lambda a: pl.BlockSpec(a.shape, lambda i: (0, 0))
    return pl.pallas_call(
        _mem_route_kernel,
        grid=(t // tm,),
        in_specs=[
            pl.BlockSpec((tm, D_MODEL), lambda i: (i, 0)),
            full(gx), _resident(wq),
            pl.BlockSpec((MEM_LEN, kv.shape[1]), lambda i: (i // per_batch, 0)),
            _resident(wo), full(gf), full(wrhl), full(br),
        ],
        out_specs=[
            pl.BlockSpec((tm, D_MODEL), lambda i: (i, 0)),
            pl.BlockSpec((tm, D_MODEL), lambda i: (i, 0)),
            pl.BlockSpec((tm, LANES), lambda i: (i, 0)),
        ],
        out_shape=[
            jax.ShapeDtypeStruct((t, D_MODEL), F32),
            jax.ShapeDtypeStruct((t, D_MODEL), F32),
            jax.ShapeDtypeStruct((t, LANES), F32),
        ],
        scratch_shapes=[pltpu.VMEM(wq.shape, BF16), pltpu.VMEM(wo.shape, BF16)],
        compiler_params=_params("arbitrary"),
        name="mem_route",
    )(x1, gx, wq, kv, wo, gf, wrhl, br)


SLOT_BLK = 256


def _slots_kernel(route_ref, slot_ref, meta_ref, cum_ref, *, n_tok):
    nblk = n_tok // SLOT_BLK
    e_iota = lax.broadcasted_iota(jnp.int32, (LANES, SLOT_BLK), 0)
    incl = (lax.broadcasted_iota(jnp.int32, (SLOT_BLK, SLOT_BLK), 0)
            <= lax.broadcasted_iota(jnp.int32, (SLOT_BLK, SLOT_BLK), 1)).astype(BF16)

    def onehots(b):
        ts = pl.multiple_of(b * SLOT_BLK, SLOT_BLK)
        ids = route_ref[pl.ds(ts, SLOT_BLK), :].T
        oh1 = e_iota == ids[0:1, :].astype(jnp.int32)
        oh2 = e_iota == ids[1:2, :].astype(jnp.int32)
        return ts, oh1, oh2

    def count(b, carry):
        ts, oh1, oh2 = onehots(b)
        cnt = (oh1 | oh2).astype(F32).astype(BF16)
        c = _dot(cnt, incl) + carry
        cum_ref[:, pl.ds(ts, SLOT_BLK)] = c
        return c[:, SLOT_BLK - 1:SLOT_BLK]

    total = lax.fori_loop(0, nblk, count, jnp.zeros((LANES, 1), F32))
    tiles = jnp.floor((total + (SLOT_TILE - 1)) * (1.0 / SLOT_TILE))
    below = (lax.broadcasted_iota(jnp.int32, (LANES, LANES), 1)
             < lax.broadcasted_iota(jnp.int32, (LANES, LANES), 0)).astype(BF16)
    tile_lo = _dot(below, jnp.broadcast_to(tiles, (LANES, LANES)).astype(BF16))
    base = tile_lo[:, 0:1] * SLOT_TILE

    def assign(b, carry):
        ts, oh1, oh2 = onehots(b)
        pos = base + cum_ref[:, pl.ds(ts, SLOT_BLK)] - 1.0
        s1 = jnp.sum(jnp.where(oh1, pos, 0.0), axis=0, keepdims=True)
        s2 = jnp.sum(jnp.where(oh2, pos, 0.0), axis=0, keepdims=True)
        row = lax.broadcasted_iota(jnp.int32, (8, SLOT_BLK), 0)
        slot_ref[:, pl.ds(ts, SLOT_BLK)] = jnp.where(row == 0, s1, jnp.where(row == 1, s2, 0.0)).astype(jnp.int32)
        return carry

    lax.fori_loop(0, nblk, assign, 0)

    tile_hi = tile_lo + tiles
    tix = lax.broadcasted_iota(jnp.int32, (LANES, LANES), 1).astype(F32)
    is_e = lax.broadcasted_iota(jnp.int32, (LANES, LANES), 0) < N_EXPERTS
    owner = jnp.sum(jnp.where(is_e & (tile_hi <= tix), 1.0, 0.0), axis=0, keepdims=True)
    n_tiles = jnp.max(jnp.where(is_e, tile_hi, 0.0), axis=0, keepdims=True)
    valid = tix[0:1, :] < n_tiles
    last_owner = jnp.max(jnp.where(valid, owner, 0.0), axis=1, keepdims=True)
    owner = jnp.where(valid, owner, last_owner)
    eye = (lax.broadcasted_iota(jnp.int32, (LANES, LANES), 0)
           == lax.broadcasted_iota(jnp.int32, (LANES, LANES), 1))
    to_lanes = lambda colvec: jnp.sum(jnp.where(eye, colvec, 0.0), axis=0, keepdims=True)
    pad_first = to_lanes(base + total)
    pad_end = to_lanes(base + tiles * SLOT_TILE)
    row = lax.broadcasted_iota(jnp.int32, (8, LANES), 0)
    meta = jnp.where(row == 0, owner,
                     jnp.where(row == 1, valid.astype(F32),
                               jnp.where(row == 2, pad_first,
                                         jnp.where(row == 3, pad_end,
                                                   jnp.where(row == 4, n_tiles, 0.0)))))
    meta_ref[...] = meta.astype(jnp.int32)


def _slots(route):
    t = route.shape[0]
    kern = functools.partial(_slots_kernel, n_tok=t)
    return pl.pallas_call(
        kern,
        grid=(1,),
        in_specs=[pl.BlockSpec(route.shape, lambda i: (0, 0))],
        out_specs=[
            pl.BlockSpec((8, t), lambda i: (0, 0)),
            pl.BlockSpec((8, LANES), lambda i: (0, 0)),
        ],
        out_shape=[
            jax.ShapeDtypeStruct((8, t), jnp.int32),
            jax.ShapeDtypeStruct((8, LANES), jnp.int32),
        ],
        scratch_shapes=[pltpu.VMEM((LANES, t), F32)],
        compiler_params=_params("arbitrary"),
        name="slots",
    )(route)


def _row_copy(src_ref, src_row, dst_ref, dst_row, sem):
    return pltpu.make_async_copy(src_ref.at[pl.ds(src_row, 1), :], dst_ref.at[pl.ds(dst_row, 1), :], sem)


def _dispatch_kernel(s1_ref, s2_ref, pad_first_ref, pad_end_ref, n_tiles_ref, h_ref, xs_ref, zero_ref,
                     sem, pad_sem, *, tm):
    i = pl.program_id(0)
    t0 = i * tm
    n_slot_tiles = xs_ref.shape[0] // SLOT_TILE

    def copies(r):
        return (_row_copy(h_ref, r, xs_ref, s1_ref[t0 + r], sem),
                _row_copy(h_ref, r, xs_ref, s2_ref[t0 + r], sem))

    def issue(r, c):
        first, second = copies(r)
        first.start(priority=0)
        second.start(priority=1)
        return c

    lax.fori_loop(0, tm, issue, 0, unroll=8)

    @pl.when(i == pl.num_programs(0) - 1)
    def _():
        zero_ref[...] = jnp.zeros(zero_ref.shape, F32)

        def pad_copy(slot):
            return _row_copy(zero_ref, 0, xs_ref, slot, pad_sem)

        def fill(e, c):
            def one(s, cc):
                pad_copy(s).start()
                return cc

            return lax.fori_loop(pad_first_ref[e], pad_end_ref[e], one, c)

        lax.fori_loop(0, N_EXPERTS, fill, 0)

        def tile_copy(tile):
            return pltpu.make_async_copy(zero_ref, xs_ref.at[pl.ds(tile * SLOT_TILE, SLOT_TILE), :], pad_sem)

        def fill_tile(tile, c):
            tile_copy(tile).start()
            return c

        lax.fori_loop(n_tiles_ref[0], n_slot_tiles, fill_tile, 0)

        def settle(e, c):
            def one(s, cc):
                pad_copy(s).wait()
                return cc

            return lax.fori_loop(pad_first_ref[e], pad_end_ref[e], one, c)

        lax.fori_loop(0, N_EXPERTS, settle, 0)

        def settle_tile(tile, c):
            tile_copy(tile).wait()
            return c

        lax.fori_loop(n_tiles_ref[0], n_slot_tiles, settle_tile, 0)

    def drain(r, c):
        for cp in copies(r):
            cp.wait()
        return c

    lax.fori_loop(0, tm, drain, 0, unroll=8)


def _dispatch(slot1, slot2, pad_first, pad_end, n_tiles, h2, n_slots, tm=1024):
    t = h2.shape[0]
    kern = functools.partial(_dispatch_kernel, tm=tm)
    return pl.pallas_call(
        kern,
        grid_spec=pltpu.PrefetchScalarGridSpec(
            num_scalar_prefetch=5,
            grid=(t // tm,),
            in_specs=[pl.BlockSpec((tm, D_MODEL), lambda i, *_: (i, 0))],
            out_specs=pl.BlockSpec(memory_space=pl.ANY),
            scratch_shapes=[
                pltpu.VMEM((SLOT_TILE, D_MODEL), F32),
                pltpu.SemaphoreType.DMA(()),
                pltpu.SemaphoreType.DMA(()),
            ],
        ),
        out_shape=jax.ShapeDtypeStruct((n_slots, D_MODEL), F32),
        compiler_params=_params("arbitrary"),
        name="dispatch",
    )(slot1, slot2, pad_first, pad_end, n_tiles, h2)


def _experts_kernel(own_ref, valid_ref, xs_ref, wg_hbm, wu_hbm, wd_hbm, ys_ref,
                    wgf, wuf, wdf, wgb, wub, wdb, wsem, wslot_ref):
    i = pl.program_id(0)
    nt = pl.num_programs(0)
    at = lambda ref, j: ref[jnp.minimum(j, nt - 1)]

    def weight_copies(e, s):
        return (pltpu.make_async_copy(wg_hbm.at[e], wgf.at[s], wsem.at[s]),
                pltpu.make_async_copy(wu_hbm.at[e], wuf.at[s], wsem.at[s]),
                pltpu.make_async_copy(wd_hbm.at[e], wdf.at[s], wsem.at[s]))

    def next_expert_tile(j0):
        e0 = at(own_ref, j0)
        return lax.while_loop(lambda j: (j < nt) & (at(own_ref, j) == e0), lambda j: j + 1, j0 + 1)

    def start_weights(j, s):
        @pl.when((j < nt) & (at(valid_ref, j) > 0))
        def _():
            for cp in weight_copies(at(own_ref, j), s):
                cp.start()

    @pl.when(i == 0)
    def _():
        wslot_ref[0] = 0
        j = 0
        for s in range(WEIGHT_SLOTS - 1):
            start_weights(j, s)
            j = next_expert_tile(j)

    @pl.when(valid_ref[i] > 0)
    def _():
        e = own_ref[i]

        @pl.when((i == 0) | (own_ref[jnp.maximum(i - 1, 0)] != e))
        def _():
            s = wslot_ref[0]
            for cp in weight_copies(e, s):
                cp.wait()
            j = i
            for _ in range(WEIGHT_SLOTS - 1):
                j = next_expert_tile(j)
            start_weights(j, (s + WEIGHT_SLOTS - 1) % WEIGHT_SLOTS)

            wgb[...] = wgf[s].astype(BF16)
            wub[...] = wuf[s].astype(BF16)
            wdb[...] = wdf[s].astype(BF16)
            wslot_ref[0] = (s + 1) % WEIGHT_SLOTS

        x = xs_ref[...].astype(BF16)
        a = _dot(x, wgb[...])
        u = _dot(x, wub[...])
        act = (a * jax.nn.sigmoid(a) * u).astype(BF16)
        ys_ref[...] = _dot(act, wdb[...])

    @pl.when(valid_ref[i] == 0)
    def _():
        ys_ref[...] = jnp.zeros(ys_ref.shape, F32)


def _experts(own, valid, xs, wg, wu, wd):
    n_tiles = xs.shape[0] // SLOT_TILE
    hbm = pl.BlockSpec(memory_space=pl.ANY)
    tile = (SLOT_TILE, D_MODEL)
    return pl.pallas_call(
        _experts_kernel,
        grid_spec=pltpu.PrefetchScalarGridSpec(
            num_scalar_prefetch=2,
            grid=(n_tiles,),
            in_specs=[
                pl.BlockSpec(tile, lambda i, o, v: (jnp.where(v[i] > 0, i, 0), 0)),
                hbm, hbm, hbm,
            ],
            out_specs=pl.BlockSpec(tile, lambda i, o, v: (i, 0)),
            scratch_shapes=[
                pltpu.VMEM((WEIGHT_SLOTS, D_MODEL, D_EXPERT), F32),
                pltpu.VMEM((WEIGHT_SLOTS, D_MODEL, D_EXPERT), F32),
                pltpu.VMEM((WEIGHT_SLOTS, D_EXPERT, D_MODEL), F32),
                pltpu.VMEM((D_MODEL, D_EXPERT), BF16),
                pltpu.VMEM((D_MODEL, D_EXPERT), BF16),
                pltpu.VMEM((D_EXPERT, D_MODEL), BF16),
                pltpu.SemaphoreType.DMA((WEIGHT_SLOTS,)),
                pltpu.SMEM((1,), jnp.int32),
            ],
        ),
        out_shape=jax.ShapeDtypeStruct(xs.shape, F32),
        compiler_params=_params("arbitrary"),
        name="experts",
    )(own, valid, xs, wg, wu, wd)


def _combine_kernel(s1_ref, s2_ref, x_ref, route_ref, g_ref, ys_ref, y_ref, buf_ref, sem, *, tm):
    i = pl.program_id(0)
    n = pl.num_programs(0)

    def copies(tile, p, r):
        t = tile * tm + r
        return (_row_copy(ys_ref, s1_ref[t], buf_ref.at[p, 0], r, sem.at[p]),
                _row_copy(ys_ref, s2_ref[t], buf_ref.at[p, 1], r, sem.at[p]))

    def gather_start(tile, p):
        def issue(r, c):
            first, second = copies(tile, p, r)
            first.start(priority=0)
            second.start(priority=1)
            return c

        lax.fori_loop(0, tm, issue, 0, unroll=8)

    def gather_wait(tile, p):
        def drain(r, c):
            for cp in copies(tile, p, r):
                cp.wait()
            return c

        lax.fori_loop(0, tm, drain, 0, unroll=8)

    @pl.when(i == 0)
    def _():
        gather_start(0, 0)

    p = i % 2

    @pl.when(i + 1 < n)
    def _():
        gather_start(i + 1, 1 - p)

    gather_wait(i, p)
    y = x_ref[...] + route_ref[:, 2:3] * buf_ref[p, 0] + route_ref[:, 3:4] * buf_ref[p, 1]
    y_ref[...] = _rms(y, g_ref[...])


def _combine(slot1, slot2, x2, route, g, ys, tm=256):
    t = x2.shape[0]
    kern = functools.partial(_combine_kernel, tm=tm)
    return pl.pallas_call(
        kern,
        grid_spec=pltpu.PrefetchScalarGridSpec(
            num_scalar_prefetch=2,
            grid=(t // tm,),
            in_specs=[
                pl.BlockSpec((tm, D_MODEL), lambda i, s1, s2: (i, 0)),
                pl.BlockSpec((tm, LANES), lambda i, s1, s2: (i, 0)),
                pl.BlockSpec((1, D_MODEL), lambda i, s1, s2: (0, 0)),
                pl.BlockSpec(memory_space=pl.ANY),
            ],
            out_specs=pl.BlockSpec((tm, D_MODEL), lambda i, s1, s2: (i, 0)),
            scratch_shapes=[
                pltpu.VMEM((2, 2, tm, D_MODEL), F32),
                pltpu.SemaphoreType.DMA((2,)),
            ],
        ),
        out_shape=jax.ShapeDtypeStruct((t, D_MODEL), F32),
        compiler_params=_params("arbitrary"),
        name="combine",
    )(slot1, slot2, x2, route, g, ys)


def _row(v):
    return v.reshape(1, -1).astype(F32)


def kernel(x, mem, positions, norm_mix, w_in, norm_q_lat, w_q_b, norm_kv_lat, w_kv_b, norm_mla_out, norm_sb_out, w_out, norm_mem_x, norm_mem_src, w_mem_q, w_mem_kv, w_mem_o, norm_ffn, w_group, b_group, w_expert_router, b_expert, w_gate, w_up, w_down, norm_final):
    batch, seq, d = x.shape
    t = batch * seq
    depth = w_in.shape[0]
    assert depth == 1, "single-layer trunk only"
    xt = x.reshape(t, d)
    pos = positions.reshape(t, 1)
    inv_freq = ROPE_THETA ** (-jnp.arange(0, MLA_ROPE, 2, dtype=F32) / MLA_ROPE)
    invf = jnp.concatenate([inv_freq, inv_freq, jnp.zeros((LANES - MLA_ROPE,), F32)]).reshape(1, LANES)
    n_slots = (2 * t // SLOT_TILE + N_EXPERTS) * SLOT_TILE

    for l in range(depth):
        sb0 = MLA_Q_RANK + MLA_KV_RANK + MLA_ROPE
        row_scale = jnp.concatenate([jnp.ones((sb0,), F32), jnp.full((SB_OUT,), SB_DIM ** -0.5 * LOG2E, F32),
                                     jnp.ones((2 * SB_OUT,), F32)])
        w_in_t = (w_in[l].T * row_scale[:, None]).astype(BF16)
        wq = w_q_b[l].reshape(MLA_Q_RANK, MLA_HEADS, MLA_QK) * (MLA_QK ** -0.5 * LOG2E)
        zq = jnp.zeros((MLA_Q_RANK, MLA_HEADS, MLA_PAD - MLA_QK), F32)
        wqa = jnp.concatenate([wq, zq], axis=-1).reshape(MLA_Q_RANK, MLA_HEADS * MLA_PAD).astype(BF16)
        wkv = w_kv_b[l].reshape(MLA_KV_RANK, MLA_HEADS, MLA_NOPE + MLA_V)
        wk = wkv[:, :, :MLA_NOPE].reshape(MLA_KV_RANK, MLA_OUT).astype(BF16)
        wvt = wkv[:, :, MLA_NOPE:].reshape(MLA_KV_RANK, MLA_OUT).T.astype(BF16)
        w_router = jnp.concatenate(
            [w_expert_router[l], w_group[l], jnp.zeros((d, LANES - N_EXPERTS - N_GROUPS), F32)], axis=1)
        wr_hi = w_router.astype(BF16)
        wr_hl = jnp.concatenate([wr_hi, (w_router - wr_hi.astype(F32)).astype(BF16)], axis=1)
        b_router = jnp.concatenate(
            [b_expert[l].astype(F32), b_group[l].astype(F32), jnp.zeros((LANES - N_EXPERTS - N_GROUPS,), F32)]
        ).reshape(1, LANES)

        lat, sb_qk, sb_vt = _proj_in(xt, _row(norm_mix[l]), w_in_t, sb0)
        q, k, vt = _mla_proj(lat, pos, invf, _row(norm_q_lat[l]), _row(norm_kv_lat[l]), wqa, wk, wvt)
        o_mla = _mla_attn(q, k, vt, batch, seq)
        o_sb = _sb_attn(sb_qk, sb_vt, batch, seq)
        x1 = _out_proj(xt, o_mla, o_sb, _row(norm_mla_out[l]), _row(norm_sb_out[l]), w_out[l])

        kv = _mem_kv(mem.reshape(batch * MEM_LEN, d), _row(norm_mem_src[l]), w_mem_kv[l])
        x2, h2, route = _mem_route(x1, _row(norm_mem_x[l]), w_mem_q[l], kv,
                                   w_mem_o[l], _row(norm_ffn[l]), wr_hl, b_router, seq)

        slots, meta = _slots(route)
        xs = _dispatch(slots[0], slots[1], meta[2], meta[3], meta[4], h2, n_slots)
        ys = _experts(meta[0], meta[1], xs, w_gate[l], w_up[l], w_down[l])
        xt = _combine(slots[0], slots[1], x2, route, _row(norm_final), ys)
    return xt.reshape(batch, seq, d)
```

```python
import functools

import jax
import jax.numpy as jnp
from jax import lax
from jax.experimental import pallas as pl
from jax.experimental.pallas import tpu as pltpu

F32 = jnp.float32
BF16 = jnp.bfloat16

EPS = 1e-6
ROPE_THETA = 10000.0

D_MODEL = 2048
MEM_LEN = 256
MLA_HEADS = 8
MLA_NOPE = 128
MLA_ROPE = 64
MLA_QK = MLA_NOPE + MLA_ROPE
MLA_V = 128
MLA_Q_RANK = 512
MLA_KV_RANK = 256
MLA_PAD = 256
SB_HEADS = 8
SB_DIM = 128
MLA_OUT = MLA_HEADS * MLA_V
SB_OUT = SB_HEADS * SB_DIM
MEM_HEADS = 4
MEM_DIM = 128
N_GROUPS = 4
EXPERTS_PER_GROUP = 8
N_EXPERTS = N_GROUPS * EXPERTS_PER_GROUP
D_EXPERT = 512

LANES = 128
LAT_COLS = 1024

SLOT_TILE = 256
WEIGHT_SLOTS = 2
VMEM_LIMIT = 56 * 1024 * 1024


def _rms(x, g):
    return x * lax.rsqrt(jnp.mean(x * x, axis=-1, keepdims=True) + EPS) * g


def _dot(a, b):
    return jnp.dot(a, b, preferred_element_type=F32)


def _dot_nt(a, b):
    return lax.dot_general(a, b, (((1,), (1,)), ((), ())), preferred_element_type=F32)


def _split_bf16(x):
    hi = x.astype(BF16)
    lo = (x - hi.astype(F32)).astype(BF16)
    return hi, lo


def _params(*sem):
    return pltpu.CompilerParams(dimension_semantics=sem, vmem_limit_bytes=VMEM_LIMIT)


def _proj_in_kernel(x_ref, g_ref, wl_ref, ws_ref, lat_ref, qk_ref, vt_ref, h_ref):
    j = pl.program_id(1)
    last = pl.num_programs(1) - 1

    @pl.when(j == 0)
    def _():
        h_ref[...] = _rms(x_ref[...], g_ref[...]).astype(BF16)
        lat_ref[...] = _dot_nt(h_ref[...], wl_ref[...])

    @pl.when((j > 0) & (j < last))
    def _():
        qk_ref[...] = _dot_nt(h_ref[...], ws_ref[...]).astype(BF16)

    @pl.when(j == last)
    def _():
        vt_ref[...] = _dot_nt(ws_ref[...], h_ref[...]).astype(BF16)


def _proj_in(x, g, w_t, sb0, tm=1024):
    t = x.shape[0]
    tn = LAT_COLS
    n_sb = (w_t.shape[0] - sb0) // tn
    return pl.pallas_call(
        _proj_in_kernel,
        grid=(t // tm, n_sb + 1),
        in_specs=[
            pl.BlockSpec((tm, D_MODEL), lambda i, j: (i, 0)),
            pl.BlockSpec((1, D_MODEL), lambda i, j: (0, 0)),
            pl.BlockSpec((tn, D_MODEL), lambda i, j: (0, 0), pipeline_mode=pl.Buffered(1)),
            pl.BlockSpec((pl.Element(tn), pl.Element(D_MODEL)),
                         lambda i, j: (pl.multiple_of(sb0 + jnp.maximum(j - 1, 0) * tn, MLA_ROPE), 0)),
        ],
        out_specs=[
            pl.BlockSpec((tm, tn), lambda i, j: (i, 0)),
            pl.BlockSpec((tm, tn), lambda i, j: (i, jnp.clip(j - 1, 0, n_sb - 2))),
            pl.BlockSpec((SB_OUT, tm), lambda i, j: (0, i)),
        ],
        out_shape=[
            jax.ShapeDtypeStruct((t, LAT_COLS), F32),
            jax.ShapeDtypeStruct((t, 2 * SB_OUT), BF16),
            jax.ShapeDtypeStruct((SB_OUT, t), BF16),
        ],
        scratch_shapes=[pltpu.VMEM((tm, D_MODEL), BF16)],
        compiler_params=_params("parallel", "arbitrary"),
        name="proj_in",
    )(x, g, w_t, w_t)


def _mla_proj_kernel(lat_ref, pos_ref, invf_ref, gq_ref, gkv_ref, wqa_ref, wk_ref, wvt_ref,
                     q_ref, k_ref, vt_ref):
    cq = _rms(lat_ref[:, :MLA_Q_RANK], gq_ref[...]).astype(BF16)
    ckv = _rms(lat_ref[:, MLA_Q_RANK:MLA_Q_RANK + MLA_KV_RANK], gkv_ref[...]).astype(BF16)
    ang = pos_ref[...].astype(F32) * invf_ref[...]
    lane = lax.broadcasted_iota(jnp.int32, ang.shape, 1)
    live = lane < MLA_ROPE
    cos2 = jnp.where(live, jnp.cos(ang), 0.0)
    sin2 = jnp.where(live, jnp.sin(ang), 0.0)
    half = MLA_ROPE // 2

    def swapped(x):
        return jnp.where(lane < half, -pltpu.roll(x, LANES - half, 1), pltpu.roll(x, half, 1))

    qa = _dot(cq, wqa_ref[...])
    kn = _dot(ckv, wk_ref[...])
    vt_ref[...] = _dot_nt(wvt_ref[...], ckv).astype(BF16)
    k_raw = lat_ref[:, MLA_Q_RANK + MLA_KV_RANK:MLA_Q_RANK + MLA_KV_RANK + LANES]
    k_pe = jnp.where(live, k_raw * cos2 + swapped(k_raw) * sin2, 0.0).astype(BF16)
    for h in range(MLA_HEADS):
        lo = h * MLA_PAD
        mid = lo + LANES
        q_ref[:, lo:mid] = qa[:, lo:mid].astype(BF16)
        q_pe = qa[:, mid:mid + LANES]
        q_ref[:, mid:mid + LANES] = (q_pe * cos2 + swapped(q_pe) * sin2).astype(BF16)
        k_ref[:, lo:mid] = kn[:, h * LANES:(h + 1) * LANES].astype(BF16)
        k_ref[:, mid:mid + LANES] = k_pe


def _mla_proj(lat, pos, invf, gq, gkv, wqa, wk, wvt, tm=512):
    t = lat.shape[0]
    full = lambda a: pl.BlockSpec(a.shape, lambda i: (0, 0))
    return pl.pallas_call(
        _mla_proj_kernel,
        grid=(t // tm,),
        in_specs=[
            pl.BlockSpec((tm, LAT_COLS), lambda i: (i, 0)),
            pl.BlockSpec((tm, 1), lambda i: (i, 0)),
            full(invf), full(gq), full(gkv), full(wqa), full(wk), full(wvt),
        ],
        out_specs=[
            pl.BlockSpec((tm, MLA_HEADS * MLA_PAD), lambda i: (i, 0)),
            pl.BlockSpec((tm, MLA_HEADS * MLA_PAD), lambda i: (i, 0)),
            pl.BlockSpec((MLA_OUT, tm), lambda i: (0, i)),
        ],
        out_shape=[
            jax.ShapeDtypeStruct((t, MLA_HEADS * MLA_PAD), BF16),
            jax.ShapeDtypeStruct((t, MLA_HEADS * MLA_PAD), BF16),
            jax.ShapeDtypeStruct((MLA_OUT, t), BF16),
        ],
        compiler_params=_params("parallel"),
        name="mla_proj",
    )(lat, pos, invf, gq, gkv, wqa, wk, wvt)


LOG2E = 1.4426950408889634


def _three_stage(n_pairs, stage_a, stage_b, stage_c):
    a_issue, a_finish = stage_a
    b_issue, b_finish = stage_b
    c_issue, c_finish = stage_c

    def run_a(n, slot, first):
        a_finish(n, slot, a_issue(n, slot, first), first)

    run_a(0, 0, True)
    run_a(1, 1, True)
    b_finish(0, 0, b_issue(0, 0))

    def half(na, sa, nb, sb, nc, sc):
        ra = a_issue(na, sa, False)
        rb = b_issue(nb, sb)
        rc = c_issue(nc, sc)
        a_finish(na, sa, ra, False)
        b_finish(nb, sb, rb)
        c_finish(nc, sc, rc)

    def body(p, carry):
        half(2 * p, 0, 2 * p - 1, 1, 2 * p - 2, 0)
        half(2 * p + 1, 1, 2 * p, 0, 2 * p - 1, 1)
        return carry

    lax.fori_loop(1, n_pairs + 1, body, 0)
    last = 2 * n_pairs + 1
    rb = b_issue(last, 1)
    rc = c_issue(last - 1, 0)
    b_finish(last, 1, rb)
    c_finish(last - 1, 0, rc)
    c_finish(last, 1, c_issue(last, 1))


def _two_stage(n_pairs, produce, consume):
    produce(0, 0, True)
    produce(1, 1, True)
    consume(0, 0)

    def body(p, carry):
        produce(2 * p, 0, False)
        consume(2 * p - 1, 1)
        produce(2 * p + 1, 1, False)
        consume(2 * p, 0)
        return carry

    lax.fori_loop(1, n_pairs + 1, body, 0)
    consume(2 * n_pairs + 1, 1)


MLA_SUM_ROWS = 16
MLA_GROUP = 8


def _mla_attn_kernel(q_ref, k_ref, vt_ref, o_ref, s_ref, m_ref, acc_ref, *, tq, tk):
    i = pl.program_id(2)
    heads = range(MLA_GROUP)
    dq, dv = MLA_PAD, MLA_V
    m_ref[...] = jnp.full(m_ref.shape, -jnp.inf, F32)
    acc_ref[...] = jnp.zeros(acc_ref.shape, F32)
    ones = jnp.ones((MLA_SUM_ROWS, tk), BF16)

    def key_start(n):
        tile = jnp.where(n < 2, 2 * i + n, 2 * i + 1 - n)
        return pl.multiple_of(tile * tk, tk)

    def produce(n, slot, diagonal):
        ks = key_start(n)
        for g in heads:
            st = _dot_nt(k_ref[pl.ds(ks, tk), g * dq:(g + 1) * dq], q_ref[:, g * dq:(g + 1) * dq])
            if diagonal:
                kpos = ks + lax.broadcasted_iota(jnp.int32, st.shape, 0)
                qpos = i * tq + lax.broadcasted_iota(jnp.int32, st.shape, 1)
                st = jnp.where(kpos <= qpos, st, -jnp.inf)
            s_ref[g, slot] = st

    def consume(n, slot):
        ks = key_start(n)
        for g in heads:
            st = s_ref[g, slot]
            m_old = m_ref[g]
            m_new = jnp.maximum(m_old, jnp.max(st, axis=0, keepdims=True))
            alpha = jnp.exp2(m_old - m_new)
            p = jnp.exp2(st - m_new).astype(BF16)
            v_ones = jnp.concatenate([vt_ref[g * dv:(g + 1) * dv, pl.ds(ks, tk)], ones], axis=0)
            acc_ref[g] = alpha * acc_ref[g] + _dot(v_ones, p)
            m_ref[g] = m_new

    _two_stage(i, produce, consume)
    for g in heads:
        o_ref[:, g * dv:(g + 1) * dv] = (acc_ref[g, :dv, :] / acc_ref[g, dv:dv + 1, :]).T


def _mla_attn(q, k, vt, batch, seq, tq=512):
    tk = tq // 2
    nq = seq // tq
    groups = MLA_HEADS // MLA_GROUP
    kern = functools.partial(_mla_attn_kernel, tq=tq, tk=tk)
    return pl.pallas_call(
        kern,
        grid=(batch, groups, nq),
        in_specs=[
            pl.BlockSpec((tq, MLA_GROUP * MLA_PAD), lambda b, h, i: (b * nq + i, h)),
            pl.BlockSpec((seq, MLA_GROUP * MLA_PAD), lambda b, h, i: (b, h), pipeline_mode=pl.Buffered(1)),
            pl.BlockSpec((MLA_GROUP * MLA_V, seq), lambda b, h, i: (h, b), pipeline_mode=pl.Buffered(1)),
        ],
        out_specs=pl.BlockSpec((tq, MLA_GROUP * MLA_V), lambda b, h, i: (b * nq + i, h)),
        out_shape=jax.ShapeDtypeStruct((batch * seq, MLA_OUT), F32),
        scratch_shapes=[
            pltpu.VMEM((MLA_GROUP, 2, tk, tq), F32),
            pltpu.VMEM((MLA_GROUP, 1, tq), F32),
            pltpu.VMEM((MLA_GROUP, MLA_V + MLA_SUM_ROWS, tq), F32),
        ],
        compiler_params=_params("parallel", "parallel", "arbitrary"),
        name="mla_attn",
    )(q, k, vt)


SB_EXP_CLAMP = 126.0
SB_GROUP = 4


def _sb_attn_kernel(q_ref, k_ref, vt_ref, o_ref, z_ref, hl_ref, arg_ref, acc_ref, c_ref, *, tq, tk):
    i = pl.program_id(2)
    heads = range(SB_GROUP)
    d = SB_DIM
    acc_ref[...] = jnp.zeros(acc_ref.shape, F32)
    c_ref[...] = jnp.zeros(c_ref.shape, F32)
    col = lax.broadcasted_iota(jnp.int32, (tk, tk), 1)
    row = lax.broadcasted_iota(jnp.int32, (tk, tk), 0)
    neg_tri = jnp.where(col >= row, -1.0, 0.0).astype(BF16)

    def key_start(n):
        return pl.multiple_of((2 * i + 1 - n) * tk, tk)

    def a_issue(n, slot, masked):
        ks = key_start(n)
        return [_dot_nt(k_ref[pl.ds(ks, tk), g * d:(g + 1) * d], q_ref[:, g * d:(g + 1) * d])
                for g in heads]

    def a_finish(n, slot, z2s, masked):
        for g in heads:
            z2 = z2s[g]
            sp = jnp.maximum(z2, jnp.log2(1.0 + jnp.exp2(jnp.minimum(z2, SB_EXP_CLAMP))))
            if masked:
                kpos = key_start(n) + lax.broadcasted_iota(jnp.int32, z2.shape, 0)
                qpos = i * tq + lax.broadcasted_iota(jnp.int32, z2.shape, 1)
                mask = kpos < qpos
                sp = jnp.where(mask, sp, 0.0)
                z2 = jnp.where(mask, z2, -jnp.inf)
            hl_ref[g, slot] = sp.astype(BF16)
            z_ref[g, slot] = z2

    def b_issue(n, slot):
        return [_dot(neg_tri, hl_ref[g, slot]) for g in heads]

    def b_finish(n, slot, laters):
        for g in heads:
            arg_ref[g, slot] = z_ref[g, slot] + laters[g] + c_ref[g]
            c_ref[g] += laters[g][0:1, :]

    def c_issue(n, slot):
        ks = key_start(n)
        return [_dot(vt_ref[g * d:(g + 1) * d, pl.ds(ks, tk)], jnp.exp2(arg_ref[g, slot]).astype(BF16))
                for g in heads]

    def c_finish(n, slot, pvs):
        for g in heads:
            acc_ref[g] += pvs[g]

    _three_stage(i, (a_issue, a_finish), (b_issue, b_finish), (c_issue, c_finish))
    for g in heads:
        o_ref[:, g * d:(g + 1) * d] = acc_ref[g].T


def _sb_attn(qk, vt, batch, seq, tq=512):
    tk = tq // 2
    nq = seq // tq
    groups = SB_HEADS // SB_GROUP
    gd = SB_GROUP * SB_DIM
    kern = functools.partial(_sb_attn_kernel, tq=tq, tk=tk)
    return pl.pallas_call(
        kern,
        grid=(batch, groups, nq),
        in_specs=[
            pl.BlockSpec((tq, gd), lambda b, h, i: (b * nq + i, h)),
            pl.BlockSpec((seq, gd), lambda b, h, i: (b, groups + h)),
            pl.BlockSpec((gd, seq), lambda b, h, i: (h, b)),
        ],
        out_specs=pl.BlockSpec((tq, gd), lambda b, h, i: (b * nq + i, h)),
        out_shape=jax.ShapeDtypeStruct((batch * seq, SB_OUT), F32),
        scratch_shapes=[
            pltpu.VMEM((SB_GROUP, 2, tk, tq), F32),
            pltpu.VMEM((SB_GROUP, 2, tk, tq), BF16),
            pltpu.VMEM((SB_GROUP, 2, tk, tq), F32),
            pltpu.VMEM((SB_GROUP, SB_DIM, tq), F32),
            pltpu.VMEM((SB_GROUP, 1, tq), F32),
        ],
        compiler_params=_params("parallel", "parallel", "arbitrary"),
        name="sb_attn",
    )(qk, qk, vt)


def _resident(a):
    return pl.BlockSpec(a.shape, lambda i: (0,) * a.ndim, pipeline_mode=pl.Buffered(1))


def _cast_on_first_step(pairs):
    @pl.when(pl.program_id(0) == 0)
    def _():
        for src_ref, dst_ref in pairs:
            dst_ref[...] = src_ref[...].astype(BF16)


def _out_proj_kernel(x_ref, oa_ref, ob_ref, ga_ref, gb_ref, w_ref, y_ref, wb_ref):
    _cast_on_first_step([(w_ref, wb_ref)])
    na = _rms(oa_ref[...], ga_ref[...]).astype(BF16)
    nb = _rms(ob_ref[...], gb_ref[...]).astype(BF16)
    y_ref[...] = x_ref[...] + _dot(na, wb_ref[:MLA_OUT, :]) + _dot(nb, wb_ref[MLA_OUT:, :])


def _out_proj(x, oa, ob, ga, gb, w, tm=512):
    t = x.shape[0]
    full = lambda a: pl.BlockSpec(a.shape, lambda i: (0, 0))
    return pl.pallas_call(
        _out_proj_kernel,
        grid=(t // tm,),
        in_specs=[
            pl.BlockSpec((tm, D_MODEL), lambda i: (i, 0)),
            pl.BlockSpec((tm, MLA_OUT), lambda i: (i, 0)),
            pl.BlockSpec((tm, SB_OUT), lambda i: (i, 0)),
            full(ga), full(gb), _resident(w),
        ],
        out_specs=pl.BlockSpec((tm, D_MODEL), lambda i: (i, 0)),
        out_shape=jax.ShapeDtypeStruct((t, D_MODEL), F32),
        scratch_shapes=[pltpu.VMEM(w.shape, BF16)],
        compiler_params=_params("arbitrary"),
        name="out_proj",
    )(x, oa, ob, ga, gb, w)


def _mem_kv_kernel(mem_ref, g_ref, w_ref, kv_ref):
    kv_ref[...] = _dot(_rms(mem_ref[...], g_ref[...]).astype(BF16), w_ref[...].astype(BF16)).astype(BF16)


def _mem_kv(mem, g, w):
    m = mem.shape[0]
    n = w.shape[1]
    full = lambda a: pl.BlockSpec(a.shape, lambda i: (0, 0))
    return pl.pallas_call(
        _mem_kv_kernel,
        grid=(1,),
        in_specs=[full(mem), full(g), full(w)],
        out_specs=pl.BlockSpec((m, n), lambda i: (0, 0)),
        out_shape=jax.ShapeDtypeStruct((m, n), BF16),
        compiler_params=_params("arbitrary"),
        name="mem_kv",
    )(mem, g, w)


GROUP_LANE0 = N_EXPERTS


def _mem_route_kernel(x_ref, gx_ref, wq_ref, kv_ref, wo_ref, gf_ref, wrhl_ref, br_ref,
                      x2_ref, h2_ref, route_ref, wqb_ref, wob_ref):
    _cast_on_first_step([(wq_ref, wqb_ref), (wo_ref, wob_ref)])
    x1 = x_ref[...]
    q = _dot(_rms(x1, gx_ref[...]).astype(BF16), wqb_ref[...]).astype(BF16)
    scale = MEM_DIM ** -0.5
    kw = MEM_HEADS * MEM_DIM
    heads = []
    for h in range(MEM_HEADS):
        lo = h * MEM_DIM
        s = _dot_nt(q[:, lo:lo + MEM_DIM], kv_ref[:, lo:lo + MEM_DIM]) * scale
        e = jnp.exp(s - jnp.max(s, axis=-1, keepdims=True))
        p = (e / jnp.sum(e, axis=-1, keepdims=True)).astype(BF16)
        heads.append(_dot(p, kv_ref[:, kw + lo:kw + lo + MEM_DIM]).astype(BF16))
    o = jnp.concatenate(heads, axis=-1)
    x2 = x1 + _dot(o, wob_ref[...])
    x2_ref[...] = x2
    h2 = _rms(x2, gf_ref[...])
    h2_ref[...] = h2

    hh, hl = _split_bf16(h2)
    both = _dot(hh, wrhl_ref[...])
    lg = both[:, :LANES] + both[:, LANES:] + _dot(hl, wrhl_ref[:, :LANES]) + br_ref[...]
    lane = lax.broadcasted_iota(jnp.int32, lg.shape, 1)
    big = jnp.int32(1 << 20)
    ninf = -jnp.inf

    def lane_max(v):
        return jnp.max(v, axis=-1, keepdims=True)

    def first_lane(cond):
        return jnp.min(jnp.where(cond, lane, big), axis=-1, keepdims=True)

    is_g = (lane >= GROUP_LANE0) & (lane < GROUP_LANE0 + N_GROUPS)
    g_max = lane_max(jnp.where(is_g, lg, ninf))
    g_sum = jnp.sum(jnp.where(is_g, jnp.exp(lg - g_max), 0.0), axis=-1, keepdims=True)
    p_g = 1.0 / g_sum
    g_idx = first_lane(is_g & (lg == g_max)) - GROUP_LANE0
    in_grp = (lane < N_EXPERTS) & ((lane // EXPERTS_PER_GROUP) == g_idx)
    e_max = lane_max(jnp.where(in_grp, lg, ninf))
    e_sum = jnp.sum(jnp.where(in_grp, jnp.exp(lg - e_max), 0.0), axis=-1, keepdims=True)
    i1 = first_lane(in_grp & (lg == e_max))
    rest = in_grp & (lane != i1)
    e_max2 = lane_max(jnp.where(rest, lg, ninf))
    i2 = first_lane(rest & (lg == e_max2))
    p1 = 1.0 / e_sum
    p2 = jnp.exp(e_max2 - e_max) / e_sum
    den = p1 + p2
    gate1 = p_g * (p1 / den)
    gate2 = p_g * (p2 / den)
    route = jnp.where(lane == 0, i1.astype(F32),
                      jnp.where(lane == 1, i2.astype(F32),
                                jnp.where(lane == 2, gate1,
                                          jnp.where(lane == 3, gate2, 0.0))))
    route_ref[...] = route


def _mem_route(x1, gx, wq, kv, wo, gf, wrhl, br, seq, tm=512):
    t = x1.shape[0]
    per_batch = seq // tm
    full = lambda a: pl.BlockSpec(a.shape, lambda i: (0, 0))
    return pl.pallas_call(
        _mem_route_kernel,
        grid=(t // tm,),
        in_specs=[
            pl.BlockSpec((tm, D_MODEL), lambda i: (i, 0)),
            full(gx), _resident(wq),
            pl.BlockSpec((MEM_LEN, kv.shape[1]), lambda i: (i // per_batch, 0)),
            _resident(wo), full(gf), full(wrhl), full(br),
        ],
        out_specs=[
            pl.BlockSpec((tm, D_MODEL), lambda i: (i, 0)),
            pl.BlockSpec((tm, D_MODEL), lambda i: (i, 0)),
            pl.BlockSpec((tm, LANES), lambda i: (i, 0)),
        ],
        out_shape=[
            jax.ShapeDtypeStruct((t, D_MODEL), F32),
            jax.ShapeDtypeStruct((t, D_MODEL), F32),
            jax.ShapeDtypeStruct((t, LANES), F32),
        ],
        scratch_shapes=[pltpu.VMEM(wq.shape, BF16), pltpu.VMEM(wo.shape, BF16)],
        compiler_params=_params("arbitrary"),
        name="mem_route",
    )(x1, gx, wq, kv, wo, gf, wrhl, br)


SLOT_BLK = 256


def _slots_kernel(route_ref, slot_ref, meta_ref, cum_ref, *, n_tok):
    nblk = n_tok // SLOT_BLK
    e_iota = lax.broadcasted_iota(jnp.int32, (LANES, SLOT_BLK), 0)
    incl = (lax.broadcasted_iota(jnp.int32, (SLOT_BLK, SLOT_BLK), 0)
            <= lax.broadcasted_iota(jnp.int32, (SLOT_BLK, SLOT_BLK), 1)).astype(BF16)

    def onehots(b):
        ts = pl.multiple_of(b * SLOT_BLK, SLOT_BLK)
        ids = route_ref[pl.ds(ts, SLOT_BLK), :].T
        oh1 = e_iota == ids[0:1, :].astype(jnp.int32)
        oh2 = e_iota == ids[1:2, :].astype(jnp.int32)
        return ts, oh1, oh2

    def count(b, carry):
        ts, oh1, oh2 = onehots(b)
        cnt = (oh1 | oh2).astype(F32).astype(BF16)
        c = _dot(cnt, incl) + carry
        cum_ref[:, pl.ds(ts, SLOT_BLK)] = c
        return c[:, SLOT_BLK - 1:SLOT_BLK]

    total = lax.fori_loop(0, nblk, count, jnp.zeros((LANES, 1), F32))
    tiles = jnp.floor((total + (SLOT_TILE - 1)) * (1.0 / SLOT_TILE))
    below = (lax.broadcasted_iota(jnp.int32, (LANES, LANES), 1)
             < lax.broadcasted_iota(jnp.int32, (LANES, LANES), 0)).astype(BF16)
    tile_lo = _dot(below, jnp.broadcast_to(tiles, (LANES, LANES)).astype(BF16))
    base = tile_lo[:, 0:1] * SLOT_TILE

    def assign(b, carry):
        ts, oh1, oh2 = onehots(b)
        pos = base + cum_ref[:, pl.ds(ts, SLOT_BLK)] - 1.0
        s1 = jnp.sum(jnp.where(oh1, pos, 0.0), axis=0, keepdims=True)
        s2 = jnp.sum(jnp.where(oh2, pos, 0.0), axis=0, keepdims=True)
        row = lax.broadcasted_iota(jnp.int32, (8, SLOT_BLK), 0)
        slot_ref[:, pl.ds(ts, SLOT_BLK)] = jnp.where(row == 0, s1, jnp.where(row == 1, s2, 0.0)).astype(jnp.int32)
        return carry

    lax.fori_loop(0, nblk, assign, 0)

    tile_hi = tile_lo + tiles
    tix = lax.broadcasted_iota(jnp.int32, (LANES, LANES), 1).astype(F32)
    is_e = lax.broadcasted_iota(jnp.int32, (LANES, LANES), 0) < N_EXPERTS
    owner = jnp.sum(jnp.where(is_e & (tile_hi <= tix), 1.0, 0.0), axis=0, keepdims=True)
    n_tiles = jnp.max(jnp.where(is_e, tile_hi, 0.0), axis=0, keepdims=True)
    valid = tix[0:1, :] < n_tiles
    last_owner = jnp.max(jnp.where(valid, owner, 0.0), axis=1, keepdims=True)
    owner = jnp.where(valid, owner, last_owner)
    eye = (lax.broadcasted_iota(jnp.int32, (LANES, LANES), 0)
           == lax.broadcasted_iota(jnp.int32, (LANES, LANES), 1))
    to_lanes = lambda colvec: jnp.sum(jnp.where(eye, colvec, 0.0), axis=0, keepdims=True)
    pad_first = to_lanes(base + total)
    pad_end = to_lanes(base + tiles * SLOT_TILE)
    row = lax.broadcasted_iota(jnp.int32, (8, LANES), 0)
    meta = jnp.where(row == 0, owner,
                     jnp.where(row == 1, valid.astype(F32),
                               jnp.where(row == 2, pad_first,
                                         jnp.where(row == 3, pad_end,
                                                   jnp.where(row == 4, n_tiles, 0.0)))))
    meta_ref[...] = meta.astype(jnp.int32)


def _slots(route):
    t = route.shape[0]
    kern = functools.partial(_slots_kernel, n_tok=t)
    return pl.pallas_call(
        kern,
        grid=(1,),
        in_specs=[pl.BlockSpec(route.shape, lambda i: (0, 0))],
        out_specs=[
            pl.BlockSpec((8, t), lambda i: (0, 0)),
            pl.BlockSpec((8, LANES), lambda i: (0, 0)),
        ],
        out_shape=[
            jax.ShapeDtypeStruct((8, t), jnp.int32),
            jax.ShapeDtypeStruct((8, LANES), jnp.int32),
        ],
        scratch_shapes=[pltpu.VMEM((LANES, t), F32)],
        compiler_params=_params("arbitrary"),
        name="slots",
    )(route)


def _row_copy(src_ref, src_row, dst_ref, dst_row, sem):
    return pltpu.make_async_copy(src_ref.at[pl.ds(src_row, 1), :], dst_ref.at[pl.ds(dst_row, 1), :], sem)


def _dispatch_kernel(s1_ref, s2_ref, pad_first_ref, pad_end_ref, n_tiles_ref, h_ref, xs_ref, zero_ref,
                     sem, pad_sem, *, tm):
    i = pl.program_id(0)
    t0 = i * tm
    n_slot_tiles = xs_ref.shape[0] // SLOT_TILE

    def copies(r):
        return (_row_copy(h_ref, r, xs_ref, s1_ref[t0 + r], sem),
                _row_copy(h_ref, r, xs_ref, s2_ref[t0 + r], sem))

    def issue(r, c):
        first, second = copies(r)
        first.start(priority=0)
        second.start(priority=1)
        return c

    lax.fori_loop(0, tm, issue, 0, unroll=8)

    @pl.when(i == pl.num_programs(0) - 1)
    def _():
        zero_ref[...] = jnp.zeros(zero_ref.shape, F32)

        def pad_copy(slot):
            return _row_copy(zero_ref, 0, xs_ref, slot, pad_sem)

        def fill(e, c):
            def one(s, cc):
                pad_copy(s).start()
                return cc

            return lax.fori_loop(pad_first_ref[e], pad_end_ref[e], one, c)

        lax.fori_loop(0, N_EXPERTS, fill, 0)

        def tile_copy(tile):
            return pltpu.make_async_copy(zero_ref, xs_ref.at[pl.ds(tile * SLOT_TILE, SLOT_TILE), :], pad_sem)

        def fill_tile(tile, c):
            tile_copy(tile).start()
            return c

        lax.fori_loop(n_tiles_ref[0], n_slot_tiles, fill_tile, 0)

        def settle(e, c):
            def one(s, cc):
                pad_copy(s).wait()
                return cc

            return lax.fori_loop(pad_first_ref[e], pad_end_ref[e], one, c)

        lax.fori_loop(0, N_EXPERTS, settle, 0)

        def settle_tile(tile, c):
            tile_copy(tile).wait()
            return c

        lax.fori_loop(n_tiles_ref[0], n_slot_tiles, settle_tile, 0)

    def drain(r, c):
        for cp in copies(r):
            cp.wait()
        return c

    lax.fori_loop(0, tm, drain, 0, unroll=8)


def _dispatch(slot1, slot2, pad_first, pad_end, n_tiles, h2, n_slots, tm=1024):
    t = h2.shape[0]
    kern = functools.partial(_dispatch_kernel, tm=tm)
    return pl.pallas_call(
        kern,
        grid_spec=pltpu.PrefetchScalarGridSpec(
            num_scalar_prefetch=5,
            grid=(t // tm,),
            in_specs=[pl.BlockSpec((tm, D_MODEL), lambda i, *_: (i, 0))],
            out_specs=pl.BlockSpec(memory_space=pl.ANY),
            scratch_shapes=[
                pltpu.VMEM((SLOT_TILE, D_MODEL), F32),
                pltpu.SemaphoreType.DMA(()),
                pltpu.SemaphoreType.DMA(()),
            ],
        ),
        out_shape=jax.ShapeDtypeStruct((n_slots, D_MODEL), F32),
        compiler_params=_params("arbitrary"),
        name="dispatch",
    )(slot1, slot2, pad_first, pad_end, n_tiles, h2)


def _experts_kernel(own_ref, valid_ref, xs_ref, wg_hbm, wu_hbm, wd_hbm, ys_ref,
                    wgf, wuf, wdf, wgb, wub, wdb, wsem, wslot_ref):
    i = pl.program_id(0)
    nt = pl.num_programs(0)
    at = lambda ref, j: ref[jnp.minimum(j, nt - 1)]

    def weight_copies(e, s):
        return (pltpu.make_async_copy(wg_hbm.at[e], wgf.at[s], wsem.at[s]),
                pltpu.make_async_copy(wu_hbm.at[e], wuf.at[s], wsem.at[s]),
                pltpu.make_async_copy(wd_hbm.at[e], wdf.at[s], wsem.at[s]))

    def next_expert_tile(j0):
        e0 = at(own_ref, j0)
        return lax.while_loop(lambda j: (j < nt) & (at(own_ref, j) == e0), lambda j: j + 1, j0 + 1)

    def start_weights(j, s):
        @pl.when((j < nt) & (at(valid_ref, j) > 0))
        def _():
            for cp in weight_copies(at(own_ref, j), s):
                cp.start()

    @pl.when(i == 0)
    def _():
        wslot_ref[0] = 0
        j = 0
        for s in range(WEIGHT_SLOTS - 1):
            start_weights(j, s)
            j = next_expert_tile(j)

    @pl.when(valid_ref[i] > 0)
    def _():
        e = own_ref[i]

        @pl.when((i == 0) | (own_ref[jnp.maximum(i - 1, 0)] != e))
        def _():
            s = wslot_ref[0]
            for cp in weight_copies(e, s):
                cp.wait()
            j = i
            for _ in range(WEIGHT_SLOTS - 1):
                j = next_expert_tile(j)
            start_weights(j, (s + WEIGHT_SLOTS - 1) % WEIGHT_SLOTS)

            wgb[...] = wgf[s].astype(BF16)
            wub[...] = wuf[s].astype(BF16)
            wdb[...] = wdf[s].astype(BF16)
            wslot_ref[0] = (s + 1) % WEIGHT_SLOTS

        x = xs_ref[...].astype(BF16)
        a = _dot(x, wgb[...])
        u = _dot(x, wub[...])
        act = (a * jax.nn.sigmoid(a) * u).astype(BF16)
        ys_ref[...] = _dot(act, wdb[...])

    @pl.when(valid_ref[i] == 0)
    def _():
        ys_ref[...] = jnp.zeros(ys_ref.shape, F32)


def _experts(own, valid, xs, wg, wu, wd):
    n_tiles = xs.shape[0] // SLOT_TILE
    hbm = pl.BlockSpec(memory_space=pl.ANY)
    tile = (SLOT_TILE, D_MODEL)
    return pl.pallas_call(
        _experts_kernel,
        grid_spec=pltpu.PrefetchScalarGridSpec(
            num_scalar_prefetch=2,
            grid=(n_tiles,),
            in_specs=[
                pl.BlockSpec(tile, lambda i, o, v: (jnp.where(v[i] > 0, i, 0), 0)),
                hbm, hbm, hbm,
            ],
            out_specs=pl.BlockSpec(tile, lambda i, o, v: (i, 0)),
            scratch_shapes=[
                pltpu.VMEM((WEIGHT_SLOTS, D_MODEL, D_EXPERT), F32),
                pltpu.VMEM((WEIGHT_SLOTS, D_MODEL, D_EXPERT), F32),
                pltpu.VMEM((WEIGHT_SLOTS, D_EXPERT, D_MODEL), F32),
                pltpu.VMEM((D_MODEL, D_EXPERT), BF16),
                pltpu.VMEM((D_MODEL, D_EXPERT), BF16),
                pltpu.VMEM((D_EXPERT, D_MODEL), BF16),
                pltpu.SemaphoreType.DMA((WEIGHT_SLOTS,)),
                pltpu.SMEM((1,), jnp.int32),
            ],
        ),
        out_shape=jax.ShapeDtypeStruct(xs.shape, F32),
        compiler_params=_params("arbitrary"),
        name="experts",
    )(own, valid, xs, wg, wu, wd)


def _combine_kernel(s1_ref, s2_ref, x_ref, route_ref, g_ref, ys_ref, y_ref, buf_ref, sem, *, tm):
    i = pl.program_id(0)
    n = pl.num_programs(0)

    def copies(tile, p, r):
        t = tile * tm + r
        return (_row_copy(ys_ref, s1_ref[t], buf_ref.at[p, 0], r, sem.at[p]),
                _row_copy(ys_ref, s2_ref[t], buf_ref.at[p, 1], r, sem.at[p]))

    def gather_start(tile, p):
        def issue(r, c):
            first, second = copies(tile, p, r)
            first.start(priority=0)
            second.start(priority=1)
            return c

        lax.fori_loop(0, tm, issue, 0, unroll=8)

    def gather_wait(tile, p):
        def drain(r, c):
            for cp in copies(tile, p, r):
                cp.wait()
            return c

        lax.fori_loop(0, tm, drain, 0, unroll=8)

    @pl.when(i == 0)
    def _():
        gather_start(0, 0)

    p = i % 2

    @pl.when(i + 1 < n)
    def _():
        gather_start(i + 1, 1 - p)

    gather_wait(i, p)
    y = x_ref[...] + route_ref[:, 2:3] * buf_ref[p, 0] + route_ref[:, 3:4] * buf_ref[p, 1]
    y_ref[...] = _rms(y, g_ref[...])


def _combine(slot1, slot2, x2, route, g, ys, tm=256):
    t = x2.shape[0]
    kern = functools.partial(_combine_kernel, tm=tm)
    return pl.pallas_call(
        kern,
        grid_spec=pltpu.PrefetchScalarGridSpec(
            num_scalar_prefetch=2,
            grid=(t // tm,),
            in_specs=[
                pl.BlockSpec((tm, D_MODEL), lambda i, s1, s2: (i, 0)),
                pl.BlockSpec((tm, LANES), lambda i, s1, s2: (i, 0)),
                pl.BlockSpec((1, D_MODEL), lambda i, s1, s2: (0, 0)),
                pl.BlockSpec(memory_space=pl.ANY),
            ],
            out_specs=pl.BlockSpec((tm, D_MODEL), lambda i, s1, s2: (i, 0)),
            scratch_shapes=[
                pltpu.VMEM((2, 2, tm, D_MODEL), F32),
                pltpu.SemaphoreType.DMA((2,)),
            ],
        ),
        out_shape=jax.ShapeDtypeStruct((t, D_MODEL), F32),
        compiler_params=_params("arbitrary"),
        name="combine",
    )(slot1, slot2, x2, route, g, ys)


def _row(v):
    return v.reshape(1, -1).astype(F32)


def kernel(x, mem, positions, norm_mix, w_in, norm_q_lat, w_q_b, norm_kv_lat, w_kv_b, norm_mla_out, norm_sb_out, w_out, norm_mem_x, norm_mem_src, w_mem_q, w_mem_kv, w_mem_o, norm_ffn, w_group, b_group, w_expert_router, b_expert, w_gate, w_up, w_down, norm_final):
    batch, seq, d = x.shape
    t = batch * seq
    depth = w_in.shape[0]
    assert depth == 1, "single-layer trunk only"
    xt = x.reshape(t, d)
    pos = positions.reshape(t, 1)
    inv_freq = ROPE_THETA ** (-jnp.arange(0, MLA_ROPE, 2, dtype=F32) / MLA_ROPE)
    invf = jnp.concatenate([inv_freq, inv_freq, jnp.zeros((LANES - MLA_ROPE,), F32)]).reshape(1, LANES)
    n_slots = (2 * t // SLOT_TILE + N_EXPERTS) * SLOT_TILE

    for l in range(depth):
        sb0 = MLA_Q_RANK + MLA_KV_RANK + MLA_ROPE
        row_scale = jnp.concatenate([jnp.ones((sb0,), F32), jnp.full((SB_OUT,), SB_DIM ** -0.5 * LOG2E, F32),
                                     jnp.ones((2 * SB_OUT,), F32)])
        w_in_t = (w_in[l].T * row_scale[:, None]).astype(BF16)
        wq = w_q_b[l].reshape(MLA_Q_RANK, MLA_HEADS, MLA_QK) * (MLA_QK ** -0.5 * LOG2E)
        zq = jnp.zeros((MLA_Q_RANK, MLA_HEADS, MLA_PAD - MLA_QK), F32)
        wqa = jnp.concatenate([wq, zq], axis=-1).reshape(MLA_Q_RANK, MLA_HEADS * MLA_PAD).astype(BF16)
        wkv = w_kv_b[l].reshape(MLA_KV_RANK, MLA_HEADS, MLA_NOPE + MLA_V)
        wk = wkv[:, :, :MLA_NOPE].reshape(MLA_KV_RANK, MLA_OUT).astype(BF16)
        wvt = wkv[:, :, MLA_NOPE:].reshape(MLA_KV_RANK, MLA_OUT).T.astype(BF16)
        w_router = jnp.concatenate(
            [w_expert_router[l], w_group[l], jnp.zeros((d, LANES - N_EXPERTS - N_GROUPS), F32)], axis=1)
        wr_hi = w_router.astype(BF16)
        wr_hl = jnp.concatenate([wr_hi, (w_router - wr_hi.astype(F32)).astype(BF16)], axis=1)
        b_router = jnp.concatenate(
            [b_expert[l].astype(F32), b_group[l].astype(F32), jnp.zeros((LANES - N_EXPERTS - N_GROUPS,), F32)]
        ).reshape(1, LANES)

        lat, sb_qk, sb_vt = _proj_in(xt, _row(norm_mix[l]), w_in_t, sb0)
        q, k, vt = _mla_proj(lat, pos, invf, _row(norm_q_lat[l]), _row(norm_kv_lat[l]), wqa, wk, wvt)
        o_mla = _mla_attn(q, k, vt, batch, seq)
        o_sb = _sb_attn(sb_qk, sb_vt, batch, seq)
        x1 = _out_proj(xt, o_mla, o_sb, _row(norm_mla_out[l]), _row(norm_sb_out[l]), w_out[l])

        kv = _mem_kv(mem.reshape(batch * MEM_LEN, d), _row(norm_mem_src[l]), w_mem_kv[l])
        x2, h2, route = _mem_route(x1, _row(norm_mem_x[l]), w_mem_q[l], kv,
                                   w_mem_o[l], _row(norm_ffn[l]), wr_hl, b_router, seq)

        slots, meta = _slots(route)
        xs = _dispatch(slots[0], slots[1], meta[2], meta[3], meta[4], h2, n_slots)
        ys = _experts(meta[0], meta[1], xs, w_gate[l], w_up[l], w_down[l])
        xt = _combine(slots[0], slots[1], x2, route, _row(norm_final), ys)
    return xt.reshape(batch, seq, d)
```

```python
import functools

import jax
import jax.numpy as jnp
from jax import lax
from jax.experimental import pallas as pl
from jax.experimental.pallas import tpu as pltpu

F32 = jnp.float32
BF16 = jnp.bfloat16

EPS = 1e-6
ROPE_THETA = 10000.0

D_MODEL = 2048
MEM_LEN = 256
MLA_HEADS = 8
MLA_NOPE = 128
MLA_ROPE = 64
MLA_QK = MLA_NOPE + MLA_ROPE
MLA_V = 128
MLA_Q_RANK = 512
MLA_KV_RANK = 256
MLA_PAD = 256
SB_HEADS = 8
SB_DIM = 128
MLA_OUT = MLA_HEADS * MLA_V
SB_OUT = SB_HEADS * SB_DIM
MEM_HEADS = 4
MEM_DIM = 128
N_GROUPS = 4
EXPERTS_PER_GROUP = 8
N_EXPERTS = N_GROUPS * EXPERTS_PER_GROUP
D_EXPERT = 512

LANES = 128
LAT_COLS = 1024

SLOT_TILE = 256
WEIGHT_SLOTS = 2
VMEM_LIMIT = 56 * 1024 * 1024


def _rms(x, g):
    return x * lax.rsqrt(jnp.mean(x * x, axis=-1, keepdims=True) + EPS) * g


def _dot(a, b):
    return jnp.dot(a, b, preferred_element_type=F32)


def _dot_nt(a, b):
    return lax.dot_general(a, b, (((1,), (1,)), ((), ())), preferred_element_type=F32)


def _split_bf16(x):
    hi = x.astype(BF16)
    lo = (x - hi.astype(F32)).astype(BF16)
    return hi, lo


def _params(*sem):
    return pltpu.CompilerParams(dimension_semantics=sem, vmem_limit_bytes=VMEM_LIMIT)


def _proj_in_kernel(x_ref, g_ref, wl_ref, ws_ref, lat_ref, qk_ref, vt_ref, h_ref):
    j = pl.program_id(1)
    last = pl.num_programs(1) - 1

    @pl.when(j == 0)
    def _():
        h_ref[...] = _rms(x_ref[...], g_ref[...]).astype(BF16)
        lat_ref[...] = _dot_nt(h_ref[...], wl_ref[...])

    @pl.when((j > 0) & (j < last))
    def _():
        qk_ref[...] = _dot_nt(h_ref[...], ws_ref[...]).astype(BF16)

    @pl.when(j == last)
    def _():
        vt_ref[...] = _dot_nt(ws_ref[...], h_ref[...]).astype(BF16)


def _proj_in(x, g, w_t, sb0, tm=1024):
    t = x.shape[0]
    tn = LAT_COLS
    n_sb = (w_t.shape[0] - sb0) // tn
    return pl.pallas_call(
        _proj_in_kernel,
        grid=(t // tm, n_sb + 1),
        in_specs=[
            pl.BlockSpec((tm, D_MODEL), lambda i, j: (i, 0)),
            pl.BlockSpec((1, D_MODEL), lambda i, j: (0, 0)),
            pl.BlockSpec((tn, D_MODEL), lambda i, j: (0, 0), pipeline_mode=pl.Buffered(1)),
            pl.BlockSpec((pl.Element(tn), pl.Element(D_MODEL)),
                         lambda i, j: (pl.multiple_of(sb0 + jnp.maximum(j - 1, 0) * tn, MLA_ROPE), 0)),
        ],
        out_specs=[
            pl.BlockSpec((tm, tn), lambda i, j: (i, 0)),
            pl.BlockSpec((tm, tn), lambda i, j: (i, jnp.clip(j - 1, 0, n_sb - 2))),
            pl.BlockSpec((SB_OUT, tm), lambda i, j: (0, i)),
        ],
        out_shape=[
            jax.ShapeDtypeStruct((t, LAT_COLS), F32),
            jax.ShapeDtypeStruct((t, 2 * SB_OUT), BF16),
            jax.ShapeDtypeStruct((SB_OUT, t), BF16),
        ],
        scratch_shapes=[pltpu.VMEM((tm, D_MODEL), BF16)],
        compiler_params=_params("parallel", "arbitrary"),
        name="proj_in",
    )(x, g, w_t, w_t)


def _mla_proj_kernel(lat_ref, pos_ref, invf_ref, gq_ref, gkv_ref, wqa_ref, wk_ref, wvt_ref,
                     q_ref, k_ref, vt_ref):
    cq = _rms(lat_ref[:, :MLA_Q_RANK], gq_ref[...]).astype(BF16)
    ckv = _rms(lat_ref[:, MLA_Q_RANK:MLA_Q_RANK + MLA_KV_RANK], gkv_ref[...]).astype(BF16)
    ang = pos_ref[...].astype(F32) * invf_ref[...]
    lane = lax.broadcasted_iota(jnp.int32, ang.shape, 1)
    live = lane < MLA_ROPE
    cos2 = jnp.where(live, jnp.cos(ang), 0.0)
    sin2 = jnp.where(live, jnp.sin(ang), 0.0)
    half = MLA_ROPE // 2

    def swapped(x):
        return jnp.where(lane < half, -pltpu.roll(x, LANES - half, 1), pltpu.roll(x, half, 1))

    qa = _dot(cq, wqa_ref[...])
    kn = _dot(ckv, wk_ref[...])
    vt_ref[...] = _dot_nt(wvt_ref[...], ckv).astype(BF16)
    k_raw = lat_ref[:, MLA_Q_RANK + MLA_KV_RANK:MLA_Q_RANK + MLA_KV_RANK + LANES]
    k_pe = jnp.where(live, k_raw * cos2 + swapped(k_raw) * sin2, 0.0).astype(BF16)
    for h in range(MLA_HEADS):
        lo = h * MLA_PAD
        mid = lo + LANES
        q_ref[:, lo:mid] = qa[:, lo:mid].astype(BF16)
        q_pe = qa[:, mid:mid + LANES]
        q_ref[:, mid:mid + LANES] = (q_pe * cos2 + swapped(q_pe) * sin2).astype(BF16)
        k_ref[:, lo:mid] = kn[:, h * LANES:(h + 1) * LANES].astype(BF16)
        k_ref[:, mid:mid + LANES] = k_pe


def _mla_proj(lat, pos, invf, gq, gkv, wqa, wk, wvt, tm=512):
    t = lat.shape[0]
    full = lambda a: pl.BlockSpec(a.shape, lambda i: (0, 0))
    return pl.pallas_call(
        _mla_proj_kernel,
        grid=(t // tm,),
        in_specs=[
            pl.BlockSpec((tm, LAT_COLS), lambda i: (i, 0)),
            pl.BlockSpec((tm, 1), lambda i: (i, 0)),
            full(invf), full(gq), full(gkv), full(wqa), full(wk), full(wvt),
        ],
        out_specs=[
            pl.BlockSpec((tm, MLA_HEADS * MLA_PAD), lambda i: (i, 0)),
            pl.BlockSpec((tm, MLA_HEADS * MLA_PAD), lambda i: (i, 0)),
            pl.BlockSpec((MLA_OUT, tm), lambda i: (0, i)),
        ],
        out_shape=[
            jax.ShapeDtypeStruct((t, MLA_HEADS * MLA_PAD), BF16),
            jax.ShapeDtypeStruct((t, MLA_HEADS * MLA_PAD), BF16),
            jax.ShapeDtypeStruct((MLA_OUT, t), BF16),
        ],
        compiler_params=_params("parallel"),
        name="mla_proj",
    )(lat, pos, invf, gq, gkv, wqa, wk, wvt)


LOG2E = 1.4426950408889634


def _three_stage(n_pairs, stage_a, stage_b, stage_c):
    a_issue, a_finish = stage_a
    b_issue, b_finish = stage_b
    c_issue, c_finish = stage_c

    def run_a(n, slot, first):
        a_finish(n, slot, a_issue(n, slot, first), first)

    run_a(0, 0, True)
    run_a(1, 1, True)
    b_finish(0, 0, b_issue(0, 0))

    def half(na, sa, nb, sb, nc, sc):
        ra = a_issue(na, sa, False)
        rb = b_issue(nb, sb)
        rc = c_issue(nc, sc)
        a_finish(na, sa, ra, False)
        b_finish(nb, sb, rb)
        c_finish(nc, sc, rc)

    def body(p, carry):
        half(2 * p, 0, 2 * p - 1, 1, 2 * p - 2, 0)
        half(2 * p + 1, 1, 2 * p, 0, 2 * p - 1, 1)
        return carry

    lax.fori_loop(1, n_pairs + 1, body, 0)
    last = 2 * n_pairs + 1
    rb = b_issue(last, 1)
    rc = c_issue(last - 1, 0)
    b_finish(last, 1, rb)
    c_finish(last - 1, 0, rc)
    c_finish(last, 1, c_issue(last, 1))


def _two_stage(n_pairs, produce, consume):
    produce(0, 0, True)
    produce(1, 1, True)
    consume(0, 0)

    def body(p, carry):
        produce(2 * p, 0, False)
        consume(2 * p - 1, 1)
        produce(2 * p + 1, 1, False)
        consume(2 * p, 0)
        return carry

    lax.fori_loop(1, n_pairs + 1, body, 0)
    consume(2 * n_pairs + 1, 1)


MLA_SUM_ROWS = 16
MLA_GROUP = 8


def _mla_attn_kernel(q_ref, k_ref, vt_ref, o_ref, s_ref, m_ref, acc_ref, *, tq, tk):
    i = pl.program_id(2)
    heads = range(MLA_GROUP)
    dq, dv = MLA_PAD, MLA_V
    m_ref[...] = jnp.full(m_ref.shape, -jnp.inf, F32)
    acc_ref[...] = jnp.zeros(acc_ref.shape, F32)
    ones = jnp.ones((MLA_SUM_ROWS, tk), BF16)

    def key_start(n):
        tile = jnp.where(n < 2, 2 * i + n, 2 * i + 1 - n)
        return pl.multiple_of(tile * tk, tk)

    def produce(n, slot, diagonal):
        ks = key_start(n)
        for g in heads:
            st = _dot_nt(k_ref[pl.ds(ks, tk), g * dq:(g + 1) * dq], q_ref[:, g * dq:(g + 1) * dq])
            if diagonal:
                kpos = ks + lax.broadcasted_iota(jnp.int32, st.shape, 0)
                qpos = i * tq + lax.broadcasted_iota(jnp.int32, st.shape, 1)
                st = jnp.where(kpos <= qpos, st, -jnp.inf)
            s_ref[g, slot] = st

    def consume(n, slot):
        ks = key_start(n)
        for g in heads:
            st = s_ref[g, slot]
            m_old = m_ref[g]
            m_new = jnp.maximum(m_old, jnp.max(st, axis=0, keepdims=True))
            alpha = jnp.exp2(m_old - m_new)
            p = jnp.exp2(st - m_new).astype(BF16)
            v_ones = jnp.concatenate([vt_ref[g * dv:(g + 1) * dv, pl.ds(ks, tk)], ones], axis=0)
            acc_ref[g] = alpha * acc_ref[g] + _dot(v_ones, p)
            m_ref[g] = m_new

    _two_stage(i, produce, consume)
    for g in heads:
        o_ref[:, g * dv:(g + 1) * dv] = (acc_ref[g, :dv, :] / acc_ref[g, dv:dv + 1, :]).T


def _mla_attn(q, k, vt, batch, seq, tq=512):
    tk = tq // 2
    nq = seq // tq
    groups = MLA_HEADS // MLA_GROUP
    kern = functools.partial(_mla_attn_kernel, tq=tq, tk=tk)
    return pl.pallas_call(
        kern,
        grid=(batch, groups, nq),
        in_specs=[
            pl.BlockSpec((tq, MLA_GROUP * MLA_PAD), lambda b, h, i: (b * nq + i, h)),
            pl.BlockSpec((seq, MLA_GROUP * MLA_PAD), lambda b, h, i: (b, h), pipeline_mode=pl.Buffered(1)),
            pl.BlockSpec((MLA_GROUP * MLA_V, seq), lambda b, h, i: (h, b), pipeline_mode=pl.Buffered(1)),
        ],
        out_specs=pl.BlockSpec((tq, MLA_GROUP * MLA_V), lambda b, h, i: (b * nq + i, h)),
        out_shape=jax.ShapeDtypeStruct((batch * seq, MLA_OUT), F32),
        scratch_shapes=[
            pltpu.VMEM((MLA_GROUP, 2, tk, tq), F32),
            pltpu.VMEM((MLA_GROUP, 1, tq), F32),
            pltpu.VMEM((MLA_GROUP, MLA_V + MLA_SUM_ROWS, tq), F32),
        ],
        compiler_params=_params("parallel", "parallel", "arbitrary"),
        name="mla_attn",
    )(q, k, vt)


SB_EXP_CLAMP = 126.0
SB_GROUP = 4


def _sb_attn_kernel(q_ref, k_ref, vt_ref, o_ref, z_ref, hl_ref, arg_ref, acc_ref, c_ref, *, tq, tk):
    i = pl.program_id(2)
    heads = range(SB_GROUP)
    d = SB_DIM
    acc_ref[...] = jnp.zeros(acc_ref.shape, F32)
    c_ref[...] = jnp.zeros(c_ref.shape, F32)
    col = lax.broadcasted_iota(jnp.int32, (tk, tk), 1)
    row = lax.broadcasted_iota(jnp.int32, (tk, tk), 0)
    neg_tri = jnp.where(col >= row, -1.0, 0.0).astype(BF16)

    def key_start(n):
        return pl.multiple_of((2 * i + 1 - n) * tk, tk)

    def a_issue(n, slot, masked):
        ks = key_start(n)
        return [_dot_nt(k_ref[pl.ds(ks, tk), g * d:(g + 1) * d], q_ref[:, g * d:(g + 1) * d])
                for g in heads]

    def a_finish(n, slot, z2s, masked):
        for g in heads:
            z2 = z2s[g]
            sp = jnp.maximum(z2, jnp.log2(1.0 + jnp.exp2(jnp.minimum(z2, SB_EXP_CLAMP))))
            if masked:
                kpos = key_start(n) + lax.broadcasted_iota(jnp.int32, z2.shape, 0)
                qpos = i * tq + lax.broadcasted_iota(jnp.int32, z2.shape, 1)
                mask = kpos < qpos
                sp = jnp.where(mask, sp, 0.0)
                z2 = jnp.where(mask, z2, -jnp.inf)
            hl_ref[g, slot] = sp.astype(BF16)
            z_ref[g, slot] = z2

    def b_issue(n, slot):
        return [_dot(neg_tri, hl_ref[g, slot]) for g in heads]

    def b_finish(n, slot, laters):
        for g in heads:
            arg_ref[g, slot] = z_ref[g, slot] + laters[g] + c_ref[g]
            c_ref[g] += laters[g][0:1, :]

    def c_issue(n, slot):
        ks = key_start(n)
        return [_dot(vt_ref[g * d:(g + 1) * d, pl.ds(ks, tk)], jnp.exp2(arg_ref[g, slot]).astype(BF16))
                for g in heads]

    def c_finish(n, slot, pvs):
        for g in heads:
            acc_ref[g] += pvs[g]

    _three_stage(i, (a_issue, a_finish), (b_issue, b_finish), (c_issue, c_finish))
    for g in heads:
        o_ref[:, g * d:(g + 1) * d] = acc_ref[g].T


def _sb_attn(qk, vt, batch, seq, tq=512):
    tk = tq // 2
    nq = seq // tq
    groups = SB_HEADS // SB_GROUP
    gd = SB_GROUP * SB_DIM
    kern = functools.partial(_sb_attn_kernel, tq=tq, tk=tk)
    return pl.pallas_call(
        kern,
        grid=(batch, groups, nq),
        in_specs=[
            pl.BlockSpec((tq, gd), lambda b, h, i: (b * nq + i, h)),
            pl.BlockSpec((seq, gd), lambda b, h, i: (b, groups + h)),
            pl.BlockSpec((gd, seq), lambda b, h, i: (h, b)),
        ],
        out_specs=pl.BlockSpec((tq, gd), lambda b, h, i: (b * nq + i, h)),
        out_shape=jax.ShapeDtypeStruct((batch * seq, SB_OUT), F32),
        scratch_shapes=[
            pltpu.VMEM((SB_GROUP, 2, tk, tq), F32),
            pltpu.VMEM((SB_GROUP, 2, tk, tq), BF16),
            pltpu.VMEM((SB_GROUP, 2, tk, tq), F32),
            pltpu.VMEM((SB_GROUP, SB_DIM, tq), F32),
            pltpu.VMEM((SB_GROUP, 1, tq), F32),
        ],
        compiler_params=_params("parallel", "parallel", "arbitrary"),
        name="sb_attn",
    )(qk, qk, vt)


def _resident(a):
    return pl.BlockSpec(a.shape, lambda i: (0,) * a.ndim, pipeline_mode=pl.Buffered(1))


def _cast_on_first_step(pairs):
    @pl.when(pl.program_id(0) == 0)
    def _():
        for src_ref, dst_ref in pairs:
            dst_ref[...] = src_ref[...].astype(BF16)


def _out_proj_kernel(x_ref, oa_ref, ob_ref, ga_ref, gb_ref, w_ref, y_ref, wb_ref):
    _cast_on_first_step([(w_ref, wb_ref)])
    na = _rms(oa_ref[...], ga_ref[...]).astype(BF16)
    nb = _rms(ob_ref[...], gb_ref[...]).astype(BF16)
    y_ref[...] = x_ref[...] + _dot(na, wb_ref[:MLA_OUT, :]) + _dot(nb, wb_ref[MLA_OUT:, :])


def _out_proj(x, oa, ob, ga, gb, w, tm=512):
    t = x.shape[0]
    full = lambda a: pl.BlockSpec(a.shape, lambda i: (0, 0))
    return pl.pallas_call(
        _out_proj_kernel,
        grid=(t // tm,),
        in_specs=[
            pl.BlockSpec((tm, D_MODEL), lambda i: (i, 0)),
            pl.BlockSpec((tm, MLA_OUT), lambda i: (i, 0)),
            pl.BlockSpec((tm, SB_OUT), lambda i: (i, 0)),
            full(ga), full(gb), _resident(w),
        ],
        out_specs=pl.BlockSpec((tm, D_MODEL), lambda i: (i, 0)),
        out_shape=jax.ShapeDtypeStruct((t, D_MODEL), F32),
        scratch_shapes=[pltpu.VMEM(w.shape, BF16)],
        compiler_params=_params("arbitrary"),
        name="out_proj",
    )(x, oa, ob, ga, gb, w)


def _mem_kv_kernel(mem_ref, g_ref, w_ref, kv_ref):
    kv_ref[...] = _dot(_rms(mem_ref[...], g_ref[...]).astype(BF16), w_ref[...].astype(BF16)).astype(BF16)


def _mem_kv(mem, g, w):
    m = mem.shape[0]
    n = w.shape[1]
    full = lambda a: pl.BlockSpec(a.shape, lambda i: (0, 0))
    return pl.pallas_call(
        _mem_kv_kernel,
        grid=(1,),
        in_specs=[full(mem), full(g), full(w)],
        out_specs=pl.BlockSpec((m, n), lambda i: (0, 0)),
        out_shape=jax.ShapeDtypeStruct((m, n), BF16),
        compiler_params=_params("arbitrary"),
        name="mem_kv",
    )(mem, g, w)


GROUP_LANE0 = N_EXPERTS


def _mem_route_kernel(x_ref, gx_ref, wq_ref, kv_ref, wo_ref, gf_ref, wrhl_ref, br_ref,
                      x2_ref, route_ref, wqb_ref, wob_ref):
    _cast_on_first_step([(wq_ref, wqb_ref), (wo_ref, wob_ref)])
    x1 = x_ref[...]
    q = _dot(_rms(x1, gx_ref[...]).astype(BF16), wqb_ref[...]).astype(BF16)
    scale = MEM_DIM ** -0.5
    kw = MEM_HEADS * MEM_DIM
    heads = []
    for h in range(MEM_HEADS):
        lo = h * MEM_DIM
        s = _dot_nt(q[:, lo:lo + MEM_DIM], kv_ref[:, lo:lo + MEM_DIM]) * scale
        e = jnp.exp(s - jnp.max(s, axis=-1, keepdims=True))
        p = (e / jnp.sum(e, axis=-1, keepdims=True)).astype(BF16)
        heads.append(_dot(p, kv_ref[:, kw + lo:kw + lo + MEM_DIM]).astype(BF16))
    o = jnp.concatenate(heads, axis=-1)
    x2 = x1 + _dot(o, wob_ref[...])
    x2_ref[...] = x2
    h2 = _rms(x2, gf_ref[...])

    hh, hl = _split_bf16(h2)
    both = _dot(hh, wrhl_ref[...])
    lg = both[:, :LANES] + both[:, LANES:] + _dot(hl, wrhl_ref[:, :LANES]) + br_ref[...]
    lane = lax.broadcasted_iota(jnp.int32, lg.shape, 1)
    big = jnp.int32(1 << 20)
    ninf = -jnp.inf

    def lane_max(v):
        return jnp.max(v, axis=-1, keepdims=True)

    def first_lane(cond):
        return jnp.min(jnp.where(cond, lane, big), axis=-1, keepdims=True)

    is_g = (lane >= GROUP_LANE0) & (lane < GROUP_LANE0 + N_GROUPS)
    g_max = lane_max(jnp.where(is_g, lg, ninf))
    g_sum = jnp.sum(jnp.where(is_g, jnp.exp(lg - g_max), 0.0), axis=-1, keepdims=True)
    p_g = 1.0 / g_sum
    g_idx = first_lane(is_g & (lg == g_max)) - GROUP_LANE0
    in_grp = (lane < N_EXPERTS) & ((lane // EXPERTS_PER_GROUP) == g_idx)
    e_max = lane_max(jnp.where(in_grp, lg, ninf))
    e_sum = jnp.sum(jnp.where(in_grp, jnp.exp(lg - e_max), 0.0), axis=-1, keepdims=True)
    i1 = first_lane(in_grp & (lg == e_max))
    rest = in_grp & (lane != i1)
    e_max2 = lane_max(jnp.where(rest, lg, ninf))
    i2 = first_lane(rest & (lg == e_max2))
    p1 = 1.0 / e_sum
    p2 = jnp.exp(e_max2 - e_max) / e_sum
    den = p1 + p2
    gate1 = p_g * (p1 / den)
    gate2 = p_g * (p2 / den)
    route = jnp.where(lane == 0, i1.astype(F32),
                      jnp.where(lane == 1, i2.astype(F32),
                                jnp.where(lane == 2, gate1,
                                          jnp.where(lane == 3, gate2, 0.0))))
    route_ref[...] = route


def _mem_route(x1, gx, wq, kv, wo, gf, wrhl, br, seq, tm=512):
    t = x1.shape[0]
    per_batch = seq // tm
    full = lambda a: pl.BlockSpec(a.shape, lambda i: (0, 0))
    return pl.pallas_call(
        _mem_route_kernel,
        grid=(t // tm,),
        in_specs=[
            pl.BlockSpec((tm, D_MODEL), lambda i: (i, 0)),
            full(gx), _resident(wq),
            pl.BlockSpec((MEM_LEN, kv.shape[1]), lambda i: (i // per_batch, 0)),
            _resident(wo), full(gf), full(wrhl), full(br),
        ],
        out_specs=[
            pl.BlockSpec((tm, D_MODEL), lambda i: (i, 0)),
            pl.BlockSpec((tm, LANES), lambda i: (i, 0)),
        ],
        out_shape=[
            jax.ShapeDtypeStruct((t, D_MODEL), F32),
            jax.ShapeDtypeStruct((t, LANES), F32),
        ],
        scratch_shapes=[pltpu.VMEM(wq.shape, BF16), pltpu.VMEM(wo.shape, BF16)],
        compiler_params=_params("arbitrary"),
        name="mem_route",
    )(x1, gx, wq, kv, wo, gf, wrhl, br)


SLOT_BLK = 256


def _slots_kernel(route_ref, slot_ref, meta_ref, cum_ref, *, n_tok):
    nblk = n_tok // SLOT_BLK
    e_iota = lax.broadcasted_iota(jnp.int32, (LANES, SLOT_BLK), 0)
    incl = (lax.broadcasted_iota(jnp.int32, (SLOT_BLK, SLOT_BLK), 0)
            <= lax.broadcasted_iota(jnp.int32, (SLOT_BLK, SLOT_BLK), 1)).astype(BF16)

    def onehots(b):
        ts = pl.multiple_of(b * SLOT_BLK, SLOT_BLK)
        ids = route_ref[pl.ds(ts, SLOT_BLK), :].T
        oh1 = e_iota == ids[0:1, :].astype(jnp.int32)
        oh2 = e_iota == ids[1:2, :].astype(jnp.int32)
        return ts, oh1, oh2

    def count(b, carry):
        ts, oh1, oh2 = onehots(b)
        cnt = (oh1 | oh2).astype(F32).astype(BF16)
        c = _dot(cnt, incl) + carry
        cum_ref[:, pl.ds(ts, SLOT_BLK)] = c
        return c[:, SLOT_BLK - 1:SLOT_BLK]

    total = lax.fori_loop(0, nblk, count, jnp.zeros((LANES, 1), F32))
    tiles = jnp.floor((total + (SLOT_TILE - 1)) * (1.0 / SLOT_TILE))
    below = (lax.broadcasted_iota(jnp.int32, (LANES, LANES), 1)
             < lax.broadcasted_iota(jnp.int32, (LANES, LANES), 0)).astype(BF16)
    tile_lo = _dot(below, jnp.broadcast_to(tiles, (LANES, LANES)).astype(BF16))
    base = tile_lo[:, 0:1] * SLOT_TILE

    def assign(b, carry):
        ts, oh1, oh2 = onehots(b)
        pos = base + cum_ref[:, pl.ds(ts, SLOT_BLK)] - 1.0
        s1 = jnp.sum(jnp.where(oh1, pos, 0.0), axis=0, keepdims=True)
        s2 = jnp.sum(jnp.where(oh2, pos, 0.0), axis=0, keepdims=True)
        row = lax.broadcasted_iota(jnp.int32, (8, SLOT_BLK), 0)
        slot_ref[:, pl.ds(ts, SLOT_BLK)] = jnp.where(row == 0, s1, jnp.where(row == 1, s2, 0.0)).astype(jnp.int32)
        return carry

    lax.fori_loop(0, nblk, assign, 0)

    tile_hi = tile_lo + tiles
    tix = lax.broadcasted_iota(jnp.int32, (LANES, LANES), 1).astype(F32)
    is_e = lax.broadcasted_iota(jnp.int32, (LANES, LANES), 0) < N_EXPERTS
    owner = jnp.sum(jnp.where(is_e & (tile_hi <= tix), 1.0, 0.0), axis=0, keepdims=True)
    n_tiles = jnp.max(jnp.where(is_e, tile_hi, 0.0), axis=0, keepdims=True)
    valid = tix[0:1, :] < n_tiles
    last_owner = jnp.max(jnp.where(valid, owner, 0.0), axis=1, keepdims=True)
    owner = jnp.where(valid, owner, last_owner)
    eye = (lax.broadcasted_iota(jnp.int32, (LANES, LANES), 0)
           == lax.broadcasted_iota(jnp.int32, (LANES, LANES), 1))
    to_lanes = lambda colvec: jnp.sum(jnp.where(eye, colvec, 0.0), axis=0, keepdims=True)
    pad_first = to_lanes(base + total)
    pad_end = to_lanes(base + tiles * SLOT_TILE)
    row = lax.broadcasted_iota(jnp.int32, (8, LANES), 0)
    meta = jnp.where(row == 0, owner,
                     jnp.where(row == 1, valid.astype(F32),
                               jnp.where(row == 2, pad_first,
                                         jnp.where(row == 3, pad_end,
                                                   jnp.where(row == 4, n_tiles, 0.0)))))
    meta_ref[...] = meta.astype(jnp.int32)


def _slots(route):
    t = route.shape[0]
    kern = functools.partial(_slots_kernel, n_tok=t)
    return pl.pallas_call(
        kern,
        grid=(1,),
        in_specs=[pl.BlockSpec(route.shape, lambda i: (0, 0))],
        out_specs=[
            pl.BlockSpec((8, t), lambda i: (0, 0)),
            pl.BlockSpec((8, LANES), lambda i: (0, 0)),
        ],
        out_shape=[
            jax.ShapeDtypeStruct((8, t), jnp.int32),
            jax.ShapeDtypeStruct((8, LANES), jnp.int32),
        ],
        scratch_shapes=[pltpu.VMEM((LANES, t), F32)],
        compiler_params=_params("arbitrary"),
        name="slots",
    )(route)


def _row_copy(src_ref, src_row, dst_ref, dst_row, sem):
    return pltpu.make_async_copy(src_ref.at[pl.ds(src_row, 1), :], dst_ref.at[pl.ds(dst_row, 1), :], sem)


def _dispatch_kernel(s1_ref, s2_ref, pad_first_ref, pad_end_ref, n_tiles_ref, x_ref, g_ref, xs_ref, h_ref,
                     zero_ref, sem, pad_sem, *, tm):
    i = pl.program_id(0)
    t0 = i * tm
    n_slot_tiles = xs_ref.shape[0] // SLOT_TILE
    h_ref[...] = _rms(x_ref[...], g_ref[...])

    def copies(r):
        return (_row_copy(h_ref, r, xs_ref, s1_ref[t0 + r], sem),
                _row_copy(h_ref, r, xs_ref, s2_ref[t0 + r], sem))

    def issue(r, c):
        first, second = copies(r)
        first.start(priority=0)
        second.start(priority=1)
        return c

    lax.fori_loop(0, tm, issue, 0, unroll=8)

    @pl.when(i == pl.num_programs(0) - 1)
    def _():
        zero_ref[...] = jnp.zeros(zero_ref.shape, F32)

        def pad_copy(slot):
            return _row_copy(zero_ref, 0, xs_ref, slot, pad_sem)

        def fill(e, c):
            def one(s, cc):
                pad_copy(s).start()
                return cc

            return lax.fori_loop(pad_first_ref[e], pad_end_ref[e], one, c)

        lax.fori_loop(0, N_EXPERTS, fill, 0)

        def tile_copy(tile):
            return pltpu.make_async_copy(zero_ref, xs_ref.at[pl.ds(tile * SLOT_TILE, SLOT_TILE), :], pad_sem)

        def fill_tile(tile, c):
            tile_copy(tile).start()
            return c

        lax.fori_loop(n_tiles_ref[0], n_slot_tiles, fill_tile, 0)

        def settle(e, c):
            def one(s, cc):
                pad_copy(s).wait()
                return cc

            return lax.fori_loop(pad_first_ref[e], pad_end_ref[e], one, c)

        lax.fori_loop(0, N_EXPERTS, settle, 0)

        def settle_tile(tile, c):
            tile_copy(tile).wait()
            return c

        lax.fori_loop(n_tiles_ref[0], n_slot_tiles, settle_tile, 0)

    def drain(r, c):
        for cp in copies(r):
            cp.wait()
        return c

    lax.fori_loop(0, tm, drain, 0, unroll=8)


def _dispatch(slot1, slot2, pad_first, pad_end, n_tiles, x2, g, n_slots, tm=1024):
    t = x2.shape[0]
    kern = functools.partial(_dispatch_kernel, tm=tm)
    return pl.pallas_call(
        kern,
        grid_spec=pltpu.PrefetchScalarGridSpec(
            num_scalar_prefetch=5,
            grid=(t // tm,),
            in_specs=[pl.BlockSpec((tm, D_MODEL), lambda i, *_: (i, 0)),
                      pl.BlockSpec((1, D_MODEL), lambda i, *_: (0, 0))],
            out_specs=pl.BlockSpec(memory_space=pl.ANY),
            scratch_shapes=[
                pltpu.VMEM((tm, D_MODEL), F32),
                pltpu.VMEM((SLOT_TILE, D_MODEL), F32),
                pltpu.SemaphoreType.DMA(()),
                pltpu.SemaphoreType.DMA(()),
            ],
        ),
        out_shape=jax.ShapeDtypeStruct((n_slots, D_MODEL), F32),
        compiler_params=_params("arbitrary"),
        name="dispatch",
    )(slot1, slot2, pad_first, pad_end, n_tiles, x2, g)


def _experts_kernel(own_ref, valid_ref, xs_ref, wg_hbm, wu_hbm, wd_hbm, ys_ref,
                    wgf, wuf, wdf, wgb, wub, wdb, wsem, wslot_ref):
    i = pl.program_id(0)
    nt = pl.num_programs(0)
    at = lambda ref, j: ref[jnp.minimum(j, nt - 1)]

    def weight_copies(e, s):
        return (pltpu.make_async_copy(wg_hbm.at[e], wgf.at[s], wsem.at[s]),
                pltpu.make_async_copy(wu_hbm.at[e], wuf.at[s], wsem.at[s]),
                pltpu.make_async_copy(wd_hbm.at[e], wdf.at[s], wsem.at[s]))

    def next_expert_tile(j0):
        e0 = at(own_ref, j0)
        return lax.while_loop(lambda j: (j < nt) & (at(own_ref, j) == e0), lambda j: j + 1, j0 + 1)

    def start_weights(j, s):
        @pl.when((j < nt) & (at(valid_ref, j) > 0))
        def _():
            for cp in weight_copies(at(own_ref, j), s):
                cp.start()

    @pl.when(i == 0)
    def _():
        wslot_ref[0] = 0
        j = 0
        for s in range(WEIGHT_SLOTS - 1):
            start_weights(j, s)
            j = next_expert_tile(j)

    @pl.when(valid_ref[i] > 0)
    def _():
        e = own_ref[i]

        @pl.when((i == 0) | (own_ref[jnp.maximum(i - 1, 0)] != e))
        def _():
            s = wslot_ref[0]
            for cp in weight_copies(e, s):
                cp.wait()
            j = i
            for _ in range(WEIGHT_SLOTS - 1):
                j = next_expert_tile(j)
            start_weights(j, (s + WEIGHT_SLOTS - 1) % WEIGHT_SLOTS)

            wgb[...] = wgf[s].astype(BF16)
            wub[...] = wuf[s].astype(BF16)
            wdb[...] = wdf[s].astype(BF16)
            wslot_ref[0] = (s + 1) % WEIGHT_SLOTS

        x = xs_ref[...].astype(BF16)
        a = _dot(x, wgb[...])
        u = _dot(x, wub[...])
        act = (a * jax.nn.sigmoid(a) * u).astype(BF16)
        ys_ref[...] = _dot(act, wdb[...])

    @pl.when(valid_ref[i] == 0)
    def _():
        ys_ref[...] = jnp.zeros(ys_ref.shape, F32)


def _experts(own, valid, xs, wg, wu, wd):
    n_tiles = xs.shape[0] // SLOT_TILE
    hbm = pl.BlockSpec(memory_space=pl.ANY)
    tile = (SLOT_TILE, D_MODEL)
    return pl.pallas_call(
        _experts_kernel,
        grid_spec=pltpu.PrefetchScalarGridSpec(
            num_scalar_prefetch=2,
            grid=(n_tiles,),
            in_specs=[
                pl.BlockSpec(tile, lambda i, o, v: (jnp.where(v[i] > 0, i, 0), 0)),
                hbm, hbm, hbm,
            ],
            out_specs=pl.BlockSpec(tile, lambda i, o, v: (i, 0)),
            scratch_shapes=[
                pltpu.VMEM((WEIGHT_SLOTS, D_MODEL, D_EXPERT), F32),
                pltpu.VMEM((WEIGHT_SLOTS, D_MODEL, D_EXPERT), F32),
                pltpu.VMEM((WEIGHT_SLOTS, D_EXPERT, D_MODEL), F32),
                pltpu.VMEM((D_MODEL, D_EXPERT), BF16),
                pltpu.VMEM((D_MODEL, D_EXPERT), BF16),
                pltpu.VMEM((D_EXPERT, D_MODEL), BF16),
                pltpu.SemaphoreType.DMA((WEIGHT_SLOTS,)),
                pltpu.SMEM((1,), jnp.int32),
            ],
        ),
        out_shape=jax.ShapeDtypeStruct(xs.shape, F32),
        compiler_params=_params("arbitrary"),
        name="experts",
    )(own, valid, xs, wg, wu, wd)


def _combine_kernel(s1_ref, s2_ref, x_ref, route_ref, g_ref, ys_ref, y_ref, buf_ref, sem, *, tm):
    i = pl.program_id(0)
    n = pl.num_programs(0)

    def copies(tile, p, r):
        t = tile * tm + r
        return (_row_copy(ys_ref, s1_ref[t], buf_ref.at[p, 0], r, sem.at[p]),
                _row_copy(ys_ref, s2_ref[t], buf_ref.at[p, 1], r, sem.at[p]))

    def gather_start(tile, p):
        def issue(r, c):
            first, second = copies(tile, p, r)
            first.start(priority=0)
            second.start(priority=1)
            return c

        lax.fori_loop(0, tm, issue, 0, unroll=8)

    def gather_wait(tile, p):
        def drain(r, c):
            for cp in copies(tile, p, r):
                cp.wait()
            return c

        lax.fori_loop(0, tm, drain, 0, unroll=8)

    @pl.when(i == 0)
    def _():
        gather_start(0, 0)

    p = i % 2

    @pl.when(i + 1 < n)
    def _():
        gather_start(i + 1, 1 - p)

    gather_wait(i, p)
    y = x_ref[...] + route_ref[:, 2:3] * buf_ref[p, 0] + route_ref[:, 3:4] * buf_ref[p, 1]
    y_ref[...] = _rms(y, g_ref[...])


def _combine(slot1, slot2, x2, route, g, ys, tm=256):
    t = x2.shape[0]
    kern = functools.partial(_combine_kernel, tm=tm)
    return pl.pallas_call(
        kern,
        grid_spec=pltpu.PrefetchScalarGridSpec(
            num_scalar_prefetch=2,
            grid=(t // tm,),
            in_specs=[
                pl.BlockSpec((tm, D_MODEL), lambda i, s1, s2: (i, 0)),
                pl.BlockSpec((tm, LANES), lambda i, s1, s2: (i, 0)),
                pl.BlockSpec((1, D_MODEL), lambda i, s1, s2: (0, 0)),
                pl.BlockSpec(memory_space=pl.ANY),
            ],
            out_specs=pl.BlockSpec((tm, D_MODEL), lambda i, s1, s2: (i, 0)),
            scratch_shapes=[
                pltpu.VMEM((2, 2, tm, D_MODEL), F32),
                pltpu.SemaphoreType.DMA((2,)),
            ],
        ),
        out_shape=jax.ShapeDtypeStruct((t, D_MODEL), F32),
        compiler_params=_params("arbitrary"),
        name="combine",
    )(slot1, slot2, x2, route, g, ys)


def _row(v):
    return v.reshape(1, -1).astype(F32)


def kernel(x, mem, positions, norm_mix, w_in, norm_q_lat, w_q_b, norm_kv_lat, w_kv_b, norm_mla_out, norm_sb_out, w_out, norm_mem_x, norm_mem_src, w_mem_q, w_mem_kv, w_mem_o, norm_ffn, w_group, b_group, w_expert_router, b_expert, w_gate, w_up, w_down, norm_final):
    batch, seq, d = x.shape
    t = batch * seq
    depth = w_in.shape[0]
    assert depth == 1, "single-layer trunk only"
    xt = x.reshape(t, d)
    pos = positions.reshape(t, 1)
    inv_freq = ROPE_THETA ** (-jnp.arange(0, MLA_ROPE, 2, dtype=F32) / MLA_ROPE)
    invf = jnp.concatenate([inv_freq, inv_freq, jnp.zeros((LANES - MLA_ROPE,), F32)]).reshape(1, LANES)
    n_slots = (2 * t // SLOT_TILE + N_EXPERTS) * SLOT_TILE

    for l in range(depth):
        sb0 = MLA_Q_RANK + MLA_KV_RANK + MLA_ROPE
        row_scale = jnp.concatenate([jnp.ones((sb0,), F32), jnp.full((SB_OUT,), SB_DIM ** -0.5 * LOG2E, F32),
                                     jnp.ones((2 * SB_OUT,), F32)])
        w_in_t = (w_in[l].T * row_scale[:, None]).astype(BF16)
        wq = w_q_b[l].reshape(MLA_Q_RANK, MLA_HEADS, MLA_QK) * (MLA_QK ** -0.5 * LOG2E)
        zq = jnp.zeros((MLA_Q_RANK, MLA_HEADS, MLA_PAD - MLA_QK), F32)
        wqa = jnp.concatenate([wq, zq], axis=-1).reshape(MLA_Q_RANK, MLA_HEADS * MLA_PAD).astype(BF16)
        wkv = w_kv_b[l].reshape(MLA_KV_RANK, MLA_HEADS, MLA_NOPE + MLA_V)
        wk = wkv[:, :, :MLA_NOPE].reshape(MLA_KV_RANK, MLA_OUT).astype(BF16)
        wvt = wkv[:, :, MLA_NOPE:].reshape(MLA_KV_RANK, MLA_OUT).T.astype(BF16)
        w_router = jnp.concatenate(
            [w_expert_router[l], w_group[l], jnp.zeros((d, LANES - N_EXPERTS - N_GROUPS), F32)], axis=1)
        wr_hi = w_router.astype(BF16)
        wr_hl = jnp.concatenate([wr_hi, (w_router - wr_hi.astype(F32)).astype(BF16)], axis=1)
        b_router = jnp.concatenate(
            [b_expert[l].astype(F32), b_group[l].astype(F32), jnp.zeros((LANES - N_EXPERTS - N_GROUPS,), F32)]
        ).reshape(1, LANES)

        lat, sb_qk, sb_vt = _proj_in(xt, _row(norm_mix[l]), w_in_t, sb0)
        q, k, vt = _mla_proj(lat, pos, invf, _row(norm_q_lat[l]), _row(norm_kv_lat[l]), wqa, wk, wvt)
        o_mla = _mla_attn(q, k, vt, batch, seq)
        o_sb = _sb_attn(sb_qk, sb_vt, batch, seq)
        x1 = _out_proj(xt, o_mla, o_sb, _row(norm_mla_out[l]), _row(norm_sb_out[l]), w_out[l])

        kv = _mem_kv(mem.reshape(batch * MEM_LEN, d), _row(norm_mem_src[l]), w_mem_kv[l])
        x2, route = _mem_route(x1, _row(norm_mem_x[l]), w_mem_q[l], kv,
                               w_mem_o[l], _row(norm_ffn[l]), wr_hl, b_router, seq)

        slots, meta = _slots(route)
        xs = _dispatch(slots[0], slots[1], meta[2], meta[3], meta[4], x2, _row(norm_ffn[l]), n_slots)
        ys = _experts(meta[0], meta[1], xs, w_gate[l], w_up[l], w_down[l])
        xt = _combine(slots[0], slots[1], x2, route, _row(norm_final), ys)
    return xt.reshape(batch, seq, d)
```

```python
import functools

import jax
import jax.numpy as jnp
from jax import lax
from jax.experimental import pallas as pl
from jax.experimental.pallas import tpu as pltpu

F32 = jnp.float32
BF16 = jnp.bfloat16

EPS = 1e-6
ROPE_THETA = 10000.0

D_MODEL = 2048
MEM_LEN = 256
MLA_HEADS = 8
MLA_NOPE = 128
MLA_ROPE = 64
MLA_QK = MLA_NOPE + MLA_ROPE
MLA_V = 128
MLA_Q_RANK = 512
MLA_KV_RANK = 256
MLA_PAD = 256
SB_HEADS = 8
SB_DIM = 128
MLA_OUT = MLA_HEADS * MLA_V
SB_OUT = SB_HEADS * SB_DIM
MEM_HEADS = 4
MEM_DIM = 128
N_GROUPS = 4
EXPERTS_PER_GROUP = 8
N_EXPERTS = N_GROUPS * EXPERTS_PER_GROUP
D_EXPERT = 512

LANES = 128
LAT_COLS = 1024

SLOT_TILE = 256
WEIGHT_SLOTS = 2
VMEM_LIMIT = 56 * 1024 * 1024


def _rms(x, g):
    return x * lax.rsqrt(jnp.mean(x * x, axis=-1, keepdims=True) + EPS) * g


def _dot(a, b):
    return jnp.dot(a, b, preferred_element_type=F32)


def _dot_nt(a, b):
    return lax.dot_general(a, b, (((1,), (1,)), ((), ())), preferred_element_type=F32)


def _split_bf16(x):
    hi = x.astype(BF16)
    lo = (x - hi.astype(F32)).astype(BF16)
    return hi, lo


def _params(*sem):
    return pltpu.CompilerParams(dimension_semantics=sem, vmem_limit_bytes=VMEM_LIMIT)


def _proj_in_kernel(x_ref, g_ref, wl_ref, ws_ref, lat_ref, qk_ref, vt_ref, h_ref):
    j = pl.program_id(1)
    last = pl.num_programs(1) - 1

    @pl.when(j == 0)
    def _():
        h_ref[...] = _rms(x_ref[...], g_ref[...]).astype(BF16)
        lat_ref[...] = _dot_nt(h_ref[...], wl_ref[...])

    @pl.when((j > 0) & (j < last))
    def _():
        qk_ref[...] = _dot_nt(h_ref[...], ws_ref[...]).astype(BF16)

    @pl.when(j == last)
    def _():
        vt_ref[...] = _dot_nt(ws_ref[...], h_ref[...]).astype(BF16)


def _proj_in(x, g, w_t, sb0, tm=1024):
    t = x.shape[0]
    tn = LAT_COLS
    n_sb = (w_t.shape[0] - sb0) // tn
    return pl.pallas_call(
        _proj_in_kernel,
        grid=(t // tm, n_sb + 1),
        in_specs=[
            pl.BlockSpec((tm, D_MODEL), lambda i, j: (i, 0)),
            pl.BlockSpec((1, D_MODEL), lambda i, j: (0, 0)),
            pl.BlockSpec((tn, D_MODEL), lambda i, j: (0, 0), pipeline_mode=pl.Buffered(1)),
            pl.BlockSpec((pl.Element(tn), pl.Element(D_MODEL)),
                         lambda i, j: (pl.multiple_of(sb0 + jnp.maximum(j - 1, 0) * tn, MLA_ROPE), 0)),
        ],
        out_specs=[
            pl.BlockSpec((tm, tn), lambda i, j: (i, 0)),
            pl.BlockSpec((tm, tn), lambda i, j: (i, jnp.clip(j - 1, 0, n_sb - 2))),
            pl.BlockSpec((SB_OUT, tm), lambda i, j: (0, i)),
        ],
        out_shape=[
            jax.ShapeDtypeStruct((t, LAT_COLS), F32),
            jax.ShapeDtypeStruct((t, 2 * SB_OUT), BF16),
            jax.ShapeDtypeStruct((SB_OUT, t), BF16),
        ],
        scratch_shapes=[pltpu.VMEM((tm, D_MODEL), BF16)],
        compiler_params=_params("parallel", "arbitrary"),
        name="proj_in",
    )(x, g, w_t, w_t)


def _mla_proj_kernel(lat_ref, pos_ref, invf_ref, gq_ref, gkv_ref, wqa_ref, wk_ref, wvt_ref,
                     q_ref, k_ref, vt_ref):
    cq = _rms(lat_ref[:, :MLA_Q_RANK], gq_ref[...]).astype(BF16)
    ckv = _rms(lat_ref[:, MLA_Q_RANK:MLA_Q_RANK + MLA_KV_RANK], gkv_ref[...]).astype(BF16)
    ang = pos_ref[...].astype(F32) * invf_ref[...]
    lane = lax.broadcasted_iota(jnp.int32, ang.shape, 1)
    live = lane < MLA_ROPE
    cos2 = jnp.where(live, jnp.cos(ang), 0.0)
    sin2 = jnp.where(live, jnp.sin(ang), 0.0)
    half = MLA_ROPE // 2

    def swapped(x):
        return jnp.where(lane < half, -pltpu.roll(x, LANES - half, 1), pltpu.roll(x, half, 1))

    qa = _dot(cq, wqa_ref[...])
    kn = _dot(ckv, wk_ref[...])
    vt_ref[...] = _dot_nt(wvt_ref[...], ckv).astype(BF16)
    k_raw = lat_ref[:, MLA_Q_RANK + MLA_KV_RANK:MLA_Q_RANK + MLA_KV_RANK + LANES]
    k_pe = jnp.where(live, k_raw * cos2 + swapped(k_raw) * sin2, 0.0).astype(BF16)
    for h in range(MLA_HEADS):
        lo = h * MLA_PAD
        mid = lo + LANES
        q_ref[:, lo:mid] = qa[:, lo:mid].astype(BF16)
        q_pe = qa[:, mid:mid + LANES]
        q_ref[:, mid:mid + LANES] = (q_pe * cos2 + swapped(q_pe) * sin2).astype(BF16)
        k_ref[:, lo:mid] = kn[:, h * LANES:(h + 1) * LANES].astype(BF16)
        k_ref[:, mid:mid + LANES] = k_pe


def _mla_proj(lat, pos, invf, gq, gkv, wqa, wk, wvt, tm=512):
    t = lat.shape[0]
    full = lambda a: pl.BlockSpec(a.shape, lambda i: (0, 0))
    return pl.pallas_call(
        _mla_proj_kernel,
        grid=(t // tm,),
        in_specs=[
            pl.BlockSpec((tm, LAT_COLS), lambda i: (i, 0)),
            pl.BlockSpec((tm, 1), lambda i: (i, 0)),
            full(invf), full(gq), full(gkv), full(wqa), full(wk), full(wvt),
        ],
        out_specs=[
            pl.BlockSpec((tm, MLA_HEADS * MLA_PAD), lambda i: (i, 0)),
            pl.BlockSpec((tm, MLA_HEADS * MLA_PAD), lambda i: (i, 0)),
            pl.BlockSpec((MLA_OUT, tm), lambda i: (0, i)),
        ],
        out_shape=[
            jax.ShapeDtypeStruct((t, MLA_HEADS * MLA_PAD), BF16),
            jax.ShapeDtypeStruct((t, MLA_HEADS * MLA_PAD), BF16),
            jax.ShapeDtypeStruct((MLA_OUT, t), BF16),
        ],
        compiler_params=_params("parallel"),
        name="mla_proj",
    )(lat, pos, invf, gq, gkv, wqa, wk, wvt)


LOG2E = 1.4426950408889634


def _three_stage(n_pairs, stage_a, stage_b, stage_c):
    a_issue, a_finish = stage_a
    b_issue, b_finish = stage_b
    c_issue, c_finish = stage_c

    def run_a(n, slot, first):
        a_finish(n, slot, a_issue(n, slot, first), first)

    run_a(0, 0, True)
    run_a(1, 1, True)
    b_finish(0, 0, b_issue(0, 0))

    def half(na, sa, nb, sb, nc, sc):
        ra = a_issue(na, sa, False)
        rb = b_issue(nb, sb)
        rc = c_issue(nc, sc)
        a_finish(na, sa, ra, False)
        b_finish(nb, sb, rb)
        c_finish(nc, sc, rc)

    def body(p, carry):
        half(2 * p, 0, 2 * p - 1, 1, 2 * p - 2, 0)
        half(2 * p + 1, 1, 2 * p, 0, 2 * p - 1, 1)
        return carry

    lax.fori_loop(1, n_pairs + 1, body, 0)
    last = 2 * n_pairs + 1
    rb = b_issue(last, 1)
    rc = c_issue(last - 1, 0)
    b_finish(last, 1, rb)
    c_finish(last - 1, 0, rc)
    c_finish(last, 1, c_issue(last, 1))


def _two_stage(n_pairs, produce, consume):
    produce(0, 0, True)
    produce(1, 1, True)
    consume(0, 0)

    def body(p, carry):
        produce(2 * p, 0, False)
        consume(2 * p - 1, 1)
        produce(2 * p + 1, 1, False)
        consume(2 * p, 0)
        return carry

    lax.fori_loop(1, n_pairs + 1, body, 0)
    consume(2 * n_pairs + 1, 1)


MLA_SUM_ROWS = 16
MLA_GROUP = 8


def _mla_attn_kernel(q_ref, k_ref, vt_ref, o_ref, s_ref, m_ref, acc_ref, *, tq, tk):
    i = pl.program_id(2)
    heads = range(MLA_GROUP)
    dq, dv = MLA_PAD, MLA_V
    m_ref[...] = jnp.full(m_ref.shape, -jnp.inf, F32)
    acc_ref[...] = jnp.zeros(acc_ref.shape, F32)
    ones = jnp.ones((MLA_SUM_ROWS, tk), BF16)

    def key_start(n):
        tile = jnp.where(n < 2, 2 * i + n, 2 * i + 1 - n)
        return pl.multiple_of(tile * tk, tk)

    def produce(n, slot, diagonal):
        ks = key_start(n)
        for g in heads:
            st = _dot_nt(k_ref[pl.ds(ks, tk), g * dq:(g + 1) * dq], q_ref[:, g * dq:(g + 1) * dq])
            if diagonal:
                kpos = ks + lax.broadcasted_iota(jnp.int32, st.shape, 0)
                qpos = i * tq + lax.broadcasted_iota(jnp.int32, st.shape, 1)
                st = jnp.where(kpos <= qpos, st, -jnp.inf)
            s_ref[g, slot] = st

    def consume(n, slot):
        ks = key_start(n)
        for g in heads:
            st = s_ref[g, slot]
            m_old = m_ref[g]
            m_new = jnp.maximum(m_old, jnp.max(st, axis=0, keepdims=True))
            alpha = jnp.exp2(m_old - m_new)
            p = jnp.exp2(st - m_new).astype(BF16)
            v_ones = jnp.concatenate([vt_ref[g * dv:(g + 1) * dv, pl.ds(ks, tk)], ones], axis=0)
            acc_ref[g] = alpha * acc_ref[g] + _dot(v_ones, p)
            m_ref[g] = m_new

    _two_stage(i, produce, consume)
    for g in heads:
        o_ref[:, g * dv:(g + 1) * dv] = (acc_ref[g, :dv, :] / acc_ref[g, dv:dv + 1, :]).T


def _mla_attn(q, k, vt, batch, seq, tq=512):
    tk = tq // 2
    nq = seq // tq
    groups = MLA_HEADS // MLA_GROUP
    kern = functools.partial(_mla_attn_kernel, tq=tq, tk=tk)
    return pl.pallas_call(
        kern,
        grid=(batch, groups, nq),
        in_specs=[
            pl.BlockSpec((tq, MLA_GROUP * MLA_PAD), lambda b, h, i: (b * nq + i, h)),
            pl.BlockSpec((seq, MLA_GROUP * MLA_PAD), lambda b, h, i: (b, h), pipeline_mode=pl.Buffered(1)),
            pl.BlockSpec((MLA_GROUP * MLA_V, seq), lambda b, h, i: (h, b), pipeline_mode=pl.Buffered(1)),
        ],
        out_specs=pl.BlockSpec((tq, MLA_GROUP * MLA_V), lambda b, h, i: (b * nq + i, h)),
        out_shape=jax.ShapeDtypeStruct((batch * seq, MLA_OUT), F32),
        scratch_shapes=[
            pltpu.VMEM((MLA_GROUP, 2, tk, tq), F32),
            pltpu.VMEM((MLA_GROUP, 1, tq), F32),
            pltpu.VMEM((MLA_GROUP, MLA_V + MLA_SUM_ROWS, tq), F32),
        ],
        compiler_params=_params("parallel", "parallel", "arbitrary"),
        name="mla_attn",
    )(q, k, vt)


SB_EXP_CLAMP = 126.0
SB_GROUP = 4


def _sb_attn_kernel(q_ref, k_ref, vt_ref, o_ref, z_ref, hl_ref, arg_ref, acc_ref, c_ref, *, tq, tk):
    i = pl.program_id(2)
    heads = range(SB_GROUP)
    d = SB_DIM
    acc_ref[...] = jnp.zeros(acc_ref.shape, F32)
    c_ref[...] = jnp.zeros(c_ref.shape, F32)
    col = lax.broadcasted_iota(jnp.int32, (tk, tk), 1)
    row = lax.broadcasted_iota(jnp.int32, (tk, tk), 0)
    neg_tri = jnp.where(col >= row, -1.0, 0.0).astype(BF16)

    def key_start(n):
        return pl.multiple_of((2 * i + 1 - n) * tk, tk)

    def a_issue(n, slot, masked):
        ks = key_start(n)
        return [_dot_nt(k_ref[pl.ds(ks, tk), g * d:(g + 1) * d], q_ref[:, g * d:(g + 1) * d])
                for g in heads]

    def a_finish(n, slot, z2s, masked):
        for g in heads:
            z2 = z2s[g]
            sp = jnp.maximum(z2, jnp.log2(1.0 + jnp.exp2(jnp.minimum(z2, SB_EXP_CLAMP))))
            if masked:
                kpos = key_start(n) + lax.broadcasted_iota(jnp.int32, z2.shape, 0)
                qpos = i * tq + lax.broadcasted_iota(jnp.int32, z2.shape, 1)
                mask = kpos < qpos
                sp = jnp.where(mask, sp, 0.0)
                z2 = jnp.where(mask, z2, -jnp.inf)
            hl_ref[g, slot] = sp.astype(BF16)
            z_ref[g, slot] = z2

    def b_issue(n, slot):
        return [_dot(neg_tri, hl_ref[g, slot]) for g in heads]

    def b_finish(n, slot, laters):
        for g in heads:
            arg_ref[g, slot] = z_ref[g, slot] + laters[g] + c_ref[g]
            c_ref[g] += laters[g][0:1, :]

    def c_issue(n, slot):
        ks = key_start(n)
        return [_dot(vt_ref[g * d:(g + 1) * d, pl.ds(ks, tk)], jnp.exp2(arg_ref[g, slot]).astype(BF16))
                for g in heads]

    def c_finish(n, slot, pvs):
        for g in heads:
            acc_ref[g] += pvs[g]

    _three_stage(i, (a_issue, a_finish), (b_issue, b_finish), (c_issue, c_finish))
    for g in heads:
        o_ref[:, g * d:(g + 1) * d] = acc_ref[g].T


def _sb_attn(qk, vt, batch, seq, tq=512):
    tk = tq // 2
    nq = seq // tq
    groups = SB_HEADS // SB_GROUP
    gd = SB_GROUP * SB_DIM
    kern = functools.partial(_sb_attn_kernel, tq=tq, tk=tk)
    return pl.pallas_call(
        kern,
        grid=(batch, groups, nq),
        in_specs=[
            pl.BlockSpec((tq, gd), lambda b, h, i: (b * nq + i, h)),
            pl.BlockSpec((seq, gd), lambda b, h, i: (b, groups + h)),
            pl.BlockSpec((gd, seq), lambda b, h, i: (h, b)),
        ],
        out_specs=pl.BlockSpec((tq, gd), lambda b, h, i: (b * nq + i, h)),
        out_shape=jax.ShapeDtypeStruct((batch * seq, SB_OUT), F32),
        scratch_shapes=[
            pltpu.VMEM((SB_GROUP, 2, tk, tq), F32),
            pltpu.VMEM((SB_GROUP, 2, tk, tq), BF16),
            pltpu.VMEM((SB_GROUP, 2, tk, tq), F32),
            pltpu.VMEM((SB_GROUP, SB_DIM, tq), F32),
            pltpu.VMEM((SB_GROUP, 1, tq), F32),
        ],
        compiler_params=_params("parallel", "parallel", "arbitrary"),
        name="sb_attn",
    )(qk, qk, vt)


def _resident(a):
    return pl.BlockSpec(a.shape, lambda i: (0,) * a.ndim, pipeline_mode=pl.Buffered(1))


def _cast_on_first_step(pairs):
    @pl.when(pl.program_id(0) == 0)
    def _():
        for src_ref, dst_ref in pairs:
            dst_ref[...] = src_ref[...].astype(BF16)


def _out_proj_kernel(x_ref, oa_ref, ob_ref, ga_ref, gb_ref, w_ref, y_ref, wb_ref):
    _cast_on_first_step([(w_ref, wb_ref)])
    na = _rms(oa_ref[...], ga_ref[...]).astype(BF16)
    nb = _rms(ob_ref[...], gb_ref[...]).astype(BF16)
    y_ref[...] = x_ref[...] + _dot(na, wb_ref[:MLA_OUT, :]) + _dot(nb, wb_ref[MLA_OUT:, :])


def _out_proj(x, oa, ob, ga, gb, w, tm=512):
    t = x.shape[0]
    full = lambda a: pl.BlockSpec(a.shape, lambda i: (0, 0))
    return pl.pallas_call(
        _out_proj_kernel,
        grid=(t // tm,),
        in_specs=[
            pl.BlockSpec((tm, D_MODEL), lambda i: (i, 0)),
            pl.BlockSpec((tm, MLA_OUT), lambda i: (i, 0)),
            pl.BlockSpec((tm, SB_OUT), lambda i: (i, 0)),
            full(ga), full(gb), _resident(w),
        ],
        out_specs=pl.BlockSpec((tm, D_MODEL), lambda i: (i, 0)),
        out_shape=jax.ShapeDtypeStruct((t, D_MODEL), F32),
        scratch_shapes=[pltpu.VMEM(w.shape, BF16)],
        compiler_params=_params("arbitrary"),
        name="out_proj",
    )(x, oa, ob, ga, gb, w)


def _mem_kv_kernel(mem_ref, g_ref, w_ref, kv_ref):
    kv_ref[...] = _dot(_rms(mem_ref[...], g_ref[...]).astype(BF16), w_ref[...].astype(BF16)).astype(BF16)


def _mem_kv(mem, g, w):
    m = mem.shape[0]
    n = w.shape[1]
    full = lambda a: pl.BlockSpec(a.shape, lambda i: (0, 0))
    return pl.pallas_call(
        _mem_kv_kernel,
        grid=(1,),
        in_specs=[full(mem), full(g), full(w)],
        out_specs=pl.BlockSpec((m, n), lambda i: (0, 0)),
        out_shape=jax.ShapeDtypeStruct((m, n), BF16),
        compiler_params=_params("arbitrary"),
        name="mem_kv",
    )(mem, g, w)


GROUP_LANE0 = N_EXPERTS


def _mem_route_kernel(x_ref, gx_ref, wq_ref, kv_ref, wo_ref, gf_ref, wrhl_ref, br_ref,
                      x2_ref, h2_ref, route_ref, wqb_ref, wob_ref):
    _cast_on_first_step([(wq_ref, wqb_ref), (wo_ref, wob_ref)])
    x1 = x_ref[...]
    q = _dot(_rms(x1, gx_ref[...]).astype(BF16), wqb_ref[...]).astype(BF16)
    scale = MEM_DIM ** -0.5
    kw = MEM_HEADS * MEM_DIM
    heads = []
    for h in range(MEM_HEADS):
        lo = h * MEM_DIM
        s = _dot_nt(q[:, lo:lo + MEM_DIM], kv_ref[:, lo:lo + MEM_DIM]) * scale
        e = jnp.exp(s - jnp.max(s, axis=-1, keepdims=True))
        p = (e / jnp.sum(e, axis=-1, keepdims=True)).astype(BF16)
        heads.append(_dot(p, kv_ref[:, kw + lo:kw + lo + MEM_DIM]).astype(BF16))
    o = jnp.concatenate(heads, axis=-1)
    x2 = x1 + _dot(o, wob_ref[...])
    x2_ref[...] = x2
    h2 = _rms(x2, gf_ref[...])
    h2_ref[...] = h2

    hh, hl = _split_bf16(h2)
    both = _dot(hh, wrhl_ref[...])
    lg = both[:, :LANES] + both[:, LANES:] + _dot(hl, wrhl_ref[:, :LANES]) + br_ref[...]
    lane = lax.broadcasted_iota(jnp.int32, lg.shape, 1)
    big = jnp.int32(1 << 20)
    ninf = -jnp.inf

    def lane_max(v):
        return jnp.max(v, axis=-1, keepdims=True)

    def first_lane(cond):
        return jnp.min(jnp.where(cond, lane, big), axis=-1, keepdims=True)

    is_g = (lane >= GROUP_LANE0) & (lane < GROUP_LANE0 + N_GROUPS)
    g_max = lane_max(jnp.where(is_g, lg, ninf))
    g_sum = jnp.sum(jnp.where(is_g, jnp.exp(lg - g_max), 0.0), axis=-1, keepdims=True)
    p_g = 1.0 / g_sum
    g_idx = first_lane(is_g & (lg == g_max)) - GROUP_LANE0
    in_grp = (lane < N_EXPERTS) & ((lane // EXPERTS_PER_GROUP) == g_idx)
    e_max = lane_max(jnp.where(in_grp, lg, ninf))
    e_sum = jnp.sum(jnp.where(in_grp, jnp.exp(lg - e_max), 0.0), axis=-1, keepdims=True)
    i1 = first_lane(in_grp & (lg == e_max))
    rest = in_grp & (lane != i1)
    e_max2 = lane_max(jnp.where(rest, lg, ninf))
    i2 = first_lane(rest & (lg == e_max2))
    p1 = 1.0 / e_sum
    p2 = jnp.exp(e_max2 - e_max) / e_sum
    den = p1 + p2
    gate1 = p_g * (p1 / den)
    gate2 = p_g * (p2 / den)
    route = jnp.where(lane == 0, i1.astype(F32),
                      jnp.where(lane == 1, i2.astype(F32),
                                jnp.where(lane == 2, gate1,
                                          jnp.where(lane == 3, gate2, 0.0))))
    route_ref[...] = route


def _mem_route(x1, gx, wq, kv, wo, gf, wrhl, br, seq, tm=512):
    t = x1.shape[0]
    per_batch = seq // tm
    full = lambda a: pl.BlockSpec(a.shape, lambda i: (0, 0))
    return pl.pallas_call(
        _mem_route_kernel,
        grid=(t // tm,),
        in_specs=[
            pl.BlockSpec((tm, D_MODEL), lambda i: (i, 0)),
            full(gx), _resident(wq),
            pl.BlockSpec((MEM_LEN, kv.shape[1]), lambda i: (i // per_batch, 0)),
            _resident(wo), full(gf), full(wrhl), full(br),
        ],
        out_specs=[
            pl.BlockSpec((tm, D_MODEL), lambda i: (i, 0)),
            pl.BlockSpec((tm, D_MODEL), lambda i: (i, 0)),
            pl.BlockSpec((tm, LANES), lambda i: (i, 0)),
        ],
        out_shape=[
            jax.ShapeDtypeStruct((t, D_MODEL), F32),
            jax.ShapeDtypeStruct((t, D_MODEL), F32),
            jax.ShapeDtypeStruct((t, LANES), F32),
        ],
        scratch_shapes=[pltpu.VMEM(wq.shape, BF16), pltpu.VMEM(wo.shape, BF16)],
        compiler_params=_params("arbitrary"),
        name="mem_route",
    )(x1, gx, wq, kv, wo, gf, wrhl, br)


SLOT_BLK = 256


def _slots_kernel(route_ref, slot_ref, meta_ref, cum_ref, *, n_tok):
    nblk = n_tok // SLOT_BLK
    e_iota = lax.broadcasted_iota(jnp.int32, (LANES, SLOT_BLK), 0)
    incl = (lax.broadcasted_iota(jnp.int32, (SLOT_BLK, SLOT_BLK), 0)
            <= lax.broadcasted_iota(jnp.int32, (SLOT_BLK, SLOT_BLK), 1)).astype(BF16)

    def onehots(b):
        ts = pl.multiple_of(b * SLOT_BLK, SLOT_BLK)
        ids = route_ref[pl.ds(ts, SLOT_BLK), :].T
        oh1 = e_iota == ids[0:1, :].astype(jnp.int32)
        oh2 = e_iota == ids[1:2, :].astype(jnp.int32)
        return ts, oh1, oh2

    def count(b, carry):
        ts, oh1, oh2 = onehots(b)
        cnt = (oh1 | oh2).astype(F32).astype(BF16)
        c = _dot(cnt, incl) + carry
        cum_ref[:, pl.ds(ts, SLOT_BLK)] = c
        return c[:, SLOT_BLK - 1:SLOT_BLK]

    total = lax.fori_loop(0, nblk, count, jnp.zeros((LANES, 1), F32))
    tiles = jnp.floor((total + (SLOT_TILE - 1)) * (1.0 / SLOT_TILE))
    below = (lax.broadcasted_iota(jnp.int32, (LANES, LANES), 1)
             < lax.broadcasted_iota(jnp.int32, (LANES, LANES), 0)).astype(BF16)
    tile_lo = _dot(below, jnp.broadcast_to(tiles, (LANES, LANES)).astype(BF16))
    base = tile_lo[:, 0:1] * SLOT_TILE

    def assign(b, carry):
        ts, oh1, oh2 = onehots(b)
        pos = base + cum_ref[:, pl.ds(ts, SLOT_BLK)] - 1.0
        s1 = jnp.sum(jnp.where(oh1, pos, 0.0), axis=0, keepdims=True)
        s2 = jnp.sum(jnp.where(oh2, pos, 0.0), axis=0, keepdims=True)
        row = lax.broadcasted_iota(jnp.int32, (8, SLOT_BLK), 0)
        slot_ref[:, pl.ds(ts, SLOT_BLK)] = jnp.where(row == 0, s1, jnp.where(row == 1, s2, 0.0)).astype(jnp.int32)
        return carry

    lax.fori_loop(0, nblk, assign, 0)

    tile_hi = tile_lo + tiles
    tix = lax.broadcasted_iota(jnp.int32, (LANES, LANES), 1).astype(F32)
    is_e = lax.broadcasted_iota(jnp.int32, (LANES, LANES), 0) < N_EXPERTS
    owner = jnp.sum(jnp.where(is_e & (tile_hi <= tix), 1.0, 0.0), axis=0, keepdims=True)
    n_tiles = jnp.max(jnp.where(is_e, tile_hi, 0.0), axis=0, keepdims=True)
    valid = tix[0:1, :] < n_tiles
    last_owner = jnp.max(jnp.where(valid, owner, 0.0), axis=1, keepdims=True)
    owner = jnp.where(valid, owner, last_owner)
    eye = (lax.broadcasted_iota(jnp.int32, (LANES, LANES), 0)
           == lax.broadcasted_iota(jnp.int32, (LANES, LANES), 1))
    to_lanes = lambda colvec: jnp.sum(jnp.where(eye, colvec, 0.0), axis=0, keepdims=True)
    pad_first = to_lanes(base + total)
    pad_end = to_lanes(base + tiles * SLOT_TILE)
    row = lax.broadcasted_iota(jnp.int32, (8, LANES), 0)
    meta = jnp.where(row == 0, owner,
                     jnp.where(row == 1, valid.astype(F32),
                               jnp.where(row == 2, pad_first,
                                         jnp.where(row == 3, pad_end,
                                                   jnp.where(row == 4, n_tiles, 0.0)))))
    meta_ref[...] = meta.astype(jnp.int32)


def _slots(route):
    t = route.shape[0]
    kern = functools.partial(_slots_kernel, n_tok=t)
    return pl.pallas_call(
        kern,
        grid=(1,),
        in_specs=[pl.BlockSpec(route.shape, lambda i: (0, 0))],
        out_specs=[
            pl.BlockSpec((8, t), lambda i: (0, 0)),
            pl.BlockSpec((8, LANES), lambda i: (0, 0)),
        ],
        out_shape=[
            jax.ShapeDtypeStruct((8, t), jnp.int32),
            jax.ShapeDtypeStruct((8, LANES), jnp.int32),
        ],
        scratch_shapes=[pltpu.VMEM((LANES, t), F32)],
        compiler_params=_params("arbitrary"),
        name="slots",
    )(route)


def _row_copy(src_ref, src_row, dst_ref, dst_row, sem):
    return pltpu.make_async_copy(src_ref.at[pl.ds(src_row, 1), :], dst_ref.at[pl.ds(dst_row, 1), :], sem)


def _dispatch_kernel(s1_ref, s2_ref, pad_first_ref, pad_end_ref, n_tiles_ref, h_ref, xs_ref, zero_ref,
                     sem, pad_sem, *, tm):
    i = pl.program_id(0)
    t0 = i * tm
    n_slot_tiles = xs_ref.shape[0] // SLOT_TILE

    def copies(r):
        return (_row_copy(h_ref, r, xs_ref, s1_ref[t0 + r], sem),
                _row_copy(h_ref, r, xs_ref, s2_ref[t0 + r], sem))

    def issue(r, c):
        first, second = copies(r)
        first.start(priority=0)
        second.start(priority=1)
        return c

    lax.fori_loop(0, tm, issue, 0, unroll=8)

    @pl.when(i == pl.num_programs(0) - 1)
    def _():
        zero_ref[...] = jnp.zeros(zero_ref.shape, F32)

        def pad_copy(slot):
            return _row_copy(zero_ref, 0, xs_ref, slot, pad_sem)

        def fill(e, c):
            def one(s, cc):
                pad_copy(s).start()
                return cc

            return lax.fori_loop(pad_first_ref[e], pad_end_ref[e], one, c)

        lax.fori_loop(0, N_EXPERTS, fill, 0)

        def tile_copy(tile):
            return pltpu.make_async_copy(zero_ref, xs_ref.at[pl.ds(tile * SLOT_TILE, SLOT_TILE), :], pad_sem)

        def fill_tile(tile, c):
            tile_copy(tile).start()
            return c

        lax.fori_loop(n_tiles_ref[0], n_slot_tiles, fill_tile, 0)

        def settle(e, c):
            def one(s, cc):
                pad_copy(s).wait()
                return cc

            return lax.fori_loop(pad_first_ref[e], pad_end_ref[e], one, c)

        lax.fori_loop(0, N_EXPERTS, settle, 0)

        def settle_tile(tile, c):
            tile_copy(tile).wait()
            return c

        lax.fori_loop(n_tiles_ref[0], n_slot_tiles, settle_tile, 0)

    def drain(r, c):
        for cp in copies(r):
            cp.wait()
        return c

    lax.fori_loop(0, tm, drain, 0, unroll=8)


def _dispatch(slot1, slot2, pad_first, pad_end, n_tiles, h2, n_slots, tm=1024):
    t = h2.shape[0]
    kern = functools.partial(_dispatch_kernel, tm=tm)
    return pl.pallas_call(
        kern,
        grid_spec=pltpu.PrefetchScalarGridSpec(
            num_scalar_prefetch=5,
            grid=(t // tm,),
            in_specs=[pl.BlockSpec((tm, D_MODEL), lambda i, *_: (i, 0))],
            out_specs=pl.BlockSpec(memory_space=pl.ANY),
            scratch_shapes=[
                pltpu.VMEM((SLOT_TILE, D_MODEL), F32),
                pltpu.SemaphoreType.DMA(()),
                pltpu.SemaphoreType.DMA(()),
            ],
        ),
        out_shape=jax.ShapeDtypeStruct((n_slots, D_MODEL), F32),
        compiler_params=_params("arbitrary"),
        name="dispatch",
    )(slot1, slot2, pad_first, pad_end, n_tiles, h2)


def _experts_kernel(own_ref, valid_ref, used_ref, xs_ref, wg_hbm, wu_hbm, wd_hbm, ys_ref,
                    wgf, wuf, wdf, wgb, wub, wdb, wsem, wslot_ref):
    del used_ref
    i = pl.program_id(0)
    nt = pl.num_programs(0)
    at = lambda ref, j: ref[jnp.minimum(j, nt - 1)]

    def weight_copies(e, s):
        return (pltpu.make_async_copy(wg_hbm.at[e], wgf.at[s], wsem.at[s]),
                pltpu.make_async_copy(wu_hbm.at[e], wuf.at[s], wsem.at[s]),
                pltpu.make_async_copy(wd_hbm.at[e], wdf.at[s], wsem.at[s]))

    def next_expert_tile(j0):
        e0 = at(own_ref, j0)
        return lax.while_loop(lambda j: (j < nt) & (at(own_ref, j) == e0), lambda j: j + 1, j0 + 1)

    def start_weights(j, s):
        @pl.when((j < nt) & (at(valid_ref, j) > 0))
        def _():
            for cp in weight_copies(at(own_ref, j), s):
                cp.start()

    @pl.when(i == 0)
    def _():
        wslot_ref[0] = 0
        j = 0
        for s in range(WEIGHT_SLOTS - 1):
            start_weights(j, s)
            j = next_expert_tile(j)

    @pl.when(valid_ref[i] > 0)
    def _():
        e = own_ref[i]

        @pl.when((i == 0) | (own_ref[jnp.maximum(i - 1, 0)] != e))
        def _():
            s = wslot_ref[0]
            for cp in weight_copies(e, s):
                cp.wait()
            j = i
            for _ in range(WEIGHT_SLOTS - 1):
                j = next_expert_tile(j)
            start_weights(j, (s + WEIGHT_SLOTS - 1) % WEIGHT_SLOTS)

            wgb[...] = wgf[s].astype(BF16)
            wub[...] = wuf[s].astype(BF16)
            wdb[...] = wdf[s].astype(BF16)
            wslot_ref[0] = (s + 1) % WEIGHT_SLOTS

        x = xs_ref[...].astype(BF16)
        a = _dot(x, wgb[...])
        u = _dot(x, wub[...])
        act = (a * jax.nn.sigmoid(a) * u).astype(BF16)
        ys_ref[...] = _dot(act, wdb[...])


def _experts(own, valid, used, xs, wg, wu, wd):
    n_tiles = xs.shape[0] // SLOT_TILE
    hbm = pl.BlockSpec(memory_space=pl.ANY)
    tile = (SLOT_TILE, D_MODEL)
    in_use = lambda i, o, v, u: (jnp.where(v[i] > 0, i, u[0] - 1), 0)
    return pl.pallas_call(
        _experts_kernel,
        grid_spec=pltpu.PrefetchScalarGridSpec(
            num_scalar_prefetch=3,
            grid=(n_tiles,),
            in_specs=[pl.BlockSpec(tile, in_use), hbm, hbm, hbm],
            out_specs=pl.BlockSpec(tile, in_use),
            scratch_shapes=[
                pltpu.VMEM((WEIGHT_SLOTS, D_MODEL, D_EXPERT), F32),
                pltpu.VMEM((WEIGHT_SLOTS, D_MODEL, D_EXPERT), F32),
                pltpu.VMEM((WEIGHT_SLOTS, D_EXPERT, D_MODEL), F32),
                pltpu.VMEM((D_MODEL, D_EXPERT), BF16),
                pltpu.VMEM((D_MODEL, D_EXPERT), BF16),
                pltpu.VMEM((D_EXPERT, D_MODEL), BF16),
                pltpu.SemaphoreType.DMA((WEIGHT_SLOTS,)),
                pltpu.SMEM((1,), jnp.int32),
            ],
        ),
        out_shape=jax.ShapeDtypeStruct(xs.shape, F32),
        input_output_aliases={3: 0},
        compiler_params=_params("arbitrary"),
        name="experts",
    )(own, valid, used, xs, wg, wu, wd)


def _combine_kernel(s1_ref, s2_ref, x_ref, route_ref, g_ref, ys_ref, y_ref, buf_ref, sem, *, tm):
    i = pl.program_id(0)
    n = pl.num_programs(0)

    def copies(tile, p, r):
        t = tile * tm + r
        return (_row_copy(ys_ref, s1_ref[t], buf_ref.at[p, 0], r, sem.at[p]),
                _row_copy(ys_ref, s2_ref[t], buf_ref.at[p, 1], r, sem.at[p]))

    def gather_start(tile, p):
        def issue(r, c):
            first, second = copies(tile, p, r)
            first.start(priority=0)
            second.start(priority=1)
            return c

        lax.fori_loop(0, tm, issue, 0, unroll=8)

    def gather_wait(tile, p):
        def drain(r, c):
            for cp in copies(tile, p, r):
                cp.wait()
            return c

        lax.fori_loop(0, tm, drain, 0, unroll=8)

    @pl.when(i == 0)
    def _():
        gather_start(0, 0)

    p = i % 2

    @pl.when(i + 1 < n)
    def _():
        gather_start(i + 1, 1 - p)

    gather_wait(i, p)
    y = x_ref[...] + route_ref[:, 2:3] * buf_ref[p, 0] + route_ref[:, 3:4] * buf_ref[p, 1]
    y_ref[...] = _rms(y, g_ref[...])


def _combine(slot1, slot2, x2, route, g, ys, tm=256):
    t = x2.shape[0]
    kern = functools.partial(_combine_kernel, tm=tm)
    return pl.pallas_call(
        kern,
        grid_spec=pltpu.PrefetchScalarGridSpec(
            num_scalar_prefetch=2,
            grid=(t // tm,),
            in_specs=[
                pl.BlockSpec((tm, D_MODEL), lambda i, s1, s2: (i, 0)),
                pl.BlockSpec((tm, LANES), lambda i, s1, s2: (i, 0)),
                pl.BlockSpec((1, D_MODEL), lambda i, s1, s2: (0, 0)),
                pl.BlockSpec(memory_space=pl.ANY),
            ],
            out_specs=pl.BlockSpec((tm, D_MODEL), lambda i, s1, s2: (i, 0)),
            scratch_shapes=[
                pltpu.VMEM((2, 2, tm, D_MODEL), F32),
                pltpu.SemaphoreType.DMA((2,)),
            ],
        ),
        out_shape=jax.ShapeDtypeStruct((t, D_MODEL), F32),
        compiler_params=_params("arbitrary"),
        name="combine",
    )(slot1, slot2, x2, route, g, ys)


def _row(v):
    return v.reshape(1, -1).astype(F32)


def kernel(x, mem, positions, norm_mix, w_in, norm_q_lat, w_q_b, norm_kv_lat, w_kv_b, norm_mla_out, norm_sb_out, w_out, norm_mem_x, norm_mem_src, w_mem_q, w_mem_kv, w_mem_o, norm_ffn, w_group, b_group, w_expert_router, b_expert, w_gate, w_up, w_down, norm_final):
    batch, seq, d = x.shape
    t = batch * seq
    depth = w_in.shape[0]
    assert depth == 1, "single-layer trunk only"
    xt = x.reshape(t, d)
    pos = positions.reshape(t, 1)
    inv_freq = ROPE_THETA ** (-jnp.arange(0, MLA_ROPE, 2, dtype=F32) / MLA_ROPE)
    invf = jnp.concatenate([inv_freq, inv_freq, jnp.zeros((LANES - MLA_ROPE,), F32)]).reshape(1, LANES)
    n_slots = (2 * t // SLOT_TILE + N_EXPERTS) * SLOT_TILE

    for l in range(depth):
        sb0 = MLA_Q_RANK + MLA_KV_RANK + MLA_ROPE
        row_scale = jnp.concatenate([jnp.ones((sb0,), F32), jnp.full((SB_OUT,), SB_DIM ** -0.5 * LOG2E, F32),
                                     jnp.ones((2 * SB_OUT,), F32)])
        w_in_t = (w_in[l].T * row_scale[:, None]).astype(BF16)
        wq = w_q_b[l].reshape(MLA_Q_RANK, MLA_HEADS, MLA_QK) * (MLA_QK ** -0.5 * LOG2E)
        zq = jnp.zeros((MLA_Q_RANK, MLA_HEADS, MLA_PAD - MLA_QK), F32)
        wqa = jnp.concatenate([wq, zq], axis=-1).reshape(MLA_Q_RANK, MLA_HEADS * MLA_PAD).astype(BF16)
        wkv = w_kv_b[l].reshape(MLA_KV_RANK, MLA_HEADS, MLA_NOPE + MLA_V)
        wk = wkv[:, :, :MLA_NOPE].reshape(MLA_KV_RANK, MLA_OUT).astype(BF16)
        wvt = wkv[:, :, MLA_NOPE:].reshape(MLA_KV_RANK, MLA_OUT).T.astype(BF16)
        w_router = jnp.concatenate(
            [w_expert_router[l], w_group[l], jnp.zeros((d, LANES - N_EXPERTS - N_GROUPS), F32)], axis=1)
        wr_hi = w_router.astype(BF16)
        wr_hl = jnp.concatenate([wr_hi, (w_router - wr_hi.astype(F32)).astype(BF16)], axis=1)
        b_router = jnp.concatenate(
            [b_expert[l].astype(F32), b_group[l].astype(F32), jnp.zeros((LANES - N_EXPERTS - N_GROUPS,), F32)]
        ).reshape(1, LANES)

        lat, sb_qk, sb_vt = _proj_in(xt, _row(norm_mix[l]), w_in_t, sb0)
        q, k, vt = _mla_proj(lat, pos, invf, _row(norm_q_lat[l]), _row(norm_kv_lat[l]), wqa, wk, wvt)
        o_mla = _mla_attn(q, k, vt, batch, seq)
        o_sb = _sb_attn(sb_qk, sb_vt, batch, seq)
        x1 = _out_proj(xt, o_mla, o_sb, _row(norm_mla_out[l]), _row(norm_sb_out[l]), w_out[l])

        kv = _mem_kv(mem.reshape(batch * MEM_LEN, d), _row(norm_mem_src[l]), w_mem_kv[l])
        x2, h2, route = _mem_route(x1, _row(norm_mem_x[l]), w_mem_q[l], kv,
                                   w_mem_o[l], _row(norm_ffn[l]), wr_hl, b_router, seq)

        slots, meta = _slots(route)
        xs = _dispatch(slots[0], slots[1], meta[2], meta[3], meta[4], h2, n_slots)
        ys = _experts(meta[0], meta[1], meta[4], xs, w_gate[l], w_up[l], w_down[l])
        xt = _combine(slots[0], slots[1], x2, route, _row(norm_final), ys)
    return xt.reshape(batch, seq, d)
```

```python
import functools

import jax
import jax.numpy as jnp
from jax import lax
from jax.experimental import pallas as pl
from jax.experimental.pallas import tpu as pltpu

F32 = jnp.float32
BF16 = jnp.bfloat16

EPS = 1e-6
ROPE_THETA = 10000.0

D_MODEL = 2048
MEM_LEN = 256
MLA_HEADS = 8
MLA_NOPE = 128
MLA_ROPE = 64
MLA_QK = MLA_NOPE + MLA_ROPE
MLA_V = 128
MLA_Q_RANK = 512
MLA_KV_RANK = 256
MLA_PAD = 256
SB_HEADS = 8
SB_DIM = 128
MLA_OUT = MLA_HEADS * MLA_V
SB_OUT = SB_HEADS * SB_DIM
MEM_HEADS = 4
MEM_DIM = 128
N_GROUPS = 4
EXPERTS_PER_GROUP = 8
N_EXPERTS = N_GROUPS * EXPERTS_PER_GROUP
D_EXPERT = 512

LANES = 128
LAT_COLS = 1024

SLOT_TILE = 256
WEIGHT_SLOTS = 2
VMEM_LIMIT = 56 * 1024 * 1024


def _rms(x, g):
    return x * lax.rsqrt(jnp.mean(x * x, axis=-1, keepdims=True) + EPS) * g


def _dot(a, b):
    return jnp.dot(a, b, preferred_element_type=F32)


def _dot_nt(a, b):
    return lax.dot_general(a, b, (((1,), (1,)), ((), ())), preferred_element_type=F32)


def _split_bf16(x):
    hi = x.astype(BF16)
    lo = (x - hi.astype(F32)).astype(BF16)
    return hi, lo


def _params(*sem):
    return pltpu.CompilerParams(dimension_semantics=sem, vmem_limit_bytes=VMEM_LIMIT)


def _proj_in_kernel(x_ref, g_ref, wl_ref, ws_ref, lat_ref, qk_ref, vt_ref, h_ref):
    j = pl.program_id(1)
    last = pl.num_programs(1) - 1

    @pl.when(j == 0)
    def _():
        h_ref[...] = _rms(x_ref[...], g_ref[...]).astype(BF16)
        lat_ref[...] = _dot_nt(h_ref[...], wl_ref[...])

    @pl.when((j > 0) & (j < last))
    def _():
        qk_ref[...] = _dot_nt(h_ref[...], ws_ref[...]).astype(BF16)

    @pl.when(j == last)
    def _():
        vt_ref[...] = _dot_nt(ws_ref[...], h_ref[...]).astype(BF16)


def _proj_in(x, g, w_t, sb0, tm=1024):
    t = x.shape[0]
    tn = LAT_COLS
    n_sb = (w_t.shape[0] - sb0) // tn
    return pl.pallas_call(
        _proj_in_kernel,
        grid=(t // tm, n_sb + 1),
        in_specs=[
            pl.BlockSpec((tm, D_MODEL), lambda i, j: (i, 0)),
            pl.BlockSpec((1, D_MODEL), lambda i, j: (0, 0)),
            pl.BlockSpec((tn, D_MODEL), lambda i, j: (0, 0), pipeline_mode=pl.Buffered(1)),
            pl.BlockSpec((pl.Element(tn), pl.Element(D_MODEL)),
                         lambda i, j: (pl.multiple_of(sb0 + jnp.maximum(j - 1, 0) * tn, MLA_ROPE), 0)),
        ],
        out_specs=[
            pl.BlockSpec((tm, tn), lambda i, j: (i, 0)),
            pl.BlockSpec((tm, tn), lambda i, j: (i, jnp.clip(j - 1, 0, n_sb - 2))),
            pl.BlockSpec((SB_OUT, tm), lambda i, j: (0, i)),
        ],
        out_shape=[
            jax.ShapeDtypeStruct((t, LAT_COLS), F32),
            jax.ShapeDtypeStruct((t, 2 * SB_OUT), BF16),
            jax.ShapeDtypeStruct((SB_OUT, t), BF16),
        ],
        scratch_shapes=[pltpu.VMEM((tm, D_MODEL), BF16)],
        compiler_params=_params("parallel", "arbitrary"),
        name="proj_in",
    )(x, g, w_t, w_t)


def _mla_proj_kernel(lat_ref, pos_ref, invf_ref, gq_ref, gkv_ref, wqa_ref, wk_ref, wvt_ref,
                     q_ref, k_ref, vt_ref):
    cq = _rms(lat_ref[:, :MLA_Q_RANK], gq_ref[...]).astype(BF16)
    ckv = _rms(lat_ref[:, MLA_Q_RANK:MLA_Q_RANK + MLA_KV_RANK], gkv_ref[...]).astype(BF16)
    ang = pos_ref[...].astype(F32) * invf_ref[...]
    lane = lax.broadcasted_iota(jnp.int32, ang.shape, 1)
    live = lane < MLA_ROPE
    cos2 = jnp.where(live, jnp.cos(ang), 0.0)
    sin2 = jnp.where(live, jnp.sin(ang), 0.0)
    half = MLA_ROPE // 2

    def swapped(x):
        return jnp.where(lane < half, -pltpu.roll(x, LANES - half, 1), pltpu.roll(x, half, 1))

    qa = _dot(cq, wqa_ref[...])
    kn = _dot(ckv, wk_ref[...])
    vt_ref[...] = _dot_nt(wvt_ref[...], ckv).astype(BF16)
    k_raw = lat_ref[:, MLA_Q_RANK + MLA_KV_RANK:MLA_Q_RANK + MLA_KV_RANK + LANES]
    k_pe = jnp.where(live, k_raw * cos2 + swapped(k_raw) * sin2, 0.0).astype(BF16)
    for h in range(MLA_HEADS):
        lo = h * MLA_PAD
        mid = lo + LANES
        q_ref[:, lo:mid] = qa[:, lo:mid].astype(BF16)
        q_pe = qa[:, mid:mid + LANES]
        q_ref[:, mid:mid + LANES] = (q_pe * cos2 + swapped(q_pe) * sin2).astype(BF16)
        k_ref[:, lo:mid] = kn[:, h * LANES:(h + 1) * LANES].astype(BF16)
        k_ref[:, mid:mid + LANES] = k_pe


def _mla_proj(lat, pos, invf, gq, gkv, wqa, wk, wvt, tm=512):
    t = lat.shape[0]
    full = lambda a: pl.BlockSpec(a.shape, lambda i: (0, 0))
    return pl.pallas_call(
        _mla_proj_kernel,
        grid=(t // tm,),
        in_specs=[
            pl.BlockSpec((tm, LAT_COLS), lambda i: (i, 0)),
            pl.BlockSpec((tm, 1), lambda i: (i, 0)),
            full(invf), full(gq), full(gkv), full(wqa), full(wk), full(wvt),
        ],
        out_specs=[
            pl.BlockSpec((tm, MLA_HEADS * MLA_PAD), lambda i: (i, 0)),
            pl.BlockSpec((tm, MLA_HEADS * MLA_PAD), lambda i: (i, 0)),
            pl.BlockSpec((MLA_OUT, tm), lambda i: (0, i)),
        ],
        out_shape=[
            jax.ShapeDtypeStruct((t, MLA_HEADS * MLA_PAD), BF16),
            jax.ShapeDtypeStruct((t, MLA_HEADS * MLA_PAD), BF16),
            jax.ShapeDtypeStruct((MLA_OUT, t), BF16),
        ],
        compiler_params=_params("parallel"),
        name="mla_proj",
    )(lat, pos, invf, gq, gkv, wqa, wk, wvt)


LOG2E = 1.4426950408889634


def _three_stage(n_pairs, stage_a, stage_b, stage_c):
    a_issue, a_finish = stage_a
    b_issue, b_finish = stage_b
    c_issue, c_finish = stage_c

    def run_a(n, slot, first):
        a_finish(n, slot, a_issue(n, slot, first), first)

    run_a(0, 0, True)
    run_a(1, 1, True)
    b_finish(0, 0, b_issue(0, 0))

    def half(na, sa, nb, sb, nc, sc):
        ra = a_issue(na, sa, False)
        rb = b_issue(nb, sb)
        rc = c_issue(nc, sc)
        a_finish(na, sa, ra, False)
        b_finish(nb, sb, rb)
        c_finish(nc, sc, rc)

    def body(p, carry):
        half(2 * p, 0, 2 * p - 1, 1, 2 * p - 2, 0)
        half(2 * p + 1, 1, 2 * p, 0, 2 * p - 1, 1)
        return carry

    lax.fori_loop(1, n_pairs + 1, body, 0)
    last = 2 * n_pairs + 1
    rb = b_issue(last, 1)
    rc = c_issue(last - 1, 0)
    b_finish(last, 1, rb)
    c_finish(last - 1, 0, rc)
    c_finish(last, 1, c_issue(last, 1))


def _two_stage(n_pairs, produce, consume):
    produce(0, 0, True)
    produce(1, 1, True)
    consume(0, 0)

    def body(p, carry):
        produce(2 * p, 0, False)
        consume(2 * p - 1, 1)
        produce(2 * p + 1, 1, False)
        consume(2 * p, 0)
        return carry

    lax.fori_loop(1, n_pairs + 1, body, 0)
    consume(2 * n_pairs + 1, 1)


MLA_SUM_ROWS = 16
MLA_GROUP = 8


def _mla_attn_kernel(q_ref, k_ref, vt_ref, o_ref, s_ref, m_ref, acc_ref, *, tq, tk):
    i = pl.program_id(2)
    heads = range(MLA_GROUP)
    dq, dv = MLA_PAD, MLA_V
    m_ref[...] = jnp.full(m_ref.shape, -jnp.inf, F32)
    acc_ref[...] = jnp.zeros(acc_ref.shape, F32)
    ones = jnp.ones((MLA_SUM_ROWS, tk), BF16)

    def key_start(n):
        tile = jnp.where(n < 2, 2 * i + n, 2 * i + 1 - n)
        return pl.multiple_of(tile * tk, tk)

    def produce(n, slot, diagonal):
        ks = key_start(n)
        for g in heads:
            st = _dot_nt(k_ref[pl.ds(ks, tk), g * dq:(g + 1) * dq], q_ref[:, g * dq:(g + 1) * dq])
            if diagonal:
                kpos = ks + lax.broadcasted_iota(jnp.int32, st.shape, 0)
                qpos = i * tq + lax.broadcasted_iota(jnp.int32, st.shape, 1)
                st = jnp.where(kpos <= qpos, st, -jnp.inf)
            s_ref[g, slot] = st

    def consume(n, slot):
        ks = key_start(n)
        for g in heads:
            st = s_ref[g, slot]
            m_old = m_ref[g]
            m_new = jnp.maximum(m_old, jnp.max(st, axis=0, keepdims=True))
            alpha = jnp.exp2(m_old - m_new)
            p = jnp.exp2(st - m_new).astype(BF16)
            v_ones = jnp.concatenate([vt_ref[g * dv:(g + 1) * dv, pl.ds(ks, tk)], ones], axis=0)
            acc_ref[g] = alpha * acc_ref[g] + _dot(v_ones, p)
            m_ref[g] = m_new

    _two_stage(i, produce, consume)
    for g in heads:
        o_ref[:, g * dv:(g + 1) * dv] = (acc_ref[g, :dv, :] / acc_ref[g, dv:dv + 1, :]).T


def _mla_attn(q, k, vt, batch, seq, tq=512):
    tk = tq // 2
    nq = seq // tq
    groups = MLA_HEADS // MLA_GROUP
    kern = functools.partial(_mla_attn_kernel, tq=tq, tk=tk)
    return pl.pallas_call(
        kern,
        grid=(batch, groups, nq),
        in_specs=[
            pl.BlockSpec((tq, MLA_GROUP * MLA_PAD), lambda b, h, i: (b * nq + i, h)),
            pl.BlockSpec((seq, MLA_GROUP * MLA_PAD), lambda b, h, i: (b, h), pipeline_mode=pl.Buffered(1)),
            pl.BlockSpec((MLA_GROUP * MLA_V, seq), lambda b, h, i: (h, b), pipeline_mode=pl.Buffered(1)),
        ],
        out_specs=pl.BlockSpec((tq, MLA_GROUP * MLA_V), lambda b, h, i: (b * nq + i, h)),
        out_shape=jax.ShapeDtypeStruct((batch * seq, MLA_OUT), F32),
        scratch_shapes=[
            pltpu.VMEM((MLA_GROUP, 2, tk, tq), F32),
            pltpu.VMEM((MLA_GROUP, 1, tq), F32),
            pltpu.VMEM((MLA_GROUP, MLA_V + MLA_SUM_ROWS, tq), F32),
        ],
        compiler_params=_params("parallel", "parallel", "arbitrary"),
        name="mla_attn",
    )(q, k, vt)


SB_EXP_CLAMP = 126.0
SB_GROUP = 4


def _sb_attn_kernel(q_ref, k_ref, vt_ref, o_ref, z_ref, hl_ref, arg_ref, acc_ref, c_ref, *, tq, tk):
    i = pl.program_id(2)
    heads = range(SB_GROUP)
    d = SB_DIM
    acc_ref[...] = jnp.zeros(acc_ref.shape, F32)
    c_ref[...] = jnp.zeros(c_ref.shape, F32)
    col = lax.broadcasted_iota(jnp.int32, (tk, tk), 1)
    row = lax.broadcasted_iota(jnp.int32, (tk, tk), 0)
    neg_tri = jnp.where(col >= row, -1.0, 0.0).astype(BF16)

    def key_start(n):
        return pl.multiple_of((2 * i + 1 - n) * tk, tk)

    def a_issue(n, slot, masked):
        ks = key_start(n)
        return [_dot_nt(k_ref[pl.ds(ks, tk), g * d:(g + 1) * d], q_ref[:, g * d:(g + 1) * d])
                for g in heads]

    def a_finish(n, slot, z2s, masked):
        for g in heads:
            z2 = z2s[g]
            sp = jnp.maximum(z2, jnp.log2(1.0 + jnp.exp2(jnp.minimum(z2, SB_EXP_CLAMP))))
            if masked:
                kpos = key_start(n) + lax.broadcasted_iota(jnp.int32, z2.shape, 0)
                qpos = i * tq + lax.broadcasted_iota(jnp.int32, z2.shape, 1)
                mask = kpos < qpos
                sp = jnp.where(mask, sp, 0.0)
                z2 = jnp.where(mask, z2, -jnp.inf)
            hl_ref[g, slot] = sp.astype(BF16)
            z_ref[g, slot] = z2

    def b_issue(n, slot):
        return [_dot(neg_tri, hl_ref[g, slot]) for g in heads]

    def b_finish(n, slot, laters):
        for g in heads:
            arg_ref[g, slot] = z_ref[g, slot] + laters[g] + c_ref[g]
            c_ref[g] += laters[g][0:1, :]

    def c_issue(n, slot):
        ks = key_start(n)
        return [_dot(vt_ref[g * d:(g + 1) * d, pl.ds(ks, tk)], jnp.exp2(arg_ref[g, slot]).astype(BF16))
                for g in heads]

    def c_finish(n, slot, pvs):
        for g in heads:
            acc_ref[g] += pvs[g]

    _three_stage(i, (a_issue, a_finish), (b_issue, b_finish), (c_issue, c_finish))
    for g in heads:
        o_ref[:, g * d:(g + 1) * d] = acc_ref[g].T


def _sb_attn(qk, vt, batch, seq, tq=512):
    tk = tq // 2
    nq = seq // tq
    groups = SB_HEADS // SB_GROUP
    gd = SB_GROUP * SB_DIM
    kern = functools.partial(_sb_attn_kernel, tq=tq, tk=tk)
    return pl.pallas_call(
        kern,
        grid=(batch, groups, nq),
        in_specs=[
            pl.BlockSpec((tq, gd), lambda b, h, i: (b * nq + i, h)),
            pl.BlockSpec((seq, gd), lambda b, h, i: (b, groups + h)),
            pl.BlockSpec((gd, seq), lambda b, h, i: (h, b)),
        ],
        out_specs=pl.BlockSpec((tq, gd), lambda b, h, i: (b * nq + i, h)),
        out_shape=jax.ShapeDtypeStruct((batch * seq, SB_OUT), F32),
        scratch_shapes=[
            pltpu.VMEM((SB_GROUP, 2, tk, tq), F32),
            pltpu.VMEM((SB_GROUP, 2, tk, tq), BF16),
            pltpu.VMEM((SB_GROUP, 2, tk, tq), F32),
            pltpu.VMEM((SB_GROUP, SB_DIM, tq), F32),
            pltpu.VMEM((SB_GROUP, 1, tq), F32),
        ],
        compiler_params=_params("parallel", "parallel", "arbitrary"),
        name="sb_attn",
    )(qk, qk, vt)


def _resident(a):
    return pl.BlockSpec(a.shape, lambda i: (0,) * a.ndim, pipeline_mode=pl.Buffered(1))


def _cast_on_first_step(pairs):
    @pl.when(pl.program_id(0) == 0)
    def _():
        for src_ref, dst_ref in pairs:
            dst_ref[...] = src_ref[...].astype(BF16)


def _out_proj_kernel(x_ref, oa_ref, ob_ref, ga_ref, gb_ref, w_ref, y_ref, wb_ref):
    _cast_on_first_step([(w_ref, wb_ref)])
    na = _rms(oa_ref[...], ga_ref[...]).astype(BF16)
    nb = _rms(ob_ref[...], gb_ref[...]).astype(BF16)
    y_ref[...] = x_ref[...] + _dot(na, wb_ref[:MLA_OUT, :]) + _dot(nb, wb_ref[MLA_OUT:, :])


def _out_proj(x, oa, ob, ga, gb, w, tm=512):
    t = x.shape[0]
    full = lambda a: pl.BlockSpec(a.shape, lambda i: (0, 0))
    return pl.pallas_call(
        _out_proj_kernel,
        grid=(t // tm,),
        in_specs=[
            pl.BlockSpec((tm, D_MODEL), lambda i: (i, 0)),
            pl.BlockSpec((tm, MLA_OUT), lambda i: (i, 0)),
            pl.BlockSpec((tm, SB_OUT), lambda i: (i, 0)),
            full(ga), full(gb), _resident(w),
        ],
        out_specs=pl.BlockSpec((tm, D_MODEL), lambda i: (i, 0)),
        out_shape=jax.ShapeDtypeStruct((t, D_MODEL), F32),
        scratch_shapes=[pltpu.VMEM(w.shape, BF16)],
        compiler_params=_params("arbitrary"),
        name="out_proj",
    )(x, oa, ob, ga, gb, w)


def _mem_kv_kernel(mem_ref, g_ref, w_ref, kv_ref):
    kv_ref[...] = _dot(_rms(mem_ref[...], g_ref[...]).astype(BF16), w_ref[...].astype(BF16)).astype(BF16)


def _mem_kv(mem, g, w):
    m = mem.shape[0]
    n = w.shape[1]
    full = lambda a: pl.BlockSpec(a.shape, lambda i: (0, 0))
    return pl.pallas_call(
        _mem_kv_kernel,
        grid=(1,),
        in_specs=[full(mem), full(g), full(w)],
        out_specs=pl.BlockSpec((m, n), lambda i: (0, 0)),
        out_shape=jax.ShapeDtypeStruct((m, n), BF16),
        compiler_params=_params("arbitrary"),
        name="mem_kv",
    )(mem, g, w)


GROUP_LANE0 = N_EXPERTS


def _mem_route_kernel(x_ref, gx_ref, wq_ref, kv_ref, wo_ref, gf_ref, wrhl_ref, br_ref,
                      x2_ref, h2_ref, route_ref, wqb_ref, wob_ref):
    _cast_on_first_step([(wq_ref, wqb_ref), (wo_ref, wob_ref)])
    x1 = x_ref[...]
    q = _dot(_rms(x1, gx_ref[...]).astype(BF16), wqb_ref[...]).astype(BF16)
    scale = MEM_DIM ** -0.5
    kw = MEM_HEADS * MEM_DIM
    heads = []
    for h in range(MEM_HEADS):
        lo = h * MEM_DIM
        s = _dot_nt(q[:, lo:lo + MEM_DIM], kv_ref[:, lo:lo + MEM_DIM]) * scale
        e = jnp.exp(s - jnp.max(s, axis=-1, keepdims=True))
        p = (e / jnp.sum(e, axis=-1, keepdims=True)).astype(BF16)
        heads.append(_dot(p, kv_ref[:, kw + lo:kw + lo + MEM_DIM]).astype(BF16))
    o = jnp.concatenate(heads, axis=-1)
    x2 = x1 + _dot(o, wob_ref[...])
    x2_ref[...] = x2
    h2 = _rms(x2, gf_ref[...])
    h2_ref[...] = h2

    hh, hl = _split_bf16(h2)
    both = _dot(hh, wrhl_ref[...])
    lg = both[:, :LANES] + both[:, LANES:] + _dot(hl, wrhl_ref[:, :LANES]) + br_ref[...]
    lane = lax.broadcasted_iota(jnp.int32, lg.shape, 1)
    big = jnp.int32(1 << 20)
    ninf = -jnp.inf

    def lane_max(v):
        return jnp.max(v, axis=-1, keepdims=True)

    def first_lane(cond):
        return jnp.min(jnp.where(cond, lane, big), axis=-1, keepdims=True)

    is_g = (lane >= GROUP_LANE0) & (lane < GROUP_LANE0 + N_GROUPS)
    g_max = lane_max(jnp.where(is_g, lg, ninf))
    g_sum = jnp.sum(jnp.where(is_g, jnp.exp(lg - g_max), 0.0), axis=-1, keepdims=True)
    p_g = 1.0 / g_sum
    g_idx = first_lane(is_g & (lg == g_max)) - GROUP_LANE0
    in_grp = (lane < N_EXPERTS) & ((lane // EXPERTS_PER_GROUP) == g_idx)
    e_max = lane_max(jnp.where(in_grp, lg, ninf))
    e_sum = jnp.sum(jnp.where(in_grp, jnp.exp(lg - e_max), 0.0), axis=-1, keepdims=True)
    i1 = first_lane(in_grp & (lg == e_max))
    rest = in_grp & (lane != i1)
    e_max2 = lane_max(jnp.where(rest, lg, ninf))
    i2 = first_lane(rest & (lg == e_max2))
    p1 = 1.0 / e_sum
    p2 = jnp.exp(e_max2 - e_max) / e_sum
    den = p1 + p2
    gate1 = p_g * (p1 / den)
    gate2 = p_g * (p2 / den)
    route = jnp.where(lane == 0, i1.astype(F32),
                      jnp.where(lane == 1, i2.astype(F32),
                                jnp.where(lane == 2, gate1,
                                          jnp.where(lane == 3, gate2, 0.0))))
    route_ref[...] = route


def _mem_route(x1, gx, wq, kv, wo, gf, wrhl, br, seq, tm=512):
    t = x1.shape[0]
    per_batch = seq // tm
    full = lambda a: pl.BlockSpec(a.shape, lambda i: (0, 0))
    return pl.pallas_call(
        _mem_route_kernel,
        grid=(t // tm,),
        in_specs=[
            pl.BlockSpec((tm, D_MODEL), lambda i: (i, 0)),
            full(gx), _resident(wq),
            pl.BlockSpec((MEM_LEN, kv.shape[1]), lambda i: (i // per_batch, 0)),
            _resident(wo), full(gf), full(wrhl), full(br),
        ],
        out_specs=[
            pl.BlockSpec((tm, D_MODEL), lambda i: (i, 0)),
            pl.BlockSpec((tm, D_MODEL), lambda i: (i, 0)),
            pl.BlockSpec((tm, LANES), lambda i: (i, 0)),
        ],
        out_shape=[
            jax.ShapeDtypeStruct((t, D_MODEL), F32),
            jax.ShapeDtypeStruct((t, D_MODEL), F32),
            jax.ShapeDtypeStruct((t, LANES), F32),
        ],
        scratch_shapes=[pltpu.VMEM(wq.shape, BF16), pltpu.VMEM(wo.shape, BF16)],
        compiler_params=_params("arbitrary"),
        name="mem_route",
    )(x1, gx, wq, kv, wo, gf, wrhl, br)


SLOT_BLK = 256


def _slots_kernel(route_ref, slot_ref, meta_ref, cum_ref, *, n_tok):
    nblk = n_tok // SLOT_BLK
    e_iota = lax.broadcasted_iota(jnp.int32, (LANES, SLOT_BLK), 0)
    incl = (lax.broadcasted_iota(jnp.int32, (SLOT_BLK, SLOT_BLK), 0)
            <= lax.broadcasted_iota(jnp.int32, (SLOT_BLK, SLOT_BLK), 1)).astype(BF16)

    def onehots(b):
        ts = pl.multiple_of(b * SLOT_BLK, SLOT_BLK)
        ids = route_ref[pl.ds(ts, SLOT_BLK), :].T
        oh1 = e_iota == ids[0:1, :].astype(jnp.int32)
        oh2 = e_iota == ids[1:2, :].astype(jnp.int32)
        return ts, oh1, oh2

    def count(b, carry):
        ts, oh1, oh2 = onehots(b)
        cnt = (oh1 | oh2).astype(F32).astype(BF16)
        c = _dot(cnt, incl) + carry
        cum_ref[:, pl.ds(ts, SLOT_BLK)] = c
        return c[:, SLOT_BLK - 1:SLOT_BLK]

    total = lax.fori_loop(0, nblk, count, jnp.zeros((LANES, 1), F32))
    tiles = jnp.floor((total + (SLOT_TILE - 1)) * (1.0 / SLOT_TILE))
    below = (lax.broadcasted_iota(jnp.int32, (LANES, LANES), 1)
             < lax.broadcasted_iota(jnp.int32, (LANES, LANES), 0)).astype(BF16)
    tile_lo = _dot(below, jnp.broadcast_to(tiles, (LANES, LANES)).astype(BF16))
    base = tile_lo[:, 0:1] * SLOT_TILE

    def assign(b, carry):
        ts, oh1, oh2 = onehots(b)
        pos = base + cum_ref[:, pl.ds(ts, SLOT_BLK)] - 1.0
        s1 = jnp.sum(jnp.where(oh1, pos, 0.0), axis=0, keepdims=True)
        s2 = jnp.sum(jnp.where(oh2, pos, 0.0), axis=0, keepdims=True)
        row = lax.broadcasted_iota(jnp.int32, (8, SLOT_BLK), 0)
        slot_ref[:, pl.ds(ts, SLOT_BLK)] = jnp.where(row == 0, s1, jnp.where(row == 1, s2, 0.0)).astype(jnp.int32)
        return carry

    lax.fori_loop(0, nblk, assign, 0)

    tile_hi = tile_lo + tiles
    tix = lax.broadcasted_iota(jnp.int32, (LANES, LANES), 1).astype(F32)
    is_e = lax.broadcasted_iota(jnp.int32, (LANES, LANES), 0) < N_EXPERTS
    owner = jnp.sum(jnp.where(is_e & (tile_hi <= tix), 1.0, 0.0), axis=0, keepdims=True)
    n_tiles = jnp.max(jnp.where(is_e, tile_hi, 0.0), axis=0, keepdims=True)
    valid = tix[0:1, :] < n_tiles
    last_owner = jnp.max(jnp.where(valid, owner, 0.0), axis=1, keepdims=True)
    owner = jnp.where(valid, owner, last_owner)
    eye = (lax.broadcasted_iota(jnp.int32, (LANES, LANES), 0)
           == lax.broadcasted_iota(jnp.int32, (LANES, LANES), 1))
    to_lanes = lambda colvec: jnp.sum(jnp.where(eye, colvec, 0.0), axis=0, keepdims=True)
    pad_first = to_lanes(base + total)
    pad_end = to_lanes(base + tiles * SLOT_TILE)
    row = lax.broadcasted_iota(jnp.int32, (8, LANES), 0)
    meta = jnp.where(row == 0, owner,
                     jnp.where(row == 1, valid.astype(F32),
                               jnp.where(row == 2, pad_first,
                                         jnp.where(row == 3, pad_end,
                                                   jnp.where(row == 4, n_tiles, 0.0)))))
    meta_ref[...] = meta.astype(jnp.int32)


def _slots(route):
    t = route.shape[0]
    kern = functools.partial(_slots_kernel, n_tok=t)
    return pl.pallas_call(
        kern,
        grid=(1,),
        in_specs=[pl.BlockSpec(route.shape, lambda i: (0, 0))],
        out_specs=[
            pl.BlockSpec((8, t), lambda i: (0, 0)),
            pl.BlockSpec((8, LANES), lambda i: (0, 0)),
        ],
        out_shape=[
            jax.ShapeDtypeStruct((8, t), jnp.int32),
            jax.ShapeDtypeStruct((8, LANES), jnp.int32),
        ],
        scratch_shapes=[pltpu.VMEM((LANES, t), F32)],
        compiler_params=_params("arbitrary"),
        name="slots",
    )(route)


def _row_copy(src_ref, src_row, dst_ref, dst_row, sem):
    return pltpu.make_async_copy(src_ref.at[pl.ds(src_row, 1), :], dst_ref.at[pl.ds(dst_row, 1), :], sem)


def _dispatch_kernel(s1_ref, s2_ref, pad_first_ref, pad_end_ref, n_tiles_ref, h_ref, xs_ref, zero_ref,
                     sem, pad_sem, *, tm):
    i = pl.program_id(0)
    t0 = i * tm
    n_slot_tiles = xs_ref.shape[0] // SLOT_TILE

    def copies(r):
        return (_row_copy(h_ref, r, xs_ref, s1_ref[t0 + r], sem),
                _row_copy(h_ref, r, xs_ref, s2_ref[t0 + r], sem))

    def issue(r, c):
        first, second = copies(r)
        first.start(priority=0)
        second.start(priority=1)
        return c

    lax.fori_loop(0, tm, issue, 0, unroll=8)

    @pl.when(i == pl.num_programs(0) - 1)
    def _():
        zero_ref[...] = jnp.zeros(zero_ref.shape, F32)

        def pad_copy(slot):
            return _row_copy(zero_ref, 0, xs_ref, slot, pad_sem)

        def fill(e, c):
            def one(s, cc):
                pad_copy(s).start()
                return cc

            return lax.fori_loop(pad_first_ref[e], pad_end_ref[e], one, c)

        lax.fori_loop(0, N_EXPERTS, fill, 0)

        def tile_copy(tile):
            return pltpu.make_async_copy(zero_ref, xs_ref.at[pl.ds(tile * SLOT_TILE, SLOT_TILE), :], pad_sem)

        def fill_tile(tile, c):
            tile_copy(tile).start()
            return c

        lax.fori_loop(n_tiles_ref[0], n_slot_tiles, fill_tile, 0)

        def settle(e, c):
            def one(s, cc):
                pad_copy(s).wait()
                return cc

            return lax.fori_loop(pad_first_ref[e], pad_end_ref[e], one, c)

        lax.fori_loop(0, N_EXPERTS, settle, 0)

        def settle_tile(tile, c):
            tile_copy(tile).wait()
            return c

        lax.fori_loop(n_tiles_ref[0], n_slot_tiles, settle_tile, 0)

    def drain(r, c):
        for cp in copies(r):
            cp.wait()
        return c

    lax.fori_loop(0, tm, drain, 0, unroll=8)


def _dispatch(slot1, slot2, pad_first, pad_end, n_tiles, h2, n_slots, tm=1024):
    t = h2.shape[0]
    kern = functools.partial(_dispatch_kernel, tm=tm)
    return pl.pallas_call(
        kern,
        grid_spec=pltpu.PrefetchScalarGridSpec(
            num_scalar_prefetch=5,
            grid=(t // tm,),
            in_specs=[pl.BlockSpec((tm, D_MODEL), lambda i, *_: (i, 0))],
            out_specs=pl.BlockSpec(memory_space=pl.ANY),
            scratch_shapes=[
                pltpu.VMEM((SLOT_TILE, D_MODEL), F32),
                pltpu.SemaphoreType.DMA(()),
                pltpu.SemaphoreType.DMA(()),
            ],
        ),
        out_shape=jax.ShapeDtypeStruct((n_slots, D_MODEL), F32),
        compiler_params=_params("arbitrary"),
        name="dispatch",
    )(slot1, slot2, pad_first, pad_end, n_tiles, h2)


def _experts_kernel(own_ref, valid_ref, used_ref, xs_ref, wg_hbm, wu_hbm, wd_hbm, ys_ref,
                    wgf, wuf, wdf, wgb, wub, wdb, wsem, wslot_ref):
    del used_ref
    i = pl.program_id(0)
    nt = pl.num_programs(0)
    at = lambda ref, j: ref[jnp.minimum(j, nt - 1)]

    def weight_copies(e, s):
        return (pltpu.make_async_copy(wg_hbm.at[e], wgf.at[s], wsem.at[s]),
                pltpu.make_async_copy(wu_hbm.at[e], wuf.at[s], wsem.at[s]),
                pltpu.make_async_copy(wd_hbm.at[e], wdf.at[s], wsem.at[s]))

    def next_expert_tile(j0):
        e0 = at(own_ref, j0)
        return lax.while_loop(lambda j: (j < nt) & (at(own_ref, j) == e0), lambda j: j + 1, j0 + 1)

    def start_weights(j, s):
        @pl.when((j < nt) & (at(valid_ref, j) > 0))
        def _():
            for cp in weight_copies(at(own_ref, j), s):
                cp.start()

    @pl.when(i == 0)
    def _():
        wslot_ref[0] = 0
        j = 0
        for s in range(WEIGHT_SLOTS - 1):
            start_weights(j, s)
            j = next_expert_tile(j)

    @pl.when(valid_ref[i] > 0)
    def _():
        e = own_ref[i]

        @pl.when((i == 0) | (own_ref[jnp.maximum(i - 1, 0)] != e))
        def _():
            s = wslot_ref[0]
            for cp in weight_copies(e, s):
                cp.wait()
            j = i
            for _ in range(WEIGHT_SLOTS - 1):
                j = next_expert_tile(j)
            start_weights(j, (s + WEIGHT_SLOTS - 1) % WEIGHT_SLOTS)

            wgb[...] = wgf[s].astype(BF16)
            wub[...] = wuf[s].astype(BF16)
            wdb[...] = wdf[s].astype(BF16)
            wslot_ref[0] = (s + 1) % WEIGHT_SLOTS

        x = xs_ref[...].astype(BF16)
        a = _dot(x, wgb[...])
        u = _dot(x, wub[...])
        act = (a * jax.nn.sigmoid(a) * u).astype(BF16)
        ys_ref[...] = _dot(act, wdb[...])


def _experts(own, valid, used, xs, wg, wu, wd):
    n_tiles = xs.shape[0] // SLOT_TILE
    hbm = pl.BlockSpec(memory_space=pl.ANY)
    tile = (SLOT_TILE, D_MODEL)
    in_use = lambda i, o, v, u: (jnp.where(v[i] > 0, i, u[0] - 1), 0)
    return pl.pallas_call(
        _experts_kernel,
        grid_spec=pltpu.PrefetchScalarGridSpec(
            num_scalar_prefetch=3,
            grid=(n_tiles,),
            in_specs=[pl.BlockSpec(tile, in_use), hbm, hbm, hbm],
            out_specs=pl.BlockSpec(tile, in_use),
            scratch_shapes=[
                pltpu.VMEM((WEIGHT_SLOTS, D_MODEL, D_EXPERT), F32),
                pltpu.VMEM((WEIGHT_SLOTS, D_MODEL, D_EXPERT), F32),
                pltpu.VMEM((WEIGHT_SLOTS, D_EXPERT, D_MODEL), F32),
                pltpu.VMEM((D_MODEL, D_EXPERT), BF16),
                pltpu.VMEM((D_MODEL, D_EXPERT), BF16),
                pltpu.VMEM((D_EXPERT, D_MODEL), BF16),
                pltpu.SemaphoreType.DMA((WEIGHT_SLOTS,)),
                pltpu.SMEM((1,), jnp.int32),
            ],
        ),
        out_shape=jax.ShapeDtypeStruct(xs.shape, F32),
        input_output_aliases={3: 0},
        compiler_params=_params("arbitrary"),
        name="experts",
    )(own, valid, used, xs, wg, wu, wd)


def _combine_kernel(s1_ref, s2_ref, x_ref, route_ref, g_ref, ys_ref, y_ref, buf_ref, sem, *, tm):
    i = pl.program_id(0)
    n = pl.num_programs(0)

    def copies(tile, p, r):
        t = tile * tm + r
        return (_row_copy(ys_ref, s1_ref[t], buf_ref.at[p, 0], r, sem.at[p]),
                _row_copy(ys_ref, s2_ref[t], buf_ref.at[p, 1], r, sem.at[p]))

    def gather_start(tile, p):
        def issue(r, c):
            first, second = copies(tile, p, r)
            first.start(priority=0)
            second.start(priority=1)
            return c

        lax.fori_loop(0, tm, issue, 0, unroll=32)

    def gather_wait(tile, p):
        def drain(r, c):
            for cp in copies(tile, p, r):
                cp.wait()
            return c

        lax.fori_loop(0, tm, drain, 0, unroll=8)

    @pl.when(i == 0)
    def _():
        gather_start(0, 0)

    p = i % 2

    @pl.when(i + 1 < n)
    def _():
        gather_start(i + 1, 1 - p)

    gather_wait(i, p)
    y = x_ref[...] + route_ref[:, 2:3] * buf_ref[p, 0] + route_ref[:, 3:4] * buf_ref[p, 1]
    y_ref[...] = _rms(y, g_ref[...])


def _combine(slot1, slot2, x2, route, g, ys, tm=256):
    t = x2.shape[0]
    kern = functools.partial(_combine_kernel, tm=tm)
    return pl.pallas_call(
        kern,
        grid_spec=pltpu.PrefetchScalarGridSpec(
            num_scalar_prefetch=2,
            grid=(t // tm,),
            in_specs=[
                pl.BlockSpec((tm, D_MODEL), lambda i, s1, s2: (i, 0)),
                pl.BlockSpec((tm, LANES), lambda i, s1, s2: (i, 0)),
                pl.BlockSpec((1, D_MODEL), lambda i, s1, s2: (0, 0)),
                pl.BlockSpec(memory_space=pl.ANY),
            ],
            out_specs=pl.BlockSpec((tm, D_MODEL), lambda i, s1, s2: (i, 0)),
            scratch_shapes=[
                pltpu.VMEM((2, 2, tm, D_MODEL), F32),
                pltpu.SemaphoreType.DMA((2,)),
            ],
        ),
        out_shape=jax.ShapeDtypeStruct((t, D_MODEL), F32),
        compiler_params=_params("arbitrary"),
        name="combine",
    )(slot1, slot2, x2, route, g, ys)


def _row(v):
    return v.reshape(1, -1).astype(F32)


def kernel(x, mem, positions, norm_mix, w_in, norm_q_lat, w_q_b, norm_kv_lat, w_kv_b, norm_mla_out, norm_sb_out, w_out, norm_mem_x, norm_mem_src, w_mem_q, w_mem_kv, w_mem_o, norm_ffn, w_group, b_group, w_expert_router, b_expert, w_gate, w_up, w_down, norm_final):
    batch, seq, d = x.shape
    t = batch * seq
    depth = w_in.shape[0]
    assert depth == 1, "single-layer trunk only"
    xt = x.reshape(t, d)
    pos = positions.reshape(t, 1)
    inv_freq = ROPE_THETA ** (-jnp.arange(0, MLA_ROPE, 2, dtype=F32) / MLA_ROPE)
    invf = jnp.concatenate([inv_freq, inv_freq, jnp.zeros((LANES - MLA_ROPE,), F32)]).reshape(1, LANES)
    n_slots = (2 * t // SLOT_TILE + N_EXPERTS) * SLOT_TILE

    for l in range(depth):
        sb0 = MLA_Q_RANK + MLA_KV_RANK + MLA_ROPE
        row_scale = jnp.concatenate([jnp.ones((sb0,), F32), jnp.full((SB_OUT,), SB_DIM ** -0.5 * LOG2E, F32),
                                     jnp.ones((2 * SB_OUT,), F32)])
        w_in_t = (w_in[l].T * row_scale[:, None]).astype(BF16)
        wq = w_q_b[l].reshape(MLA_Q_RANK, MLA_HEADS, MLA_QK) * (MLA_QK ** -0.5 * LOG2E)
        zq = jnp.zeros((MLA_Q_RANK, MLA_HEADS, MLA_PAD - MLA_QK), F32)
        wqa = jnp.concatenate([wq, zq], axis=-1).reshape(MLA_Q_RANK, MLA_HEADS * MLA_PAD).astype(BF16)
        wkv = w_kv_b[l].reshape(MLA_KV_RANK, MLA_HEADS, MLA_NOPE + MLA_V)
        wk = wkv[:, :, :MLA_NOPE].reshape(MLA_KV_RANK, MLA_OUT).astype(BF16)
        wvt = wkv[:, :, MLA_NOPE:].reshape(MLA_KV_RANK, MLA_OUT).T.astype(BF16)
        w_router = jnp.concatenate(
            [w_expert_router[l], w_group[l], jnp.zeros((d, LANES - N_EXPERTS - N_GROUPS), F32)], axis=1)
        wr_hi = w_router.astype(BF16)
        wr_hl = jnp.concatenate([wr_hi, (w_router - wr_hi.astype(F32)).astype(BF16)], axis=1)
        b_router = jnp.concatenate(
            [b_expert[l].astype(F32), b_group[l].astype(F32), jnp.zeros((LANES - N_EXPERTS - N_GROUPS,), F32)]
        ).reshape(1, LANES)

        lat, sb_qk, sb_vt = _proj_in(xt, _row(norm_mix[l]), w_in_t, sb0)
        q, k, vt = _mla_proj(lat, pos, invf, _row(norm_q_lat[l]), _row(norm_kv_lat[l]), wqa, wk, wvt)
        o_mla = _mla_attn(q, k, vt, batch, seq)
        o_sb = _sb_attn(sb_qk, sb_vt, batch, seq)
        x1 = _out_proj(xt, o_mla, o_sb, _row(norm_mla_out[l]), _row(norm_sb_out[l]), w_out[l])

        kv = _mem_kv(mem.reshape(batch * MEM_LEN, d), _row(norm_mem_src[l]), w_mem_kv[l])
        x2, h2, route = _mem_route(x1, _row(norm_mem_x[l]), w_mem_q[l], kv,
                                   w_mem_o[l], _row(norm_ffn[l]), wr_hl, b_router, seq)

        slots, meta = _slots(route)
        xs = _dispatch(slots[0], slots[1], meta[2], meta[3], meta[4], h2, n_slots)
        ys = _experts(meta[0], meta[1], meta[4], xs, w_gate[l], w_up[l], w_down[l])
        xt = _combine(slots[0], slots[1], x2, route, _row(norm_final), ys)
    return xt.reshape(batch, seq, d)
```
